```python
import jax, jax.numpy as jnp
from jax import lax
import numpy as np

D_MODEL = 1024
BATCH = 8
SEQ = 8192
DEPTH = 1

D_MIX = D_MODEL
GDN_HEADS = 4
GDN_HEAD_DIM = 128
GDN_WIDTH = GDN_HEADS * GDN_HEAD_DIM
POOL_WINDOWS = (2, 4, 8, 16)
POOL_GROUPS = len(POOL_WINDOWS)
POOL_WIDTH = D_MIX - GDN_WIDTH
POOL_GROUP_DIM = POOL_WIDTH // POOL_GROUPS
CONV_K = 4
CHUNK = 64
D_FF = ((8 * D_MODEL // 3 + 255) // 256) * 256
D_IN = 4 * GDN_WIDTH + 2 * GDN_HEADS + POOL_WIDTH
N_MOD = 9
EPS = 1e-6

kernel_name = "hybrid_gdn_pool_macaron_adaln"


def rms_norm(x, gain):
    xf = x.astype(jnp.float32)
    y = xf * lax.rsqrt(jnp.mean(xf * xf, axis=-1, keepdims=True) + EPS)
    return (y * gain.astype(jnp.float32)).astype(x.dtype)


def l2_normalize(x):
    xf = x.astype(jnp.float32)
    return xf * lax.rsqrt(jnp.sum(xf * xf, axis=-1, keepdims=True) + EPS)


def modulate(h, shift, scale):
    return h * (1 + scale[:, None, :]) + shift[:, None, :]


def swiglu(h, w_gate, w_up, w_down):
    return (jax.nn.silu(h @ w_gate) * (h @ w_up)) @ w_down


def causal_depthwise_conv_silu(x, w):
    C = x.shape[-1]
    y = lax.conv_general_dilated(
        x, w[:, None, :].astype(x.dtype), window_strides=(1,), padding=[(CONV_K - 1, 0)],
        dimension_numbers=("NWC", "WIO", "NWC"), feature_group_count=C)
    return jax.nn.silu(y)


def gated_delta_rule_chunked(q, k, v, g, beta):
    B, T, H, Dk = q.shape
    Dv = v.shape[-1]
    N = T // CHUNK

    def chunks(t):
        t = t.reshape((B, N, CHUNK, H) + t.shape[3:])
        return jnp.moveaxis(t, 3, 1)

    q, k, v, g, beta = chunks(q), chunks(k), chunks(v), chunks(g), chunks(beta)
    g_cum = jnp.cumsum(g, axis=-1)
    causal = jnp.tril(jnp.ones((CHUNK, CHUNK), dtype=bool))
    strict = jnp.tril(jnp.ones((CHUNK, CHUNK), dtype=bool), -1)
    diff = g_cum[..., :, None] - g_cum[..., None, :]
    decay = jnp.where(causal, jnp.exp(jnp.where(causal, diff, 0.0)), 0.0)
    k_beta = k * beta[..., None]
    m = jnp.where(strict, jnp.einsum("bhnid,bhnjd->bhnij", k_beta, k) * decay, 0.0)
    a = m + jnp.eye(CHUNK, dtype=m.dtype)
    u = lax.linalg.triangular_solve(a, v * beta[..., None], left_side=True, lower=True)
    w = lax.linalg.triangular_solve(a, k_beta * jnp.exp(g_cum)[..., None], left_side=True, lower=True)
    intra = jnp.einsum("bhnid,bhnjd->bhnij", q, k) * decay
    g_last = g_cum[..., -1:]
    q_dec = q * jnp.exp(g_cum)[..., None]
    k_dec = k * jnp.exp(g_last - g_cum)[..., None]
    chunk_decay = jnp.exp(g_last[..., 0])
    xs = tuple(jnp.moveaxis(t, 2, 0) for t in (q_dec, k_dec, u, w, intra, chunk_decay))

    def step(S, inp):
        qd, kd, un, wn, an, cd = inp
        v_new = un - jnp.einsum("bhck,bhkv->bhcv", wn, S)
        o = jnp.einsum("bhck,bhkv->bhcv", qd, S) + jnp.einsum("bhij,bhjv->bhiv", an, v_new)
        S = S * cd[..., None, None] + jnp.einsum("bhck,bhcv->bhkv", kd, v_new)
        return S, o

    S0 = jnp.zeros((B, H, Dk, Dv), jnp.float32)
    _, o = lax.scan(step, S0, xs)
    return jnp.transpose(o, (1, 0, 3, 2, 4)).reshape(B, T, H, Dv)


def multiscale_causal_pool(p):
    B, T, _ = p.shape
    pf = p.astype(jnp.float32).reshape(B, T, POOL_GROUPS, POOL_GROUP_DIM)
    cs0 = jnp.concatenate([jnp.zeros_like(pf[:, :1]), jnp.cumsum(pf, axis=1)], axis=1)
    t1 = jnp.arange(1, T + 1, dtype=jnp.float32)
    outs = []
    for gi, win in enumerate(POOL_WINDOWS):
        cur = cs0[:, 1:, gi]
        lag = jnp.concatenate(
            [jnp.zeros((B, win - 1, POOL_GROUP_DIM), jnp.float32), cs0[:, :T - win + 1, gi]], axis=1)
        cnt = jnp.minimum(t1, win)[None, :, None]
        outs.append((cur - lag) / cnt - pf[:, :, gi])
    return jnp.stack(outs, axis=2)


def _fwd_setup_inputs(seed: int = 0) -> dict:
    key = jax.random.key(seed)
    ks = jax.random.split(key, 24)
    f32 = jnp.float32
    L, D = DEPTH, D_MODEL

    def nrm(k, shape, scale):
        return jax.random.normal(k, shape, f32) * scale

    def gain(k, shape):
        return 1.0 + 0.02 * jax.random.normal(k, shape, f32)

    dt = jnp.exp(jax.random.uniform(ks[11], (L, GDN_HEADS), f32, np.log(1e-3), np.log(1e-1)))
    return {
        "x": nrm(ks[0], (BATCH, SEQ, D), 1.0),
        "c": nrm(ks[1], (BATCH, D), 1.0),
        "w_ada": nrm(ks[2], (L, D, N_MOD * D), 0.5 * D ** -0.5),
        "b_ada": nrm(ks[3], (L, N_MOD * D), 0.01),
        "norm_ffn1": gain(ks[4], (L, D)),
        "ffn1_gate": nrm(ks[5], (L, D, D_FF), D ** -0.5),
        "ffn1_up": nrm(ks[6], (L, D, D_FF), D ** -0.5),
        "ffn1_down": nrm(ks[7], (L, D_FF, D), D_FF ** -0.5),
        "norm_mix": gain(ks[8], (L, D)),
        "w_in": nrm(ks[9], (L, D, D_IN), D ** -0.5),
        "conv_w": nrm(ks[10], (L, CONV_K, 3 * GDN_WIDTH), CONV_K ** -0.5),
        "a_log": jnp.log(jax.random.uniform(ks[12], (L, GDN_HEADS), f32, 1.0, 16.0)),
        "dt_bias": dt + jnp.log(-jnp.expm1(-dt)),
        "gdn_norm": gain(ks[13], (L, GDN_HEAD_DIM)),
        "pool_w": nrm(ks[14], (L, POOL_GROUPS, POOL_GROUP_DIM, POOL_GROUP_DIM), POOL_GROUP_DIM ** -0.5),
        "pool_scale": gain(ks[15], (L, POOL_WIDTH)),
        "w_out": nrm(ks[16], (L, D_MIX, D), D_MIX ** -0.5),
        "norm_ffn2": gain(ks[17], (L, D)),
        "ffn2_gate": nrm(ks[18], (L, D, D_FF), D ** -0.5),
        "ffn2_up": nrm(ks[19], (L, D, D_FF), D ** -0.5),
        "ffn2_down": nrm(ks[20], (L, D_FF, D), D_FF ** -0.5),
        "final_norm": gain(ks[21], (D,)),
    }


def _fwd_reference(x, c, w_ada, b_ada, norm_ffn1, ffn1_gate, ffn1_up, ffn1_down, norm_mix, w_in, conv_w,
              a_log, dt_bias, gdn_norm, pool_w, pool_scale, w_out, norm_ffn2, ffn2_gate, ffn2_up,
              ffn2_down, final_norm):
    B, T, _ = x.shape
    H, Dh, GW = GDN_HEADS, GDN_HEAD_DIM, GDN_WIDTH
    split_at = [3 * GW, 4 * GW, 4 * GW + H, 4 * GW + 2 * H]
    for l in range(DEPTH):
        mod = (jax.nn.silu(c) @ w_ada[l] + b_ada[l]).reshape(B, N_MOD, D_MODEL)

        h = modulate(rms_norm(x, norm_ffn1[l]), mod[:, 0], mod[:, 1])
        x = x + 0.5 * mod[:, 2][:, None, :] * swiglu(h, ffn1_gate[l], ffn1_up[l], ffn1_down[l])

        h = modulate(rms_norm(x, norm_mix[l]), mod[:, 3], mod[:, 4])
        proj = h @ w_in[l]
        qkv, z, b_raw, a_raw, p = jnp.split(proj, split_at, axis=-1)
        qkv = causal_depthwise_conv_silu(qkv, conv_w[l])
        q, k, v = jnp.split(qkv, 3, axis=-1)
        q = l2_normalize(q.reshape(B, T, H, Dh)) * (Dh ** -0.5)
        k = l2_normalize(k.reshape(B, T, H, Dh))
        v = v.reshape(B, T, H, Dh).astype(jnp.float32)
        beta = jax.nn.sigmoid(b_raw.astype(jnp.float32))
        g = -jnp.exp(a_log[l].astype(jnp.float32)) * jax.nn.softplus(
            a_raw.astype(jnp.float32) + dt_bias[l].astype(jnp.float32))
        o = gated_delta_rule_chunked(q, k, v, g, beta)
        o = rms_norm(o, gdn_norm[l]) * jax.nn.silu(z.reshape(B, T, H, Dh).astype(jnp.float32))
        gdn_out = o.reshape(B, T, GW).astype(x.dtype)

        pooled = multiscale_causal_pool(p)
        pooled = jnp.einsum("btgi,gio->btgo", pooled, pool_w[l].astype(jnp.float32))
        pool_out = (pooled.reshape(B, T, POOL_WIDTH) * pool_scale[l].astype(jnp.float32)).astype(x.dtype)

        mixed = jnp.concatenate([gdn_out, pool_out], axis=-1) @ w_out[l]
        x = x + mod[:, 5][:, None, :] * mixed

        h = modulate(rms_norm(x, norm_ffn2[l]), mod[:, 6], mod[:, 7])
        x = x + 0.5 * mod[:, 8][:, None, :] * swiglu(h, ffn2_gate[l], ffn2_up[l], ffn2_down[l])
    return rms_norm(x, final_norm)


import jax as _jax
import jax.numpy as _jnp

TWIN_FORMAT = 'train_step'
FWD_PARAMS = ['x', 'c', 'w_ada', 'b_ada', 'norm_ffn1', 'ffn1_gate', 'ffn1_up', 'ffn1_down', 'norm_mix', 'w_in', 'conv_w', 'a_log', 'dt_bias', 'gdn_norm', 'pool_w', 'pool_scale', 'w_out', 'norm_ffn2', 'ffn2_gate', 'ffn2_up', 'ffn2_down', 'final_norm']
TWIN_WEIGHTS = ['w_ada', 'b_ada', 'norm_ffn1', 'ffn1_gate', 'ffn1_up', 'ffn1_down', 'norm_mix', 'w_in', 'conv_w', 'a_log', 'dt_bias', 'gdn_norm', 'pool_w', 'pool_scale', 'w_out', 'norm_ffn2', 'ffn2_gate', 'ffn2_up', 'ffn2_down', 'final_norm']
TWIN_DIFF_INPUT = 'x'
TWIN_INPUTS = ['x', 'c', 'w_ada', 'b_ada', 'norm_ffn1', 'ffn1_gate', 'ffn1_up', 'ffn1_down', 'norm_mix', 'w_in', 'conv_w', 'a_log', 'dt_bias', 'gdn_norm', 'pool_w', 'pool_scale', 'w_out', 'norm_ffn2', 'ffn2_gate', 'ffn2_up', 'ffn2_down', 'final_norm', 'loss_target', 'm_w_ada', 'm_b_ada', 'm_norm_ffn1', 'm_ffn1_gate', 'm_ffn1_up', 'm_ffn1_down', 'm_norm_mix', 'm_w_in', 'm_conv_w', 'm_a_log', 'm_dt_bias', 'm_gdn_norm', 'm_pool_w', 'm_pool_scale', 'm_w_out', 'm_norm_ffn2', 'm_ffn2_gate', 'm_ffn2_up', 'm_ffn2_down', 'm_final_norm', 'v_w_ada', 'v_b_ada', 'v_norm_ffn1', 'v_ffn1_gate', 'v_ffn1_up', 'v_ffn1_down', 'v_norm_mix', 'v_w_in', 'v_conv_w', 'v_a_log', 'v_dt_bias', 'v_gdn_norm', 'v_pool_w', 'v_pool_scale', 'v_w_out', 'v_norm_ffn2', 'v_ffn2_gate', 'v_ffn2_up', 'v_ffn2_down', 'v_final_norm']
TWIN_OUTPUTS = ['loss', 'grad_x', 'grad_w_ada', 'grad_b_ada', 'grad_norm_ffn1', 'grad_ffn1_gate', 'grad_ffn1_up', 'grad_ffn1_down', 'grad_norm_mix', 'grad_w_in', 'grad_conv_w', 'grad_a_log', 'grad_dt_bias', 'grad_gdn_norm', 'grad_pool_w', 'grad_pool_scale', 'grad_w_out', 'grad_norm_ffn2', 'grad_ffn2_gate', 'grad_ffn2_up', 'grad_ffn2_down', 'grad_final_norm', 'delta_w_ada', 'delta_b_ada', 'delta_norm_ffn1', 'delta_ffn1_gate', 'delta_ffn1_up', 'delta_ffn1_down', 'delta_norm_mix', 'delta_w_in', 'delta_conv_w', 'delta_a_log', 'delta_dt_bias', 'delta_gdn_norm', 'delta_pool_w', 'delta_pool_scale', 'delta_w_out', 'delta_norm_ffn2', 'delta_ffn2_gate', 'delta_ffn2_up', 'delta_ffn2_down', 'delta_final_norm', 'new_m_w_ada', 'new_m_b_ada', 'new_m_norm_ffn1', 'new_m_ffn1_gate', 'new_m_ffn1_up', 'new_m_ffn1_down', 'new_m_norm_mix', 'new_m_w_in', 'new_m_conv_w', 'new_m_a_log', 'new_m_dt_bias', 'new_m_gdn_norm', 'new_m_pool_w', 'new_m_pool_scale', 'new_m_w_out', 'new_m_norm_ffn2', 'new_m_ffn2_gate', 'new_m_ffn2_up', 'new_m_ffn2_down', 'new_m_final_norm', 'new_v_w_ada', 'new_v_b_ada', 'new_v_norm_ffn1', 'new_v_ffn1_gate', 'new_v_ffn1_up', 'new_v_ffn1_down', 'new_v_norm_mix', 'new_v_w_in', 'new_v_conv_w', 'new_v_a_log', 'new_v_dt_bias', 'new_v_gdn_norm', 'new_v_pool_w', 'new_v_pool_scale', 'new_v_w_out', 'new_v_norm_ffn2', 'new_v_ffn2_gate', 'new_v_ffn2_up', 'new_v_ffn2_down', 'new_v_final_norm']
TWIN_LEAF_KINDS = {'loss': 'loss', 'grad_x': 'grad_x', 'grad_w_ada': 'grad_w', 'grad_b_ada': 'grad_w', 'grad_norm_ffn1': 'grad_w', 'grad_ffn1_gate': 'grad_w', 'grad_ffn1_up': 'grad_w', 'grad_ffn1_down': 'grad_w', 'grad_norm_mix': 'grad_w', 'grad_w_in': 'grad_w', 'grad_conv_w': 'grad_w', 'grad_a_log': 'grad_w', 'grad_dt_bias': 'grad_w', 'grad_gdn_norm': 'grad_w', 'grad_pool_w': 'grad_w', 'grad_pool_scale': 'grad_w', 'grad_w_out': 'grad_w', 'grad_norm_ffn2': 'grad_w', 'grad_ffn2_gate': 'grad_w', 'grad_ffn2_up': 'grad_w', 'grad_ffn2_down': 'grad_w', 'grad_final_norm': 'grad_w', 'delta_w_ada': 'delta_w', 'delta_b_ada': 'delta_w', 'delta_norm_ffn1': 'delta_w', 'delta_ffn1_gate': 'delta_w', 'delta_ffn1_up': 'delta_w', 'delta_ffn1_down': 'delta_w', 'delta_norm_mix': 'delta_w', 'delta_w_in': 'delta_w', 'delta_conv_w': 'delta_w', 'delta_a_log': 'delta_w', 'delta_dt_bias': 'delta_w', 'delta_gdn_norm': 'delta_w', 'delta_pool_w': 'delta_w', 'delta_pool_scale': 'delta_w', 'delta_w_out': 'delta_w', 'delta_norm_ffn2': 'delta_w', 'delta_ffn2_gate': 'delta_w', 'delta_ffn2_up': 'delta_w', 'delta_ffn2_down': 'delta_w', 'delta_final_norm': 'delta_w', 'new_m_w_ada': 'new_m', 'new_m_b_ada': 'new_m', 'new_m_norm_ffn1': 'new_m', 'new_m_ffn1_gate': 'new_m', 'new_m_ffn1_up': 'new_m', 'new_m_ffn1_down': 'new_m', 'new_m_norm_mix': 'new_m', 'new_m_w_in': 'new_m', 'new_m_conv_w': 'new_m', 'new_m_a_log': 'new_m', 'new_m_dt_bias': 'new_m', 'new_m_gdn_norm': 'new_m', 'new_m_pool_w': 'new_m', 'new_m_pool_scale': 'new_m', 'new_m_w_out': 'new_m', 'new_m_norm_ffn2': 'new_m', 'new_m_ffn2_gate': 'new_m', 'new_m_ffn2_up': 'new_m', 'new_m_ffn2_down': 'new_m', 'new_m_final_norm': 'new_m', 'new_v_w_ada': 'new_v', 'new_v_b_ada': 'new_v', 'new_v_norm_ffn1': 'new_v', 'new_v_ffn1_gate': 'new_v', 'new_v_ffn1_up': 'new_v', 'new_v_ffn1_down': 'new_v', 'new_v_norm_mix': 'new_v', 'new_v_w_in': 'new_v', 'new_v_conv_w': 'new_v', 'new_v_a_log': 'new_v', 'new_v_dt_bias': 'new_v', 'new_v_gdn_norm': 'new_v', 'new_v_pool_w': 'new_v', 'new_v_pool_scale': 'new_v', 'new_v_w_out': 'new_v', 'new_v_norm_ffn2': 'new_v', 'new_v_ffn2_gate': 'new_v', 'new_v_ffn2_up': 'new_v', 'new_v_ffn2_down': 'new_v', 'new_v_final_norm': 'new_v'}


def _forward(args):
    return _fwd_reference(*[args[k] for k in FWD_PARAMS])


def _output_shape():
    out = _jax.eval_shape(lambda: _forward(_fwd_setup_inputs(0)))
    return out.shape, out.dtype

N_MICROBATCH = 1
ADAM_LR = 0.001
ADAM_B1 = 0.9
ADAM_B2 = 0.999
ADAM_EPS = 1e-08
ADAM_WD = 0.01
ADAM_STEP = 10
PER_EXAMPLE_BATCH_AXIS = {'x': 0, 'c': 0, 'loss_target': 0}
SHARED_INPUTS = []
_WEIGHT_DTYPES = {'w_ada': _jnp.float32, 'b_ada': _jnp.float32, 'norm_ffn1': _jnp.float32, 'ffn1_gate': _jnp.float32, 'ffn1_up': _jnp.float32, 'ffn1_down': _jnp.float32, 'norm_mix': _jnp.float32, 'w_in': _jnp.float32, 'conv_w': _jnp.float32, 'a_log': _jnp.float32, 'dt_bias': _jnp.float32, 'gdn_norm': _jnp.float32, 'pool_w': _jnp.float32, 'pool_scale': _jnp.float32, 'w_out': _jnp.float32, 'norm_ffn2': _jnp.float32, 'ffn2_gate': _jnp.float32, 'ffn2_up': _jnp.float32, 'ffn2_down': _jnp.float32, 'final_norm': _jnp.float32}
MOMENT_SCALE = {'w_ada': 5.807419e-02, 'b_ada': 9.379970e-02, 'norm_ffn1': 3.961636e-02, 'ffn1_gate': 1.729303e-02, 'ffn1_up': 1.671160e-02, 'ffn1_down': 2.772689e-02, 'norm_mix': 7.588569e-02, 'w_in': 4.772390e-02, 'conv_w': 3.733727e-02, 'a_log': 2.763755e-01, 'dt_bias': 3.456641e-01, 'gdn_norm': 1.094359e-01, 'pool_w': 6.908170e-02, 'pool_scale': 7.516372e-02, 'w_out': 6.019989e-02, 'norm_ffn2': 3.749942e-02, 'ffn2_gate': 1.660511e-02, 'ffn2_up': 1.610184e-02, 'ffn2_down': 2.664963e-02, 'final_norm': 6.395903e+01}


def _to_microbatches(a, axis):
    t = _jnp.moveaxis(a, axis, 0)
    t = t.reshape((N_MICROBATCH, t.shape[0] // N_MICROBATCH) + t.shape[1:])
    return _jnp.moveaxis(t, 1, axis + 1)


def setup_inputs(seed: int = 0) -> dict:
    inp = _fwd_setup_inputs(seed)
    key = _jax.random.fold_in(_jax.random.key(seed), 7919)
    shape, _ = _output_shape()
    out = dict(inp)
    out["loss_target"] = _jax.random.normal(_jax.random.fold_in(key, 0), shape, _jnp.float32)
    for i, name in enumerate(TWIN_WEIGHTS):
        w = inp[name].astype(_jnp.float32)
        if MOMENT_SCALE is None:
            s = _jnp.sqrt(_jnp.mean(_jnp.square(w)) + 1e-30)
        else:
            s = MOMENT_SCALE[name]
        km, kv = _jax.random.split(_jax.random.fold_in(key, i + 1))
        out[name] = w
        out["m_" + name] = s * _jax.random.normal(km, w.shape, _jnp.float32)
        out["v_" + name] = (s * s) * _jax.random.uniform(kv, w.shape, _jnp.float32, 0.5, 1.5)
    if N_MICROBATCH > 1:
        for name, axis in PER_EXAMPLE_BATCH_AXIS.items():
            out[name] = _to_microbatches(out[name], axis)
    return {'x': out['x'], 'c': out['c'], 'w_ada': out['w_ada'], 'b_ada': out['b_ada'], 'norm_ffn1': out['norm_ffn1'], 'ffn1_gate': out['ffn1_gate'], 'ffn1_up': out['ffn1_up'], 'ffn1_down': out['ffn1_down'], 'norm_mix': out['norm_mix'], 'w_in': out['w_in'], 'conv_w': out['conv_w'], 'a_log': out['a_log'], 'dt_bias': out['dt_bias'], 'gdn_norm': out['gdn_norm'], 'pool_w': out['pool_w'], 'pool_scale': out['pool_scale'], 'w_out': out['w_out'], 'norm_ffn2': out['norm_ffn2'], 'ffn2_gate': out['ffn2_gate'], 'ffn2_up': out['ffn2_up'], 'ffn2_down': out['ffn2_down'], 'final_norm': out['final_norm'], 'loss_target': out['loss_target'], 'm_w_ada': out['m_w_ada'], 'm_b_ada': out['m_b_ada'], 'm_norm_ffn1': out['m_norm_ffn1'], 'm_ffn1_gate': out['m_ffn1_gate'], 'm_ffn1_up': out['m_ffn1_up'], 'm_ffn1_down': out['m_ffn1_down'], 'm_norm_mix': out['m_norm_mix'], 'm_w_in': out['m_w_in'], 'm_conv_w': out['m_conv_w'], 'm_a_log': out['m_a_log'], 'm_dt_bias': out['m_dt_bias'], 'm_gdn_norm': out['m_gdn_norm'], 'm_pool_w': out['m_pool_w'], 'm_pool_scale': out['m_pool_scale'], 'm_w_out': out['m_w_out'], 'm_norm_ffn2': out['m_norm_ffn2'], 'm_ffn2_gate': out['m_ffn2_gate'], 'm_ffn2_up': out['m_ffn2_up'], 'm_ffn2_down': out['m_ffn2_down'], 'm_final_norm': out['m_final_norm'], 'v_w_ada': out['v_w_ada'], 'v_b_ada': out['v_b_ada'], 'v_norm_ffn1': out['v_norm_ffn1'], 'v_ffn1_gate': out['v_ffn1_gate'], 'v_ffn1_up': out['v_ffn1_up'], 'v_ffn1_down': out['v_ffn1_down'], 'v_norm_mix': out['v_norm_mix'], 'v_w_in': out['v_w_in'], 'v_conv_w': out['v_conv_w'], 'v_a_log': out['v_a_log'], 'v_dt_bias': out['v_dt_bias'], 'v_gdn_norm': out['v_gdn_norm'], 'v_pool_w': out['v_pool_w'], 'v_pool_scale': out['v_pool_scale'], 'v_w_out': out['v_w_out'], 'v_norm_ffn2': out['v_norm_ffn2'], 'v_ffn2_gate': out['v_ffn2_gate'], 'v_ffn2_up': out['v_ffn2_up'], 'v_ffn2_down': out['v_ffn2_down'], 'v_final_norm': out['v_final_norm']}


def _loss(weights, diff, rest, loss_target):
    with _jax.named_scope("forward"):
        args = {**rest, TWIN_DIFF_INPUT: diff, **{k: w.astype(_WEIGHT_DTYPES[k]) for k, w in weights.items()}}
        y = _forward(args)
    with _jax.named_scope("loss_head"):
        err = _jnp.square(y.astype(_jnp.float32) - loss_target)
        return 0.5 * _jnp.sum(_jnp.mean(err, axis=-1)) if err.ndim else 0.5 * err


def _adamw(w, g, m, v):
    m = ADAM_B1 * m + (1.0 - ADAM_B1) * g
    v = ADAM_B2 * v + (1.0 - ADAM_B2) * _jnp.square(g)
    m_hat = m / (1.0 - ADAM_B1 ** ADAM_STEP)
    v_hat = v / (1.0 - ADAM_B2 ** ADAM_STEP)
    delta = -ADAM_LR * (m_hat / (_jnp.sqrt(v_hat) + ADAM_EPS) + ADAM_WD * w)
    return delta, m, v


def reference(x, c, w_ada, b_ada, norm_ffn1, ffn1_gate, ffn1_up, ffn1_down, norm_mix, w_in, conv_w, a_log, dt_bias, gdn_norm, pool_w, pool_scale, w_out, norm_ffn2, ffn2_gate, ffn2_up, ffn2_down, final_norm, loss_target, m_w_ada, m_b_ada, m_norm_ffn1, m_ffn1_gate, m_ffn1_up, m_ffn1_down, m_norm_mix, m_w_in, m_conv_w, m_a_log, m_dt_bias, m_gdn_norm, m_pool_w, m_pool_scale, m_w_out, m_norm_ffn2, m_ffn2_gate, m_ffn2_up, m_ffn2_down, m_final_norm, v_w_ada, v_b_ada, v_norm_ffn1, v_ffn1_gate, v_ffn1_up, v_ffn1_down, v_norm_mix, v_w_in, v_conv_w, v_a_log, v_dt_bias, v_gdn_norm, v_pool_w, v_pool_scale, v_w_out, v_norm_ffn2, v_ffn2_gate, v_ffn2_up, v_ffn2_down, v_final_norm):
    given = dict(x=x, c=c, w_ada=w_ada, b_ada=b_ada, norm_ffn1=norm_ffn1, ffn1_gate=ffn1_gate, ffn1_up=ffn1_up, ffn1_down=ffn1_down, norm_mix=norm_mix, w_in=w_in, conv_w=conv_w, a_log=a_log, dt_bias=dt_bias, gdn_norm=gdn_norm, pool_w=pool_w, pool_scale=pool_scale, w_out=w_out, norm_ffn2=norm_ffn2, ffn2_gate=ffn2_gate, ffn2_up=ffn2_up, ffn2_down=ffn2_down, final_norm=final_norm, loss_target=loss_target, m_w_ada=m_w_ada, m_b_ada=m_b_ada, m_norm_ffn1=m_norm_ffn1, m_ffn1_gate=m_ffn1_gate, m_ffn1_up=m_ffn1_up, m_ffn1_down=m_ffn1_down, m_norm_mix=m_norm_mix, m_w_in=m_w_in, m_conv_w=m_conv_w, m_a_log=m_a_log, m_dt_bias=m_dt_bias, m_gdn_norm=m_gdn_norm, m_pool_w=m_pool_w, m_pool_scale=m_pool_scale, m_w_out=m_w_out, m_norm_ffn2=m_norm_ffn2, m_ffn2_gate=m_ffn2_gate, m_ffn2_up=m_ffn2_up, m_ffn2_down=m_ffn2_down, m_final_norm=m_final_norm, v_w_ada=v_w_ada, v_b_ada=v_b_ada, v_norm_ffn1=v_norm_ffn1, v_ffn1_gate=v_ffn1_gate, v_ffn1_up=v_ffn1_up, v_ffn1_down=v_ffn1_down, v_norm_mix=v_norm_mix, v_w_in=v_w_in, v_conv_w=v_conv_w, v_a_log=v_a_log, v_dt_bias=v_dt_bias, v_gdn_norm=v_gdn_norm, v_pool_w=v_pool_w, v_pool_scale=v_pool_scale, v_w_out=v_w_out, v_norm_ffn2=v_norm_ffn2, v_ffn2_gate=v_ffn2_gate, v_ffn2_up=v_ffn2_up, v_ffn2_down=v_ffn2_down, v_final_norm=v_final_norm)
    weights = {n: given[n] for n in TWIN_WEIGHTS}
    shared = {n: given[n] for n in SHARED_INPUTS}
    per_example = {n: given[n] for n in ['x', 'c']}
    grad_fn = _jax.value_and_grad(_loss, argnums=(0, 1))

    def one_microbatch(ex, loss_target):
        ex = dict(ex)
        diff = ex.pop(TWIN_DIFF_INPUT)
        return grad_fn(weights, diff, {**shared, **ex}, loss_target)

    if N_MICROBATCH == 1:
        loss, (grad_w, grad_x) = one_microbatch(per_example, given["loss_target"])
    else:
        def body(carry, xs):
            loss_sum, grad_sum = carry
            l_k, (gw_k, gx_k) = one_microbatch(xs[0], xs[1])
            with _jax.named_scope("update"):
                return (loss_sum + l_k, _jax.tree.map(_jnp.add, grad_sum, gw_k)), gx_k

        init = (_jnp.zeros((), _jnp.float32), _jax.tree.map(_jnp.zeros_like, weights))
        (loss, grad_w), grad_x = _jax.lax.scan(body, init, (per_example, given["loss_target"]))
    with _jax.named_scope("update"):
        delta_w, new_m, new_v = {}, {}, {}
        for n in TWIN_WEIGHTS:
            delta_w[n], new_m[n], new_v[n] = _adamw(weights[n], grad_w[n], given["m_" + n], given["v_" + n])
    return (loss, grad_x, *[grad_w[n] for n in TWIN_WEIGHTS], *[delta_w[n] for n in TWIN_WEIGHTS],
            *[new_m[n] for n in TWIN_WEIGHTS], *[new_v[n] for n in TWIN_WEIGHTS])
```

```python
import functools

import jax
import jax.numpy as jnp
from jax import lax
from jax.experimental import pallas as pl
from jax.experimental.pallas import tpu as pltpu

F32 = jnp.float32
BF = jnp.bfloat16

D = 1024
FF = 2816
FH = FF // 2
NH = 4
DH = 128
GW = NH * DH
CH = 64
PW = 512
NG = 4
POOL_WINDOWS = (2, 4, 8, 16)
HALO = 16
DIN = 4 * GW + 2 * NH + PW
DINP = 3 * GW + GW + PW + 128
EPS = 1e-6
ADAM_LR, ADAM_B1, ADAM_B2, ADAM_EPS, ADAM_WD, ADAM_STEP = 0.001, 0.9, 0.999, 1e-08, 0.01, 10

VMEM_LIMIT = 56 * 1024 * 1024

NT_DIMS = (((1,), (1,)), ((), ()))
TN_DIMS = (((0,), (0,)), ((), ()))
HI = lax.Precision.HIGHEST


def _nt(a, b, **kw):
    return lax.dot_general(a, b, NT_DIMS, preferred_element_type=F32, **kw)


def _tn(a, b, **kw):
    return lax.dot_general(a, b, TN_DIMS, preferred_element_type=F32, **kw)


def _nn(a, b, **kw):
    return jnp.dot(a, b, preferred_element_type=F32, **kw)


def _cparams(sem=("arbitrary",), **kw):
    return pltpu.CompilerParams(dimension_semantics=sem, vmem_limit_bytes=VMEM_LIMIT, **kw)


def _const_spec(shape):
    nd = len(shape)
    return pl.BlockSpec(shape, lambda *_: (0,) * nd, pipeline_mode=pl.Buffered(1))


def _row_spec(tm, width):
    return pl.BlockSpec((tm, width), lambda i: (i, 0))


def _sum8(v):
    return jnp.sum(v.reshape(v.shape[0] // 8, 8, v.shape[1]), axis=0)


def _sigmoid(v):
    return 1.0 / (1.0 + jnp.exp(-v))


def _tile(T, cap=512):
    return min(cap, T)


def _norm_mod_fwd(xv, gain, shift, scale):
    r = lax.rsqrt(jnp.mean(xv * xv, axis=-1, keepdims=True) + EPS)
    n = xv * r
    y = n * gain
    return n, r, y, y * (1.0 + scale) + shift


def _norm_mod_bwd(dh, n, r, y, gain, scale):
    dy = dh * (1.0 + scale)
    dn = dy * gain
    dx = r * (dn - n * jnp.mean(dn * n, axis=-1, keepdims=True))
    return dx, _sum8(dh), _sum8(dh * y), _sum8(dy * n)


def _ffn_fwd(x, shift, scale, gate, gain, wgT, wuT, wd, name):
    T = x.shape[0]
    tm = _tile(T)

    def body(x_ref, sh_ref, sc_ref, gt_ref, gn_ref, wg_ref, wu_ref, wd_ref, xo_ref, f_ref, a_ref, b_ref):
        xv = x_ref[...]
        _, _, _, h = _norm_mod_fwd(xv, gn_ref[...], sh_ref[...], sc_ref[...])
        hb = h.astype(BF)
        facc = jnp.zeros((tm, D), F32)
        for j in range(2):
            a = _nt(hb, wg_ref[j])
            b = _nt(hb, wu_ref[j])
            a_ref[:, j * FH:(j + 1) * FH] = a.astype(BF)
            b_ref[:, j * FH:(j + 1) * FH] = b.astype(BF)
            s = (a * _sigmoid(a) * b).astype(BF)
            facc = facc + _nn(s, wd_ref[j])
        f_ref[...] = facc
        xo_ref[...] = xv + 0.5 * gt_ref[...] * facc

    vec = _const_spec((1, D))
    wspec = _const_spec((2, FH, D))
    return pl.pallas_call(
        body, grid=(T // tm,), name=name,
        in_specs=[_row_spec(tm, D), vec, vec, vec, vec, wspec, wspec, wspec],
        out_specs=[_row_spec(tm, D), _row_spec(tm, D), _row_spec(tm, FF), _row_spec(tm, FF)],
        out_shape=[jax.ShapeDtypeStruct((T, D), F32), jax.ShapeDtypeStruct((T, D), F32),
                   jax.ShapeDtypeStruct((T, FF), BF), jax.ShapeDtypeStruct((T, FF), BF)],
        compiler_params=_cparams(),
    )(x, shift, scale, gate, gain, wgT, wuT, wd)


def _ffn_dgrad(dxo, x, f, a, b, shift, scale, gate, gain, wgT, wuT, wd, name):
    T = x.shape[0]
    tm = _tile(T, 256)

    def body(dxo_ref, x_ref, f_ref, a_ref, b_ref, sh_ref, sc_ref, gt_ref, gn_ref, wg_ref, wu_ref, wd_ref,
             dx_ref, da_ref, db_ref, s_ref, h_ref, df_ref, dsh_ref, dsc_ref, dgt_ref, dgn_ref):
        i = pl.program_id(0)
        dxo_v = dxo_ref[...]
        xv = x_ref[...]
        gain_v, scale_v = gn_ref[...], sc_ref[...]
        n, r, y, h = _norm_mod_fwd(xv, gain_v, sh_ref[...], scale_v)
        h_ref[...] = h.astype(BF)
        dgate = _sum8(0.5 * f_ref[...] * dxo_v)
        dfb = (0.5 * gt_ref[...] * dxo_v).astype(BF)
        df_ref[...] = dfb
        dh = jnp.zeros((tm, D), F32)
        for j in range(2):
            cols = slice(j * FH, (j + 1) * FH)
            ds = _nt(dfb, wd_ref[j])
            av = a_ref[:, cols].astype(F32)
            bv = b_ref[:, cols].astype(F32)
            sig = _sigmoid(av)
            sa = av * sig
            da = (ds * bv * (sig * (1.0 + av * (1.0 - sig)))).astype(BF)
            db = (ds * sa).astype(BF)
            da_ref[:, cols] = da
            db_ref[:, cols] = db
            s_ref[:, cols] = (sa * bv).astype(BF)
            dh = dh + _nn(da, wg_ref[j]) + _nn(db, wu_ref[j])
        dxn, dsh, dsc, dgn = _norm_mod_bwd(dh, n, r, y, gain_v, scale_v)
        dx_ref[...] = dxo_v + dxn

        @pl.when(i == 0)
        def _():
            dsh_ref[...] = dsh
            dsc_ref[...] = dsc
            dgt_ref[...] = dgate
            dgn_ref[...] = dgn

        @pl.when(i > 0)
        def _():
            dsh_ref[...] += dsh
            dsc_ref[...] += dsc
            dgt_ref[...] += dgate
            dgn_ref[...] += dgn

    vec = _const_spec((1, D))
    wspec = _const_spec((2, FH, D))
    acc = pl.BlockSpec((8, D), lambda i: (0, 0))
    accs = jax.ShapeDtypeStruct((8, D), F32)
    return pl.pallas_call(
        body, grid=(T // tm,), name=name,
        in_specs=[_row_spec(tm, D), _row_spec(tm, D), _row_spec(tm, D), _row_spec(tm, FF), _row_spec(tm, FF),
                  vec, vec, vec, vec, wspec, wspec, wspec],
        out_specs=[_row_spec(tm, D), _row_spec(tm, FF), _row_spec(tm, FF), _row_spec(tm, FF),
                   _row_spec(tm, D), _row_spec(tm, D), acc, acc, acc, acc],
        out_shape=[jax.ShapeDtypeStruct((T, D), F32), jax.ShapeDtypeStruct((T, FF), BF),
                   jax.ShapeDtypeStruct((T, FF), BF), jax.ShapeDtypeStruct((T, FF), BF),
                   jax.ShapeDtypeStruct((T, D), BF), jax.ShapeDtypeStruct((T, D), BF), accs, accs, accs, accs],
        compiler_params=_cparams(),
    )(dxo, x, f, a, b, shift, scale, gate, gain, wgT, wuT, wd)


def _ffn_wgrad(da, db, s, h, df, name):
    T = h.shape[0]
    tk = _tile(T, 256)
    nk = T // tk

    def body(da_ref, db_ref, s_ref, h_ref, df_ref, og_ref, ou_ref, od_ref, ag, au, ad):
        k = pl.program_id(1)
        pg = _tn(da_ref[...], h_ref[...])
        pu = _tn(db_ref[...], h_ref[...])
        pd = _tn(s_ref[...], df_ref[...])

        @pl.when(k == 0)
        def _():
            ag[...] = pg
            au[...] = pu
            ad[...] = pd

        @pl.when(k > 0)
        def _():
            ag[...] += pg
            au[...] += pu
            ad[...] += pd

        @pl.when(k == nk - 1)
        def _():
            og_ref[...] = ag[...].astype(BF)
            ou_ref[...] = au[...].astype(BF)
            od_ref[...] = ad[...].astype(BF)

    colblk = pl.BlockSpec((tk, FH), lambda j, k: (k, j))
    rowblk = pl.BlockSpec((tk, D), lambda j, k: (k, 0))
    outblk = pl.BlockSpec((FH, D), lambda j, k: (j, 0))
    outs = jax.ShapeDtypeStruct((FF, D), BF)
    return pl.pallas_call(
        body, grid=(2, nk), name=name,
        in_specs=[colblk, colblk, colblk, rowblk, rowblk],
        out_specs=[outblk, outblk, outblk],
        out_shape=[outs, outs, outs],
        scratch_shapes=[pltpu.VMEM((FH, D), F32)] * 3,
        compiler_params=_cparams(("arbitrary", "arbitrary")),
    )(da, db, s, h, df)


def _loss_head(x, target, gain):
    T = x.shape[0]
    tm = _tile(T)

    def body(x_ref, t_ref, gn_ref, ls_ref, dx_ref, dgn_ref):
        i = pl.program_id(0)
        xv = x_ref[...]
        gain_v = gn_ref[...]
        r = lax.rsqrt(jnp.mean(xv * xv, axis=-1, keepdims=True) + EPS)
        n = xv * r
        err = n * gain_v - t_ref[...]
        e2 = err * err
        part = e2[:, 0:128]
        for q in range(1, D // 128):
            part = part + e2[:, q * 128:(q + 1) * 128]
        lsum = _sum8(part) * (0.5 / D)
        dy = err * (1.0 / D)
        dn = dy * gain_v
        dx_ref[...] = r * (dn - n * jnp.mean(dn * n, axis=-1, keepdims=True))
        dgn = _sum8(dy * n)

        @pl.when(i == 0)
        def _():
            ls_ref[...] = lsum
            dgn_ref[...] = dgn

        @pl.when(i > 0)
        def _():
            ls_ref[...] += lsum
            dgn_ref[...] += dgn

    return pl.pallas_call(
        body, grid=(T // tm,), name="loss_head",
        in_specs=[_row_spec(tm, D), _row_spec(tm, D), _const_spec((1, D))],
        out_specs=[pl.BlockSpec((8, 128), lambda i: (0, 0)), _row_spec(tm, D), pl.BlockSpec((8, D), lambda i: (0, 0))],
        out_shape=[jax.ShapeDtypeStruct((8, 128), F32), jax.ShapeDtypeStruct((T, D), F32),
                   jax.ShapeDtypeStruct((8, D), F32)],
        compiler_params=_cparams(),
    )(x, target, gain)


def _seg_cumsum(v, row_in_chunk, reverse=False):
    n = v.shape[0]
    s = 1
    while s < CH:
        if reverse:
            moved = pltpu.roll(v, n - s, 0)
            ok = row_in_chunk < CH - s
        else:
            moved = pltpu.roll(v, s, 0)
            ok = row_in_chunk >= s
        v = v + jnp.where(ok, moved, 0.0)
        s *= 2
    return v


def _conv_silu(xq_ext_ref, cw_ref, tm):
    c = cw_ref[0:1, :] * xq_ext_ref[pl.ds(5, tm), :]
    for j in range(1, 4):
        c = c + cw_ref[j:j + 1, :] * xq_ext_ref[pl.ds(5 + j, tm), :]
    return c, _sigmoid(c)


def _gates(ba, alog, dtb, lane):
    beta = _sigmoid(ba)
    arg = ba + dtb
    softplus = jnp.maximum(arg, 0.0) + jnp.log(1.0 + jnp.exp(-jnp.abs(arg)))
    g = -jnp.exp(alog) * softplus
    return jnp.where(lane < NH, beta, 0.0), jnp.where((lane >= NH) & (lane < 2 * NH), g, 0.0), _sigmoid(arg)


def _mix_proj(x, shift, scale, gain, winT, conv_w, alog, dtb):
    T = x.shape[0]
    tm = _tile(T)

    def body(x_ref, sh_ref, sc_ref, gn_ref, w_ref, cw_ref, al_ref, dt_ref,
             xq_ref, ba_ref, qn_ref, kn_ref, v_ref, z_ref, p_ref, bg_ref, gc_ref, ext):
        i = pl.program_id(0)
        _, _, _, h = _norm_mod_fwd(x_ref[...], gn_ref[...], sh_ref[...], sc_ref[...])
        hb = h.astype(BF)

        @pl.when(i == 0)
        def _():
            ext[pl.ds(0, 8), :] = jnp.zeros((8, 3 * GW), F32)

        xq = _nt(hb, w_ref[pl.ds(0, 3 * GW), :])
        xq_ref[...] = xq
        ext[pl.ds(8, tm), :] = xq
        z_ref[...] = _nt(hb, w_ref[pl.ds(3 * GW, GW), :])
        p_ref[...] = _nt(hb, w_ref[pl.ds(4 * GW, PW), :])
        ba = _nt(hb, w_ref[pl.ds(4 * GW + PW, 128), :])
        ba_ref[...] = ba

        c, sg = _conv_silu(ext, cw_ref, tm)
        ext[pl.ds(0, 8), :] = ext[pl.ds(tm, 8), :]
        qt = c * sg
        for hd in range(NH):
            cq = slice(hd * DH, (hd + 1) * DH)
            ck = slice(GW + hd * DH, GW + (hd + 1) * DH)
            qh, kh = qt[:, cq], qt[:, ck]
            qn_ref[:, cq] = qh * (lax.rsqrt(jnp.sum(qh * qh, axis=-1, keepdims=True) + EPS) * DH ** -0.5)
            kn_ref[:, cq] = kh * lax.rsqrt(jnp.sum(kh * kh, axis=-1, keepdims=True) + EPS)
        v_ref[...] = qt[:, 2 * GW:3 * GW]

        lane = lax.broadcasted_iota(jnp.int32, (tm, 128), 1)
        row = lax.broadcasted_iota(jnp.int32, (tm, 128), 0) % CH
        beta, g, _ = _gates(ba, al_ref[...], dt_ref[...], lane)
        bg_ref[...] = beta + g
        gc_ref[...] = _seg_cumsum(g, row)

    wide = lambda w: _row_spec(tm, w)
    shp = lambda w: jax.ShapeDtypeStruct((T, w), F32)
    return pl.pallas_call(
        body, grid=(T // tm,), name="mix_proj",
        in_specs=[wide(D), _const_spec((1, D)), _const_spec((1, D)), _const_spec((1, D)), _const_spec((DINP, D)),
                  _const_spec((4, 3 * GW)), _const_spec((1, 128)), _const_spec((1, 128))],
        out_specs=[wide(3 * GW), wide(128), wide(GW), wide(GW), wide(GW), wide(GW), wide(PW), wide(128), wide(128)],
        out_shape=[shp(3 * GW), shp(128), shp(GW), shp(GW), shp(GW), shp(GW), shp(PW), shp(128), shp(128)],
        scratch_shapes=[pltpu.VMEM((tm + 8, 3 * GW), F32)],
        compiler_params=_cparams(),
    )(x, shift, scale, gain, winT, conv_w, alog, dtb)


def _tri_inverse(m, eye):
    neg = -m
    t = eye + neg
    pw = neg
    for _ in range(5):
        pw = _nn(pw, pw, precision=HI)
        t = _nn(t, eye + pw, precision=HI)
    return t


def _chunk_forward(q, k, v, beta, gcv, gl, S, ii, jj, eye, ones):
    causal = ii >= jj
    gc_b = jnp.broadcast_to(gcv, (CH, CH))
    gc_row = _nn(ones, eye * gc_b, precision=HI)
    gam = jnp.where(causal, jnp.exp(jnp.where(causal, gc_b - gc_row, 0.0)), 0.0)
    kb = k * beta
    vb = v * beta
    kbf = k.astype(BF)
    P = _nt(kb.astype(BF), kbf)
    QK = _nt(q.astype(BF), kbf)
    M = jnp.where(ii > jj, P * gam, 0.0)
    Tm = _tri_inverse(M, eye)
    E = jnp.exp(gcv)
    Fd = jnp.exp(gl - gcv)
    egl = jnp.exp(gl)
    kbE = kb * E
    u = _nn(Tm, vb, precision=HI)
    w = _nn(Tm, kbE, precision=HI)
    Sb = S.astype(BF)
    vn = u - _nn(w.astype(BF), Sb)
    Q = QK * gam
    qE = q * E
    kF = k * Fd
    o = _nn(qE.astype(BF), Sb) + _nn(Q.astype(BF), vn.astype(BF))
    S_new = S * egl + _tn(kF.astype(BF), vn.astype(BF))
    return dict(gam=gam, kb=kb, vb=vb, P=P, QK=QK, Tm=Tm, E=E, Fd=Fd, egl=egl, kbE=kbE, u=u, w=w, vn=vn, Q=Q, qE=qE,
                kF=kF, o=o, S_new=S_new)


def _pool_windows(ext, tm, reverse):
    n = tm + HALO
    outs = []
    for gi in range(NG):
        a = ext[:, gi * 128:(gi + 1) * 128]
        s = 1
        while s < POOL_WINDOWS[gi]:
            a = a + pltpu.roll(a, (n - s) if reverse else s, 0)
            s *= 2
        outs.append(a[0:tm] if reverse else a[HALO:HALO + tm])
    return jnp.concatenate(outs, axis=1)


def _pool_count(tm, tile_index):
    t1 = (lax.broadcasted_iota(jnp.int32, (tm, PW), 0) + tile_index * tm + 1).astype(F32)
    win = jnp.concatenate([jnp.full((tm, 128), float(w), F32) for w in POOL_WINDOWS], axis=1)
    return 1.0 / jnp.minimum(t1, win)


def _mix_core(x, gate, qn, kn, v, z, p, bg, gc, gnorm, pool_w, pool_scale, w_out):
    T = x.shape[0]
    tm = _tile(T)
    nc = tm // CH

    def body(x_ref, gt_ref, qn_ref, kn_ref, v_ref, z_ref, p_ref, bg_ref, gc_ref, gnm_ref, pw_ref, ps_ref, wo_ref,
             xo_ref, mx_ref, cat_ref, o_ref, sall_ref, S_scr, pext):
        i = pl.program_id(0)

        @pl.when(i == 0)
        def _():
            S_scr[...] = jnp.zeros((NH, DH, DH), F32)
            pext[pl.ds(0, HALO), :] = jnp.zeros((HALO, PW), F32)

        ii = lax.broadcasted_iota(jnp.int32, (CH, CH), 0)
        jj = lax.broadcasted_iota(jnp.int32, (CH, CH), 1)
        eye = (ii == jj).astype(F32)
        ones = jnp.ones((CH, CH), F32)

        def chunk(c, carry):
            r0 = pl.multiple_of(c * CH, CH)
            rows = pl.ds(r0, CH)
            bgv = bg_ref[rows, :]
            gcv_all = gc_ref[rows, :]
            gl_all = gc_ref[pl.ds(r0 + CH - 1, 1), :]
            for hd in range(NH):
                cols = slice(hd * DH, (hd + 1) * DH)
                S = S_scr[hd]
                sall_ref[c, hd] = S
                res = _chunk_forward(qn_ref[rows, cols], kn_ref[rows, cols], v_ref[rows, cols],
                                     bgv[:, hd:hd + 1], gcv_all[:, NH + hd:NH + hd + 1],
                                     gl_all[:, NH + hd:NH + hd + 1], S, ii, jj, eye, ones)
                o_ref[rows, cols] = res["o"]
                S_scr[hd] = res["S_new"]
            return carry

        lax.fori_loop(0, nc, chunk, 0)

        for hd in range(NH):
            cols = slice(hd * DH, (hd + 1) * DH)
            oh = o_ref[:, cols]
            zh = z_ref[:, cols]
            r = lax.rsqrt(jnp.mean(oh * oh, axis=-1, keepdims=True) + EPS)
            cat_ref[:, cols] = (oh * r * gnm_ref[...] * (zh * _sigmoid(zh))).astype(BF)

        pv = p_ref[...]
        pext[pl.ds(HALO, tm), :] = pv
        pooled = _pool_windows(pext[...], tm, False) * _pool_count(tm, i) - pv
        pext[pl.ds(0, HALO), :] = pext[pl.ds(tm, HALO), :]
        for gi in range(NG):
            cols = slice(gi * 128, (gi + 1) * 128)
            pm = _nn(pooled[:, cols].astype(BF), pw_ref[gi])
            cat_ref[:, GW + gi * 128:GW + (gi + 1) * 128] = (pm * ps_ref[:, cols]).astype(BF)

        mixed = _nn(cat_ref[...], wo_ref[...])
        mx_ref[...] = mixed
        xo_ref[...] = x_ref[...] + gt_ref[...] * mixed

    wide = lambda w: _row_spec(tm, w)
    return pl.pallas_call(
        body, grid=(T // tm,), name="mix_core",
        in_specs=[wide(D), _const_spec((1, D)), wide(GW), wide(GW), wide(GW), wide(GW), wide(PW), wide(128), wide(128),
                  _const_spec((1, DH)), _const_spec((NG, 128, 128)), _const_spec((1, PW)), _const_spec((D, D))],
        out_specs=[wide(D), wide(D), wide(D), wide(GW), pl.BlockSpec((nc, NH, DH, DH), lambda i: (i, 0, 0, 0))],
        out_shape=[jax.ShapeDtypeStruct((T, D), F32), jax.ShapeDtypeStruct((T, D), F32),
                   jax.ShapeDtypeStruct((T, D), BF), jax.ShapeDtypeStruct((T, GW), F32),
                   jax.ShapeDtypeStruct((T // CH, NH, DH, DH), F32)],
        scratch_shapes=[pltpu.VMEM((NH, DH, DH), F32), pltpu.VMEM((tm + HALO, PW), F32)],
        compiler_params=_cparams(),
    )(x, gate, qn, kn, v, z, p, bg, gc, gnorm, pool_w, pool_scale, w_out)


def _chunk_backward(fw, q, k, v, beta, S, do, dSn, ii, jj, ones_cd):
    bf = lambda t: t.astype(BF)
    Sb, dSb, dob, vnb = bf(S), bf(dSn), bf(do), bf(fw["vn"])
    dvn = _tn(bf(fw["Q"]), dob) + _nn(bf(fw["kF"]), dSb)
    dQ = _nt(dob, vnb)
    dqE = _nt(dob, Sb)
    dkF = _nt(vnb, dSb)
    dvnb = bf(dvn)
    dw = -_nt(dvnb, Sb)
    dS_new = _tn(bf(fw["qE"]), dob) + fw["egl"] * dSn - _tn(bf(fw["w"]), dvnb)
    d_egl = jnp.sum(jnp.sum(dSn * S, axis=1, keepdims=True), axis=0, keepdims=True)
    Tm = fw["Tm"]
    dvb = _tn(Tm, dvn, precision=HI)
    dkbE = _tn(Tm, dw, precision=HI)
    dTm = _nt(dvn, fw["vb"], precision=HI) + _nt(dw, fw["kbE"], precision=HI)
    dA = -_tn(Tm, _nt(dTm, Tm, precision=HI), precision=HI)
    gam = fw["gam"]
    N = jnp.where(ii > jj, dA * gam, 0.0)
    Rm = dQ * gam
    Wm = Rm * fw["QK"] + N * fw["P"]
    dgc = jnp.sum(Wm, axis=1, keepdims=True) - _tn(Wm, ones_cd, precision=HI)[:, 0:1]
    kbf, Nb, Rb = bf(k), bf(N), bf(Rm)
    E, Fd = fw["E"], fw["Fd"]
    dq = dqE * E + _nn(Rb, kbf)
    dkb = dkbE * E + _nn(Nb, kbf)
    dk = dkF * Fd + _tn(Rb, bf(q)) + _tn(Nb, bf(fw["kb"])) + beta * dkb
    dbeta = jnp.sum(dkb * k, axis=1, keepdims=True) + jnp.sum(dvb * v, axis=1, keepdims=True)
    dv = beta * dvb
    dE = jnp.sum(dqE * q, axis=1, keepdims=True) + jnp.sum(dkbE * fw["kb"], axis=1, keepdims=True)
    dFd = jnp.sum(dkF * k, axis=1, keepdims=True)
    dgl = d_egl * fw["egl"] + jnp.sum(dFd * Fd, axis=0, keepdims=True)
    last = lax.broadcasted_iota(jnp.int32, (CH, 1), 0) == CH - 1
    dgc = dgc + dE * E - dFd * Fd + jnp.where(last, dgl, 0.0)
    return dq, dk, dv, dbeta, dgc, dS_new


def _mix_core_bwd(dxo, gate, mixed, cat, o, sall, qn, kn, v, z, p, bg, gc, gnorm, pool_w, pool_scale, w_out):
    T = dxo.shape[0]
    tm = _tile(T, 256)
    nt = T // tm
    nc = tm // CH

    def body(dx_ref, gt_ref, mx_ref, cat_ref, o_ref, sall_ref, qn_ref, kn_ref, v_ref, z_ref, p_ref, ph_ref, bg_ref,
             gc_ref, gnm_ref, pw_ref, ps_ref, wo_ref,
             dq_ref, dk_ref, dv_ref, dz_ref, dp_ref, dbg_ref, dgt_ref, dwo_ref, dpw_ref, dps_ref, dgn_ref,
             dS_scr, pext, yext, do_buf, dwo_acc):
        i = pl.program_id(0)
        ti = nt - 1 - i

        @pl.when(i == 0)
        def _():
            dS_scr[...] = jnp.zeros((NH, DH, DH), F32)
            yext[pl.ds(tm, HALO), :] = jnp.zeros((HALO, PW), F32)
            dwo_acc[...] = jnp.zeros((D, D), F32)
            dgt_ref[...] = jnp.zeros((8, D), F32)
            dpw_ref[...] = jnp.zeros((NG, 128, 128), F32)
            dps_ref[...] = jnp.zeros((8, PW), F32)
            dgn_ref[...] = jnp.zeros((8, DH), F32)

        dx2 = dx_ref[...]
        dgt_ref[...] += _sum8(mx_ref[...] * dx2)
        dmix = (gt_ref[...] * dx2).astype(BF)
        dcat = _nt(dmix, wo_ref[...])
        dwo_acc[...] += _tn(cat_ref[...], dmix)

        pv = p_ref[...]
        pext[pl.ds(0, HALO), :] = jnp.where(ti == 0, 0.0, ph_ref[...])
        pext[pl.ds(HALO, tm), :] = pv
        inv_cnt = _pool_count(tm, ti)
        pooled = _pool_windows(pext[...], tm, False) * inv_cnt - pv
        dpooled = []
        for gi in range(NG):
            cols = slice(gi * 128, (gi + 1) * 128)
            pgb = pooled[:, cols].astype(BF)
            pm = _nn(pgb, pw_ref[gi])
            dpo = dcat[:, GW + gi * 128:GW + (gi + 1) * 128]
            dps_ref[:, cols] += _sum8(dpo * pm)
            dpm = (dpo * ps_ref[:, cols]).astype(BF)
            dpooled.append(_nt(dpm, pw_ref[gi]))
            dpw_ref[gi] += _tn(pgb, dpm)
        dpooled = jnp.concatenate(dpooled, axis=1)
        y = dpooled * inv_cnt
        yext[pl.ds(0, tm), :] = y
        dp_ref[...] = _pool_windows(yext[...], tm, True) - dpooled
        yext[pl.ds(tm, HALO), :] = y[0:HALO]

        gnm = gnm_ref[...]
        dgn = jnp.zeros((8, DH), F32)
        for hd in range(NH):
            cols = slice(hd * DH, (hd + 1) * DH)
            oh = o_ref[:, cols]
            zh = z_ref[:, cols]
            r = lax.rsqrt(jnp.mean(oh * oh, axis=-1, keepdims=True) + EPS)
            n = oh * r
            sg = _sigmoid(zh)
            zs = zh * sg
            dgo = dcat[:, cols]
            dgn = dgn + _sum8(dgo * zs * n)
            dn = dgo * zs * gnm
            do_buf[:, cols] = r * (dn - n * jnp.mean(dn * n, axis=-1, keepdims=True))
            dz_ref[:, cols] = dgo * n * gnm * (sg * (1.0 + zh * (1.0 - sg)))
        dgn_ref[...] += dgn

        ii = lax.broadcasted_iota(jnp.int32, (CH, CH), 0)
        jj = lax.broadcasted_iota(jnp.int32, (CH, CH), 1)
        eye = (ii == jj).astype(F32)
        ones = jnp.ones((CH, CH), F32)
        ones_cd = jnp.ones((CH, 128), F32)
        lane_c = lax.broadcasted_iota(jnp.int32, (CH, 128), 1)

        def chunk(cc, carry):
            c = nc - 1 - cc
            r0 = pl.multiple_of(c * CH, CH)
            rows = pl.ds(r0, CH)
            bgv = bg_ref[rows, :]
            gcv_all = gc_ref[rows, :]
            gl_all = gc_ref[pl.ds(r0 + CH - 1, 1), :]
            dbg = jnp.zeros((CH, 128), F32)
            for hd in range(NH):
                cols = slice(hd * DH, (hd + 1) * DH)
                S = sall_ref[c, hd]
                q, k, vv = qn_ref[rows, cols], kn_ref[rows, cols], v_ref[rows, cols]
                beta = bgv[:, hd:hd + 1]
                fw = _chunk_forward(q, k, vv, beta, gcv_all[:, NH + hd:NH + hd + 1], gl_all[:, NH + hd:NH + hd + 1],
                                    S, ii, jj, eye, ones)
                dq, dk, dv, dbeta, dgc, dS_new = _chunk_backward(fw, q, k, vv, beta, S, do_buf[rows, cols],
                                                                 dS_scr[hd], ii, jj, ones_cd)
                dq_ref[rows, cols] = dq
                dk_ref[rows, cols] = dk
                dv_ref[rows, cols] = dv
                dS_scr[hd] = dS_new
                dbg = dbg + jnp.where(lane_c == hd, dbeta, 0.0) + jnp.where(lane_c == NH + hd, dgc, 0.0)
            dbg_ref[rows, :] = dbg
            return carry

        lax.fori_loop(0, nc, chunk, 0)

        lane = lax.broadcasted_iota(jnp.int32, (tm, 128), 1)
        row = lax.broadcasted_iota(jnp.int32, (tm, 128), 0) % CH
        dbg_all = dbg_ref[...]
        dg = _seg_cumsum(jnp.where(lane >= NH, dbg_all, 0.0), row, reverse=True)
        dbg_ref[...] = jnp.where(lane < NH, dbg_all, dg)

        @pl.when(i == nt - 1)
        def _():
            dwo_ref[...] = dwo_acc[...].astype(BF)

    rev = lambda w: pl.BlockSpec((tm, w), lambda i: (nt - 1 - i, 0))
    halo = pl.BlockSpec((HALO, PW), lambda i: (jnp.maximum((nt - 1 - i) * (tm // HALO) - 1, 0), 0))
    shp = lambda w: jax.ShapeDtypeStruct((T, w), F32)
    fix = lambda *s: pl.BlockSpec(s, lambda i: (0,) * len(s))
    return pl.pallas_call(
        body, grid=(nt,), name="mix_core_bwd",
        in_specs=[rev(D), _const_spec((1, D)), rev(D), rev(D), rev(GW),
                  pl.BlockSpec((nc, NH, DH, DH), lambda i: (nt - 1 - i, 0, 0, 0)),
                  rev(GW), rev(GW), rev(GW), rev(GW), rev(PW), halo, rev(128), rev(128),
                  _const_spec((1, DH)), _const_spec((NG, 128, 128)), _const_spec((1, PW)), _const_spec((D, D))],
        out_specs=[rev(GW), rev(GW), rev(GW), rev(GW), rev(PW), rev(128),
                   fix(8, D), fix(D, D), fix(NG, 128, 128), fix(8, PW), fix(8, DH)],
        out_shape=[shp(GW), shp(GW), shp(GW), shp(GW), shp(PW), shp(128),
                   jax.ShapeDtypeStruct((8, D), F32), jax.ShapeDtypeStruct((D, D), BF),
                   jax.ShapeDtypeStruct((NG, 128, 128), F32), jax.ShapeDtypeStruct((8, PW), F32),
                   jax.ShapeDtypeStruct((8, DH), F32)],
        scratch_shapes=[pltpu.VMEM((NH, DH, DH), F32), pltpu.VMEM((tm + HALO, PW), F32),
                        pltpu.VMEM((tm + HALO, PW), F32), pltpu.VMEM((tm, GW), F32), pltpu.VMEM((D, D), F32)],
        compiler_params=_cparams(),
    )(dxo, gate, mixed, cat, o, sall, qn, kn, v, z, p, p, bg, gc, gnorm, pool_w, pool_scale, w_out)


def _mix_proj_bwd(dxo, dqn, dkn, dv, dz, dp, dbg, xq, ba, x, shift, scale, gain, winT, conv_w, alog, dtb):
    T = x.shape[0]
    tm = min(256, T)
    nt = T // tm
    W3 = 3 * GW

    def body(dxo_ref, dqn_ref, dkn_ref, dv_ref, dz_ref, dp_ref, dbg_ref, xq_ref, xh_ref, ba_ref, x_ref, sh_ref, sc_ref,
             gn_ref, w_ref, cw_ref, al_ref, dt_ref,
             dx_ref, dw_ref, dcw_ref, dal_ref, ddt_ref, dsh_ref, dsc_ref, dgn_ref,
             ext, dcext, dproj, dw_acc):
        i = pl.program_id(0)
        ti = nt - 1 - i

        @pl.when(i == 0)
        def _():
            dcext[pl.ds(tm, 8), :] = jnp.zeros((8, W3), F32)
            dw_acc[...] = jnp.zeros((DINP, D), F32)
            dcw_ref[...] = jnp.zeros((4, 8, W3), F32)
            dal_ref[...] = jnp.zeros((8, 128), F32)
            ddt_ref[...] = jnp.zeros((8, 128), F32)
            dsh_ref[...] = jnp.zeros((8, D), F32)
            dsc_ref[...] = jnp.zeros((8, D), F32)
            dgn_ref[...] = jnp.zeros((8, D), F32)

        ext[pl.ds(0, 8), :] = jnp.where(ti == 0, 0.0, xh_ref[...])
        ext[pl.ds(8, tm), :] = xq_ref[...]
        c, sg = _conv_silu(ext, cw_ref, tm)
        qt = c * sg
        dsilu = sg * (1.0 + c * (1.0 - sg))
        for hd in range(NH):
            for part, dref, mult in ((0, dqn_ref, DH ** -0.5), (1, dkn_ref, 1.0)):
                cols = slice(part * GW + hd * DH, part * GW + (hd + 1) * DH)
                xh = qt[:, cols]
                rr = lax.rsqrt(jnp.sum(xh * xh, axis=-1, keepdims=True) + EPS)
                unit = xh * rr
                du = dref[:, hd * DH:(hd + 1) * DH] * mult
                dxh = rr * (du - unit * jnp.sum(du * unit, axis=-1, keepdims=True))
                dcext[pl.ds(0, tm), cols] = dxh * dsilu[:, cols]
        dcext[pl.ds(0, tm), 2 * GW:W3] = dv_ref[...] * dsilu[:, 2 * GW:W3]
        dc = dcext[pl.ds(0, tm), :]
        dxq = jnp.zeros((tm, W3), F32)
        for j in range(4):
            dcw_ref[j] += _sum8(dc * ext[pl.ds(5 + j, tm), :])
            dxq = dxq + cw_ref[j:j + 1, :] * dcext[pl.ds(3 - j, tm), :]
        dcext[pl.ds(tm, 8), :] = dc[0:8]
        lane = lax.broadcasted_iota(jnp.int32, (tm, 128), 1)
        bav = ba_ref[...]
        beta, g, sarg = _gates(bav, al_ref[...], dt_ref[...], lane)
        dbg_v = dbg_ref[...]
        is_g = (lane >= NH) & (lane < 2 * NH)
        dbraw = jnp.where(lane < NH, dbg_v * beta * (1.0 - beta), 0.0)
        daraw = jnp.where(is_g, dbg_v * (-jnp.exp(al_ref[...])) * sarg, 0.0)
        dal_ref[...] += _sum8(jnp.where(is_g, dbg_v * g, 0.0))
        ddt_ref[...] += _sum8(daraw)
        dproj[:, 0:W3] = dxq.astype(BF)
        dproj[:, W3:W3 + GW] = dz_ref[...].astype(BF)
        dproj[:, W3 + GW:W3 + GW + PW] = dp_ref[...].astype(BF)
        dproj[:, W3 + GW + PW:DINP] = (dbraw + daraw).astype(BF)
        gain_v, scale_v = gn_ref[...], sc_ref[...]
        n, r, y, h = _norm_mod_fwd(x_ref[...], gain_v, sh_ref[...], scale_v)
        dpj = dproj[...]
        dh = _nn(dpj, w_ref[...])
        dw_acc[...] += _tn(dpj, h.astype(BF))
        dxn, dsh, dsc, dgn = _norm_mod_bwd(dh, n, r, y, gain_v, scale_v)
        dx_ref[...] = dxo_ref[...] + dxn
        dsh_ref[...] += dsh
        dsc_ref[...] += dsc
        dgn_ref[...] += dgn

        @pl.when(i == nt - 1)
        def _():
            dw_ref[...] = dw_acc[...].astype(BF)

    rev = lambda w: pl.BlockSpec((tm, w), lambda i: (nt - 1 - i, 0))
    halo = pl.BlockSpec((8, W3), lambda i: (jnp.maximum((nt - 1 - i) * (tm // 8) - 1, 0), 0))
    fix = lambda *s: pl.BlockSpec(s, lambda i: (0,) * len(s))
    vec = _const_spec((1, D))
    return pl.pallas_call(
        body, grid=(nt,), name="mix_proj_bwd",
        in_specs=[rev(D), rev(GW), rev(GW), rev(GW), rev(GW), rev(PW), rev(128), rev(W3), halo, rev(128), rev(D),
                  vec, vec, vec, _const_spec((DINP, D)), _const_spec((4, W3)), _const_spec((1, 128)),
                  _const_spec((1, 128))],
        out_specs=[rev(D), fix(DINP, D), fix(4, 8, W3), fix(8, 128), fix(8, 128), fix(8, D), fix(8, D), fix(8, D)],
        out_shape=[jax.ShapeDtypeStruct((T, D), F32), jax.ShapeDtypeStruct((DINP, D), BF),
                   jax.ShapeDtypeStruct((4, 8, W3), F32), jax.ShapeDtypeStruct((8, 128), F32),
                   jax.ShapeDtypeStruct((8, 128), F32), jax.ShapeDtypeStruct((8, D), F32),
                   jax.ShapeDtypeStruct((8, D), F32), jax.ShapeDtypeStruct((8, D), F32)],
        scratch_shapes=[pltpu.VMEM((tm + 8, W3), F32), pltpu.VMEM((tm + 8, W3), F32), pltpu.VMEM((tm, DINP), BF),
                        pltpu.VMEM((DINP, D), F32)],
        compiler_params=_cparams(),
    )(dxo, dqn, dkn, dv, dz, dp, dbg, xq, xq, ba, x, shift, scale, gain, winT, conv_w, alog, dtb)


MESH = pl.DeviceIdType.MESH
CHIP_RELS = ((1, 0), (0, 1), (1, 1))
DEV_RELS = tuple((dx, dy, dc) for dx in (0, 1) for dy in (0, 1) for dc in (0, 1) if (dx, dy, dc) != (0, 0, 0))
NMOD = 9
ADA_SH = NMOD * D // 4
WIN_SH = DIN // 4
WIN_PAD = 656
MSG_ROWS = 16
ANY = pl.BlockSpec(memory_space=pl.ANY)
VM = pl.BlockSpec(memory_space=pltpu.VMEM)


def _place():
    x, y, c = lax.axis_index("x"), lax.axis_index("y"), lax.axis_index("c")
    return x, y, c


def _ada_exchange(msg, w_ada, b_ada):
    def body(msg_ref, w_ref, b_ref, all_ref, mod_ref, modp, send1, recv1, send2, recv2, lsem):
        x, y, c = _place()
        me = 4 * x + 2 * y + c
        own = pltpu.make_async_copy(msg_ref, all_ref.at[me], lsem.at[0])
        own.start()

        def gather(k, rel, slot):
            dx, dy, dc = rel
            return pltpu.make_async_remote_copy(
                src_ref=msg_ref, dst_ref=all_ref.at[slot], send_sem=send1.at[k], recv_sem=recv1.at[k],
                device_id=(x ^ dx, y ^ dy, c ^ dc), device_id_type=MESH)

        for k, rel in enumerate(DEV_RELS):
            gather(k, rel, me).start()
        for k, (dx, dy, dc) in enumerate(DEV_RELS):
            gather(k, (dx, dy, dc), 4 * (x ^ dx) + 2 * (y ^ dy) + (c ^ dc)).wait_recv()
        for k, rel in enumerate(DEV_RELS):
            gather(k, rel, me).wait_send()
        own.wait()

        for d in range(8):
            cv = all_ref[d, 0:8, :]
            act = cv * _sigmoid(cv)
            modp[d] = _nn(act, w_ref[...], precision=HI) + b_ref[...]

        myj = 2 * x + y
        keep = pltpu.make_async_copy(modp.at[me], mod_ref.at[myj], lsem.at[1])
        keep.start()

        def scatter(k, rel):
            dx, dy = rel
            return pltpu.make_async_remote_copy(
                src_ref=modp.at[4 * (x ^ dx) + 2 * (y ^ dy) + c], dst_ref=mod_ref.at[myj],
                send_sem=send2.at[k], recv_sem=recv2.at[k], device_id=(x ^ dx, y ^ dy, c), device_id_type=MESH)

        def landed(k, rel):
            dx, dy = rel
            return pltpu.make_async_remote_copy(
                src_ref=modp.at[me], dst_ref=mod_ref.at[2 * (x ^ dx) + (y ^ dy)],
                send_sem=send2.at[k], recv_sem=recv2.at[k], device_id=(x ^ dx, y ^ dy, c), device_id_type=MESH)

        for k, rel in enumerate(CHIP_RELS):
            scatter(k, rel).start()
        for k, rel in enumerate(CHIP_RELS):
            landed(k, rel).wait_recv()
        for k, rel in enumerate(CHIP_RELS):
            scatter(k, rel).wait_send()
        keep.wait()

    return pl.pallas_call(
        body, name="ada_exchange", in_specs=[VM, VM, VM], out_specs=[VM, VM],
        out_shape=[jax.ShapeDtypeStruct((8, MSG_ROWS, D), F32), jax.ShapeDtypeStruct((4, 8, ADA_SH), F32)],
        scratch_shapes=[pltpu.VMEM((8, 8, ADA_SH), F32), pltpu.SemaphoreType.DMA((7,)), pltpu.SemaphoreType.DMA((7,)),
                        pltpu.SemaphoreType.DMA((3,)), pltpu.SemaphoreType.DMA((3,)), pltpu.SemaphoreType.DMA((2,))],
        compiler_params=pltpu.CompilerParams(vmem_limit_bytes=VMEM_LIMIT),
    )(msg, w_ada, b_ada)


def _chip_exchange(parts, name, slots_from_source):
    n = len(parts)

    def body(*refs):
        ins, outs = refs[:n], refs[n:2 * n]
        send, recv, lsem = refs[2 * n:]
        x, y, c = _place()
        myj = 2 * x + y
        local, remote = [], []
        for a in range(n):
            src_own = ins[a].at[myj] if slots_from_source else ins[a]
            local.append(pltpu.make_async_copy(src_own, outs[a].at[myj], lsem.at[a]))
            for k, (dx, dy) in enumerate(CHIP_RELS):
                pj = 2 * (x ^ dx) + (y ^ dy)
                src = ins[a].at[pj] if slots_from_source else ins[a]
                remote.append((pltpu.make_async_remote_copy(
                    src_ref=src, dst_ref=outs[a].at[myj], send_sem=send.at[a, k], recv_sem=recv.at[a, k],
                    device_id=(x ^ dx, y ^ dy, c), device_id_type=MESH), a, k, pj))
        for cp in local:
            cp.start()
        for cp, _, _, _ in remote:
            cp.start()
        for _, a, k, pj in remote:
            src = ins[a].at[pj] if slots_from_source else ins[a]
            pltpu.make_async_remote_copy(
                src_ref=src, dst_ref=outs[a].at[pj], send_sem=send.at[a, k], recv_sem=recv.at[a, k],
                device_id=(x, y, c), device_id_type=MESH).wait_recv()
        for cp, _, _, _ in remote:
            cp.wait_send()
        for cp in local:
            cp.wait()

    shapes = [jax.ShapeDtypeStruct((4,) + tuple(p.shape[-2:]), p.dtype) for p in parts]
    return pl.pallas_call(
        body, name=name, in_specs=[ANY] * n, out_specs=[ANY] * n, out_shape=shapes,
        scratch_shapes=[pltpu.SemaphoreType.DMA((n, 3)), pltpu.SemaphoreType.DMA((n, 3)), pltpu.SemaphoreType.DMA((n,))],
    )(*parts)


def _all_exchange(parts, name):
    n = len(parts)

    def body(*refs):
        ins, outs = refs[:n], refs[n:2 * n]
        send, recv, lsem = refs[2 * n:]
        x, y, c = _place()
        me = 4 * x + 2 * y + c
        local, remote = [], []
        for a in range(n):
            local.append(pltpu.make_async_copy(ins[a], outs[a].at[me], lsem.at[a]))
            for k, (dx, dy, dc) in enumerate(DEV_RELS):
                remote.append((pltpu.make_async_remote_copy(
                    src_ref=ins[a], dst_ref=outs[a].at[me], send_sem=send.at[a, k], recv_sem=recv.at[a, k],
                    device_id=(x ^ dx, y ^ dy, c ^ dc), device_id_type=MESH), a, k,
                    4 * (x ^ dx) + 2 * (y ^ dy) + (c ^ dc)))
        for cp in local:
            cp.start()
        for cp, _, _, _ in remote:
            cp.start()
        for _, a, k, pd in remote:
            pltpu.make_async_remote_copy(
                src_ref=ins[a], dst_ref=outs[a].at[pd], send_sem=send.at[a, k], recv_sem=recv.at[a, k],
                device_id=(x, y, c), device_id_type=MESH).wait_recv()
        for cp, _, _, _ in remote:
            cp.wait_send()
        for cp in local:
            cp.wait()

    shapes = [jax.ShapeDtypeStruct((8,) + tuple(p.shape), p.dtype) for p in parts]
    return pl.pallas_call(
        body, name=name, in_specs=[ANY] * n, out_specs=[ANY] * n, out_shape=shapes,
        scratch_shapes=[pltpu.SemaphoreType.DMA((n, 7)), pltpu.SemaphoreType.DMA((n, 7)), pltpu.SemaphoreType.DMA((n,))],
    )(*parts)


def _pair_exchange(parts, name):
    n = len(parts)

    def body(*refs):
        ins, outs = refs[:n], refs[n:2 * n]
        send, recv = refs[2 * n:]
        x, y, c = _place()
        cps = [pltpu.make_async_remote_copy(
            src_ref=ins[a], dst_ref=outs[a], send_sem=send.at[a], recv_sem=recv.at[a],
            device_id=(x, y, 1 - c), device_id_type=MESH) for a in range(n)]
        for cp in cps:
            cp.start()
        for cp in cps:
            cp.wait_recv()
        for cp in cps:
            cp.wait_send()

    shapes = [jax.ShapeDtypeStruct(p.shape, p.dtype) for p in parts]
    return pl.pallas_call(
        body, name=name, in_specs=[ANY] * n, out_specs=[ANY] * n, out_shape=shapes,
        scratch_shapes=[pltpu.SemaphoreType.DMA((n,)), pltpu.SemaphoreType.DMA((n,))],
    )(*parts)


def _row_tile(rows, cap):
    best = rows
    for t in range(8, min(cap, rows) + 1, 8):
        if rows % t == 0:
            best = t
    return best if rows % 8 == 0 else rows


def _sum_slots(parts, name):
    n, rows, width = parts.shape
    tr = _row_tile(rows, 256)

    def body(p_ref, o_ref):
        acc = p_ref[0].astype(F32)
        for j in range(1, n):
            acc = acc + p_ref[j].astype(F32)
        o_ref[...] = acc

    return pl.pallas_call(
        body, grid=(rows // tr,), name=name,
        in_specs=[pl.BlockSpec((n, tr, width), lambda i: (0, i, 0))],
        out_specs=pl.BlockSpec((tr, width), lambda i: (i, 0)),
        out_shape=jax.ShapeDtypeStruct((rows, width), F32),
        compiler_params=_cparams(),
    )(parts)


def _adamw_math(g, w, m, v):
    m_new = ADAM_B1 * m + (1.0 - ADAM_B1) * g
    v_new = ADAM_B2 * v + (1.0 - ADAM_B2) * (g * g)
    m_hat = m_new / (1.0 - ADAM_B1 ** ADAM_STEP)
    v_hat = v_new / (1.0 - ADAM_B2 ** ADAM_STEP)
    delta = -ADAM_LR * (m_hat / (jnp.sqrt(v_hat) + ADAM_EPS) + ADAM_WD * w)
    return delta, m_new, v_new


def _adamw(grads, w, m, v, name):
    rows, width = w.shape
    tr = _row_tile(rows, 256 if width <= 1024 else 128)
    ng = len(grads)

    def body(*refs):
        g = refs[0][...]
        for r in refs[1:ng]:
            g = g + r[...]
        w_ref, m_ref, v_ref, g_out, d_out, m_out, v_out = refs[ng:]
        delta, m_new, v_new = _adamw_math(g, w_ref[...], m_ref[...], v_ref[...])
        g_out[...] = g
        d_out[...] = delta
        m_out[...] = m_new
        v_out[...] = v_new

    blk = pl.BlockSpec((tr, width), lambda i: (i, 0))
    return pl.pallas_call(
        body, grid=(rows // tr,), name=name,
        in_specs=[blk] * (ng + 3), out_specs=[blk] * 4,
        out_shape=[jax.ShapeDtypeStruct((rows, width), F32)] * 4,
        compiler_params=_cparams(),
    )(*grads, w, m, v)


def _adamw_ada(msgs, dmods, w, m, v):
    rows, width = w.shape
    tr = 128

    def body(c_ref, dm_ref, w_ref, m_ref, v_ref, g_out, d_out, m_out, v_out):
        cv = jnp.concatenate([c_ref[d, 0:1, :] for d in range(8)], axis=0)
        act = cv * _sigmoid(cv)
        g = _tn(act, dm_ref[...], precision=HI)
        delta, m_new, v_new = _adamw_math(g, w_ref[...], m_ref[...], v_ref[...])
        g_out[...] = g
        d_out[...] = delta
        m_out[...] = m_new
        v_out[...] = v_new

    blk = pl.BlockSpec((tr, width), lambda i: (i, 0))
    return pl.pallas_call(
        body, grid=(rows // tr,), name="adamw_w_ada",
        in_specs=[pl.BlockSpec((8, MSG_ROWS, tr), lambda i: (0, 0, i)), pl.BlockSpec((8, width), lambda i: (0, 0)),
                  blk, blk, blk],
        out_specs=[blk] * 4, out_shape=[jax.ShapeDtypeStruct((rows, width), F32)] * 4,
        compiler_params=_cparams(),
    )(msgs, dmods, w, m, v)


def _adamw_small(parts, w, m, v, name):
    n, rows, width = parts.shape

    def body(p_ref, w_ref, m_ref, v_ref, g_out, d_out, m_out, v_out):
        g = p_ref[0]
        for j in range(1, n):
            g = g + p_ref[j]
        delta, m_new, v_new = _adamw_math(g, w_ref[...], m_ref[...], v_ref[...])
        g_out[...] = g
        d_out[...] = delta
        m_out[...] = m_new
        v_out[...] = v_new

    return pl.pallas_call(
        body, name=name, in_specs=[VM] * 4, out_specs=[VM] * 4,
        out_shape=[jax.ShapeDtypeStruct((rows, width), F32)] * 4,
        compiler_params=pltpu.CompilerParams(vmem_limit_bytes=VMEM_LIMIT),
    )(parts, w, m, v)


SMALL_ROWS = 24


def _pad_row(vec, width=D):
    vec = vec.reshape(1, -1)
    return jnp.pad(vec, ((0, 0), (0, width - vec.shape[1])))


def _lanes_4_7(vec4):
    return jnp.zeros((1, 128), F32).at[0, NH:2 * NH].set(vec4.reshape(NH))


def kernel(x, c, w_ada, b_ada, norm_ffn1, ffn1_gate, ffn1_up, ffn1_down, norm_mix, w_in, conv_w, a_log, dt_bias, gdn_norm, pool_w, pool_scale, w_out, norm_ffn2, ffn2_gate, ffn2_up, ffn2_down, final_norm, loss_target, m_w_ada, m_b_ada, m_norm_ffn1, m_ffn1_gate, m_ffn1_up, m_ffn1_down, m_norm_mix, m_w_in, m_conv_w, m_a_log, m_dt_bias, m_gdn_norm, m_pool_w, m_pool_scale, m_w_out, m_norm_ffn2, m_ffn2_gate, m_ffn2_up, m_ffn2_down, m_final_norm, v_w_ada, v_b_ada, v_norm_ffn1, v_ffn1_gate, v_ffn1_up, v_ffn1_down, v_norm_mix, v_w_in, v_conv_w, v_a_log, v_dt_bias, v_gdn_norm, v_pool_w, v_pool_scale, v_w_out, v_norm_ffn2, v_ffn2_gate, v_ffn2_up, v_ffn2_down, v_final_norm):
    xs = x[0]
    tgt = loss_target[0]
    chip = 2 * lax.axis_index("x") + lax.axis_index("y")
    me = 2 * chip + lax.axis_index("c")

    msg = jnp.concatenate([jnp.broadcast_to(c, (8, D)), jnp.pad(conv_w[0], ((0, 0), (0, D - 3 * GW // 4))),
                           jnp.zeros((MSG_ROWS - 12, D), F32)], axis=0)
    b_sh = lax.dynamic_slice(b_ada, (0, chip * ADA_SH), (1, ADA_SH))
    msgs, mod4 = _ada_exchange(msg, w_ada[0], b_sh)
    mod = mod4[:, 0, :].reshape(NMOD, D)
    mrow = [mod[i:i + 1] for i in range(NMOD)]
    conv_full = jnp.concatenate([msgs[2 * j, 8:12, :3 * GW // 4] for j in range(4)], axis=1)

    packed = jnp.concatenate(
        [ffn1_gate[0].T, ffn1_up[0].T, ffn1_down[0], ffn2_gate[0].T, ffn2_up[0].T, ffn2_down[0], w_out[0],
         jnp.pad(w_in[0].T, ((0, WIN_PAD - WIN_SH), (0, 0)))], axis=0).astype(BF)
    (gathered,) = _chip_exchange([packed], "weight_gather", False)
    fsh = FF // 4

    def ffn_w(k):
        return gathered[:, k * fsh:(k + 1) * fsh, :].reshape(2, FH, D)

    wg1, wu1, wd1, wg2, wu2, wd2 = [ffn_w(k) for k in range(6)]
    wo = gathered[:, 6 * fsh:6 * fsh + D // 4, :].reshape(D, D)
    win_nat = gathered[:, 6 * fsh + D // 4:6 * fsh + D // 4 + WIN_SH, :].reshape(DIN, D)
    winT = jnp.concatenate([win_nat[:4 * GW], win_nat[4 * GW + 2 * NH:], win_nat[4 * GW:4 * GW + 2 * NH],
                            jnp.zeros((128 - 2 * NH, D), BF)], axis=0)
    alog, dtb = _lanes_4_7(a_log), _lanes_4_7(dt_bias)
    gnm = gdn_norm.reshape(1, DH)
    pwb = pool_w[0].astype(BF)
    psc = pool_scale.reshape(1, PW)
    fin = final_norm.reshape(1, D)

    x1, f1, a1, b1 = _ffn_fwd(xs, mrow[0], mrow[1], mrow[2], norm_ffn1, wg1, wu1, wd1, "ffn1_fwd")
    xq, ba, qn, kn, vv, z, pp, bg, gc = _mix_proj(x1, mrow[3], mrow[4], norm_mix, winT, conv_full, alog, dtb)
    x2, mixed, cat, o, sall = _mix_core(x1, mrow[5], qn, kn, vv, z, pp, bg, gc, gnm, pwb, psc, wo)
    x3, f2, a2, b2 = _ffn_fwd(x2, mrow[6], mrow[7], mrow[8], norm_ffn2, wg2, wu2, wd2, "ffn2_fwd")
    lpart, dx3, dfin = _loss_head(x3, tgt, fin)
    loss = lax.psum(jnp.sum(lpart), ("x", "y", "c"))

    dx2, da2, db2, s2, h2, df2, dsh3, dsc3, dgt3, dn3 = _ffn_dgrad(
        dx3, x2, f2, a2, b2, mrow[6], mrow[7], mrow[8], norm_ffn2, wg2, wu2, wd2, "ffn2_dgrad")
    gg2, gu2, gd2 = _ffn_wgrad(da2, db2, s2, h2, df2, "ffn2_wgrad")
    dqn, dkn, dvv, dz, dpp, dbg, dgt2, dwo, dpw, dps, dgnm = _mix_core_bwd(
        dx2, mrow[5], mixed, cat, o, sall, qn, kn, vv, z, pp, bg, gc, gnm, pwb, psc, wo)
    dx1, dwin, dcw, dal, ddt, dsh2, dsc2, dn2 = _mix_proj_bwd(
        dx2, dqn, dkn, dvv, dz, dpp, dbg, xq, ba, x1, mrow[3], mrow[4], norm_mix, winT, conv_full, alog, dtb)
    dx0, da1, db1, s1, h1, df1, dsh1, dsc1, dgt1, dn1 = _ffn_dgrad(
        dx1, xs, f1, a1, b1, mrow[0], mrow[1], mrow[2], norm_ffn1, wg1, wu1, wd1, "ffn1_dgrad")
    gg1, gu1, gd1 = _ffn_wgrad(da1, db1, s1, h1, df1, "ffn1_wgrad")

    dwin_nat = jnp.concatenate([dwin[:4 * GW], dwin[4 * GW + PW:4 * GW + PW + 2 * NH], dwin[4 * GW:4 * GW + PW]], axis=0)
    dwin_sl = jnp.pad(dwin_nat.reshape(4, WIN_SH, D), ((0, 0), (0, WIN_PAD - WIN_SH), (0, 0)))
    big = [t.reshape(4, fsh, D) for t in (gg1, gu1, gd1, gg2, gu2, gd2)] + [dwo.reshape(4, D // 4, D), dwin_sl]
    landed = _chip_exchange(big, "grad_scatter", True)
    names = ("g1", "u1", "d1", "g2", "u2", "d2", "wo", "win")
    psum = [_sum_slots(t, "sum_" + nm) for t, nm in zip(landed, names)]
    qsum = _pair_exchange(psum, "grad_pair")

    red = lambda t: jnp.sum(t, axis=0, keepdims=True)
    small = jnp.concatenate(
        [red(dn1), red(dn2), red(dn3), red(dfin),
         red(dsh1), red(dsc1), red(dgt1), red(dsh2), red(dsc2), red(dgt2), red(dsh3), red(dsc3), red(dgt3),
         _pad_row(red(dps)), _pad_row(red(dgnm)), _pad_row(red(dal)), _pad_row(red(ddt)),
         jnp.sum(dcw, axis=1).reshape(6, D), jnp.zeros((1, D), F32)], axis=0)
    small_all, dpw_all = _all_exchange([small, dpw.reshape(NG * 128, 128)], "small_gather")

    def unT(t, cols):
        return t[:cols].T

    upd = {}
    upd["ffn1_gate"] = _adamw([unT(psum[0], fsh), unT(qsum[0], fsh)], ffn1_gate[0], m_ffn1_gate[0], v_ffn1_gate[0], "adamw_g1")
    upd["ffn1_up"] = _adamw([unT(psum[1], fsh), unT(qsum[1], fsh)], ffn1_up[0], m_ffn1_up[0], v_ffn1_up[0], "adamw_u1")
    upd["ffn1_down"] = _adamw([psum[2], qsum[2]], ffn1_down[0], m_ffn1_down[0], v_ffn1_down[0], "adamw_d1")
    upd["ffn2_gate"] = _adamw([unT(psum[3], fsh), unT(qsum[3], fsh)], ffn2_gate[0], m_ffn2_gate[0], v_ffn2_gate[0], "adamw_g2")
    upd["ffn2_up"] = _adamw([unT(psum[4], fsh), unT(qsum[4], fsh)], ffn2_up[0], m_ffn2_up[0], v_ffn2_up[0], "adamw_u2")
    upd["ffn2_down"] = _adamw([psum[5], qsum[5]], ffn2_down[0], m_ffn2_down[0], v_ffn2_down[0], "adamw_d2")
    upd["w_out"] = _adamw([psum[6], qsum[6]], w_out[0], m_w_out[0], v_w_out[0], "adamw_wo")
    upd["w_in"] = _adamw([unT(psum[7], WIN_SH), unT(qsum[7], WIN_SH)], w_in[0], m_w_in[0], v_w_in[0], "adamw_win")
    dmods = lax.dynamic_slice(small_all[:, 4:4 + NMOD, :].reshape(8, NMOD * D), (0, chip * ADA_SH), (8, ADA_SH))
    upd["w_ada"] = _adamw_ada(msgs, dmods, w_ada[0], m_w_ada[0], v_w_ada[0])

    def pack_small(nf1, nmx, nf2, fn, bada, psc_, gn_, al_, dt_):
        return jnp.concatenate(
            [nf1.reshape(1, D), nmx.reshape(1, D), nf2.reshape(1, D), fn.reshape(1, D), bada.reshape(NMOD, D),
             _pad_row(psc_), _pad_row(gn_), _pad_row(_lanes_4_7(al_)), _pad_row(_lanes_4_7(dt_)),
             jnp.zeros((7, D), F32)], axis=0)

    ws = pack_small(norm_ffn1, norm_mix, norm_ffn2, final_norm, b_ada, pool_scale, gdn_norm, a_log, dt_bias)
    ms = pack_small(m_norm_ffn1, m_norm_mix, m_norm_ffn2, m_final_norm, m_b_ada, m_pool_scale, m_gdn_norm, m_a_log, m_dt_bias)
    vs = pack_small(v_norm_ffn1, v_norm_mix, v_norm_ffn2, v_final_norm, v_b_ada, v_pool_scale, v_gdn_norm, v_a_log, v_dt_bias)
    sm = _adamw_small(small_all, ws, ms, vs, "adamw_small")
    pw2 = lambda t: t.reshape(NG * 128, 128)
    upd_pw = _adamw_small(dpw_all, pw2(pool_w), pw2(m_pool_w), pw2(v_pool_w), "adamw_pool_w")
    csh = 3 * GW // 4
    gconv = lax.dynamic_slice(sm[0][17:23].reshape(4, 3 * GW), (0, chip * csh), (4, csh))
    upd["conv_w"] = _adamw([gconv], conv_w[0], m_conv_w[0], v_conv_w[0], "adamw_conv")

    def small_out(k):
        t = sm[k]
        return {
            "norm_ffn1": t[0:1], "norm_mix": t[1:2], "norm_ffn2": t[2:3], "final_norm": t[3],
            "b_ada": t[4:4 + NMOD].reshape(1, NMOD * D), "pool_scale": t[13:14, :PW], "gdn_norm": t[14:15, :DH],
            "a_log": t[15:16, NH:2 * NH], "dt_bias": t[16:17, NH:2 * NH],
        }

    order = ["w_ada", "b_ada", "norm_ffn1", "ffn1_gate", "ffn1_up", "ffn1_down", "norm_mix", "w_in", "conv_w", "a_log",
             "dt_bias", "gdn_norm", "pool_w", "pool_scale", "w_out", "norm_ffn2", "ffn2_gate", "ffn2_up", "ffn2_down",
             "final_norm"]
    outs = [loss, dx0[None]]
    for k in range(4):
        smk = small_out(k)
        for nm in order:
            if nm in upd:
                outs.append(upd[nm][k][None])
            elif nm == "pool_w":
                outs.append(upd_pw[k].reshape(1, NG, 128, 128))
            else:
                outs.append(smk[nm])
    return tuple(outs)
```

```python
import functools

import jax
import jax.numpy as jnp
from jax import lax
from jax.experimental import pallas as pl
from jax.experimental.pallas import tpu as pltpu

F32 = jnp.float32
BF = jnp.bfloat16

D = 1024
FF = 2816
FH = FF // 2
NH = 4
DH = 128
GW = NH * DH
CH = 64
PW = 512
NG = 4
POOL_WINDOWS = (2, 4, 8, 16)
HALO = 16
DIN = 4 * GW + 2 * NH + PW
DINP = 3 * GW + GW + PW + 128
EPS = 1e-6
ADAM_LR, ADAM_B1, ADAM_B2, ADAM_EPS, ADAM_WD, ADAM_STEP = 0.001, 0.9, 0.999, 1e-08, 0.01, 10

VMEM_LIMIT = 56 * 1024 * 1024

NT_DIMS = (((1,), (1,)), ((), ()))
TN_DIMS = (((0,), (0,)), ((), ()))
HI = lax.Precision.HIGHEST


def _nt(a, b, **kw):
    return lax.dot_general(a, b, NT_DIMS, preferred_element_type=F32, **kw)


def _tn(a, b, **kw):
    return lax.dot_general(a, b, TN_DIMS, preferred_element_type=F32, **kw)


def _nn(a, b, **kw):
    return jnp.dot(a, b, preferred_element_type=F32, **kw)


def _cparams(sem=("arbitrary",), **kw):
    return pltpu.CompilerParams(dimension_semantics=sem, vmem_limit_bytes=VMEM_LIMIT, **kw)


def _const_spec(shape):
    nd = len(shape)
    return pl.BlockSpec(shape, lambda *_: (0,) * nd, pipeline_mode=pl.Buffered(1))


def _row_spec(tm, width):
    return pl.BlockSpec((tm, width), lambda i: (i, 0))


def _sum8(v):
    return jnp.sum(v.reshape(v.shape[0] // 8, 8, v.shape[1]), axis=0)


def _sigmoid(v):
    return 1.0 / (1.0 + jnp.exp(-v))


def _tile(T, cap=512):
    return min(cap, T)


def _norm_mod_fwd(xv, gain, shift, scale):
    r = lax.rsqrt(jnp.mean(xv * xv, axis=-1, keepdims=True) + EPS)
    n = xv * r
    y = n * gain
    return n, r, y, y * (1.0 + scale) + shift


def _norm_mod_bwd(dh, n, r, y, gain, scale):
    dy = dh * (1.0 + scale)
    dn = dy * gain
    dx = r * (dn - n * jnp.mean(dn * n, axis=-1, keepdims=True))
    return dx, _sum8(dh), _sum8(dh * y), _sum8(dy * n)


def _ffn_wspecs(k0):
    return [pl.BlockSpec((4, FF // 4, D), lambda i, k=k0 + n: (0, k, 0), pipeline_mode=pl.Buffered(1)) for n in range(3)]


def _half(w_ref, j):
    return w_ref[2 * j:2 * j + 2].reshape(FH, D)


def _ffn_fwd(x, shift, scale, gate, gain, wall, k0, name):
    T = x.shape[0]
    tm = _tile(T)

    def body(x_ref, sh_ref, sc_ref, gt_ref, gn_ref, wg_ref, wu_ref, wd_ref, xo_ref, f_ref, a_ref, b_ref):
        xv = x_ref[...]
        _, _, _, h = _norm_mod_fwd(xv, gn_ref[...], sh_ref[...], sc_ref[...])
        hb = h.astype(BF)
        facc = jnp.zeros((tm, D), F32)
        for j in range(2):
            a = _nt(hb, _half(wg_ref, j))
            b = _nt(hb, _half(wu_ref, j))
            a_ref[:, j * FH:(j + 1) * FH] = a.astype(BF)
            b_ref[:, j * FH:(j + 1) * FH] = b.astype(BF)
            s = (a * _sigmoid(a) * b).astype(BF)
            facc = facc + _nn(s, _half(wd_ref, j))
        f_ref[...] = facc
        xo_ref[...] = xv + 0.5 * gt_ref[...] * facc

    vec = _const_spec((1, D))
    return pl.pallas_call(
        body, grid=(T // tm,), name=name,
        in_specs=[_row_spec(tm, D), vec, vec, vec, vec] + _ffn_wspecs(k0),
        out_specs=[_row_spec(tm, D), _row_spec(tm, D), _row_spec(tm, FF), _row_spec(tm, FF)],
        out_shape=[jax.ShapeDtypeStruct((T, D), F32), jax.ShapeDtypeStruct((T, D), F32),
                   jax.ShapeDtypeStruct((T, FF), BF), jax.ShapeDtypeStruct((T, FF), BF)],
        compiler_params=_cparams(),
    )(x, shift, scale, gate, gain, wall, wall, wall)


def _ffn_dgrad(dxo, x, f, a, b, shift, scale, gate, gain, wall, k0, name):
    T = x.shape[0]
    tm = _tile(T, 256)

    def body(dxo_ref, x_ref, f_ref, a_ref, b_ref, sh_ref, sc_ref, gt_ref, gn_ref, wg_ref, wu_ref, wd_ref,
             dx_ref, da_ref, db_ref, s_ref, h_ref, df_ref, dsh_ref, dsc_ref, dgt_ref, dgn_ref):
        i = pl.program_id(0)
        dxo_v = dxo_ref[...]
        xv = x_ref[...]
        gain_v, scale_v = gn_ref[...], sc_ref[...]
        n, r, y, h = _norm_mod_fwd(xv, gain_v, sh_ref[...], scale_v)
        h_ref[...] = h.astype(BF)
        dgate = _sum8(0.5 * f_ref[...] * dxo_v)
        dfb = (0.5 * gt_ref[...] * dxo_v).astype(BF)
        df_ref[...] = dfb
        dh = jnp.zeros((tm, D), F32)
        for j in range(2):
            cols = slice(j * FH, (j + 1) * FH)
            ds = _nt(dfb, _half(wd_ref, j))
            av = a_ref[:, cols].astype(F32)
            bv = b_ref[:, cols].astype(F32)
            sig = _sigmoid(av)
            sa = av * sig
            da = (ds * bv * (sig * (1.0 + av * (1.0 - sig)))).astype(BF)
            db = (ds * sa).astype(BF)
            da_ref[:, cols] = da
            db_ref[:, cols] = db
            s_ref[:, cols] = (sa * bv).astype(BF)
            dh = dh + _nn(da, _half(wg_ref, j)) + _nn(db, _half(wu_ref, j))
        dxn, dsh, dsc, dgn = _norm_mod_bwd(dh, n, r, y, gain_v, scale_v)
        dx_ref[...] = dxo_v + dxn

        @pl.when(i == 0)
        def _():
            dsh_ref[...] = dsh
            dsc_ref[...] = dsc
            dgt_ref[...] = dgate
            dgn_ref[...] = dgn

        @pl.when(i > 0)
        def _():
            dsh_ref[...] += dsh
            dsc_ref[...] += dsc
            dgt_ref[...] += dgate
            dgn_ref[...] += dgn

    vec = _const_spec((1, D))
    acc = pl.BlockSpec((8, D), lambda i: (0, 0))
    accs = jax.ShapeDtypeStruct((8, D), F32)
    return pl.pallas_call(
        body, grid=(T // tm,), name=name,
        in_specs=[_row_spec(tm, D), _row_spec(tm, D), _row_spec(tm, D), _row_spec(tm, FF), _row_spec(tm, FF),
                  vec, vec, vec, vec] + _ffn_wspecs(k0),
        out_specs=[_row_spec(tm, D), _row_spec(tm, FF), _row_spec(tm, FF), _row_spec(tm, FF),
                   _row_spec(tm, D), _row_spec(tm, D), acc, acc, acc, acc],
        out_shape=[jax.ShapeDtypeStruct((T, D), F32), jax.ShapeDtypeStruct((T, FF), BF),
                   jax.ShapeDtypeStruct((T, FF), BF), jax.ShapeDtypeStruct((T, FF), BF),
                   jax.ShapeDtypeStruct((T, D), BF), jax.ShapeDtypeStruct((T, D), BF), accs, accs, accs, accs],
        compiler_params=_cparams(),
    )(dxo, x, f, a, b, shift, scale, gate, gain, wall, wall, wall)


def _ffn_wgrad(da, db, s, h, df, name):
    T = h.shape[0]
    tk = _tile(T, 256)
    nk = T // tk

    def body(da_ref, db_ref, s_ref, h_ref, df_ref, og_ref, ou_ref, od_ref, ag, au, ad):
        k = pl.program_id(1)
        pg = _tn(da_ref[...], h_ref[...])
        pu = _tn(db_ref[...], h_ref[...])
        pd = _tn(s_ref[...], df_ref[...])

        @pl.when(k == 0)
        def _():
            ag[...] = pg
            au[...] = pu
            ad[...] = pd

        @pl.when(k > 0)
        def _():
            ag[...] += pg
            au[...] += pu
            ad[...] += pd

        @pl.when(k == nk - 1)
        def _():
            og_ref[...] = ag[...].astype(BF)
            ou_ref[...] = au[...].astype(BF)
            od_ref[...] = ad[...].astype(BF)

    colblk = pl.BlockSpec((tk, FH), lambda j, k: (k, j))
    rowblk = pl.BlockSpec((tk, D), lambda j, k: (k, 0))
    outblk = pl.BlockSpec((FH, D), lambda j, k: (j, 0))
    outs = jax.ShapeDtypeStruct((FF, D), BF)
    return pl.pallas_call(
        body, grid=(2, nk), name=name,
        in_specs=[colblk, colblk, colblk, rowblk, rowblk],
        out_specs=[outblk, outblk, outblk],
        out_shape=[outs, outs, outs],
        scratch_shapes=[pltpu.VMEM((FH, D), F32)] * 3,
        compiler_params=_cparams(("arbitrary", "arbitrary")),
    )(da, db, s, h, df)


def _loss_head(x, target, gain):
    T = x.shape[0]
    tm = _tile(T)

    def body(x_ref, t_ref, gn_ref, ls_ref, dx_ref, dgn_ref):
        i = pl.program_id(0)
        xv = x_ref[...]
        gain_v = gn_ref[...]
        r = lax.rsqrt(jnp.mean(xv * xv, axis=-1, keepdims=True) + EPS)
        n = xv * r
        err = n * gain_v - t_ref[...]
        e2 = err * err
        part = e2[:, 0:128]
        for q in range(1, D // 128):
            part = part + e2[:, q * 128:(q + 1) * 128]
        lsum = _sum8(part) * (0.5 / D)
        dy = err * (1.0 / D)
        dn = dy * gain_v
        dx_ref[...] = r * (dn - n * jnp.mean(dn * n, axis=-1, keepdims=True))
        dgn = _sum8(dy * n)

        @pl.when(i == 0)
        def _():
            ls_ref[...] = lsum
            dgn_ref[...] = dgn

        @pl.when(i > 0)
        def _():
            ls_ref[...] += lsum
            dgn_ref[...] += dgn

    return pl.pallas_call(
        body, grid=(T // tm,), name="loss_head",
        in_specs=[_row_spec(tm, D), _row_spec(tm, D), _const_spec((1, D))],
        out_specs=[pl.BlockSpec((8, 128), lambda i: (0, 0)), _row_spec(tm, D), pl.BlockSpec((8, D), lambda i: (0, 0))],
        out_shape=[jax.ShapeDtypeStruct((8, 128), F32), jax.ShapeDtypeStruct((T, D), F32),
                   jax.ShapeDtypeStruct((8, D), F32)],
        compiler_params=_cparams(),
    )(x, target, gain)


def _seg_cumsum(v, row_in_chunk, reverse=False):
    n = v.shape[0]
    s = 1
    while s < CH:
        if reverse:
            moved = pltpu.roll(v, n - s, 0)
            ok = row_in_chunk < CH - s
        else:
            moved = pltpu.roll(v, s, 0)
            ok = row_in_chunk >= s
        v = v + jnp.where(ok, moved, 0.0)
        s *= 2
    return v


def _conv_silu(xq_ext_ref, cw_ref, tm):
    c = cw_ref[0:1, :] * xq_ext_ref[pl.ds(5, tm), :]
    for j in range(1, 4):
        c = c + cw_ref[j:j + 1, :] * xq_ext_ref[pl.ds(5 + j, tm), :]
    return c, _sigmoid(c)


def _gates(ba, alog, dtb, lane):
    beta = _sigmoid(ba)
    arg = ba + dtb
    softplus = jnp.maximum(arg, 0.0) + jnp.log(1.0 + jnp.exp(-jnp.abs(arg)))
    g = -jnp.exp(alog) * softplus
    return jnp.where(lane < NH, beta, 0.0), jnp.where((lane >= NH) & (lane < 2 * NH), g, 0.0), _sigmoid(arg)


def _mix_proj(x, shift, scale, gain, winT, conv_w, alog, dtb):
    T = x.shape[0]
    tm = _tile(T)

    def body(x_ref, sh_ref, sc_ref, gn_ref, w_ref, cw_ref, al_ref, dt_ref,
             xq_ref, ba_ref, qn_ref, kn_ref, v_ref, z_ref, p_ref, bg_ref, gc_ref, ext):
        i = pl.program_id(0)
        _, _, _, h = _norm_mod_fwd(x_ref[...], gn_ref[...], sh_ref[...], sc_ref[...])
        hb = h.astype(BF)

        @pl.when(i == 0)
        def _():
            ext[pl.ds(0, 8), :] = jnp.zeros((8, 3 * GW), F32)

        xq = _nt(hb, w_ref[pl.ds(0, 3 * GW), :])
        xq_ref[...] = xq
        ext[pl.ds(8, tm), :] = xq
        z_ref[...] = _nt(hb, w_ref[pl.ds(3 * GW, GW), :])
        p_ref[...] = _nt(hb, w_ref[pl.ds(4 * GW, PW), :])
        ba = _nt(hb, w_ref[pl.ds(4 * GW + PW, 128), :])
        ba_ref[...] = ba

        c, sg = _conv_silu(ext, cw_ref, tm)
        ext[pl.ds(0, 8), :] = ext[pl.ds(tm, 8), :]
        qt = c * sg
        for hd in range(NH):
            cq = slice(hd * DH, (hd + 1) * DH)
            ck = slice(GW + hd * DH, GW + (hd + 1) * DH)
            qh, kh = qt[:, cq], qt[:, ck]
            qn_ref[:, cq] = qh * (lax.rsqrt(jnp.sum(qh * qh, axis=-1, keepdims=True) + EPS) * DH ** -0.5)
            kn_ref[:, cq] = kh * lax.rsqrt(jnp.sum(kh * kh, axis=-1, keepdims=True) + EPS)
        v_ref[...] = qt[:, 2 * GW:3 * GW]

        lane = lax.broadcasted_iota(jnp.int32, (tm, 128), 1)
        row = lax.broadcasted_iota(jnp.int32, (tm, 128), 0) % CH
        beta, g, _ = _gates(ba, al_ref[...], dt_ref[...], lane)
        bg_ref[...] = beta + g
        gc_ref[...] = _seg_cumsum(g, row)

    wide = lambda w: _row_spec(tm, w)
    shp = lambda w: jax.ShapeDtypeStruct((T, w), F32)
    return pl.pallas_call(
        body, grid=(T // tm,), name="mix_proj",
        in_specs=[wide(D), _const_spec((1, D)), _const_spec((1, D)), _const_spec((1, D)), _const_spec((DINP, D)),
                  _const_spec((4, 3 * GW)), _const_spec((1, 128)), _const_spec((1, 128))],
        out_specs=[wide(3 * GW), wide(128), wide(GW), wide(GW), wide(GW), wide(GW), wide(PW), wide(128), wide(128)],
        out_shape=[shp(3 * GW), shp(128), shp(GW), shp(GW), shp(GW), shp(GW), shp(PW), shp(128), shp(128)],
        scratch_shapes=[pltpu.VMEM((tm + 8, 3 * GW), F32)],
        compiler_params=_cparams(),
    )(x, shift, scale, gain, winT, conv_w, alog, dtb)


R2 = 2 * CH
TRI_PREC = None


def _tmm(fn, a, b):
    if TRI_PREC is None:
        return fn(a.astype(BF), b.astype(BF))
    return fn(a, b, precision=TRI_PREC)


def _pair_consts():
    ii = lax.broadcasted_iota(jnp.int32, (R2, R2), 0)
    jj = lax.broadcasted_iota(jnp.int32, (R2, R2), 1)
    same = (ii < CH) == (jj < CH)
    r = lax.broadcasted_iota(jnp.int32, (R2, 1), 0)
    return dict(causal=same & (ii >= jj), strict=same & (ii > jj), eye=(ii == jj).astype(F32), rowA=r < CH,
                last=(r == CH - 1) | (r == R2 - 1),
                rowS=lax.broadcasted_iota(jnp.int32, (2 * DH, 1), 0) < DH)


def _tri_inverse(m, eye):
    neg = -m
    t = eye + neg
    pw = neg
    for _ in range(5):
        pw = _tmm(_nn, pw, pw)
        t = _tmm(_nn, t, eye + pw)
    return t


def _pair_forward(q, k, v, beta, gcv, gl, S, cn):
    causal, rowA = cn["causal"], cn["rowA"]
    gc_b = jnp.broadcast_to(gcv, (R2, R2))
    gam = jnp.where(causal, jnp.exp(jnp.where(causal, gc_b - gc_b.T, 0.0)), 0.0)
    kb = k * beta
    vb = v * beta
    kbf = k.astype(BF)
    P = _nt(kb.astype(BF), kbf)
    QK = _nt(q.astype(BF), kbf)
    M = jnp.where(cn["strict"], P * gam, 0.0)
    Tm = _tri_inverse(M, cn["eye"])
    E = jnp.exp(gcv)
    Fd = jnp.exp(gl - gcv)
    egl = jnp.exp(gl)
    egl_st = jnp.concatenate([jnp.broadcast_to(egl[0:1], (DH, 1)), jnp.broadcast_to(egl[CH:CH + 1], (DH, 1))], axis=0)
    kbE = kb * E
    u = _tmm(_nn, Tm, vb)
    w = _tmm(_nn, Tm, kbE)
    Sb = S.astype(BF)

    def bd(t):
        return jnp.concatenate([jnp.where(rowA, t, 0.0), jnp.where(rowA, 0.0, t)], axis=1).astype(BF)

    w_bd = bd(w)
    vn = u - _nn(w_bd, Sb)
    Q = QK * gam
    qE_bd = bd(q * E)
    kF_bd = bd(k * Fd)
    vnb = vn.astype(BF)
    o = _nn(qE_bd, Sb) + _nn(Q.astype(BF), vnb)
    S_new = S * egl_st + _tn(kF_bd, vnb)
    return dict(gam=gam, kb=kb, vb=vb, P=P, QK=QK, Tm=Tm, E=E, Fd=Fd, egl=egl, egl_st=egl_st, kbE=kbE, w_bd=w_bd,
                vnb=vnb, Q=Q, qE_bd=qE_bd, kF_bd=kF_bd, o=o, S_new=S_new)


def _stack_heads(ref, rows, pair):
    return jnp.concatenate([ref[rows, (2 * pair) * DH:(2 * pair + 1) * DH],
                            ref[rows, (2 * pair + 1) * DH:(2 * pair + 2) * DH]], axis=0)


def _stack_cols(val, lane_a, lane_b, bcast_rows=None):
    a, b = val[:, lane_a:lane_a + 1], val[:, lane_b:lane_b + 1]
    if bcast_rows:
        a, b = jnp.broadcast_to(a, (bcast_rows, 1)), jnp.broadcast_to(b, (bcast_rows, 1))
    return jnp.concatenate([a, b], axis=0)


def _pool_windows(ext, tm, reverse):
    n = tm + HALO
    outs = []
    for gi in range(NG):
        a = ext[:, gi * 128:(gi + 1) * 128]
        s = 1
        while s < POOL_WINDOWS[gi]:
            a = a + pltpu.roll(a, (n - s) if reverse else s, 0)
            s *= 2
        outs.append(a[0:tm] if reverse else a[HALO:HALO + tm])
    return jnp.concatenate(outs, axis=1)


def _pool_count(tm, tile_index):
    t1 = (lax.broadcasted_iota(jnp.int32, (tm, PW), 0) + tile_index * tm + 1).astype(F32)
    win = jnp.concatenate([jnp.full((tm, 128), float(w), F32) for w in POOL_WINDOWS], axis=1)
    return 1.0 / jnp.minimum(t1, win)


def _mix_core(x, gate, qn, kn, v, z, p, bg, gc, gnorm, pool_w, pool_scale, w_out):
    T = x.shape[0]
    tm = _tile(T)
    nc = tm // CH

    def body(x_ref, gt_ref, qn_ref, kn_ref, v_ref, z_ref, p_ref, bg_ref, gc_ref, gnm_ref, pw_ref, ps_ref, wo_ref,
             xo_ref, mx_ref, cat_ref, o_ref, sall_ref, S_scr, pext):
        i = pl.program_id(0)

        @pl.when(i == 0)
        def _():
            S_scr[...] = jnp.zeros((NH * DH, DH), F32)
            pext[pl.ds(0, HALO), :] = jnp.zeros((HALO, PW), F32)

        cn = _pair_consts()

        def chunk(c, carry):
            r0 = pl.multiple_of(c * CH, CH)
            rows = pl.ds(r0, CH)
            bgv = bg_ref[rows, :]
            gcv_all = gc_ref[rows, :]
            gl_all = gc_ref[pl.ds(r0 + CH - 1, 1), :]
            for pr in range(NH // 2):
                ha, hb = 2 * pr, 2 * pr + 1
                S = S_scr[pl.ds(pr * 2 * DH, 2 * DH), :]
                sall_ref[c, ha:hb + 1] = S.reshape(2, DH, DH)
                res = _pair_forward(_stack_heads(qn_ref, rows, pr), _stack_heads(kn_ref, rows, pr),
                                    _stack_heads(v_ref, rows, pr), _stack_cols(bgv, ha, hb),
                                    _stack_cols(gcv_all, NH + ha, NH + hb), _stack_cols(gl_all, NH + ha, NH + hb, CH),
                                    S, cn)
                o_ref[rows, ha * DH:(ha + 1) * DH] = res["o"][0:CH]
                o_ref[rows, hb * DH:(hb + 1) * DH] = res["o"][CH:R2]
                S_scr[pl.ds(pr * 2 * DH, 2 * DH), :] = res["S_new"]
            return carry

        lax.fori_loop(0, nc, chunk, 0)

        for hd in range(NH):
            cols = slice(hd * DH, (hd + 1) * DH)
            oh = o_ref[:, cols]
            zh = z_ref[:, cols]
            r = lax.rsqrt(jnp.mean(oh * oh, axis=-1, keepdims=True) + EPS)
            cat_ref[:, cols] = (oh * r * gnm_ref[...] * (zh * _sigmoid(zh))).astype(BF)

        pv = p_ref[...]
        pext[pl.ds(HALO, tm), :] = pv
        pooled = _pool_windows(pext[...], tm, False) * _pool_count(tm, i) - pv
        pext[pl.ds(0, HALO), :] = pext[pl.ds(tm, HALO), :]
        for gi in range(NG):
            cols = slice(gi * 128, (gi + 1) * 128)
            pm = _nn(pooled[:, cols].astype(BF), pw_ref[gi])
            cat_ref[:, GW + gi * 128:GW + (gi + 1) * 128] = (pm * ps_ref[:, cols]).astype(BF)

        mixed = _nn(cat_ref[...], wo_ref[...])
        mx_ref[...] = mixed
        xo_ref[...] = x_ref[...] + gt_ref[...] * mixed

    wide = lambda w: _row_spec(tm, w)
    return pl.pallas_call(
        body, grid=(T // tm,), name="mix_core",
        in_specs=[wide(D), _const_spec((1, D)), wide(GW), wide(GW), wide(GW), wide(GW), wide(PW), wide(128), wide(128),
                  _const_spec((1, DH)), _const_spec((NG, 128, 128)), _const_spec((1, PW)), _const_spec((D, D))],
        out_specs=[wide(D), wide(D), wide(D), wide(GW), pl.BlockSpec((nc, NH, DH, DH), lambda i: (i, 0, 0, 0))],
        out_shape=[jax.ShapeDtypeStruct((T, D), F32), jax.ShapeDtypeStruct((T, D), F32),
                   jax.ShapeDtypeStruct((T, D), BF), jax.ShapeDtypeStruct((T, GW), F32),
                   jax.ShapeDtypeStruct((T // CH, NH, DH, DH), F32)],
        scratch_shapes=[pltpu.VMEM((NH * DH, DH), F32), pltpu.VMEM((tm + HALO, PW), F32)],
        compiler_params=_cparams(),
    )(x, gate, qn, kn, v, z, p, bg, gc, gnorm, pool_w, pool_scale, w_out)


def _pair_backward(fw, q, k, v, beta, S, do, dSn, cn):
    bf = lambda t: t.astype(BF)
    rowA, rowS = cn["rowA"], cn["rowS"]
    sel = lambda t: jnp.where(rowA, t[:, 0:DH], t[:, DH:2 * DH])
    Sb, dSb, dob, vnb = bf(S), bf(dSn), bf(do), fw["vnb"]
    dvn = _tn(bf(fw["Q"]), dob) + _nn(fw["kF_bd"], dSb)
    dQ = _nt(dob, vnb)
    dqE = sel(_nt(dob, Sb))
    dkF = sel(_nt(vnb, dSb))
    dvnb = bf(dvn)
    dw = -sel(_nt(dvnb, Sb))
    dS_new = _tn(fw["qE_bd"], dob) + fw["egl_st"] * dSn - _tn(fw["w_bd"], dvnb)
    prod = jnp.sum(dSn * S, axis=1, keepdims=True)
    d_egl_a = jnp.sum(jnp.where(rowS, prod, 0.0), axis=0, keepdims=True)
    d_egl_b = jnp.sum(jnp.where(rowS, 0.0, prod), axis=0, keepdims=True)
    Tm = fw["Tm"]
    dvb = _tmm(_tn, Tm, dvn)
    dkbE = _tmm(_tn, Tm, dw)
    dTm = _tmm(_nt, dvn, fw["vb"]) + _tmm(_nt, dw, fw["kbE"])
    dA = -_tmm(_tn, Tm, _tmm(_nt, dTm, Tm))
    gam = fw["gam"]
    N = jnp.where(cn["strict"], dA * gam, 0.0)
    Rm = dQ * gam
    Wm = Rm * fw["QK"] + N * fw["P"]
    dgc = jnp.sum(Wm, axis=1, keepdims=True) - jnp.sum(Wm.T, axis=1, keepdims=True)
    kbf, Nb, Rb = bf(k), bf(N), bf(Rm)
    E, Fd, egl = fw["E"], fw["Fd"], fw["egl"]
    dq = dqE * E + _nn(Rb, kbf)
    dkb = dkbE * E + _nn(Nb, kbf)
    dk = dkF * Fd + _tn(Rb, bf(q)) + _tn(Nb, bf(fw["kb"])) + beta * dkb
    dbeta = jnp.sum(dkb * k, axis=1, keepdims=True) + jnp.sum(dvb * v, axis=1, keepdims=True)
    dv = beta * dvb
    dE = jnp.sum(dqE * q, axis=1, keepdims=True) + jnp.sum(dkbE * fw["kb"], axis=1, keepdims=True)
    fdf = jnp.sum(dkF * k, axis=1, keepdims=True) * Fd
    dgl_a = d_egl_a * egl[0:1] + jnp.sum(jnp.where(rowA, fdf, 0.0), axis=0, keepdims=True)
    dgl_b = d_egl_b * egl[CH:CH + 1] + jnp.sum(jnp.where(rowA, 0.0, fdf), axis=0, keepdims=True)
    dgc = dgc + dE * E - fdf + jnp.where(cn["last"], jnp.where(rowA, dgl_a, dgl_b), 0.0)
    return dq, dk, dv, dbeta, dgc, dS_new


def _mix_core_bwd(dxo, gate, mixed, cat, o, sall, qn, kn, v, z, p, bg, gc, gnorm, pool_w, pool_scale, w_out):
    T = dxo.shape[0]
    tm = _tile(T, 256)
    nt = T // tm
    nc = tm // CH

    def body(dx_ref, gt_ref, mx_ref, cat_ref, o_ref, sall_ref, qn_ref, kn_ref, v_ref, z_ref, p_ref, ph_ref, bg_ref,
             gc_ref, gnm_ref, pw_ref, ps_ref, wo_ref,
             dq_ref, dk_ref, dv_ref, dz_ref, dp_ref, dbg_ref, dgt_ref, dwo_ref, dpw_ref, dps_ref, dgn_ref,
             dS_scr, pext, yext, do_buf, dwo_acc):
        i = pl.program_id(0)
        ti = nt - 1 - i

        @pl.when(i == 0)
        def _():
            dS_scr[...] = jnp.zeros((NH * DH, DH), F32)
            yext[pl.ds(tm, HALO), :] = jnp.zeros((HALO, PW), F32)
            dwo_acc[...] = jnp.zeros((D, D), F32)
            dgt_ref[...] = jnp.zeros((8, D), F32)
            dpw_ref[...] = jnp.zeros((NG, 128, 128), F32)
            dps_ref[...] = jnp.zeros((8, PW), F32)
            dgn_ref[...] = jnp.zeros((8, DH), F32)

        dx2 = dx_ref[...]
        dgt_ref[...] += _sum8(mx_ref[...] * dx2)
        dmix = (gt_ref[...] * dx2).astype(BF)
        dcat = _nt(dmix, wo_ref[...])
        dwo_acc[...] += _tn(cat_ref[...], dmix)

        pv = p_ref[...]
        pext[pl.ds(0, HALO), :] = jnp.where(ti == 0, 0.0, ph_ref[...])
        pext[pl.ds(HALO, tm), :] = pv
        inv_cnt = _pool_count(tm, ti)
        pooled = _pool_windows(pext[...], tm, False) * inv_cnt - pv
        dpooled = []
        for gi in range(NG):
            cols = slice(gi * 128, (gi + 1) * 128)
            pgb = pooled[:, cols].astype(BF)
            pm = _nn(pgb, pw_ref[gi])
            dpo = dcat[:, GW + gi * 128:GW + (gi + 1) * 128]
            dps_ref[:, cols] += _sum8(dpo * pm)
            dpm = (dpo * ps_ref[:, cols]).astype(BF)
            dpooled.append(_nt(dpm, pw_ref[gi]))
            dpw_ref[gi] += _tn(pgb, dpm)
        dpooled = jnp.concatenate(dpooled, axis=1)
        y = dpooled * inv_cnt
        yext[pl.ds(0, tm), :] = y
        dp_ref[...] = _pool_windows(yext[...], tm, True) - dpooled
        yext[pl.ds(tm, HALO), :] = y[0:HALO]

        gnm = gnm_ref[...]
        dgn = jnp.zeros((8, DH), F32)
        for hd in range(NH):
            cols = slice(hd * DH, (hd + 1) * DH)
            oh = o_ref[:, cols]
            zh = z_ref[:, cols]
            r = lax.rsqrt(jnp.mean(oh * oh, axis=-1, keepdims=True) + EPS)
            n = oh * r
            sg = _sigmoid(zh)
            zs = zh * sg
            dgo = dcat[:, cols]
            dgn = dgn + _sum8(dgo * zs * n)
            dn = dgo * zs * gnm
            do_buf[:, cols] = r * (dn - n * jnp.mean(dn * n, axis=-1, keepdims=True))
            dz_ref[:, cols] = dgo * n * gnm * (sg * (1.0 + zh * (1.0 - sg)))
        dgn_ref[...] += dgn

        cn = _pair_consts()
        lane_c = lax.broadcasted_iota(jnp.int32, (CH, 128), 1)

        def chunk(cc, carry):
            c = nc - 1 - cc
            r0 = pl.multiple_of(c * CH, CH)
            rows = pl.ds(r0, CH)
            bgv = bg_ref[rows, :]
            gcv_all = gc_ref[rows, :]
            gl_all = gc_ref[pl.ds(r0 + CH - 1, 1), :]
            dbg = jnp.zeros((CH, 128), F32)
            for pr in range(NH // 2):
                ha, hb = 2 * pr, 2 * pr + 1
                srows = pl.ds(pr * 2 * DH, 2 * DH)
                S = sall_ref[c, ha:hb + 1].reshape(2 * DH, DH)
                q, k, vv = _stack_heads(qn_ref, rows, pr), _stack_heads(kn_ref, rows, pr), _stack_heads(v_ref, rows, pr)
                beta = _stack_cols(bgv, ha, hb)
                fw = _pair_forward(q, k, vv, beta, _stack_cols(gcv_all, NH + ha, NH + hb),
                                   _stack_cols(gl_all, NH + ha, NH + hb, CH), S, cn)
                dq, dk, dv, dbeta, dgc, dS_new = _pair_backward(fw, q, k, vv, beta, S, _stack_heads(do_buf, rows, pr),
                                                                dS_scr[srows, :], cn)
                for hd, half in ((ha, slice(0, CH)), (hb, slice(CH, R2))):
                    cols = slice(hd * DH, (hd + 1) * DH)
                    dq_ref[rows, cols] = dq[half]
                    dk_ref[rows, cols] = dk[half]
                    dv_ref[rows, cols] = dv[half]
                    dbg = dbg + jnp.where(lane_c == hd, dbeta[half], 0.0) + jnp.where(lane_c == NH + hd, dgc[half], 0.0)
                dS_scr[srows, :] = dS_new
            dbg_ref[rows, :] = dbg
            return carry

        lax.fori_loop(0, nc, chunk, 0)

        lane = lax.broadcasted_iota(jnp.int32, (tm, 128), 1)
        row = lax.broadcasted_iota(jnp.int32, (tm, 128), 0) % CH
        dbg_all = dbg_ref[...]
        dg = _seg_cumsum(jnp.where(lane >= NH, dbg_all, 0.0), row, reverse=True)
        dbg_ref[...] = jnp.where(lane < NH, dbg_all, dg)

        @pl.when(i == nt - 1)
        def _():
            dwo_ref[...] = dwo_acc[...].astype(BF)

    rev = lambda w: pl.BlockSpec((tm, w), lambda i: (nt - 1 - i, 0))
    halo = pl.BlockSpec((HALO, PW), lambda i: (jnp.maximum((nt - 1 - i) * (tm // HALO) - 1, 0), 0))
    shp = lambda w: jax.ShapeDtypeStruct((T, w), F32)
    fix = lambda *s: pl.BlockSpec(s, lambda i: (0,) * len(s))
    return pl.pallas_call(
        body, grid=(nt,), name="mix_core_bwd",
        in_specs=[rev(D), _const_spec((1, D)), rev(D), rev(D), rev(GW),
                  pl.BlockSpec((nc, NH, DH, DH), lambda i: (nt - 1 - i, 0, 0, 0)),
                  rev(GW), rev(GW), rev(GW), rev(GW), rev(PW), halo, rev(128), rev(128),
                  _const_spec((1, DH)), _const_spec((NG, 128, 128)), _const_spec((1, PW)), _const_spec((D, D))],
        out_specs=[rev(GW), rev(GW), rev(GW), rev(GW), rev(PW), rev(128),
                   fix(8, D), fix(D, D), fix(NG, 128, 128), fix(8, PW), fix(8, DH)],
        out_shape=[shp(GW), shp(GW), shp(GW), shp(GW), shp(PW), shp(128),
                   jax.ShapeDtypeStruct((8, D), F32), jax.ShapeDtypeStruct((D, D), BF),
                   jax.ShapeDtypeStruct((NG, 128, 128), F32), jax.ShapeDtypeStruct((8, PW), F32),
                   jax.ShapeDtypeStruct((8, DH), F32)],
        scratch_shapes=[pltpu.VMEM((NH * DH, DH), F32), pltpu.VMEM((tm + HALO, PW), F32),
                        pltpu.VMEM((tm + HALO, PW), F32), pltpu.VMEM((tm, GW), F32), pltpu.VMEM((D, D), F32)],
        compiler_params=_cparams(),
    )(dxo, gate, mixed, cat, o, sall, qn, kn, v, z, p, p, bg, gc, gnorm, pool_w, pool_scale, w_out)


def _mix_proj_bwd(dxo, dqn, dkn, dv, dz, dp, dbg, xq, ba, x, shift, scale, gain, winT, conv_w, alog, dtb):
    T = x.shape[0]
    tm = min(256, T)
    nt = T // tm
    W3 = 3 * GW

    def body(dxo_ref, dqn_ref, dkn_ref, dv_ref, dz_ref, dp_ref, dbg_ref, xq_ref, xh_ref, ba_ref, x_ref, sh_ref, sc_ref,
             gn_ref, w_ref, cw_ref, al_ref, dt_ref,
             dx_ref, dw_ref, dcw_ref, dal_ref, ddt_ref, dsh_ref, dsc_ref, dgn_ref,
             ext, dcext, dproj, dw_acc):
        i = pl.program_id(0)
        ti = nt - 1 - i

        @pl.when(i == 0)
        def _():
            dcext[pl.ds(tm, 8), :] = jnp.zeros((8, W3), F32)
            dw_acc[...] = jnp.zeros((DINP, D), F32)
            dcw_ref[...] = jnp.zeros((4, 8, W3), F32)
            dal_ref[...] = jnp.zeros((8, 128), F32)
            ddt_ref[...] = jnp.zeros((8, 128), F32)
            dsh_ref[...] = jnp.zeros((8, D), F32)
            dsc_ref[...] = jnp.zeros((8, D), F32)
            dgn_ref[...] = jnp.zeros((8, D), F32)

        ext[pl.ds(0, 8), :] = jnp.where(ti == 0, 0.0, xh_ref[...])
        ext[pl.ds(8, tm), :] = xq_ref[...]
        c, sg = _conv_silu(ext, cw_ref, tm)
        qt = c * sg
        dsilu = sg * (1.0 + c * (1.0 - sg))
        for hd in range(NH):
            for part, dref, mult in ((0, dqn_ref, DH ** -0.5), (1, dkn_ref, 1.0)):
                cols = slice(part * GW + hd * DH, part * GW + (hd + 1) * DH)
                xh = qt[:, cols]
                rr = lax.rsqrt(jnp.sum(xh * xh, axis=-1, keepdims=True) + EPS)
                unit = xh * rr
                du = dref[:, hd * DH:(hd + 1) * DH] * mult
                dxh = rr * (du - unit * jnp.sum(du * unit, axis=-1, keepdims=True))
                dcext[pl.ds(0, tm), cols] = dxh * dsilu[:, cols]
        dcext[pl.ds(0, tm), 2 * GW:W3] = dv_ref[...] * dsilu[:, 2 * GW:W3]
        dc = dcext[pl.ds(0, tm), :]
        dxq = jnp.zeros((tm, W3), F32)
        for j in range(4):
            dcw_ref[j] += _sum8(dc * ext[pl.ds(5 + j, tm), :])
            dxq = dxq + cw_ref[j:j + 1, :] * dcext[pl.ds(3 - j, tm), :]
        dcext[pl.ds(tm, 8), :] = dc[0:8]
        lane = lax.broadcasted_iota(jnp.int32, (tm, 128), 1)
        bav = ba_ref[...]
        beta, g, sarg = _gates(bav, al_ref[...], dt_ref[...], lane)
        dbg_v = dbg_ref[...]
        is_g = (lane >= NH) & (lane < 2 * NH)
        dbraw = jnp.where(lane < NH, dbg_v * beta * (1.0 - beta), 0.0)
        daraw = jnp.where(is_g, dbg_v * (-jnp.exp(al_ref[...])) * sarg, 0.0)
        dal_ref[...] += _sum8(jnp.where(is_g, dbg_v * g, 0.0))
        ddt_ref[...] += _sum8(daraw)
        dproj[:, 0:W3] = dxq.astype(BF)
        dproj[:, W3:W3 + GW] = dz_ref[...].astype(BF)
        dproj[:, W3 + GW:W3 + GW + PW] = dp_ref[...].astype(BF)
        dproj[:, W3 + GW + PW:DINP] = (dbraw + daraw).astype(BF)
        gain_v, scale_v = gn_ref[...], sc_ref[...]
        n, r, y, h = _norm_mod_fwd(x_ref[...], gain_v, sh_ref[...], scale_v)
        dpj = dproj[...]
        dh = _nn(dpj, w_ref[...])
        dw_acc[...] += _tn(dpj, h.astype(BF))
        dxn, dsh, dsc, dgn = _norm_mod_bwd(dh, n, r, y, gain_v, scale_v)
        dx_ref[...] = dxo_ref[...] + dxn
        dsh_ref[...] += dsh
        dsc_ref[...] += dsc
        dgn_ref[...] += dgn

        @pl.when(i == nt - 1)
        def _():
            dw_ref[...] = dw_acc[...].astype(BF)

    rev = lambda w: pl.BlockSpec((tm, w), lambda i: (nt - 1 - i, 0))
    halo = pl.BlockSpec((8, W3), lambda i: (jnp.maximum((nt - 1 - i) * (tm // 8) - 1, 0), 0))
    fix = lambda *s: pl.BlockSpec(s, lambda i: (0,) * len(s))
    vec = _const_spec((1, D))
    return pl.pallas_call(
        body, grid=(nt,), name="mix_proj_bwd",
        in_specs=[rev(D), rev(GW), rev(GW), rev(GW), rev(GW), rev(PW), rev(128), rev(W3), halo, rev(128), rev(D),
                  vec, vec, vec, _const_spec((DINP, D)), _const_spec((4, W3)), _const_spec((1, 128)),
                  _const_spec((1, 128))],
        out_specs=[rev(D), fix(DINP, D), fix(4, 8, W3), fix(8, 128), fix(8, 128), fix(8, D), fix(8, D), fix(8, D)],
        out_shape=[jax.ShapeDtypeStruct((T, D), F32), jax.ShapeDtypeStruct((DINP, D), BF),
                   jax.ShapeDtypeStruct((4, 8, W3), F32), jax.ShapeDtypeStruct((8, 128), F32),
                   jax.ShapeDtypeStruct((8, 128), F32), jax.ShapeDtypeStruct((8, D), F32),
                   jax.ShapeDtypeStruct((8, D), F32), jax.ShapeDtypeStruct((8, D), F32)],
        scratch_shapes=[pltpu.VMEM((tm + 8, W3), F32), pltpu.VMEM((tm + 8, W3), F32), pltpu.VMEM((tm, DINP), BF),
                        pltpu.VMEM((DINP, D), F32)],
        compiler_params=_cparams(),
    )(dxo, dqn, dkn, dv, dz, dp, dbg, xq, xq, ba, x, shift, scale, gain, winT, conv_w, alog, dtb)


MESH = pl.DeviceIdType.MESH
CHIP_RELS = ((1, 0), (0, 1), (1, 1))
DEV_RELS = tuple((dx, dy, dc) for dx in (0, 1) for dy in (0, 1) for dc in (0, 1) if (dx, dy, dc) != (0, 0, 0))
NMOD = 9
ADA_SH = NMOD * D // 4
WIN_SH = DIN // 4
WIN_PAD = 656
MSG_ROWS = 16
ANY = pl.BlockSpec(memory_space=pl.ANY)
VM = pl.BlockSpec(memory_space=pltpu.VMEM)


def _place():
    x, y, c = lax.axis_index("x"), lax.axis_index("y"), lax.axis_index("c")
    return x, y, c


def _ada_exchange(msg, w_ada, b_ada):
    def body(msg_ref, w_ref, b_ref, all_ref, mod_ref, modp, send1, recv1, send2, recv2, lsem):
        x, y, c = _place()
        me = 4 * x + 2 * y + c
        own = pltpu.make_async_copy(msg_ref, all_ref.at[me], lsem.at[0])
        own.start()

        def gather(k, rel, slot):
            dx, dy, dc = rel
            return pltpu.make_async_remote_copy(
                src_ref=msg_ref, dst_ref=all_ref.at[slot], send_sem=send1.at[k], recv_sem=recv1.at[k],
                device_id=(x ^ dx, y ^ dy, c ^ dc), device_id_type=MESH)

        for k, rel in enumerate(DEV_RELS):
            gather(k, rel, me).start()
        for k, (dx, dy, dc) in enumerate(DEV_RELS):
            gather(k, (dx, dy, dc), 4 * (x ^ dx) + 2 * (y ^ dy) + (c ^ dc)).wait_recv()
        for k, rel in enumerate(DEV_RELS):
            gather(k, rel, me).wait_send()
        own.wait()

        for d in range(8):
            cv = all_ref[d, 0:8, :]
            act = cv * _sigmoid(cv)
            modp[d] = _nn(act, w_ref[...], precision=HI) + b_ref[...]

        myj = 2 * x + y
        keep = pltpu.make_async_copy(modp.at[me], mod_ref.at[myj], lsem.at[1])
        keep.start()

        def scatter(k, rel):
            dx, dy = rel
            return pltpu.make_async_remote_copy(
                src_ref=modp.at[4 * (x ^ dx) + 2 * (y ^ dy) + c], dst_ref=mod_ref.at[myj],
                send_sem=send2.at[k], recv_sem=recv2.at[k], device_id=(x ^ dx, y ^ dy, c), device_id_type=MESH)

        def landed(k, rel):
            dx, dy = rel
            return pltpu.make_async_remote_copy(
                src_ref=modp.at[me], dst_ref=mod_ref.at[2 * (x ^ dx) + (y ^ dy)],
                send_sem=send2.at[k], recv_sem=recv2.at[k], device_id=(x ^ dx, y ^ dy, c), device_id_type=MESH)

        for k, rel in enumerate(CHIP_RELS):
            scatter(k, rel).start()
        for k, rel in enumerate(CHIP_RELS):
            landed(k, rel).wait_recv()
        for k, rel in enumerate(CHIP_RELS):
            scatter(k, rel).wait_send()
        keep.wait()

    return pl.pallas_call(
        body, name="ada_exchange", in_specs=[VM, VM, VM], out_specs=[VM, VM],
        out_shape=[jax.ShapeDtypeStruct((8, MSG_ROWS, D), F32), jax.ShapeDtypeStruct((4, 8, ADA_SH), F32)],
        scratch_shapes=[pltpu.VMEM((8, 8, ADA_SH), F32), pltpu.SemaphoreType.DMA((7,)), pltpu.SemaphoreType.DMA((7,)),
                        pltpu.SemaphoreType.DMA((3,)), pltpu.SemaphoreType.DMA((3,)), pltpu.SemaphoreType.DMA((2,))],
        compiler_params=pltpu.CompilerParams(vmem_limit_bytes=VMEM_LIMIT),
    )(msg, w_ada, b_ada)


def _chip_exchange(parts, name, slots_from_source):
    n = len(parts)

    def body(*refs):
        ins, outs = refs[:n], refs[n:2 * n]
        send, recv, lsem = refs[2 * n:]
        x, y, c = _place()
        myj = 2 * x + y
        local, remote = [], []
        for a in range(n):
            src_own = ins[a].at[myj] if slots_from_source else ins[a]
            local.append(pltpu.make_async_copy(src_own, outs[a].at[myj], lsem.at[a]))
            for k, (dx, dy) in enumerate(CHIP_RELS):
                pj = 2 * (x ^ dx) + (y ^ dy)
                src = ins[a].at[pj] if slots_from_source else ins[a]
                remote.append((pltpu.make_async_remote_copy(
                    src_ref=src, dst_ref=outs[a].at[myj], send_sem=send.at[a, k], recv_sem=recv.at[a, k],
                    device_id=(x ^ dx, y ^ dy, c), device_id_type=MESH), a, k, pj))
        for cp in local:
            cp.start()
        for cp, _, _, _ in remote:
            cp.start()
        for _, a, k, pj in remote:
            src = ins[a].at[pj] if slots_from_source else ins[a]
            pltpu.make_async_remote_copy(
                src_ref=src, dst_ref=outs[a].at[pj], send_sem=send.at[a, k], recv_sem=recv.at[a, k],
                device_id=(x, y, c), device_id_type=MESH).wait_recv()
        for cp, _, _, _ in remote:
            cp.wait_send()
        for cp in local:
            cp.wait()

    shapes = [jax.ShapeDtypeStruct((4,) + tuple(p.shape[-2:]), p.dtype) for p in parts]
    return pl.pallas_call(
        body, name=name, in_specs=[ANY] * n, out_specs=[ANY] * n, out_shape=shapes,
        scratch_shapes=[pltpu.SemaphoreType.DMA((n, 3)), pltpu.SemaphoreType.DMA((n, 3)), pltpu.SemaphoreType.DMA((n,))],
    )(*parts)


def _all_exchange(parts, name):
    n = len(parts)

    def body(*refs):
        ins, outs = refs[:n], refs[n:2 * n]
        send, recv, lsem = refs[2 * n:]
        x, y, c = _place()
        me = 4 * x + 2 * y + c
        local, remote = [], []
        for a in range(n):
            local.append(pltpu.make_async_copy(ins[a], outs[a].at[me], lsem.at[a]))
            for k, (dx, dy, dc) in enumerate(DEV_RELS):
                remote.append((pltpu.make_async_remote_copy(
                    src_ref=ins[a], dst_ref=outs[a].at[me], send_sem=send.at[a, k], recv_sem=recv.at[a, k],
                    device_id=(x ^ dx, y ^ dy, c ^ dc), device_id_type=MESH), a, k,
                    4 * (x ^ dx) + 2 * (y ^ dy) + (c ^ dc)))
        for cp in local:
            cp.start()
        for cp, _, _, _ in remote:
            cp.start()
        for _, a, k, pd in remote:
            pltpu.make_async_remote_copy(
                src_ref=ins[a], dst_ref=outs[a].at[pd], send_sem=send.at[a, k], recv_sem=recv.at[a, k],
                device_id=(x, y, c), device_id_type=MESH).wait_recv()
        for cp, _, _, _ in remote:
            cp.wait_send()
        for cp in local:
            cp.wait()

    shapes = [jax.ShapeDtypeStruct((8,) + tuple(p.shape), p.dtype) for p in parts]
    return pl.pallas_call(
        body, name=name, in_specs=[ANY] * n, out_specs=[ANY] * n, out_shape=shapes,
        scratch_shapes=[pltpu.SemaphoreType.DMA((n, 7)), pltpu.SemaphoreType.DMA((n, 7)), pltpu.SemaphoreType.DMA((n,))],
    )(*parts)


def _pair_exchange(parts, name):
    n = len(parts)

    def body(*refs):
        ins, outs = refs[:n], refs[n:2 * n]
        send, recv = refs[2 * n:]
        x, y, c = _place()
        cps = [pltpu.make_async_remote_copy(
            src_ref=ins[a], dst_ref=outs[a], send_sem=send.at[a], recv_sem=recv.at[a],
            device_id=(x, y, 1 - c), device_id_type=MESH) for a in range(n)]
        for cp in cps:
            cp.start()
        for cp in cps:
            cp.wait_recv()
        for cp in cps:
            cp.wait_send()

    shapes = [jax.ShapeDtypeStruct(p.shape, p.dtype) for p in parts]
    return pl.pallas_call(
        body, name=name, in_specs=[ANY] * n, out_specs=[ANY] * n, out_shape=shapes,
        scratch_shapes=[pltpu.SemaphoreType.DMA((n,)), pltpu.SemaphoreType.DMA((n,))],
    )(*parts)


def _row_tile(rows, cap):
    best = rows
    for t in range(8, min(cap, rows) + 1, 8):
        if rows % t == 0:
            best = t
    return best if rows % 8 == 0 else rows


def _sum_slots(parts, name):
    n, rows, width = parts.shape
    tr = _row_tile(rows, 256)

    def body(p_ref, o_ref):
        acc = p_ref[0].astype(F32)
        for j in range(1, n):
            acc = acc + p_ref[j].astype(F32)
        o_ref[...] = acc

    return pl.pallas_call(
        body, grid=(rows // tr,), name=name,
        in_specs=[pl.BlockSpec((n, tr, width), lambda i: (0, i, 0))],
        out_specs=pl.BlockSpec((tr, width), lambda i: (i, 0)),
        out_shape=jax.ShapeDtypeStruct((rows, width), F32),
        compiler_params=_cparams(),
    )(parts)


def _adamw_math(g, w, m, v):
    m_new = ADAM_B1 * m + (1.0 - ADAM_B1) * g
    v_new = ADAM_B2 * v + (1.0 - ADAM_B2) * (g * g)
    m_hat = m_new / (1.0 - ADAM_B1 ** ADAM_STEP)
    v_hat = v_new / (1.0 - ADAM_B2 ** ADAM_STEP)
    delta = -ADAM_LR * (m_hat / (jnp.sqrt(v_hat) + ADAM_EPS) + ADAM_WD * w)
    return delta, m_new, v_new


def _adamw(grads, w, m, v, name):
    rows, width = w.shape
    tr = _row_tile(rows, 256 if width <= 1024 else 128)
    ng = len(grads)

    def body(*refs):
        g = refs[0][...]
        for r in refs[1:ng]:
            g = g + r[...]
        w_ref, m_ref, v_ref, g_out, d_out, m_out, v_out = refs[ng:]
        delta, m_new, v_new = _adamw_math(g, w_ref[...], m_ref[...], v_ref[...])
        g_out[...] = g
        d_out[...] = delta
        m_out[...] = m_new
        v_out[...] = v_new

    blk = pl.BlockSpec((tr, width), lambda i: (i, 0))
    return pl.pallas_call(
        body, grid=(rows // tr,), name=name,
        in_specs=[blk] * (ng + 3), out_specs=[blk] * 4,
        out_shape=[jax.ShapeDtypeStruct((rows, width), F32)] * 4,
        compiler_params=_cparams(),
    )(*grads, w, m, v)


def _adamw_ada(msgs, dmods, w, m, v):
    rows, width = w.shape
    tr = 128

    def body(c_ref, dm_ref, w_ref, m_ref, v_ref, g_out, d_out, m_out, v_out):
        cv = jnp.concatenate([c_ref[d, 0:1, :] for d in range(8)], axis=0)
        act = cv * _sigmoid(cv)
        g = _tn(act, dm_ref[...], precision=HI)
        delta, m_new, v_new = _adamw_math(g, w_ref[...], m_ref[...], v_ref[...])
        g_out[...] = g
        d_out[...] = delta
        m_out[...] = m_new
        v_out[...] = v_new

    blk = pl.BlockSpec((tr, width), lambda i: (i, 0))
    return pl.pallas_call(
        body, grid=(rows // tr,), name="adamw_w_ada",
        in_specs=[pl.BlockSpec((8, MSG_ROWS, tr), lambda i: (0, 0, i)), pl.BlockSpec((8, width), lambda i: (0, 0)),
                  blk, blk, blk],
        out_specs=[blk] * 4, out_shape=[jax.ShapeDtypeStruct((rows, width), F32)] * 4,
        compiler_params=_cparams(),
    )(msgs, dmods, w, m, v)


def _adamw_small(parts, w, m, v, name):
    n, rows, width = parts.shape

    def body(p_ref, w_ref, m_ref, v_ref, g_out, d_out, m_out, v_out):
        g = p_ref[0]
        for j in range(1, n):
            g = g + p_ref[j]
        delta, m_new, v_new = _adamw_math(g, w_ref[...], m_ref[...], v_ref[...])
        g_out[...] = g
        d_out[...] = delta
        m_out[...] = m_new
        v_out[...] = v_new

    return pl.pallas_call(
        body, name=name, in_specs=[VM] * 4, out_specs=[VM] * 4,
        out_shape=[jax.ShapeDtypeStruct((rows, width), F32)] * 4,
        compiler_params=pltpu.CompilerParams(vmem_limit_bytes=VMEM_LIMIT),
    )(parts, w, m, v)


SMALL_ROWS = 24


def _pad_row(vec, width=D):
    vec = vec.reshape(1, -1)
    return jnp.pad(vec, ((0, 0), (0, width - vec.shape[1])))


def _lanes_4_7(vec4):
    return jnp.zeros((1, 128), F32).at[0, NH:2 * NH].set(vec4.reshape(NH))


def kernel(x, c, w_ada, b_ada, norm_ffn1, ffn1_gate, ffn1_up, ffn1_down, norm_mix, w_in, conv_w, a_log, dt_bias, gdn_norm, pool_w, pool_scale, w_out, norm_ffn2, ffn2_gate, ffn2_up, ffn2_down, final_norm, loss_target, m_w_ada, m_b_ada, m_norm_ffn1, m_ffn1_gate, m_ffn1_up, m_ffn1_down, m_norm_mix, m_w_in, m_conv_w, m_a_log, m_dt_bias, m_gdn_norm, m_pool_w, m_pool_scale, m_w_out, m_norm_ffn2, m_ffn2_gate, m_ffn2_up, m_ffn2_down, m_final_norm, v_w_ada, v_b_ada, v_norm_ffn1, v_ffn1_gate, v_ffn1_up, v_ffn1_down, v_norm_mix, v_w_in, v_conv_w, v_a_log, v_dt_bias, v_gdn_norm, v_pool_w, v_pool_scale, v_w_out, v_norm_ffn2, v_ffn2_gate, v_ffn2_up, v_ffn2_down, v_final_norm):
    xs = x[0]
    tgt = loss_target[0]
    chip = 2 * lax.axis_index("x") + lax.axis_index("y")
    me = 2 * chip + lax.axis_index("c")

    msg = jnp.concatenate([jnp.broadcast_to(c, (8, D)), jnp.pad(conv_w[0], ((0, 0), (0, D - 3 * GW // 4))),
                           jnp.zeros((MSG_ROWS - 12, D), F32)], axis=0)
    b_sh = lax.dynamic_slice(b_ada, (0, chip * ADA_SH), (1, ADA_SH))
    msgs, mod4 = _ada_exchange(msg, w_ada[0], b_sh)
    mod = mod4[:, 0, :].reshape(NMOD, D)
    mrow = [mod[i:i + 1] for i in range(NMOD)]
    conv_full = jnp.concatenate([msgs[2 * j, 8:12, :3 * GW // 4] for j in range(4)], axis=1)

    packed = jnp.concatenate(
        [ffn1_gate[0].T, ffn1_up[0].T, ffn1_down[0], ffn2_gate[0].T, ffn2_up[0].T, ffn2_down[0], w_out[0],
         jnp.pad(w_in[0].T, ((0, WIN_PAD - WIN_SH), (0, 0)))], axis=0).astype(BF)
    (gathered,) = _chip_exchange([packed], "weight_gather", False)
    fsh = FF // 4

    wo = gathered[:, 6 * fsh:6 * fsh + D // 4, :].reshape(D, D)
    win_nat = gathered[:, 6 * fsh + D // 4:6 * fsh + D // 4 + WIN_SH, :].reshape(DIN, D)
    winT = jnp.concatenate([win_nat[:4 * GW], win_nat[4 * GW + 2 * NH:], win_nat[4 * GW:4 * GW + 2 * NH],
                            jnp.zeros((128 - 2 * NH, D), BF)], axis=0)
    alog, dtb = _lanes_4_7(a_log), _lanes_4_7(dt_bias)
    gnm = gdn_norm.reshape(1, DH)
    pwb = pool_w[0].astype(BF)
    psc = pool_scale.reshape(1, PW)
    fin = final_norm.reshape(1, D)

    x1, f1, a1, b1 = _ffn_fwd(xs, mrow[0], mrow[1], mrow[2], norm_ffn1, gathered, 0, "ffn1_fwd")
    xq, ba, qn, kn, vv, z, pp, bg, gc = _mix_proj(x1, mrow[3], mrow[4], norm_mix, winT, conv_full, alog, dtb)
    x2, mixed, cat, o, sall = _mix_core(x1, mrow[5], qn, kn, vv, z, pp, bg, gc, gnm, pwb, psc, wo)
    x3, f2, a2, b2 = _ffn_fwd(x2, mrow[6], mrow[7], mrow[8], norm_ffn2, gathered, 3, "ffn2_fwd")
    lpart, dx3, dfin = _loss_head(x3, tgt, fin)
    loss = lax.psum(jnp.sum(lpart), ("x", "y", "c"))

    dx2, da2, db2, s2, h2, df2, dsh3, dsc3, dgt3, dn3 = _ffn_dgrad(
        dx3, x2, f2, a2, b2, mrow[6], mrow[7], mrow[8], norm_ffn2, gathered, 3, "ffn2_dgrad")
    gg2, gu2, gd2 = _ffn_wgrad(da2, db2, s2, h2, df2, "ffn2_wgrad")
    dqn, dkn, dvv, dz, dpp, dbg, dgt2, dwo, dpw, dps, dgnm = _mix_core_bwd(
        dx2, mrow[5], mixed, cat, o, sall, qn, kn, vv, z, pp, bg, gc, gnm, pwb, psc, wo)
    dx1, dwin, dcw, dal, ddt, dsh2, dsc2, dn2 = _mix_proj_bwd(
        dx2, dqn, dkn, dvv, dz, dpp, dbg, xq, ba, x1, mrow[3], mrow[4], norm_mix, winT, conv_full, alog, dtb)
    dx0, da1, db1, s1, h1, df1, dsh1, dsc1, dgt1, dn1 = _ffn_dgrad(
        dx1, xs, f1, a1, b1, mrow[0], mrow[1], mrow[2], norm_ffn1, gathered, 0, "ffn1_dgrad")
    gg1, gu1, gd1 = _ffn_wgrad(da1, db1, s1, h1, df1, "ffn1_wgrad")

    dwin_nat = jnp.concatenate([dwin[:4 * GW], dwin[4 * GW + PW:4 * GW + PW + 2 * NH], dwin[4 * GW:4 * GW + PW]], axis=0)
    dwin_sl = jnp.pad(dwin_nat.reshape(4, WIN_SH, D), ((0, 0), (0, WIN_PAD - WIN_SH), (0, 0)))
    big = [t.reshape(4, fsh, D) for t in (gg1, gu1, gd1, gg2, gu2, gd2)] + [dwo.reshape(4, D // 4, D), dwin_sl]
    landed = _chip_exchange(big, "grad_scatter", True)
    names = ("g1", "u1", "d1", "g2", "u2", "d2", "wo", "win")
    psum = [_sum_slots(t, "sum_" + nm) for t, nm in zip(landed, names)]
    qsum = _pair_exchange(psum, "grad_pair")

    red = lambda t: jnp.sum(t, axis=0, keepdims=True)
    small = jnp.concatenate(
        [red(dn1), red(dn2), red(dn3), red(dfin),
         red(dsh1), red(dsc1), red(dgt1), red(dsh2), red(dsc2), red(dgt2), red(dsh3), red(dsc3), red(dgt3),
         _pad_row(red(dps)), _pad_row(red(dgnm)), _pad_row(red(dal)), _pad_row(red(ddt)),
         jnp.sum(dcw, axis=1).reshape(6, D), jnp.zeros((1, D), F32)], axis=0)
    small_all, dpw_all = _all_exchange([small, dpw.reshape(NG * 128, 128)], "small_gather")

    def unT(t, cols):
        return t[:cols].T

    upd = {}
    upd["ffn1_gate"] = _adamw([unT(psum[0], fsh), unT(qsum[0], fsh)], ffn1_gate[0], m_ffn1_gate[0], v_ffn1_gate[0], "adamw_g1")
    upd["ffn1_up"] = _adamw([unT(psum[1], fsh), unT(qsum[1], fsh)], ffn1_up[0], m_ffn1_up[0], v_ffn1_up[0], "adamw_u1")
    upd["ffn1_down"] = _adamw([psum[2], qsum[2]], ffn1_down[0], m_ffn1_down[0], v_ffn1_down[0], "adamw_d1")
    upd["ffn2_gate"] = _adamw([unT(psum[3], fsh), unT(qsum[3], fsh)], ffn2_gate[0], m_ffn2_gate[0], v_ffn2_gate[0], "adamw_g2")
    upd["ffn2_up"] = _adamw([unT(psum[4], fsh), unT(qsum[4], fsh)], ffn2_up[0], m_ffn2_up[0], v_ffn2_up[0], "adamw_u2")
    upd["ffn2_down"] = _adamw([psum[5], qsum[5]], ffn2_down[0], m_ffn2_down[0], v_ffn2_down[0], "adamw_d2")
    upd["w_out"] = _adamw([psum[6], qsum[6]], w_out[0], m_w_out[0], v_w_out[0], "adamw_wo")
    upd["w_in"] = _adamw([unT(psum[7], WIN_SH), unT(qsum[7], WIN_SH)], w_in[0], m_w_in[0], v_w_in[0], "adamw_win")
    dmods = lax.dynamic_slice(small_all[:, 4:4 + NMOD, :].reshape(8, NMOD * D), (0, chip * ADA_SH), (8, ADA_SH))
    upd["w_ada"] = _adamw_ada(msgs, dmods, w_ada[0], m_w_ada[0], v_w_ada[0])

    def pack_small(nf1, nmx, nf2, fn, bada, psc_, gn_, al_, dt_):
        return jnp.concatenate(
            [nf1.reshape(1, D), nmx.reshape(1, D), nf2.reshape(1, D), fn.reshape(1, D), bada.reshape(NMOD, D),
             _pad_row(psc_), _pad_row(gn_), _pad_row(_lanes_4_7(al_)), _pad_row(_lanes_4_7(dt_)),
             jnp.zeros((7, D), F32)], axis=0)

    ws = pack_small(norm_ffn1, norm_mix, norm_ffn2, final_norm, b_ada, pool_scale, gdn_norm, a_log, dt_bias)
    ms = pack_small(m_norm_ffn1, m_norm_mix, m_norm_ffn2, m_final_norm, m_b_ada, m_pool_scale, m_gdn_norm, m_a_log, m_dt_bias)
    vs = pack_small(v_norm_ffn1, v_norm_mix, v_norm_ffn2, v_final_norm, v_b_ada, v_pool_scale, v_gdn_norm, v_a_log, v_dt_bias)
    sm = _adamw_small(small_all, ws, ms, vs, "adamw_small")
    pw2 = lambda t: t.reshape(NG * 128, 128)
    upd_pw = _adamw_small(dpw_all, pw2(pool_w), pw2(m_pool_w), pw2(v_pool_w), "adamw_pool_w")
    csh = 3 * GW // 4
    gconv = lax.dynamic_slice(sm[0][17:23].reshape(4, 3 * GW), (0, chip * csh), (4, csh))
    upd["conv_w"] = _adamw([gconv], conv_w[0], m_conv_w[0], v_conv_w[0], "adamw_conv")

    def small_out(k):
        t = sm[k]
        return {
            "norm_ffn1": t[0:1], "norm_mix": t[1:2], "norm_ffn2": t[2:3], "final_norm": t[3],
            "b_ada": t[4:4 + NMOD].reshape(1, NMOD * D), "pool_scale": t[13:14, :PW], "gdn_norm": t[14:15, :DH],
            "a_log": t[15:16, NH:2 * NH], "dt_bias": t[16:17, NH:2 * NH],
        }

    order = ["w_ada", "b_ada", "norm_ffn1", "ffn1_gate", "ffn1_up", "ffn1_down", "norm_mix", "w_in", "conv_w", "a_log",
             "dt_bias", "gdn_norm", "pool_w", "pool_scale", "w_out", "norm_ffn2", "ffn2_gate", "ffn2_up", "ffn2_down",
             "final_norm"]
    outs = [loss, dx0[None]]
    for k in range(4):
        smk = small_out(k)
        for nm in order:
            if nm in upd:
                outs.append(upd[nm][k][None])
            elif nm == "pool_w":
                outs.append(upd_pw[k].reshape(1, NG, 128, 128))
            else:
                outs.append(smk[nm])
    return tuple(outs)
```

```python
import functools

import jax
import jax.numpy as jnp
from jax import lax
from jax.experimental import pallas as pl
from jax.experimental.pallas import tpu as pltpu

F32 = jnp.float32
BF = jnp.bfloat16

D = 1024
FF = 2816
FH = FF // 2
NH = 4
DH = 128
GW = NH * DH
CH = 64
PW = 512
NG = 4
POOL_WINDOWS = (2, 4, 8, 16)
HALO = 16
DIN = 4 * GW + 2 * NH + PW
DINP = 3 * GW + GW + PW + 128
EPS = 1e-6
ADAM_LR, ADAM_B1, ADAM_B2, ADAM_EPS, ADAM_WD, ADAM_STEP = 0.001, 0.9, 0.999, 1e-08, 0.01, 10

VMEM_LIMIT = 56 * 1024 * 1024

NT_DIMS = (((1,), (1,)), ((), ()))
TN_DIMS = (((0,), (0,)), ((), ()))
HI = lax.Precision.HIGHEST


def _nt(a, b, **kw):
    return lax.dot_general(a, b, NT_DIMS, preferred_element_type=F32, **kw)


def _tn(a, b, **kw):
    return lax.dot_general(a, b, TN_DIMS, preferred_element_type=F32, **kw)


def _nn(a, b, **kw):
    return jnp.dot(a, b, preferred_element_type=F32, **kw)


def _cparams(sem=("arbitrary",), **kw):
    return pltpu.CompilerParams(dimension_semantics=sem, vmem_limit_bytes=VMEM_LIMIT, **kw)


def _const_spec(shape):
    nd = len(shape)
    return pl.BlockSpec(shape, lambda *_: (0,) * nd, pipeline_mode=pl.Buffered(1))


def _row_spec(tm, width):
    return pl.BlockSpec((tm, width), lambda i: (i, 0))


def _sum8(v):
    return jnp.sum(v.reshape(v.shape[0] // 8, 8, v.shape[1]), axis=0)


def _sigmoid(v):
    return 1.0 / (1.0 + jnp.exp(-v))


def _tile(T, cap=512):
    return min(cap, T)


def _norm_mod_fwd(xv, gain, shift, scale):
    r = lax.rsqrt(jnp.mean(xv * xv, axis=-1, keepdims=True) + EPS)
    n = xv * r
    y = n * gain
    return n, r, y, y * (1.0 + scale) + shift


def _norm_mod_bwd(dh, n, r, y, gain, scale):
    dy = dh * (1.0 + scale)
    dn = dy * gain
    dx = r * (dn - n * jnp.mean(dn * n, axis=-1, keepdims=True))
    return dx, _sum8(dh), _sum8(dh * y), _sum8(dy * n)


def _ffn_wspecs(k0):
    return [pl.BlockSpec((4, FF // 4, D), lambda i, k=k0 + n: (0, k, 0), pipeline_mode=pl.Buffered(1)) for n in range(3)]


def _half(w_ref, j):
    return w_ref[2 * j:2 * j + 2].reshape(FH, D)


def _ffn_fwd(x, shift, scale, gate, gain, wall, k0, name, gather_block=None):
    T = x.shape[0]
    tm = _tile(T)
    nt = T // tm
    comm = gather_block is not None

    def body(*refs):
        x_ref, sh_ref, sc_ref, gt_ref, gn_ref, wg_ref, wu_ref, wd_ref = refs[:8]
        if comm:
            xo_ref, f_ref, a_ref, b_ref = refs[9:13]
            plan = _SplitGather(refs[8], refs[13], *refs[14:])
            _comm_begin(pl.program_id(0), plan, nt - 1)
        else:
            xo_ref, f_ref, a_ref, b_ref = refs[8:12]
        xv = x_ref[...]
        _, _, _, h = _norm_mod_fwd(xv, gn_ref[...], sh_ref[...], sc_ref[...])
        hb = h.astype(BF)
        facc = jnp.zeros((tm, D), F32)
        for j in range(2):
            a = _nt(hb, _half(wg_ref, j))
            b = _nt(hb, _half(wu_ref, j))
            a_ref[:, j * FH:(j + 1) * FH] = a.astype(BF)
            b_ref[:, j * FH:(j + 1) * FH] = b.astype(BF)
            s = (a * _sigmoid(a) * b).astype(BF)
            facc = facc + _nn(s, _half(wd_ref, j))
        f_ref[...] = facc
        xo_ref[...] = xv + 0.5 * gt_ref[...] * facc
        if comm:
            _comm_end(pl.program_id(0), plan, nt - 1)

    vec = _const_spec((1, D))
    extra_in = [gather_block] if comm else []
    return pl.pallas_call(
        body, grid=(nt,), name=name,
        in_specs=[_row_spec(tm, D), vec, vec, vec, vec] + _ffn_wspecs(k0) + [ANY] * comm,
        out_specs=[_row_spec(tm, D), _row_spec(tm, D), _row_spec(tm, FF), _row_spec(tm, FF)] + [ANY] * comm,
        out_shape=[jax.ShapeDtypeStruct((T, D), F32), jax.ShapeDtypeStruct((T, D), F32),
                   jax.ShapeDtypeStruct((T, FF), BF), jax.ShapeDtypeStruct((T, FF), BF)]
        + ([jax.ShapeDtypeStruct((4,) + gather_block.shape, gather_block.dtype)] if comm else []),
        scratch_shapes=_gather_sems() if comm else [],
        compiler_params=_cparams(),
    )(x, shift, scale, gate, gain, wall, wall, wall, *extra_in)


def _ffn_dgrad(dxo, x, f, a, b, shift, scale, gate, gain, wall, k0, name, scatter=()):
    T = x.shape[0]
    tm = _tile(T, 256)
    nt = T // tm
    ns = len(scatter)

    def body(*refs):
        dxo_ref, x_ref, f_ref, a_ref, b_ref, sh_ref, sc_ref, gt_ref, gn_ref, wg_ref, wu_ref, wd_ref = refs[:12]
        dx_ref, da_ref, db_ref, s_ref, h_ref, df_ref, dsh_ref, dsc_ref, dgt_ref, dgn_ref = refs[12 + ns:22 + ns]
        i = pl.program_id(0)
        if ns:
            plan = _Scatter(refs[12:12 + ns], refs[22 + ns:22 + 2 * ns], *refs[22 + 2 * ns:])
            _comm_begin(i, plan, nt - 1)
        dxo_v = dxo_ref[...]
        xv = x_ref[...]
        gain_v, scale_v = gn_ref[...], sc_ref[...]
        n, r, y, h = _norm_mod_fwd(xv, gain_v, sh_ref[...], scale_v)
        h_ref[...] = h.astype(BF)
        dgate = _sum8(0.5 * f_ref[...] * dxo_v)
        dfb = (0.5 * gt_ref[...] * dxo_v).astype(BF)
        df_ref[...] = dfb
        dh = jnp.zeros((tm, D), F32)
        for j in range(2):
            cols = slice(j * FH, (j + 1) * FH)
            ds = _nt(dfb, _half(wd_ref, j))
            av = a_ref[:, cols].astype(F32)
            bv = b_ref[:, cols].astype(F32)
            sig = _sigmoid(av)
            sa = av * sig
            da = (ds * bv * (sig * (1.0 + av * (1.0 - sig)))).astype(BF)
            db = (ds * sa).astype(BF)
            da_ref[:, cols] = da
            db_ref[:, cols] = db
            s_ref[:, cols] = (sa * bv).astype(BF)
            dh = dh + _nn(da, _half(wg_ref, j)) + _nn(db, _half(wu_ref, j))
        dxn, dsh, dsc, dgn = _norm_mod_bwd(dh, n, r, y, gain_v, scale_v)
        dx_ref[...] = dxo_v + dxn

        @pl.when(i == 0)
        def _():
            dsh_ref[...] = dsh
            dsc_ref[...] = dsc
            dgt_ref[...] = dgate
            dgn_ref[...] = dgn

        @pl.when(i > 0)
        def _():
            dsh_ref[...] += dsh
            dsc_ref[...] += dsc
            dgt_ref[...] += dgate
            dgn_ref[...] += dgn

        if ns:
            _comm_end(i, plan, nt - 1)

    vec = _const_spec((1, D))
    acc = pl.BlockSpec((8, D), lambda i: (0, 0))
    accs = jax.ShapeDtypeStruct((8, D), F32)
    return pl.pallas_call(
        body, grid=(nt,), name=name,
        in_specs=[_row_spec(tm, D), _row_spec(tm, D), _row_spec(tm, D), _row_spec(tm, FF), _row_spec(tm, FF),
                  vec, vec, vec, vec] + _ffn_wspecs(k0) + [ANY] * ns,
        out_specs=[_row_spec(tm, D), _row_spec(tm, FF), _row_spec(tm, FF), _row_spec(tm, FF),
                   _row_spec(tm, D), _row_spec(tm, D), acc, acc, acc, acc] + [ANY] * ns,
        out_shape=[jax.ShapeDtypeStruct((T, D), F32), jax.ShapeDtypeStruct((T, FF), BF),
                   jax.ShapeDtypeStruct((T, FF), BF), jax.ShapeDtypeStruct((T, FF), BF),
                   jax.ShapeDtypeStruct((T, D), BF), jax.ShapeDtypeStruct((T, D), BF), accs, accs, accs, accs]
        + [jax.ShapeDtypeStruct(t.shape, t.dtype) for t in scatter],
        scratch_shapes=_scatter_sems(ns) if ns else [],
        compiler_params=_cparams(),
    )(dxo, x, f, a, b, shift, scale, gate, gain, wall, wall, wall, *scatter)


def _ffn_wgrad(da, db, s, h, df, name):
    T = h.shape[0]
    tk = _tile(T, 256)
    nk = T // tk

    def body(da_ref, db_ref, s_ref, h_ref, df_ref, og_ref, ou_ref, od_ref, ag, au, ad):
        k = pl.program_id(1)
        pg = _tn(da_ref[...], h_ref[...])
        pu = _tn(db_ref[...], h_ref[...])
        pd = _tn(s_ref[...], df_ref[...])

        @pl.when(k == 0)
        def _():
            ag[...] = pg
            au[...] = pu
            ad[...] = pd

        @pl.when(k > 0)
        def _():
            ag[...] += pg
            au[...] += pu
            ad[...] += pd

        @pl.when(k == nk - 1)
        def _():
            og_ref[...] = ag[...].astype(BF)
            ou_ref[...] = au[...].astype(BF)
            od_ref[...] = ad[...].astype(BF)

    colblk = pl.BlockSpec((tk, FH), lambda j, k: (k, j))
    rowblk = pl.BlockSpec((tk, D), lambda j, k: (k, 0))
    outblk = pl.BlockSpec((FH, D), lambda j, k: (j, 0))
    outs = jax.ShapeDtypeStruct((FF, D), BF)
    return pl.pallas_call(
        body, grid=(2, nk), name=name,
        in_specs=[colblk, colblk, colblk, rowblk, rowblk],
        out_specs=[outblk, outblk, outblk],
        out_shape=[outs, outs, outs],
        scratch_shapes=[pltpu.VMEM((FH, D), F32)] * 3,
        compiler_params=_cparams(("arbitrary", "arbitrary")),
    )(da, db, s, h, df)


def _loss_head(x, target, gain):
    T = x.shape[0]
    tm = _tile(T)

    def body(x_ref, t_ref, gn_ref, ls_ref, dx_ref, dgn_ref):
        i = pl.program_id(0)
        xv = x_ref[...]
        gain_v = gn_ref[...]
        r = lax.rsqrt(jnp.mean(xv * xv, axis=-1, keepdims=True) + EPS)
        n = xv * r
        err = n * gain_v - t_ref[...]
        e2 = err * err
        part = e2[:, 0:128]
        for q in range(1, D // 128):
            part = part + e2[:, q * 128:(q + 1) * 128]
        lsum = _sum8(part) * (0.5 / D)
        dy = err * (1.0 / D)
        dn = dy * gain_v
        dx_ref[...] = r * (dn - n * jnp.mean(dn * n, axis=-1, keepdims=True))
        dgn = _sum8(dy * n)

        @pl.when(i == 0)
        def _():
            ls_ref[...] = lsum
            dgn_ref[...] = dgn

        @pl.when(i > 0)
        def _():
            ls_ref[...] += lsum
            dgn_ref[...] += dgn

    return pl.pallas_call(
        body, grid=(T // tm,), name="loss_head",
        in_specs=[_row_spec(tm, D), _row_spec(tm, D), _const_spec((1, D))],
        out_specs=[pl.BlockSpec((8, 128), lambda i: (0, 0)), _row_spec(tm, D), pl.BlockSpec((8, D), lambda i: (0, 0))],
        out_shape=[jax.ShapeDtypeStruct((8, 128), F32), jax.ShapeDtypeStruct((T, D), F32),
                   jax.ShapeDtypeStruct((8, D), F32)],
        compiler_params=_cparams(),
    )(x, target, gain)


def _seg_cumsum(v, row_in_chunk, reverse=False):
    n = v.shape[0]
    s = 1
    while s < CH:
        if reverse:
            moved = pltpu.roll(v, n - s, 0)
            ok = row_in_chunk < CH - s
        else:
            moved = pltpu.roll(v, s, 0)
            ok = row_in_chunk >= s
        v = v + jnp.where(ok, moved, 0.0)
        s *= 2
    return v


def _conv_silu(xq_ext_ref, cw_ref, tm):
    c = cw_ref[0:1, :] * xq_ext_ref[pl.ds(5, tm), :]
    for j in range(1, 4):
        c = c + cw_ref[j:j + 1, :] * xq_ext_ref[pl.ds(5 + j, tm), :]
    return c, _sigmoid(c)


def _gates(ba, alog, dtb, lane):
    beta = _sigmoid(ba)
    arg = ba + dtb
    softplus = jnp.maximum(arg, 0.0) + jnp.log(1.0 + jnp.exp(-jnp.abs(arg)))
    g = -jnp.exp(alog) * softplus
    return jnp.where(lane < NH, beta, 0.0), jnp.where((lane >= NH) & (lane < 2 * NH), g, 0.0), _sigmoid(arg)


def _mix_proj(x, shift, scale, gain, winT, conv_w, alog, dtb):
    T = x.shape[0]
    tm = _tile(T)

    def body(x_ref, sh_ref, sc_ref, gn_ref, w_ref, cw_ref, al_ref, dt_ref,
             xq_ref, ba_ref, qn_ref, kn_ref, v_ref, z_ref, p_ref, bg_ref, gc_ref, ext):
        i = pl.program_id(0)
        _, _, _, h = _norm_mod_fwd(x_ref[...], gn_ref[...], sh_ref[...], sc_ref[...])
        hb = h.astype(BF)

        @pl.when(i == 0)
        def _():
            ext[pl.ds(0, 8), :] = jnp.zeros((8, 3 * GW), F32)

        xq = _nt(hb, w_ref[pl.ds(0, 3 * GW), :])
        xq_ref[...] = xq
        ext[pl.ds(8, tm), :] = xq
        z_ref[...] = _nt(hb, w_ref[pl.ds(3 * GW, GW), :])
        p_ref[...] = _nt(hb, w_ref[pl.ds(4 * GW, PW), :])
        ba = _nt(hb, w_ref[pl.ds(4 * GW + PW, 128), :])
        ba_ref[...] = ba

        c, sg = _conv_silu(ext, cw_ref, tm)
        ext[pl.ds(0, 8), :] = ext[pl.ds(tm, 8), :]
        qt = c * sg
        for hd in range(NH):
            cq = slice(hd * DH, (hd + 1) * DH)
            ck = slice(GW + hd * DH, GW + (hd + 1) * DH)
            qh, kh = qt[:, cq], qt[:, ck]
            qn_ref[:, cq] = qh * (lax.rsqrt(jnp.sum(qh * qh, axis=-1, keepdims=True) + EPS) * DH ** -0.5)
            kn_ref[:, cq] = kh * lax.rsqrt(jnp.sum(kh * kh, axis=-1, keepdims=True) + EPS)
        v_ref[...] = qt[:, 2 * GW:3 * GW]

        lane = lax.broadcasted_iota(jnp.int32, (tm, 128), 1)
        row = lax.broadcasted_iota(jnp.int32, (tm, 128), 0) % CH
        beta, g, _ = _gates(ba, al_ref[...], dt_ref[...], lane)
        bg_ref[...] = beta + g
        gc_ref[...] = _seg_cumsum(g, row)

    wide = lambda w: _row_spec(tm, w)
    shp = lambda w: jax.ShapeDtypeStruct((T, w), F32)
    return pl.pallas_call(
        body, grid=(T // tm,), name="mix_proj",
        in_specs=[wide(D), _const_spec((1, D)), _const_spec((1, D)), _const_spec((1, D)), _const_spec((DINP, D)),
                  _const_spec((4, 3 * GW)), _const_spec((1, 128)), _const_spec((1, 128))],
        out_specs=[wide(3 * GW), wide(128), wide(GW), wide(GW), wide(GW), wide(GW), wide(PW), wide(128), wide(128)],
        out_shape=[shp(3 * GW), shp(128), shp(GW), shp(GW), shp(GW), shp(GW), shp(PW), shp(128), shp(128)],
        scratch_shapes=[pltpu.VMEM((tm + 8, 3 * GW), F32)],
        compiler_params=_cparams(),
    )(x, shift, scale, gain, winT, conv_w, alog, dtb)


R2 = 2 * CH
TRI_PREC = None


def _tmm(fn, a, b):
    if TRI_PREC is None:
        return fn(a.astype(BF), b.astype(BF))
    return fn(a, b, precision=TRI_PREC)


def _pair_consts():
    ii = lax.broadcasted_iota(jnp.int32, (R2, R2), 0)
    jj = lax.broadcasted_iota(jnp.int32, (R2, R2), 1)
    same = (ii < CH) == (jj < CH)
    r = lax.broadcasted_iota(jnp.int32, (R2, 1), 0)
    return dict(causal=same & (ii >= jj), strict=same & (ii > jj), eye=(ii == jj).astype(F32), rowA=r < CH,
                last=(r == CH - 1) | (r == R2 - 1),
                rowS=lax.broadcasted_iota(jnp.int32, (2 * DH, 1), 0) < DH)


def _tri_inverse(m, eye):
    neg = -m
    t = eye + neg
    pw = neg
    for _ in range(5):
        pw = _tmm(_nn, pw, pw)
        t = _tmm(_nn, t, eye + pw)
    return t


def _pair_forward(q, k, v, beta, gcv, gl, S, cn):
    causal, rowA = cn["causal"], cn["rowA"]
    gc_b = jnp.broadcast_to(gcv, (R2, R2))
    gam = jnp.where(causal, jnp.exp(jnp.where(causal, gc_b - gc_b.T, 0.0)), 0.0)
    kb = k * beta
    vb = v * beta
    kbf = k.astype(BF)
    P = _nt(kb.astype(BF), kbf)
    QK = _nt(q.astype(BF), kbf)
    M = jnp.where(cn["strict"], P * gam, 0.0)
    Tm = _tri_inverse(M, cn["eye"])
    E = jnp.exp(gcv)
    Fd = jnp.exp(gl - gcv)
    egl = jnp.exp(gl)
    egl_st = jnp.concatenate([jnp.broadcast_to(egl[0:1], (DH, 1)), jnp.broadcast_to(egl[CH:CH + 1], (DH, 1))], axis=0)
    kbE = kb * E
    u = _tmm(_nn, Tm, vb)
    w = _tmm(_nn, Tm, kbE)
    Sb = S.astype(BF)

    def bd(t):
        return jnp.concatenate([jnp.where(rowA, t, 0.0), jnp.where(rowA, 0.0, t)], axis=1).astype(BF)

    w_bd = bd(w)
    vn = u - _nn(w_bd, Sb)
    Q = QK * gam
    qE_bd = bd(q * E)
    kF_bd = bd(k * Fd)
    vnb = vn.astype(BF)
    o = _nn(qE_bd, Sb) + _nn(Q.astype(BF), vnb)
    S_new = S * egl_st + _tn(kF_bd, vnb)
    return dict(gam=gam, kb=kb, vb=vb, P=P, QK=QK, Tm=Tm, E=E, Fd=Fd, egl=egl, egl_st=egl_st, kbE=kbE, w_bd=w_bd,
                vnb=vnb, Q=Q, qE_bd=qE_bd, kF_bd=kF_bd, o=o, S_new=S_new)


def _stack_heads(ref, rows, pair):
    return jnp.concatenate([ref[rows, (2 * pair) * DH:(2 * pair + 1) * DH],
                            ref[rows, (2 * pair + 1) * DH:(2 * pair + 2) * DH]], axis=0)


def _stack_cols(val, lane_a, lane_b, bcast_rows=None):
    a, b = val[:, lane_a:lane_a + 1], val[:, lane_b:lane_b + 1]
    if bcast_rows:
        a, b = jnp.broadcast_to(a, (bcast_rows, 1)), jnp.broadcast_to(b, (bcast_rows, 1))
    return jnp.concatenate([a, b], axis=0)


def _pool_windows(ext, tm, reverse):
    n = tm + HALO
    outs = []
    for gi in range(NG):
        a = ext[:, gi * 128:(gi + 1) * 128]
        s = 1
        while s < POOL_WINDOWS[gi]:
            a = a + pltpu.roll(a, (n - s) if reverse else s, 0)
            s *= 2
        outs.append(a[0:tm] if reverse else a[HALO:HALO + tm])
    return jnp.concatenate(outs, axis=1)


def _pool_count(tm, tile_index):
    t1 = (lax.broadcasted_iota(jnp.int32, (tm, PW), 0) + tile_index * tm + 1).astype(F32)
    win = jnp.concatenate([jnp.full((tm, 128), float(w), F32) for w in POOL_WINDOWS], axis=1)
    return 1.0 / jnp.minimum(t1, win)


def _mix_core(x, gate, qn, kn, v, z, p, bg, gc, gnorm, pool_w, pool_scale, w_out):
    T = x.shape[0]
    tm = _tile(T)
    nc = tm // CH

    def body(x_ref, gt_ref, qn_ref, kn_ref, v_ref, z_ref, p_ref, bg_ref, gc_ref, gnm_ref, pw_ref, ps_ref, wo_ref,
             xo_ref, mx_ref, cat_ref, o_ref, sall_ref, S_scr, pext):
        i = pl.program_id(0)

        @pl.when(i == 0)
        def _():
            S_scr[...] = jnp.zeros((NH * DH, DH), F32)
            pext[pl.ds(0, HALO), :] = jnp.zeros((HALO, PW), F32)

        cn = _pair_consts()

        def chunk(c, carry):
            r0 = pl.multiple_of(c * CH, CH)
            rows = pl.ds(r0, CH)
            bgv = bg_ref[rows, :]
            gcv_all = gc_ref[rows, :]
            gl_all = gc_ref[pl.ds(r0 + CH - 1, 1), :]
            for pr in range(NH // 2):
                ha, hb = 2 * pr, 2 * pr + 1
                S = S_scr[pl.ds(pr * 2 * DH, 2 * DH), :]
                sall_ref[c, ha:hb + 1] = S.reshape(2, DH, DH)
                res = _pair_forward(_stack_heads(qn_ref, rows, pr), _stack_heads(kn_ref, rows, pr),
                                    _stack_heads(v_ref, rows, pr), _stack_cols(bgv, ha, hb),
                                    _stack_cols(gcv_all, NH + ha, NH + hb), _stack_cols(gl_all, NH + ha, NH + hb, CH),
                                    S, cn)
                o_ref[rows, ha * DH:(ha + 1) * DH] = res["o"][0:CH]
                o_ref[rows, hb * DH:(hb + 1) * DH] = res["o"][CH:R2]
                S_scr[pl.ds(pr * 2 * DH, 2 * DH), :] = res["S_new"]
            return carry

        lax.fori_loop(0, nc, chunk, 0)

        for hd in range(NH):
            cols = slice(hd * DH, (hd + 1) * DH)
            oh = o_ref[:, cols]
            zh = z_ref[:, cols]
            r = lax.rsqrt(jnp.mean(oh * oh, axis=-1, keepdims=True) + EPS)
            cat_ref[:, cols] = (oh * r * gnm_ref[...] * (zh * _sigmoid(zh))).astype(BF)

        pv = p_ref[...]
        pext[pl.ds(HALO, tm), :] = pv
        pooled = _pool_windows(pext[...], tm, False) * _pool_count(tm, i) - pv
        pext[pl.ds(0, HALO), :] = pext[pl.ds(tm, HALO), :]
        for gi in range(NG):
            cols = slice(gi * 128, (gi + 1) * 128)
            pm = _nn(pooled[:, cols].astype(BF), pw_ref[gi])
            cat_ref[:, GW + gi * 128:GW + (gi + 1) * 128] = (pm * ps_ref[:, cols]).astype(BF)

        mixed = _nn(cat_ref[...], wo_ref[...])
        mx_ref[...] = mixed
        xo_ref[...] = x_ref[...] + gt_ref[...] * mixed

    wide = lambda w: _row_spec(tm, w)
    return pl.pallas_call(
        body, grid=(T // tm,), name="mix_core",
        in_specs=[wide(D), _const_spec((1, D)), wide(GW), wide(GW), wide(GW), wide(GW), wide(PW), wide(128), wide(128),
                  _const_spec((1, DH)), _const_spec((NG, 128, 128)), _const_spec((1, PW)), _const_spec((D, D))],
        out_specs=[wide(D), wide(D), wide(D), wide(GW), pl.BlockSpec((nc, NH, DH, DH), lambda i: (i, 0, 0, 0))],
        out_shape=[jax.ShapeDtypeStruct((T, D), F32), jax.ShapeDtypeStruct((T, D), F32),
                   jax.ShapeDtypeStruct((T, D), BF), jax.ShapeDtypeStruct((T, GW), F32),
                   jax.ShapeDtypeStruct((T // CH, NH, DH, DH), F32)],
        scratch_shapes=[pltpu.VMEM((NH * DH, DH), F32), pltpu.VMEM((tm + HALO, PW), F32)],
        compiler_params=_cparams(),
    )(x, gate, qn, kn, v, z, p, bg, gc, gnorm, pool_w, pool_scale, w_out)


def _pair_backward(fw, q, k, v, beta, S, do, dSn, cn):
    bf = lambda t: t.astype(BF)
    rowA, rowS = cn["rowA"], cn["rowS"]
    sel = lambda t: jnp.where(rowA, t[:, 0:DH], t[:, DH:2 * DH])
    Sb, dSb, dob, vnb = bf(S), bf(dSn), bf(do), fw["vnb"]
    dvn = _tn(bf(fw["Q"]), dob) + _nn(fw["kF_bd"], dSb)
    dQ = _nt(dob, vnb)
    dqE = sel(_nt(dob, Sb))
    dkF = sel(_nt(vnb, dSb))
    dvnb = bf(dvn)
    dw = -sel(_nt(dvnb, Sb))
    dS_new = _tn(fw["qE_bd"], dob) + fw["egl_st"] * dSn - _tn(fw["w_bd"], dvnb)
    prod = jnp.sum(dSn * S, axis=1, keepdims=True)
    d_egl_a = jnp.sum(jnp.where(rowS, prod, 0.0), axis=0, keepdims=True)
    d_egl_b = jnp.sum(jnp.where(rowS, 0.0, prod), axis=0, keepdims=True)
    Tm = fw["Tm"]
    dvb = _tmm(_tn, Tm, dvn)
    dkbE = _tmm(_tn, Tm, dw)
    dTm = _tmm(_nt, dvn, fw["vb"]) + _tmm(_nt, dw, fw["kbE"])
    dA = -_tmm(_tn, Tm, _tmm(_nt, dTm, Tm))
    gam = fw["gam"]
    N = jnp.where(cn["strict"], dA * gam, 0.0)
    Rm = dQ * gam
    Wm = Rm * fw["QK"] + N * fw["P"]
    dgc = jnp.sum(Wm, axis=1, keepdims=True) - jnp.sum(Wm.T, axis=1, keepdims=True)
    kbf, Nb, Rb = bf(k), bf(N), bf(Rm)
    E, Fd, egl = fw["E"], fw["Fd"], fw["egl"]
    dq = dqE * E + _nn(Rb, kbf)
    dkb = dkbE * E + _nn(Nb, kbf)
    dk = dkF * Fd + _tn(Rb, bf(q)) + _tn(Nb, bf(fw["kb"])) + beta * dkb
    dbeta = jnp.sum(dkb * k, axis=1, keepdims=True) + jnp.sum(dvb * v, axis=1, keepdims=True)
    dv = beta * dvb
    dE = jnp.sum(dqE * q, axis=1, keepdims=True) + jnp.sum(dkbE * fw["kb"], axis=1, keepdims=True)
    fdf = jnp.sum(dkF * k, axis=1, keepdims=True) * Fd
    dgl_a = d_egl_a * egl[0:1] + jnp.sum(jnp.where(rowA, fdf, 0.0), axis=0, keepdims=True)
    dgl_b = d_egl_b * egl[CH:CH + 1] + jnp.sum(jnp.where(rowA, 0.0, fdf), axis=0, keepdims=True)
    dgc = dgc + dE * E - fdf + jnp.where(cn["last"], jnp.where(rowA, dgl_a, dgl_b), 0.0)
    return dq, dk, dv, dbeta, dgc, dS_new


def _mix_core_bwd(dxo, gate, mixed, cat, o, sall, qn, kn, v, z, p, bg, gc, gnorm, pool_w, pool_scale, w_out,
                  scatter=()):
    T = dxo.shape[0]
    tm = _tile(T, 256)
    nt = T // tm
    nc = tm // CH
    ns = len(scatter)

    def body(*refs):
        (dx_ref, gt_ref, mx_ref, cat_ref, o_ref, sall_ref, qn_ref, kn_ref, v_ref, z_ref, p_ref, ph_ref, bg_ref,
         gc_ref, gnm_ref, pw_ref, ps_ref, wo_ref) = refs[:18]
        (dq_ref, dk_ref, dv_ref, dz_ref, dp_ref, dbg_ref, dgt_ref, dwo_ref, dpw_ref, dps_ref,
         dgn_ref) = refs[18 + ns:29 + ns]
        dS_scr, pext, yext, do_buf, dwo_acc = refs[29 + 2 * ns:34 + 2 * ns]
        i = pl.program_id(0)
        ti = nt - 1 - i
        if ns:
            plan = _Scatter(refs[18:18 + ns], refs[29 + ns:29 + 2 * ns], *refs[34 + 2 * ns:])
            _comm_begin(i, plan, nt - 1)

        @pl.when(i == 0)
        def _():
            dS_scr[...] = jnp.zeros((NH * DH, DH), F32)
            yext[pl.ds(tm, HALO), :] = jnp.zeros((HALO, PW), F32)
            dwo_acc[...] = jnp.zeros((D, D), F32)
            dgt_ref[...] = jnp.zeros((8, D), F32)
            dpw_ref[...] = jnp.zeros((NG, 128, 128), F32)
            dps_ref[...] = jnp.zeros((8, PW), F32)
            dgn_ref[...] = jnp.zeros((8, DH), F32)

        dx2 = dx_ref[...]
        dgt_ref[...] += _sum8(mx_ref[...] * dx2)
        dmix = (gt_ref[...] * dx2).astype(BF)
        dcat = _nt(dmix, wo_ref[...])
        dwo_acc[...] += _tn(cat_ref[...], dmix)

        pv = p_ref[...]
        pext[pl.ds(0, HALO), :] = jnp.where(ti == 0, 0.0, ph_ref[...])
        pext[pl.ds(HALO, tm), :] = pv
        inv_cnt = _pool_count(tm, ti)
        pooled = _pool_windows(pext[...], tm, False) * inv_cnt - pv
        dpooled = []
        for gi in range(NG):
            cols = slice(gi * 128, (gi + 1) * 128)
            pgb = pooled[:, cols].astype(BF)
            pm = _nn(pgb, pw_ref[gi])
            dpo = dcat[:, GW + gi * 128:GW + (gi + 1) * 128]
            dps_ref[:, cols] += _sum8(dpo * pm)
            dpm = (dpo * ps_ref[:, cols]).astype(BF)
            dpooled.append(_nt(dpm, pw_ref[gi]))
            dpw_ref[gi] += _tn(pgb, dpm)
        dpooled = jnp.concatenate(dpooled, axis=1)
        y = dpooled * inv_cnt
        yext[pl.ds(0, tm), :] = y
        dp_ref[...] = _pool_windows(yext[...], tm, True) - dpooled
        yext[pl.ds(tm, HALO), :] = y[0:HALO]

        gnm = gnm_ref[...]
        dgn = jnp.zeros((8, DH), F32)
        for hd in range(NH):
            cols = slice(hd * DH, (hd + 1) * DH)
            oh = o_ref[:, cols]
            zh = z_ref[:, cols]
            r = lax.rsqrt(jnp.mean(oh * oh, axis=-1, keepdims=True) + EPS)
            n = oh * r
            sg = _sigmoid(zh)
            zs = zh * sg
            dgo = dcat[:, cols]
            dgn = dgn + _sum8(dgo * zs * n)
            dn = dgo * zs * gnm
            do_buf[:, cols] = r * (dn - n * jnp.mean(dn * n, axis=-1, keepdims=True))
            dz_ref[:, cols] = dgo * n * gnm * (sg * (1.0 + zh * (1.0 - sg)))
        dgn_ref[...] += dgn

        cn = _pair_consts()
        lane_c = lax.broadcasted_iota(jnp.int32, (CH, 128), 1)

        def chunk(cc, carry):
            c = nc - 1 - cc
            r0 = pl.multiple_of(c * CH, CH)
            rows = pl.ds(r0, CH)
            bgv = bg_ref[rows, :]
            gcv_all = gc_ref[rows, :]
            gl_all = gc_ref[pl.ds(r0 + CH - 1, 1), :]
            dbg = jnp.zeros((CH, 128), F32)
            for pr in range(NH // 2):
                ha, hb = 2 * pr, 2 * pr + 1
                srows = pl.ds(pr * 2 * DH, 2 * DH)
                S = sall_ref[c, ha:hb + 1].reshape(2 * DH, DH)
                q, k, vv = _stack_heads(qn_ref, rows, pr), _stack_heads(kn_ref, rows, pr), _stack_heads(v_ref, rows, pr)
                beta = _stack_cols(bgv, ha, hb)
                fw = _pair_forward(q, k, vv, beta, _stack_cols(gcv_all, NH + ha, NH + hb),
                                   _stack_cols(gl_all, NH + ha, NH + hb, CH), S, cn)
                dq, dk, dv, dbeta, dgc, dS_new = _pair_backward(fw, q, k, vv, beta, S, _stack_heads(do_buf, rows, pr),
                                                                dS_scr[srows, :], cn)
                for hd, half in ((ha, slice(0, CH)), (hb, slice(CH, R2))):
                    cols = slice(hd * DH, (hd + 1) * DH)
                    dq_ref[rows, cols] = dq[half]
                    dk_ref[rows, cols] = dk[half]
                    dv_ref[rows, cols] = dv[half]
                    dbg = dbg + jnp.where(lane_c == hd, dbeta[half], 0.0) + jnp.where(lane_c == NH + hd, dgc[half], 0.0)
                dS_scr[srows, :] = dS_new
            dbg_ref[rows, :] = dbg
            return carry

        lax.fori_loop(0, nc, chunk, 0)

        lane = lax.broadcasted_iota(jnp.int32, (tm, 128), 1)
        row = lax.broadcasted_iota(jnp.int32, (tm, 128), 0) % CH
        dbg_all = dbg_ref[...]
        dg = _seg_cumsum(jnp.where(lane >= NH, dbg_all, 0.0), row, reverse=True)
        dbg_ref[...] = jnp.where(lane < NH, dbg_all, dg)

        @pl.when(i == nt - 1)
        def _():
            dwo_ref[...] = dwo_acc[...].astype(BF)

        if ns:
            _comm_end(i, plan, nt - 1)

    rev = lambda w: pl.BlockSpec((tm, w), lambda i: (nt - 1 - i, 0))
    halo = pl.BlockSpec((HALO, PW), lambda i: (jnp.maximum((nt - 1 - i) * (tm // HALO) - 1, 0), 0))
    shp = lambda w: jax.ShapeDtypeStruct((T, w), F32)
    fix = lambda *s: pl.BlockSpec(s, lambda i: (0,) * len(s))
    return pl.pallas_call(
        body, grid=(nt,), name="mix_core_bwd",
        in_specs=[rev(D), _const_spec((1, D)), rev(D), rev(D), rev(GW),
                  pl.BlockSpec((nc, NH, DH, DH), lambda i: (nt - 1 - i, 0, 0, 0)),
                  rev(GW), rev(GW), rev(GW), rev(GW), rev(PW), halo, rev(128), rev(128),
                  _const_spec((1, DH)), _const_spec((NG, 128, 128)), _const_spec((1, PW)), _const_spec((D, D))]
        + [ANY] * ns,
        out_specs=[rev(GW), rev(GW), rev(GW), rev(GW), rev(PW), rev(128),
                   fix(8, D), fix(D, D), fix(NG, 128, 128), fix(8, PW), fix(8, DH)] + [ANY] * ns,
        out_shape=[shp(GW), shp(GW), shp(GW), shp(GW), shp(PW), shp(128),
                   jax.ShapeDtypeStruct((8, D), F32), jax.ShapeDtypeStruct((D, D), BF),
                   jax.ShapeDtypeStruct((NG, 128, 128), F32), jax.ShapeDtypeStruct((8, PW), F32),
                   jax.ShapeDtypeStruct((8, DH), F32)] + [jax.ShapeDtypeStruct(t.shape, t.dtype) for t in scatter],
        scratch_shapes=[pltpu.VMEM((NH * DH, DH), F32), pltpu.VMEM((tm + HALO, PW), F32),
                        pltpu.VMEM((tm + HALO, PW), F32), pltpu.VMEM((tm, GW), F32), pltpu.VMEM((D, D), F32)]
        + (_scatter_sems(ns) if ns else []),
        compiler_params=_cparams(),
    )(dxo, gate, mixed, cat, o, sall, qn, kn, v, z, p, p, bg, gc, gnorm, pool_w, pool_scale, w_out, *scatter)


def _mix_proj_bwd(dxo, dqn, dkn, dv, dz, dp, dbg, xq, ba, x, shift, scale, gain, winT, conv_w, alog, dtb):
    T = x.shape[0]
    tm = min(256, T)
    nt = T // tm
    W3 = 3 * GW

    def body(dxo_ref, dqn_ref, dkn_ref, dv_ref, dz_ref, dp_ref, dbg_ref, xq_ref, xh_ref, ba_ref, x_ref, sh_ref, sc_ref,
             gn_ref, w_ref, cw_ref, al_ref, dt_ref,
             dx_ref, dw_ref, dcw_ref, dal_ref, ddt_ref, dsh_ref, dsc_ref, dgn_ref,
             ext, dcext, dproj, dw_acc):
        i = pl.program_id(0)
        ti = nt - 1 - i

        @pl.when(i == 0)
        def _():
            dcext[pl.ds(tm, 8), :] = jnp.zeros((8, W3), F32)
            dw_acc[...] = jnp.zeros((DINP, D), F32)
            dcw_ref[...] = jnp.zeros((4, 8, W3), F32)
            dal_ref[...] = jnp.zeros((8, 128), F32)
            ddt_ref[...] = jnp.zeros((8, 128), F32)
            dsh_ref[...] = jnp.zeros((8, D), F32)
            dsc_ref[...] = jnp.zeros((8, D), F32)
            dgn_ref[...] = jnp.zeros((8, D), F32)

        ext[pl.ds(0, 8), :] = jnp.where(ti == 0, 0.0, xh_ref[...])
        ext[pl.ds(8, tm), :] = xq_ref[...]
        c, sg = _conv_silu(ext, cw_ref, tm)
        qt = c * sg
        dsilu = sg * (1.0 + c * (1.0 - sg))
        for hd in range(NH):
            for part, dref, mult in ((0, dqn_ref, DH ** -0.5), (1, dkn_ref, 1.0)):
                cols = slice(part * GW + hd * DH, part * GW + (hd + 1) * DH)
                xh = qt[:, cols]
                rr = lax.rsqrt(jnp.sum(xh * xh, axis=-1, keepdims=True) + EPS)
                unit = xh * rr
                du = dref[:, hd * DH:(hd + 1) * DH] * mult
                dxh = rr * (du - unit * jnp.sum(du * unit, axis=-1, keepdims=True))
                dcext[pl.ds(0, tm), cols] = dxh * dsilu[:, cols]
        dcext[pl.ds(0, tm), 2 * GW:W3] = dv_ref[...] * dsilu[:, 2 * GW:W3]
        dc = dcext[pl.ds(0, tm), :]
        dxq = jnp.zeros((tm, W3), F32)
        for j in range(4):
            dcw_ref[j] += _sum8(dc * ext[pl.ds(5 + j, tm), :])
            dxq = dxq + cw_ref[j:j + 1, :] * dcext[pl.ds(3 - j, tm), :]
        dcext[pl.ds(tm, 8), :] = dc[0:8]
        lane = lax.broadcasted_iota(jnp.int32, (tm, 128), 1)
        bav = ba_ref[...]
        beta, g, sarg = _gates(bav, al_ref[...], dt_ref[...], lane)
        dbg_v = dbg_ref[...]
        is_g = (lane >= NH) & (lane < 2 * NH)
        dbraw = jnp.where(lane < NH, dbg_v * beta * (1.0 - beta), 0.0)
        daraw = jnp.where(is_g, dbg_v * (-jnp.exp(al_ref[...])) * sarg, 0.0)
        dal_ref[...] += _sum8(jnp.where(is_g, dbg_v * g, 0.0))
        ddt_ref[...] += _sum8(daraw)
        dproj[:, 0:W3] = dxq.astype(BF)
        dproj[:, W3:W3 + GW] = dz_ref[...].astype(BF)
        dproj[:, W3 + GW:W3 + GW + PW] = dp_ref[...].astype(BF)
        dproj[:, W3 + GW + PW:DINP] = (dbraw + daraw).astype(BF)
        gain_v, scale_v = gn_ref[...], sc_ref[...]
        n, r, y, h = _norm_mod_fwd(x_ref[...], gain_v, sh_ref[...], scale_v)
        dpj = dproj[...]
        dh = _nn(dpj, w_ref[...])
        dw_acc[...] += _tn(dpj, h.astype(BF))
        dxn, dsh, dsc, dgn = _norm_mod_bwd(dh, n, r, y, gain_v, scale_v)
        dx_ref[...] = dxo_ref[...] + dxn
        dsh_ref[...] += dsh
        dsc_ref[...] += dsc
        dgn_ref[...] += dgn

        @pl.when(i == nt - 1)
        def _():
            dw_ref[...] = dw_acc[...].astype(BF)

    rev = lambda w: pl.BlockSpec((tm, w), lambda i: (nt - 1 - i, 0))
    halo = pl.BlockSpec((8, W3), lambda i: (jnp.maximum((nt - 1 - i) * (tm // 8) - 1, 0), 0))
    fix = lambda *s: pl.BlockSpec(s, lambda i: (0,) * len(s))
    vec = _const_spec((1, D))
    return pl.pallas_call(
        body, grid=(nt,), name="mix_proj_bwd",
        in_specs=[rev(D), rev(GW), rev(GW), rev(GW), rev(GW), rev(PW), rev(128), rev(W3), halo, rev(128), rev(D),
                  vec, vec, vec, _const_spec((DINP, D)), _const_spec((4, W3)), _const_spec((1, 128)),
                  _const_spec((1, 128))],
        out_specs=[rev(D), fix(DINP, D), fix(4, 8, W3), fix(8, 128), fix(8, 128), fix(8, D), fix(8, D), fix(8, D)],
        out_shape=[jax.ShapeDtypeStruct((T, D), F32), jax.ShapeDtypeStruct((DINP, D), BF),
                   jax.ShapeDtypeStruct((4, 8, W3), F32), jax.ShapeDtypeStruct((8, 128), F32),
                   jax.ShapeDtypeStruct((8, 128), F32), jax.ShapeDtypeStruct((8, D), F32),
                   jax.ShapeDtypeStruct((8, D), F32), jax.ShapeDtypeStruct((8, D), F32)],
        scratch_shapes=[pltpu.VMEM((tm + 8, W3), F32), pltpu.VMEM((tm + 8, W3), F32), pltpu.VMEM((tm, DINP), BF),
                        pltpu.VMEM((DINP, D), F32)],
        compiler_params=_cparams(),
    )(dxo, dqn, dkn, dv, dz, dp, dbg, xq, xq, ba, x, shift, scale, gain, winT, conv_w, alog, dtb)


MESH = pl.DeviceIdType.MESH
CHIP_RELS = ((1, 0), (0, 1), (1, 1))
DEV_RELS = tuple((dx, dy, dc) for dx in (0, 1) for dy in (0, 1) for dc in (0, 1) if (dx, dy, dc) != (0, 0, 0))
NMOD = 9
ADA_SH = NMOD * D // 4
WIN_SH = DIN // 4
WIN_PAD = 672
MSG_ROWS = 16
ANY = pl.BlockSpec(memory_space=pl.ANY)
VM = pl.BlockSpec(memory_space=pltpu.VMEM)


def _place():
    x, y, c = lax.axis_index("x"), lax.axis_index("y"), lax.axis_index("c")
    return x, y, c


class _SplitGather:
    N_SEMS = (3, 3, 3, 3, 1)

    def __init__(self, src, dst, ici_s, ici_r, d2d_s, d2d_r, lsem):
        self.src, self.dst = src, dst
        self.sems = (ici_s, ici_r, d2d_s, d2d_r)
        self.x, self.y, self.c = _place()
        self.myj = 2 * self.x + self.y
        half = src.shape[0] // 2
        self.mine = pl.ds(pl.multiple_of(self.c * half, 16), half)
        self.other = pl.ds(pl.multiple_of((1 - self.c) * half, 16), half)
        self.local = pltpu.make_async_copy(src, dst.at[self.myj], lsem.at[0])

    def _ici(self, k, slot):
        dx, dy = CHIP_RELS[k]
        return pltpu.make_async_remote_copy(
            src_ref=self.src.at[self.mine], dst_ref=self.dst.at[slot, self.mine], send_sem=self.sems[0].at[k],
            recv_sem=self.sems[1].at[k], device_id=(self.x ^ dx, self.y ^ dy, self.c), device_id_type=MESH)

    def _d2d(self, k, rows):
        dx, dy = CHIP_RELS[k]
        blk = self.dst.at[2 * (self.x ^ dx) + (self.y ^ dy), rows]
        return pltpu.make_async_remote_copy(
            src_ref=blk, dst_ref=blk, send_sem=self.sems[2].at[k], recv_sem=self.sems[3].at[k],
            device_id=(self.x, self.y, 1 - self.c), device_id_type=MESH)

    def start(self):
        self.local.start()
        for k in range(3):
            self._ici(k, self.myj).start()

    def forward(self):
        for k, (dx, dy) in enumerate(CHIP_RELS):
            self._ici(k, 2 * (self.x ^ dx) + (self.y ^ dy)).wait_recv()
            self._d2d(k, self.mine).start()

    def finish(self):
        for k in range(3):
            self._d2d(k, self.other).wait_recv()
        for k in range(3):
            self._d2d(k, self.mine).wait_send()
            self._ici(k, self.myj).wait_send()
        self.local.wait()


class _Scatter:
    def __init__(self, ins, outs, send, recv, lsem):
        self.ins, self.outs, self.send, self.recv, self.lsem = ins, outs, send, recv, lsem
        self.x, self.y, self.c = _place()
        self.myj = 2 * self.x + self.y

    def _copy(self, a, k, landing):
        dx, dy = CHIP_RELS[k]
        pj = 2 * (self.x ^ dx) + (self.y ^ dy)
        return pltpu.make_async_remote_copy(
            src_ref=self.ins[a].at[pj], dst_ref=self.outs[a].at[pj if landing else self.myj],
            send_sem=self.send.at[a, k], recv_sem=self.recv.at[a, k],
            device_id=(self.x ^ dx, self.y ^ dy, self.c), device_id_type=MESH)

    def _local(self, a):
        return pltpu.make_async_copy(self.ins[a].at[self.myj], self.outs[a].at[self.myj], self.lsem.at[a])

    def start(self):
        for a in range(len(self.ins)):
            self._local(a).start()
            for k in range(3):
                self._copy(a, k, False).start()

    def finish(self):
        for a in range(len(self.ins)):
            for k in range(3):
                self._copy(a, k, True).wait_recv()
            for k in range(3):
                self._copy(a, k, False).wait_send()
            self._local(a).wait()


def _scatter_sems(n):
    return [pltpu.SemaphoreType.DMA((n, 3)), pltpu.SemaphoreType.DMA((n, 3)), pltpu.SemaphoreType.DMA((n,))]


def _gather_sems():
    return [pltpu.SemaphoreType.DMA((k,)) for k in _SplitGather.N_SEMS]


def _comm_begin(i, plan, last):
    @pl.when(i == 0)
    def _():
        plan.start()

    if hasattr(plan, "forward"):
        @pl.when(i == max(last - 3, 0))
        def _():
            plan.forward()


def _comm_end(i, plan, last):
    @pl.when(i == last)
    def _():
        plan.finish()


def _ada_exchange(msg, w_ada, b_ada, wblock):
    def body(msg_ref, w_ref, b_ref, wb_ref, all_ref, mod_ref, wg_ref, modp, send1, recv1, send2, recv2, lsem, *gsems):
        x, y, c = _place()
        me = 4 * x + 2 * y + c
        wgather = _SplitGather(wb_ref, wg_ref, *gsems)
        wgather.start()
        own = pltpu.make_async_copy(msg_ref, all_ref.at[me], lsem.at[0])
        own.start()

        def gather(k, rel, slot):
            dx, dy, dc = rel
            return pltpu.make_async_remote_copy(
                src_ref=msg_ref, dst_ref=all_ref.at[slot], send_sem=send1.at[k], recv_sem=recv1.at[k],
                device_id=(x ^ dx, y ^ dy, c ^ dc), device_id_type=MESH)

        for k, rel in enumerate(DEV_RELS):
            gather(k, rel, me).start()
        for k, (dx, dy, dc) in enumerate(DEV_RELS):
            gather(k, (dx, dy, dc), 4 * (x ^ dx) + 2 * (y ^ dy) + (c ^ dc)).wait_recv()
        for k, rel in enumerate(DEV_RELS):
            gather(k, rel, me).wait_send()
        own.wait()

        for d in range(8):
            cv = all_ref[d, 0:8, :]
            act = cv * _sigmoid(cv)
            modp[d] = _nn(act, w_ref[...], precision=HI) + b_ref[...]

        myj = 2 * x + y
        keep = pltpu.make_async_copy(modp.at[me], mod_ref.at[myj], lsem.at[1])
        keep.start()

        def scatter(k, rel):
            dx, dy = rel
            return pltpu.make_async_remote_copy(
                src_ref=modp.at[4 * (x ^ dx) + 2 * (y ^ dy) + c], dst_ref=mod_ref.at[myj],
                send_sem=send2.at[k], recv_sem=recv2.at[k], device_id=(x ^ dx, y ^ dy, c), device_id_type=MESH)

        def landed(k, rel):
            dx, dy = rel
            return pltpu.make_async_remote_copy(
                src_ref=modp.at[me], dst_ref=mod_ref.at[2 * (x ^ dx) + (y ^ dy)],
                send_sem=send2.at[k], recv_sem=recv2.at[k], device_id=(x ^ dx, y ^ dy, c), device_id_type=MESH)

        for k, rel in enumerate(CHIP_RELS):
            scatter(k, rel).start()
        for k, rel in enumerate(CHIP_RELS):
            landed(k, rel).wait_recv()
        for k, rel in enumerate(CHIP_RELS):
            scatter(k, rel).wait_send()
        keep.wait()
        wgather.forward()
        wgather.finish()

    return pl.pallas_call(
        body, name="ada_exchange", in_specs=[VM, VM, VM, ANY], out_specs=[VM, VM, ANY],
        out_shape=[jax.ShapeDtypeStruct((8, MSG_ROWS, D), F32), jax.ShapeDtypeStruct((4, 8, ADA_SH), F32),
                   jax.ShapeDtypeStruct((4,) + wblock.shape, wblock.dtype)],
        scratch_shapes=[pltpu.VMEM((8, 8, ADA_SH), F32), pltpu.SemaphoreType.DMA((7,)), pltpu.SemaphoreType.DMA((7,)),
                        pltpu.SemaphoreType.DMA((3,)), pltpu.SemaphoreType.DMA((3,)), pltpu.SemaphoreType.DMA((2,))]
        + _gather_sems(),
        compiler_params=pltpu.CompilerParams(vmem_limit_bytes=VMEM_LIMIT),
    )(msg, w_ada, b_ada, wblock)


def _chip_exchange(parts, name):
    n = len(parts)

    def body(*refs):
        plan = _Scatter(refs[:n], refs[n:2 * n], *refs[2 * n:])
        plan.start()
        plan.finish()

    return pl.pallas_call(
        body, name=name, in_specs=[ANY] * n, out_specs=[ANY] * n,
        out_shape=[jax.ShapeDtypeStruct(p.shape, p.dtype) for p in parts], scratch_shapes=_scatter_sems(n),
    )(*parts)


def _all_exchange(parts, name):
    n = len(parts)

    def body(*refs):
        ins, outs = refs[:n], refs[n:2 * n]
        send, recv, lsem = refs[2 * n:]
        x, y, c = _place()
        me = 4 * x + 2 * y + c
        local, remote = [], []
        for a in range(n):
            local.append(pltpu.make_async_copy(ins[a], outs[a].at[me], lsem.at[a]))
            for k, (dx, dy, dc) in enumerate(DEV_RELS):
                remote.append((pltpu.make_async_remote_copy(
                    src_ref=ins[a], dst_ref=outs[a].at[me], send_sem=send.at[a, k], recv_sem=recv.at[a, k],
                    device_id=(x ^ dx, y ^ dy, c ^ dc), device_id_type=MESH), a, k,
                    4 * (x ^ dx) + 2 * (y ^ dy) + (c ^ dc)))
        for cp in local:
            cp.start()
        for cp, _, _, _ in remote:
            cp.start()
        for _, a, k, pd in remote:
            pltpu.make_async_remote_copy(
                src_ref=ins[a], dst_ref=outs[a].at[pd], send_sem=send.at[a, k], recv_sem=recv.at[a, k],
                device_id=(x, y, c), device_id_type=MESH).wait_recv()
        for cp, _, _, _ in remote:
            cp.wait_send()
        for cp in local:
            cp.wait()

    shapes = [jax.ShapeDtypeStruct((8,) + tuple(p.shape), p.dtype) for p in parts]
    return pl.pallas_call(
        body, name=name, in_specs=[ANY] * n, out_specs=[ANY] * n, out_shape=shapes,
        scratch_shapes=[pltpu.SemaphoreType.DMA((n, 7)), pltpu.SemaphoreType.DMA((n, 7)), pltpu.SemaphoreType.DMA((n,))],
    )(*parts)


def _pair_exchange(parts, name):
    n = len(parts)

    def body(*refs):
        ins, outs = refs[:n], refs[n:2 * n]
        send, recv = refs[2 * n:]
        x, y, c = _place()
        cps = [pltpu.make_async_remote_copy(
            src_ref=ins[a], dst_ref=outs[a], send_sem=send.at[a], recv_sem=recv.at[a],
            device_id=(x, y, 1 - c), device_id_type=MESH) for a in range(n)]
        for cp in cps:
            cp.start()
        for cp in cps:
            cp.wait_recv()
        for cp in cps:
            cp.wait_send()

    shapes = [jax.ShapeDtypeStruct(p.shape, p.dtype) for p in parts]
    return pl.pallas_call(
        body, name=name, in_specs=[ANY] * n, out_specs=[ANY] * n, out_shape=shapes,
        scratch_shapes=[pltpu.SemaphoreType.DMA((n,)), pltpu.SemaphoreType.DMA((n,))],
    )(*parts)


def _row_tile(rows, cap):
    best = rows
    for t in range(8, min(cap, rows) + 1, 8):
        if rows % t == 0:
            best = t
    return best if rows % 8 == 0 else rows


def _sum_slots(parts, name):
    n, rows, width = parts.shape
    tr = _row_tile(rows, 352)

    def body(p_ref, o_ref):
        acc = p_ref[0].astype(F32)
        for j in range(1, n):
            acc = acc + p_ref[j].astype(F32)
        o_ref[...] = acc

    return pl.pallas_call(
        body, grid=(rows // tr,), name=name,
        in_specs=[pl.BlockSpec((n, tr, width), lambda i: (0, i, 0))],
        out_specs=pl.BlockSpec((tr, width), lambda i: (i, 0)),
        out_shape=jax.ShapeDtypeStruct((rows, width), F32),
        compiler_params=_cparams(),
    )(parts)


def _adamw_math(g, w, m, v):
    m_new = ADAM_B1 * m + (1.0 - ADAM_B1) * g
    v_new = ADAM_B2 * v + (1.0 - ADAM_B2) * (g * g)
    m_hat = m_new / (1.0 - ADAM_B1 ** ADAM_STEP)
    v_hat = v_new / (1.0 - ADAM_B2 ** ADAM_STEP)
    delta = -ADAM_LR * (m_hat / (jnp.sqrt(v_hat) + ADAM_EPS) + ADAM_WD * w)
    return delta, m_new, v_new


def _adamw(grads, w, m, v, name):
    rows, width = w.shape
    tr = _row_tile(rows, 256 if width <= 1024 else 128)
    ng = len(grads)

    def body(*refs):
        g = refs[0][...]
        for r in refs[1:ng]:
            g = g + r[...]
        w_ref, m_ref, v_ref, g_out, d_out, m_out, v_out = refs[ng:]
        delta, m_new, v_new = _adamw_math(g, w_ref[...], m_ref[...], v_ref[...])
        g_out[...] = g
        d_out[...] = delta
        m_out[...] = m_new
        v_out[...] = v_new

    blk = pl.BlockSpec((tr, width), lambda i: (i, 0))
    return pl.pallas_call(
        body, grid=(rows // tr,), name=name,
        in_specs=[blk] * (ng + 3), out_specs=[blk] * 4,
        out_shape=[jax.ShapeDtypeStruct((rows, width), F32)] * 4,
        compiler_params=_cparams(),
    )(*grads, w, m, v)


def _adamw_ada(msgs, dmods, w, m, v):
    rows, width = w.shape
    tr = 128

    def body(c_ref, dm_ref, w_ref, m_ref, v_ref, g_out, d_out, m_out, v_out):
        cv = jnp.concatenate([c_ref[d, 0:1, :] for d in range(8)], axis=0)
        act = cv * _sigmoid(cv)
        g = _tn(act, dm_ref[...], precision=HI)
        delta, m_new, v_new = _adamw_math(g, w_ref[...], m_ref[...], v_ref[...])
        g_out[...] = g
        d_out[...] = delta
        m_out[...] = m_new
        v_out[...] = v_new

    blk = pl.BlockSpec((tr, width), lambda i: (i, 0))
    return pl.pallas_call(
        body, grid=(rows // tr,), name="adamw_w_ada",
        in_specs=[pl.BlockSpec((8, MSG_ROWS, tr), lambda i: (0, 0, i)), pl.BlockSpec((8, width), lambda i: (0, 0)),
                  blk, blk, blk],
        out_specs=[blk] * 4, out_shape=[jax.ShapeDtypeStruct((rows, width), F32)] * 4,
        compiler_params=_cparams(),
    )(msgs, dmods, w, m, v)


def _adamw_small(parts, w, m, v, name):
    n, rows, width = parts.shape

    def body(p_ref, w_ref, m_ref, v_ref, g_out, d_out, m_out, v_out):
        g = p_ref[0]
        for j in range(1, n):
            g = g + p_ref[j]
        delta, m_new, v_new = _adamw_math(g, w_ref[...], m_ref[...], v_ref[...])
        g_out[...] = g
        d_out[...] = delta
        m_out[...] = m_new
        v_out[...] = v_new

    return pl.pallas_call(
        body, name=name, in_specs=[VM] * 4, out_specs=[VM] * 4,
        out_shape=[jax.ShapeDtypeStruct((rows, width), F32)] * 4,
        compiler_params=pltpu.CompilerParams(vmem_limit_bytes=VMEM_LIMIT),
    )(parts, w, m, v)


SMALL_ROWS = 24


def _pad_row(vec, width=D):
    vec = vec.reshape(1, -1)
    return jnp.pad(vec, ((0, 0), (0, width - vec.shape[1])))


def _lanes_4_7(vec4):
    return jnp.zeros((1, 128), F32).at[0, NH:2 * NH].set(vec4.reshape(NH))


def kernel(x, c, w_ada, b_ada, norm_ffn1, ffn1_gate, ffn1_up, ffn1_down, norm_mix, w_in, conv_w, a_log, dt_bias, gdn_norm, pool_w, pool_scale, w_out, norm_ffn2, ffn2_gate, ffn2_up, ffn2_down, final_norm, loss_target, m_w_ada, m_b_ada, m_norm_ffn1, m_ffn1_gate, m_ffn1_up, m_ffn1_down, m_norm_mix, m_w_in, m_conv_w, m_a_log, m_dt_bias, m_gdn_norm, m_pool_w, m_pool_scale, m_w_out, m_norm_ffn2, m_ffn2_gate, m_ffn2_up, m_ffn2_down, m_final_norm, v_w_ada, v_b_ada, v_norm_ffn1, v_ffn1_gate, v_ffn1_up, v_ffn1_down, v_norm_mix, v_w_in, v_conv_w, v_a_log, v_dt_bias, v_gdn_norm, v_pool_w, v_pool_scale, v_w_out, v_norm_ffn2, v_ffn2_gate, v_ffn2_up, v_ffn2_down, v_final_norm):
    xs = x[0]
    tgt = loss_target[0]
    chip = 2 * lax.axis_index("x") + lax.axis_index("y")
    me = 2 * chip + lax.axis_index("c")

    fsh = FF // 4
    block_a = jnp.concatenate([ffn1_gate[0].T, ffn1_up[0].T, ffn1_down[0]], axis=0).astype(BF)
    block_b = jnp.concatenate([ffn2_gate[0].T, ffn2_up[0].T, ffn2_down[0], w_out[0],
                               jnp.pad(w_in[0].T, ((0, WIN_PAD - WIN_SH), (0, 0)))], axis=0).astype(BF)

    msg = jnp.concatenate([jnp.broadcast_to(c, (8, D)), jnp.pad(conv_w[0], ((0, 0), (0, D - 3 * GW // 4))),
                           jnp.zeros((MSG_ROWS - 12, D), F32)], axis=0)
    b_sh = lax.dynamic_slice(b_ada, (0, chip * ADA_SH), (1, ADA_SH))
    msgs, mod4, gath_a = _ada_exchange(msg, w_ada[0], b_sh, block_a)
    mod = mod4[:, 0, :].reshape(NMOD, D)
    mrow = [mod[i:i + 1] for i in range(NMOD)]
    conv_full = jnp.concatenate([msgs[2 * j, 8:12, :3 * GW // 4] for j in range(4)], axis=1)
    alog, dtb = _lanes_4_7(a_log), _lanes_4_7(dt_bias)
    gnm = gdn_norm.reshape(1, DH)
    pwb = pool_w[0].astype(BF)
    psc = pool_scale.reshape(1, PW)
    fin = final_norm.reshape(1, D)

    x1, f1, a1, b1, gath_b = _ffn_fwd(xs, mrow[0], mrow[1], mrow[2], norm_ffn1, gath_a, 0, "ffn1_fwd", block_b)
    wo = gath_b[:, 3 * fsh:3 * fsh + D // 4, :].reshape(D, D)
    win_nat = gath_b[:, 3 * fsh + D // 4:3 * fsh + D // 4 + WIN_SH, :].reshape(DIN, D)
    winT = jnp.concatenate([win_nat[:4 * GW], win_nat[4 * GW + 2 * NH:], win_nat[4 * GW:4 * GW + 2 * NH],
                            jnp.zeros((128 - 2 * NH, D), BF)], axis=0)
    xq, ba, qn, kn, vv, z, pp, bg, gc = _mix_proj(x1, mrow[3], mrow[4], norm_mix, winT, conv_full, alog, dtb)
    x2, mixed, cat, o, sall = _mix_core(x1, mrow[5], qn, kn, vv, z, pp, bg, gc, gnm, pwb, psc, wo)
    x3, f2, a2, b2 = _ffn_fwd(x2, mrow[6], mrow[7], mrow[8], norm_ffn2, gath_b, 0, "ffn2_fwd")
    lpart, dx3, dfin = _loss_head(x3, tgt, fin)
    loss = lax.psum(jnp.sum(lpart), ("x", "y", "c"))

    slots = lambda t: t.reshape(4, t.shape[0] // 4, D)
    dx2, da2, db2, s2, h2, df2, dsh3, dsc3, dgt3, dn3 = _ffn_dgrad(
        dx3, x2, f2, a2, b2, mrow[6], mrow[7], mrow[8], norm_ffn2, gath_b, 0, "ffn2_dgrad")
    gg2, gu2, gd2 = _ffn_wgrad(da2, db2, s2, h2, df2, "ffn2_wgrad")
    dqn, dkn, dvv, dz, dpp, dbg, dgt2, dwo, dpw, dps, dgnm, *landed2 = _mix_core_bwd(
        dx2, mrow[5], mixed, cat, o, sall, qn, kn, vv, z, pp, bg, gc, gnm, pwb, psc, wo,
        scatter=[slots(gg2), slots(gu2), slots(gd2)])
    dx1, dwin, dcw, dal, ddt, dsh2, dsc2, dn2 = _mix_proj_bwd(
        dx2, dqn, dkn, dvv, dz, dpp, dbg, xq, ba, x1, mrow[3], mrow[4], norm_mix, winT, conv_full, alog, dtb)
    dwin_nat = jnp.concatenate([dwin[:4 * GW], dwin[4 * GW + PW:4 * GW + PW + 2 * NH], dwin[4 * GW:4 * GW + PW]], axis=0)
    dwin_sl = jnp.pad(dwin_nat.reshape(4, WIN_SH, D), ((0, 0), (0, WIN_PAD - WIN_SH), (0, 0)))
    dx0, da1, db1, s1, h1, df1, dsh1, dsc1, dgt1, dn1, *landed_mix = _ffn_dgrad(
        dx1, xs, f1, a1, b1, mrow[0], mrow[1], mrow[2], norm_ffn1, gath_a, 0, "ffn1_dgrad",
        scatter=[slots(dwo), dwin_sl])
    gg1, gu1, gd1 = _ffn_wgrad(da1, db1, s1, h1, df1, "ffn1_wgrad")
    landed1 = _chip_exchange([slots(gg1), slots(gu1), slots(gd1)], "grad_scatter")

    landed = list(landed1) + list(landed2) + list(landed_mix)
    names = ("g1", "u1", "d1", "g2", "u2", "d2", "wo", "win")
    psum = [_sum_slots(t, "sum_" + nm) for t, nm in zip(landed, names)]
    qsum = _pair_exchange(psum, "grad_pair")

    red = lambda t: jnp.sum(t, axis=0, keepdims=True)
    small = jnp.concatenate(
        [red(dn1), red(dn2), red(dn3), red(dfin),
         red(dsh1), red(dsc1), red(dgt1), red(dsh2), red(dsc2), red(dgt2), red(dsh3), red(dsc3), red(dgt3),
         _pad_row(red(dps)), _pad_row(red(dgnm)), _pad_row(red(dal)), _pad_row(red(ddt)),
         jnp.sum(dcw, axis=1).reshape(6, D), jnp.zeros((1, D), F32)], axis=0)
    small_all, dpw_all = _all_exchange([small, dpw.reshape(NG * 128, 128)], "small_gather")

    def unT(t, cols):
        return t[:cols].T

    upd = {}
    upd["ffn1_gate"] = _adamw([unT(psum[0], fsh), unT(qsum[0], fsh)], ffn1_gate[0], m_ffn1_gate[0], v_ffn1_gate[0], "adamw_g1")
    upd["ffn1_up"] = _adamw([unT(psum[1], fsh), unT(qsum[1], fsh)], ffn1_up[0], m_ffn1_up[0], v_ffn1_up[0], "adamw_u1")
    upd["ffn1_down"] = _adamw([psum[2], qsum[2]], ffn1_down[0], m_ffn1_down[0], v_ffn1_down[0], "adamw_d1")
    upd["ffn2_gate"] = _adamw([unT(psum[3], fsh), unT(qsum[3], fsh)], ffn2_gate[0], m_ffn2_gate[0], v_ffn2_gate[0], "adamw_g2")
    upd["ffn2_up"] = _adamw([unT(psum[4], fsh), unT(qsum[4], fsh)], ffn2_up[0], m_ffn2_up[0], v_ffn2_up[0], "adamw_u2")
    upd["ffn2_down"] = _adamw([psum[5], qsum[5]], ffn2_down[0], m_ffn2_down[0], v_ffn2_down[0], "adamw_d2")
    upd["w_out"] = _adamw([psum[6], qsum[6]], w_out[0], m_w_out[0], v_w_out[0], "adamw_wo")
    upd["w_in"] = _adamw([unT(psum[7], WIN_SH), unT(qsum[7], WIN_SH)], w_in[0], m_w_in[0], v_w_in[0], "adamw_win")
    dmods = lax.dynamic_slice(small_all[:, 4:4 + NMOD, :].reshape(8, NMOD * D), (0, chip * ADA_SH), (8, ADA_SH))
    upd["w_ada"] = _adamw_ada(msgs, dmods, w_ada[0], m_w_ada[0], v_w_ada[0])

    def pack_small(nf1, nmx, nf2, fn, bada, psc_, gn_, al_, dt_):
        return jnp.concatenate(
            [nf1.reshape(1, D), nmx.reshape(1, D), nf2.reshape(1, D), fn.reshape(1, D), bada.reshape(NMOD, D),
             _pad_row(psc_), _pad_row(gn_), _pad_row(_lanes_4_7(al_)), _pad_row(_lanes_4_7(dt_)),
             jnp.zeros((7, D), F32)], axis=0)

    ws = pack_small(norm_ffn1, norm_mix, norm_ffn2, final_norm, b_ada, pool_scale, gdn_norm, a_log, dt_bias)
    ms = pack_small(m_norm_ffn1, m_norm_mix, m_norm_ffn2, m_final_norm, m_b_ada, m_pool_scale, m_gdn_norm, m_a_log, m_dt_bias)
    vs = pack_small(v_norm_ffn1, v_norm_mix, v_norm_ffn2, v_final_norm, v_b_ada, v_pool_scale, v_gdn_norm, v_a_log, v_dt_bias)
    sm = _adamw_small(small_all, ws, ms, vs, "adamw_small")
    pw2 = lambda t: t.reshape(NG * 128, 128)
    upd_pw = _adamw_small(dpw_all, pw2(pool_w), pw2(m_pool_w), pw2(v_pool_w), "adamw_pool_w")
    csh = 3 * GW // 4
    gconv = lax.dynamic_slice(sm[0][17:23].reshape(4, 3 * GW), (0, chip * csh), (4, csh))
    upd["conv_w"] = _adamw([gconv], conv_w[0], m_conv_w[0], v_conv_w[0], "adamw_conv")

    def small_out(k):
        t = sm[k]
        return {
            "norm_ffn1": t[0:1], "norm_mix": t[1:2], "norm_ffn2": t[2:3], "final_norm": t[3],
            "b_ada": t[4:4 + NMOD].reshape(1, NMOD * D), "pool_scale": t[13:14, :PW], "gdn_norm": t[14:15, :DH],
            "a_log": t[15:16, NH:2 * NH], "dt_bias": t[16:17, NH:2 * NH],
        }

    order = ["w_ada", "b_ada", "norm_ffn1", "ffn1_gate", "ffn1_up", "ffn1_down", "norm_mix", "w_in", "conv_w", "a_log",
             "dt_bias", "gdn_norm", "pool_w", "pool_scale", "w_out", "norm_ffn2", "ffn2_gate", "ffn2_up", "ffn2_down",
             "final_norm"]
    outs = [loss, dx0[None]]
    for k in range(4):
        smk = small_out(k)
        for nm in order:
            if nm in upd:
                outs.append(upd[nm][k][None])
            elif nm == "pool_w":
                outs.append(upd_pw[k].reshape(1, NG, 128, 128))
            else:
                outs.append(smk[nm])
    return tuple(outs)
```

```python
import functools

import jax
import jax.numpy as jnp
from jax import lax
from jax.experimental import pallas as pl
from jax.experimental.pallas import tpu as pltpu

F32 = jnp.float32
BF = jnp.bfloat16

D = 1024
FF = 2816
FH = FF // 2
NH = 4
DH = 128
GW = NH * DH
CH = 64
PW = 512
NG = 4
POOL_WINDOWS = (2, 4, 8, 16)
HALO = 16
DIN = 4 * GW + 2 * NH + PW
DINP = 3 * GW + GW + PW + 128
EPS = 1e-6
ADAM_LR, ADAM_B1, ADAM_B2, ADAM_EPS, ADAM_WD, ADAM_STEP = 0.001, 0.9, 0.999, 1e-08, 0.01, 10

VMEM_LIMIT = 56 * 1024 * 1024

NT_DIMS = (((1,), (1,)), ((), ()))
TN_DIMS = (((0,), (0,)), ((), ()))
HI = lax.Precision.HIGHEST


def _nt(a, b, **kw):
    return lax.dot_general(a, b, NT_DIMS, preferred_element_type=F32, **kw)


def _tn(a, b, **kw):
    return lax.dot_general(a, b, TN_DIMS, preferred_element_type=F32, **kw)


def _nn(a, b, **kw):
    return jnp.dot(a, b, preferred_element_type=F32, **kw)


def _cparams(sem=("arbitrary",), **kw):
    return pltpu.CompilerParams(dimension_semantics=sem, vmem_limit_bytes=VMEM_LIMIT, **kw)


def _const_spec(shape):
    nd = len(shape)
    return pl.BlockSpec(shape, lambda *_: (0,) * nd, pipeline_mode=pl.Buffered(1))


def _row_spec(tm, width):
    return pl.BlockSpec((tm, width), lambda i: (i, 0))


def _sum8(v):
    return jnp.sum(v.reshape(v.shape[0] // 8, 8, v.shape[1]), axis=0)


def _sigmoid(v):
    return 1.0 / (1.0 + jnp.exp(-v))


def _tile(T, cap=512):
    return min(cap, T)


def _norm_mod_fwd(xv, gain, shift, scale):
    r = lax.rsqrt(jnp.mean(xv * xv, axis=-1, keepdims=True) + EPS)
    n = xv * r
    y = n * gain
    return n, r, y, y * (1.0 + scale) + shift


def _norm_mod_bwd(dh, n, r, y, gain, scale):
    dy = dh * (1.0 + scale)
    dn = dy * gain
    dx = r * (dn - n * jnp.mean(dn * n, axis=-1, keepdims=True))
    return dx, _sum8(dh), _sum8(dh * y), _sum8(dy * n)


def _ffn_wspecs(k0):
    return [pl.BlockSpec((4, FF // 4, D), lambda i, k=k0 + n: (0, k, 0), pipeline_mode=pl.Buffered(1)) for n in range(3)]


def _half(w_ref, j):
    return w_ref[2 * j:2 * j + 2].reshape(FH, D)


def _ffn_fwd(x, shift, scale, gate, gain, wall, k0, name, gather_block=None):
    T = x.shape[0]
    tm = _tile(T)
    nt = T // tm
    comm = gather_block is not None

    def body(*refs):
        x_ref, sh_ref, sc_ref, gt_ref, gn_ref, wg_ref, wu_ref, wd_ref = refs[:8]
        if comm:
            xo_ref, f_ref, a_ref, b_ref = refs[9:13]
            plan = _SplitGather(refs[8], refs[13], *refs[14:])
            _comm_begin(pl.program_id(0), plan, nt - 1)
        else:
            xo_ref, f_ref, a_ref, b_ref = refs[8:12]
        xv = x_ref[...]
        _, _, _, h = _norm_mod_fwd(xv, gn_ref[...], sh_ref[...], sc_ref[...])
        hb = h.astype(BF)
        facc = jnp.zeros((tm, D), F32)
        for j in range(2):
            a = _nt(hb, _half(wg_ref, j))
            b = _nt(hb, _half(wu_ref, j))
            a_ref[:, j * FH:(j + 1) * FH] = a.astype(BF)
            b_ref[:, j * FH:(j + 1) * FH] = b.astype(BF)
            s = (a * _sigmoid(a) * b).astype(BF)
            facc = facc + _nn(s, _half(wd_ref, j))
        f_ref[...] = facc
        xo_ref[...] = xv + 0.5 * gt_ref[...] * facc
        if comm:
            _comm_end(pl.program_id(0), plan, nt - 1)

    vec = _const_spec((1, D))
    extra_in = [gather_block] if comm else []
    return pl.pallas_call(
        body, grid=(nt,), name=name,
        in_specs=[_row_spec(tm, D), vec, vec, vec, vec] + _ffn_wspecs(k0) + [ANY] * comm,
        out_specs=[_row_spec(tm, D), _row_spec(tm, D), _row_spec(tm, FF), _row_spec(tm, FF)] + [ANY] * comm,
        out_shape=[jax.ShapeDtypeStruct((T, D), F32), jax.ShapeDtypeStruct((T, D), F32),
                   jax.ShapeDtypeStruct((T, FF), BF), jax.ShapeDtypeStruct((T, FF), BF)]
        + ([jax.ShapeDtypeStruct((4,) + gather_block.shape, gather_block.dtype)] if comm else []),
        scratch_shapes=_gather_sems() if comm else [],
        compiler_params=_cparams(),
    )(x, shift, scale, gate, gain, wall, wall, wall, *extra_in)


def _ffn_dgrad(dxo, x, f, a, b, shift, scale, gate, gain, wall, k0, name, scatter=()):
    T = x.shape[0]
    tm = _tile(T, 256)
    nt = T // tm
    ns = len(scatter)

    def body(*refs):
        dxo_ref, x_ref, f_ref, a_ref, b_ref, sh_ref, sc_ref, gt_ref, gn_ref, wg_ref, wu_ref, wd_ref = refs[:12]
        dx_ref, da_ref, db_ref, s_ref, h_ref, df_ref, dsh_ref, dsc_ref, dgt_ref, dgn_ref = refs[12 + ns:22 + ns]
        i = pl.program_id(0)
        if ns:
            plan = _Scatter(refs[12:12 + ns], refs[22 + ns:22 + 2 * ns], *refs[22 + 2 * ns:])
            _comm_begin(i, plan, nt - 1)
        dxo_v = dxo_ref[...]
        xv = x_ref[...]
        gain_v, scale_v = gn_ref[...], sc_ref[...]
        n, r, y, h = _norm_mod_fwd(xv, gain_v, sh_ref[...], scale_v)
        h_ref[...] = h.astype(BF)
        dgate = _sum8(0.5 * f_ref[...] * dxo_v)
        dfb = (0.5 * gt_ref[...] * dxo_v).astype(BF)
        df_ref[...] = dfb
        dh = jnp.zeros((tm, D), F32)
        for j in range(2):
            cols = slice(j * FH, (j + 1) * FH)
            ds = _nt(dfb, _half(wd_ref, j))
            av = a_ref[:, cols].astype(F32)
            bv = b_ref[:, cols].astype(F32)
            sig = _sigmoid(av)
            sa = av * sig
            da = (ds * bv * (sig * (1.0 + av * (1.0 - sig)))).astype(BF)
            db = (ds * sa).astype(BF)
            da_ref[:, cols] = da
            db_ref[:, cols] = db
            s_ref[:, cols] = (sa * bv).astype(BF)
            dh = dh + _nn(da, _half(wg_ref, j)) + _nn(db, _half(wu_ref, j))
        dxn, dsh, dsc, dgn = _norm_mod_bwd(dh, n, r, y, gain_v, scale_v)
        dx_ref[...] = dxo_v + dxn

        @pl.when(i == 0)
        def _():
            dsh_ref[...] = dsh
            dsc_ref[...] = dsc
            dgt_ref[...] = dgate
            dgn_ref[...] = dgn

        @pl.when(i > 0)
        def _():
            dsh_ref[...] += dsh
            dsc_ref[...] += dsc
            dgt_ref[...] += dgate
            dgn_ref[...] += dgn

        if ns:
            _comm_end(i, plan, nt - 1)

    vec = _const_spec((1, D))
    acc = pl.BlockSpec((8, D), lambda i: (0, 0))
    accs = jax.ShapeDtypeStruct((8, D), F32)
    return pl.pallas_call(
        body, grid=(nt,), name=name,
        in_specs=[_row_spec(tm, D), _row_spec(tm, D), _row_spec(tm, D), _row_spec(tm, FF), _row_spec(tm, FF),
                  vec, vec, vec, vec] + _ffn_wspecs(k0) + [ANY] * ns,
        out_specs=[_row_spec(tm, D), _row_spec(tm, FF), _row_spec(tm, FF), _row_spec(tm, FF),
                   _row_spec(tm, D), _row_spec(tm, D), acc, acc, acc, acc] + [ANY] * ns,
        out_shape=[jax.ShapeDtypeStruct((T, D), F32), jax.ShapeDtypeStruct((T, FF), BF),
                   jax.ShapeDtypeStruct((T, FF), BF), jax.ShapeDtypeStruct((T, FF), BF),
                   jax.ShapeDtypeStruct((T, D), BF), jax.ShapeDtypeStruct((T, D), BF), accs, accs, accs, accs]
        + [jax.ShapeDtypeStruct(t.shape, t.dtype) for t in scatter],
        scratch_shapes=_scatter_sems(ns) if ns else [],
        compiler_params=_cparams(),
    )(dxo, x, f, a, b, shift, scale, gate, gain, wall, wall, wall, *scatter)


def _ffn_wgrad(da, db, s, h, df, name, scatter=()):
    T = h.shape[0]
    tk = _tile(T, 512)
    nk = T // tk
    ns = len(scatter)

    def body(*refs):
        da_ref, db_ref, s_ref, h_ref, df_ref = refs[:5]
        og_ref, ou_ref, od_ref = refs[5 + ns:8 + ns]
        ag, au, ad = refs[8 + 2 * ns:11 + 2 * ns]
        k = pl.program_id(1)
        step = pl.program_id(0) * nk + k
        if ns:
            plan = _Scatter(refs[5:5 + ns], refs[8 + ns:8 + 2 * ns], *refs[11 + 2 * ns:])
            _comm_begin(step, plan, 2 * nk - 1)
        @pl.when(k == 0)
        def _():
            for acc in (ag, au, ad):
                acc[...] = jnp.zeros((FH, D), F32)

        ag[...] += _tn(da_ref[...], h_ref[...])
        au[...] += _tn(db_ref[...], h_ref[...])
        ad[...] += _tn(s_ref[...], df_ref[...])

        @pl.when(k == nk - 1)
        def _():
            og_ref[...] = ag[...].astype(BF)
            ou_ref[...] = au[...].astype(BF)
            od_ref[...] = ad[...].astype(BF)

        if ns:
            _comm_end(step, plan, 2 * nk - 1)

    colblk = pl.BlockSpec((tk, FH), lambda j, k: (k, j))
    rowblk = pl.BlockSpec((tk, D), lambda j, k: (k, 0))
    outblk = pl.BlockSpec((FH, D), lambda j, k: (j, 0))
    outs = jax.ShapeDtypeStruct((FF, D), BF)
    return pl.pallas_call(
        body, grid=(2, nk), name=name,
        in_specs=[colblk, colblk, colblk, rowblk, rowblk] + [ANY] * ns,
        out_specs=[outblk, outblk, outblk] + [ANY] * ns,
        out_shape=[outs, outs, outs] + [jax.ShapeDtypeStruct(t.shape, t.dtype) for t in scatter],
        scratch_shapes=[pltpu.VMEM((FH, D), F32)] * 3 + (_scatter_sems(ns) if ns else []),
        compiler_params=_cparams(("arbitrary", "arbitrary")),
    )(da, db, s, h, df, *scatter)


def _loss_head(x, target, gain):
    T = x.shape[0]
    tm = _tile(T)

    def body(x_ref, t_ref, gn_ref, ls_ref, dx_ref, dgn_ref):
        i = pl.program_id(0)
        xv = x_ref[...]
        gain_v = gn_ref[...]
        r = lax.rsqrt(jnp.mean(xv * xv, axis=-1, keepdims=True) + EPS)
        n = xv * r
        err = n * gain_v - t_ref[...]
        e2 = err * err
        part = e2[:, 0:128]
        for q in range(1, D // 128):
            part = part + e2[:, q * 128:(q + 1) * 128]
        lsum = _sum8(part) * (0.5 / D)
        dy = err * (1.0 / D)
        dn = dy * gain_v
        dx_ref[...] = r * (dn - n * jnp.mean(dn * n, axis=-1, keepdims=True))
        dgn = _sum8(dy * n)

        @pl.when(i == 0)
        def _():
            ls_ref[...] = lsum
            dgn_ref[...] = dgn

        @pl.when(i > 0)
        def _():
            ls_ref[...] += lsum
            dgn_ref[...] += dgn

    return pl.pallas_call(
        body, grid=(T // tm,), name="loss_head",
        in_specs=[_row_spec(tm, D), _row_spec(tm, D), _const_spec((1, D))],
        out_specs=[pl.BlockSpec((8, 128), lambda i: (0, 0)), _row_spec(tm, D), pl.BlockSpec((8, D), lambda i: (0, 0))],
        out_shape=[jax.ShapeDtypeStruct((8, 128), F32), jax.ShapeDtypeStruct((T, D), F32),
                   jax.ShapeDtypeStruct((8, D), F32)],
        compiler_params=_cparams(),
    )(x, target, gain)


def _seg_cumsum(v, row_in_chunk, reverse=False):
    n = v.shape[0]
    s = 1
    while s < CH:
        if reverse:
            moved = pltpu.roll(v, n - s, 0)
            ok = row_in_chunk < CH - s
        else:
            moved = pltpu.roll(v, s, 0)
            ok = row_in_chunk >= s
        v = v + jnp.where(ok, moved, 0.0)
        s *= 2
    return v


def _conv_silu(xq_ext_ref, cw_ref, tm):
    c = cw_ref[0:1, :] * xq_ext_ref[pl.ds(5, tm), :]
    for j in range(1, 4):
        c = c + cw_ref[j:j + 1, :] * xq_ext_ref[pl.ds(5 + j, tm), :]
    return c, _sigmoid(c)


def _gates(ba, alog, dtb, lane):
    beta = _sigmoid(ba)
    arg = ba + dtb
    softplus = jnp.maximum(arg, 0.0) + jnp.log(1.0 + jnp.exp(-jnp.abs(arg)))
    g = -jnp.exp(alog) * softplus
    return jnp.where(lane < NH, beta, 0.0), jnp.where((lane >= NH) & (lane < 2 * NH), g, 0.0), _sigmoid(arg)


def _mix_proj(x, shift, scale, gain, winT, conv_w, alog, dtb):
    T = x.shape[0]
    tm = _tile(T)

    def body(x_ref, sh_ref, sc_ref, gn_ref, w_ref, cw_ref, al_ref, dt_ref,
             xq_ref, ba_ref, qn_ref, kn_ref, v_ref, z_ref, p_ref, bg_ref, gc_ref, ext):
        i = pl.program_id(0)
        _, _, _, h = _norm_mod_fwd(x_ref[...], gn_ref[...], sh_ref[...], sc_ref[...])
        hb = h.astype(BF)

        @pl.when(i == 0)
        def _():
            ext[pl.ds(0, 8), :] = jnp.zeros((8, 3 * GW), F32)

        xq = _nt(hb, w_ref[pl.ds(0, 3 * GW), :])
        xq_ref[...] = xq
        ext[pl.ds(8, tm), :] = xq
        z_ref[...] = _nt(hb, w_ref[pl.ds(3 * GW, GW), :])
        p_ref[...] = _nt(hb, w_ref[pl.ds(4 * GW, PW), :])
        ba = _nt(hb, w_ref[pl.ds(4 * GW + PW, 128), :])
        ba_ref[...] = ba

        c, sg = _conv_silu(ext, cw_ref, tm)
        ext[pl.ds(0, 8), :] = ext[pl.ds(tm, 8), :]
        qt = c * sg
        for hd in range(NH):
            cq = slice(hd * DH, (hd + 1) * DH)
            ck = slice(GW + hd * DH, GW + (hd + 1) * DH)
            qh, kh = qt[:, cq], qt[:, ck]
            qn_ref[:, cq] = qh * (lax.rsqrt(jnp.sum(qh * qh, axis=-1, keepdims=True) + EPS) * DH ** -0.5)
            kn_ref[:, cq] = kh * lax.rsqrt(jnp.sum(kh * kh, axis=-1, keepdims=True) + EPS)
        v_ref[...] = qt[:, 2 * GW:3 * GW]

        lane = lax.broadcasted_iota(jnp.int32, (tm, 128), 1)
        row = lax.broadcasted_iota(jnp.int32, (tm, 128), 0) % CH
        beta, g, _ = _gates(ba, al_ref[...], dt_ref[...], lane)
        bg_ref[...] = beta + g
        gc_ref[...] = _seg_cumsum(g, row)

    wide = lambda w: _row_spec(tm, w)
    shp = lambda w: jax.ShapeDtypeStruct((T, w), F32)
    return pl.pallas_call(
        body, grid=(T // tm,), name="mix_proj",
        in_specs=[wide(D), _const_spec((1, D)), _const_spec((1, D)), _const_spec((1, D)), _const_spec((DINP, D)),
                  _const_spec((4, 3 * GW)), _const_spec((1, 128)), _const_spec((1, 128))],
        out_specs=[wide(3 * GW), wide(128), wide(GW), wide(GW), wide(GW), wide(GW), wide(PW), wide(128), wide(128)],
        out_shape=[shp(3 * GW), shp(128), shp(GW), shp(GW), shp(GW), shp(GW), shp(PW), shp(128), shp(128)],
        scratch_shapes=[pltpu.VMEM((tm + 8, 3 * GW), F32)],
        compiler_params=_cparams(),
    )(x, shift, scale, gain, winT, conv_w, alog, dtb)


R2 = 2 * CH
TRI_PREC = None


def _tmm(fn, a, b):
    if TRI_PREC is None:
        return fn(a.astype(BF), b.astype(BF))
    return fn(a, b, precision=TRI_PREC)


def _pair_consts():
    ii = lax.broadcasted_iota(jnp.int32, (R2, R2), 0)
    jj = lax.broadcasted_iota(jnp.int32, (R2, R2), 1)
    same = (ii < CH) == (jj < CH)
    r = lax.broadcasted_iota(jnp.int32, (R2, 1), 0)
    return dict(causal=same & (ii >= jj), strict=same & (ii > jj), eye=(ii == jj).astype(F32), rowA=r < CH,
                last=(r == CH - 1) | (r == R2 - 1),
                rowS=lax.broadcasted_iota(jnp.int32, (2 * DH, 1), 0) < DH)


def _tri_inverse(m, eye):
    neg = -m
    t = eye + neg
    pw = neg
    for _ in range(5):
        pw = _tmm(_nn, pw, pw)
        t = _tmm(_nn, t, eye + pw)
    return t


def _pair_forward(q, k, v, beta, gcv, gl, S, cn):
    causal, rowA = cn["causal"], cn["rowA"]
    gc_b = jnp.broadcast_to(gcv, (R2, R2))
    gam = jnp.where(causal, jnp.exp(jnp.where(causal, gc_b - gc_b.T, 0.0)), 0.0)
    kb = k * beta
    vb = v * beta
    kbf = k.astype(BF)
    P = _nt(kb.astype(BF), kbf)
    QK = _nt(q.astype(BF), kbf)
    M = jnp.where(cn["strict"], P * gam, 0.0)
    Tm = _tri_inverse(M, cn["eye"])
    E = jnp.exp(gcv)
    Fd = jnp.exp(gl - gcv)
    egl = jnp.exp(gl)
    egl_st = jnp.concatenate([jnp.broadcast_to(egl[0:1], (DH, 1)), jnp.broadcast_to(egl[CH:CH + 1], (DH, 1))], axis=0)
    kbE = kb * E
    u = _tmm(_nn, Tm, vb)
    w = _tmm(_nn, Tm, kbE)
    Sb = S.astype(BF)

    def bd(t):
        return jnp.concatenate([jnp.where(rowA, t, 0.0), jnp.where(rowA, 0.0, t)], axis=1).astype(BF)

    w_bd = bd(w)
    vn = u - _nn(w_bd, Sb)
    Q = QK * gam
    qE_bd = bd(q * E)
    kF_bd = bd(k * Fd)
    vnb = vn.astype(BF)
    o = _nn(qE_bd, Sb) + _nn(Q.astype(BF), vnb)
    S_new = S * egl_st + _tn(kF_bd, vnb)
    return dict(gam=gam, kb=kb, vb=vb, P=P, QK=QK, Tm=Tm, E=E, Fd=Fd, egl=egl, egl_st=egl_st, kbE=kbE, w_bd=w_bd,
                vnb=vnb, Q=Q, qE_bd=qE_bd, kF_bd=kF_bd, o=o, S_new=S_new)


def _stack_heads(ref, rows, pair):
    return jnp.concatenate([ref[rows, (2 * pair) * DH:(2 * pair + 1) * DH],
                            ref[rows, (2 * pair + 1) * DH:(2 * pair + 2) * DH]], axis=0)


def _stack_cols(val, lane_a, lane_b, bcast_rows=None):
    a, b = val[:, lane_a:lane_a + 1], val[:, lane_b:lane_b + 1]
    if bcast_rows:
        a, b = jnp.broadcast_to(a, (bcast_rows, 1)), jnp.broadcast_to(b, (bcast_rows, 1))
    return jnp.concatenate([a, b], axis=0)


def _pool_windows(ext, tm, reverse):
    n = tm + HALO
    outs = []
    for gi in range(NG):
        a = ext[:, gi * 128:(gi + 1) * 128]
        s = 1
        while s < POOL_WINDOWS[gi]:
            a = a + pltpu.roll(a, (n - s) if reverse else s, 0)
            s *= 2
        outs.append(a[0:tm] if reverse else a[HALO:HALO + tm])
    return jnp.concatenate(outs, axis=1)


def _pool_count(tm, tile_index):
    t1 = (lax.broadcasted_iota(jnp.int32, (tm, PW), 0) + tile_index * tm + 1).astype(F32)
    win = jnp.concatenate([jnp.full((tm, 128), float(w), F32) for w in POOL_WINDOWS], axis=1)
    return 1.0 / jnp.minimum(t1, win)


def _mix_core(x, gate, qn, kn, v, z, p, bg, gc, gnorm, pool_w, pool_scale, w_out):
    T = x.shape[0]
    tm = _tile(T)
    nc = tm // CH

    def body(x_ref, gt_ref, qn_ref, kn_ref, v_ref, z_ref, p_ref, bg_ref, gc_ref, gnm_ref, pw_ref, ps_ref, wo_ref,
             xo_ref, mx_ref, cat_ref, o_ref, sall_ref, S_scr, pext):
        i = pl.program_id(0)

        @pl.when(i == 0)
        def _():
            S_scr[...] = jnp.zeros((NH * DH, DH), F32)
            pext[pl.ds(0, HALO), :] = jnp.zeros((HALO, PW), F32)

        cn = _pair_consts()

        def chunk(c, carry):
            r0 = pl.multiple_of(c * CH, CH)
            rows = pl.ds(r0, CH)
            bgv = bg_ref[rows, :]
            gcv_all = gc_ref[rows, :]
            gl_all = gc_ref[pl.ds(r0 + CH - 1, 1), :]
            for pr in range(NH // 2):
                ha, hb = 2 * pr, 2 * pr + 1
                S = S_scr[pl.ds(pr * 2 * DH, 2 * DH), :]
                sall_ref[c, ha:hb + 1] = S.reshape(2, DH, DH)
                res = _pair_forward(_stack_heads(qn_ref, rows, pr), _stack_heads(kn_ref, rows, pr),
                                    _stack_heads(v_ref, rows, pr), _stack_cols(bgv, ha, hb),
                                    _stack_cols(gcv_all, NH + ha, NH + hb), _stack_cols(gl_all, NH + ha, NH + hb, CH),
                                    S, cn)
                o_ref[rows, ha * DH:(ha + 1) * DH] = res["o"][0:CH]
                o_ref[rows, hb * DH:(hb + 1) * DH] = res["o"][CH:R2]
                S_scr[pl.ds(pr * 2 * DH, 2 * DH), :] = res["S_new"]
            return carry

        lax.fori_loop(0, nc, chunk, 0)

        for hd in range(NH):
            cols = slice(hd * DH, (hd + 1) * DH)
            oh = o_ref[:, cols]
            zh = z_ref[:, cols]
            r = lax.rsqrt(jnp.mean(oh * oh, axis=-1, keepdims=True) + EPS)
            cat_ref[:, cols] = (oh * r * gnm_ref[...] * (zh * _sigmoid(zh))).astype(BF)

        pv = p_ref[...]
        pext[pl.ds(HALO, tm), :] = pv
        pooled = _pool_windows(pext[...], tm, False) * _pool_count(tm, i) - pv
        pext[pl.ds(0, HALO), :] = pext[pl.ds(tm, HALO), :]
        for gi in range(NG):
            cols = slice(gi * 128, (gi + 1) * 128)
            pm = _nn(pooled[:, cols].astype(BF), pw_ref[gi])
            cat_ref[:, GW + gi * 128:GW + (gi + 1) * 128] = (pm * ps_ref[:, cols]).astype(BF)

        mixed = _nn(cat_ref[...], wo_ref[...])
        mx_ref[...] = mixed
        xo_ref[...] = x_ref[...] + gt_ref[...] * mixed

    wide = lambda w: _row_spec(tm, w)
    return pl.pallas_call(
        body, grid=(T // tm,), name="mix_core",
        in_specs=[wide(D), _const_spec((1, D)), wide(GW), wide(GW), wide(GW), wide(GW), wide(PW), wide(128), wide(128),
                  _const_spec((1, DH)), _const_spec((NG, 128, 128)), _const_spec((1, PW)), _const_spec((D, D))],
        out_specs=[wide(D), wide(D), wide(D), wide(GW), pl.BlockSpec((nc, NH, DH, DH), lambda i: (i, 0, 0, 0))],
        out_shape=[jax.ShapeDtypeStruct((T, D), F32), jax.ShapeDtypeStruct((T, D), F32),
                   jax.ShapeDtypeStruct((T, D), BF), jax.ShapeDtypeStruct((T, GW), F32),
                   jax.ShapeDtypeStruct((T // CH, NH, DH, DH), F32)],
        scratch_shapes=[pltpu.VMEM((NH * DH, DH), F32), pltpu.VMEM((tm + HALO, PW), F32)],
        compiler_params=_cparams(),
    )(x, gate, qn, kn, v, z, p, bg, gc, gnorm, pool_w, pool_scale, w_out)


def _pair_backward(fw, q, k, v, beta, S, do, dSn, cn):
    bf = lambda t: t.astype(BF)
    rowA, rowS = cn["rowA"], cn["rowS"]
    sel = lambda t: jnp.where(rowA, t[:, 0:DH], t[:, DH:2 * DH])
    Sb, dSb, dob, vnb = bf(S), bf(dSn), bf(do), fw["vnb"]
    dvn = _tn(bf(fw["Q"]), dob) + _nn(fw["kF_bd"], dSb)
    dQ = _nt(dob, vnb)
    dqE = sel(_nt(dob, Sb))
    dkF = sel(_nt(vnb, dSb))
    dvnb = bf(dvn)
    dw = -sel(_nt(dvnb, Sb))
    dS_new = _tn(fw["qE_bd"], dob) + fw["egl_st"] * dSn - _tn(fw["w_bd"], dvnb)
    prod = jnp.sum(dSn * S, axis=1, keepdims=True)
    d_egl_a = jnp.sum(jnp.where(rowS, prod, 0.0), axis=0, keepdims=True)
    d_egl_b = jnp.sum(jnp.where(rowS, 0.0, prod), axis=0, keepdims=True)
    Tm = fw["Tm"]
    dvb = _tmm(_tn, Tm, dvn)
    dkbE = _tmm(_tn, Tm, dw)
    dTm = _tmm(_nt, dvn, fw["vb"]) + _tmm(_nt, dw, fw["kbE"])
    dA = -_tmm(_tn, Tm, _tmm(_nt, dTm, Tm))
    gam = fw["gam"]
    N = jnp.where(cn["strict"], dA * gam, 0.0)
    Rm = dQ * gam
    Wm = Rm * fw["QK"] + N * fw["P"]
    dgc = jnp.sum(Wm, axis=1, keepdims=True) - jnp.sum(Wm.T, axis=1, keepdims=True)
    kbf, Nb, Rb = bf(k), bf(N), bf(Rm)
    E, Fd, egl = fw["E"], fw["Fd"], fw["egl"]
    dq = dqE * E + _nn(Rb, kbf)
    dkb = dkbE * E + _nn(Nb, kbf)
    dk = dkF * Fd + _tn(Rb, bf(q)) + _tn(Nb, bf(fw["kb"])) + beta * dkb
    dbeta = jnp.sum(dkb * k, axis=1, keepdims=True) + jnp.sum(dvb * v, axis=1, keepdims=True)
    dv = beta * dvb
    dE = jnp.sum(dqE * q, axis=1, keepdims=True) + jnp.sum(dkbE * fw["kb"], axis=1, keepdims=True)
    fdf = jnp.sum(dkF * k, axis=1, keepdims=True) * Fd
    dgl_a = d_egl_a * egl[0:1] + jnp.sum(jnp.where(rowA, fdf, 0.0), axis=0, keepdims=True)
    dgl_b = d_egl_b * egl[CH:CH + 1] + jnp.sum(jnp.where(rowA, 0.0, fdf), axis=0, keepdims=True)
    dgc = dgc + dE * E - fdf + jnp.where(cn["last"], jnp.where(rowA, dgl_a, dgl_b), 0.0)
    return dq, dk, dv, dbeta, dgc, dS_new


def _mix_core_bwd(dxo, gate, mixed, cat, o, sall, qn, kn, v, z, p, bg, gc, gnorm, pool_w, pool_scale, w_out,
                  scatter=()):
    T = dxo.shape[0]
    tm = _tile(T, 256)
    nt = T // tm
    nc = tm // CH
    ns = len(scatter)

    def body(*refs):
        (dx_ref, gt_ref, mx_ref, cat_ref, o_ref, sall_ref, qn_ref, kn_ref, v_ref, z_ref, p_ref, ph_ref, bg_ref,
         gc_ref, gnm_ref, pw_ref, ps_ref, wo_ref) = refs[:18]
        (dq_ref, dk_ref, dv_ref, dz_ref, dp_ref, dbg_ref, dgt_ref, dwo_ref, dpw_ref, dps_ref,
         dgn_ref) = refs[18 + ns:29 + ns]
        dS_scr, pext, yext, do_buf, dwo_acc = refs[29 + 2 * ns:34 + 2 * ns]
        i = pl.program_id(0)
        ti = nt - 1 - i
        if ns:
            plan = _Scatter(refs[18:18 + ns], refs[29 + ns:29 + 2 * ns], *refs[34 + 2 * ns:])
            _comm_begin(i, plan, nt - 1)

        @pl.when(i == 0)
        def _():
            dS_scr[...] = jnp.zeros((NH * DH, DH), F32)
            yext[pl.ds(tm, HALO), :] = jnp.zeros((HALO, PW), F32)
            dwo_acc[...] = jnp.zeros((D, D), F32)
            dgt_ref[...] = jnp.zeros((8, D), F32)
            dpw_ref[...] = jnp.zeros((NG, 128, 128), F32)
            dps_ref[...] = jnp.zeros((8, PW), F32)
            dgn_ref[...] = jnp.zeros((8, DH), F32)

        dx2 = dx_ref[...]
        dgt_ref[...] += _sum8(mx_ref[...] * dx2)
        dmix = (gt_ref[...] * dx2).astype(BF)
        dcat = _nt(dmix, wo_ref[...])
        dwo_acc[...] += _tn(cat_ref[...], dmix)

        pv = p_ref[...]
        pext[pl.ds(0, HALO), :] = jnp.where(ti == 0, 0.0, ph_ref[...])
        pext[pl.ds(HALO, tm), :] = pv
        inv_cnt = _pool_count(tm, ti)
        pooled = _pool_windows(pext[...], tm, False) * inv_cnt - pv
        dpooled = []
        for gi in range(NG):
            cols = slice(gi * 128, (gi + 1) * 128)
            pgb = pooled[:, cols].astype(BF)
            pm = _nn(pgb, pw_ref[gi])
            dpo = dcat[:, GW + gi * 128:GW + (gi + 1) * 128]
            dps_ref[:, cols] += _sum8(dpo * pm)
            dpm = (dpo * ps_ref[:, cols]).astype(BF)
            dpooled.append(_nt(dpm, pw_ref[gi]))
            dpw_ref[gi] += _tn(pgb, dpm)
        dpooled = jnp.concatenate(dpooled, axis=1)
        y = dpooled * inv_cnt
        yext[pl.ds(0, tm), :] = y
        dp_ref[...] = _pool_windows(yext[...], tm, True) - dpooled
        yext[pl.ds(tm, HALO), :] = y[0:HALO]

        gnm = gnm_ref[...]
        dgn = jnp.zeros((8, DH), F32)
        for hd in range(NH):
            cols = slice(hd * DH, (hd + 1) * DH)
            oh = o_ref[:, cols]
            zh = z_ref[:, cols]
            r = lax.rsqrt(jnp.mean(oh * oh, axis=-1, keepdims=True) + EPS)
            n = oh * r
            sg = _sigmoid(zh)
            zs = zh * sg
            dgo = dcat[:, cols]
            dgn = dgn + _sum8(dgo * zs * n)
            dn = dgo * zs * gnm
            do_buf[:, cols] = r * (dn - n * jnp.mean(dn * n, axis=-1, keepdims=True))
            dz_ref[:, cols] = dgo * n * gnm * (sg * (1.0 + zh * (1.0 - sg)))
        dgn_ref[...] += dgn

        cn = _pair_consts()
        lane_c = lax.broadcasted_iota(jnp.int32, (CH, 128), 1)

        def chunk(cc, carry):
            c = nc - 1 - cc
            r0 = pl.multiple_of(c * CH, CH)
            rows = pl.ds(r0, CH)
            bgv = bg_ref[rows, :]
            gcv_all = gc_ref[rows, :]
            gl_all = gc_ref[pl.ds(r0 + CH - 1, 1), :]
            dbg = jnp.zeros((CH, 128), F32)
            for pr in range(NH // 2):
                ha, hb = 2 * pr, 2 * pr + 1
                srows = pl.ds(pr * 2 * DH, 2 * DH)
                S = sall_ref[c, ha:hb + 1].reshape(2 * DH, DH)
                q, k, vv = _stack_heads(qn_ref, rows, pr), _stack_heads(kn_ref, rows, pr), _stack_heads(v_ref, rows, pr)
                beta = _stack_cols(bgv, ha, hb)
                fw = _pair_forward(q, k, vv, beta, _stack_cols(gcv_all, NH + ha, NH + hb),
                                   _stack_cols(gl_all, NH + ha, NH + hb, CH), S, cn)
                dq, dk, dv, dbeta, dgc, dS_new = _pair_backward(fw, q, k, vv, beta, S, _stack_heads(do_buf, rows, pr),
                                                                dS_scr[srows, :], cn)
                for hd, half in ((ha, slice(0, CH)), (hb, slice(CH, R2))):
                    cols = slice(hd * DH, (hd + 1) * DH)
                    dq_ref[rows, cols] = dq[half]
                    dk_ref[rows, cols] = dk[half]
                    dv_ref[rows, cols] = dv[half]
                    dbg = dbg + jnp.where(lane_c == hd, dbeta[half], 0.0) + jnp.where(lane_c == NH + hd, dgc[half], 0.0)
                dS_scr[srows, :] = dS_new
            dbg_ref[rows, :] = dbg
            return carry

        lax.fori_loop(0, nc, chunk, 0)

        lane = lax.broadcasted_iota(jnp.int32, (tm, 128), 1)
        row = lax.broadcasted_iota(jnp.int32, (tm, 128), 0) % CH
        dbg_all = dbg_ref[...]
        dg = _seg_cumsum(jnp.where(lane >= NH, dbg_all, 0.0), row, reverse=True)
        dbg_ref[...] = jnp.where(lane < NH, dbg_all, dg)

        @pl.when(i == nt - 1)
        def _():
            dwo_ref[...] = dwo_acc[...].astype(BF)

        if ns:
            _comm_end(i, plan, nt - 1)

    rev = lambda w: pl.BlockSpec((tm, w), lambda i: (nt - 1 - i, 0))
    halo = pl.BlockSpec((HALO, PW), lambda i: (jnp.maximum((nt - 1 - i) * (tm // HALO) - 1, 0), 0))
    shp = lambda w: jax.ShapeDtypeStruct((T, w), F32)
    fix = lambda *s: pl.BlockSpec(s, lambda i: (0,) * len(s))
    return pl.pallas_call(
        body, grid=(nt,), name="mix_core_bwd",
        in_specs=[rev(D), _const_spec((1, D)), rev(D), rev(D), rev(GW),
                  pl.BlockSpec((nc, NH, DH, DH), lambda i: (nt - 1 - i, 0, 0, 0)),
                  rev(GW), rev(GW), rev(GW), rev(GW), rev(PW), halo, rev(128), rev(128),
                  _const_spec((1, DH)), _const_spec((NG, 128, 128)), _const_spec((1, PW)), _const_spec((D, D))]
        + [ANY] * ns,
        out_specs=[rev(GW), rev(GW), rev(GW), rev(GW), rev(PW), rev(128),
                   fix(8, D), fix(D, D), fix(NG, 128, 128), fix(8, PW), fix(8, DH)] + [ANY] * ns,
        out_shape=[shp(GW), shp(GW), shp(GW), shp(GW), shp(PW), shp(128),
                   jax.ShapeDtypeStruct((8, D), F32), jax.ShapeDtypeStruct((D, D), BF),
                   jax.ShapeDtypeStruct((NG, 128, 128), F32), jax.ShapeDtypeStruct((8, PW), F32),
                   jax.ShapeDtypeStruct((8, DH), F32)] + [jax.ShapeDtypeStruct(t.shape, t.dtype) for t in scatter],
        scratch_shapes=[pltpu.VMEM((NH * DH, DH), F32), pltpu.VMEM((tm + HALO, PW), F32),
                        pltpu.VMEM((tm + HALO, PW), F32), pltpu.VMEM((tm, GW), F32), pltpu.VMEM((D, D), F32)]
        + (_scatter_sems(ns) if ns else []),
        compiler_params=_cparams(),
    )(dxo, gate, mixed, cat, o, sall, qn, kn, v, z, p, p, bg, gc, gnorm, pool_w, pool_scale, w_out, *scatter)


def _mix_proj_bwd(dxo, dqn, dkn, dv, dz, dp, dbg, xq, ba, x, shift, scale, gain, winT, conv_w, alog, dtb):
    T = x.shape[0]
    tm = min(256, T)
    nt = T // tm
    W3 = 3 * GW

    def body(dxo_ref, dqn_ref, dkn_ref, dv_ref, dz_ref, dp_ref, dbg_ref, xq_ref, xh_ref, ba_ref, x_ref, sh_ref, sc_ref,
             gn_ref, w_ref, cw_ref, al_ref, dt_ref,
             dx_ref, dw_ref, dcw_ref, dal_ref, ddt_ref, dsh_ref, dsc_ref, dgn_ref,
             ext, dcext, dproj, dw_acc):
        i = pl.program_id(0)
        ti = nt - 1 - i

        @pl.when(i == 0)
        def _():
            dcext[pl.ds(tm, 8), :] = jnp.zeros((8, W3), F32)
            dw_acc[...] = jnp.zeros((DINP, D), F32)
            dcw_ref[...] = jnp.zeros((4, 8, W3), F32)
            dal_ref[...] = jnp.zeros((8, 128), F32)
            ddt_ref[...] = jnp.zeros((8, 128), F32)
            dsh_ref[...] = jnp.zeros((8, D), F32)
            dsc_ref[...] = jnp.zeros((8, D), F32)
            dgn_ref[...] = jnp.zeros((8, D), F32)

        ext[pl.ds(0, 8), :] = jnp.where(ti == 0, 0.0, xh_ref[...])
        ext[pl.ds(8, tm), :] = xq_ref[...]
        c, sg = _conv_silu(ext, cw_ref, tm)
        qt = c * sg
        dsilu = sg * (1.0 + c * (1.0 - sg))
        for hd in range(NH):
            for part, dref, mult in ((0, dqn_ref, DH ** -0.5), (1, dkn_ref, 1.0)):
                cols = slice(part * GW + hd * DH, part * GW + (hd + 1) * DH)
                xh = qt[:, cols]
                rr = lax.rsqrt(jnp.sum(xh * xh, axis=-1, keepdims=True) + EPS)
                unit = xh * rr
                du = dref[:, hd * DH:(hd + 1) * DH] * mult
                dxh = rr * (du - unit * jnp.sum(du * unit, axis=-1, keepdims=True))
                dcext[pl.ds(0, tm), cols] = dxh * dsilu[:, cols]
        dcext[pl.ds(0, tm), 2 * GW:W3] = dv_ref[...] * dsilu[:, 2 * GW:W3]
        dc = dcext[pl.ds(0, tm), :]
        dxq = jnp.zeros((tm, W3), F32)
        for j in range(4):
            dcw_ref[j] += _sum8(dc * ext[pl.ds(5 + j, tm), :])
            dxq = dxq + cw_ref[j:j + 1, :] * dcext[pl.ds(3 - j, tm), :]
        dcext[pl.ds(tm, 8), :] = dc[0:8]
        lane = lax.broadcasted_iota(jnp.int32, (tm, 128), 1)
        bav = ba_ref[...]
        beta, g, sarg = _gates(bav, al_ref[...], dt_ref[...], lane)
        dbg_v = dbg_ref[...]
        is_g = (lane >= NH) & (lane < 2 * NH)
        dbraw = jnp.where(lane < NH, dbg_v * beta * (1.0 - beta), 0.0)
        daraw = jnp.where(is_g, dbg_v * (-jnp.exp(al_ref[...])) * sarg, 0.0)
        dal_ref[...] += _sum8(jnp.where(is_g, dbg_v * g, 0.0))
        ddt_ref[...] += _sum8(daraw)
        dproj[:, 0:W3] = dxq.astype(BF)
        dproj[:, W3:W3 + GW] = dz_ref[...].astype(BF)
        dproj[:, W3 + GW:W3 + GW + PW] = dp_ref[...].astype(BF)
        dproj[:, W3 + GW + PW:DINP] = (dbraw + daraw).astype(BF)
        gain_v, scale_v = gn_ref[...], sc_ref[...]
        n, r, y, h = _norm_mod_fwd(x_ref[...], gain_v, sh_ref[...], scale_v)
        dpj = dproj[...]
        dh = _nn(dpj, w_ref[...])
        dw_acc[...] += _tn(dpj, h.astype(BF))
        dxn, dsh, dsc, dgn = _norm_mod_bwd(dh, n, r, y, gain_v, scale_v)
        dx_ref[...] = dxo_ref[...] + dxn
        dsh_ref[...] += dsh
        dsc_ref[...] += dsc
        dgn_ref[...] += dgn

        @pl.when(i == nt - 1)
        def _():
            dw_ref[...] = dw_acc[...].astype(BF)

    rev = lambda w: pl.BlockSpec((tm, w), lambda i: (nt - 1 - i, 0))
    halo = pl.BlockSpec((8, W3), lambda i: (jnp.maximum((nt - 1 - i) * (tm // 8) - 1, 0), 0))
    fix = lambda *s: pl.BlockSpec(s, lambda i: (0,) * len(s))
    vec = _const_spec((1, D))
    return pl.pallas_call(
        body, grid=(nt,), name="mix_proj_bwd",
        in_specs=[rev(D), rev(GW), rev(GW), rev(GW), rev(GW), rev(PW), rev(128), rev(W3), halo, rev(128), rev(D),
                  vec, vec, vec, _const_spec((DINP, D)), _const_spec((4, W3)), _const_spec((1, 128)),
                  _const_spec((1, 128))],
        out_specs=[rev(D), fix(DINP, D), fix(4, 8, W3), fix(8, 128), fix(8, 128), fix(8, D), fix(8, D), fix(8, D)],
        out_shape=[jax.ShapeDtypeStruct((T, D), F32), jax.ShapeDtypeStruct((DINP, D), BF),
                   jax.ShapeDtypeStruct((4, 8, W3), F32), jax.ShapeDtypeStruct((8, 128), F32),
                   jax.ShapeDtypeStruct((8, 128), F32), jax.ShapeDtypeStruct((8, D), F32),
                   jax.ShapeDtypeStruct((8, D), F32), jax.ShapeDtypeStruct((8, D), F32)],
        scratch_shapes=[pltpu.VMEM((tm + 8, W3), F32), pltpu.VMEM((tm + 8, W3), F32), pltpu.VMEM((tm, DINP), BF),
                        pltpu.VMEM((DINP, D), F32)],
        compiler_params=_cparams(),
    )(dxo, dqn, dkn, dv, dz, dp, dbg, xq, xq, ba, x, shift, scale, gain, winT, conv_w, alog, dtb)


MESH = pl.DeviceIdType.MESH
CHIP_RELS = ((1, 0), (0, 1), (1, 1))
DEV_RELS = tuple((dx, dy, dc) for dx in (0, 1) for dy in (0, 1) for dc in (0, 1) if (dx, dy, dc) != (0, 0, 0))
NMOD = 9
ADA_SH = NMOD * D // 4
WIN_SH = DIN // 4
WIN_PAD = 672
MSG_ROWS = 16
ANY = pl.BlockSpec(memory_space=pl.ANY)
VM = pl.BlockSpec(memory_space=pltpu.VMEM)


def _place():
    x, y, c = lax.axis_index("x"), lax.axis_index("y"), lax.axis_index("c")
    return x, y, c


class _SplitGather:
    N_SEMS = (3, 3, 3, 3, 1)

    def __init__(self, src, dst, ici_s, ici_r, d2d_s, d2d_r, lsem):
        self.src, self.dst = src, dst
        self.sems = (ici_s, ici_r, d2d_s, d2d_r)
        self.x, self.y, self.c = _place()
        self.myj = 2 * self.x + self.y
        half = src.shape[0] // 2
        self.mine = pl.ds(pl.multiple_of(self.c * half, 16), half)
        self.other = pl.ds(pl.multiple_of((1 - self.c) * half, 16), half)
        self.local = pltpu.make_async_copy(src, dst.at[self.myj], lsem.at[0])

    def _ici(self, k, slot):
        dx, dy = CHIP_RELS[k]
        return pltpu.make_async_remote_copy(
            src_ref=self.src.at[self.mine], dst_ref=self.dst.at[slot, self.mine], send_sem=self.sems[0].at[k],
            recv_sem=self.sems[1].at[k], device_id=(self.x ^ dx, self.y ^ dy, self.c), device_id_type=MESH)

    def _d2d(self, k, rows):
        dx, dy = CHIP_RELS[k]
        blk = self.dst.at[2 * (self.x ^ dx) + (self.y ^ dy), rows]
        return pltpu.make_async_remote_copy(
            src_ref=blk, dst_ref=blk, send_sem=self.sems[2].at[k], recv_sem=self.sems[3].at[k],
            device_id=(self.x, self.y, 1 - self.c), device_id_type=MESH)

    def start(self):
        self.local.start()
        for k in range(3):
            self._ici(k, self.myj).start()

    def forward(self):
        for k, (dx, dy) in enumerate(CHIP_RELS):
            self._ici(k, 2 * (self.x ^ dx) + (self.y ^ dy)).wait_recv()
            self._d2d(k, self.mine).start()

    def finish(self):
        for k in range(3):
            self._d2d(k, self.other).wait_recv()
        for k in range(3):
            self._d2d(k, self.mine).wait_send()
            self._ici(k, self.myj).wait_send()
        self.local.wait()


class _Scatter:
    def __init__(self, ins, outs, send, recv, lsem):
        self.ins, self.outs, self.send, self.recv, self.lsem = ins, outs, send, recv, lsem
        self.x, self.y, self.c = _place()
        self.myj = 2 * self.x + self.y

    def _copy(self, a, k, landing):
        dx, dy = CHIP_RELS[k]
        pj = 2 * (self.x ^ dx) + (self.y ^ dy)
        return pltpu.make_async_remote_copy(
            src_ref=self.ins[a].at[pj], dst_ref=self.outs[a].at[pj if landing else self.myj],
            send_sem=self.send.at[a, k], recv_sem=self.recv.at[a, k],
            device_id=(self.x ^ dx, self.y ^ dy, self.c), device_id_type=MESH)

    def _local(self, a):
        return pltpu.make_async_copy(self.ins[a].at[self.myj], self.outs[a].at[self.myj], self.lsem.at[a])

    def start(self):
        for a in range(len(self.ins)):
            self._local(a).start()
            for k in range(3):
                self._copy(a, k, False).start()

    def finish(self):
        for a in range(len(self.ins)):
            for k in range(3):
                self._copy(a, k, True).wait_recv()
            for k in range(3):
                self._copy(a, k, False).wait_send()
            self._local(a).wait()


def _scatter_sems(n):
    return [pltpu.SemaphoreType.DMA((n, 3)), pltpu.SemaphoreType.DMA((n, 3)), pltpu.SemaphoreType.DMA((n,))]


def _gather_sems():
    return [pltpu.SemaphoreType.DMA((k,)) for k in _SplitGather.N_SEMS]


def _comm_begin(i, plan, last):
    @pl.when(i == 0)
    def _():
        plan.start()

    if hasattr(plan, "forward"):
        @pl.when(i == max(last - 3, 0))
        def _():
            plan.forward()


def _comm_end(i, plan, last):
    @pl.when(i == last)
    def _():
        plan.finish()


def _ada_exchange(msg, w_ada, b_ada, wblock):
    def body(msg_ref, w_ref, b_ref, wb_ref, all_ref, mod_ref, wg_ref, modp, send1, recv1, send2, recv2, lsem, *gsems):
        x, y, c = _place()
        me = 4 * x + 2 * y + c
        wgather = _SplitGather(wb_ref, wg_ref, *gsems)
        wgather.start()
        own = pltpu.make_async_copy(msg_ref, all_ref.at[me], lsem.at[0])
        own.start()

        def gather(k, rel, slot):
            dx, dy, dc = rel
            return pltpu.make_async_remote_copy(
                src_ref=msg_ref, dst_ref=all_ref.at[slot], send_sem=send1.at[k], recv_sem=recv1.at[k],
                device_id=(x ^ dx, y ^ dy, c ^ dc), device_id_type=MESH)

        for k, rel in enumerate(DEV_RELS):
            gather(k, rel, me).start()
        for k, (dx, dy, dc) in enumerate(DEV_RELS):
            gather(k, (dx, dy, dc), 4 * (x ^ dx) + 2 * (y ^ dy) + (c ^ dc)).wait_recv()
        for k, rel in enumerate(DEV_RELS):
            gather(k, rel, me).wait_send()
        own.wait()

        for d in range(8):
            cv = all_ref[d, 0:8, :]
            act = cv * _sigmoid(cv)
            modp[d] = _nn(act, w_ref[...], precision=HI) + b_ref[...]

        myj = 2 * x + y
        keep = pltpu.make_async_copy(modp.at[me], mod_ref.at[myj], lsem.at[1])
        keep.start()

        def scatter(k, rel):
            dx, dy = rel
            return pltpu.make_async_remote_copy(
                src_ref=modp.at[4 * (x ^ dx) + 2 * (y ^ dy) + c], dst_ref=mod_ref.at[myj],
                send_sem=send2.at[k], recv_sem=recv2.at[k], device_id=(x ^ dx, y ^ dy, c), device_id_type=MESH)

        def landed(k, rel):
            dx, dy = rel
            return pltpu.make_async_remote_copy(
                src_ref=modp.at[me], dst_ref=mod_ref.at[2 * (x ^ dx) + (y ^ dy)],
                send_sem=send2.at[k], recv_sem=recv2.at[k], device_id=(x ^ dx, y ^ dy, c), device_id_type=MESH)

        for k, rel in enumerate(CHIP_RELS):
            scatter(k, rel).start()
        for k, rel in enumerate(CHIP_RELS):
            landed(k, rel).wait_recv()
        for k, rel in enumerate(CHIP_RELS):
            scatter(k, rel).wait_send()
        keep.wait()
        wgather.forward()
        wgather.finish()

    return pl.pallas_call(
        body, name="ada_exchange", in_specs=[VM, VM, VM, ANY], out_specs=[VM, VM, ANY],
        out_shape=[jax.ShapeDtypeStruct((8, MSG_ROWS, D), F32), jax.ShapeDtypeStruct((4, 8, ADA_SH), F32),
                   jax.ShapeDtypeStruct((4,) + wblock.shape, wblock.dtype)],
        scratch_shapes=[pltpu.VMEM((8, 8, ADA_SH), F32), pltpu.SemaphoreType.DMA((7,)), pltpu.SemaphoreType.DMA((7,)),
                        pltpu.SemaphoreType.DMA((3,)), pltpu.SemaphoreType.DMA((3,)), pltpu.SemaphoreType.DMA((2,))]
        + _gather_sems(),
        compiler_params=pltpu.CompilerParams(vmem_limit_bytes=VMEM_LIMIT),
    )(msg, w_ada, b_ada, wblock)


def _chip_exchange(parts, name):
    n = len(parts)

    def body(*refs):
        plan = _Scatter(refs[:n], refs[n:2 * n], *refs[2 * n:])
        plan.start()
        plan.finish()

    return pl.pallas_call(
        body, name=name, in_specs=[ANY] * n, out_specs=[ANY] * n,
        out_shape=[jax.ShapeDtypeStruct(p.shape, p.dtype) for p in parts], scratch_shapes=_scatter_sems(n),
    )(*parts)


def _all_exchange(parts, name):
    n = len(parts)

    def body(*refs):
        ins, outs = refs[:n], refs[n:2 * n]
        send, recv, lsem = refs[2 * n:]
        x, y, c = _place()
        me = 4 * x + 2 * y + c
        local, remote = [], []
        for a in range(n):
            local.append(pltpu.make_async_copy(ins[a], outs[a].at[me], lsem.at[a]))
            for k, (dx, dy, dc) in enumerate(DEV_RELS):
                remote.append((pltpu.make_async_remote_copy(
                    src_ref=ins[a], dst_ref=outs[a].at[me], send_sem=send.at[a, k], recv_sem=recv.at[a, k],
                    device_id=(x ^ dx, y ^ dy, c ^ dc), device_id_type=MESH), a, k,
                    4 * (x ^ dx) + 2 * (y ^ dy) + (c ^ dc)))
        for cp in local:
            cp.start()
        for cp, _, _, _ in remote:
            cp.start()
        for _, a, k, pd in remote:
            pltpu.make_async_remote_copy(
                src_ref=ins[a], dst_ref=outs[a].at[pd], send_sem=send.at[a, k], recv_sem=recv.at[a, k],
                device_id=(x, y, c), device_id_type=MESH).wait_recv()
        for cp, _, _, _ in remote:
            cp.wait_send()
        for cp in local:
            cp.wait()

    shapes = [jax.ShapeDtypeStruct((8,) + tuple(p.shape), p.dtype) for p in parts]
    return pl.pallas_call(
        body, name=name, in_specs=[ANY] * n, out_specs=[ANY] * n, out_shape=shapes,
        scratch_shapes=[pltpu.SemaphoreType.DMA((n, 7)), pltpu.SemaphoreType.DMA((n, 7)), pltpu.SemaphoreType.DMA((n,))],
    )(*parts)


def _pair_exchange(parts, name):
    n = len(parts)

    def body(*refs):
        ins, outs = refs[:n], refs[n:2 * n]
        send, recv = refs[2 * n:]
        x, y, c = _place()
        cps = [pltpu.make_async_remote_copy(
            src_ref=ins[a], dst_ref=outs[a], send_sem=send.at[a], recv_sem=recv.at[a],
            device_id=(x, y, 1 - c), device_id_type=MESH) for a in range(n)]
        for cp in cps:
            cp.start()
        for cp in cps:
            cp.wait_recv()
        for cp in cps:
            cp.wait_send()

    shapes = [jax.ShapeDtypeStruct(p.shape, p.dtype) for p in parts]
    return pl.pallas_call(
        body, name=name, in_specs=[ANY] * n, out_specs=[ANY] * n, out_shape=shapes,
        scratch_shapes=[pltpu.SemaphoreType.DMA((n,)), pltpu.SemaphoreType.DMA((n,))],
    )(*parts)


def _row_tile(rows, cap):
    best = rows
    for t in range(8, min(cap, rows) + 1, 8):
        if rows % t == 0:
            best = t
    return best if rows % 8 == 0 else rows


def _sum_slots(parts, name):
    n, rows, width = parts.shape
    tr = _row_tile(rows, 352)

    def body(p_ref, o_ref):
        acc = p_ref[0].astype(F32)
        for j in range(1, n):
            acc = acc + p_ref[j].astype(F32)
        o_ref[...] = acc

    return pl.pallas_call(
        body, grid=(rows // tr,), name=name,
        in_specs=[pl.BlockSpec((n, tr, width), lambda i: (0, i, 0))],
        out_specs=pl.BlockSpec((tr, width), lambda i: (i, 0)),
        out_shape=jax.ShapeDtypeStruct((rows, width), F32),
        compiler_params=_cparams(),
    )(parts)


def _adamw_math(g, w, m, v):
    m_new = ADAM_B1 * m + (1.0 - ADAM_B1) * g
    v_new = ADAM_B2 * v + (1.0 - ADAM_B2) * (g * g)
    m_hat = m_new / (1.0 - ADAM_B1 ** ADAM_STEP)
    v_hat = v_new / (1.0 - ADAM_B2 ** ADAM_STEP)
    delta = -ADAM_LR * (m_hat / (jnp.sqrt(v_hat) + ADAM_EPS) + ADAM_WD * w)
    return delta, m_new, v_new


def _adamw(grads, w, m, v, name):
    rows, width = w.shape
    tr = _row_tile(rows, 256 if width <= 1024 else 128)
    ng = len(grads)

    def body(*refs):
        g = refs[0][...]
        for r in refs[1:ng]:
            g = g + r[...]
        w_ref, m_ref, v_ref, g_out, d_out, m_out, v_out = refs[ng:]
        delta, m_new, v_new = _adamw_math(g, w_ref[...], m_ref[...], v_ref[...])
        g_out[...] = g
        d_out[...] = delta
        m_out[...] = m_new
        v_out[...] = v_new

    blk = pl.BlockSpec((tr, width), lambda i: (i, 0))
    return pl.pallas_call(
        body, grid=(rows // tr,), name=name,
        in_specs=[blk] * (ng + 3), out_specs=[blk] * 4,
        out_shape=[jax.ShapeDtypeStruct((rows, width), F32)] * 4,
        compiler_params=_cparams(),
    )(*grads, w, m, v)


def _adamw_ada(msgs, dmods, w, m, v):
    rows, width = w.shape
    tr = 128

    def body(c_ref, dm_ref, w_ref, m_ref, v_ref, g_out, d_out, m_out, v_out):
        cv = jnp.concatenate([c_ref[d, 0:1, :] for d in range(8)], axis=0)
        act = cv * _sigmoid(cv)
        g = _tn(act, dm_ref[...], precision=HI)
        delta, m_new, v_new = _adamw_math(g, w_ref[...], m_ref[...], v_ref[...])
        g_out[...] = g
        d_out[...] = delta
        m_out[...] = m_new
        v_out[...] = v_new

    blk = pl.BlockSpec((tr, width), lambda i: (i, 0))
    return pl.pallas_call(
        body, grid=(rows // tr,), name="adamw_w_ada",
        in_specs=[pl.BlockSpec((8, MSG_ROWS, tr), lambda i: (0, 0, i)), pl.BlockSpec((8, width), lambda i: (0, 0)),
                  blk, blk, blk],
        out_specs=[blk] * 4, out_shape=[jax.ShapeDtypeStruct((rows, width), F32)] * 4,
        compiler_params=_cparams(),
    )(msgs, dmods, w, m, v)


def _adamw_small(parts, w, m, v, name):
    n, rows, width = parts.shape

    def body(p_ref, w_ref, m_ref, v_ref, g_out, d_out, m_out, v_out):
        g = p_ref[0]
        for j in range(1, n):
            g = g + p_ref[j]
        delta, m_new, v_new = _adamw_math(g, w_ref[...], m_ref[...], v_ref[...])
        g_out[...] = g
        d_out[...] = delta
        m_out[...] = m_new
        v_out[...] = v_new

    return pl.pallas_call(
        body, name=name, in_specs=[VM] * 4, out_specs=[VM] * 4,
        out_shape=[jax.ShapeDtypeStruct((rows, width), F32)] * 4,
        compiler_params=pltpu.CompilerParams(vmem_limit_bytes=VMEM_LIMIT),
    )(parts, w, m, v)


SMALL_ROWS = 24


def _pad_row(vec, width=D):
    vec = vec.reshape(1, -1)
    return jnp.pad(vec, ((0, 0), (0, width - vec.shape[1])))


def _lanes_4_7(vec4):
    return jnp.zeros((1, 128), F32).at[0, NH:2 * NH].set(vec4.reshape(NH))


def kernel(x, c, w_ada, b_ada, norm_ffn1, ffn1_gate, ffn1_up, ffn1_down, norm_mix, w_in, conv_w, a_log, dt_bias, gdn_norm, pool_w, pool_scale, w_out, norm_ffn2, ffn2_gate, ffn2_up, ffn2_down, final_norm, loss_target, m_w_ada, m_b_ada, m_norm_ffn1, m_ffn1_gate, m_ffn1_up, m_ffn1_down, m_norm_mix, m_w_in, m_conv_w, m_a_log, m_dt_bias, m_gdn_norm, m_pool_w, m_pool_scale, m_w_out, m_norm_ffn2, m_ffn2_gate, m_ffn2_up, m_ffn2_down, m_final_norm, v_w_ada, v_b_ada, v_norm_ffn1, v_ffn1_gate, v_ffn1_up, v_ffn1_down, v_norm_mix, v_w_in, v_conv_w, v_a_log, v_dt_bias, v_gdn_norm, v_pool_w, v_pool_scale, v_w_out, v_norm_ffn2, v_ffn2_gate, v_ffn2_up, v_ffn2_down, v_final_norm):
    xs = x[0]
    tgt = loss_target[0]
    chip = 2 * lax.axis_index("x") + lax.axis_index("y")
    me = 2 * chip + lax.axis_index("c")

    fsh = FF // 4
    block_a = jnp.concatenate([ffn1_gate[0].T, ffn1_up[0].T, ffn1_down[0]], axis=0).astype(BF)
    block_b = jnp.concatenate([ffn2_gate[0].T, ffn2_up[0].T, ffn2_down[0], w_out[0],
                               jnp.pad(w_in[0].T, ((0, WIN_PAD - WIN_SH), (0, 0)))], axis=0).astype(BF)

    msg = jnp.concatenate([jnp.broadcast_to(c, (8, D)), jnp.pad(conv_w[0], ((0, 0), (0, D - 3 * GW // 4))),
                           jnp.zeros((MSG_ROWS - 12, D), F32)], axis=0)
    b_sh = lax.dynamic_slice(b_ada, (0, chip * ADA_SH), (1, ADA_SH))
    msgs, mod4, gath_a = _ada_exchange(msg, w_ada[0], b_sh, block_a)
    mod = mod4[:, 0, :].reshape(NMOD, D)
    mrow = [mod[i:i + 1] for i in range(NMOD)]
    conv_full = jnp.concatenate([msgs[2 * j, 8:12, :3 * GW // 4] for j in range(4)], axis=1)
    alog, dtb = _lanes_4_7(a_log), _lanes_4_7(dt_bias)
    gnm = gdn_norm.reshape(1, DH)
    pwb = pool_w[0].astype(BF)
    psc = pool_scale.reshape(1, PW)
    fin = final_norm.reshape(1, D)

    x1, f1, a1, b1, gath_b = _ffn_fwd(xs, mrow[0], mrow[1], mrow[2], norm_ffn1, gath_a, 0, "ffn1_fwd", block_b)
    wo = gath_b[:, 3 * fsh:3 * fsh + D // 4, :].reshape(D, D)
    win_nat = gath_b[:, 3 * fsh + D // 4:3 * fsh + D // 4 + WIN_SH, :].reshape(DIN, D)
    winT = jnp.concatenate([win_nat[:4 * GW], win_nat[4 * GW + 2 * NH:], win_nat[4 * GW:4 * GW + 2 * NH],
                            jnp.zeros((128 - 2 * NH, D), BF)], axis=0)
    xq, ba, qn, kn, vv, z, pp, bg, gc = _mix_proj(x1, mrow[3], mrow[4], norm_mix, winT, conv_full, alog, dtb)
    x2, mixed, cat, o, sall = _mix_core(x1, mrow[5], qn, kn, vv, z, pp, bg, gc, gnm, pwb, psc, wo)
    x3, f2, a2, b2 = _ffn_fwd(x2, mrow[6], mrow[7], mrow[8], norm_ffn2, gath_b, 0, "ffn2_fwd")
    lpart, dx3, dfin = _loss_head(x3, tgt, fin)
    loss = lax.psum(jnp.sum(lpart), ("x", "y", "c"))

    slots = lambda t: t.reshape(4, t.shape[0] // 4, D)
    dx2, da2, db2, s2, h2, df2, dsh3, dsc3, dgt3, dn3 = _ffn_dgrad(
        dx3, x2, f2, a2, b2, mrow[6], mrow[7], mrow[8], norm_ffn2, gath_b, 0, "ffn2_dgrad")
    gg2, gu2, gd2 = _ffn_wgrad(da2, db2, s2, h2, df2, "ffn2_wgrad")
    dqn, dkn, dvv, dz, dpp, dbg, dgt2, dwo, dpw, dps, dgnm, *landed2 = _mix_core_bwd(
        dx2, mrow[5], mixed, cat, o, sall, qn, kn, vv, z, pp, bg, gc, gnm, pwb, psc, wo,
        scatter=[slots(gg2), slots(gu2), slots(gd2)])
    dx1, dwin, dcw, dal, ddt, dsh2, dsc2, dn2 = _mix_proj_bwd(
        dx2, dqn, dkn, dvv, dz, dpp, dbg, xq, ba, x1, mrow[3], mrow[4], norm_mix, winT, conv_full, alog, dtb)
    dwin_nat = jnp.concatenate([dwin[:4 * GW], dwin[4 * GW + PW:4 * GW + PW + 2 * NH], dwin[4 * GW:4 * GW + PW]], axis=0)
    dwin_sl = jnp.pad(dwin_nat.reshape(4, WIN_SH, D), ((0, 0), (0, WIN_PAD - WIN_SH), (0, 0)))
    dx0, da1, db1, s1, h1, df1, dsh1, dsc1, dgt1, dn1 = _ffn_dgrad(
        dx1, xs, f1, a1, b1, mrow[0], mrow[1], mrow[2], norm_ffn1, gath_a, 0, "ffn1_dgrad")
    gg1, gu1, gd1, *landed_mix = _ffn_wgrad(da1, db1, s1, h1, df1, "ffn1_wgrad", scatter=[slots(dwo), dwin_sl])
    landed1 = _chip_exchange([slots(gg1), slots(gu1), slots(gd1)], "grad_scatter")

    landed = list(landed1) + list(landed2) + list(landed_mix)
    names = ("g1", "u1", "d1", "g2", "u2", "d2", "wo", "win")
    psum = [_sum_slots(t, "sum_" + nm) for t, nm in zip(landed, names)]
    qsum = _pair_exchange(psum, "grad_pair")

    red = lambda t: jnp.sum(t, axis=0, keepdims=True)
    small = jnp.concatenate(
        [red(dn1), red(dn2), red(dn3), red(dfin),
         red(dsh1), red(dsc1), red(dgt1), red(dsh2), red(dsc2), red(dgt2), red(dsh3), red(dsc3), red(dgt3),
         _pad_row(red(dps)), _pad_row(red(dgnm)), _pad_row(red(dal)), _pad_row(red(ddt)),
         jnp.sum(dcw, axis=1).reshape(6, D), jnp.zeros((1, D), F32)], axis=0)
    small_all, dpw_all = _all_exchange([small, dpw.reshape(NG * 128, 128)], "small_gather")

    def unT(t, cols):
        return t[:cols].T

    upd = {}
    upd["ffn1_gate"] = _adamw([unT(psum[0], fsh), unT(qsum[0], fsh)], ffn1_gate[0], m_ffn1_gate[0], v_ffn1_gate[0], "adamw_g1")
    upd["ffn1_up"] = _adamw([unT(psum[1], fsh), unT(qsum[1], fsh)], ffn1_up[0], m_ffn1_up[0], v_ffn1_up[0], "adamw_u1")
    upd["ffn1_down"] = _adamw([psum[2], qsum[2]], ffn1_down[0], m_ffn1_down[0], v_ffn1_down[0], "adamw_d1")
    upd["ffn2_gate"] = _adamw([unT(psum[3], fsh), unT(qsum[3], fsh)], ffn2_gate[0], m_ffn2_gate[0], v_ffn2_gate[0], "adamw_g2")
    upd["ffn2_up"] = _adamw([unT(psum[4], fsh), unT(qsum[4], fsh)], ffn2_up[0], m_ffn2_up[0], v_ffn2_up[0], "adamw_u2")
    upd["ffn2_down"] = _adamw([psum[5], qsum[5]], ffn2_down[0], m_ffn2_down[0], v_ffn2_down[0], "adamw_d2")
    upd["w_out"] = _adamw([psum[6], qsum[6]], w_out[0], m_w_out[0], v_w_out[0], "adamw_wo")
    upd["w_in"] = _adamw([unT(psum[7], WIN_SH), unT(qsum[7], WIN_SH)], w_in[0], m_w_in[0], v_w_in[0], "adamw_win")
    dmods = lax.dynamic_slice(small_all[:, 4:4 + NMOD, :].reshape(8, NMOD * D), (0, chip * ADA_SH), (8, ADA_SH))
    upd["w_ada"] = _adamw_ada(msgs, dmods, w_ada[0], m_w_ada[0], v_w_ada[0])

    def pack_small(nf1, nmx, nf2, fn, bada, psc_, gn_, al_, dt_):
        return jnp.concatenate(
            [nf1.reshape(1, D), nmx.reshape(1, D), nf2.reshape(1, D), fn.reshape(1, D), bada.reshape(NMOD, D),
             _pad_row(psc_), _pad_row(gn_), _pad_row(_lanes_4_7(al_)), _pad_row(_lanes_4_7(dt_)),
             jnp.zeros((7, D), F32)], axis=0)

    ws = pack_small(norm_ffn1, norm_mix, norm_ffn2, final_norm, b_ada, pool_scale, gdn_norm, a_log, dt_bias)
    ms = pack_small(m_norm_ffn1, m_norm_mix, m_norm_ffn2, m_final_norm, m_b_ada, m_pool_scale, m_gdn_norm, m_a_log, m_dt_bias)
    vs = pack_small(v_norm_ffn1, v_norm_mix, v_norm_ffn2, v_final_norm, v_b_ada, v_pool_scale, v_gdn_norm, v_a_log, v_dt_bias)
    sm = _adamw_small(small_all, ws, ms, vs, "adamw_small")
    pw2 = lambda t: t.reshape(NG * 128, 128)
    upd_pw = _adamw_small(dpw_all, pw2(pool_w), pw2(m_pool_w), pw2(v_pool_w), "adamw_pool_w")
    csh = 3 * GW // 4
    gconv = lax.dynamic_slice(sm[0][17:23].reshape(4, 3 * GW), (0, chip * csh), (4, csh))
    upd["conv_w"] = _adamw([gconv], conv_w[0], m_conv_w[0], v_conv_w[0], "adamw_conv")

    def small_out(k):
        t = sm[k]
        return {
            "norm_ffn1": t[0:1], "norm_mix": t[1:2], "norm_ffn2": t[2:3], "final_norm": t[3],
            "b_ada": t[4:4 + NMOD].reshape(1, NMOD * D), "pool_scale": t[13:14, :PW], "gdn_norm": t[14:15, :DH],
            "a_log": t[15:16, NH:2 * NH], "dt_bias": t[16:17, NH:2 * NH],
        }

    order = ["w_ada", "b_ada", "norm_ffn1", "ffn1_gate", "ffn1_up", "ffn1_down", "norm_mix", "w_in", "conv_w", "a_log",
             "dt_bias", "gdn_norm", "pool_w", "pool_scale", "w_out", "norm_ffn2", "ffn2_gate", "ffn2_up", "ffn2_down",
             "final_norm"]
    outs = [loss, dx0[None]]
    for k in range(4):
        smk = small_out(k)
        for nm in order:
            if nm in upd:
                outs.append(upd[nm][k][None])
            elif nm == "pool_w":
                outs.append(upd_pw[k].reshape(1, NG, 128, 128))
            else:
                outs.append(smk[nm])
    return tuple(outs)
```

```python
import functools

import jax
import jax.numpy as jnp
from jax import lax
from jax.experimental import pallas as pl
from jax.experimental.pallas import tpu as pltpu

F32 = jnp.float32
BF = jnp.bfloat16

D = 1024
FF = 2816
FH = FF // 2
NH = 4
DH = 128
GW = NH * DH
CH = 64
PW = 512
NG = 4
POOL_WINDOWS = (2, 4, 8, 16)
HALO = 16
DIN = 4 * GW + 2 * NH + PW
DINP = 3 * GW + GW + PW + 128
EPS = 1e-6
ADAM_LR, ADAM_B1, ADAM_B2, ADAM_EPS, ADAM_WD, ADAM_STEP = 0.001, 0.9, 0.999, 1e-08, 0.01, 10

VMEM_LIMIT = 56 * 1024 * 1024

NT_DIMS = (((1,), (1,)), ((), ()))
TN_DIMS = (((0,), (0,)), ((), ()))
HI = lax.Precision.HIGHEST


def _nt(a, b, **kw):
    return lax.dot_general(a, b, NT_DIMS, preferred_element_type=F32, **kw)


def _tn(a, b, **kw):
    return lax.dot_general(a, b, TN_DIMS, preferred_element_type=F32, **kw)


def _nn(a, b, **kw):
    return jnp.dot(a, b, preferred_element_type=F32, **kw)


def _cparams(sem=("arbitrary",), **kw):
    return pltpu.CompilerParams(dimension_semantics=sem, vmem_limit_bytes=VMEM_LIMIT, **kw)


def _const_spec(shape):
    nd = len(shape)
    return pl.BlockSpec(shape, lambda *_: (0,) * nd, pipeline_mode=pl.Buffered(1))


def _row_spec(tm, width):
    return pl.BlockSpec((tm, width), lambda i: (i, 0))


def _sum8(v):
    return jnp.sum(v.reshape(v.shape[0] // 8, 8, v.shape[1]), axis=0)


def _sigmoid(v):
    return 1.0 / (1.0 + jnp.exp(-v))


def _tile(T, cap=512):
    return min(cap, T)


def _norm_mod_fwd(xv, gain, shift, scale):
    r = lax.rsqrt(jnp.mean(xv * xv, axis=-1, keepdims=True) + EPS)
    n = xv * r
    y = n * gain
    return n, r, y, y * (1.0 + scale) + shift


def _norm_mod_bwd(dh, n, r, y, gain, scale):
    dy = dh * (1.0 + scale)
    dn = dy * gain
    dx = r * (dn - n * jnp.mean(dn * n, axis=-1, keepdims=True))
    return dx, _sum8(dh), _sum8(dh * y), _sum8(dy * n)


def _ffn_wspecs(k0):
    return [pl.BlockSpec((4, FF // 4, D), lambda i, k=k0 + n: (0, k, 0), pipeline_mode=pl.Buffered(1)) for n in range(3)]


def _half(w_ref, j):
    return w_ref[2 * j:2 * j + 2].reshape(FH, D)


def _ffn_fwd(x, shift, scale, gate, gain, wall, k0, name, gather_block=None):
    T = x.shape[0]
    tm = _tile(T)
    nt = T // tm
    comm = gather_block is not None

    def body(*refs):
        x_ref, sh_ref, sc_ref, gt_ref, gn_ref, wg_ref, wu_ref, wd_ref = refs[:8]
        if comm:
            xo_ref, f_ref, a_ref, b_ref = refs[9:13]
            plan = _SplitGather(refs[8], refs[13], *refs[14:])
            _comm_begin(pl.program_id(0), plan, nt - 1)
        else:
            xo_ref, f_ref, a_ref, b_ref = refs[8:12]
        xv = x_ref[...]
        _, _, _, h = _norm_mod_fwd(xv, gn_ref[...], sh_ref[...], sc_ref[...])
        hb = h.astype(BF)
        facc = jnp.zeros((tm, D), F32)
        for j in range(2):
            a = _nt(hb, _half(wg_ref, j))
            b = _nt(hb, _half(wu_ref, j))
            a_ref[:, j * FH:(j + 1) * FH] = a.astype(BF)
            b_ref[:, j * FH:(j + 1) * FH] = b.astype(BF)
            s = (a * _sigmoid(a) * b).astype(BF)
            facc = facc + _nn(s, _half(wd_ref, j))
        f_ref[...] = facc
        xo_ref[...] = xv + 0.5 * gt_ref[...] * facc
        if comm:
            _comm_end(pl.program_id(0), plan, nt - 1)

    vec = _const_spec((1, D))
    extra_in = [gather_block] if comm else []
    return pl.pallas_call(
        body, grid=(nt,), name=name,
        in_specs=[_row_spec(tm, D), vec, vec, vec, vec] + _ffn_wspecs(k0) + [ANY] * comm,
        out_specs=[_row_spec(tm, D), _row_spec(tm, D), _row_spec(tm, FF), _row_spec(tm, FF)] + [ANY] * comm,
        out_shape=[jax.ShapeDtypeStruct((T, D), F32), jax.ShapeDtypeStruct((T, D), F32),
                   jax.ShapeDtypeStruct((T, FF), BF), jax.ShapeDtypeStruct((T, FF), BF)]
        + ([jax.ShapeDtypeStruct((4,) + gather_block.shape, gather_block.dtype)] if comm else []),
        scratch_shapes=_gather_sems() if comm else [],
        compiler_params=_cparams(),
    )(x, shift, scale, gate, gain, wall, wall, wall, *extra_in)


def _ffn_dgrad(dxo, x, f, a, b, shift, scale, gate, gain, wall, k0, name, scatter=()):
    T = x.shape[0]
    tm = _tile(T, 256)
    nt = T // tm
    ns = len(scatter)

    def body(*refs):
        dxo_ref, x_ref, f_ref, a_ref, b_ref, sh_ref, sc_ref, gt_ref, gn_ref, wg_ref, wu_ref, wd_ref = refs[:12]
        dx_ref, da_ref, db_ref, s_ref, h_ref, df_ref, dsh_ref, dsc_ref, dgt_ref, dgn_ref = refs[12 + ns:22 + ns]
        i = pl.program_id(0)
        if ns:
            plan = _Scatter(refs[12:12 + ns], refs[22 + ns:22 + 2 * ns], *refs[22 + 2 * ns:])
            _comm_begin(i, plan, nt - 1)
        dxo_v = dxo_ref[...]
        xv = x_ref[...]
        gain_v, scale_v = gn_ref[...], sc_ref[...]
        n, r, y, h = _norm_mod_fwd(xv, gain_v, sh_ref[...], scale_v)
        h_ref[...] = h.astype(BF)
        dgate = _sum8(0.5 * f_ref[...] * dxo_v)
        dfb = (0.5 * gt_ref[...] * dxo_v).astype(BF)
        df_ref[...] = dfb
        dh = jnp.zeros((tm, D), F32)
        for j in range(2):
            cols = slice(j * FH, (j + 1) * FH)
            ds = _nt(dfb, _half(wd_ref, j))
            av = a_ref[:, cols].astype(F32)
            bv = b_ref[:, cols].astype(F32)
            sig = _sigmoid(av)
            sa = av * sig
            da = (ds * bv * (sig * (1.0 + av * (1.0 - sig)))).astype(BF)
            db = (ds * sa).astype(BF)
            da_ref[:, cols] = da
            db_ref[:, cols] = db
            s_ref[:, cols] = (sa * bv).astype(BF)
            dh = dh + _nn(da, _half(wg_ref, j)) + _nn(db, _half(wu_ref, j))
        dxn, dsh, dsc, dgn = _norm_mod_bwd(dh, n, r, y, gain_v, scale_v)
        dx_ref[...] = dxo_v + dxn

        @pl.when(i == 0)
        def _():
            dsh_ref[...] = dsh
            dsc_ref[...] = dsc
            dgt_ref[...] = dgate
            dgn_ref[...] = dgn

        @pl.when(i > 0)
        def _():
            dsh_ref[...] += dsh
            dsc_ref[...] += dsc
            dgt_ref[...] += dgate
            dgn_ref[...] += dgn

        if ns:
            _comm_end(i, plan, nt - 1)

    vec = _const_spec((1, D))
    acc = pl.BlockSpec((8, D), lambda i: (0, 0))
    accs = jax.ShapeDtypeStruct((8, D), F32)
    return pl.pallas_call(
        body, grid=(nt,), name=name,
        in_specs=[_row_spec(tm, D), _row_spec(tm, D), _row_spec(tm, D), _row_spec(tm, FF), _row_spec(tm, FF),
                  vec, vec, vec, vec] + _ffn_wspecs(k0) + [ANY] * ns,
        out_specs=[_row_spec(tm, D), _row_spec(tm, FF), _row_spec(tm, FF), _row_spec(tm, FF),
                   _row_spec(tm, D), _row_spec(tm, D), acc, acc, acc, acc] + [ANY] * ns,
        out_shape=[jax.ShapeDtypeStruct((T, D), F32), jax.ShapeDtypeStruct((T, FF), BF),
                   jax.ShapeDtypeStruct((T, FF), BF), jax.ShapeDtypeStruct((T, FF), BF),
                   jax.ShapeDtypeStruct((T, D), BF), jax.ShapeDtypeStruct((T, D), BF), accs, accs, accs, accs]
        + [jax.ShapeDtypeStruct(t.shape, t.dtype) for t in scatter],
        scratch_shapes=_scatter_sems(ns) if ns else [],
        compiler_params=_cparams(),
    )(dxo, x, f, a, b, shift, scale, gate, gain, wall, wall, wall, *scatter)


def _ffn_wgrad(da, db, s, h, df, name, scatter=()):
    T = h.shape[0]
    tk = _tile(T, 512)
    nk = T // tk
    ns = len(scatter)

    def body(*refs):
        da_ref, db_ref, s_ref, h_ref, df_ref = refs[:5]
        og_ref, ou_ref, od_ref = refs[5 + ns:8 + ns]
        ag, au, ad = refs[8 + 2 * ns:11 + 2 * ns]
        k = pl.program_id(1)
        step = pl.program_id(0) * nk + k
        if ns:
            plan = _Scatter(refs[5:5 + ns], refs[8 + ns:8 + 2 * ns], *refs[11 + 2 * ns:])
            _comm_begin(step, plan, 2 * nk - 1)
        @pl.when(k == 0)
        def _():
            for acc in (ag, au, ad):
                acc[...] = jnp.zeros((FH, D), F32)

        ag[...] += _tn(da_ref[...], h_ref[...])
        au[...] += _tn(db_ref[...], h_ref[...])
        ad[...] += _tn(s_ref[...], df_ref[...])

        @pl.when(k == nk - 1)
        def _():
            og_ref[...] = ag[...].astype(BF)
            ou_ref[...] = au[...].astype(BF)
            od_ref[...] = ad[...].astype(BF)

        if ns:
            _comm_end(step, plan, 2 * nk - 1)

    colblk = pl.BlockSpec((tk, FH), lambda j, k: (k, j))
    rowblk = pl.BlockSpec((tk, D), lambda j, k: (k, 0))
    outblk = pl.BlockSpec((FH, D), lambda j, k: (j, 0))
    outs = jax.ShapeDtypeStruct((FF, D), BF)
    return pl.pallas_call(
        body, grid=(2, nk), name=name,
        in_specs=[colblk, colblk, colblk, rowblk, rowblk] + [ANY] * ns,
        out_specs=[outblk, outblk, outblk] + [ANY] * ns,
        out_shape=[outs, outs, outs] + [jax.ShapeDtypeStruct(t.shape, t.dtype) for t in scatter],
        scratch_shapes=[pltpu.VMEM((FH, D), F32)] * 3 + (_scatter_sems(ns) if ns else []),
        compiler_params=_cparams(("arbitrary", "arbitrary")),
    )(da, db, s, h, df, *scatter)


def _loss_head(x, target, gain):
    T = x.shape[0]
    tm = _tile(T)

    def body(x_ref, t_ref, gn_ref, ls_ref, dx_ref, dgn_ref):
        i = pl.program_id(0)
        xv = x_ref[...]
        gain_v = gn_ref[...]
        r = lax.rsqrt(jnp.mean(xv * xv, axis=-1, keepdims=True) + EPS)
        n = xv * r
        err = n * gain_v - t_ref[...]
        e2 = err * err
        part = e2[:, 0:128]
        for q in range(1, D // 128):
            part = part + e2[:, q * 128:(q + 1) * 128]
        lsum = _sum8(part) * (0.5 / D)
        dy = err * (1.0 / D)
        dn = dy * gain_v
        dx_ref[...] = r * (dn - n * jnp.mean(dn * n, axis=-1, keepdims=True))
        dgn = _sum8(dy * n)

        @pl.when(i == 0)
        def _():
            ls_ref[...] = lsum
            dgn_ref[...] = dgn

        @pl.when(i > 0)
        def _():
            ls_ref[...] += lsum
            dgn_ref[...] += dgn

    return pl.pallas_call(
        body, grid=(T // tm,), name="loss_head",
        in_specs=[_row_spec(tm, D), _row_spec(tm, D), _const_spec((1, D))],
        out_specs=[pl.BlockSpec((8, 128), lambda i: (0, 0)), _row_spec(tm, D), pl.BlockSpec((8, D), lambda i: (0, 0))],
        out_shape=[jax.ShapeDtypeStruct((8, 128), F32), jax.ShapeDtypeStruct((T, D), F32),
                   jax.ShapeDtypeStruct((8, D), F32)],
        compiler_params=_cparams(),
    )(x, target, gain)


def _seg_cumsum(v, row_in_chunk, reverse=False):
    n = v.shape[0]
    s = 1
    while s < CH:
        if reverse:
            moved = pltpu.roll(v, n - s, 0)
            ok = row_in_chunk < CH - s
        else:
            moved = pltpu.roll(v, s, 0)
            ok = row_in_chunk >= s
        v = v + jnp.where(ok, moved, 0.0)
        s *= 2
    return v


def _conv_silu(xq_ext_ref, cw_ref, tm):
    c = cw_ref[0:1, :] * xq_ext_ref[pl.ds(5, tm), :]
    for j in range(1, 4):
        c = c + cw_ref[j:j + 1, :] * xq_ext_ref[pl.ds(5 + j, tm), :]
    return c, _sigmoid(c)


def _gates(ba, alog, dtb, lane):
    beta = _sigmoid(ba)
    arg = ba + dtb
    softplus = jnp.maximum(arg, 0.0) + jnp.log(1.0 + jnp.exp(-jnp.abs(arg)))
    g = -jnp.exp(alog) * softplus
    return jnp.where(lane < NH, beta, 0.0), jnp.where((lane >= NH) & (lane < 2 * NH), g, 0.0), _sigmoid(arg)


def _mix_proj(x, shift, scale, gain, winT, conv_w, alog, dtb):
    T = x.shape[0]
    tm = _tile(T)

    def body(x_ref, sh_ref, sc_ref, gn_ref, w_ref, cw_ref, al_ref, dt_ref,
             xq_ref, ba_ref, qn_ref, kn_ref, v_ref, z_ref, p_ref, bg_ref, gc_ref, ext):
        i = pl.program_id(0)
        _, _, _, h = _norm_mod_fwd(x_ref[...], gn_ref[...], sh_ref[...], sc_ref[...])
        hb = h.astype(BF)

        @pl.when(i == 0)
        def _():
            ext[pl.ds(0, 8), :] = jnp.zeros((8, 3 * GW), F32)

        xq = _nt(hb, w_ref[pl.ds(0, 3 * GW), :])
        xq_ref[...] = xq
        ext[pl.ds(8, tm), :] = xq
        z_ref[...] = _nt(hb, w_ref[pl.ds(3 * GW, GW), :])
        p_ref[...] = _nt(hb, w_ref[pl.ds(4 * GW, PW), :])
        ba = _nt(hb, w_ref[pl.ds(4 * GW + PW, 128), :])
        ba_ref[...] = ba

        c, sg = _conv_silu(ext, cw_ref, tm)
        ext[pl.ds(0, 8), :] = ext[pl.ds(tm, 8), :]
        qt = c * sg
        for hd in range(NH):
            cq = slice(hd * DH, (hd + 1) * DH)
            ck = slice(GW + hd * DH, GW + (hd + 1) * DH)
            qh, kh = qt[:, cq], qt[:, ck]
            qn_ref[:, cq] = qh * (lax.rsqrt(jnp.sum(qh * qh, axis=-1, keepdims=True) + EPS) * DH ** -0.5)
            kn_ref[:, cq] = kh * lax.rsqrt(jnp.sum(kh * kh, axis=-1, keepdims=True) + EPS)
        v_ref[...] = qt[:, 2 * GW:3 * GW]

        lane = lax.broadcasted_iota(jnp.int32, (tm, 128), 1)
        row = lax.broadcasted_iota(jnp.int32, (tm, 128), 0) % CH
        beta, g, _ = _gates(ba, al_ref[...], dt_ref[...], lane)
        bg_ref[...] = beta + g
        gc_ref[...] = _seg_cumsum(g, row)

    wide = lambda w: _row_spec(tm, w)
    shp = lambda w: jax.ShapeDtypeStruct((T, w), F32)
    return pl.pallas_call(
        body, grid=(T // tm,), name="mix_proj",
        in_specs=[wide(D), _const_spec((1, D)), _const_spec((1, D)), _const_spec((1, D)), _const_spec((DINP, D)),
                  _const_spec((4, 3 * GW)), _const_spec((1, 128)), _const_spec((1, 128))],
        out_specs=[wide(3 * GW), wide(128), wide(GW), wide(GW), wide(GW), wide(GW), wide(PW), wide(128), wide(128)],
        out_shape=[shp(3 * GW), shp(128), shp(GW), shp(GW), shp(GW), shp(GW), shp(PW), shp(128), shp(128)],
        scratch_shapes=[pltpu.VMEM((tm + 8, 3 * GW), F32)],
        compiler_params=_cparams(),
    )(x, shift, scale, gain, winT, conv_w, alog, dtb)


R2 = 2 * CH
TRI_PREC = None


def _tmm(fn, a, b):
    if TRI_PREC is None:
        return fn(a.astype(BF), b.astype(BF))
    return fn(a, b, precision=TRI_PREC)


def _pair_consts():
    ii = lax.broadcasted_iota(jnp.int32, (R2, R2), 0)
    jj = lax.broadcasted_iota(jnp.int32, (R2, R2), 1)
    same = (ii < CH) == (jj < CH)
    r = lax.broadcasted_iota(jnp.int32, (R2, 1), 0)
    return dict(causal=same & (ii >= jj), strict=same & (ii > jj), eye=(ii == jj).astype(F32), rowA=r < CH,
                last=(r == CH - 1) | (r == R2 - 1),
                rowS=lax.broadcasted_iota(jnp.int32, (2 * DH, 1), 0) < DH)


def _tri_inverse_many(ms, eye):
    pws = [-m for m in ms]
    ts = [eye + p for p in pws]
    for _ in range(5):
        pws = [_tmm(_nn, p, p) for p in pws]
        ts = [_tmm(_nn, t, eye + p) for t, p in zip(ts, pws)]
    return ts


def _egl_rows(gl):
    egl = jnp.exp(gl)
    return egl, jnp.concatenate([jnp.broadcast_to(egl[0:1], (DH, 1)), jnp.broadcast_to(egl[CH:CH + 1], (DH, 1))], axis=0)


def _pair_intra(items, cn):
    causal, rowA = cn["causal"], cn["rowA"]

    def bd(t):
        return jnp.concatenate([jnp.where(rowA, t, 0.0), jnp.where(rowA, 0.0, t)], axis=1).astype(BF)

    outs = []
    for q, k, v, beta, gcv, gl in items:
        gc_b = jnp.broadcast_to(gcv, (R2, R2))
        gam = jnp.where(causal, jnp.exp(jnp.where(causal, gc_b - gc_b.T, 0.0)), 0.0)
        kb = k * beta
        kbf = k.astype(BF)
        P = _nt(kb.astype(BF), kbf)
        QK = _nt(q.astype(BF), kbf)
        E = jnp.exp(gcv)
        outs.append(dict(gam=gam, kb=kb, vb=v * beta, P=P, QK=QK, E=E, Fd=jnp.exp(gl - gcv), kbE=kb * E,
                         Q=QK * gam, qE_bd=bd(q * E)))
    tms = _tri_inverse_many([jnp.where(cn["strict"], d["P"] * d["gam"], 0.0) for d in outs], cn["eye"])
    for d, tm_, (q, k, v, beta, gcv, gl) in zip(outs, tms, items):
        d["Tm"] = tm_
        d["u"] = _tmm(_nn, tm_, d["vb"])
        d["w_bd"] = bd(_tmm(_nn, tm_, d["kbE"]))
        d["kF_bd"] = bd(k * d["Fd"])
    return outs


def _pair_scan(it, S, egl_st):
    Sb = S.astype(BF)
    vn = it["u"] - _nn(it["w_bd"], Sb)
    vnb = vn.astype(BF)
    o = _nn(it["qE_bd"], Sb) + _nn(it["Q"].astype(BF), vnb)
    return vnb, o, S * egl_st + _tn(it["kF_bd"], vnb)


def _pair_forward(q, k, v, beta, gcv, gl, S, cn):
    fw = _pair_intra([(q, k, v, beta, gcv, gl)], cn)[0]
    fw["egl"], fw["egl_st"] = _egl_rows(gl)
    fw["vnb"], fw["o"], fw["S_new"] = _pair_scan(fw, S, fw["egl_st"])
    return fw


def _stack_heads(ref, rows, pair):
    return jnp.concatenate([ref[rows, (2 * pair) * DH:(2 * pair + 1) * DH],
                            ref[rows, (2 * pair + 1) * DH:(2 * pair + 2) * DH]], axis=0)


def _stack_cols(val, lane_a, lane_b, bcast_rows=None):
    a, b = val[:, lane_a:lane_a + 1], val[:, lane_b:lane_b + 1]
    if bcast_rows:
        a, b = jnp.broadcast_to(a, (bcast_rows, 1)), jnp.broadcast_to(b, (bcast_rows, 1))
    return jnp.concatenate([a, b], axis=0)


def _pool_windows(ext, tm, reverse):
    n = tm + HALO
    outs = []
    for gi in range(NG):
        a = ext[:, gi * 128:(gi + 1) * 128]
        s = 1
        while s < POOL_WINDOWS[gi]:
            a = a + pltpu.roll(a, (n - s) if reverse else s, 0)
            s *= 2
        outs.append(a[0:tm] if reverse else a[HALO:HALO + tm])
    return jnp.concatenate(outs, axis=1)


def _pool_count(tm, tile_index):
    t1 = (lax.broadcasted_iota(jnp.int32, (tm, PW), 0) + tile_index * tm + 1).astype(F32)
    win = jnp.concatenate([jnp.full((tm, 128), float(w), F32) for w in POOL_WINDOWS], axis=1)
    return 1.0 / jnp.minimum(t1, win)


def _chunk_item(qn_ref, kn_ref, v_ref, bg_ref, gc_ref, c, pr):
    r0 = pl.multiple_of(c * CH, CH)
    rows = pl.ds(r0, CH)
    bgv = bg_ref[rows, :]
    gcv_all = gc_ref[rows, :]
    gl_all = gc_ref[pl.ds(r0 + CH - 1, 1), :]
    ha, hb = 2 * pr, 2 * pr + 1
    return (_stack_heads(qn_ref, rows, pr), _stack_heads(kn_ref, rows, pr), _stack_heads(v_ref, rows, pr),
            _stack_cols(bgv, ha, hb), _stack_cols(gcv_all, NH + ha, NH + hb), _stack_cols(gl_all, NH + ha, NH + hb, CH))


CHUNK_GROUP = 4


def _mix_core(x, gate, qn, kn, v, z, p, bg, gc, gnorm, pool_w, pool_scale, w_out):
    T = x.shape[0]
    tm = _tile(T)
    nc = tm // CH
    cg = CHUNK_GROUP if nc % CHUNK_GROUP == 0 else 1
    npb = nc * (NH // 2)

    def body(x_ref, gt_ref, qn_ref, kn_ref, v_ref, z_ref, p_ref, bg_ref, gc_ref, gnm_ref, pw_ref, ps_ref, wo_ref,
             xo_ref, mx_ref, cat_ref, o_ref, sall_ref, S_scr, pext, u_s, w_s, qe_s, kf_s, q_s):
        i = pl.program_id(0)

        @pl.when(i == 0)
        def _():
            S_scr[...] = jnp.zeros((NH * DH, DH), F32)
            pext[pl.ds(0, HALO), :] = jnp.zeros((HALO, PW), F32)

        cn = _pair_consts()

        def intra(g, carry):
            idx = [(g * cg + dc, pr) for dc in range(cg) for pr in range(NH // 2)]
            res = _pair_intra([_chunk_item(qn_ref, kn_ref, v_ref, bg_ref, gc_ref, c, pr) for c, pr in idx], cn)
            for (c, pr), d in zip(idx, res):
                pi = c * (NH // 2) + pr
                u_s[pi] = d["u"]
                w_s[pi] = d["w_bd"]
                qe_s[pi] = d["qE_bd"]
                kf_s[pi] = d["kF_bd"]
                q_s[pi] = d["Q"].astype(BF)
            return carry

        lax.fori_loop(0, nc // cg, intra, 0)

        def scan(c, carry):
            r0 = pl.multiple_of(c * CH, CH)
            rows = pl.ds(r0, CH)
            gl_all = gc_ref[pl.ds(r0 + CH - 1, 1), :]
            for pr in range(NH // 2):
                ha, hb = 2 * pr, 2 * pr + 1
                pi = c * (NH // 2) + pr
                S = S_scr[pl.ds(pr * 2 * DH, 2 * DH), :]
                sall_ref[c, ha:hb + 1] = S.reshape(2, DH, DH)
                _, egl_st = _egl_rows(_stack_cols(gl_all, NH + ha, NH + hb, CH))
                it = dict(u=u_s[pi], w_bd=w_s[pi], qE_bd=qe_s[pi], kF_bd=kf_s[pi], Q=q_s[pi])
                _, o, S_new = _pair_scan(it, S, egl_st)
                o_ref[rows, ha * DH:(ha + 1) * DH] = o[0:CH]
                o_ref[rows, hb * DH:(hb + 1) * DH] = o[CH:R2]
                S_scr[pl.ds(pr * 2 * DH, 2 * DH), :] = S_new
            return carry

        lax.fori_loop(0, nc, scan, 0)

        for hd in range(NH):
            cols = slice(hd * DH, (hd + 1) * DH)
            oh = o_ref[:, cols]
            zh = z_ref[:, cols]
            r = lax.rsqrt(jnp.mean(oh * oh, axis=-1, keepdims=True) + EPS)
            cat_ref[:, cols] = (oh * r * gnm_ref[...] * (zh * _sigmoid(zh))).astype(BF)

        pv = p_ref[...]
        pext[pl.ds(HALO, tm), :] = pv
        pooled = _pool_windows(pext[...], tm, False) * _pool_count(tm, i) - pv
        pext[pl.ds(0, HALO), :] = pext[pl.ds(tm, HALO), :]
        for gi in range(NG):
            cols = slice(gi * 128, (gi + 1) * 128)
            pm = _nn(pooled[:, cols].astype(BF), pw_ref[gi])
            cat_ref[:, GW + gi * 128:GW + (gi + 1) * 128] = (pm * ps_ref[:, cols]).astype(BF)

        mixed = _nn(cat_ref[...], wo_ref[...])
        mx_ref[...] = mixed
        xo_ref[...] = x_ref[...] + gt_ref[...] * mixed

    wide = lambda w: _row_spec(tm, w)
    return pl.pallas_call(
        body, grid=(T // tm,), name="mix_core",
        in_specs=[wide(D), _const_spec((1, D)), wide(GW), wide(GW), wide(GW), wide(GW), wide(PW), wide(128), wide(128),
                  _const_spec((1, DH)), _const_spec((NG, 128, 128)), _const_spec((1, PW)), _const_spec((D, D))],
        out_specs=[wide(D), wide(D), wide(D), wide(GW), pl.BlockSpec((nc, NH, DH, DH), lambda i: (i, 0, 0, 0))],
        out_shape=[jax.ShapeDtypeStruct((T, D), F32), jax.ShapeDtypeStruct((T, D), F32),
                   jax.ShapeDtypeStruct((T, D), BF), jax.ShapeDtypeStruct((T, GW), F32),
                   jax.ShapeDtypeStruct((T // CH, NH, DH, DH), F32)],
        scratch_shapes=[pltpu.VMEM((NH * DH, DH), F32), pltpu.VMEM((tm + HALO, PW), F32),
                        pltpu.VMEM((npb, R2, DH), F32), pltpu.VMEM((npb, R2, 2 * DH), BF),
                        pltpu.VMEM((npb, R2, 2 * DH), BF), pltpu.VMEM((npb, R2, 2 * DH), BF),
                        pltpu.VMEM((npb, R2, R2), BF)],
        compiler_params=_cparams(),
    )(x, gate, qn, kn, v, z, p, bg, gc, gnorm, pool_w, pool_scale, w_out)


def _pair_scan_bwd(it, S, dSn, do, egl, egl_st, cn):
    bf = lambda t: t.astype(BF)
    rowA, rowS = cn["rowA"], cn["rowS"]
    sel = lambda t: jnp.where(rowA, t[:, 0:DH], t[:, DH:2 * DH])
    Sb, dSb, dob = bf(S), bf(dSn), bf(do)
    vnb = bf(it["u"] - _nn(it["w_bd"], Sb))
    dvn = _tn(bf(it["Q"]), dob) + _nn(it["kF_bd"], dSb)
    dQ = _nt(dob, vnb)
    dqE = sel(_nt(dob, Sb))
    dkF = sel(_nt(vnb, dSb))
    dvnb = bf(dvn)
    dw = -sel(_nt(dvnb, Sb))
    dS_new = _tn(it["qE_bd"], dob) + egl_st * dSn - _tn(it["w_bd"], dvnb)
    prod = jnp.sum(dSn * S, axis=1, keepdims=True)
    d_egl_a = jnp.sum(jnp.where(rowS, prod, 0.0), axis=0, keepdims=True)
    d_egl_b = jnp.sum(jnp.where(rowS, 0.0, prod), axis=0, keepdims=True)
    return dict(dvn=dvn, dQ=dQ, dqE=dqE, dkF=dkF, dw=dw, degl=jnp.where(rowA, d_egl_a, d_egl_b) * egl), dS_new


def _pair_intra_bwd(items, cn):
    bf = lambda t: t.astype(BF)
    rowA = cn["rowA"]
    for d in items:
        d["kb"] = d["k"] * d["beta"]
        d["E"] = jnp.exp(d["gcv"])
        d["Fd"] = jnp.exp(d["gl"] - d["gcv"])
        d["dvb"] = _tmm(_tn, d["Tm"], d["dvn"])
        d["dkbE"] = _tmm(_tn, d["Tm"], d["dw"])
        dTm = _tmm(_nt, d["dvn"], d["v"] * d["beta"]) + _tmm(_nt, d["dw"], d["kb"] * d["E"])
        d["X"] = _tmm(_nt, dTm, d["Tm"])
    for d in items:
        d["dA"] = -_tmm(_tn, d["Tm"], d["X"])
    outs = []
    for d in items:
        q, k, v, beta, gam = d["q"], d["k"], d["v"], d["beta"], d["gam"]
        N = jnp.where(cn["strict"], d["dA"] * gam, 0.0)
        Rm = d["dQ"] * gam
        Wm = Rm * d["QK"] + N * d["P"]
        dgc = jnp.sum(Wm, axis=1, keepdims=True) - jnp.sum(Wm.T, axis=1, keepdims=True)
        kbf, Nb, Rb = bf(k), bf(N), bf(Rm)
        E, Fd = d["E"], d["Fd"]
        dq = d["dqE"] * E + _nn(Rb, kbf)
        dkb = d["dkbE"] * E + _nn(Nb, kbf)
        dk = d["dkF"] * Fd + _tn(Rb, bf(q)) + _tn(Nb, bf(d["kb"])) + beta * dkb
        dbeta = jnp.sum(dkb * k, axis=1, keepdims=True) + jnp.sum(d["dvb"] * v, axis=1, keepdims=True)
        dE = jnp.sum(d["dqE"] * q, axis=1, keepdims=True) + jnp.sum(d["dkbE"] * d["kb"], axis=1, keepdims=True)
        fdf = jnp.sum(d["dkF"] * k, axis=1, keepdims=True) * Fd
        dgl = d["degl"] + jnp.where(rowA, jnp.sum(jnp.where(rowA, fdf, 0.0), axis=0, keepdims=True),
                                    jnp.sum(jnp.where(rowA, 0.0, fdf), axis=0, keepdims=True))
        dgc = dgc + dE * E - fdf + jnp.where(cn["last"], dgl, 0.0)
        outs.append((dq, dk, beta * d["dvb"], dbeta, dgc))
    return outs


def _mix_core_bwd(dxo, gate, mixed, cat, o, sall, qn, kn, v, z, p, bg, gc, gnorm, pool_w, pool_scale, w_out,
                  scatter=()):
    T = dxo.shape[0]
    tm = _tile(T, 256)
    nt = T // tm
    nc = tm // CH
    ns = len(scatter)
    cg = CHUNK_GROUP if nc % CHUNK_GROUP == 0 else 1
    npb = nc * (NH // 2)

    def body(*refs):
        (dx_ref, gt_ref, mx_ref, cat_ref, o_ref, sall_ref, qn_ref, kn_ref, v_ref, z_ref, p_ref, ph_ref, bg_ref,
         gc_ref, gnm_ref, pw_ref, ps_ref, wo_ref) = refs[:18]
        (dq_ref, dk_ref, dv_ref, dz_ref, dp_ref, dbg_ref, dgt_ref, dwo_ref, dpw_ref, dps_ref,
         dgn_ref) = refs[18 + ns:29 + ns]
        dS_scr, pext, yext, do_buf, dwo_acc = refs[29 + 2 * ns:34 + 2 * ns]
        (gam_s, p_s, qk_s, tm_s, dqq_s, u_s, dvn_s, dqe_s, dkf_s, dw_s, w_s, qe_s, kf_s, q_s,
         degl_s) = refs[34 + 2 * ns:49 + 2 * ns]
        i = pl.program_id(0)
        ti = nt - 1 - i
        if ns:
            plan = _Scatter(refs[18:18 + ns], refs[29 + ns:29 + 2 * ns], *refs[49 + 2 * ns:])
            _comm_begin(i, plan, nt - 1)

        @pl.when(i == 0)
        def _():
            dS_scr[...] = jnp.zeros((NH * DH, DH), F32)
            yext[pl.ds(tm, HALO), :] = jnp.zeros((HALO, PW), F32)
            dwo_acc[...] = jnp.zeros((D, D), F32)
            dgt_ref[...] = jnp.zeros((8, D), F32)
            dpw_ref[...] = jnp.zeros((NG, 128, 128), F32)
            dps_ref[...] = jnp.zeros((8, PW), F32)
            dgn_ref[...] = jnp.zeros((8, DH), F32)

        dx2 = dx_ref[...]
        dgt_ref[...] += _sum8(mx_ref[...] * dx2)
        dmix = (gt_ref[...] * dx2).astype(BF)
        dcat = _nt(dmix, wo_ref[...])
        dwo_acc[...] += _tn(cat_ref[...], dmix)

        pv = p_ref[...]
        pext[pl.ds(0, HALO), :] = jnp.where(ti == 0, 0.0, ph_ref[...])
        pext[pl.ds(HALO, tm), :] = pv
        inv_cnt = _pool_count(tm, ti)
        pooled = _pool_windows(pext[...], tm, False) * inv_cnt - pv
        dpooled = []
        for gi in range(NG):
            cols = slice(gi * 128, (gi + 1) * 128)
            pgb = pooled[:, cols].astype(BF)
            pm = _nn(pgb, pw_ref[gi])
            dpo = dcat[:, GW + gi * 128:GW + (gi + 1) * 128]
            dps_ref[:, cols] += _sum8(dpo * pm)
            dpm = (dpo * ps_ref[:, cols]).astype(BF)
            dpooled.append(_nt(dpm, pw_ref[gi]))
            dpw_ref[gi] += _tn(pgb, dpm)
        dpooled = jnp.concatenate(dpooled, axis=1)
        y = dpooled * inv_cnt
        yext[pl.ds(0, tm), :] = y
        dp_ref[...] = _pool_windows(yext[...], tm, True) - dpooled
        yext[pl.ds(tm, HALO), :] = y[0:HALO]

        gnm = gnm_ref[...]
        dgn = jnp.zeros((8, DH), F32)
        for hd in range(NH):
            cols = slice(hd * DH, (hd + 1) * DH)
            oh = o_ref[:, cols]
            zh = z_ref[:, cols]
            r = lax.rsqrt(jnp.mean(oh * oh, axis=-1, keepdims=True) + EPS)
            n = oh * r
            sg = _sigmoid(zh)
            zs = zh * sg
            dgo = dcat[:, cols]
            dgn = dgn + _sum8(dgo * zs * n)
            dn = dgo * zs * gnm
            do_buf[:, cols] = r * (dn - n * jnp.mean(dn * n, axis=-1, keepdims=True))
            dz_ref[:, cols] = dgo * n * gnm * (sg * (1.0 + zh * (1.0 - sg)))
        dgn_ref[...] += dgn

        cn = _pair_consts()
        lane_c = lax.broadcasted_iota(jnp.int32, (CH, 128), 1)

        def intra(g, carry):
            idx = [(g * cg + dc, pr) for dc in range(cg) for pr in range(NH // 2)]
            res = _pair_intra([_chunk_item(qn_ref, kn_ref, v_ref, bg_ref, gc_ref, c, pr) for c, pr in idx], cn)
            for (c, pr), d in zip(idx, res):
                pi = c * (NH // 2) + pr
                gam_s[pi], p_s[pi], qk_s[pi], tm_s[pi], u_s[pi] = d["gam"], d["P"], d["QK"], d["Tm"], d["u"]
                w_s[pi], qe_s[pi], kf_s[pi], q_s[pi] = d["w_bd"], d["qE_bd"], d["kF_bd"], d["Q"].astype(BF)
            return carry

        lax.fori_loop(0, nc // cg, intra, 0)

        def scan(cc, carry):
            c = nc - 1 - cc
            r0 = pl.multiple_of(c * CH, CH)
            rows = pl.ds(r0, CH)
            gl_all = gc_ref[pl.ds(r0 + CH - 1, 1), :]
            for pr in range(NH // 2):
                ha, hb = 2 * pr, 2 * pr + 1
                pi = c * (NH // 2) + pr
                srows = pl.ds(pr * 2 * DH, 2 * DH)
                S = sall_ref[c, ha:hb + 1].reshape(2 * DH, DH)
                egl, egl_st = _egl_rows(_stack_cols(gl_all, NH + ha, NH + hb, CH))
                it = dict(u=u_s[pi], w_bd=w_s[pi], qE_bd=qe_s[pi], kF_bd=kf_s[pi], Q=q_s[pi])
                g, dS_new = _pair_scan_bwd(it, S, dS_scr[srows, :], _stack_heads(do_buf, rows, pr), egl, egl_st, cn)
                dvn_s[pi], dqq_s[pi], dqe_s[pi], dkf_s[pi], dw_s[pi] = g["dvn"], g["dQ"], g["dqE"], g["dkF"], g["dw"]
                degl_s[pi] = g["degl"]
                dS_scr[srows, :] = dS_new
            return carry

        lax.fori_loop(0, nc, scan, 0)

        def intra_bwd(g, carry):
            idx = [(g * cg + dc, pr) for dc in range(cg) for pr in range(NH // 2)]
            items = []
            for c, pr in idx:
                pi = c * (NH // 2) + pr
                q, k, vv, beta, gcv, gl = _chunk_item(qn_ref, kn_ref, v_ref, bg_ref, gc_ref, c, pr)
                items.append(dict(q=q, k=k, v=vv, beta=beta, gcv=gcv, gl=gl, gam=gam_s[pi], P=p_s[pi], QK=qk_s[pi],
                                  Tm=tm_s[pi], dvn=dvn_s[pi], dQ=dqq_s[pi], dqE=dqe_s[pi], dkF=dkf_s[pi], dw=dw_s[pi],
                                  degl=degl_s[pi]))
            res = _pair_intra_bwd(items, cn)
            for dc in range(cg):
                c = g * cg + dc
                rows = pl.ds(pl.multiple_of(c * CH, CH), CH)
                dbg = jnp.zeros((CH, 128), F32)
                for pr in range(NH // 2):
                    dq, dk, dv, dbeta, dgc = res[dc * (NH // 2) + pr]
                    for hd, half in ((2 * pr, slice(0, CH)), (2 * pr + 1, slice(CH, R2))):
                        cols = slice(hd * DH, (hd + 1) * DH)
                        dq_ref[rows, cols] = dq[half]
                        dk_ref[rows, cols] = dk[half]
                        dv_ref[rows, cols] = dv[half]
                        dbg = dbg + jnp.where(lane_c == hd, dbeta[half], 0.0) + jnp.where(lane_c == NH + hd, dgc[half], 0.0)
                dbg_ref[rows, :] = dbg
            return carry

        lax.fori_loop(0, nc // cg, intra_bwd, 0)

        lane = lax.broadcasted_iota(jnp.int32, (tm, 128), 1)
        row = lax.broadcasted_iota(jnp.int32, (tm, 128), 0) % CH
        dbg_all = dbg_ref[...]
        dg = _seg_cumsum(jnp.where(lane >= NH, dbg_all, 0.0), row, reverse=True)
        dbg_ref[...] = jnp.where(lane < NH, dbg_all, dg)

        @pl.when(i == nt - 1)
        def _():
            dwo_ref[...] = dwo_acc[...].astype(BF)

        if ns:
            _comm_end(i, plan, nt - 1)

    rev = lambda w: pl.BlockSpec((tm, w), lambda i: (nt - 1 - i, 0))
    halo = pl.BlockSpec((HALO, PW), lambda i: (jnp.maximum((nt - 1 - i) * (tm // HALO) - 1, 0), 0))
    shp = lambda w: jax.ShapeDtypeStruct((T, w), F32)
    fix = lambda *s: pl.BlockSpec(s, lambda i: (0,) * len(s))
    return pl.pallas_call(
        body, grid=(nt,), name="mix_core_bwd",
        in_specs=[rev(D), _const_spec((1, D)), rev(D), rev(D), rev(GW),
                  pl.BlockSpec((nc, NH, DH, DH), lambda i: (nt - 1 - i, 0, 0, 0)),
                  rev(GW), rev(GW), rev(GW), rev(GW), rev(PW), halo, rev(128), rev(128),
                  _const_spec((1, DH)), _const_spec((NG, 128, 128)), _const_spec((1, PW)), _const_spec((D, D))]
        + [ANY] * ns,
        out_specs=[rev(GW), rev(GW), rev(GW), rev(GW), rev(PW), rev(128),
                   fix(8, D), fix(D, D), fix(NG, 128, 128), fix(8, PW), fix(8, DH)] + [ANY] * ns,
        out_shape=[shp(GW), shp(GW), shp(GW), shp(GW), shp(PW), shp(128),
                   jax.ShapeDtypeStruct((8, D), F32), jax.ShapeDtypeStruct((D, D), BF),
                   jax.ShapeDtypeStruct((NG, 128, 128), F32), jax.ShapeDtypeStruct((8, PW), F32),
                   jax.ShapeDtypeStruct((8, DH), F32)] + [jax.ShapeDtypeStruct(t.shape, t.dtype) for t in scatter],
        scratch_shapes=[pltpu.VMEM((NH * DH, DH), F32), pltpu.VMEM((tm + HALO, PW), F32),
                        pltpu.VMEM((tm + HALO, PW), F32), pltpu.VMEM((tm, GW), F32), pltpu.VMEM((D, D), F32)]
        + [pltpu.VMEM((npb, R2, R2), F32)] * 5 + [pltpu.VMEM((npb, R2, DH), F32)] * 5
        + [pltpu.VMEM((npb, R2, 2 * DH), BF)] * 3 + [pltpu.VMEM((npb, R2, R2), BF), pltpu.VMEM((npb, R2, 1), F32)]
        + (_scatter_sems(ns) if ns else []),
        compiler_params=_cparams(),
    )(dxo, gate, mixed, cat, o, sall, qn, kn, v, z, p, p, bg, gc, gnorm, pool_w, pool_scale, w_out, *scatter)


def _mix_proj_bwd(dxo, dqn, dkn, dv, dz, dp, dbg, xq, ba, x, shift, scale, gain, winT, conv_w, alog, dtb):
    T = x.shape[0]
    tm = min(256, T)
    nt = T // tm
    W3 = 3 * GW

    def body(dxo_ref, dqn_ref, dkn_ref, dv_ref, dz_ref, dp_ref, dbg_ref, xq_ref, xh_ref, ba_ref, x_ref, sh_ref, sc_ref,
             gn_ref, w_ref, cw_ref, al_ref, dt_ref,
             dx_ref, dw_ref, dcw_ref, dal_ref, ddt_ref, dsh_ref, dsc_ref, dgn_ref,
             ext, dcext, dproj, dw_acc):
        i = pl.program_id(0)
        ti = nt - 1 - i

        @pl.when(i == 0)
        def _():
            dcext[pl.ds(tm, 8), :] = jnp.zeros((8, W3), F32)
            dw_acc[...] = jnp.zeros((DINP, D), F32)
            dcw_ref[...] = jnp.zeros((4, 8, W3), F32)
            dal_ref[...] = jnp.zeros((8, 128), F32)
            ddt_ref[...] = jnp.zeros((8, 128), F32)
            dsh_ref[...] = jnp.zeros((8, D), F32)
            dsc_ref[...] = jnp.zeros((8, D), F32)
            dgn_ref[...] = jnp.zeros((8, D), F32)

        ext[pl.ds(0, 8), :] = jnp.where(ti == 0, 0.0, xh_ref[...])
        ext[pl.ds(8, tm), :] = xq_ref[...]
        c, sg = _conv_silu(ext, cw_ref, tm)
        qt = c * sg
        dsilu = sg * (1.0 + c * (1.0 - sg))
        for hd in range(NH):
            for part, dref, mult in ((0, dqn_ref, DH ** -0.5), (1, dkn_ref, 1.0)):
                cols = slice(part * GW + hd * DH, part * GW + (hd + 1) * DH)
                xh = qt[:, cols]
                rr = lax.rsqrt(jnp.sum(xh * xh, axis=-1, keepdims=True) + EPS)
                unit = xh * rr
                du = dref[:, hd * DH:(hd + 1) * DH] * mult
                dxh = rr * (du - unit * jnp.sum(du * unit, axis=-1, keepdims=True))
                dcext[pl.ds(0, tm), cols] = dxh * dsilu[:, cols]
        dcext[pl.ds(0, tm), 2 * GW:W3] = dv_ref[...] * dsilu[:, 2 * GW:W3]
        dc = dcext[pl.ds(0, tm), :]
        dxq = jnp.zeros((tm, W3), F32)
        for j in range(4):
            dcw_ref[j] += _sum8(dc * ext[pl.ds(5 + j, tm), :])
            dxq = dxq + cw_ref[j:j + 1, :] * dcext[pl.ds(3 - j, tm), :]
        dcext[pl.ds(tm, 8), :] = dc[0:8]
        lane = lax.broadcasted_iota(jnp.int32, (tm, 128), 1)
        bav = ba_ref[...]
        beta, g, sarg = _gates(bav, al_ref[...], dt_ref[...], lane)
        dbg_v = dbg_ref[...]
        is_g = (lane >= NH) & (lane < 2 * NH)
        dbraw = jnp.where(lane < NH, dbg_v * beta * (1.0 - beta), 0.0)
        daraw = jnp.where(is_g, dbg_v * (-jnp.exp(al_ref[...])) * sarg, 0.0)
        dal_ref[...] += _sum8(jnp.where(is_g, dbg_v * g, 0.0))
        ddt_ref[...] += _sum8(daraw)
        dproj[:, 0:W3] = dxq.astype(BF)
        dproj[:, W3:W3 + GW] = dz_ref[...].astype(BF)
        dproj[:, W3 + GW:W3 + GW + PW] = dp_ref[...].astype(BF)
        dproj[:, W3 + GW + PW:DINP] = (dbraw + daraw).astype(BF)
        gain_v, scale_v = gn_ref[...], sc_ref[...]
        n, r, y, h = _norm_mod_fwd(x_ref[...], gain_v, sh_ref[...], scale_v)
        dpj = dproj[...]
        dh = _nn(dpj, w_ref[...])
        dw_acc[...] += _tn(dpj, h.astype(BF))
        dxn, dsh, dsc, dgn = _norm_mod_bwd(dh, n, r, y, gain_v, scale_v)
        dx_ref[...] = dxo_ref[...] + dxn
        dsh_ref[...] += dsh
        dsc_ref[...] += dsc
        dgn_ref[...] += dgn

        @pl.when(i == nt - 1)
        def _():
            dw_ref[...] = dw_acc[...].astype(BF)

    rev = lambda w: pl.BlockSpec((tm, w), lambda i: (nt - 1 - i, 0))
    halo = pl.BlockSpec((8, W3), lambda i: (jnp.maximum((nt - 1 - i) * (tm // 8) - 1, 0), 0))
    fix = lambda *s: pl.BlockSpec(s, lambda i: (0,) * len(s))
    vec = _const_spec((1, D))
    return pl.pallas_call(
        body, grid=(nt,), name="mix_proj_bwd",
        in_specs=[rev(D), rev(GW), rev(GW), rev(GW), rev(GW), rev(PW), rev(128), rev(W3), halo, rev(128), rev(D),
                  vec, vec, vec, _const_spec((DINP, D)), _const_spec((4, W3)), _const_spec((1, 128)),
                  _const_spec((1, 128))],
        out_specs=[rev(D), fix(DINP, D), fix(4, 8, W3), fix(8, 128), fix(8, 128), fix(8, D), fix(8, D), fix(8, D)],
        out_shape=[jax.ShapeDtypeStruct((T, D), F32), jax.ShapeDtypeStruct((DINP, D), BF),
                   jax.ShapeDtypeStruct((4, 8, W3), F32), jax.ShapeDtypeStruct((8, 128), F32),
                   jax.ShapeDtypeStruct((8, 128), F32), jax.ShapeDtypeStruct((8, D), F32),
                   jax.ShapeDtypeStruct((8, D), F32), jax.ShapeDtypeStruct((8, D), F32)],
        scratch_shapes=[pltpu.VMEM((tm + 8, W3), F32), pltpu.VMEM((tm + 8, W3), F32), pltpu.VMEM((tm, DINP), BF),
                        pltpu.VMEM((DINP, D), F32)],
        compiler_params=_cparams(),
    )(dxo, dqn, dkn, dv, dz, dp, dbg, xq, xq, ba, x, shift, scale, gain, winT, conv_w, alog, dtb)


MESH = pl.DeviceIdType.MESH
CHIP_RELS = ((1, 0), (0, 1), (1, 1))
DEV_RELS = tuple((dx, dy, dc) for dx in (0, 1) for dy in (0, 1) for dc in (0, 1) if (dx, dy, dc) != (0, 0, 0))
NMOD = 9
ADA_SH = NMOD * D // 4
WIN_SH = DIN // 4
WIN_PAD = 672
MSG_ROWS = 16
ANY = pl.BlockSpec(memory_space=pl.ANY)
VM = pl.BlockSpec(memory_space=pltpu.VMEM)


def _place():
    x, y, c = lax.axis_index("x"), lax.axis_index("y"), lax.axis_index("c")
    return x, y, c


class _SplitGather:
    N_SEMS = (3, 3, 3, 3, 1)

    def __init__(self, src, dst, ici_s, ici_r, d2d_s, d2d_r, lsem):
        self.src, self.dst = src, dst
        self.sems = (ici_s, ici_r, d2d_s, d2d_r)
        self.x, self.y, self.c = _place()
        self.myj = 2 * self.x + self.y
        half = src.shape[0] // 2
        self.mine = pl.ds(pl.multiple_of(self.c * half, 16), half)
        self.other = pl.ds(pl.multiple_of((1 - self.c) * half, 16), half)
        self.local = pltpu.make_async_copy(src, dst.at[self.myj], lsem.at[0])

    def _ici(self, k, slot):
        dx, dy = CHIP_RELS[k]
        return pltpu.make_async_remote_copy(
            src_ref=self.src.at[self.mine], dst_ref=self.dst.at[slot, self.mine], send_sem=self.sems[0].at[k],
            recv_sem=self.sems[1].at[k], device_id=(self.x ^ dx, self.y ^ dy, self.c), device_id_type=MESH)

    def _d2d(self, k, rows):
        dx, dy = CHIP_RELS[k]
        blk = self.dst.at[2 * (self.x ^ dx) + (self.y ^ dy), rows]
        return pltpu.make_async_remote_copy(
            src_ref=blk, dst_ref=blk, send_sem=self.sems[2].at[k], recv_sem=self.sems[3].at[k],
            device_id=(self.x, self.y, 1 - self.c), device_id_type=MESH)

    def start(self):
        self.local.start()
        for k in range(3):
            self._ici(k, self.myj).start()

    def forward(self):
        for k, (dx, dy) in enumerate(CHIP_RELS):
            self._ici(k, 2 * (self.x ^ dx) + (self.y ^ dy)).wait_recv()
            self._d2d(k, self.mine).start()

    def finish(self):
        for k in range(3):
            self._d2d(k, self.other).wait_recv()
        for k in range(3):
            self._d2d(k, self.mine).wait_send()
            self._ici(k, self.myj).wait_send()
        self.local.wait()


class _Scatter:
    def __init__(self, ins, outs, send, recv, lsem):
        self.ins, self.outs, self.send, self.recv, self.lsem = ins, outs, send, recv, lsem
        self.x, self.y, self.c = _place()
        self.myj = 2 * self.x + self.y

    def _copy(self, a, k, landing):
        dx, dy = CHIP_RELS[k]
        pj = 2 * (self.x ^ dx) + (self.y ^ dy)
        return pltpu.make_async_remote_copy(
            src_ref=self.ins[a].at[pj], dst_ref=self.outs[a].at[pj if landing else self.myj],
            send_sem=self.send.at[a, k], recv_sem=self.recv.at[a, k],
            device_id=(self.x ^ dx, self.y ^ dy, self.c), device_id_type=MESH)

    def _local(self, a):
        return pltpu.make_async_copy(self.ins[a].at[self.myj], self.outs[a].at[self.myj], self.lsem.at[a])

    def start(self):
        for a in range(len(self.ins)):
            self._local(a).start()
            for k in range(3):
                self._copy(a, k, False).start()

    def finish(self):
        for a in range(len(self.ins)):
            for k in range(3):
                self._copy(a, k, True).wait_recv()
            for k in range(3):
                self._copy(a, k, False).wait_send()
            self._local(a).wait()


def _scatter_sems(n):
    return [pltpu.SemaphoreType.DMA((n, 3)), pltpu.SemaphoreType.DMA((n, 3)), pltpu.SemaphoreType.DMA((n,))]


def _gather_sems():
    return [pltpu.SemaphoreType.DMA((k,)) for k in _SplitGather.N_SEMS]


def _comm_begin(i, plan, last):
    @pl.when(i == 0)
    def _():
        plan.start()

    if hasattr(plan, "forward"):
        @pl.when(i == max(last - 3, 0))
        def _():
            plan.forward()


def _comm_end(i, plan, last):
    @pl.when(i == last)
    def _():
        plan.finish()


def _ada_exchange(msg, w_ada, b_ada, wblock):
    def body(msg_ref, w_ref, b_ref, wb_ref, all_ref, mod_ref, wg_ref, modp, send1, recv1, send2, recv2, lsem, *gsems):
        x, y, c = _place()
        me = 4 * x + 2 * y + c
        wgather = _SplitGather(wb_ref, wg_ref, *gsems)
        wgather.start()
        own = pltpu.make_async_copy(msg_ref, all_ref.at[me], lsem.at[0])
        own.start()

        def gather(k, rel, slot):
            dx, dy, dc = rel
            return pltpu.make_async_remote_copy(
                src_ref=msg_ref, dst_ref=all_ref.at[slot], send_sem=send1.at[k], recv_sem=recv1.at[k],
                device_id=(x ^ dx, y ^ dy, c ^ dc), device_id_type=MESH)

        for k, rel in enumerate(DEV_RELS):
            gather(k, rel, me).start()
        for k, (dx, dy, dc) in enumerate(DEV_RELS):
            gather(k, (dx, dy, dc), 4 * (x ^ dx) + 2 * (y ^ dy) + (c ^ dc)).wait_recv()
        for k, rel in enumerate(DEV_RELS):
            gather(k, rel, me).wait_send()
        own.wait()

        for d in range(8):
            cv = all_ref[d, 0:8, :]
            act = cv * _sigmoid(cv)
            modp[d] = _nn(act, w_ref[...], precision=HI) + b_ref[...]

        myj = 2 * x + y
        keep = pltpu.make_async_copy(modp.at[me], mod_ref.at[myj], lsem.at[1])
        keep.start()

        def scatter(k, rel):
            dx, dy = rel
            return pltpu.make_async_remote_copy(
                src_ref=modp.at[4 * (x ^ dx) + 2 * (y ^ dy) + c], dst_ref=mod_ref.at[myj],
                send_sem=send2.at[k], recv_sem=recv2.at[k], device_id=(x ^ dx, y ^ dy, c), device_id_type=MESH)

        def landed(k, rel):
            dx, dy = rel
            return pltpu.make_async_remote_copy(
                src_ref=modp.at[me], dst_ref=mod_ref.at[2 * (x ^ dx) + (y ^ dy)],
                send_sem=send2.at[k], recv_sem=recv2.at[k], device_id=(x ^ dx, y ^ dy, c), device_id_type=MESH)

        for k, rel in enumerate(CHIP_RELS):
            scatter(k, rel).start()
        for k, rel in enumerate(CHIP_RELS):
            landed(k, rel).wait_recv()
        for k, rel in enumerate(CHIP_RELS):
            scatter(k, rel).wait_send()
        keep.wait()
        wgather.forward()
        wgather.finish()

    return pl.pallas_call(
        body, name="ada_exchange", in_specs=[VM, VM, VM, ANY], out_specs=[VM, VM, ANY],
        out_shape=[jax.ShapeDtypeStruct((8, MSG_ROWS, D), F32), jax.ShapeDtypeStruct((4, 8, ADA_SH), F32),
                   jax.ShapeDtypeStruct((4,) + wblock.shape, wblock.dtype)],
        scratch_shapes=[pltpu.VMEM((8, 8, ADA_SH), F32), pltpu.SemaphoreType.DMA((7,)), pltpu.SemaphoreType.DMA((7,)),
                        pltpu.SemaphoreType.DMA((3,)), pltpu.SemaphoreType.DMA((3,)), pltpu.SemaphoreType.DMA((2,))]
        + _gather_sems(),
        compiler_params=pltpu.CompilerParams(vmem_limit_bytes=VMEM_LIMIT),
    )(msg, w_ada, b_ada, wblock)


def _chip_exchange(parts, name):
    n = len(parts)

    def body(*refs):
        plan = _Scatter(refs[:n], refs[n:2 * n], *refs[2 * n:])
        plan.start()
        plan.finish()

    return pl.pallas_call(
        body, name=name, in_specs=[ANY] * n, out_specs=[ANY] * n,
        out_shape=[jax.ShapeDtypeStruct(p.shape, p.dtype) for p in parts], scratch_shapes=_scatter_sems(n),
    )(*parts)


def _all_exchange(parts, name):
    n = len(parts)

    def body(*refs):
        ins, outs = refs[:n], refs[n:2 * n]
        send, recv, lsem = refs[2 * n:]
        x, y, c = _place()
        me = 4 * x + 2 * y + c
        local, remote = [], []
        for a in range(n):
            local.append(pltpu.make_async_copy(ins[a], outs[a].at[me], lsem.at[a]))
            for k, (dx, dy, dc) in enumerate(DEV_RELS):
                remote.append((pltpu.make_async_remote_copy(
                    src_ref=ins[a], dst_ref=outs[a].at[me], send_sem=send.at[a, k], recv_sem=recv.at[a, k],
                    device_id=(x ^ dx, y ^ dy, c ^ dc), device_id_type=MESH), a, k,
                    4 * (x ^ dx) + 2 * (y ^ dy) + (c ^ dc)))
        for cp in local:
            cp.start()
        for cp, _, _, _ in remote:
            cp.start()
        for _, a, k, pd in remote:
            pltpu.make_async_remote_copy(
                src_ref=ins[a], dst_ref=outs[a].at[pd], send_sem=send.at[a, k], recv_sem=recv.at[a, k],
                device_id=(x, y, c), device_id_type=MESH).wait_recv()
        for cp, _, _, _ in remote:
            cp.wait_send()
        for cp in local:
            cp.wait()

    shapes = [jax.ShapeDtypeStruct((8,) + tuple(p.shape), p.dtype) for p in parts]
    return pl.pallas_call(
        body, name=name, in_specs=[ANY] * n, out_specs=[ANY] * n, out_shape=shapes,
        scratch_shapes=[pltpu.SemaphoreType.DMA((n, 7)), pltpu.SemaphoreType.DMA((n, 7)), pltpu.SemaphoreType.DMA((n,))],
    )(*parts)


def _pair_exchange(parts, name):
    n = len(parts)

    def body(*refs):
        ins, outs = refs[:n], refs[n:2 * n]
        send, recv = refs[2 * n:]
        x, y, c = _place()
        cps = [pltpu.make_async_remote_copy(
            src_ref=ins[a], dst_ref=outs[a], send_sem=send.at[a], recv_sem=recv.at[a],
            device_id=(x, y, 1 - c), device_id_type=MESH) for a in range(n)]
        for cp in cps:
            cp.start()
        for cp in cps:
            cp.wait_recv()
        for cp in cps:
            cp.wait_send()

    shapes = [jax.ShapeDtypeStruct(p.shape, p.dtype) for p in parts]
    return pl.pallas_call(
        body, name=name, in_specs=[ANY] * n, out_specs=[ANY] * n, out_shape=shapes,
        scratch_shapes=[pltpu.SemaphoreType.DMA((n,)), pltpu.SemaphoreType.DMA((n,))],
    )(*parts)


def _row_tile(rows, cap):
    best = rows
    for t in range(8, min(cap, rows) + 1, 8):
        if rows % t == 0:
            best = t
    return best if rows % 8 == 0 else rows


def _sum_slots(parts, name):
    n, rows, width = parts.shape
    tr = _row_tile(rows, 352)

    def body(p_ref, o_ref):
        acc = p_ref[0].astype(F32)
        for j in range(1, n):
            acc = acc + p_ref[j].astype(F32)
        o_ref[...] = acc

    return pl.pallas_call(
        body, grid=(rows // tr,), name=name,
        in_specs=[pl.BlockSpec((n, tr, width), lambda i: (0, i, 0))],
        out_specs=pl.BlockSpec((tr, width), lambda i: (i, 0)),
        out_shape=jax.ShapeDtypeStruct((rows, width), F32),
        compiler_params=_cparams(),
    )(parts)


def _adamw_math(g, w, m, v):
    m_new = ADAM_B1 * m + (1.0 - ADAM_B1) * g
    v_new = ADAM_B2 * v + (1.0 - ADAM_B2) * (g * g)
    m_hat = m_new / (1.0 - ADAM_B1 ** ADAM_STEP)
    v_hat = v_new / (1.0 - ADAM_B2 ** ADAM_STEP)
    delta = -ADAM_LR * (m_hat / (jnp.sqrt(v_hat) + ADAM_EPS) + ADAM_WD * w)
    return delta, m_new, v_new


def _adamw(grads, w, m, v, name):
    rows, width = w.shape
    tr = _row_tile(rows, 256 if width <= 1024 else 128)
    ng = len(grads)

    def body(*refs):
        g = refs[0][...]
        for r in refs[1:ng]:
            g = g + r[...]
        w_ref, m_ref, v_ref, g_out, d_out, m_out, v_out = refs[ng:]
        delta, m_new, v_new = _adamw_math(g, w_ref[...], m_ref[...], v_ref[...])
        g_out[...] = g
        d_out[...] = delta
        m_out[...] = m_new
        v_out[...] = v_new

    blk = pl.BlockSpec((tr, width), lambda i: (i, 0))
    return pl.pallas_call(
        body, grid=(rows // tr,), name=name,
        in_specs=[blk] * (ng + 3), out_specs=[blk] * 4,
        out_shape=[jax.ShapeDtypeStruct((rows, width), F32)] * 4,
        compiler_params=_cparams(),
    )(*grads, w, m, v)


def _adamw_ada(msgs, dmods, w, m, v):
    rows, width = w.shape
    tr = 128

    def body(c_ref, dm_ref, w_ref, m_ref, v_ref, g_out, d_out, m_out, v_out):
        cv = jnp.concatenate([c_ref[d, 0:1, :] for d in range(8)], axis=0)
        act = cv * _sigmoid(cv)
        g = _tn(act, dm_ref[...], precision=HI)
        delta, m_new, v_new = _adamw_math(g, w_ref[...], m_ref[...], v_ref[...])
        g_out[...] = g
        d_out[...] = delta
        m_out[...] = m_new
        v_out[...] = v_new

    blk = pl.BlockSpec((tr, width), lambda i: (i, 0))
    return pl.pallas_call(
        body, grid=(rows // tr,), name="adamw_w_ada",
        in_specs=[pl.BlockSpec((8, MSG_ROWS, tr), lambda i: (0, 0, i)), pl.BlockSpec((8, width), lambda i: (0, 0)),
                  blk, blk, blk],
        out_specs=[blk] * 4, out_shape=[jax.ShapeDtypeStruct((rows, width), F32)] * 4,
        compiler_params=_cparams(),
    )(msgs, dmods, w, m, v)


def _adamw_small(parts, w, m, v, name):
    n, rows, width = parts.shape

    def body(p_ref, w_ref, m_ref, v_ref, g_out, d_out, m_out, v_out):
        g = p_ref[0]
        for j in range(1, n):
            g = g + p_ref[j]
        delta, m_new, v_new = _adamw_math(g, w_ref[...], m_ref[...], v_ref[...])
        g_out[...] = g
        d_out[...] = delta
        m_out[...] = m_new
        v_out[...] = v_new

    return pl.pallas_call(
        body, name=name, in_specs=[VM] * 4, out_specs=[VM] * 4,
        out_shape=[jax.ShapeDtypeStruct((rows, width), F32)] * 4,
        compiler_params=pltpu.CompilerParams(vmem_limit_bytes=VMEM_LIMIT),
    )(parts, w, m, v)


SMALL_ROWS = 24


def _pad_row(vec, width=D):
    vec = vec.reshape(1, -1)
    return jnp.pad(vec, ((0, 0), (0, width - vec.shape[1])))


def _lanes_4_7(vec4):
    return jnp.zeros((1, 128), F32).at[0, NH:2 * NH].set(vec4.reshape(NH))


def kernel(x, c, w_ada, b_ada, norm_ffn1, ffn1_gate, ffn1_up, ffn1_down, norm_mix, w_in, conv_w, a_log, dt_bias, gdn_norm, pool_w, pool_scale, w_out, norm_ffn2, ffn2_gate, ffn2_up, ffn2_down, final_norm, loss_target, m_w_ada, m_b_ada, m_norm_ffn1, m_ffn1_gate, m_ffn1_up, m_ffn1_down, m_norm_mix, m_w_in, m_conv_w, m_a_log, m_dt_bias, m_gdn_norm, m_pool_w, m_pool_scale, m_w_out, m_norm_ffn2, m_ffn2_gate, m_ffn2_up, m_ffn2_down, m_final_norm, v_w_ada, v_b_ada, v_norm_ffn1, v_ffn1_gate, v_ffn1_up, v_ffn1_down, v_norm_mix, v_w_in, v_conv_w, v_a_log, v_dt_bias, v_gdn_norm, v_pool_w, v_pool_scale, v_w_out, v_norm_ffn2, v_ffn2_gate, v_ffn2_up, v_ffn2_down, v_final_norm):
    xs = x[0]
    tgt = loss_target[0]
    chip = 2 * lax.axis_index("x") + lax.axis_index("y")
    me = 2 * chip + lax.axis_index("c")

    fsh = FF // 4
    block_a = jnp.concatenate([ffn1_gate[0].T, ffn1_up[0].T, ffn1_down[0]], axis=0).astype(BF)
    block_b = jnp.concatenate([ffn2_gate[0].T, ffn2_up[0].T, ffn2_down[0], w_out[0],
                               jnp.pad(w_in[0].T, ((0, WIN_PAD - WIN_SH), (0, 0)))], axis=0).astype(BF)

    msg = jnp.concatenate([jnp.broadcast_to(c, (8, D)), jnp.pad(conv_w[0], ((0, 0), (0, D - 3 * GW // 4))),
                           jnp.zeros((MSG_ROWS - 12, D), F32)], axis=0)
    b_sh = lax.dynamic_slice(b_ada, (0, chip * ADA_SH), (1, ADA_SH))
    msgs, mod4, gath_a = _ada_exchange(msg, w_ada[0], b_sh, block_a)
    mod = mod4[:, 0, :].reshape(NMOD, D)
    mrow = [mod[i:i + 1] for i in range(NMOD)]
    conv_full = jnp.concatenate([msgs[2 * j, 8:12, :3 * GW // 4] for j in range(4)], axis=1)
    alog, dtb = _lanes_4_7(a_log), _lanes_4_7(dt_bias)
    gnm = gdn_norm.reshape(1, DH)
    pwb = pool_w[0].astype(BF)
    psc = pool_scale.reshape(1, PW)
    fin = final_norm.reshape(1, D)

    x1, f1, a1, b1, gath_b = _ffn_fwd(xs, mrow[0], mrow[1], mrow[2], norm_ffn1, gath_a, 0, "ffn1_fwd", block_b)
    wo = gath_b[:, 3 * fsh:3 * fsh + D // 4, :].reshape(D, D)
    win_nat = gath_b[:, 3 * fsh + D // 4:3 * fsh + D // 4 + WIN_SH, :].reshape(DIN, D)
    winT = jnp.concatenate([win_nat[:4 * GW], win_nat[4 * GW + 2 * NH:], win_nat[4 * GW:4 * GW + 2 * NH],
                            jnp.zeros((128 - 2 * NH, D), BF)], axis=0)
    xq, ba, qn, kn, vv, z, pp, bg, gc = _mix_proj(x1, mrow[3], mrow[4], norm_mix, winT, conv_full, alog, dtb)
    x2, mixed, cat, o, sall = _mix_core(x1, mrow[5], qn, kn, vv, z, pp, bg, gc, gnm, pwb, psc, wo)
    x3, f2, a2, b2 = _ffn_fwd(x2, mrow[6], mrow[7], mrow[8], norm_ffn2, gath_b, 0, "ffn2_fwd")
    lpart, dx3, dfin = _loss_head(x3, tgt, fin)
    loss = lax.psum(jnp.sum(lpart), ("x", "y", "c"))

    slots = lambda t: t.reshape(4, t.shape[0] // 4, D)
    dx2, da2, db2, s2, h2, df2, dsh3, dsc3, dgt3, dn3 = _ffn_dgrad(
        dx3, x2, f2, a2, b2, mrow[6], mrow[7], mrow[8], norm_ffn2, gath_b, 0, "ffn2_dgrad")
    gg2, gu2, gd2 = _ffn_wgrad(da2, db2, s2, h2, df2, "ffn2_wgrad")
    dqn, dkn, dvv, dz, dpp, dbg, dgt2, dwo, dpw, dps, dgnm, *landed2 = _mix_core_bwd(
        dx2, mrow[5], mixed, cat, o, sall, qn, kn, vv, z, pp, bg, gc, gnm, pwb, psc, wo,
        scatter=[slots(gg2), slots(gu2), slots(gd2)])
    dx1, dwin, dcw, dal, ddt, dsh2, dsc2, dn2 = _mix_proj_bwd(
        dx2, dqn, dkn, dvv, dz, dpp, dbg, xq, ba, x1, mrow[3], mrow[4], norm_mix, winT, conv_full, alog, dtb)
    dwin_nat = jnp.concatenate([dwin[:4 * GW], dwin[4 * GW + PW:4 * GW + PW + 2 * NH], dwin[4 * GW:4 * GW + PW]], axis=0)
    dwin_sl = jnp.pad(dwin_nat.reshape(4, WIN_SH, D), ((0, 0), (0, WIN_PAD - WIN_SH), (0, 0)))
    dx0, da1, db1, s1, h1, df1, dsh1, dsc1, dgt1, dn1 = _ffn_dgrad(
        dx1, xs, f1, a1, b1, mrow[0], mrow[1], mrow[2], norm_ffn1, gath_a, 0, "ffn1_dgrad")
    gg1, gu1, gd1, *landed_mix = _ffn_wgrad(da1, db1, s1, h1, df1, "ffn1_wgrad", scatter=[slots(dwo), dwin_sl])
    landed1 = _chip_exchange([slots(gg1), slots(gu1), slots(gd1)], "grad_scatter")

    landed = list(landed1) + list(landed2) + list(landed_mix)
    names = ("g1", "u1", "d1", "g2", "u2", "d2", "wo", "win")
    psum = [_sum_slots(t, "sum_" + nm) for t, nm in zip(landed, names)]
    qsum = _pair_exchange(psum, "grad_pair")

    red = lambda t: jnp.sum(t, axis=0, keepdims=True)
    small = jnp.concatenate(
        [red(dn1), red(dn2), red(dn3), red(dfin),
         red(dsh1), red(dsc1), red(dgt1), red(dsh2), red(dsc2), red(dgt2), red(dsh3), red(dsc3), red(dgt3),
         _pad_row(red(dps)), _pad_row(red(dgnm)), _pad_row(red(dal)), _pad_row(red(ddt)),
         jnp.sum(dcw, axis=1).reshape(6, D), jnp.zeros((1, D), F32)], axis=0)
    small_all, dpw_all = _all_exchange([small, dpw.reshape(NG * 128, 128)], "small_gather")

    def unT(t, cols):
        return t[:cols].T

    upd = {}
    upd["ffn1_gate"] = _adamw([unT(psum[0], fsh), unT(qsum[0], fsh)], ffn1_gate[0], m_ffn1_gate[0], v_ffn1_gate[0], "adamw_g1")
    upd["ffn1_up"] = _adamw([unT(psum[1], fsh), unT(qsum[1], fsh)], ffn1_up[0], m_ffn1_up[0], v_ffn1_up[0], "adamw_u1")
    upd["ffn1_down"] = _adamw([psum[2], qsum[2]], ffn1_down[0], m_ffn1_down[0], v_ffn1_down[0], "adamw_d1")
    upd["ffn2_gate"] = _adamw([unT(psum[3], fsh), unT(qsum[3], fsh)], ffn2_gate[0], m_ffn2_gate[0], v_ffn2_gate[0], "adamw_g2")
    upd["ffn2_up"] = _adamw([unT(psum[4], fsh), unT(qsum[4], fsh)], ffn2_up[0], m_ffn2_up[0], v_ffn2_up[0], "adamw_u2")
    upd["ffn2_down"] = _adamw([psum[5], qsum[5]], ffn2_down[0], m_ffn2_down[0], v_ffn2_down[0], "adamw_d2")
    upd["w_out"] = _adamw([psum[6], qsum[6]], w_out[0], m_w_out[0], v_w_out[0], "adamw_wo")
    upd["w_in"] = _adamw([unT(psum[7], WIN_SH), unT(qsum[7], WIN_SH)], w_in[0], m_w_in[0], v_w_in[0], "adamw_win")
    dmods = lax.dynamic_slice(small_all[:, 4:4 + NMOD, :].reshape(8, NMOD * D), (0, chip * ADA_SH), (8, ADA_SH))
    upd["w_ada"] = _adamw_ada(msgs, dmods, w_ada[0], m_w_ada[0], v_w_ada[0])

    def pack_small(nf1, nmx, nf2, fn, bada, psc_, gn_, al_, dt_):
        return jnp.concatenate(
            [nf1.reshape(1, D), nmx.reshape(1, D), nf2.reshape(1, D), fn.reshape(1, D), bada.reshape(NMOD, D),
             _pad_row(psc_), _pad_row(gn_), _pad_row(_lanes_4_7(al_)), _pad_row(_lanes_4_7(dt_)),
             jnp.zeros((7, D), F32)], axis=0)

    ws = pack_small(norm_ffn1, norm_mix, norm_ffn2, final_norm, b_ada, pool_scale, gdn_norm, a_log, dt_bias)
    ms = pack_small(m_norm_ffn1, m_norm_mix, m_norm_ffn2, m_final_norm, m_b_ada, m_pool_scale, m_gdn_norm, m_a_log, m_dt_bias)
    vs = pack_small(v_norm_ffn1, v_norm_mix, v_norm_ffn2, v_final_norm, v_b_ada, v_pool_scale, v_gdn_norm, v_a_log, v_dt_bias)
    sm = _adamw_small(small_all, ws, ms, vs, "adamw_small")
    pw2 = lambda t: t.reshape(NG * 128, 128)
    upd_pw = _adamw_small(dpw_all, pw2(pool_w), pw2(m_pool_w), pw2(v_pool_w), "adamw_pool_w")
    csh = 3 * GW // 4
    gconv = lax.dynamic_slice(sm[0][17:23].reshape(4, 3 * GW), (0, chip * csh), (4, csh))
    upd["conv_w"] = _adamw([gconv], conv_w[0], m_conv_w[0], v_conv_w[0], "adamw_conv")

    def small_out(k):
        t = sm[k]
        return {
            "norm_ffn1": t[0:1], "norm_mix": t[1:2], "norm_ffn2": t[2:3], "final_norm": t[3],
            "b_ada": t[4:4 + NMOD].reshape(1, NMOD * D), "pool_scale": t[13:14, :PW], "gdn_norm": t[14:15, :DH],
            "a_log": t[15:16, NH:2 * NH], "dt_bias": t[16:17, NH:2 * NH],
        }

    order = ["w_ada", "b_ada", "norm_ffn1", "ffn1_gate", "ffn1_up", "ffn1_down", "norm_mix", "w_in", "conv_w", "a_log",
             "dt_bias", "gdn_norm", "pool_w", "pool_scale", "w_out", "norm_ffn2", "ffn2_gate", "ffn2_up", "ffn2_down",
             "final_norm"]
    outs = [loss, dx0[None]]
    for k in range(4):
        smk = small_out(k)
        for nm in order:
            if nm in upd:
                outs.append(upd[nm][k][None])
            elif nm == "pool_w":
                outs.append(upd_pw[k].reshape(1, NG, 128, 128))
            else:
                outs.append(smk[nm])
    return tuple(outs)
```

```python
import functools

import jax
import jax.numpy as jnp
from jax import lax
from jax.experimental import pallas as pl
from jax.experimental.pallas import tpu as pltpu

F32 = jnp.float32
BF = jnp.bfloat16

D = 1024
FF = 2816
FH = FF // 2
NH = 4
DH = 128
GW = NH * DH
CH = 64
PW = 512
NG = 4
POOL_WINDOWS = (2, 4, 8, 16)
HALO = 16
DIN = 4 * GW + 2 * NH + PW
DINP = 3 * GW + GW + PW + 128
EPS = 1e-6
ADAM_LR, ADAM_B1, ADAM_B2, ADAM_EPS, ADAM_WD, ADAM_STEP = 0.001, 0.9, 0.999, 1e-08, 0.01, 10

VMEM_LIMIT = 56 * 1024 * 1024

NT_DIMS = (((1,), (1,)), ((), ()))
TN_DIMS = (((0,), (0,)), ((), ()))
HI = lax.Precision.HIGHEST


def _nt(a, b, **kw):
    return lax.dot_general(a, b, NT_DIMS, preferred_element_type=F32, **kw)


def _tn(a, b, **kw):
    return lax.dot_general(a, b, TN_DIMS, preferred_element_type=F32, **kw)


def _nn(a, b, **kw):
    return jnp.dot(a, b, preferred_element_type=F32, **kw)


def _cparams(sem=("arbitrary",), **kw):
    return pltpu.CompilerParams(dimension_semantics=sem, vmem_limit_bytes=VMEM_LIMIT, **kw)


def _const_spec(shape):
    nd = len(shape)
    return pl.BlockSpec(shape, lambda *_: (0,) * nd, pipeline_mode=pl.Buffered(1))


def _row_spec(tm, width):
    return pl.BlockSpec((tm, width), lambda i: (i, 0))


def _sum8(v):
    return jnp.sum(v.reshape(v.shape[0] // 8, 8, v.shape[1]), axis=0)


def _sigmoid(v):
    return 0.5 * jnp.tanh(0.5 * v) + 0.5


def _tile(T, cap=512):
    return min(cap, T)


def _norm_mod_fwd(xv, gain, shift, scale):
    r = lax.rsqrt(jnp.mean(xv * xv, axis=-1, keepdims=True) + EPS)
    n = xv * r
    y = n * gain
    return n, r, y, y * (1.0 + scale) + shift


def _norm_mod_bwd(dh, n, r, y, gain, scale):
    dy = dh * (1.0 + scale)
    dn = dy * gain
    dx = r * (dn - n * jnp.mean(dn * n, axis=-1, keepdims=True))
    return dx, _sum8(dh), _sum8(dh * y), _sum8(dy * n)


def _ffn_wspecs(k0):
    return [pl.BlockSpec((4, FF // 4, D), lambda i, k=k0 + n: (0, k, 0), pipeline_mode=pl.Buffered(1)) for n in range(3)]


def _half(w_ref, j):
    return w_ref[2 * j:2 * j + 2].reshape(FH, D)


def _ffn_fwd(x, shift, scale, gate, gain, wall, k0, name, gather_block=None):
    T = x.shape[0]
    tm = _tile(T)
    nt = T // tm
    comm = gather_block is not None

    def body(*refs):
        x_ref, sh_ref, sc_ref, gt_ref, gn_ref, wg_ref, wu_ref, wd_ref = refs[:8]
        if comm:
            xo_ref, f_ref, a_ref, b_ref = refs[9:13]
            plan = _SplitGather(refs[8], refs[13], *refs[14:])
            _comm_begin(pl.program_id(0), plan, nt - 1)
        else:
            xo_ref, f_ref, a_ref, b_ref = refs[8:12]
        xv = x_ref[...]
        _, _, _, h = _norm_mod_fwd(xv, gn_ref[...], sh_ref[...], sc_ref[...])
        hb = h.astype(BF)
        facc = jnp.zeros((tm, D), F32)
        for j in range(2):
            a = _nt(hb, _half(wg_ref, j))
            b = _nt(hb, _half(wu_ref, j))
            a_ref[:, j * FH:(j + 1) * FH] = a.astype(BF)
            b_ref[:, j * FH:(j + 1) * FH] = b.astype(BF)
            s = (a * _sigmoid(a) * b).astype(BF)
            facc = facc + _nn(s, _half(wd_ref, j))
        f_ref[...] = facc
        xo_ref[...] = xv + 0.5 * gt_ref[...] * facc
        if comm:
            _comm_end(pl.program_id(0), plan, nt - 1)

    vec = _const_spec((1, D))
    extra_in = [gather_block] if comm else []
    return pl.pallas_call(
        body, grid=(nt,), name=name,
        in_specs=[_row_spec(tm, D), vec, vec, vec, vec] + _ffn_wspecs(k0) + [ANY] * comm,
        out_specs=[_row_spec(tm, D), _row_spec(tm, D), _row_spec(tm, FF), _row_spec(tm, FF)] + [ANY] * comm,
        out_shape=[jax.ShapeDtypeStruct((T, D), F32), jax.ShapeDtypeStruct((T, D), F32),
                   jax.ShapeDtypeStruct((T, FF), BF), jax.ShapeDtypeStruct((T, FF), BF)]
        + ([jax.ShapeDtypeStruct((4,) + gather_block.shape, gather_block.dtype)] if comm else []),
        scratch_shapes=_gather_sems() if comm else [],
        compiler_params=_cparams(),
    )(x, shift, scale, gate, gain, wall, wall, wall, *extra_in)


def _ffn_dgrad(dxo, x, f, a, b, shift, scale, gate, gain, wall, k0, name, scatter=()):
    T = x.shape[0]
    tm = _tile(T, 256)
    nt = T // tm
    ns = len(scatter)

    def body(*refs):
        dxo_ref, x_ref, f_ref, a_ref, b_ref, sh_ref, sc_ref, gt_ref, gn_ref, wg_ref, wu_ref, wd_ref = refs[:12]
        dx_ref, da_ref, db_ref, s_ref, h_ref, df_ref, dsh_ref, dsc_ref, dgt_ref, dgn_ref = refs[12 + ns:22 + ns]
        i = pl.program_id(0)
        if ns:
            plan = _Scatter(refs[12:12 + ns], refs[22 + ns:22 + 2 * ns], *refs[22 + 2 * ns:])
            _comm_begin(i, plan, nt - 1)
        dxo_v = dxo_ref[...]
        xv = x_ref[...]
        gain_v, scale_v = gn_ref[...], sc_ref[...]
        n, r, y, h = _norm_mod_fwd(xv, gain_v, sh_ref[...], scale_v)
        h_ref[...] = h.astype(BF)
        dgate = _sum8(0.5 * f_ref[...] * dxo_v)
        dfb = (0.5 * gt_ref[...] * dxo_v).astype(BF)
        df_ref[...] = dfb
        dh = jnp.zeros((tm, D), F32)
        for j in range(2):
            cols = slice(j * FH, (j + 1) * FH)
            ds = _nt(dfb, _half(wd_ref, j))
            av = a_ref[:, cols].astype(F32)
            bv = b_ref[:, cols].astype(F32)
            sig = _sigmoid(av)
            sa = av * sig
            da = (ds * bv * (sig * (1.0 + av * (1.0 - sig)))).astype(BF)
            db = (ds * sa).astype(BF)
            da_ref[:, cols] = da
            db_ref[:, cols] = db
            s_ref[:, cols] = (sa * bv).astype(BF)
            dh = dh + _nn(da, _half(wg_ref, j)) + _nn(db, _half(wu_ref, j))
        dxn, dsh, dsc, dgn = _norm_mod_bwd(dh, n, r, y, gain_v, scale_v)
        dx_ref[...] = dxo_v + dxn

        @pl.when(i == 0)
        def _():
            dsh_ref[...] = dsh
            dsc_ref[...] = dsc
            dgt_ref[...] = dgate
            dgn_ref[...] = dgn

        @pl.when(i > 0)
        def _():
            dsh_ref[...] += dsh
            dsc_ref[...] += dsc
            dgt_ref[...] += dgate
            dgn_ref[...] += dgn

        if ns:
            _comm_end(i, plan, nt - 1)

    vec = _const_spec((1, D))
    acc = pl.BlockSpec((8, D), lambda i: (0, 0))
    accs = jax.ShapeDtypeStruct((8, D), F32)
    return pl.pallas_call(
        body, grid=(nt,), name=name,
        in_specs=[_row_spec(tm, D), _row_spec(tm, D), _row_spec(tm, D), _row_spec(tm, FF), _row_spec(tm, FF),
                  vec, vec, vec, vec] + _ffn_wspecs(k0) + [ANY] * ns,
        out_specs=[_row_spec(tm, D), _row_spec(tm, FF), _row_spec(tm, FF), _row_spec(tm, FF),
                   _row_spec(tm, D), _row_spec(tm, D), acc, acc, acc, acc] + [ANY] * ns,
        out_shape=[jax.ShapeDtypeStruct((T, D), F32), jax.ShapeDtypeStruct((T, FF), BF),
                   jax.ShapeDtypeStruct((T, FF), BF), jax.ShapeDtypeStruct((T, FF), BF),
                   jax.ShapeDtypeStruct((T, D), BF), jax.ShapeDtypeStruct((T, D), BF), accs, accs, accs, accs]
        + [jax.ShapeDtypeStruct(t.shape, t.dtype) for t in scatter],
        scratch_shapes=_scatter_sems(ns) if ns else [],
        compiler_params=_cparams(),
    )(dxo, x, f, a, b, shift, scale, gate, gain, wall, wall, wall, *scatter)


def _ffn_wgrad(da, db, s, h, df, name, scatter=()):
    T = h.shape[0]
    tk = _tile(T, 512)
    nk = T // tk
    ns = len(scatter)

    def body(*refs):
        da_ref, db_ref, s_ref, h_ref, df_ref = refs[:5]
        og_ref, ou_ref, od_ref = refs[5 + ns:8 + ns]
        ag, au, ad = refs[8 + 2 * ns:11 + 2 * ns]
        k = pl.program_id(1)
        step = pl.program_id(0) * nk + k
        if ns:
            plan = _Scatter(refs[5:5 + ns], refs[8 + ns:8 + 2 * ns], *refs[11 + 2 * ns:])
            _comm_begin(step, plan, 2 * nk - 1)
        @pl.when(k == 0)
        def _():
            for acc in (ag, au, ad):
                acc[...] = jnp.zeros((FH, D), F32)

        ag[...] += _tn(da_ref[...], h_ref[...])
        au[...] += _tn(db_ref[...], h_ref[...])
        ad[...] += _tn(s_ref[...], df_ref[...])

        @pl.when(k == nk - 1)
        def _():
            og_ref[...] = ag[...].astype(BF)
            ou_ref[...] = au[...].astype(BF)
            od_ref[...] = ad[...].astype(BF)

        if ns:
            _comm_end(step, plan, 2 * nk - 1)

    colblk = pl.BlockSpec((tk, FH), lambda j, k: (k, j))
    rowblk = pl.BlockSpec((tk, D), lambda j, k: (k, 0))
    outblk = pl.BlockSpec((FH, D), lambda j, k: (j, 0))
    outs = jax.ShapeDtypeStruct((FF, D), BF)
    return pl.pallas_call(
        body, grid=(2, nk), name=name,
        in_specs=[colblk, colblk, colblk, rowblk, rowblk] + [ANY] * ns,
        out_specs=[outblk, outblk, outblk] + [ANY] * ns,
        out_shape=[outs, outs, outs] + [jax.ShapeDtypeStruct(t.shape, t.dtype) for t in scatter],
        scratch_shapes=[pltpu.VMEM((FH, D), F32)] * 3 + (_scatter_sems(ns) if ns else []),
        compiler_params=_cparams(("arbitrary", "arbitrary")),
    )(da, db, s, h, df, *scatter)


def _loss_head(x, target, gain):
    T = x.shape[0]
    tm = _tile(T)

    def body(x_ref, t_ref, gn_ref, ls_ref, dx_ref, dgn_ref):
        i = pl.program_id(0)
        xv = x_ref[...]
        gain_v = gn_ref[...]
        r = lax.rsqrt(jnp.mean(xv * xv, axis=-1, keepdims=True) + EPS)
        n = xv * r
        err = n * gain_v - t_ref[...]
        e2 = err * err
        part = e2[:, 0:128]
        for q in range(1, D // 128):
            part = part + e2[:, q * 128:(q + 1) * 128]
        lsum = _sum8(part) * (0.5 / D)
        dy = err * (1.0 / D)
        dn = dy * gain_v
        dx_ref[...] = r * (dn - n * jnp.mean(dn * n, axis=-1, keepdims=True))
        dgn = _sum8(dy * n)

        @pl.when(i == 0)
        def _():
            ls_ref[...] = lsum
            dgn_ref[...] = dgn

        @pl.when(i > 0)
        def _():
            ls_ref[...] += lsum
            dgn_ref[...] += dgn

    return pl.pallas_call(
        body, grid=(T // tm,), name="loss_head",
        in_specs=[_row_spec(tm, D), _row_spec(tm, D), _const_spec((1, D))],
        out_specs=[pl.BlockSpec((8, 128), lambda i: (0, 0)), _row_spec(tm, D), pl.BlockSpec((8, D), lambda i: (0, 0))],
        out_shape=[jax.ShapeDtypeStruct((8, 128), F32), jax.ShapeDtypeStruct((T, D), F32),
                   jax.ShapeDtypeStruct((8, D), F32)],
        compiler_params=_cparams(),
    )(x, target, gain)


def _seg_cumsum(v, row_in_chunk, reverse=False):
    n = v.shape[0]
    s = 1
    while s < CH:
        if reverse:
            moved = pltpu.roll(v, n - s, 0)
            ok = row_in_chunk < CH - s
        else:
            moved = pltpu.roll(v, s, 0)
            ok = row_in_chunk >= s
        v = v + jnp.where(ok, moved, 0.0)
        s *= 2
    return v


def _conv_silu(xq_ext_ref, cw_ref, tm):
    c = cw_ref[0:1, :] * xq_ext_ref[pl.ds(5, tm), :]
    for j in range(1, 4):
        c = c + cw_ref[j:j + 1, :] * xq_ext_ref[pl.ds(5 + j, tm), :]
    return c, _sigmoid(c)


def _gates(ba, alog, dtb, lane):
    beta = _sigmoid(ba)
    arg = ba + dtb
    softplus = jnp.maximum(arg, 0.0) + jnp.log(1.0 + jnp.exp(-jnp.abs(arg)))
    g = -jnp.exp(alog) * softplus
    return jnp.where(lane < NH, beta, 0.0), jnp.where((lane >= NH) & (lane < 2 * NH), g, 0.0), _sigmoid(arg)


def _mix_proj(x, shift, scale, gain, winT, conv_w, alog, dtb):
    T = x.shape[0]
    tm = _tile(T)

    def body(x_ref, sh_ref, sc_ref, gn_ref, w_ref, cw_ref, al_ref, dt_ref,
             xq_ref, ba_ref, qn_ref, kn_ref, v_ref, z_ref, p_ref, bg_ref, gc_ref, ext):
        i = pl.program_id(0)
        _, _, _, h = _norm_mod_fwd(x_ref[...], gn_ref[...], sh_ref[...], sc_ref[...])
        hb = h.astype(BF)

        @pl.when(i == 0)
        def _():
            ext[pl.ds(0, 8), :] = jnp.zeros((8, 3 * GW), F32)

        xq = _nt(hb, w_ref[pl.ds(0, 3 * GW), :])
        xq_ref[...] = xq
        ext[pl.ds(8, tm), :] = xq
        z_ref[...] = _nt(hb, w_ref[pl.ds(3 * GW, GW), :])
        p_ref[...] = _nt(hb, w_ref[pl.ds(4 * GW, PW), :])
        ba = _nt(hb, w_ref[pl.ds(4 * GW + PW, 128), :])
        ba_ref[...] = ba

        c, sg = _conv_silu(ext, cw_ref, tm)
        ext[pl.ds(0, 8), :] = ext[pl.ds(tm, 8), :]
        qt = c * sg
        for hd in range(NH):
            cq = slice(hd * DH, (hd + 1) * DH)
            ck = slice(GW + hd * DH, GW + (hd + 1) * DH)
            qh, kh = qt[:, cq], qt[:, ck]
            qn_ref[:, cq] = qh * (lax.rsqrt(jnp.sum(qh * qh, axis=-1, keepdims=True) + EPS) * DH ** -0.5)
            kn_ref[:, cq] = kh * lax.rsqrt(jnp.sum(kh * kh, axis=-1, keepdims=True) + EPS)
        v_ref[...] = qt[:, 2 * GW:3 * GW]

        lane = lax.broadcasted_iota(jnp.int32, (tm, 128), 1)
        row = lax.broadcasted_iota(jnp.int32, (tm, 128), 0) % CH
        beta, g, _ = _gates(ba, al_ref[...], dt_ref[...], lane)
        bg_ref[...] = beta + g
        gc_ref[...] = _seg_cumsum(g, row)

    wide = lambda w: _row_spec(tm, w)
    shp = lambda w: jax.ShapeDtypeStruct((T, w), F32)
    return pl.pallas_call(
        body, grid=(T // tm,), name="mix_proj",
        in_specs=[wide(D), _const_spec((1, D)), _const_spec((1, D)), _const_spec((1, D)), _const_spec((DINP, D)),
                  _const_spec((4, 3 * GW)), _const_spec((1, 128)), _const_spec((1, 128))],
        out_specs=[wide(3 * GW), wide(128), wide(GW), wide(GW), wide(GW), wide(GW), wide(PW), wide(128), wide(128)],
        out_shape=[shp(3 * GW), shp(128), shp(GW), shp(GW), shp(GW), shp(GW), shp(PW), shp(128), shp(128)],
        scratch_shapes=[pltpu.VMEM((tm + 8, 3 * GW), F32)],
        compiler_params=_cparams(),
    )(x, shift, scale, gain, winT, conv_w, alog, dtb)


R2 = 2 * CH
TRI_PREC = None


def _tmm(fn, a, b):
    if TRI_PREC is None:
        return fn(a.astype(BF), b.astype(BF))
    return fn(a, b, precision=TRI_PREC)


def _pair_consts():
    ii = lax.broadcasted_iota(jnp.int32, (R2, R2), 0)
    jj = lax.broadcasted_iota(jnp.int32, (R2, R2), 1)
    same = (ii < CH) == (jj < CH)
    r = lax.broadcasted_iota(jnp.int32, (R2, 1), 0)
    return dict(causal=same & (ii >= jj), strict=same & (ii > jj), eye=(ii == jj).astype(F32), rowA=r < CH,
                last=(r == CH - 1) | (r == R2 - 1),
                rowS=lax.broadcasted_iota(jnp.int32, (2 * DH, 1), 0) < DH)


def _tri_inverse_many(ms, eye):
    pws = [-m for m in ms]
    ts = [eye + p for p in pws]
    for _ in range(5):
        pws = [_tmm(_nn, p, p) for p in pws]
        ts = [_tmm(_nn, t, eye + p) for t, p in zip(ts, pws)]
    return ts


def _egl_rows(gl):
    egl = jnp.exp(gl)
    return egl, jnp.concatenate([jnp.broadcast_to(egl[0:1], (DH, 1)), jnp.broadcast_to(egl[CH:CH + 1], (DH, 1))], axis=0)


def _pair_intra(items, cn):
    causal, rowA = cn["causal"], cn["rowA"]

    def bd(t):
        return jnp.concatenate([jnp.where(rowA, t, 0.0), jnp.where(rowA, 0.0, t)], axis=1).astype(BF)

    outs = []
    for q, k, v, beta, gcv, gl in items:
        gc_b = jnp.broadcast_to(gcv, (R2, R2))
        gam = jnp.where(causal, jnp.exp(jnp.where(causal, gc_b - gc_b.T, 0.0)), 0.0)
        kb = k * beta
        kbf = k.astype(BF)
        P = _nt(kb.astype(BF), kbf)
        QK = _nt(q.astype(BF), kbf)
        E = jnp.exp(gcv)
        outs.append(dict(gam=gam, kb=kb, vb=v * beta, P=P, QK=QK, E=E, Fd=jnp.exp(gl - gcv), kbE=kb * E,
                         Q=QK * gam, qE_bd=bd(q * E)))
    tms = _tri_inverse_many([jnp.where(cn["strict"], d["P"] * d["gam"], 0.0) for d in outs], cn["eye"])
    for d, tm_, (q, k, v, beta, gcv, gl) in zip(outs, tms, items):
        d["Tm"] = tm_
        d["u"] = _tmm(_nn, tm_, d["vb"])
        d["w_bd"] = bd(_tmm(_nn, tm_, d["kbE"]))
        d["kF_bd"] = bd(k * d["Fd"])
    return outs


def _pair_scan(it, S, egl_st):
    Sb = S.astype(BF)
    vn = it["u"] - _nn(it["w_bd"], Sb)
    vnb = vn.astype(BF)
    o = _nn(it["qE_bd"], Sb) + _nn(it["Q"].astype(BF), vnb)
    return vnb, o, S * egl_st + _tn(it["kF_bd"], vnb)


def _pair_forward(q, k, v, beta, gcv, gl, S, cn):
    fw = _pair_intra([(q, k, v, beta, gcv, gl)], cn)[0]
    fw["egl"], fw["egl_st"] = _egl_rows(gl)
    fw["vnb"], fw["o"], fw["S_new"] = _pair_scan(fw, S, fw["egl_st"])
    return fw


def _stack_heads(ref, rows, pair):
    return jnp.concatenate([ref[rows, (2 * pair) * DH:(2 * pair + 1) * DH],
                            ref[rows, (2 * pair + 1) * DH:(2 * pair + 2) * DH]], axis=0)


def _stack_cols(val, lane_a, lane_b, bcast_rows=None):
    a, b = val[:, lane_a:lane_a + 1], val[:, lane_b:lane_b + 1]
    if bcast_rows:
        a, b = jnp.broadcast_to(a, (bcast_rows, 1)), jnp.broadcast_to(b, (bcast_rows, 1))
    return jnp.concatenate([a, b], axis=0)


def _pool_windows(ext, tm, reverse):
    n = tm + HALO
    outs = []
    for gi in range(NG):
        a = ext[:, gi * 128:(gi + 1) * 128]
        s = 1
        while s < POOL_WINDOWS[gi]:
            a = a + pltpu.roll(a, (n - s) if reverse else s, 0)
            s *= 2
        outs.append(a[0:tm] if reverse else a[HALO:HALO + tm])
    return jnp.concatenate(outs, axis=1)


def _pool_count(tm, tile_index):
    t1 = (lax.broadcasted_iota(jnp.int32, (tm, PW), 0) + tile_index * tm + 1).astype(F32)
    win = jnp.concatenate([jnp.full((tm, 128), float(w), F32) for w in POOL_WINDOWS], axis=1)
    return 1.0 / jnp.minimum(t1, win)


def _chunk_item(qn_ref, kn_ref, v_ref, bg_ref, gc_ref, c, pr):
    r0 = pl.multiple_of(c * CH, CH)
    rows = pl.ds(r0, CH)
    bgv = bg_ref[rows, :]
    gcv_all = gc_ref[rows, :]
    gl_all = gc_ref[pl.ds(r0 + CH - 1, 1), :]
    ha, hb = 2 * pr, 2 * pr + 1
    return (_stack_heads(qn_ref, rows, pr), _stack_heads(kn_ref, rows, pr), _stack_heads(v_ref, rows, pr),
            _stack_cols(bgv, ha, hb), _stack_cols(gcv_all, NH + ha, NH + hb), _stack_cols(gl_all, NH + ha, NH + hb, CH))


CHUNK_GROUP = 4


def _mix_core(x, gate, qn, kn, v, z, p, bg, gc, gnorm, pool_w, pool_scale, w_out):
    T = x.shape[0]
    tm = _tile(T)
    nc = tm // CH
    cg = CHUNK_GROUP if nc % CHUNK_GROUP == 0 else 1
    npb = nc * (NH // 2)

    def body(x_ref, gt_ref, qn_ref, kn_ref, v_ref, z_ref, p_ref, bg_ref, gc_ref, gnm_ref, pw_ref, ps_ref, wo_ref,
             xo_ref, mx_ref, cat_ref, o_ref, sall_ref, S_scr, pext, u_s, w_s, qe_s, kf_s, q_s):
        i = pl.program_id(0)

        @pl.when(i == 0)
        def _():
            S_scr[...] = jnp.zeros((NH * DH, DH), F32)
            pext[pl.ds(0, HALO), :] = jnp.zeros((HALO, PW), F32)

        cn = _pair_consts()

        def intra(g, carry):
            idx = [(g * cg + dc, pr) for dc in range(cg) for pr in range(NH // 2)]
            res = _pair_intra([_chunk_item(qn_ref, kn_ref, v_ref, bg_ref, gc_ref, c, pr) for c, pr in idx], cn)
            for (c, pr), d in zip(idx, res):
                pi = c * (NH // 2) + pr
                u_s[pi] = d["u"]
                w_s[pi] = d["w_bd"]
                qe_s[pi] = d["qE_bd"]
                kf_s[pi] = d["kF_bd"]
                q_s[pi] = d["Q"].astype(BF)
            return carry

        lax.fori_loop(0, nc // cg, intra, 0)

        def scan(c, carry):
            r0 = pl.multiple_of(c * CH, CH)
            rows = pl.ds(r0, CH)
            gl_all = gc_ref[pl.ds(r0 + CH - 1, 1), :]
            for pr in range(NH // 2):
                ha, hb = 2 * pr, 2 * pr + 1
                pi = c * (NH // 2) + pr
                S = S_scr[pl.ds(pr * 2 * DH, 2 * DH), :]
                sall_ref[c, ha:hb + 1] = S.reshape(2, DH, DH)
                _, egl_st = _egl_rows(_stack_cols(gl_all, NH + ha, NH + hb, CH))
                it = dict(u=u_s[pi], w_bd=w_s[pi], qE_bd=qe_s[pi], kF_bd=kf_s[pi], Q=q_s[pi])
                _, o, S_new = _pair_scan(it, S, egl_st)
                o_ref[rows, ha * DH:(ha + 1) * DH] = o[0:CH]
                o_ref[rows, hb * DH:(hb + 1) * DH] = o[CH:R2]
                S_scr[pl.ds(pr * 2 * DH, 2 * DH), :] = S_new
            return carry

        lax.fori_loop(0, nc, scan, 0)

        for hd in range(NH):
            cols = slice(hd * DH, (hd + 1) * DH)
            oh = o_ref[:, cols]
            zh = z_ref[:, cols]
            r = lax.rsqrt(jnp.mean(oh * oh, axis=-1, keepdims=True) + EPS)
            cat_ref[:, cols] = (oh * r * gnm_ref[...] * (zh * _sigmoid(zh))).astype(BF)

        pv = p_ref[...]
        pext[pl.ds(HALO, tm), :] = pv
        pooled = _pool_windows(pext[...], tm, False) * _pool_count(tm, i) - pv
        pext[pl.ds(0, HALO), :] = pext[pl.ds(tm, HALO), :]
        for gi in range(NG):
            cols = slice(gi * 128, (gi + 1) * 128)
            pm = _nn(pooled[:, cols].astype(BF), pw_ref[gi])
            cat_ref[:, GW + gi * 128:GW + (gi + 1) * 128] = (pm * ps_ref[:, cols]).astype(BF)

        mixed = _nn(cat_ref[...], wo_ref[...])
        mx_ref[...] = mixed
        xo_ref[...] = x_ref[...] + gt_ref[...] * mixed

    wide = lambda w: _row_spec(tm, w)
    return pl.pallas_call(
        body, grid=(T // tm,), name="mix_core",
        in_specs=[wide(D), _const_spec((1, D)), wide(GW), wide(GW), wide(GW), wide(GW), wide(PW), wide(128), wide(128),
                  _const_spec((1, DH)), _const_spec((NG, 128, 128)), _const_spec((1, PW)), _const_spec((D, D))],
        out_specs=[wide(D), wide(D), wide(D), wide(GW), pl.BlockSpec((nc, NH, DH, DH), lambda i: (i, 0, 0, 0))],
        out_shape=[jax.ShapeDtypeStruct((T, D), F32), jax.ShapeDtypeStruct((T, D), F32),
                   jax.ShapeDtypeStruct((T, D), BF), jax.ShapeDtypeStruct((T, GW), F32),
                   jax.ShapeDtypeStruct((T // CH, NH, DH, DH), F32)],
        scratch_shapes=[pltpu.VMEM((NH * DH, DH), F32), pltpu.VMEM((tm + HALO, PW), F32),
                        pltpu.VMEM((npb, R2, DH), F32), pltpu.VMEM((npb, R2, 2 * DH), BF),
                        pltpu.VMEM((npb, R2, 2 * DH), BF), pltpu.VMEM((npb, R2, 2 * DH), BF),
                        pltpu.VMEM((npb, R2, R2), BF)],
        compiler_params=_cparams(),
    )(x, gate, qn, kn, v, z, p, bg, gc, gnorm, pool_w, pool_scale, w_out)


def _pair_scan_bwd(it, S, dSn, do, egl, egl_st, cn):
    bf = lambda t: t.astype(BF)
    rowA, rowS = cn["rowA"], cn["rowS"]
    sel = lambda t: jnp.where(rowA, t[:, 0:DH], t[:, DH:2 * DH])
    Sb, dSb, dob = bf(S), bf(dSn), bf(do)
    vnb = bf(it["u"] - _nn(it["w_bd"], Sb))
    dvn = _tn(bf(it["Q"]), dob) + _nn(it["kF_bd"], dSb)
    dQ = _nt(dob, vnb)
    dqE = sel(_nt(dob, Sb))
    dkF = sel(_nt(vnb, dSb))
    dvnb = bf(dvn)
    dw = -sel(_nt(dvnb, Sb))
    dS_new = _tn(it["qE_bd"], dob) + egl_st * dSn - _tn(it["w_bd"], dvnb)
    prod = jnp.sum(dSn * S, axis=1, keepdims=True)
    d_egl_a = jnp.sum(jnp.where(rowS, prod, 0.0), axis=0, keepdims=True)
    d_egl_b = jnp.sum(jnp.where(rowS, 0.0, prod), axis=0, keepdims=True)
    return dict(dvn=dvn, dQ=dQ, dqE=dqE, dkF=dkF, dw=dw, degl=jnp.where(rowA, d_egl_a, d_egl_b) * egl), dS_new


def _pair_intra_bwd(items, cn):
    bf = lambda t: t.astype(BF)
    rowA = cn["rowA"]
    for d in items:
        d["kb"] = d["k"] * d["beta"]
        d["E"] = jnp.exp(d["gcv"])
        d["Fd"] = jnp.exp(d["gl"] - d["gcv"])
        d["dvb"] = _tmm(_tn, d["Tm"], d["dvn"])
        d["dkbE"] = _tmm(_tn, d["Tm"], d["dw"])
        dTm = _tmm(_nt, d["dvn"], d["v"] * d["beta"]) + _tmm(_nt, d["dw"], d["kb"] * d["E"])
        d["X"] = _tmm(_nt, dTm, d["Tm"])
    for d in items:
        d["dA"] = -_tmm(_tn, d["Tm"], d["X"])
    outs = []
    for d in items:
        q, k, v, beta, gam = d["q"], d["k"], d["v"], d["beta"], d["gam"]
        N = jnp.where(cn["strict"], d["dA"] * gam, 0.0)
        Rm = d["dQ"] * gam
        Wm = Rm * d["QK"] + N * d["P"]
        dgc = jnp.sum(Wm, axis=1, keepdims=True) - jnp.sum(Wm.T, axis=1, keepdims=True)
        kbf, Nb, Rb = bf(k), bf(N), bf(Rm)
        E, Fd = d["E"], d["Fd"]
        dq = d["dqE"] * E + _nn(Rb, kbf)
        dkb = d["dkbE"] * E + _nn(Nb, kbf)
        dk = d["dkF"] * Fd + _tn(Rb, bf(q)) + _tn(Nb, bf(d["kb"])) + beta * dkb
        dbeta = jnp.sum(dkb * k, axis=1, keepdims=True) + jnp.sum(d["dvb"] * v, axis=1, keepdims=True)
        dE = jnp.sum(d["dqE"] * q, axis=1, keepdims=True) + jnp.sum(d["dkbE"] * d["kb"], axis=1, keepdims=True)
        fdf = jnp.sum(d["dkF"] * k, axis=1, keepdims=True) * Fd
        dgl = d["degl"] + jnp.where(rowA, jnp.sum(jnp.where(rowA, fdf, 0.0), axis=0, keepdims=True),
                                    jnp.sum(jnp.where(rowA, 0.0, fdf), axis=0, keepdims=True))
        dgc = dgc + dE * E - fdf + jnp.where(cn["last"], dgl, 0.0)
        outs.append((dq, dk, beta * d["dvb"], dbeta, dgc))
    return outs


def _mix_core_bwd(dxo, gate, mixed, cat, o, sall, qn, kn, v, z, p, bg, gc, gnorm, pool_w, pool_scale, w_out,
                  scatter=()):
    T = dxo.shape[0]
    tm = _tile(T, 256)
    nt = T // tm
    nc = tm // CH
    ns = len(scatter)
    cg = CHUNK_GROUP if nc % CHUNK_GROUP == 0 else 1
    npb = nc * (NH // 2)

    def body(*refs):
        (dx_ref, gt_ref, mx_ref, cat_ref, o_ref, sall_ref, qn_ref, kn_ref, v_ref, z_ref, p_ref, ph_ref, bg_ref,
         gc_ref, gnm_ref, pw_ref, ps_ref, wo_ref) = refs[:18]
        (dq_ref, dk_ref, dv_ref, dz_ref, dp_ref, dbg_ref, dgt_ref, dwo_ref, dpw_ref, dps_ref,
         dgn_ref) = refs[18 + ns:29 + ns]
        dS_scr, pext, yext, do_buf, dwo_acc = refs[29 + 2 * ns:34 + 2 * ns]
        (gam_s, p_s, qk_s, tm_s, dqq_s, u_s, dvn_s, dqe_s, dkf_s, dw_s, w_s, qe_s, kf_s, q_s,
         degl_s) = refs[34 + 2 * ns:49 + 2 * ns]
        i = pl.program_id(0)
        ti = nt - 1 - i
        if ns:
            plan = _Scatter(refs[18:18 + ns], refs[29 + ns:29 + 2 * ns], *refs[49 + 2 * ns:])
            _comm_begin(i, plan, nt - 1)

        @pl.when(i == 0)
        def _():
            dS_scr[...] = jnp.zeros((NH * DH, DH), F32)
            yext[pl.ds(tm, HALO), :] = jnp.zeros((HALO, PW), F32)
            dwo_acc[...] = jnp.zeros((D, D), F32)
            dgt_ref[...] = jnp.zeros((8, D), F32)
            dpw_ref[...] = jnp.zeros((NG, 128, 128), F32)
            dps_ref[...] = jnp.zeros((8, PW), F32)
            dgn_ref[...] = jnp.zeros((8, DH), F32)

        dx2 = dx_ref[...]
        dgt_ref[...] += _sum8(mx_ref[...] * dx2)
        dmix = (gt_ref[...] * dx2).astype(BF)
        dcat = _nt(dmix, wo_ref[...])
        dwo_acc[...] += _tn(cat_ref[...], dmix)

        pv = p_ref[...]
        pext[pl.ds(0, HALO), :] = jnp.where(ti == 0, 0.0, ph_ref[...])
        pext[pl.ds(HALO, tm), :] = pv
        inv_cnt = _pool_count(tm, ti)
        pooled = _pool_windows(pext[...], tm, False) * inv_cnt - pv
        dpooled = []
        for gi in range(NG):
            cols = slice(gi * 128, (gi + 1) * 128)
            pgb = pooled[:, cols].astype(BF)
            pm = _nn(pgb, pw_ref[gi])
            dpo = dcat[:, GW + gi * 128:GW + (gi + 1) * 128]
            dps_ref[:, cols] += _sum8(dpo * pm)
            dpm = (dpo * ps_ref[:, cols]).astype(BF)
            dpooled.append(_nt(dpm, pw_ref[gi]))
            dpw_ref[gi] += _tn(pgb, dpm)
        dpooled = jnp.concatenate(dpooled, axis=1)
        y = dpooled * inv_cnt
        yext[pl.ds(0, tm), :] = y
        dp_ref[...] = _pool_windows(yext[...], tm, True) - dpooled
        yext[pl.ds(tm, HALO), :] = y[0:HALO]

        gnm = gnm_ref[...]
        dgn = jnp.zeros((8, DH), F32)
        for hd in range(NH):
            cols = slice(hd * DH, (hd + 1) * DH)
            oh = o_ref[:, cols]
            zh = z_ref[:, cols]
            r = lax.rsqrt(jnp.mean(oh * oh, axis=-1, keepdims=True) + EPS)
            n = oh * r
            sg = _sigmoid(zh)
            zs = zh * sg
            dgo = dcat[:, cols]
            dgn = dgn + _sum8(dgo * zs * n)
            dn = dgo * zs * gnm
            do_buf[:, cols] = r * (dn - n * jnp.mean(dn * n, axis=-1, keepdims=True))
            dz_ref[:, cols] = dgo * n * gnm * (sg * (1.0 + zh * (1.0 - sg)))
        dgn_ref[...] += dgn

        cn = _pair_consts()
        lane_c = lax.broadcasted_iota(jnp.int32, (CH, 128), 1)

        def intra(g, carry):
            idx = [(g * cg + dc, pr) for dc in range(cg) for pr in range(NH // 2)]
            res = _pair_intra([_chunk_item(qn_ref, kn_ref, v_ref, bg_ref, gc_ref, c, pr) for c, pr in idx], cn)
            for (c, pr), d in zip(idx, res):
                pi = c * (NH // 2) + pr
                gam_s[pi], p_s[pi], qk_s[pi], tm_s[pi], u_s[pi] = d["gam"], d["P"], d["QK"], d["Tm"], d["u"]
                w_s[pi], qe_s[pi], kf_s[pi], q_s[pi] = d["w_bd"], d["qE_bd"], d["kF_bd"], d["Q"].astype(BF)
            return carry

        lax.fori_loop(0, nc // cg, intra, 0)

        def scan(cc, carry):
            c = nc - 1 - cc
            r0 = pl.multiple_of(c * CH, CH)
            rows = pl.ds(r0, CH)
            gl_all = gc_ref[pl.ds(r0 + CH - 1, 1), :]
            for pr in range(NH // 2):
                ha, hb = 2 * pr, 2 * pr + 1
                pi = c * (NH // 2) + pr
                srows = pl.ds(pr * 2 * DH, 2 * DH)
                S = sall_ref[c, ha:hb + 1].reshape(2 * DH, DH)
                egl, egl_st = _egl_rows(_stack_cols(gl_all, NH + ha, NH + hb, CH))
                it = dict(u=u_s[pi], w_bd=w_s[pi], qE_bd=qe_s[pi], kF_bd=kf_s[pi], Q=q_s[pi])
                g, dS_new = _pair_scan_bwd(it, S, dS_scr[srows, :], _stack_heads(do_buf, rows, pr), egl, egl_st, cn)
                dvn_s[pi], dqq_s[pi], dqe_s[pi], dkf_s[pi], dw_s[pi] = g["dvn"], g["dQ"], g["dqE"], g["dkF"], g["dw"]
                degl_s[pi] = g["degl"]
                dS_scr[srows, :] = dS_new
            return carry

        lax.fori_loop(0, nc, scan, 0)

        def intra_bwd(g, carry):
            idx = [(g * cg + dc, pr) for dc in range(cg) for pr in range(NH // 2)]
            items = []
            for c, pr in idx:
                pi = c * (NH // 2) + pr
                q, k, vv, beta, gcv, gl = _chunk_item(qn_ref, kn_ref, v_ref, bg_ref, gc_ref, c, pr)
                items.append(dict(q=q, k=k, v=vv, beta=beta, gcv=gcv, gl=gl, gam=gam_s[pi], P=p_s[pi], QK=qk_s[pi],
                                  Tm=tm_s[pi], dvn=dvn_s[pi], dQ=dqq_s[pi], dqE=dqe_s[pi], dkF=dkf_s[pi], dw=dw_s[pi],
                                  degl=degl_s[pi]))
            res = _pair_intra_bwd(items, cn)
            for dc in range(cg):
                c = g * cg + dc
                rows = pl.ds(pl.multiple_of(c * CH, CH), CH)
                dbg = jnp.zeros((CH, 128), F32)
                for pr in range(NH // 2):
                    dq, dk, dv, dbeta, dgc = res[dc * (NH // 2) + pr]
                    for hd, half in ((2 * pr, slice(0, CH)), (2 * pr + 1, slice(CH, R2))):
                        cols = slice(hd * DH, (hd + 1) * DH)
                        dq_ref[rows, cols] = dq[half]
                        dk_ref[rows, cols] = dk[half]
                        dv_ref[rows, cols] = dv[half]
                        dbg = dbg + jnp.where(lane_c == hd, dbeta[half], 0.0) + jnp.where(lane_c == NH + hd, dgc[half], 0.0)
                dbg_ref[rows, :] = dbg
            return carry

        lax.fori_loop(0, nc // cg, intra_bwd, 0)

        lane = lax.broadcasted_iota(jnp.int32, (tm, 128), 1)
        row = lax.broadcasted_iota(jnp.int32, (tm, 128), 0) % CH
        dbg_all = dbg_ref[...]
        dg = _seg_cumsum(jnp.where(lane >= NH, dbg_all, 0.0), row, reverse=True)
        dbg_ref[...] = jnp.where(lane < NH, dbg_all, dg)

        @pl.when(i == nt - 1)
        def _():
            dwo_ref[...] = dwo_acc[...].astype(BF)

        if ns:
            _comm_end(i, plan, nt - 1)

    rev = lambda w: pl.BlockSpec((tm, w), lambda i: (nt - 1 - i, 0))
    halo = pl.BlockSpec((HALO, PW), lambda i: (jnp.maximum((nt - 1 - i) * (tm // HALO) - 1, 0), 0))
    shp = lambda w: jax.ShapeDtypeStruct((T, w), F32)
    fix = lambda *s: pl.BlockSpec(s, lambda i: (0,) * len(s))
    return pl.pallas_call(
        body, grid=(nt,), name="mix_core_bwd",
        in_specs=[rev(D), _const_spec((1, D)), rev(D), rev(D), rev(GW),
                  pl.BlockSpec((nc, NH, DH, DH), lambda i: (nt - 1 - i, 0, 0, 0)),
                  rev(GW), rev(GW), rev(GW), rev(GW), rev(PW), halo, rev(128), rev(128),
                  _const_spec((1, DH)), _const_spec((NG, 128, 128)), _const_spec((1, PW)), _const_spec((D, D))]
        + [ANY] * ns,
        out_specs=[rev(GW), rev(GW), rev(GW), rev(GW), rev(PW), rev(128),
                   fix(8, D), fix(D, D), fix(NG, 128, 128), fix(8, PW), fix(8, DH)] + [ANY] * ns,
        out_shape=[shp(GW), shp(GW), shp(GW), shp(GW), shp(PW), shp(128),
                   jax.ShapeDtypeStruct((8, D), F32), jax.ShapeDtypeStruct((D, D), BF),
                   jax.ShapeDtypeStruct((NG, 128, 128), F32), jax.ShapeDtypeStruct((8, PW), F32),
                   jax.ShapeDtypeStruct((8, DH), F32)] + [jax.ShapeDtypeStruct(t.shape, t.dtype) for t in scatter],
        scratch_shapes=[pltpu.VMEM((NH * DH, DH), F32), pltpu.VMEM((tm + HALO, PW), F32),
                        pltpu.VMEM((tm + HALO, PW), F32), pltpu.VMEM((tm, GW), F32), pltpu.VMEM((D, D), F32)]
        + [pltpu.VMEM((npb, R2, R2), F32)] * 5 + [pltpu.VMEM((npb, R2, DH), F32)] * 5
        + [pltpu.VMEM((npb, R2, 2 * DH), BF)] * 3 + [pltpu.VMEM((npb, R2, R2), BF), pltpu.VMEM((npb, R2, 1), F32)]
        + (_scatter_sems(ns) if ns else []),
        compiler_params=_cparams(),
    )(dxo, gate, mixed, cat, o, sall, qn, kn, v, z, p, p, bg, gc, gnorm, pool_w, pool_scale, w_out, *scatter)


def _mix_proj_bwd(dxo, dqn, dkn, dv, dz, dp, dbg, xq, ba, x, shift, scale, gain, winT, conv_w, alog, dtb):
    T = x.shape[0]
    tm = min(256, T)
    nt = T // tm
    W3 = 3 * GW

    def body(dxo_ref, dqn_ref, dkn_ref, dv_ref, dz_ref, dp_ref, dbg_ref, xq_ref, xh_ref, ba_ref, x_ref, sh_ref, sc_ref,
             gn_ref, w_ref, cw_ref, al_ref, dt_ref,
             dx_ref, dw_ref, dcw_ref, dal_ref, ddt_ref, dsh_ref, dsc_ref, dgn_ref,
             ext, dcext, dproj, dw_acc):
        i = pl.program_id(0)
        ti = nt - 1 - i

        @pl.when(i == 0)
        def _():
            dcext[pl.ds(tm, 8), :] = jnp.zeros((8, W3), F32)
            dw_acc[...] = jnp.zeros((DINP, D), F32)
            dcw_ref[...] = jnp.zeros((4, 8, W3), F32)
            dal_ref[...] = jnp.zeros((8, 128), F32)
            ddt_ref[...] = jnp.zeros((8, 128), F32)
            dsh_ref[...] = jnp.zeros((8, D), F32)
            dsc_ref[...] = jnp.zeros((8, D), F32)
            dgn_ref[...] = jnp.zeros((8, D), F32)

        ext[pl.ds(0, 8), :] = jnp.where(ti == 0, 0.0, xh_ref[...])
        ext[pl.ds(8, tm), :] = xq_ref[...]
        c, sg = _conv_silu(ext, cw_ref, tm)
        qt = c * sg
        dsilu = sg * (1.0 + c * (1.0 - sg))
        for hd in range(NH):
            for part, dref, mult in ((0, dqn_ref, DH ** -0.5), (1, dkn_ref, 1.0)):
                cols = slice(part * GW + hd * DH, part * GW + (hd + 1) * DH)
                xh = qt[:, cols]
                rr = lax.rsqrt(jnp.sum(xh * xh, axis=-1, keepdims=True) + EPS)
                unit = xh * rr
                du = dref[:, hd * DH:(hd + 1) * DH] * mult
                dxh = rr * (du - unit * jnp.sum(du * unit, axis=-1, keepdims=True))
                dcext[pl.ds(0, tm), cols] = dxh * dsilu[:, cols]
        dcext[pl.ds(0, tm), 2 * GW:W3] = dv_ref[...] * dsilu[:, 2 * GW:W3]
        dc = dcext[pl.ds(0, tm), :]
        dxq = jnp.zeros((tm, W3), F32)
        for j in range(4):
            dcw_ref[j] += _sum8(dc * ext[pl.ds(5 + j, tm), :])
            dxq = dxq + cw_ref[j:j + 1, :] * dcext[pl.ds(3 - j, tm), :]
        dcext[pl.ds(tm, 8), :] = dc[0:8]
        lane = lax.broadcasted_iota(jnp.int32, (tm, 128), 1)
        bav = ba_ref[...]
        beta, g, sarg = _gates(bav, al_ref[...], dt_ref[...], lane)
        dbg_v = dbg_ref[...]
        is_g = (lane >= NH) & (lane < 2 * NH)
        dbraw = jnp.where(lane < NH, dbg_v * beta * (1.0 - beta), 0.0)
        daraw = jnp.where(is_g, dbg_v * (-jnp.exp(al_ref[...])) * sarg, 0.0)
        dal_ref[...] += _sum8(jnp.where(is_g, dbg_v * g, 0.0))
        ddt_ref[...] += _sum8(daraw)
        dproj[:, 0:W3] = dxq.astype(BF)
        dproj[:, W3:W3 + GW] = dz_ref[...].astype(BF)
        dproj[:, W3 + GW:W3 + GW + PW] = dp_ref[...].astype(BF)
        dproj[:, W3 + GW + PW:DINP] = (dbraw + daraw).astype(BF)
        gain_v, scale_v = gn_ref[...], sc_ref[...]
        n, r, y, h = _norm_mod_fwd(x_ref[...], gain_v, sh_ref[...], scale_v)
        dpj = dproj[...]
        dh = _nn(dpj, w_ref[...])
        dw_acc[...] += _tn(dpj, h.astype(BF))
        dxn, dsh, dsc, dgn = _norm_mod_bwd(dh, n, r, y, gain_v, scale_v)
        dx_ref[...] = dxo_ref[...] + dxn
        dsh_ref[...] += dsh
        dsc_ref[...] += dsc
        dgn_ref[...] += dgn

        @pl.when(i == nt - 1)
        def _():
            dw_ref[...] = dw_acc[...].astype(BF)

    rev = lambda w: pl.BlockSpec((tm, w), lambda i: (nt - 1 - i, 0))
    halo = pl.BlockSpec((8, W3), lambda i: (jnp.maximum((nt - 1 - i) * (tm // 8) - 1, 0), 0))
    fix = lambda *s: pl.BlockSpec(s, lambda i: (0,) * len(s))
    vec = _const_spec((1, D))
    return pl.pallas_call(
        body, grid=(nt,), name="mix_proj_bwd",
        in_specs=[rev(D), rev(GW), rev(GW), rev(GW), rev(GW), rev(PW), rev(128), rev(W3), halo, rev(128), rev(D),
                  vec, vec, vec, _const_spec((DINP, D)), _const_spec((4, W3)), _const_spec((1, 128)),
                  _const_spec((1, 128))],
        out_specs=[rev(D), fix(DINP, D), fix(4, 8, W3), fix(8, 128), fix(8, 128), fix(8, D), fix(8, D), fix(8, D)],
        out_shape=[jax.ShapeDtypeStruct((T, D), F32), jax.ShapeDtypeStruct((DINP, D), BF),
                   jax.ShapeDtypeStruct((4, 8, W3), F32), jax.ShapeDtypeStruct((8, 128), F32),
                   jax.ShapeDtypeStruct((8, 128), F32), jax.ShapeDtypeStruct((8, D), F32),
                   jax.ShapeDtypeStruct((8, D), F32), jax.ShapeDtypeStruct((8, D), F32)],
        scratch_shapes=[pltpu.VMEM((tm + 8, W3), F32), pltpu.VMEM((tm + 8, W3), F32), pltpu.VMEM((tm, DINP), BF),
                        pltpu.VMEM((DINP, D), F32)],
        compiler_params=_cparams(),
    )(dxo, dqn, dkn, dv, dz, dp, dbg, xq, xq, ba, x, shift, scale, gain, winT, conv_w, alog, dtb)


MESH = pl.DeviceIdType.MESH
CHIP_RELS = ((1, 0), (0, 1), (1, 1))
DEV_RELS = tuple((dx, dy, dc) for dx in (0, 1) for dy in (0, 1) for dc in (0, 1) if (dx, dy, dc) != (0, 0, 0))
NMOD = 9
ADA_SH = NMOD * D // 4
WIN_SH = DIN // 4
WIN_PAD = 672
MSG_ROWS = 16
ANY = pl.BlockSpec(memory_space=pl.ANY)
VM = pl.BlockSpec(memory_space=pltpu.VMEM)


def _place():
    x, y, c = lax.axis_index("x"), lax.axis_index("y"), lax.axis_index("c")
    return x, y, c


class _SplitGather:
    N_SEMS = (3, 3, 3, 3, 1)

    def __init__(self, src, dst, ici_s, ici_r, d2d_s, d2d_r, lsem):
        self.src, self.dst = src, dst
        self.sems = (ici_s, ici_r, d2d_s, d2d_r)
        self.x, self.y, self.c = _place()
        self.myj = 2 * self.x + self.y
        half = src.shape[0] // 2
        self.mine = pl.ds(pl.multiple_of(self.c * half, 16), half)
        self.other = pl.ds(pl.multiple_of((1 - self.c) * half, 16), half)
        self.local = pltpu.make_async_copy(src, dst.at[self.myj], lsem.at[0])

    def _ici(self, k, slot):
        dx, dy = CHIP_RELS[k]
        return pltpu.make_async_remote_copy(
            src_ref=self.src.at[self.mine], dst_ref=self.dst.at[slot, self.mine], send_sem=self.sems[0].at[k],
            recv_sem=self.sems[1].at[k], device_id=(self.x ^ dx, self.y ^ dy, self.c), device_id_type=MESH)

    def _d2d(self, k, rows):
        dx, dy = CHIP_RELS[k]
        blk = self.dst.at[2 * (self.x ^ dx) + (self.y ^ dy), rows]
        return pltpu.make_async_remote_copy(
            src_ref=blk, dst_ref=blk, send_sem=self.sems[2].at[k], recv_sem=self.sems[3].at[k],
            device_id=(self.x, self.y, 1 - self.c), device_id_type=MESH)

    def start(self):
        self.local.start()
        for k in range(3):
            self._ici(k, self.myj).start()

    def forward(self):
        for k, (dx, dy) in enumerate(CHIP_RELS):
            self._ici(k, 2 * (self.x ^ dx) + (self.y ^ dy)).wait_recv()
            self._d2d(k, self.mine).start()

    def finish(self):
        for k in range(3):
            self._d2d(k, self.other).wait_recv()
        for k in range(3):
            self._d2d(k, self.mine).wait_send()
            self._ici(k, self.myj).wait_send()
        self.local.wait()


class _Scatter:
    def __init__(self, ins, outs, send, recv, lsem):
        self.ins, self.outs, self.send, self.recv, self.lsem = ins, outs, send, recv, lsem
        self.x, self.y, self.c = _place()
        self.myj = 2 * self.x + self.y

    def _copy(self, a, k, landing):
        dx, dy = CHIP_RELS[k]
        pj = 2 * (self.x ^ dx) + (self.y ^ dy)
        return pltpu.make_async_remote_copy(
            src_ref=self.ins[a].at[pj], dst_ref=self.outs[a].at[pj if landing else self.myj],
            send_sem=self.send.at[a, k], recv_sem=self.recv.at[a, k],
            device_id=(self.x ^ dx, self.y ^ dy, self.c), device_id_type=MESH)

    def _local(self, a):
        return pltpu.make_async_copy(self.ins[a].at[self.myj], self.outs[a].at[self.myj], self.lsem.at[a])

    def start(self):
        for a in range(len(self.ins)):
            self._local(a).start()
            for k in range(3):
                self._copy(a, k, False).start()

    def finish(self):
        for a in range(len(self.ins)):
            for k in range(3):
                self._copy(a, k, True).wait_recv()
            for k in range(3):
                self._copy(a, k, False).wait_send()
            self._local(a).wait()


def _scatter_sems(n):
    return [pltpu.SemaphoreType.DMA((n, 3)), pltpu.SemaphoreType.DMA((n, 3)), pltpu.SemaphoreType.DMA((n,))]


def _gather_sems():
    return [pltpu.SemaphoreType.DMA((k,)) for k in _SplitGather.N_SEMS]


def _comm_begin(i, plan, last):
    @pl.when(i == 0)
    def _():
        plan.start()

    if hasattr(plan, "forward"):
        @pl.when(i == max(last - 3, 0))
        def _():
            plan.forward()


def _comm_end(i, plan, last):
    @pl.when(i == last)
    def _():
        plan.finish()


def _ada_exchange(msg, w_ada, b_ada, wblock):
    def body(msg_ref, w_ref, b_ref, wb_ref, all_ref, mod_ref, wg_ref, modp, send1, recv1, send2, recv2, lsem, *gsems):
        x, y, c = _place()
        me = 4 * x + 2 * y + c
        own = pltpu.make_async_copy(msg_ref, all_ref.at[me], lsem.at[0])
        own.start()

        def gather(k, rel, slot):
            dx, dy, dc = rel
            return pltpu.make_async_remote_copy(
                src_ref=msg_ref, dst_ref=all_ref.at[slot], send_sem=send1.at[k], recv_sem=recv1.at[k],
                device_id=(x ^ dx, y ^ dy, c ^ dc), device_id_type=MESH)

        for k, rel in enumerate(DEV_RELS):
            gather(k, rel, me).start()
        for k, (dx, dy, dc) in enumerate(DEV_RELS):
            gather(k, (dx, dy, dc), 4 * (x ^ dx) + 2 * (y ^ dy) + (c ^ dc)).wait_recv()
        for k, rel in enumerate(DEV_RELS):
            gather(k, rel, me).wait_send()
        own.wait()
        wgather = _SplitGather(wb_ref, wg_ref, *gsems)
        wgather.start()

        for d in range(8):
            cv = all_ref[d, 0:8, :]
            act = cv * _sigmoid(cv)
            modp[d] = _nn(act, w_ref[...], precision=HI) + b_ref[...]

        myj = 2 * x + y
        keep = pltpu.make_async_copy(modp.at[me], mod_ref.at[myj], lsem.at[1])
        keep.start()

        def scatter(k, rel):
            dx, dy = rel
            return pltpu.make_async_remote_copy(
                src_ref=modp.at[4 * (x ^ dx) + 2 * (y ^ dy) + c], dst_ref=mod_ref.at[myj],
                send_sem=send2.at[k], recv_sem=recv2.at[k], device_id=(x ^ dx, y ^ dy, c), device_id_type=MESH)

        def landed(k, rel):
            dx, dy = rel
            return pltpu.make_async_remote_copy(
                src_ref=modp.at[me], dst_ref=mod_ref.at[2 * (x ^ dx) + (y ^ dy)],
                send_sem=send2.at[k], recv_sem=recv2.at[k], device_id=(x ^ dx, y ^ dy, c), device_id_type=MESH)

        for k, rel in enumerate(CHIP_RELS):
            scatter(k, rel).start()
        for k, rel in enumerate(CHIP_RELS):
            landed(k, rel).wait_recv()
        for k, rel in enumerate(CHIP_RELS):
            scatter(k, rel).wait_send()
        keep.wait()
        wgather.forward()
        wgather.finish()

    return pl.pallas_call(
        body, name="ada_exchange", in_specs=[VM, VM, VM, ANY], out_specs=[VM, VM, ANY],
        out_shape=[jax.ShapeDtypeStruct((8, MSG_ROWS, D), F32), jax.ShapeDtypeStruct((4, 8, ADA_SH), F32),
                   jax.ShapeDtypeStruct((4,) + wblock.shape, wblock.dtype)],
        scratch_shapes=[pltpu.VMEM((8, 8, ADA_SH), F32), pltpu.SemaphoreType.DMA((7,)), pltpu.SemaphoreType.DMA((7,)),
                        pltpu.SemaphoreType.DMA((3,)), pltpu.SemaphoreType.DMA((3,)), pltpu.SemaphoreType.DMA((2,))]
        + _gather_sems(),
        compiler_params=pltpu.CompilerParams(vmem_limit_bytes=VMEM_LIMIT),
    )(msg, w_ada, b_ada, wblock)


def _chip_exchange(parts, name):
    n = len(parts)

    def body(*refs):
        plan = _Scatter(refs[:n], refs[n:2 * n], *refs[2 * n:])
        plan.start()
        plan.finish()

    return pl.pallas_call(
        body, name=name, in_specs=[ANY] * n, out_specs=[ANY] * n,
        out_shape=[jax.ShapeDtypeStruct(p.shape, p.dtype) for p in parts], scratch_shapes=_scatter_sems(n),
    )(*parts)


def _all_exchange(parts, name):
    n = len(parts)

    def body(*refs):
        ins, outs = refs[:n], refs[n:2 * n]
        send, recv, lsem = refs[2 * n:]
        x, y, c = _place()
        me = 4 * x + 2 * y + c
        local, remote = [], []
        for a in range(n):
            local.append(pltpu.make_async_copy(ins[a], outs[a].at[me], lsem.at[a]))
            for k, (dx, dy, dc) in enumerate(DEV_RELS):
                remote.append((pltpu.make_async_remote_copy(
                    src_ref=ins[a], dst_ref=outs[a].at[me], send_sem=send.at[a, k], recv_sem=recv.at[a, k],
                    device_id=(x ^ dx, y ^ dy, c ^ dc), device_id_type=MESH), a, k,
                    4 * (x ^ dx) + 2 * (y ^ dy) + (c ^ dc)))
        for cp in local:
            cp.start()
        for cp, _, _, _ in remote:
            cp.start()
        for _, a, k, pd in remote:
            pltpu.make_async_remote_copy(
                src_ref=ins[a], dst_ref=outs[a].at[pd], send_sem=send.at[a, k], recv_sem=recv.at[a, k],
                device_id=(x, y, c), device_id_type=MESH).wait_recv()
        for cp, _, _, _ in remote:
            cp.wait_send()
        for cp in local:
            cp.wait()

    shapes = [jax.ShapeDtypeStruct((8,) + tuple(p.shape), p.dtype) for p in parts]
    return pl.pallas_call(
        body, name=name, in_specs=[ANY] * n, out_specs=[ANY] * n, out_shape=shapes,
        scratch_shapes=[pltpu.SemaphoreType.DMA((n, 7)), pltpu.SemaphoreType.DMA((n, 7)), pltpu.SemaphoreType.DMA((n,))],
    )(*parts)


def _pair_exchange(parts, name):
    n = len(parts)

    def body(*refs):
        ins, outs = refs[:n], refs[n:2 * n]
        send, recv = refs[2 * n:]
        x, y, c = _place()
        cps = [pltpu.make_async_remote_copy(
            src_ref=ins[a], dst_ref=outs[a], send_sem=send.at[a], recv_sem=recv.at[a],
            device_id=(x, y, 1 - c), device_id_type=MESH) for a in range(n)]
        for cp in cps:
            cp.start()
        for cp in cps:
            cp.wait_recv()
        for cp in cps:
            cp.wait_send()

    shapes = [jax.ShapeDtypeStruct(p.shape, p.dtype) for p in parts]
    return pl.pallas_call(
        body, name=name, in_specs=[ANY] * n, out_specs=[ANY] * n, out_shape=shapes,
        scratch_shapes=[pltpu.SemaphoreType.DMA((n,)), pltpu.SemaphoreType.DMA((n,))],
    )(*parts)


def _row_tile(rows, cap):
    best = rows
    for t in range(8, min(cap, rows) + 1, 8):
        if rows % t == 0:
            best = t
    return best if rows % 8 == 0 else rows


def _sum_slots(parts, name):
    n, rows, width = parts.shape
    tr = _row_tile(rows, 352)

    def body(p_ref, o_ref):
        acc = p_ref[0].astype(F32)
        for j in range(1, n):
            acc = acc + p_ref[j].astype(F32)
        o_ref[...] = acc

    return pl.pallas_call(
        body, grid=(rows // tr,), name=name,
        in_specs=[pl.BlockSpec((n, tr, width), lambda i: (0, i, 0))],
        out_specs=pl.BlockSpec((tr, width), lambda i: (i, 0)),
        out_shape=jax.ShapeDtypeStruct((rows, width), F32),
        compiler_params=_cparams(),
    )(parts)


def _adamw_math(g, w, m, v):
    m_new = ADAM_B1 * m + (1.0 - ADAM_B1) * g
    v_new = ADAM_B2 * v + (1.0 - ADAM_B2) * (g * g)
    m_hat = m_new / (1.0 - ADAM_B1 ** ADAM_STEP)
    v_hat = v_new / (1.0 - ADAM_B2 ** ADAM_STEP)
    delta = -ADAM_LR * (m_hat / (jnp.sqrt(v_hat) + ADAM_EPS) + ADAM_WD * w)
    return delta, m_new, v_new


def _adamw(grads, w, m, v, name):
    rows, width = w.shape
    tr = _row_tile(rows, 256 if width <= 1024 else 128)
    ng = len(grads)

    def body(*refs):
        g = refs[0][...]
        for r in refs[1:ng]:
            g = g + r[...]
        w_ref, m_ref, v_ref, g_out, d_out, m_out, v_out = refs[ng:]
        delta, m_new, v_new = _adamw_math(g, w_ref[...], m_ref[...], v_ref[...])
        g_out[...] = g
        d_out[...] = delta
        m_out[...] = m_new
        v_out[...] = v_new

    if rows % 8 == 0 or width % 512:
        blk, steps = pl.BlockSpec((tr, width), lambda i: (i, 0)), rows // tr
    else:
        blk, steps = pl.BlockSpec((rows, 256), lambda i: (0, i)), width // 256
    return pl.pallas_call(
        body, grid=(steps,), name=name,
        in_specs=[blk] * (ng + 3), out_specs=[blk] * 4,
        out_shape=[jax.ShapeDtypeStruct((rows, width), F32)] * 4,
        compiler_params=_cparams(),
    )(*grads, w, m, v)


def _adamw_ada(msgs, dmods, w, m, v):
    rows, width = w.shape
    tr = 128

    def body(c_ref, dm_ref, w_ref, m_ref, v_ref, g_out, d_out, m_out, v_out):
        cv = jnp.concatenate([c_ref[d, 0:1, :] for d in range(8)], axis=0)
        act = cv * _sigmoid(cv)
        g = _tn(act, dm_ref[...], precision=HI)
        delta, m_new, v_new = _adamw_math(g, w_ref[...], m_ref[...], v_ref[...])
        g_out[...] = g
        d_out[...] = delta
        m_out[...] = m_new
        v_out[...] = v_new

    blk = pl.BlockSpec((tr, width), lambda i: (i, 0))
    return pl.pallas_call(
        body, grid=(rows // tr,), name="adamw_w_ada",
        in_specs=[pl.BlockSpec((8, MSG_ROWS, tr), lambda i: (0, 0, i)), pl.BlockSpec((8, width), lambda i: (0, 0)),
                  blk, blk, blk],
        out_specs=[blk] * 4, out_shape=[jax.ShapeDtypeStruct((rows, width), F32)] * 4,
        compiler_params=_cparams(),
    )(msgs, dmods, w, m, v)


def _adamw_small(parts, w, m, v, name):
    n, rows, width = parts.shape

    def body(p_ref, w_ref, m_ref, v_ref, g_out, d_out, m_out, v_out):
        g = p_ref[0]
        for j in range(1, n):
            g = g + p_ref[j]
        delta, m_new, v_new = _adamw_math(g, w_ref[...], m_ref[...], v_ref[...])
        g_out[...] = g
        d_out[...] = delta
        m_out[...] = m_new
        v_out[...] = v_new

    return pl.pallas_call(
        body, name=name, in_specs=[VM] * 4, out_specs=[VM] * 4,
        out_shape=[jax.ShapeDtypeStruct((rows, width), F32)] * 4,
        compiler_params=pltpu.CompilerParams(vmem_limit_bytes=VMEM_LIMIT),
    )(parts, w, m, v)


SMALL_ROWS = 24


def _pad_row(vec, width=D):
    vec = vec.reshape(1, -1)
    return jnp.pad(vec, ((0, 0), (0, width - vec.shape[1])))


def _lanes_4_7(vec4):
    return jnp.zeros((1, 128), F32).at[0, NH:2 * NH].set(vec4.reshape(NH))


def kernel(x, c, w_ada, b_ada, norm_ffn1, ffn1_gate, ffn1_up, ffn1_down, norm_mix, w_in, conv_w, a_log, dt_bias, gdn_norm, pool_w, pool_scale, w_out, norm_ffn2, ffn2_gate, ffn2_up, ffn2_down, final_norm, loss_target, m_w_ada, m_b_ada, m_norm_ffn1, m_ffn1_gate, m_ffn1_up, m_ffn1_down, m_norm_mix, m_w_in, m_conv_w, m_a_log, m_dt_bias, m_gdn_norm, m_pool_w, m_pool_scale, m_w_out, m_norm_ffn2, m_ffn2_gate, m_ffn2_up, m_ffn2_down, m_final_norm, v_w_ada, v_b_ada, v_norm_ffn1, v_ffn1_gate, v_ffn1_up, v_ffn1_down, v_norm_mix, v_w_in, v_conv_w, v_a_log, v_dt_bias, v_gdn_norm, v_pool_w, v_pool_scale, v_w_out, v_norm_ffn2, v_ffn2_gate, v_ffn2_up, v_ffn2_down, v_final_norm):
    xs = x[0]
    tgt = loss_target[0]
    chip = 2 * lax.axis_index("x") + lax.axis_index("y")
    me = 2 * chip + lax.axis_index("c")

    fsh = FF // 4
    block_a = jnp.concatenate([ffn1_gate[0].T, ffn1_up[0].T, ffn1_down[0]], axis=0).astype(BF)
    block_b = jnp.concatenate([ffn2_gate[0].T, ffn2_up[0].T, ffn2_down[0], w_out[0],
                               jnp.pad(w_in[0].T, ((0, WIN_PAD - WIN_SH), (0, 0)))], axis=0).astype(BF)

    msg = jnp.concatenate([jnp.broadcast_to(c, (8, D)), jnp.pad(conv_w[0], ((0, 0), (0, D - 3 * GW // 4))),
                           jnp.zeros((MSG_ROWS - 12, D), F32)], axis=0)
    b_sh = lax.dynamic_slice(b_ada, (0, chip * ADA_SH), (1, ADA_SH))
    msgs, mod4, gath_a = _ada_exchange(msg, w_ada[0], b_sh, block_a)
    mod = mod4[:, 0, :].reshape(NMOD, D)
    mrow = [mod[i:i + 1] for i in range(NMOD)]
    conv_full = jnp.concatenate([msgs[2 * j, 8:12, :3 * GW // 4] for j in range(4)], axis=1)
    alog, dtb = _lanes_4_7(a_log), _lanes_4_7(dt_bias)
    gnm = gdn_norm.reshape(1, DH)
    pwb = pool_w[0].astype(BF)
    psc = pool_scale.reshape(1, PW)
    fin = final_norm.reshape(1, D)

    x1, f1, a1, b1, gath_b = _ffn_fwd(xs, mrow[0], mrow[1], mrow[2], norm_ffn1, gath_a, 0, "ffn1_fwd", block_b)
    wo = gath_b[:, 3 * fsh:3 * fsh + D // 4, :].reshape(D, D)
    win_nat = gath_b[:, 3 * fsh + D // 4:3 * fsh + D // 4 + WIN_SH, :].reshape(DIN, D)
    winT = jnp.concatenate([win_nat[:4 * GW], win_nat[4 * GW + 2 * NH:], win_nat[4 * GW:4 * GW + 2 * NH],
                            jnp.zeros((128 - 2 * NH, D), BF)], axis=0)
    xq, ba, qn, kn, vv, z, pp, bg, gc = _mix_proj(x1, mrow[3], mrow[4], norm_mix, winT, conv_full, alog, dtb)
    x2, mixed, cat, o, sall = _mix_core(x1, mrow[5], qn, kn, vv, z, pp, bg, gc, gnm, pwb, psc, wo)
    x3, f2, a2, b2 = _ffn_fwd(x2, mrow[6], mrow[7], mrow[8], norm_ffn2, gath_b, 0, "ffn2_fwd")
    lpart, dx3, dfin = _loss_head(x3, tgt, fin)
    loss = lax.psum(jnp.sum(lpart), ("x", "y", "c"))

    slots = lambda t: t.reshape(4, t.shape[0] // 4, D)
    dx2, da2, db2, s2, h2, df2, dsh3, dsc3, dgt3, dn3 = _ffn_dgrad(
        dx3, x2, f2, a2, b2, mrow[6], mrow[7], mrow[8], norm_ffn2, gath_b, 0, "ffn2_dgrad")
    gg2, gu2, gd2 = _ffn_wgrad(da2, db2, s2, h2, df2, "ffn2_wgrad")
    dqn, dkn, dvv, dz, dpp, dbg, dgt2, dwo, dpw, dps, dgnm, *landed2 = _mix_core_bwd(
        dx2, mrow[5], mixed, cat, o, sall, qn, kn, vv, z, pp, bg, gc, gnm, pwb, psc, wo,
        scatter=[slots(gg2), slots(gu2), slots(gd2)])
    dx1, dwin, dcw, dal, ddt, dsh2, dsc2, dn2 = _mix_proj_bwd(
        dx2, dqn, dkn, dvv, dz, dpp, dbg, xq, ba, x1, mrow[3], mrow[4], norm_mix, winT, conv_full, alog, dtb)
    dwin_nat = jnp.concatenate([dwin[:4 * GW], dwin[4 * GW + PW:4 * GW + PW + 2 * NH], dwin[4 * GW:4 * GW + PW]], axis=0)
    dwin_sl = jnp.pad(dwin_nat.reshape(4, WIN_SH, D), ((0, 0), (0, WIN_PAD - WIN_SH), (0, 0)))
    dx0, da1, db1, s1, h1, df1, dsh1, dsc1, dgt1, dn1 = _ffn_dgrad(
        dx1, xs, f1, a1, b1, mrow[0], mrow[1], mrow[2], norm_ffn1, gath_a, 0, "ffn1_dgrad")
    gg1, gu1, gd1, *landed_mix = _ffn_wgrad(da1, db1, s1, h1, df1, "ffn1_wgrad", scatter=[slots(dwo), dwin_sl])
    landed1 = _chip_exchange([slots(gg1), slots(gu1), slots(gd1)], "grad_scatter")

    landed = list(landed1) + list(landed2) + list(landed_mix)
    names = ("g1", "u1", "d1", "g2", "u2", "d2", "wo", "win")
    psum = [_sum_slots(t, "sum_" + nm) for t, nm in zip(landed, names)]
    qsum = _pair_exchange(psum, "grad_pair")

    red = lambda t: jnp.sum(t, axis=0, keepdims=True)
    small = jnp.concatenate(
        [red(dn1), red(dn2), red(dn3), red(dfin),
         red(dsh1), red(dsc1), red(dgt1), red(dsh2), red(dsc2), red(dgt2), red(dsh3), red(dsc3), red(dgt3),
         _pad_row(red(dps)), _pad_row(red(dgnm)), _pad_row(red(dal)), _pad_row(red(ddt)),
         jnp.sum(dcw, axis=1).reshape(6, D), jnp.zeros((1, D), F32)], axis=0)
    small_all, dpw_all = _all_exchange([small, dpw.reshape(NG * 128, 128)], "small_gather")

    def adamw_t(i, w, m, v, name, rows):
        res = _adamw([psum[i][:rows], qsum[i][:rows]], w[0].T, m[0].T, v[0].T, name)
        return [t.T for t in res]

    upd = {}
    upd["ffn1_gate"] = adamw_t(0, ffn1_gate, m_ffn1_gate, v_ffn1_gate, "adamw_g1", fsh)
    upd["ffn1_up"] = adamw_t(1, ffn1_up, m_ffn1_up, v_ffn1_up, "adamw_u1", fsh)
    upd["ffn1_down"] = _adamw([psum[2], qsum[2]], ffn1_down[0], m_ffn1_down[0], v_ffn1_down[0], "adamw_d1")
    upd["ffn2_gate"] = adamw_t(3, ffn2_gate, m_ffn2_gate, v_ffn2_gate, "adamw_g2", fsh)
    upd["ffn2_up"] = adamw_t(4, ffn2_up, m_ffn2_up, v_ffn2_up, "adamw_u2", fsh)
    upd["ffn2_down"] = _adamw([psum[5], qsum[5]], ffn2_down[0], m_ffn2_down[0], v_ffn2_down[0], "adamw_d2")
    upd["w_out"] = _adamw([psum[6], qsum[6]], w_out[0], m_w_out[0], v_w_out[0], "adamw_wo")
    upd["w_in"] = adamw_t(7, w_in, m_w_in, v_w_in, "adamw_win", WIN_SH)
    dmods = lax.dynamic_slice(small_all[:, 4:4 + NMOD, :].reshape(8, NMOD * D), (0, chip * ADA_SH), (8, ADA_SH))
    upd["w_ada"] = _adamw_ada(msgs, dmods, w_ada[0], m_w_ada[0], v_w_ada[0])

    def pack_small(nf1, nmx, nf2, fn, bada, psc_, gn_, al_, dt_):
        return jnp.concatenate(
            [nf1.reshape(1, D), nmx.reshape(1, D), nf2.reshape(1, D), fn.reshape(1, D), bada.reshape(NMOD, D),
             _pad_row(psc_), _pad_row(gn_), _pad_row(_lanes_4_7(al_)), _pad_row(_lanes_4_7(dt_)),
             jnp.zeros((7, D), F32)], axis=0)

    ws = pack_small(norm_ffn1, norm_mix, norm_ffn2, final_norm, b_ada, pool_scale, gdn_norm, a_log, dt_bias)
    ms = pack_small(m_norm_ffn1, m_norm_mix, m_norm_ffn2, m_final_norm, m_b_ada, m_pool_scale, m_gdn_norm, m_a_log, m_dt_bias)
    vs = pack_small(v_norm_ffn1, v_norm_mix, v_norm_ffn2, v_final_norm, v_b_ada, v_pool_scale, v_gdn_norm, v_a_log, v_dt_bias)
    sm = _adamw_small(small_all, ws, ms, vs, "adamw_small")
    pw2 = lambda t: t.reshape(NG * 128, 128)
    upd_pw = _adamw_small(dpw_all, pw2(pool_w), pw2(m_pool_w), pw2(v_pool_w), "adamw_pool_w")
    csh = 3 * GW // 4
    gconv = lax.dynamic_slice(sm[0][17:23].reshape(4, 3 * GW), (0, chip * csh), (4, csh))
    upd["conv_w"] = _adamw([gconv], conv_w[0], m_conv_w[0], v_conv_w[0], "adamw_conv")

    def small_out(k):
        t = sm[k]
        return {
            "norm_ffn1": t[0:1], "norm_mix": t[1:2], "norm_ffn2": t[2:3], "final_norm": t[3],
            "b_ada": t[4:4 + NMOD].reshape(1, NMOD * D), "pool_scale": t[13:14, :PW], "gdn_norm": t[14:15, :DH],
            "a_log": t[15:16, NH:2 * NH], "dt_bias": t[16:17, NH:2 * NH],
        }

    order = ["w_ada", "b_ada", "norm_ffn1", "ffn1_gate", "ffn1_up", "ffn1_down", "norm_mix", "w_in", "conv_w", "a_log",
             "dt_bias", "gdn_norm", "pool_w", "pool_scale", "w_out", "norm_ffn2", "ffn2_gate", "ffn2_up", "ffn2_down",
             "final_norm"]
    outs = [loss, dx0[None]]
    for k in range(4):
        smk = small_out(k)
        for nm in order:
            if nm in upd:
                outs.append(upd[nm][k][None])
            elif nm == "pool_w":
                outs.append(upd_pw[k].reshape(1, NG, 128, 128))
            else:
                outs.append(smk[nm])
    return tuple(outs)
```

```python
import functools

import jax
import jax.numpy as jnp
from jax import lax
from jax.experimental import pallas as pl
from jax.experimental.pallas import tpu as pltpu

F32 = jnp.float32
BF = jnp.bfloat16

D = 1024
FF = 2816
FH = FF // 2
NH = 4
DH = 128
GW = NH * DH
CH = 64
PW = 512
NG = 4
POOL_WINDOWS = (2, 4, 8, 16)
HALO = 16
DIN = 4 * GW + 2 * NH + PW
DINP = 3 * GW + GW + PW + 128
EPS = 1e-6
ADAM_LR, ADAM_B1, ADAM_B2, ADAM_EPS, ADAM_WD, ADAM_STEP = 0.001, 0.9, 0.999, 1e-08, 0.01, 10

VMEM_LIMIT = 56 * 1024 * 1024

NT_DIMS = (((1,), (1,)), ((), ()))
TN_DIMS = (((0,), (0,)), ((), ()))
HI = lax.Precision.HIGHEST


def _nt(a, b, **kw):
    return lax.dot_general(a, b, NT_DIMS, preferred_element_type=F32, **kw)


def _tn(a, b, **kw):
    return lax.dot_general(a, b, TN_DIMS, preferred_element_type=F32, **kw)


def _nn(a, b, **kw):
    return jnp.dot(a, b, preferred_element_type=F32, **kw)


def _cparams(sem=("arbitrary",), **kw):
    return pltpu.CompilerParams(dimension_semantics=sem, vmem_limit_bytes=VMEM_LIMIT, **kw)


def _const_spec(shape):
    nd = len(shape)
    return pl.BlockSpec(shape, lambda *_: (0,) * nd, pipeline_mode=pl.Buffered(1))


def _row_spec(tm, width):
    return pl.BlockSpec((tm, width), lambda i: (i, 0))


def _sum8(v):
    return jnp.sum(v.reshape(v.shape[0] // 8, 8, v.shape[1]), axis=0)


def _sigmoid(v):
    return 0.5 * jnp.tanh(0.5 * v) + 0.5


def _tile(T, cap=512):
    return min(cap, T)


def _norm_mod_fwd(xv, gain, shift, scale):
    r = lax.rsqrt(jnp.mean(xv * xv, axis=-1, keepdims=True) + EPS)
    n = xv * r
    y = n * gain
    return n, r, y, y * (1.0 + scale) + shift


def _norm_mod_bwd(dh, n, r, y, gain, scale):
    dy = dh * (1.0 + scale)
    dn = dy * gain
    dx = r * (dn - n * jnp.mean(dn * n, axis=-1, keepdims=True))
    return dx, _sum8(dh), _sum8(dh * y), _sum8(dy * n)


def _ffn_wspecs(k0):
    return [pl.BlockSpec((4, FF // 4, D), lambda i, k=k0 + n: (0, k, 0), pipeline_mode=pl.Buffered(1)) for n in range(3)]


def _half(w_ref, j):
    return w_ref[2 * j:2 * j + 2].reshape(FH, D)


def _ffn_fwd(x, shift, scale, gate, gain, wall, k0, name, gather_block=None):
    T = x.shape[0]
    tm = _tile(T)
    nt = T // tm
    comm = gather_block is not None

    def body(*refs):
        x_ref, sh_ref, sc_ref, gt_ref, gn_ref, wg_ref, wu_ref, wd_ref = refs[:8]
        if comm:
            xo_ref, f_ref, a_ref, b_ref = refs[9:13]
            plan = _SplitGather(refs[8], refs[13], *refs[14:])
            _comm_begin(pl.program_id(0), plan, nt - 1)
        else:
            xo_ref, f_ref, a_ref, b_ref = refs[8:12]
        xv = x_ref[...]
        _, _, _, h = _norm_mod_fwd(xv, gn_ref[...], sh_ref[...], sc_ref[...])
        hb = h.astype(BF)
        facc = jnp.zeros((tm, D), F32)
        for j in range(2):
            a = _nt(hb, _half(wg_ref, j))
            b = _nt(hb, _half(wu_ref, j))
            a_ref[:, j * FH:(j + 1) * FH] = a.astype(BF)
            b_ref[:, j * FH:(j + 1) * FH] = b.astype(BF)
            s = (a * _sigmoid(a) * b).astype(BF)
            facc = facc + _nn(s, _half(wd_ref, j))
        f_ref[...] = facc
        xo_ref[...] = xv + 0.5 * gt_ref[...] * facc
        if comm:
            _comm_end(pl.program_id(0), plan, nt - 1)

    vec = _const_spec((1, D))
    extra_in = [gather_block] if comm else []
    return pl.pallas_call(
        body, grid=(nt,), name=name,
        in_specs=[_row_spec(tm, D), vec, vec, vec, vec] + _ffn_wspecs(k0) + [ANY] * comm,
        out_specs=[_row_spec(tm, D), _row_spec(tm, D), _row_spec(tm, FF), _row_spec(tm, FF)] + [ANY] * comm,
        out_shape=[jax.ShapeDtypeStruct((T, D), F32), jax.ShapeDtypeStruct((T, D), F32),
                   jax.ShapeDtypeStruct((T, FF), BF), jax.ShapeDtypeStruct((T, FF), BF)]
        + ([jax.ShapeDtypeStruct((4,) + gather_block.shape, gather_block.dtype)] if comm else []),
        scratch_shapes=_gather_sems() if comm else [],
        compiler_params=_cparams(),
    )(x, shift, scale, gate, gain, wall, wall, wall, *extra_in)


def _ffn_dgrad(dxo, x, f, a, b, shift, scale, gate, gain, wall, k0, name, scatter=()):
    T = x.shape[0]
    tm = _tile(T, 256)
    nt = T // tm
    ns = len(scatter)

    def body(*refs):
        dxo_ref, x_ref, f_ref, a_ref, b_ref, sh_ref, sc_ref, gt_ref, gn_ref, wg_ref, wu_ref, wd_ref = refs[:12]
        dx_ref, da_ref, db_ref, s_ref, h_ref, df_ref, dsh_ref, dsc_ref, dgt_ref, dgn_ref = refs[12 + ns:22 + ns]
        i = pl.program_id(0)
        if ns:
            plan = _Scatter(refs[12:12 + ns], refs[22 + ns:22 + 2 * ns], *refs[22 + 2 * ns:])
            _comm_begin(i, plan, nt - 1)
        dxo_v = dxo_ref[...]
        xv = x_ref[...]
        gain_v, scale_v = gn_ref[...], sc_ref[...]
        n, r, y, h = _norm_mod_fwd(xv, gain_v, sh_ref[...], scale_v)
        h_ref[...] = h.astype(BF)
        dgate = _sum8(0.5 * f_ref[...] * dxo_v)
        dfb = (0.5 * gt_ref[...] * dxo_v).astype(BF)
        df_ref[...] = dfb
        dh = jnp.zeros((tm, D), F32)
        for j in range(2):
            cols = slice(j * FH, (j + 1) * FH)
            ds = _nt(dfb, _half(wd_ref, j))
            av = a_ref[:, cols].astype(F32)
            bv = b_ref[:, cols].astype(F32)
            sig = _sigmoid(av)
            sa = av * sig
            da = (ds * bv * (sig * (1.0 + av * (1.0 - sig)))).astype(BF)
            db = (ds * sa).astype(BF)
            da_ref[:, cols] = da
            db_ref[:, cols] = db
            s_ref[:, cols] = (sa * bv).astype(BF)
            dh = dh + _nn(da, _half(wg_ref, j)) + _nn(db, _half(wu_ref, j))
        dxn, dsh, dsc, dgn = _norm_mod_bwd(dh, n, r, y, gain_v, scale_v)
        dx_ref[...] = dxo_v + dxn

        @pl.when(i == 0)
        def _():
            dsh_ref[...] = dsh
            dsc_ref[...] = dsc
            dgt_ref[...] = dgate
            dgn_ref[...] = dgn

        @pl.when(i > 0)
        def _():
            dsh_ref[...] += dsh
            dsc_ref[...] += dsc
            dgt_ref[...] += dgate
            dgn_ref[...] += dgn

        if ns:
            _comm_end(i, plan, nt - 1)

    vec = _const_spec((1, D))
    acc = pl.BlockSpec((8, D), lambda i: (0, 0))
    accs = jax.ShapeDtypeStruct((8, D), F32)
    return pl.pallas_call(
        body, grid=(nt,), name=name,
        in_specs=[_row_spec(tm, D), _row_spec(tm, D), _row_spec(tm, D), _row_spec(tm, FF), _row_spec(tm, FF),
                  vec, vec, vec, vec] + _ffn_wspecs(k0) + [ANY] * ns,
        out_specs=[_row_spec(tm, D), _row_spec(tm, FF), _row_spec(tm, FF), _row_spec(tm, FF),
                   _row_spec(tm, D), _row_spec(tm, D), acc, acc, acc, acc] + [ANY] * ns,
        out_shape=[jax.ShapeDtypeStruct((T, D), F32), jax.ShapeDtypeStruct((T, FF), BF),
                   jax.ShapeDtypeStruct((T, FF), BF), jax.ShapeDtypeStruct((T, FF), BF),
                   jax.ShapeDtypeStruct((T, D), BF), jax.ShapeDtypeStruct((T, D), BF), accs, accs, accs, accs]
        + [jax.ShapeDtypeStruct(t.shape, t.dtype) for t in scatter],
        scratch_shapes=_scatter_sems(ns) if ns else [],
        compiler_params=_cparams(),
    )(dxo, x, f, a, b, shift, scale, gate, gain, wall, wall, wall, *scatter)


def _ffn_wgrad(da, db, s, h, df, name, half=None, scatter=(), allgather=(), scatter_lo=()):
    T = h.shape[0]
    tk = _tile(T, 512)
    nk = T // tk
    nj = 2 if half is None else 1
    groups = [g for g in (
        (list(scatter), _Scatter, _scatter_sems, lambda t: t.shape),
        (list(allgather), _AllGather, _allgather_sems, lambda t: (8,) + t.shape),
        (list(scatter_lo), functools.partial(_ScatterHalf, half=0), _scatter_sems, lambda t: (4,) + t.shape[1:]),
    ) if g[0]]
    extra = [t for g in groups for t in g[0]]
    ne = len(extra)

    def body(*refs):
        da_ref, db_ref, s_ref, h_ref, df_ref = refs[:5]
        og_ref, ou_ref, od_ref = refs[5 + ne:8 + ne]
        ag, au, ad = refs[8 + 2 * ne:11 + 2 * ne]
        k = pl.program_id(1)
        step = pl.program_id(0) * nk + k
        if ne:
            plans, at, sem_at = [], 0, 11 + 2 * ne
            for arrs, make, _, _ in groups:
                n = len(arrs)
                plans.append(make(refs[5 + at:5 + at + n], refs[8 + ne + at:8 + ne + at + n], *refs[sem_at:sem_at + 3]))
                at, sem_at = at + n, sem_at + 3
            plan = _Plans(plans)
            _comm_begin(step, plan, nj * nk - 1)

        @pl.when(k == 0)
        def _():
            for acc in (ag, au, ad):
                acc[...] = jnp.zeros((FH, D), F32)

        ag[...] += _tn(da_ref[...], h_ref[...])
        au[...] += _tn(db_ref[...], h_ref[...])
        ad[...] += _tn(s_ref[...], df_ref[...])

        @pl.when(k == nk - 1)
        def _():
            og_ref[...] = ag[...].astype(BF)
            ou_ref[...] = au[...].astype(BF)
            od_ref[...] = ad[...].astype(BF)

        if ne:
            _comm_end(step, plan, nj * nk - 1)

    j0 = 0 if half is None else half
    colblk = pl.BlockSpec((tk, FH), lambda j, k: (k, j + j0))
    rowblk = pl.BlockSpec((tk, D), lambda j, k: (k, 0))
    outblk = pl.BlockSpec((FH, D), lambda j, k: (j, 0))
    outs = jax.ShapeDtypeStruct((nj * FH, D), BF)
    return pl.pallas_call(
        body, grid=(nj, nk), name=name,
        in_specs=[colblk, colblk, colblk, rowblk, rowblk] + [ANY] * ne,
        out_specs=[outblk, outblk, outblk] + [ANY] * ne,
        out_shape=[outs, outs, outs] + [jax.ShapeDtypeStruct(shape(t), t.dtype) for g in groups for t, shape in
                                        ((t, g[3]) for t in g[0])],
        scratch_shapes=[pltpu.VMEM((FH, D), F32)] * 3 + [sm for g in groups for sm in g[2](len(g[0]))],
        compiler_params=_cparams(("arbitrary", "arbitrary")),
    )(da, db, s, h, df, *extra)


def _loss_head(x, target, gain):
    T = x.shape[0]
    tm = _tile(T)

    def body(x_ref, t_ref, gn_ref, ls_ref, dx_ref, dgn_ref):
        i = pl.program_id(0)
        xv = x_ref[...]
        gain_v = gn_ref[...]
        r = lax.rsqrt(jnp.mean(xv * xv, axis=-1, keepdims=True) + EPS)
        n = xv * r
        err = n * gain_v - t_ref[...]
        e2 = err * err
        part = e2[:, 0:128]
        for q in range(1, D // 128):
            part = part + e2[:, q * 128:(q + 1) * 128]
        lsum = _sum8(part) * (0.5 / D)
        dy = err * (1.0 / D)
        dn = dy * gain_v
        dx_ref[...] = r * (dn - n * jnp.mean(dn * n, axis=-1, keepdims=True))
        dgn = _sum8(dy * n)

        @pl.when(i == 0)
        def _():
            ls_ref[...] = lsum
            dgn_ref[...] = dgn

        @pl.when(i > 0)
        def _():
            ls_ref[...] += lsum
            dgn_ref[...] += dgn

    return pl.pallas_call(
        body, grid=(T // tm,), name="loss_head",
        in_specs=[_row_spec(tm, D), _row_spec(tm, D), _const_spec((1, D))],
        out_specs=[pl.BlockSpec((8, 128), lambda i: (0, 0)), _row_spec(tm, D), pl.BlockSpec((8, D), lambda i: (0, 0))],
        out_shape=[jax.ShapeDtypeStruct((8, 128), F32), jax.ShapeDtypeStruct((T, D), F32),
                   jax.ShapeDtypeStruct((8, D), F32)],
        compiler_params=_cparams(),
    )(x, target, gain)


def _seg_cumsum(v, row_in_chunk, reverse=False):
    n = v.shape[0]
    s = 1
    while s < CH:
        if reverse:
            moved = pltpu.roll(v, n - s, 0)
            ok = row_in_chunk < CH - s
        else:
            moved = pltpu.roll(v, s, 0)
            ok = row_in_chunk >= s
        v = v + jnp.where(ok, moved, 0.0)
        s *= 2
    return v


def _conv_silu(xq_ext_ref, cw_ref, tm):
    c = cw_ref[0:1, :] * xq_ext_ref[pl.ds(5, tm), :]
    for j in range(1, 4):
        c = c + cw_ref[j:j + 1, :] * xq_ext_ref[pl.ds(5 + j, tm), :]
    return c, _sigmoid(c)


def _gates(ba, alog, dtb, lane):
    beta = _sigmoid(ba)
    arg = ba + dtb
    softplus = jnp.maximum(arg, 0.0) + jnp.log(1.0 + jnp.exp(-jnp.abs(arg)))
    g = -jnp.exp(alog) * softplus
    return jnp.where(lane < NH, beta, 0.0), jnp.where((lane >= NH) & (lane < 2 * NH), g, 0.0), _sigmoid(arg)


def _mix_proj(x, shift, scale, gain, winT, conv_w, alog, dtb):
    T = x.shape[0]
    tm = _tile(T)

    def body(x_ref, sh_ref, sc_ref, gn_ref, w_ref, cw_ref, al_ref, dt_ref,
             xq_ref, ba_ref, qn_ref, kn_ref, v_ref, z_ref, p_ref, bg_ref, gc_ref, ext):
        i = pl.program_id(0)
        _, _, _, h = _norm_mod_fwd(x_ref[...], gn_ref[...], sh_ref[...], sc_ref[...])
        hb = h.astype(BF)

        @pl.when(i == 0)
        def _():
            ext[pl.ds(0, 8), :] = jnp.zeros((8, 3 * GW), F32)

        xq = _nt(hb, w_ref[pl.ds(0, 3 * GW), :])
        xq_ref[...] = xq
        ext[pl.ds(8, tm), :] = xq
        z_ref[...] = _nt(hb, w_ref[pl.ds(3 * GW, GW), :])
        p_ref[...] = _nt(hb, w_ref[pl.ds(4 * GW, PW), :])
        ba = _nt(hb, w_ref[pl.ds(4 * GW + PW, 128), :])
        ba_ref[...] = ba

        c, sg = _conv_silu(ext, cw_ref, tm)
        ext[pl.ds(0, 8), :] = ext[pl.ds(tm, 8), :]
        qt = c * sg
        for hd in range(NH):
            cq = slice(hd * DH, (hd + 1) * DH)
            ck = slice(GW + hd * DH, GW + (hd + 1) * DH)
            qh, kh = qt[:, cq], qt[:, ck]
            qn_ref[:, cq] = qh * (lax.rsqrt(jnp.sum(qh * qh, axis=-1, keepdims=True) + EPS) * DH ** -0.5)
            kn_ref[:, cq] = kh * lax.rsqrt(jnp.sum(kh * kh, axis=-1, keepdims=True) + EPS)
        v_ref[...] = qt[:, 2 * GW:3 * GW]

        lane = lax.broadcasted_iota(jnp.int32, (tm, 128), 1)
        row = lax.broadcasted_iota(jnp.int32, (tm, 128), 0) % CH
        beta, g, _ = _gates(ba, al_ref[...], dt_ref[...], lane)
        bg_ref[...] = beta + g
        gc_ref[...] = _seg_cumsum(g, row)

    wide = lambda w: _row_spec(tm, w)
    shp = lambda w: jax.ShapeDtypeStruct((T, w), F32)
    return pl.pallas_call(
        body, grid=(T // tm,), name="mix_proj",
        in_specs=[wide(D), _const_spec((1, D)), _const_spec((1, D)), _const_spec((1, D)), _const_spec((DINP, D)),
                  _const_spec((4, 3 * GW)), _const_spec((1, 128)), _const_spec((1, 128))],
        out_specs=[wide(3 * GW), wide(128), wide(GW), wide(GW), wide(GW), wide(GW), wide(PW), wide(128), wide(128)],
        out_shape=[shp(3 * GW), shp(128), shp(GW), shp(GW), shp(GW), shp(GW), shp(PW), shp(128), shp(128)],
        scratch_shapes=[pltpu.VMEM((tm + 8, 3 * GW), F32)],
        compiler_params=_cparams(),
    )(x, shift, scale, gain, winT, conv_w, alog, dtb)


R2 = 2 * CH
TRI_PREC = None


def _tmm(fn, a, b):
    if TRI_PREC is None:
        return fn(a.astype(BF), b.astype(BF))
    return fn(a, b, precision=TRI_PREC)


def _pair_consts():
    ii = lax.broadcasted_iota(jnp.int32, (R2, R2), 0)
    jj = lax.broadcasted_iota(jnp.int32, (R2, R2), 1)
    same = (ii < CH) == (jj < CH)
    r = lax.broadcasted_iota(jnp.int32, (R2, 1), 0)
    return dict(causal=same & (ii >= jj), strict=same & (ii > jj), eye=(ii == jj).astype(F32), rowA=r < CH,
                last=(r == CH - 1) | (r == R2 - 1),
                rowS=lax.broadcasted_iota(jnp.int32, (2 * DH, 1), 0) < DH)


def _tri_inverse_many(ms, eye):
    pws = [-m for m in ms]
    ts = [eye + p for p in pws]
    for _ in range(5):
        pws = [_tmm(_nn, p, p) for p in pws]
        ts = [_tmm(_nn, t, eye + p) for t, p in zip(ts, pws)]
    return ts


def _egl_rows(gl):
    egl = jnp.exp(gl)
    return egl, jnp.concatenate([jnp.broadcast_to(egl[0:1], (DH, 1)), jnp.broadcast_to(egl[CH:CH + 1], (DH, 1))], axis=0)


def _pair_intra(items, cn):
    causal, rowA = cn["causal"], cn["rowA"]

    def bd(t):
        return jnp.concatenate([jnp.where(rowA, t, 0.0), jnp.where(rowA, 0.0, t)], axis=1).astype(BF)

    outs = []
    for q, k, v, beta, gcv, gl in items:
        gc_b = jnp.broadcast_to(gcv, (R2, R2))
        gam = jnp.where(causal, jnp.exp(jnp.where(causal, gc_b - gc_b.T, 0.0)), 0.0)
        kb = k * beta
        kbf = k.astype(BF)
        P = _nt(kb.astype(BF), kbf)
        QK = _nt(q.astype(BF), kbf)
        E = jnp.exp(gcv)
        outs.append(dict(gam=gam, kb=kb, vb=v * beta, P=P, QK=QK, E=E, Fd=jnp.exp(gl - gcv), kbE=kb * E,
                         Q=QK * gam, qE_bd=bd(q * E)))
    tms = _tri_inverse_many([jnp.where(cn["strict"], d["P"] * d["gam"], 0.0) for d in outs], cn["eye"])
    for d, tm_, (q, k, v, beta, gcv, gl) in zip(outs, tms, items):
        d["Tm"] = tm_
        d["u"] = _tmm(_nn, tm_, d["vb"])
        d["w_bd"] = bd(_tmm(_nn, tm_, d["kbE"]))
        d["kF_bd"] = bd(k * d["Fd"])
    return outs


def _pair_scan(it, S, egl_st):
    Sb = S.astype(BF)
    vn = it["u"] - _nn(it["w_bd"], Sb)
    vnb = vn.astype(BF)
    o = _nn(it["qE_bd"], Sb) + _nn(it["Q"].astype(BF), vnb)
    return vnb, o, S * egl_st + _tn(it["kF_bd"], vnb)


def _pair_forward(q, k, v, beta, gcv, gl, S, cn):
    fw = _pair_intra([(q, k, v, beta, gcv, gl)], cn)[0]
    fw["egl"], fw["egl_st"] = _egl_rows(gl)
    fw["vnb"], fw["o"], fw["S_new"] = _pair_scan(fw, S, fw["egl_st"])
    return fw


def _stack_heads(ref, rows, pair):
    return jnp.concatenate([ref[rows, (2 * pair) * DH:(2 * pair + 1) * DH],
                            ref[rows, (2 * pair + 1) * DH:(2 * pair + 2) * DH]], axis=0)


def _stack_cols(val, lane_a, lane_b, bcast_rows=None):
    a, b = val[:, lane_a:lane_a + 1], val[:, lane_b:lane_b + 1]
    if bcast_rows:
        a, b = jnp.broadcast_to(a, (bcast_rows, 1)), jnp.broadcast_to(b, (bcast_rows, 1))
    return jnp.concatenate([a, b], axis=0)


def _pool_windows(ext, tm, reverse):
    n = tm + HALO
    outs = []
    for gi in range(NG):
        a = ext[:, gi * 128:(gi + 1) * 128]
        s = 1
        while s < POOL_WINDOWS[gi]:
            a = a + pltpu.roll(a, (n - s) if reverse else s, 0)
            s *= 2
        outs.append(a[0:tm] if reverse else a[HALO:HALO + tm])
    return jnp.concatenate(outs, axis=1)


def _pool_count(tm, tile_index):
    t1 = (lax.broadcasted_iota(jnp.int32, (tm, PW), 0) + tile_index * tm + 1).astype(F32)
    win = jnp.concatenate([jnp.full((tm, 128), float(w), F32) for w in POOL_WINDOWS], axis=1)
    return 1.0 / jnp.minimum(t1, win)


def _chunk_item(qn_ref, kn_ref, v_ref, bg_ref, gc_ref, c, pr):
    r0 = pl.multiple_of(c * CH, CH)
    rows = pl.ds(r0, CH)
    bgv = bg_ref[rows, :]
    gcv_all = gc_ref[rows, :]
    gl_all = gc_ref[pl.ds(r0 + CH - 1, 1), :]
    ha, hb = 2 * pr, 2 * pr + 1
    return (_stack_heads(qn_ref, rows, pr), _stack_heads(kn_ref, rows, pr), _stack_heads(v_ref, rows, pr),
            _stack_cols(bgv, ha, hb), _stack_cols(gcv_all, NH + ha, NH + hb), _stack_cols(gl_all, NH + ha, NH + hb, CH))


CHUNK_GROUP = 4


def _mix_core(x, gate, qn, kn, v, z, p, bg, gc, gnorm, pool_w, pool_scale, w_out):
    T = x.shape[0]
    tm = _tile(T)
    nc = tm // CH
    cg = CHUNK_GROUP if nc % CHUNK_GROUP == 0 else 1
    npb = nc * (NH // 2)

    def body(x_ref, gt_ref, qn_ref, kn_ref, v_ref, z_ref, p_ref, bg_ref, gc_ref, gnm_ref, pw_ref, ps_ref, wo_ref,
             xo_ref, mx_ref, cat_ref, o_ref, sall_ref, S_scr, pext, u_s, w_s, qe_s, kf_s, q_s):
        i = pl.program_id(0)

        @pl.when(i == 0)
        def _():
            S_scr[...] = jnp.zeros((NH * DH, DH), F32)
            pext[pl.ds(0, HALO), :] = jnp.zeros((HALO, PW), F32)

        cn = _pair_consts()

        def intra(g, carry):
            idx = [(g * cg + dc, pr) for dc in range(cg) for pr in range(NH // 2)]
            res = _pair_intra([_chunk_item(qn_ref, kn_ref, v_ref, bg_ref, gc_ref, c, pr) for c, pr in idx], cn)
            for (c, pr), d in zip(idx, res):
                pi = c * (NH // 2) + pr
                u_s[pi] = d["u"]
                w_s[pi] = d["w_bd"]
                qe_s[pi] = d["qE_bd"]
                kf_s[pi] = d["kF_bd"]
                q_s[pi] = d["Q"].astype(BF)
            return carry

        lax.fori_loop(0, nc // cg, intra, 0)

        def scan(c, carry):
            r0 = pl.multiple_of(c * CH, CH)
            rows = pl.ds(r0, CH)
            gl_all = gc_ref[pl.ds(r0 + CH - 1, 1), :]
            for pr in range(NH // 2):
                ha, hb = 2 * pr, 2 * pr + 1
                pi = c * (NH // 2) + pr
                S = S_scr[pl.ds(pr * 2 * DH, 2 * DH), :]
                sall_ref[c, ha:hb + 1] = S.reshape(2, DH, DH)
                _, egl_st = _egl_rows(_stack_cols(gl_all, NH + ha, NH + hb, CH))
                it = dict(u=u_s[pi], w_bd=w_s[pi], qE_bd=qe_s[pi], kF_bd=kf_s[pi], Q=q_s[pi])
                _, o, S_new = _pair_scan(it, S, egl_st)
                o_ref[rows, ha * DH:(ha + 1) * DH] = o[0:CH]
                o_ref[rows, hb * DH:(hb + 1) * DH] = o[CH:R2]
                S_scr[pl.ds(pr * 2 * DH, 2 * DH), :] = S_new
            return carry

        lax.fori_loop(0, nc, scan, 0)

        for hd in range(NH):
            cols = slice(hd * DH, (hd + 1) * DH)
            oh = o_ref[:, cols]
            zh = z_ref[:, cols]
            r = lax.rsqrt(jnp.mean(oh * oh, axis=-1, keepdims=True) + EPS)
            cat_ref[:, cols] = (oh * r * gnm_ref[...] * (zh * _sigmoid(zh))).astype(BF)

        pv = p_ref[...]
        pext[pl.ds(HALO, tm), :] = pv
        pooled = _pool_windows(pext[...], tm, False) * _pool_count(tm, i) - pv
        pext[pl.ds(0, HALO), :] = pext[pl.ds(tm, HALO), :]
        for gi in range(NG):
            cols = slice(gi * 128, (gi + 1) * 128)
            pm = _nn(pooled[:, cols].astype(BF), pw_ref[gi])
            cat_ref[:, GW + gi * 128:GW + (gi + 1) * 128] = (pm * ps_ref[:, cols]).astype(BF)

        mixed = _nn(cat_ref[...], wo_ref[...])
        mx_ref[...] = mixed
        xo_ref[...] = x_ref[...] + gt_ref[...] * mixed

    wide = lambda w: _row_spec(tm, w)
    return pl.pallas_call(
        body, grid=(T // tm,), name="mix_core",
        in_specs=[wide(D), _const_spec((1, D)), wide(GW), wide(GW), wide(GW), wide(GW), wide(PW), wide(128), wide(128),
                  _const_spec((1, DH)), _const_spec((NG, 128, 128)), _const_spec((1, PW)), _const_spec((D, D))],
        out_specs=[wide(D), wide(D), wide(D), wide(GW), pl.BlockSpec((nc, NH, DH, DH), lambda i: (i, 0, 0, 0))],
        out_shape=[jax.ShapeDtypeStruct((T, D), F32), jax.ShapeDtypeStruct((T, D), F32),
                   jax.ShapeDtypeStruct((T, D), BF), jax.ShapeDtypeStruct((T, GW), F32),
                   jax.ShapeDtypeStruct((T // CH, NH, DH, DH), F32)],
        scratch_shapes=[pltpu.VMEM((NH * DH, DH), F32), pltpu.VMEM((tm + HALO, PW), F32),
                        pltpu.VMEM((npb, R2, DH), F32), pltpu.VMEM((npb, R2, 2 * DH), BF),
                        pltpu.VMEM((npb, R2, 2 * DH), BF), pltpu.VMEM((npb, R2, 2 * DH), BF),
                        pltpu.VMEM((npb, R2, R2), BF)],
        compiler_params=_cparams(),
    )(x, gate, qn, kn, v, z, p, bg, gc, gnorm, pool_w, pool_scale, w_out)


def _pair_scan_bwd(it, S, dSn, do, egl, egl_st, cn):
    bf = lambda t: t.astype(BF)
    rowA, rowS = cn["rowA"], cn["rowS"]
    sel = lambda t: jnp.where(rowA, t[:, 0:DH], t[:, DH:2 * DH])
    Sb, dSb, dob = bf(S), bf(dSn), bf(do)
    vnb = bf(it["u"] - _nn(it["w_bd"], Sb))
    dvn = _tn(bf(it["Q"]), dob) + _nn(it["kF_bd"], dSb)
    dQ = _nt(dob, vnb)
    dqE = sel(_nt(dob, Sb))
    dkF = sel(_nt(vnb, dSb))
    dvnb = bf(dvn)
    dw = -sel(_nt(dvnb, Sb))
    dS_new = _tn(it["qE_bd"], dob) + egl_st * dSn - _tn(it["w_bd"], dvnb)
    prod = jnp.sum(dSn * S, axis=1, keepdims=True)
    d_egl_a = jnp.sum(jnp.where(rowS, prod, 0.0), axis=0, keepdims=True)
    d_egl_b = jnp.sum(jnp.where(rowS, 0.0, prod), axis=0, keepdims=True)
    return dict(dvn=dvn, dQ=dQ, dqE=dqE, dkF=dkF, dw=dw, degl=jnp.where(rowA, d_egl_a, d_egl_b) * egl), dS_new


def _pair_intra_bwd(items, cn):
    bf = lambda t: t.astype(BF)
    rowA = cn["rowA"]
    for d in items:
        d["kb"] = d["k"] * d["beta"]
        d["E"] = jnp.exp(d["gcv"])
        d["Fd"] = jnp.exp(d["gl"] - d["gcv"])
        d["TmT"] = d["Tm"].T
        d["dvb"] = _tmm(_nn, d["TmT"], d["dvn"])
        d["dkbE"] = _tmm(_nn, d["TmT"], d["dw"])
        dTm = _tmm(_nt, d["dvn"], d["v"] * d["beta"]) + _tmm(_nt, d["dw"], d["kb"] * d["E"])
        d["X"] = _tmm(_nt, dTm, d["Tm"])
    for d in items:
        d["dA"] = -_tmm(_nn, d["TmT"], d["X"])
    outs = []
    for d in items:
        q, k, v, beta, gam = d["q"], d["k"], d["v"], d["beta"], d["gam"]
        N = jnp.where(cn["strict"], d["dA"] * gam, 0.0)
        Rm = d["dQ"] * gam
        Wm = Rm * d["QK"] + N * d["P"]
        dgc = jnp.sum(Wm, axis=1, keepdims=True) - jnp.sum(Wm.T, axis=1, keepdims=True)
        kbf, Nb, Rb = bf(k), bf(N), bf(Rm)
        E, Fd = d["E"], d["Fd"]
        dq = d["dqE"] * E + _nn(Rb, kbf)
        dkb = d["dkbE"] * E + _nn(Nb, kbf)
        dk = d["dkF"] * Fd + _tn(Rb, bf(q)) + _tn(Nb, bf(d["kb"])) + beta * dkb
        dbeta = jnp.sum(dkb * k + d["dvb"] * v, axis=1, keepdims=True)
        dE = jnp.sum(d["dqE"] * q + d["dkbE"] * d["kb"], axis=1, keepdims=True)
        fdf = jnp.sum(d["dkF"] * k, axis=1, keepdims=True) * Fd
        dgl = d["degl"] + jnp.where(rowA, jnp.sum(jnp.where(rowA, fdf, 0.0), axis=0, keepdims=True),
                                    jnp.sum(jnp.where(rowA, 0.0, fdf), axis=0, keepdims=True))
        dgc = dgc + dE * E - fdf + jnp.where(cn["last"], dgl, 0.0)
        outs.append((dq, dk, beta * d["dvb"], dbeta, dgc))
    return outs


def _mix_core_bwd(dxo, gate, mixed, cat, o, sall, qn, kn, v, z, p, bg, gc, gnorm, pool_w, pool_scale, w_out,
                  scatter=()):
    T = dxo.shape[0]
    tm = _tile(T, 256)
    nt = T // tm
    nc = tm // CH
    ns = len(scatter)
    cg = CHUNK_GROUP if nc % CHUNK_GROUP == 0 else 1
    npb = nc * (NH // 2)

    def body(*refs):
        (dx_ref, gt_ref, mx_ref, cat_ref, o_ref, sall_ref, qn_ref, kn_ref, v_ref, z_ref, p_ref, ph_ref, bg_ref,
         gc_ref, gnm_ref, pw_ref, ps_ref, wo_ref) = refs[:18]
        (dq_ref, dk_ref, dv_ref, dz_ref, dp_ref, dbg_ref, dgt_ref, dwo_ref, dpw_ref, dps_ref,
         dgn_ref) = refs[18 + ns:29 + ns]
        dS_scr, pext, yext, do_buf, dwo_acc = refs[29 + 2 * ns:34 + 2 * ns]
        (gam_s, p_s, qk_s, tm_s, dqq_s, u_s, dvn_s, dqe_s, dkf_s, dw_s, w_s, qe_s, kf_s, q_s,
         degl_s) = refs[34 + 2 * ns:49 + 2 * ns]
        i = pl.program_id(0)
        ti = nt - 1 - i
        if ns:
            plan = _Scatter(refs[18:18 + ns], refs[29 + ns:29 + 2 * ns], *refs[49 + 2 * ns:])
            _comm_begin(i, plan, nt - 1)

        @pl.when(i == 0)
        def _():
            dS_scr[...] = jnp.zeros((NH * DH, DH), F32)
            yext[pl.ds(tm, HALO), :] = jnp.zeros((HALO, PW), F32)
            dwo_acc[...] = jnp.zeros((D, D), F32)
            dgt_ref[...] = jnp.zeros((8, D), F32)
            dpw_ref[...] = jnp.zeros((NG, 128, 128), F32)
            dps_ref[...] = jnp.zeros((8, PW), F32)
            dgn_ref[...] = jnp.zeros((8, DH), F32)

        dx2 = dx_ref[...]
        dgt_ref[...] += _sum8(mx_ref[...] * dx2)
        dmix = (gt_ref[...] * dx2).astype(BF)
        dcat = _nt(dmix, wo_ref[...])
        dwo_acc[...] += _tn(cat_ref[...], dmix)

        pv = p_ref[...]
        pext[pl.ds(0, HALO), :] = jnp.where(ti == 0, 0.0, ph_ref[...])
        pext[pl.ds(HALO, tm), :] = pv
        inv_cnt = _pool_count(tm, ti)
        pooled = _pool_windows(pext[...], tm, False) * inv_cnt - pv
        dpooled = []
        for gi in range(NG):
            cols = slice(gi * 128, (gi + 1) * 128)
            pgb = pooled[:, cols].astype(BF)
            pm = _nn(pgb, pw_ref[gi])
            dpo = dcat[:, GW + gi * 128:GW + (gi + 1) * 128]
            dps_ref[:, cols] += _sum8(dpo * pm)
            dpm = (dpo * ps_ref[:, cols]).astype(BF)
            dpooled.append(_nt(dpm, pw_ref[gi]))
            dpw_ref[gi] += _tn(pgb, dpm)
        dpooled = jnp.concatenate(dpooled, axis=1)
        y = dpooled * inv_cnt
        yext[pl.ds(0, tm), :] = y
        dp_ref[...] = _pool_windows(yext[...], tm, True) - dpooled
        yext[pl.ds(tm, HALO), :] = y[0:HALO]

        gnm = gnm_ref[...]
        dgn = jnp.zeros((8, DH), F32)
        for hd in range(NH):
            cols = slice(hd * DH, (hd + 1) * DH)
            oh = o_ref[:, cols]
            zh = z_ref[:, cols]
            r = lax.rsqrt(jnp.mean(oh * oh, axis=-1, keepdims=True) + EPS)
            n = oh * r
            sg = _sigmoid(zh)
            zs = zh * sg
            dgo = dcat[:, cols]
            dgn = dgn + _sum8(dgo * zs * n)
            dn = dgo * zs * gnm
            do_buf[:, cols] = r * (dn - n * jnp.mean(dn * n, axis=-1, keepdims=True))
            dz_ref[:, cols] = dgo * n * gnm * (sg * (1.0 + zh * (1.0 - sg)))
        dgn_ref[...] += dgn

        cn = _pair_consts()
        lane_c = lax.broadcasted_iota(jnp.int32, (CH, 128), 1)

        def intra(g, carry):
            idx = [(g * cg + dc, pr) for dc in range(cg) for pr in range(NH // 2)]
            res = _pair_intra([_chunk_item(qn_ref, kn_ref, v_ref, bg_ref, gc_ref, c, pr) for c, pr in idx], cn)
            for (c, pr), d in zip(idx, res):
                pi = c * (NH // 2) + pr
                gam_s[pi], p_s[pi], qk_s[pi], tm_s[pi], u_s[pi] = d["gam"], d["P"], d["QK"], d["Tm"], d["u"]
                w_s[pi], qe_s[pi], kf_s[pi], q_s[pi] = d["w_bd"], d["qE_bd"], d["kF_bd"], d["Q"].astype(BF)
            return carry

        lax.fori_loop(0, nc // cg, intra, 0)

        def scan(cc, carry):
            c = nc - 1 - cc
            r0 = pl.multiple_of(c * CH, CH)
            rows = pl.ds(r0, CH)
            gl_all = gc_ref[pl.ds(r0 + CH - 1, 1), :]
            for pr in range(NH // 2):
                ha, hb = 2 * pr, 2 * pr + 1
                pi = c * (NH // 2) + pr
                srows = pl.ds(pr * 2 * DH, 2 * DH)
                S = sall_ref[c, ha:hb + 1].reshape(2 * DH, DH)
                egl, egl_st = _egl_rows(_stack_cols(gl_all, NH + ha, NH + hb, CH))
                it = dict(u=u_s[pi], w_bd=w_s[pi], qE_bd=qe_s[pi], kF_bd=kf_s[pi], Q=q_s[pi])
                g, dS_new = _pair_scan_bwd(it, S, dS_scr[srows, :], _stack_heads(do_buf, rows, pr), egl, egl_st, cn)
                dvn_s[pi], dqq_s[pi], dqe_s[pi], dkf_s[pi], dw_s[pi] = g["dvn"], g["dQ"], g["dqE"], g["dkF"], g["dw"]
                degl_s[pi] = g["degl"]
                dS_scr[srows, :] = dS_new
            return carry

        lax.fori_loop(0, nc, scan, 0)

        def intra_bwd(g, carry):
            idx = [(g * cg + dc, pr) for dc in range(cg) for pr in range(NH // 2)]
            items = []
            for c, pr in idx:
                pi = c * (NH // 2) + pr
                q, k, vv, beta, gcv, gl = _chunk_item(qn_ref, kn_ref, v_ref, bg_ref, gc_ref, c, pr)
                items.append(dict(q=q, k=k, v=vv, beta=beta, gcv=gcv, gl=gl, gam=gam_s[pi], P=p_s[pi], QK=qk_s[pi],
                                  Tm=tm_s[pi], dvn=dvn_s[pi], dQ=dqq_s[pi], dqE=dqe_s[pi], dkF=dkf_s[pi], dw=dw_s[pi],
                                  degl=degl_s[pi]))
            res = _pair_intra_bwd(items, cn)
            for dc in range(cg):
                c = g * cg + dc
                rows = pl.ds(pl.multiple_of(c * CH, CH), CH)
                dbg = jnp.zeros((CH, 128), F32)
                for pr in range(NH // 2):
                    dq, dk, dv, dbeta, dgc = res[dc * (NH // 2) + pr]
                    for hd, half in ((2 * pr, slice(0, CH)), (2 * pr + 1, slice(CH, R2))):
                        cols = slice(hd * DH, (hd + 1) * DH)
                        dq_ref[rows, cols] = dq[half]
                        dk_ref[rows, cols] = dk[half]
                        dv_ref[rows, cols] = dv[half]
                        dbg = dbg + jnp.where(lane_c == hd, dbeta[half], 0.0) + jnp.where(lane_c == NH + hd, dgc[half], 0.0)
                dbg_ref[rows, :] = dbg
            return carry

        lax.fori_loop(0, nc // cg, intra_bwd, 0)

        lane = lax.broadcasted_iota(jnp.int32, (tm, 128), 1)
        row = lax.broadcasted_iota(jnp.int32, (tm, 128), 0) % CH
        dbg_all = dbg_ref[...]
        dg = _seg_cumsum(jnp.where(lane >= NH, dbg_all, 0.0), row, reverse=True)
        dbg_ref[...] = jnp.where(lane < NH, dbg_all, dg)

        @pl.when(i == nt - 1)
        def _():
            dwo_ref[...] = dwo_acc[...].astype(BF)

        if ns:
            _comm_end(i, plan, nt - 1)

    rev = lambda w: pl.BlockSpec((tm, w), lambda i: (nt - 1 - i, 0))
    halo = pl.BlockSpec((HALO, PW), lambda i: (jnp.maximum((nt - 1 - i) * (tm // HALO) - 1, 0), 0))
    shp = lambda w: jax.ShapeDtypeStruct((T, w), F32)
    fix = lambda *s: pl.BlockSpec(s, lambda i: (0,) * len(s))
    return pl.pallas_call(
        body, grid=(nt,), name="mix_core_bwd",
        in_specs=[rev(D), _const_spec((1, D)), rev(D), rev(D), rev(GW),
                  pl.BlockSpec((nc, NH, DH, DH), lambda i: (nt - 1 - i, 0, 0, 0)),
                  rev(GW), rev(GW), rev(GW), rev(GW), rev(PW), halo, rev(128), rev(128),
                  _const_spec((1, DH)), _const_spec((NG, 128, 128)), _const_spec((1, PW)), _const_spec((D, D))]
        + [ANY] * ns,
        out_specs=[rev(GW), rev(GW), rev(GW), rev(GW), rev(PW), rev(128),
                   fix(8, D), fix(D, D), fix(NG, 128, 128), fix(8, PW), fix(8, DH)] + [ANY] * ns,
        out_shape=[shp(GW), shp(GW), shp(GW), shp(GW), shp(PW), shp(128),
                   jax.ShapeDtypeStruct((8, D), F32), jax.ShapeDtypeStruct((D, D), BF),
                   jax.ShapeDtypeStruct((NG, 128, 128), F32), jax.ShapeDtypeStruct((8, PW), F32),
                   jax.ShapeDtypeStruct((8, DH), F32)] + [jax.ShapeDtypeStruct(t.shape, t.dtype) for t in scatter],
        scratch_shapes=[pltpu.VMEM((NH * DH, DH), F32), pltpu.VMEM((tm + HALO, PW), F32),
                        pltpu.VMEM((tm + HALO, PW), F32), pltpu.VMEM((tm, GW), F32), pltpu.VMEM((D, D), F32)]
        + [pltpu.VMEM((npb, R2, R2), F32)] * 5 + [pltpu.VMEM((npb, R2, DH), F32)] * 5
        + [pltpu.VMEM((npb, R2, 2 * DH), BF)] * 3 + [pltpu.VMEM((npb, R2, R2), BF), pltpu.VMEM((npb, R2, 1), F32)]
        + (_scatter_sems(ns) if ns else []),
        compiler_params=_cparams(),
    )(dxo, gate, mixed, cat, o, sall, qn, kn, v, z, p, p, bg, gc, gnorm, pool_w, pool_scale, w_out, *scatter)


def _mix_proj_bwd(dxo, dqn, dkn, dv, dz, dp, dbg, xq, ba, x, shift, scale, gain, winT, conv_w, alog, dtb):
    T = x.shape[0]
    tm = min(256, T)
    nt = T // tm
    W3 = 3 * GW

    def body(dxo_ref, dqn_ref, dkn_ref, dv_ref, dz_ref, dp_ref, dbg_ref, xq_ref, xh_ref, ba_ref, x_ref, sh_ref, sc_ref,
             gn_ref, w_ref, cw_ref, al_ref, dt_ref,
             dx_ref, dw_ref, dcw_ref, dal_ref, ddt_ref, dsh_ref, dsc_ref, dgn_ref,
             ext, dcext, dproj, dw_acc):
        i = pl.program_id(0)
        ti = nt - 1 - i

        @pl.when(i == 0)
        def _():
            dcext[pl.ds(tm, 8), :] = jnp.zeros((8, W3), F32)
            dw_acc[...] = jnp.zeros((DINP, D), F32)
            dcw_ref[...] = jnp.zeros((4, 8, W3), F32)
            dal_ref[...] = jnp.zeros((8, 128), F32)
            ddt_ref[...] = jnp.zeros((8, 128), F32)
            dsh_ref[...] = jnp.zeros((8, D), F32)
            dsc_ref[...] = jnp.zeros((8, D), F32)
            dgn_ref[...] = jnp.zeros((8, D), F32)

        ext[pl.ds(0, 8), :] = jnp.where(ti == 0, 0.0, xh_ref[...])
        ext[pl.ds(8, tm), :] = xq_ref[...]
        c, sg = _conv_silu(ext, cw_ref, tm)
        qt = c * sg
        dsilu = sg * (1.0 + c * (1.0 - sg))
        for hd in range(NH):
            for part, dref, mult in ((0, dqn_ref, DH ** -0.5), (1, dkn_ref, 1.0)):
                cols = slice(part * GW + hd * DH, part * GW + (hd + 1) * DH)
                xh = qt[:, cols]
                rr = lax.rsqrt(jnp.sum(xh * xh, axis=-1, keepdims=True) + EPS)
                unit = xh * rr
                du = dref[:, hd * DH:(hd + 1) * DH] * mult
                dxh = rr * (du - unit * jnp.sum(du * unit, axis=-1, keepdims=True))
                dcext[pl.ds(0, tm), cols] = dxh * dsilu[:, cols]
        dcext[pl.ds(0, tm), 2 * GW:W3] = dv_ref[...] * dsilu[:, 2 * GW:W3]
        dc = dcext[pl.ds(0, tm), :]
        dxq = jnp.zeros((tm, W3), F32)
        for j in range(4):
            dcw_ref[j] += _sum8(dc * ext[pl.ds(5 + j, tm), :])
            dxq = dxq + cw_ref[j:j + 1, :] * dcext[pl.ds(3 - j, tm), :]
        dcext[pl.ds(tm, 8), :] = dc[0:8]
        lane = lax.broadcasted_iota(jnp.int32, (tm, 128), 1)
        bav = ba_ref[...]
        beta, g, sarg = _gates(bav, al_ref[...], dt_ref[...], lane)
        dbg_v = dbg_ref[...]
        is_g = (lane >= NH) & (lane < 2 * NH)
        dbraw = jnp.where(lane < NH, dbg_v * beta * (1.0 - beta), 0.0)
        daraw = jnp.where(is_g, dbg_v * (-jnp.exp(al_ref[...])) * sarg, 0.0)
        dal_ref[...] += _sum8(jnp.where(is_g, dbg_v * g, 0.0))
        ddt_ref[...] += _sum8(daraw)
        dproj[:, 0:W3] = dxq.astype(BF)
        dproj[:, W3:W3 + GW] = dz_ref[...].astype(BF)
        dproj[:, W3 + GW:W3 + GW + PW] = dp_ref[...].astype(BF)
        dproj[:, W3 + GW + PW:DINP] = (dbraw + daraw).astype(BF)
        gain_v, scale_v = gn_ref[...], sc_ref[...]
        n, r, y, h = _norm_mod_fwd(x_ref[...], gain_v, sh_ref[...], scale_v)
        dpj = dproj[...]
        dh = _nn(dpj, w_ref[...])
        dw_acc[...] += _tn(dpj, h.astype(BF))
        dxn, dsh, dsc, dgn = _norm_mod_bwd(dh, n, r, y, gain_v, scale_v)
        dx_ref[...] = dxo_ref[...] + dxn
        dsh_ref[...] += dsh
        dsc_ref[...] += dsc
        dgn_ref[...] += dgn

        @pl.when(i == nt - 1)
        def _():
            dw_ref[...] = dw_acc[...].astype(BF)

    rev = lambda w: pl.BlockSpec((tm, w), lambda i: (nt - 1 - i, 0))
    halo = pl.BlockSpec((8, W3), lambda i: (jnp.maximum((nt - 1 - i) * (tm // 8) - 1, 0), 0))
    fix = lambda *s: pl.BlockSpec(s, lambda i: (0,) * len(s))
    vec = _const_spec((1, D))
    return pl.pallas_call(
        body, grid=(nt,), name="mix_proj_bwd",
        in_specs=[rev(D), rev(GW), rev(GW), rev(GW), rev(GW), rev(PW), rev(128), rev(W3), halo, rev(128), rev(D),
                  vec, vec, vec, _const_spec((DINP, D)), _const_spec((4, W3)), _const_spec((1, 128)),
                  _const_spec((1, 128))],
        out_specs=[rev(D), fix(DINP, D), fix(4, 8, W3), fix(8, 128), fix(8, 128), fix(8, D), fix(8, D), fix(8, D)],
        out_shape=[jax.ShapeDtypeStruct((T, D), F32), jax.ShapeDtypeStruct((DINP, D), BF),
                   jax.ShapeDtypeStruct((4, 8, W3), F32), jax.ShapeDtypeStruct((8, 128), F32),
                   jax.ShapeDtypeStruct((8, 128), F32), jax.ShapeDtypeStruct((8, D), F32),
                   jax.ShapeDtypeStruct((8, D), F32), jax.ShapeDtypeStruct((8, D), F32)],
        scratch_shapes=[pltpu.VMEM((tm + 8, W3), F32), pltpu.VMEM((tm + 8, W3), F32), pltpu.VMEM((tm, DINP), BF),
                        pltpu.VMEM((DINP, D), F32)],
        compiler_params=_cparams(),
    )(dxo, dqn, dkn, dv, dz, dp, dbg, xq, xq, ba, x, shift, scale, gain, winT, conv_w, alog, dtb)


MESH = pl.DeviceIdType.MESH
CHIP_RELS = ((1, 0), (0, 1), (1, 1))
DEV_RELS = tuple((dx, dy, dc) for dx in (0, 1) for dy in (0, 1) for dc in (0, 1) if (dx, dy, dc) != (0, 0, 0))
NMOD = 9
ADA_SH = NMOD * D // 4
WIN_SH = DIN // 4
WIN_PAD = 672
MSG_ROWS = 16
ANY = pl.BlockSpec(memory_space=pl.ANY)
VM = pl.BlockSpec(memory_space=pltpu.VMEM)


def _place():
    x, y, c = lax.axis_index("x"), lax.axis_index("y"), lax.axis_index("c")
    return x, y, c


class _SplitGather:
    N_SEMS = (3, 3, 3, 3, 1)

    def __init__(self, src, dst, ici_s, ici_r, d2d_s, d2d_r, lsem):
        self.src, self.dst = src, dst
        self.sems = (ici_s, ici_r, d2d_s, d2d_r)
        self.x, self.y, self.c = _place()
        self.myj = 2 * self.x + self.y
        half = src.shape[0] // 2
        self.mine = pl.ds(pl.multiple_of(self.c * half, 16), half)
        self.other = pl.ds(pl.multiple_of((1 - self.c) * half, 16), half)
        self.local = pltpu.make_async_copy(src, dst.at[self.myj], lsem.at[0])

    def _ici(self, k, slot):
        dx, dy = CHIP_RELS[k]
        return pltpu.make_async_remote_copy(
            src_ref=self.src.at[self.mine], dst_ref=self.dst.at[slot, self.mine], send_sem=self.sems[0].at[k],
            recv_sem=self.sems[1].at[k], device_id=(self.x ^ dx, self.y ^ dy, self.c), device_id_type=MESH)

    def _d2d(self, k, rows):
        dx, dy = CHIP_RELS[k]
        blk = self.dst.at[2 * (self.x ^ dx) + (self.y ^ dy), rows]
        return pltpu.make_async_remote_copy(
            src_ref=blk, dst_ref=blk, send_sem=self.sems[2].at[k], recv_sem=self.sems[3].at[k],
            device_id=(self.x, self.y, 1 - self.c), device_id_type=MESH)

    def start(self):
        self.local.start()
        for k in range(3):
            self._ici(k, self.myj).start()

    def forward(self):
        for k, (dx, dy) in enumerate(CHIP_RELS):
            self._ici(k, 2 * (self.x ^ dx) + (self.y ^ dy)).wait_recv()
            self._d2d(k, self.mine).start()

    def finish(self):
        for k in range(3):
            self._d2d(k, self.other).wait_recv()
        for k in range(3):
            self._d2d(k, self.mine).wait_send()
            self._ici(k, self.myj).wait_send()
        self.local.wait()


class _Scatter:
    def __init__(self, ins, outs, send, recv, lsem):
        self.ins, self.outs, self.send, self.recv, self.lsem = ins, outs, send, recv, lsem
        self.x, self.y, self.c = _place()
        self.myj = 2 * self.x + self.y

    def _copy(self, a, k, landing):
        dx, dy = CHIP_RELS[k]
        pj = 2 * (self.x ^ dx) + (self.y ^ dy)
        return pltpu.make_async_remote_copy(
            src_ref=self.ins[a].at[pj], dst_ref=self.outs[a].at[pj if landing else self.myj],
            send_sem=self.send.at[a, k], recv_sem=self.recv.at[a, k],
            device_id=(self.x ^ dx, self.y ^ dy, self.c), device_id_type=MESH)

    def _local(self, a):
        return pltpu.make_async_copy(self.ins[a].at[self.myj], self.outs[a].at[self.myj], self.lsem.at[a])

    def start(self):
        for a in range(len(self.ins)):
            self._local(a).start()
            for k in range(3):
                self._copy(a, k, False).start()

    def finish(self):
        for a in range(len(self.ins)):
            for k in range(3):
                self._copy(a, k, True).wait_recv()
            for k in range(3):
                self._copy(a, k, False).wait_send()
            self._local(a).wait()


class _ScatterHalf:
    def __init__(self, ins, outs, send, recv, lsem, half):
        self.ins, self.outs, self.send, self.recv, self.lsem, self.half = ins, outs, send, recv, lsem, half
        self.x, self.y, self.c = _place()
        self.myj = 2 * self.x + self.y

    def _copy(self, a, k, landing):
        dx, dy = CHIP_RELS[k]
        pj = 2 * (self.x ^ dx) + (self.y ^ dy)
        return pltpu.make_async_remote_copy(
            src_ref=self.ins[a].at[(self.y ^ dy)], dst_ref=self.outs[a].at[pj if landing else self.myj],
            send_sem=self.send.at[a, k], recv_sem=self.recv.at[a, k],
            device_id=(self.x ^ dx, self.y ^ dy, self.c), device_id_type=MESH)

    def _local(self, a):
        return pltpu.make_async_copy(self.ins[a].at[self.y], self.outs[a].at[self.myj], self.lsem.at[a])

    def _each(self, fn_mine, fn_to):
        for a in range(len(self.ins)):
            if fn_mine is not None:
                pl.when(self.x == self.half)(functools.partial(fn_mine, a))
            for k, (dx, dy) in enumerate(CHIP_RELS):
                pl.when((self.x ^ dx) == self.half)(functools.partial(fn_to, a, k))

    def start(self):
        self._each(lambda a: self._local(a).start(), lambda a, k: self._copy(a, k, False).start())

    def finish(self):
        def mine(a):
            for k in range(3):
                self._copy(a, k, True).wait_recv()
            self._local(a).wait()

        self._each(mine, lambda a, k: self._copy(a, k, False).wait_send())


def _scatter_sems(n):
    return [pltpu.SemaphoreType.DMA((n, 3)), pltpu.SemaphoreType.DMA((n, 3)), pltpu.SemaphoreType.DMA((n,))]


def _gather_sems():
    return [pltpu.SemaphoreType.DMA((k,)) for k in _SplitGather.N_SEMS]


def _comm_begin(i, plan, last):
    @pl.when(i == 0)
    def _():
        plan.start()

    if hasattr(plan, "forward"):
        @pl.when(i == max(last - 3, 0))
        def _():
            plan.forward()


def _comm_end(i, plan, last):
    @pl.when(i == last)
    def _():
        plan.finish()


def _ada_exchange(msg, w_ada, b_ada, wblock):
    def body(msg_ref, w_ref, b_ref, wb_ref, all_ref, mod_ref, wg_ref, modp, send1, recv1, send2, recv2, lsem, *gsems):
        x, y, c = _place()
        me = 4 * x + 2 * y + c
        own = pltpu.make_async_copy(msg_ref, all_ref.at[me], lsem.at[0])
        own.start()

        def gather(k, rel, slot):
            dx, dy, dc = rel
            return pltpu.make_async_remote_copy(
                src_ref=msg_ref, dst_ref=all_ref.at[slot], send_sem=send1.at[k], recv_sem=recv1.at[k],
                device_id=(x ^ dx, y ^ dy, c ^ dc), device_id_type=MESH)

        for k, rel in enumerate(DEV_RELS):
            gather(k, rel, me).start()
        for k, (dx, dy, dc) in enumerate(DEV_RELS):
            gather(k, (dx, dy, dc), 4 * (x ^ dx) + 2 * (y ^ dy) + (c ^ dc)).wait_recv()
        for k, rel in enumerate(DEV_RELS):
            gather(k, rel, me).wait_send()
        own.wait()
        wgather = _SplitGather(wb_ref, wg_ref, *gsems)
        wgather.start()

        for d in range(8):
            cv = all_ref[d, 0:8, :]
            act = cv * _sigmoid(cv)
            modp[d] = _nn(act, w_ref[...], precision=HI) + b_ref[...]

        myj = 2 * x + y
        keep = pltpu.make_async_copy(modp.at[me], mod_ref.at[myj], lsem.at[1])
        keep.start()

        def scatter(k, rel):
            dx, dy = rel
            return pltpu.make_async_remote_copy(
                src_ref=modp.at[4 * (x ^ dx) + 2 * (y ^ dy) + c], dst_ref=mod_ref.at[myj],
                send_sem=send2.at[k], recv_sem=recv2.at[k], device_id=(x ^ dx, y ^ dy, c), device_id_type=MESH)

        def landed(k, rel):
            dx, dy = rel
            return pltpu.make_async_remote_copy(
                src_ref=modp.at[me], dst_ref=mod_ref.at[2 * (x ^ dx) + (y ^ dy)],
                send_sem=send2.at[k], recv_sem=recv2.at[k], device_id=(x ^ dx, y ^ dy, c), device_id_type=MESH)

        for k, rel in enumerate(CHIP_RELS):
            scatter(k, rel).start()
        for k, rel in enumerate(CHIP_RELS):
            landed(k, rel).wait_recv()
        for k, rel in enumerate(CHIP_RELS):
            scatter(k, rel).wait_send()
        keep.wait()
        wgather.forward()
        wgather.finish()

    return pl.pallas_call(
        body, name="ada_exchange", in_specs=[VM, VM, VM, ANY], out_specs=[VM, VM, ANY],
        out_shape=[jax.ShapeDtypeStruct((8, MSG_ROWS, D), F32), jax.ShapeDtypeStruct((4, 8, ADA_SH), F32),
                   jax.ShapeDtypeStruct((4,) + wblock.shape, wblock.dtype)],
        scratch_shapes=[pltpu.VMEM((8, 8, ADA_SH), F32), pltpu.SemaphoreType.DMA((7,)), pltpu.SemaphoreType.DMA((7,)),
                        pltpu.SemaphoreType.DMA((3,)), pltpu.SemaphoreType.DMA((3,)), pltpu.SemaphoreType.DMA((2,))]
        + _gather_sems(),
        compiler_params=pltpu.CompilerParams(vmem_limit_bytes=VMEM_LIMIT),
    )(msg, w_ada, b_ada, wblock)


def _chip_exchange_hi(parts, landing, name):
    n = len(parts)

    def body(*refs):
        plan = _ScatterHalf(refs[:n], refs[2 * n:3 * n], *refs[3 * n:], half=1)
        plan.start()
        plan.finish()

    return pl.pallas_call(
        body, name=name, in_specs=[ANY] * (2 * n), out_specs=[ANY] * n,
        out_shape=[jax.ShapeDtypeStruct(t.shape, t.dtype) for t in landing], scratch_shapes=_scatter_sems(n),
        input_output_aliases={n + a: a for a in range(n)},
    )(*parts, *landing)


def _chip_exchange(parts, name):
    n = len(parts)

    def body(*refs):
        plan = _Scatter(refs[:n], refs[n:2 * n], *refs[2 * n:])
        plan.start()
        plan.finish()

    return pl.pallas_call(
        body, name=name, in_specs=[ANY] * n, out_specs=[ANY] * n,
        out_shape=[jax.ShapeDtypeStruct(p.shape, p.dtype) for p in parts], scratch_shapes=_scatter_sems(n),
    )(*parts)


class _AllGather:
    def __init__(self, ins, outs, send, recv, lsem):
        self.ins, self.outs, self.send, self.recv, self.lsem = ins, outs, send, recv, lsem
        self.x, self.y, self.c = _place()
        self.me = 4 * self.x + 2 * self.y + self.c

    def _copy(self, a, k, landing):
        dx, dy, dc = DEV_RELS[k]
        peer = 4 * (self.x ^ dx) + 2 * (self.y ^ dy) + (self.c ^ dc)
        return pltpu.make_async_remote_copy(
            src_ref=self.ins[a], dst_ref=self.outs[a].at[peer if landing else self.me],
            send_sem=self.send.at[a, k], recv_sem=self.recv.at[a, k],
            device_id=(self.x ^ dx, self.y ^ dy, self.c ^ dc), device_id_type=MESH)

    def _local(self, a):
        return pltpu.make_async_copy(self.ins[a], self.outs[a].at[self.me], self.lsem.at[a])

    def start(self):
        for a in range(len(self.ins)):
            self._local(a).start()
            for k in range(7):
                self._copy(a, k, False).start()

    def finish(self):
        for a in range(len(self.ins)):
            for k in range(7):
                self._copy(a, k, True).wait_recv()
            for k in range(7):
                self._copy(a, k, False).wait_send()
            self._local(a).wait()


def _allgather_sems(n):
    return [pltpu.SemaphoreType.DMA((n, 7)), pltpu.SemaphoreType.DMA((n, 7)), pltpu.SemaphoreType.DMA((n,))]


class _Plans:
    def __init__(self, plans):
        self.plans = plans

    def start(self):
        for p in self.plans:
            p.start()

    def finish(self):
        for p in self.plans:
            p.finish()


def _pair_exchange(parts, name):
    n = len(parts)

    def body(*refs):
        ins, outs = refs[:n], refs[n:2 * n]
        send, recv = refs[2 * n:]
        x, y, c = _place()
        cps = [pltpu.make_async_remote_copy(
            src_ref=ins[a], dst_ref=outs[a], send_sem=send.at[a], recv_sem=recv.at[a],
            device_id=(x, y, 1 - c), device_id_type=MESH) for a in range(n)]
        for cp in cps:
            cp.start()
        for cp in cps:
            cp.wait_recv()
        for cp in cps:
            cp.wait_send()

    shapes = [jax.ShapeDtypeStruct(p.shape, p.dtype) for p in parts]
    return pl.pallas_call(
        body, name=name, in_specs=[ANY] * n, out_specs=[ANY] * n, out_shape=shapes,
        scratch_shapes=[pltpu.SemaphoreType.DMA((n,)), pltpu.SemaphoreType.DMA((n,))],
    )(*parts)


def _row_tile(rows, cap):
    best = rows
    for t in range(8, min(cap, rows) + 1, 8):
        if rows % t == 0:
            best = t
    return best if rows % 8 == 0 else rows


def _sum_slots(parts, name):
    n, rows, width = parts.shape
    tr = _row_tile(rows, 352)

    def body(p_ref, o_ref):
        acc = p_ref[0].astype(F32)
        for j in range(1, n):
            acc = acc + p_ref[j].astype(F32)
        o_ref[...] = acc

    return pl.pallas_call(
        body, grid=(rows // tr,), name=name,
        in_specs=[pl.BlockSpec((n, tr, width), lambda i: (0, i, 0))],
        out_specs=pl.BlockSpec((tr, width), lambda i: (i, 0)),
        out_shape=jax.ShapeDtypeStruct((rows, width), F32),
        compiler_params=_cparams(),
    )(parts)


def _adamw_math(g, w, m, v):
    m_new = ADAM_B1 * m + (1.0 - ADAM_B1) * g
    v_new = ADAM_B2 * v + (1.0 - ADAM_B2) * (g * g)
    m_hat = m_new / (1.0 - ADAM_B1 ** ADAM_STEP)
    v_hat = v_new / (1.0 - ADAM_B2 ** ADAM_STEP)
    delta = -ADAM_LR * (m_hat / (jnp.sqrt(v_hat) + ADAM_EPS) + ADAM_WD * w)
    return delta, m_new, v_new


def _adamw(grads, w, m, v, name):
    rows, width = w.shape
    tr = _row_tile(rows, 256 if width <= 1024 else 128)
    ng = len(grads)

    def body(*refs):
        g = refs[0][...]
        for r in refs[1:ng]:
            g = g + r[...]
        w_ref, m_ref, v_ref, g_out, d_out, m_out, v_out = refs[ng:]
        delta, m_new, v_new = _adamw_math(g, w_ref[...], m_ref[...], v_ref[...])
        g_out[...] = g
        d_out[...] = delta
        m_out[...] = m_new
        v_out[...] = v_new

    if rows % 8 == 0 or width % 512:
        blk, steps = pl.BlockSpec((tr, width), lambda i: (i, 0)), rows // tr
    else:
        blk, steps = pl.BlockSpec((rows, 256), lambda i: (0, i)), width // 256
    return pl.pallas_call(
        body, grid=(steps,), name=name,
        in_specs=[blk] * (ng + 3), out_specs=[blk] * 4,
        out_shape=[jax.ShapeDtypeStruct((rows, width), F32)] * 4,
        compiler_params=_cparams(),
    )(*grads, w, m, v)


def _adamw_ada(msgs, dmods, w, m, v):
    rows, width = w.shape
    tr = 128

    def body(c_ref, dm_ref, w_ref, m_ref, v_ref, g_out, d_out, m_out, v_out):
        cv = jnp.concatenate([c_ref[d, 0:1, :] for d in range(8)], axis=0)
        act = cv * _sigmoid(cv)
        g = _tn(act, dm_ref[...], precision=HI)
        delta, m_new, v_new = _adamw_math(g, w_ref[...], m_ref[...], v_ref[...])
        g_out[...] = g
        d_out[...] = delta
        m_out[...] = m_new
        v_out[...] = v_new

    blk = pl.BlockSpec((tr, width), lambda i: (i, 0))
    return pl.pallas_call(
        body, grid=(rows // tr,), name="adamw_w_ada",
        in_specs=[pl.BlockSpec((8, MSG_ROWS, tr), lambda i: (0, 0, i)), pl.BlockSpec((8, width), lambda i: (0, 0)),
                  blk, blk, blk],
        out_specs=[blk] * 4, out_shape=[jax.ShapeDtypeStruct((rows, width), F32)] * 4,
        compiler_params=_cparams(),
    )(msgs, dmods, w, m, v)


def _adamw_small(parts, w, m, v, name):
    n, rows, width = parts.shape

    def body(p_ref, w_ref, m_ref, v_ref, g_out, d_out, m_out, v_out):
        g = p_ref[0]
        for j in range(1, n):
            g = g + p_ref[j]
        delta, m_new, v_new = _adamw_math(g, w_ref[...], m_ref[...], v_ref[...])
        g_out[...] = g
        d_out[...] = delta
        m_out[...] = m_new
        v_out[...] = v_new

    return pl.pallas_call(
        body, name=name, in_specs=[VM] * 4, out_specs=[VM] * 4,
        out_shape=[jax.ShapeDtypeStruct((rows, width), F32)] * 4,
        compiler_params=pltpu.CompilerParams(vmem_limit_bytes=VMEM_LIMIT),
    )(parts, w, m, v)


SMALL_ROWS = 24


def _pad_row(vec, width=D):
    vec = vec.reshape(1, -1)
    return jnp.pad(vec, ((0, 0), (0, width - vec.shape[1])))


def _lanes_4_7(vec4):
    return jnp.zeros((1, 128), F32).at[0, NH:2 * NH].set(vec4.reshape(NH))


def kernel(x, c, w_ada, b_ada, norm_ffn1, ffn1_gate, ffn1_up, ffn1_down, norm_mix, w_in, conv_w, a_log, dt_bias, gdn_norm, pool_w, pool_scale, w_out, norm_ffn2, ffn2_gate, ffn2_up, ffn2_down, final_norm, loss_target, m_w_ada, m_b_ada, m_norm_ffn1, m_ffn1_gate, m_ffn1_up, m_ffn1_down, m_norm_mix, m_w_in, m_conv_w, m_a_log, m_dt_bias, m_gdn_norm, m_pool_w, m_pool_scale, m_w_out, m_norm_ffn2, m_ffn2_gate, m_ffn2_up, m_ffn2_down, m_final_norm, v_w_ada, v_b_ada, v_norm_ffn1, v_ffn1_gate, v_ffn1_up, v_ffn1_down, v_norm_mix, v_w_in, v_conv_w, v_a_log, v_dt_bias, v_gdn_norm, v_pool_w, v_pool_scale, v_w_out, v_norm_ffn2, v_ffn2_gate, v_ffn2_up, v_ffn2_down, v_final_norm):
    xs = x[0]
    tgt = loss_target[0]
    chip = 2 * lax.axis_index("x") + lax.axis_index("y")
    me = 2 * chip + lax.axis_index("c")

    fsh = FF // 4
    block_a = jnp.concatenate([ffn1_gate[0].T, ffn1_up[0].T, ffn1_down[0]], axis=0).astype(BF)
    block_b = jnp.concatenate([ffn2_gate[0].T, ffn2_up[0].T, ffn2_down[0], w_out[0],
                               jnp.pad(w_in[0].T, ((0, WIN_PAD - WIN_SH), (0, 0)))], axis=0).astype(BF)

    msg = jnp.concatenate([jnp.broadcast_to(c, (8, D)), jnp.pad(conv_w[0], ((0, 0), (0, D - 3 * GW // 4))),
                           jnp.zeros((MSG_ROWS - 12, D), F32)], axis=0)
    b_sh = lax.dynamic_slice(b_ada, (0, chip * ADA_SH), (1, ADA_SH))
    msgs, mod4, gath_a = _ada_exchange(msg, w_ada[0], b_sh, block_a)
    mod = mod4[:, 0, :].reshape(NMOD, D)
    mrow = [mod[i:i + 1] for i in range(NMOD)]
    conv_full = jnp.concatenate([msgs[2 * j, 8:12, :3 * GW // 4] for j in range(4)], axis=1)
    alog, dtb = _lanes_4_7(a_log), _lanes_4_7(dt_bias)
    gnm = gdn_norm.reshape(1, DH)
    pwb = pool_w[0].astype(BF)
    psc = pool_scale.reshape(1, PW)
    fin = final_norm.reshape(1, D)

    x1, f1, a1, b1, gath_b = _ffn_fwd(xs, mrow[0], mrow[1], mrow[2], norm_ffn1, gath_a, 0, "ffn1_fwd", block_b)
    wo = gath_b[:, 3 * fsh:3 * fsh + D // 4, :].reshape(D, D)
    win_nat = gath_b[:, 3 * fsh + D // 4:3 * fsh + D // 4 + WIN_SH, :].reshape(DIN, D)
    winT = jnp.concatenate([win_nat[:4 * GW], win_nat[4 * GW + 2 * NH:], win_nat[4 * GW:4 * GW + 2 * NH],
                            jnp.zeros((128 - 2 * NH, D), BF)], axis=0)
    xq, ba, qn, kn, vv, z, pp, bg, gc = _mix_proj(x1, mrow[3], mrow[4], norm_mix, winT, conv_full, alog, dtb)
    x2, mixed, cat, o, sall = _mix_core(x1, mrow[5], qn, kn, vv, z, pp, bg, gc, gnm, pwb, psc, wo)
    x3, f2, a2, b2 = _ffn_fwd(x2, mrow[6], mrow[7], mrow[8], norm_ffn2, gath_b, 0, "ffn2_fwd")
    lpart, dx3, dfin = _loss_head(x3, tgt, fin)
    loss = lax.psum(jnp.sum(lpart), ("x", "y", "c"))

    slots = lambda t: t.reshape(4, t.shape[0] // 4, D)
    dx2, da2, db2, s2, h2, df2, dsh3, dsc3, dgt3, dn3 = _ffn_dgrad(
        dx3, x2, f2, a2, b2, mrow[6], mrow[7], mrow[8], norm_ffn2, gath_b, 0, "ffn2_dgrad")
    gg2, gu2, gd2 = _ffn_wgrad(da2, db2, s2, h2, df2, "ffn2_wgrad")
    dqn, dkn, dvv, dz, dpp, dbg, dgt2, dwo, dpw, dps, dgnm, *landed2 = _mix_core_bwd(
        dx2, mrow[5], mixed, cat, o, sall, qn, kn, vv, z, pp, bg, gc, gnm, pwb, psc, wo,
        scatter=[slots(gg2), slots(gu2), slots(gd2)])
    dx1, dwin, dcw, dal, ddt, dsh2, dsc2, dn2 = _mix_proj_bwd(
        dx2, dqn, dkn, dvv, dz, dpp, dbg, xq, ba, x1, mrow[3], mrow[4], norm_mix, winT, conv_full, alog, dtb)
    dwin_nat = jnp.concatenate([dwin[:4 * GW], dwin[4 * GW + PW:4 * GW + PW + 2 * NH], dwin[4 * GW:4 * GW + PW]], axis=0)
    dwin_sl = jnp.pad(dwin_nat.reshape(4, WIN_SH, D), ((0, 0), (0, WIN_PAD - WIN_SH), (0, 0)))
    dx0, da1, db1, s1, h1, df1, dsh1, dsc1, dgt1, dn1 = _ffn_dgrad(
        dx1, xs, f1, a1, b1, mrow[0], mrow[1], mrow[2], norm_ffn1, gath_a, 0, "ffn1_dgrad")
    red = lambda t: jnp.sum(t, axis=0, keepdims=True)
    small = jnp.concatenate(
        [red(dn1), red(dn2), red(dn3), red(dfin),
         red(dsh1), red(dsc1), red(dgt1), red(dsh2), red(dsc2), red(dgt2), red(dsh3), red(dsc3), red(dgt3),
         _pad_row(red(dps)), _pad_row(red(dgnm)), _pad_row(red(dal)), _pad_row(red(ddt)),
         jnp.sum(dcw, axis=1).reshape(6, D), jnp.zeros((1, D), F32)], axis=0)
    pair = lambda t: t.reshape(2, fsh, D)
    lo = _ffn_wgrad(da1, db1, s1, h1, df1, "ffn1_wgrad_lo", half=0)
    *hi, land_wo, land_win, small_all, dpw_all, l1g, l1u, l1d = _ffn_wgrad(
        da1, db1, s1, h1, df1, "ffn1_wgrad_hi", half=1, scatter=[slots(dwo), dwin_sl],
        allgather=[small, dpw.reshape(NG * 128, 128)], scatter_lo=[pair(t) for t in lo])
    landed1 = _chip_exchange_hi([pair(t) for t in hi], [l1g, l1u, l1d], "grad_scatter")

    landed = list(landed1) + list(landed2) + [land_wo, land_win]
    names = ("g1", "u1", "d1", "g2", "u2", "d2", "wo", "win")
    psum = [_sum_slots(t, "sum_" + nm) for t, nm in zip(landed, names)]
    qsum = _pair_exchange(psum, "grad_pair")

    def adamw_t(i, w, m, v, name, rows):
        res = _adamw([psum[i][:rows], qsum[i][:rows]], w[0].T, m[0].T, v[0].T, name)
        return [t.T for t in res]

    upd = {}
    upd["ffn1_gate"] = adamw_t(0, ffn1_gate, m_ffn1_gate, v_ffn1_gate, "adamw_g1", fsh)
    upd["ffn1_up"] = adamw_t(1, ffn1_up, m_ffn1_up, v_ffn1_up, "adamw_u1", fsh)
    upd["ffn1_down"] = _adamw([psum[2], qsum[2]], ffn1_down[0], m_ffn1_down[0], v_ffn1_down[0], "adamw_d1")
    upd["ffn2_gate"] = adamw_t(3, ffn2_gate, m_ffn2_gate, v_ffn2_gate, "adamw_g2", fsh)
    upd["ffn2_up"] = adamw_t(4, ffn2_up, m_ffn2_up, v_ffn2_up, "adamw_u2", fsh)
    upd["ffn2_down"] = _adamw([psum[5], qsum[5]], ffn2_down[0], m_ffn2_down[0], v_ffn2_down[0], "adamw_d2")
    upd["w_out"] = _adamw([psum[6], qsum[6]], w_out[0], m_w_out[0], v_w_out[0], "adamw_wo")
    upd["w_in"] = adamw_t(7, w_in, m_w_in, v_w_in, "adamw_win", WIN_SH)
    dmods = lax.dynamic_slice(small_all[:, 4:4 + NMOD, :].reshape(8, NMOD * D), (0, chip * ADA_SH), (8, ADA_SH))
    upd["w_ada"] = _adamw_ada(msgs, dmods, w_ada[0], m_w_ada[0], v_w_ada[0])

    def pack_small(nf1, nmx, nf2, fn, bada, psc_, gn_, al_, dt_):
        return jnp.concatenate(
            [nf1.reshape(1, D), nmx.reshape(1, D), nf2.reshape(1, D), fn.reshape(1, D), bada.reshape(NMOD, D),
             _pad_row(psc_), _pad_row(gn_), _pad_row(_lanes_4_7(al_)), _pad_row(_lanes_4_7(dt_)),
             jnp.zeros((7, D), F32)], axis=0)

    ws = pack_small(norm_ffn1, norm_mix, norm_ffn2, final_norm, b_ada, pool_scale, gdn_norm, a_log, dt_bias)
    ms = pack_small(m_norm_ffn1, m_norm_mix, m_norm_ffn2, m_final_norm, m_b_ada, m_pool_scale, m_gdn_norm, m_a_log, m_dt_bias)
    vs = pack_small(v_norm_ffn1, v_norm_mix, v_norm_ffn2, v_final_norm, v_b_ada, v_pool_scale, v_gdn_norm, v_a_log, v_dt_bias)
    sm = _adamw_small(small_all, ws, ms, vs, "adamw_small")
    pw2 = lambda t: t.reshape(NG * 128, 128)
    upd_pw = _adamw_small(dpw_all, pw2(pool_w), pw2(m_pool_w), pw2(v_pool_w), "adamw_pool_w")
    csh = 3 * GW // 4
    gconv = lax.dynamic_slice(sm[0][17:23].reshape(4, 3 * GW), (0, chip * csh), (4, csh))
    upd["conv_w"] = _adamw([gconv], conv_w[0], m_conv_w[0], v_conv_w[0], "adamw_conv")

    def small_out(k):
        t = sm[k]
        return {
            "norm_ffn1": t[0:1], "norm_mix": t[1:2], "norm_ffn2": t[2:3], "final_norm": t[3],
            "b_ada": t[4:4 + NMOD].reshape(1, NMOD * D), "pool_scale": t[13:14, :PW], "gdn_norm": t[14:15, :DH],
            "a_log": t[15:16, NH:2 * NH], "dt_bias": t[16:17, NH:2 * NH],
        }

    order = ["w_ada", "b_ada", "norm_ffn1", "ffn1_gate", "ffn1_up", "ffn1_down", "norm_mix", "w_in", "conv_w", "a_log",
             "dt_bias", "gdn_norm", "pool_w", "pool_scale", "w_out", "norm_ffn2", "ffn2_gate", "ffn2_up", "ffn2_down",
             "final_norm"]
    outs = [loss, dx0[None]]
    for k in range(4):
        smk = small_out(k)
        for nm in order:
            if nm in upd:
                outs.append(upd[nm][k][None])
            elif nm == "pool_w":
                outs.append(upd_pw[k].reshape(1, NG, 128, 128))
            else:
                outs.append(smk[nm])
    return tuple(outs)
```

```python
import functools

import jax
import jax.numpy as jnp
from jax import lax
from jax.experimental import pallas as pl
from jax.experimental.pallas import tpu as pltpu

F32 = jnp.float32
BF = jnp.bfloat16

D = 1024
FF = 2816
FH = FF // 2
NH = 4
DH = 128
GW = NH * DH
CH = 64
PW = 512
NG = 4
POOL_WINDOWS = (2, 4, 8, 16)
HALO = 16
DIN = 4 * GW + 2 * NH + PW
DINP = 3 * GW + GW + PW + 128
EPS = 1e-6
ADAM_LR, ADAM_B1, ADAM_B2, ADAM_EPS, ADAM_WD, ADAM_STEP = 0.001, 0.9, 0.999, 1e-08, 0.01, 10

VMEM_LIMIT = 56 * 1024 * 1024

NT_DIMS = (((1,), (1,)), ((), ()))
TN_DIMS = (((0,), (0,)), ((), ()))
HI = lax.Precision.HIGHEST


def _nt(a, b, **kw):
    return lax.dot_general(a, b, NT_DIMS, preferred_element_type=F32, **kw)


def _tn(a, b, **kw):
    return lax.dot_general(a, b, TN_DIMS, preferred_element_type=F32, **kw)


def _nn(a, b, **kw):
    return jnp.dot(a, b, preferred_element_type=F32, **kw)


def _cparams(sem=("arbitrary",), **kw):
    return pltpu.CompilerParams(dimension_semantics=sem, vmem_limit_bytes=VMEM_LIMIT, **kw)


def _const_spec(shape):
    nd = len(shape)
    return pl.BlockSpec(shape, lambda *_: (0,) * nd, pipeline_mode=pl.Buffered(1))


def _row_spec(tm, width):
    return pl.BlockSpec((tm, width), lambda i: (i, 0))


def _sum8(v):
    return jnp.sum(v.reshape(v.shape[0] // 8, 8, v.shape[1]), axis=0)


def _sigmoid(v):
    return 0.5 * jnp.tanh(0.5 * v) + 0.5


def _tile(T, cap=512):
    return min(cap, T)


def _norm_mod_fwd(xv, gain, shift, scale):
    r = lax.rsqrt(jnp.mean(xv * xv, axis=-1, keepdims=True) + EPS)
    n = xv * r
    y = n * gain
    return n, r, y, y * (1.0 + scale) + shift


def _norm_mod_bwd(dh, n, r, y, gain, scale):
    dy = dh * (1.0 + scale)
    dn = dy * gain
    dx = r * (dn - n * jnp.mean(dn * n, axis=-1, keepdims=True))
    return dx, _sum8(dh), _sum8(dh * y), _sum8(dy * n)


def _ffn_wspecs(k0):
    return [pl.BlockSpec((4, FF // 4, D), lambda i, k=k0 + n: (0, k, 0), pipeline_mode=pl.Buffered(1)) for n in range(3)]


def _half(w_ref, j):
    return w_ref[2 * j:2 * j + 2].reshape(FH, D)


def _ffn_fwd(x, shift, scale, gate, gain, wall, k0, name, gather_block=None):
    T = x.shape[0]
    tm = _tile(T)
    nt = T // tm
    comm = gather_block is not None

    def body(*refs):
        x_ref, sh_ref, sc_ref, gt_ref, gn_ref, wg_ref, wu_ref, wd_ref = refs[:8]
        if comm:
            xo_ref, f_ref, sa_ref, ga_ref, s_ref = refs[9:14]
            plan = _SplitGather(refs[8], refs[14], *refs[15:])
            _comm_begin(pl.program_id(0), plan, nt - 1)
        else:
            xo_ref, f_ref, sa_ref, ga_ref, s_ref = refs[8:13]
        xv = x_ref[...]
        _, _, _, h = _norm_mod_fwd(xv, gn_ref[...], sh_ref[...], sc_ref[...])
        hb = h.astype(BF)
        facc = jnp.zeros((tm, D), F32)
        for j in range(2):
            cols = slice(j * FH, (j + 1) * FH)
            a = _nt(hb, _half(wg_ref, j))
            b = _nt(hb, _half(wu_ref, j))
            sig = _sigmoid(a)
            sa = a * sig
            sa_ref[:, cols] = sa.astype(BF)
            ga_ref[:, cols] = (b * (sig * (1.0 + a * (1.0 - sig)))).astype(BF)
            s = (sa * b).astype(BF)
            s_ref[:, cols] = s
            facc = facc + _nn(s, _half(wd_ref, j))
        f_ref[...] = facc
        xo_ref[...] = xv + 0.5 * gt_ref[...] * facc
        if comm:
            _comm_end(pl.program_id(0), plan, nt - 1)

    vec = _const_spec((1, D))
    extra_in = [gather_block] if comm else []
    return pl.pallas_call(
        body, grid=(nt,), name=name,
        in_specs=[_row_spec(tm, D), vec, vec, vec, vec] + _ffn_wspecs(k0) + [ANY] * comm,
        out_specs=[_row_spec(tm, D), _row_spec(tm, D)] + [_row_spec(tm, FF)] * 3 + [ANY] * comm,
        out_shape=[jax.ShapeDtypeStruct((T, D), F32), jax.ShapeDtypeStruct((T, D), F32)]
        + [jax.ShapeDtypeStruct((T, FF), BF)] * 3
        + ([jax.ShapeDtypeStruct((4,) + gather_block.shape, gather_block.dtype)] if comm else []),
        scratch_shapes=_gather_sems() if comm else [],
        compiler_params=_cparams(),
    )(x, shift, scale, gate, gain, wall, wall, wall, *extra_in)


def _ffn_dgrad(dxo, x, f, sa, ga, shift, scale, gate, gain, wall, k0, name):
    T = x.shape[0]
    tm = _tile(T)
    nt = T // tm
    wspecs = _ffn_wspecs(k0)
    vec = _const_spec((1, D))
    acc = pl.BlockSpec((8, D), lambda i: (0, 0))
    accs = jax.ShapeDtypeStruct((8, D), F32)

    def body_a(dxo_ref, f_ref, sa_ref, ga_ref, gt_ref, wd_ref, da_ref, db_ref, df_ref, dgt_ref):
        i = pl.program_id(0)
        dxo_v = dxo_ref[...]
        dgate = _sum8(0.5 * f_ref[...] * dxo_v)
        dfb = (0.5 * gt_ref[...] * dxo_v).astype(BF)
        df_ref[...] = dfb
        for j in range(2):
            cols = slice(j * FH, (j + 1) * FH)
            ds = _nt(dfb, _half(wd_ref, j))
            da_ref[:, cols] = (ds * ga_ref[:, cols].astype(F32)).astype(BF)
            db_ref[:, cols] = (ds * sa_ref[:, cols].astype(F32)).astype(BF)

        @pl.when(i == 0)
        def _():
            dgt_ref[...] = dgate

        @pl.when(i > 0)
        def _():
            dgt_ref[...] += dgate

    wide = jax.ShapeDtypeStruct((T, FF), BF)
    da, db, df, dgt = pl.pallas_call(
        body_a, grid=(nt,), name=name + "_a",
        in_specs=[_row_spec(tm, D), _row_spec(tm, D), _row_spec(tm, FF), _row_spec(tm, FF), vec, wspecs[2]],
        out_specs=[_row_spec(tm, FF), _row_spec(tm, FF), _row_spec(tm, D), acc],
        out_shape=[wide, wide, jax.ShapeDtypeStruct((T, D), BF), accs],
        compiler_params=_cparams(),
    )(dxo, f, sa, ga, gate, wall)

    def body_b(dxo_ref, x_ref, da_ref, db_ref, sh_ref, sc_ref, gn_ref, wg_ref, wu_ref,
               dx_ref, h_ref, dsh_ref, dsc_ref, dgn_ref):
        i = pl.program_id(0)
        gain_v, scale_v = gn_ref[...], sc_ref[...]
        n, r, y, h = _norm_mod_fwd(x_ref[...], gain_v, sh_ref[...], scale_v)
        h_ref[...] = h.astype(BF)
        dh = _nn(da_ref[...], wg_ref[...].reshape(FF, D)) + _nn(db_ref[...], wu_ref[...].reshape(FF, D))
        dxn, dsh, dsc, dgn = _norm_mod_bwd(dh, n, r, y, gain_v, scale_v)
        dx_ref[...] = dxo_ref[...] + dxn

        @pl.when(i == 0)
        def _():
            dsh_ref[...] = dsh
            dsc_ref[...] = dsc
            dgn_ref[...] = dgn

        @pl.when(i > 0)
        def _():
            dsh_ref[...] += dsh
            dsc_ref[...] += dsc
            dgn_ref[...] += dgn

    dx, h, dsh, dsc, dgn = pl.pallas_call(
        body_b, grid=(nt,), name=name + "_b",
        in_specs=[_row_spec(tm, D), _row_spec(tm, D), _row_spec(tm, FF), _row_spec(tm, FF), vec, vec, vec,
                  wspecs[0], wspecs[1]],
        out_specs=[_row_spec(tm, D), _row_spec(tm, D), acc, acc, acc],
        out_shape=[jax.ShapeDtypeStruct((T, D), F32), jax.ShapeDtypeStruct((T, D), BF), accs, accs, accs],
        compiler_params=_cparams(),
    )(dxo, x, da, db, shift, scale, gain, wall, wall)
    return dx, da, db, h, df, dsh, dsc, dgt, dgn


def _ffn_wgrad(da, db, s, h, df, name, scatter=(), allgather=()):
    T = h.shape[0]
    tk = _tile(T, 512)
    nk = T // tk
    nj = 2
    groups = [g for g in (
        (list(scatter), _Scatter, _scatter_sems, lambda t: t.shape),
        (list(allgather), _AllGather, _allgather_sems, lambda t: (8,) + t.shape),
    ) if g[0]]
    extra = [t for g in groups for t in g[0]]
    ne = len(extra)

    def body(*refs):
        da_ref, db_ref, s_ref, h_ref, df_ref = refs[:5]
        og_ref, ou_ref, od_ref = refs[5 + ne:8 + ne]
        ag, au, ad = refs[8 + 2 * ne:11 + 2 * ne]
        k = pl.program_id(1)
        step = pl.program_id(0) * nk + k
        if ne:
            plans, at, sem_at = [], 0, 11 + 2 * ne
            for arrs, make, _, _ in groups:
                n = len(arrs)
                plans.append(make(refs[5 + at:5 + at + n], refs[8 + ne + at:8 + ne + at + n], *refs[sem_at:sem_at + 3]))
                at, sem_at = at + n, sem_at + 3
            plan = _Plans(plans)
            _comm_begin(step, plan, nj * nk - 1)

        @pl.when(k == 0)
        def _():
            for acc in (ag, au, ad):
                acc[...] = jnp.zeros((FH, D), F32)

        ag[...] += _tn(da_ref[...], h_ref[...])
        au[...] += _tn(db_ref[...], h_ref[...])
        ad[...] += _tn(s_ref[...], df_ref[...])

        @pl.when(k == nk - 1)
        def _():
            og_ref[...] = ag[...].astype(BF)
            ou_ref[...] = au[...].astype(BF)
            od_ref[...] = ad[...].astype(BF)

        if ne:
            _comm_end(step, plan, nj * nk - 1)

    colblk = pl.BlockSpec((tk, FH), lambda j, k: (k, j))
    rowblk = pl.BlockSpec((tk, D), lambda j, k: (k, 0))
    outblk = pl.BlockSpec((FH, D), lambda j, k: (j, 0))
    outs = jax.ShapeDtypeStruct((nj * FH, D), BF)
    return pl.pallas_call(
        body, grid=(nj, nk), name=name,
        in_specs=[colblk, colblk, colblk, rowblk, rowblk] + [ANY] * ne,
        out_specs=[outblk, outblk, outblk] + [ANY] * ne,
        out_shape=[outs, outs, outs] + [jax.ShapeDtypeStruct(shape(t), t.dtype) for g in groups for t, shape in
                                        ((t, g[3]) for t in g[0])],
        scratch_shapes=[pltpu.VMEM((FH, D), F32)] * 3 + [sm for g in groups for sm in g[2](len(g[0]))],
        compiler_params=_cparams(("arbitrary", "arbitrary")),
    )(da, db, s, h, df, *extra)


def _loss_head(x, target, gain):
    T = x.shape[0]
    tm = _tile(T)

    def body(x_ref, t_ref, gn_ref, ls_ref, dx_ref, dgn_ref):
        i = pl.program_id(0)
        xv = x_ref[...]
        gain_v = gn_ref[...]
        r = lax.rsqrt(jnp.mean(xv * xv, axis=-1, keepdims=True) + EPS)
        n = xv * r
        err = n * gain_v - t_ref[...]
        e2 = err * err
        part = e2[:, 0:128]
        for q in range(1, D // 128):
            part = part + e2[:, q * 128:(q + 1) * 128]
        lsum = _sum8(part) * (0.5 / D)
        dy = err * (1.0 / D)
        dn = dy * gain_v
        dx_ref[...] = r * (dn - n * jnp.mean(dn * n, axis=-1, keepdims=True))
        dgn = _sum8(dy * n)

        @pl.when(i == 0)
        def _():
            ls_ref[...] = lsum
            dgn_ref[...] = dgn

        @pl.when(i > 0)
        def _():
            ls_ref[...] += lsum
            dgn_ref[...] += dgn

    return pl.pallas_call(
        body, grid=(T // tm,), name="loss_head",
        in_specs=[_row_spec(tm, D), _row_spec(tm, D), _const_spec((1, D))],
        out_specs=[pl.BlockSpec((8, 128), lambda i: (0, 0)), _row_spec(tm, D), pl.BlockSpec((8, D), lambda i: (0, 0))],
        out_shape=[jax.ShapeDtypeStruct((8, 128), F32), jax.ShapeDtypeStruct((T, D), F32),
                   jax.ShapeDtypeStruct((8, D), F32)],
        compiler_params=_cparams(),
    )(x, target, gain)


def _seg_cumsum(v, row_in_chunk, reverse=False):
    n = v.shape[0]
    s = 1
    while s < CH:
        if reverse:
            moved = pltpu.roll(v, n - s, 0)
            ok = row_in_chunk < CH - s
        else:
            moved = pltpu.roll(v, s, 0)
            ok = row_in_chunk >= s
        v = v + jnp.where(ok, moved, 0.0)
        s *= 2
    return v


def _conv_silu(xq_ext_ref, cw_ref, tm):
    c = cw_ref[0:1, :] * xq_ext_ref[pl.ds(5, tm), :]
    for j in range(1, 4):
        c = c + cw_ref[j:j + 1, :] * xq_ext_ref[pl.ds(5 + j, tm), :]
    return c, _sigmoid(c)


def _gates(ba, alog, dtb, lane):
    beta = _sigmoid(ba)
    arg = ba + dtb
    softplus = jnp.maximum(arg, 0.0) + jnp.log(1.0 + jnp.exp(-jnp.abs(arg)))
    g = -jnp.exp(alog) * softplus
    return jnp.where(lane < NH, beta, 0.0), jnp.where((lane >= NH) & (lane < 2 * NH), g, 0.0), _sigmoid(arg)


def _mix_proj(x, shift, scale, gain, winT, conv_w, alog, dtb):
    T = x.shape[0]
    tm = _tile(T)

    def body(x_ref, sh_ref, sc_ref, gn_ref, w_ref, cw_ref, al_ref, dt_ref,
             xq_ref, ba_ref, qn_ref, kn_ref, v_ref, z_ref, p_ref, bg_ref, gc_ref, ext):
        i = pl.program_id(0)
        _, _, _, h = _norm_mod_fwd(x_ref[...], gn_ref[...], sh_ref[...], sc_ref[...])
        hb = h.astype(BF)

        @pl.when(i == 0)
        def _():
            ext[pl.ds(0, 8), :] = jnp.zeros((8, 3 * GW), F32)

        xq = _nt(hb, w_ref[pl.ds(0, 3 * GW), :])
        xq_ref[...] = xq
        ext[pl.ds(8, tm), :] = xq
        z_ref[...] = _nt(hb, w_ref[pl.ds(3 * GW, GW), :])
        p_ref[...] = _nt(hb, w_ref[pl.ds(4 * GW, PW), :])
        ba = _nt(hb, w_ref[pl.ds(4 * GW + PW, 128), :])
        ba_ref[...] = ba

        c, sg = _conv_silu(ext, cw_ref, tm)
        ext[pl.ds(0, 8), :] = ext[pl.ds(tm, 8), :]
        qt = c * sg
        for hd in range(NH):
            cq = slice(hd * DH, (hd + 1) * DH)
            ck = slice(GW + hd * DH, GW + (hd + 1) * DH)
            qh, kh = qt[:, cq], qt[:, ck]
            qn_ref[:, cq] = qh * (lax.rsqrt(jnp.sum(qh * qh, axis=-1, keepdims=True) + EPS) * DH ** -0.5)
            kn_ref[:, cq] = kh * lax.rsqrt(jnp.sum(kh * kh, axis=-1, keepdims=True) + EPS)
        v_ref[...] = qt[:, 2 * GW:3 * GW]

        lane = lax.broadcasted_iota(jnp.int32, (tm, 128), 1)
        row = lax.broadcasted_iota(jnp.int32, (tm, 128), 0) % CH
        beta, g, _ = _gates(ba, al_ref[...], dt_ref[...], lane)
        bg_ref[...] = beta + g
        gc_ref[...] = _seg_cumsum(g, row)

    wide = lambda w: _row_spec(tm, w)
    shp = lambda w: jax.ShapeDtypeStruct((T, w), F32)
    return pl.pallas_call(
        body, grid=(T // tm,), name="mix_proj",
        in_specs=[wide(D), _const_spec((1, D)), _const_spec((1, D)), _const_spec((1, D)), _const_spec((DINP, D)),
                  _const_spec((4, 3 * GW)), _const_spec((1, 128)), _const_spec((1, 128))],
        out_specs=[wide(3 * GW), wide(128), wide(GW), wide(GW), wide(GW), wide(GW), wide(PW), wide(128), wide(128)],
        out_shape=[shp(3 * GW), shp(128), shp(GW), shp(GW), shp(GW), shp(GW), shp(PW), shp(128), shp(128)],
        scratch_shapes=[pltpu.VMEM((tm + 8, 3 * GW), F32)],
        compiler_params=_cparams(),
    )(x, shift, scale, gain, winT, conv_w, alog, dtb)


R2 = 2 * CH
TRI_PREC = None


def _tmm(fn, a, b):
    if TRI_PREC is None:
        return fn(a.astype(BF), b.astype(BF))
    return fn(a, b, precision=TRI_PREC)


def _pair_consts():
    ii = lax.broadcasted_iota(jnp.int32, (R2, R2), 0)
    jj = lax.broadcasted_iota(jnp.int32, (R2, R2), 1)
    same = (ii < CH) == (jj < CH)
    r = lax.broadcasted_iota(jnp.int32, (R2, 1), 0)
    return dict(causal=same & (ii >= jj), strict=same & (ii > jj), eye=(ii == jj).astype(F32), rowA=r < CH,
                last=(r == CH - 1) | (r == R2 - 1),
                rowS=lax.broadcasted_iota(jnp.int32, (2 * DH, 1), 0) < DH)


def _tri_inverse_many(ms, eye):
    pws = [-m for m in ms]
    ts = [eye + p for p in pws]
    for _ in range(5):
        pws = [_tmm(_nn, p, p) for p in pws]
        ts = [_tmm(_nn, t, eye + p) for t, p in zip(ts, pws)]
    return ts


def _egl_rows(gl):
    egl = jnp.exp(gl)
    return egl, jnp.concatenate([jnp.broadcast_to(egl[0:1], (DH, 1)), jnp.broadcast_to(egl[CH:CH + 1], (DH, 1))], axis=0)


def _pair_intra(items, cn):
    causal, rowA = cn["causal"], cn["rowA"]

    def bd(t):
        return jnp.concatenate([jnp.where(rowA, t, 0.0), jnp.where(rowA, 0.0, t)], axis=1).astype(BF)

    outs = []
    for q, k, v, beta, gcv, gl in items:
        gc_b = jnp.broadcast_to(gcv, (R2, R2))
        gam = jnp.where(causal, jnp.exp(jnp.where(causal, gc_b - gc_b.T, 0.0)), 0.0)
        kb = k * beta
        kbf = k.astype(BF)
        P = _nt(kb.astype(BF), kbf)
        QK = _nt(q.astype(BF), kbf)
        E = jnp.exp(gcv)
        outs.append(dict(gam=gam, kb=kb, vb=v * beta, P=P, QK=QK, E=E, Fd=jnp.exp(gl - gcv), kbE=kb * E,
                         Q=QK * gam, qE_bd=bd(q * E)))
    tms = _tri_inverse_many([jnp.where(cn["strict"], d["P"] * d["gam"], 0.0) for d in outs], cn["eye"])
    for d, tm_, (q, k, v, beta, gcv, gl) in zip(outs, tms, items):
        d["Tm"] = tm_
        d["u"] = _tmm(_nn, tm_, d["vb"])
        d["w_bd"] = bd(_tmm(_nn, tm_, d["kbE"]))
        d["kF_bd"] = bd(k * d["Fd"])
    return outs


def _pair_scan(it, S, egl_st):
    Sb = S.astype(BF)
    vn = it["u"] - _nn(it["w_bd"], Sb)
    vnb = vn.astype(BF)
    o = _nn(it["qE_bd"], Sb) + _nn(it["Q"].astype(BF), vnb)
    return vnb, o, S * egl_st + _tn(it["kF_bd"], vnb)


def _pair_forward(q, k, v, beta, gcv, gl, S, cn):
    fw = _pair_intra([(q, k, v, beta, gcv, gl)], cn)[0]
    fw["egl"], fw["egl_st"] = _egl_rows(gl)
    fw["vnb"], fw["o"], fw["S_new"] = _pair_scan(fw, S, fw["egl_st"])
    return fw


def _stack_heads(ref, rows, pair):
    return jnp.concatenate([ref[rows, (2 * pair) * DH:(2 * pair + 1) * DH],
                            ref[rows, (2 * pair + 1) * DH:(2 * pair + 2) * DH]], axis=0)


def _stack_cols(val, lane_a, lane_b, bcast_rows=None):
    a, b = val[:, lane_a:lane_a + 1], val[:, lane_b:lane_b + 1]
    if bcast_rows:
        a, b = jnp.broadcast_to(a, (bcast_rows, 1)), jnp.broadcast_to(b, (bcast_rows, 1))
    return jnp.concatenate([a, b], axis=0)


def _pool_windows(ext, tm, reverse):
    n = tm + HALO
    outs = []
    for gi in range(NG):
        a = ext[:, gi * 128:(gi + 1) * 128]
        s = 1
        while s < POOL_WINDOWS[gi]:
            a = a + pltpu.roll(a, (n - s) if reverse else s, 0)
            s *= 2
        outs.append(a[0:tm] if reverse else a[HALO:HALO + tm])
    return jnp.concatenate(outs, axis=1)


def _pool_count(tm, tile_index):
    t1 = (lax.broadcasted_iota(jnp.int32, (tm, PW), 0) + tile_index * tm + 1).astype(F32)
    win = jnp.concatenate([jnp.full((tm, 128), float(w), F32) for w in POOL_WINDOWS], axis=1)
    return 1.0 / jnp.minimum(t1, win)


def _chunk_item(qn_ref, kn_ref, v_ref, bg_ref, gc_ref, c, pr):
    r0 = pl.multiple_of(c * CH, CH)
    rows = pl.ds(r0, CH)
    bgv = bg_ref[rows, :]
    gcv_all = gc_ref[rows, :]
    gl_all = gc_ref[pl.ds(r0 + CH - 1, 1), :]
    ha, hb = 2 * pr, 2 * pr + 1
    return (_stack_heads(qn_ref, rows, pr), _stack_heads(kn_ref, rows, pr), _stack_heads(v_ref, rows, pr),
            _stack_cols(bgv, ha, hb), _stack_cols(gcv_all, NH + ha, NH + hb), _stack_cols(gl_all, NH + ha, NH + hb, CH))


CHUNK_GROUP = 4


def _mix_core(x, gate, qn, kn, v, z, p, bg, gc, gnorm, pool_w, pool_scale, w_out):
    T = x.shape[0]
    tm = _tile(T)
    nc = tm // CH
    cg = CHUNK_GROUP if nc % CHUNK_GROUP == 0 else 1
    npb = nc * (NH // 2)

    def body(x_ref, gt_ref, qn_ref, kn_ref, v_ref, z_ref, p_ref, bg_ref, gc_ref, gnm_ref, pw_ref, ps_ref, wo_ref,
             xo_ref, mx_ref, cat_ref, o_ref, sall_ref, S_scr, pext, u_s, w_s, qe_s, kf_s, q_s):
        i = pl.program_id(0)

        @pl.when(i == 0)
        def _():
            S_scr[...] = jnp.zeros((NH * DH, DH), F32)
            pext[pl.ds(0, HALO), :] = jnp.zeros((HALO, PW), F32)

        cn = _pair_consts()

        def intra(g, carry):
            idx = [(g * cg + dc, pr) for dc in range(cg) for pr in range(NH // 2)]
            res = _pair_intra([_chunk_item(qn_ref, kn_ref, v_ref, bg_ref, gc_ref, c, pr) for c, pr in idx], cn)
            for (c, pr), d in zip(idx, res):
                pi = c * (NH // 2) + pr
                u_s[pi] = d["u"]
                w_s[pi] = d["w_bd"]
                qe_s[pi] = d["qE_bd"]
                kf_s[pi] = d["kF_bd"]
                q_s[pi] = d["Q"].astype(BF)
            return carry

        lax.fori_loop(0, nc // cg, intra, 0)

        def scan(c, carry):
            r0 = pl.multiple_of(c * CH, CH)
            rows = pl.ds(r0, CH)
            gl_all = gc_ref[pl.ds(r0 + CH - 1, 1), :]
            for pr in range(NH // 2):
                ha, hb = 2 * pr, 2 * pr + 1
                pi = c * (NH // 2) + pr
                S = S_scr[pl.ds(pr * 2 * DH, 2 * DH), :]
                sall_ref[c, ha:hb + 1] = S.reshape(2, DH, DH)
                _, egl_st = _egl_rows(_stack_cols(gl_all, NH + ha, NH + hb, CH))
                it = dict(u=u_s[pi], w_bd=w_s[pi], qE_bd=qe_s[pi], kF_bd=kf_s[pi], Q=q_s[pi])
                _, o, S_new = _pair_scan(it, S, egl_st)
                o_ref[rows, ha * DH:(ha + 1) * DH] = o[0:CH]
                o_ref[rows, hb * DH:(hb + 1) * DH] = o[CH:R2]
                S_scr[pl.ds(pr * 2 * DH, 2 * DH), :] = S_new
            return carry

        lax.fori_loop(0, nc, scan, 0)

        for hd in range(NH):
            cols = slice(hd * DH, (hd + 1) * DH)
            oh = o_ref[:, cols]
            zh = z_ref[:, cols]
            r = lax.rsqrt(jnp.mean(oh * oh, axis=-1, keepdims=True) + EPS)
            cat_ref[:, cols] = (oh * r * gnm_ref[...] * (zh * _sigmoid(zh))).astype(BF)

        pv = p_ref[...]
        pext[pl.ds(HALO, tm), :] = pv
        pooled = _pool_windows(pext[...], tm, False) * _pool_count(tm, i) - pv
        pext[pl.ds(0, HALO), :] = pext[pl.ds(tm, HALO), :]
        for gi in range(NG):
            cols = slice(gi * 128, (gi + 1) * 128)
            pm = _nn(pooled[:, cols].astype(BF), pw_ref[gi])
            cat_ref[:, GW + gi * 128:GW + (gi + 1) * 128] = (pm * ps_ref[:, cols]).astype(BF)

        mixed = _nn(cat_ref[...], wo_ref[...])
        mx_ref[...] = mixed
        xo_ref[...] = x_ref[...] + gt_ref[...] * mixed

    wide = lambda w: _row_spec(tm, w)
    return pl.pallas_call(
        body, grid=(T // tm,), name="mix_core",
        in_specs=[wide(D), _const_spec((1, D)), wide(GW), wide(GW), wide(GW), wide(GW), wide(PW), wide(128), wide(128),
                  _const_spec((1, DH)), _const_spec((NG, 128, 128)), _const_spec((1, PW)), _const_spec((D, D))],
        out_specs=[wide(D), wide(D), wide(D), wide(GW), pl.BlockSpec((nc, NH, DH, DH), lambda i: (i, 0, 0, 0))],
        out_shape=[jax.ShapeDtypeStruct((T, D), F32), jax.ShapeDtypeStruct((T, D), F32),
                   jax.ShapeDtypeStruct((T, D), BF), jax.ShapeDtypeStruct((T, GW), F32),
                   jax.ShapeDtypeStruct((T // CH, NH, DH, DH), F32)],
        scratch_shapes=[pltpu.VMEM((NH * DH, DH), F32), pltpu.VMEM((tm + HALO, PW), F32),
                        pltpu.VMEM((npb, R2, DH), F32), pltpu.VMEM((npb, R2, 2 * DH), BF),
                        pltpu.VMEM((npb, R2, 2 * DH), BF), pltpu.VMEM((npb, R2, 2 * DH), BF),
                        pltpu.VMEM((npb, R2, R2), BF)],
        compiler_params=_cparams(),
    )(x, gate, qn, kn, v, z, p, bg, gc, gnorm, pool_w, pool_scale, w_out)


def _pair_scan_bwd(it, S, dSn, do, egl, egl_st, cn):
    bf = lambda t: t.astype(BF)
    rowA, rowS = cn["rowA"], cn["rowS"]
    sel = lambda t: jnp.where(rowA, t[:, 0:DH], t[:, DH:2 * DH])
    Sb, dSb, dob = bf(S), bf(dSn), bf(do)
    vnb = bf(it["u"] - _nn(it["w_bd"], Sb))
    dvn = _tn(bf(it["Q"]), dob) + _nn(it["kF_bd"], dSb)
    dQ = _nt(dob, vnb)
    dqE = sel(_nt(dob, Sb))
    dkF = sel(_nt(vnb, dSb))
    dvnb = bf(dvn)
    dw = -sel(_nt(dvnb, Sb))
    dS_new = _tn(it["qE_bd"], dob) + egl_st * dSn - _tn(it["w_bd"], dvnb)
    prod = jnp.sum(dSn * S, axis=1, keepdims=True)
    d_egl_a = jnp.sum(jnp.where(rowS, prod, 0.0), axis=0, keepdims=True)
    d_egl_b = jnp.sum(jnp.where(rowS, 0.0, prod), axis=0, keepdims=True)
    return dict(dvn=dvn, dQ=dQ, dqE=dqE, dkF=dkF, dw=dw, degl=jnp.where(rowA, d_egl_a, d_egl_b) * egl), dS_new


def _pair_intra_bwd(items, cn):
    bf = lambda t: t.astype(BF)
    rowA = cn["rowA"]
    for d in items:
        d["kb"] = d["k"] * d["beta"]
        d["E"] = jnp.exp(d["gcv"])
        d["Fd"] = jnp.exp(d["gl"] - d["gcv"])
        d["TmT"] = d["Tm"].T
        d["dvb"] = _tmm(_nn, d["TmT"], d["dvn"])
        d["dkbE"] = _tmm(_nn, d["TmT"], d["dw"])
        dTm = _tmm(_nt, d["dvn"], d["v"] * d["beta"]) + _tmm(_nt, d["dw"], d["kb"] * d["E"])
        d["X"] = _tmm(_nt, dTm, d["Tm"])
    for d in items:
        d["dA"] = -_tmm(_nn, d["TmT"], d["X"])
    outs = []
    for d in items:
        q, k, v, beta, gam = d["q"], d["k"], d["v"], d["beta"], d["gam"]
        N = jnp.where(cn["strict"], d["dA"] * gam, 0.0)
        Rm = d["dQ"] * gam
        Wm = Rm * d["QK"] + N * d["P"]
        dgc = jnp.sum(Wm, axis=1, keepdims=True) - jnp.sum(Wm.T, axis=1, keepdims=True)
        kbf, Nb, Rb = bf(k), bf(N), bf(Rm)
        E, Fd = d["E"], d["Fd"]
        dq = d["dqE"] * E + _nn(Rb, kbf)
        dkb = d["dkbE"] * E + _nn(Nb, kbf)
        dk = d["dkF"] * Fd + _tn(Rb, bf(q)) + _tn(Nb, bf(d["kb"])) + beta * dkb
        dbeta = jnp.sum(dkb * k + d["dvb"] * v, axis=1, keepdims=True)
        dE = jnp.sum(d["dqE"] * q + d["dkbE"] * d["kb"], axis=1, keepdims=True)
        fdf = jnp.sum(d["dkF"] * k, axis=1, keepdims=True) * Fd
        dgl = d["degl"] + jnp.where(rowA, jnp.sum(jnp.where(rowA, fdf, 0.0), axis=0, keepdims=True),
                                    jnp.sum(jnp.where(rowA, 0.0, fdf), axis=0, keepdims=True))
        dgc = dgc + dE * E - fdf + jnp.where(cn["last"], dgl, 0.0)
        outs.append((dq, dk, beta * d["dvb"], dbeta, dgc))
    return outs


def _mix_core_bwd(dxo, gate, mixed, cat, o, sall, qn, kn, v, z, p, bg, gc, gnorm, pool_w, pool_scale, w_out,
                  scatter=()):
    T = dxo.shape[0]
    tm = _tile(T, 256)
    nt = T // tm
    nc = tm // CH
    ns = len(scatter)
    cg = CHUNK_GROUP if nc % CHUNK_GROUP == 0 else 1
    npb = nc * (NH // 2)

    def body(*refs):
        (dx_ref, gt_ref, mx_ref, cat_ref, o_ref, sall_ref, qn_ref, kn_ref, v_ref, z_ref, p_ref, ph_ref, bg_ref,
         gc_ref, gnm_ref, pw_ref, ps_ref, wo_ref) = refs[:18]
        (dq_ref, dk_ref, dv_ref, dz_ref, dp_ref, dbg_ref, dgt_ref, dwo_ref, dpw_ref, dps_ref,
         dgn_ref) = refs[18 + ns:29 + ns]
        dS_scr, pext, yext, do_buf, dwo_acc = refs[29 + 2 * ns:34 + 2 * ns]
        (gam_s, p_s, qk_s, tm_s, dqq_s, u_s, dvn_s, dqe_s, dkf_s, dw_s, w_s, qe_s, kf_s, q_s,
         degl_s) = refs[34 + 2 * ns:49 + 2 * ns]
        i = pl.program_id(0)
        ti = nt - 1 - i
        if ns:
            plan = _Scatter(refs[18:18 + ns], refs[29 + ns:29 + 2 * ns], *refs[49 + 2 * ns:])
            _comm_begin(i, plan, nt - 1)

        @pl.when(i == 0)
        def _():
            dS_scr[...] = jnp.zeros((NH * DH, DH), F32)
            yext[pl.ds(tm, HALO), :] = jnp.zeros((HALO, PW), F32)
            dwo_acc[...] = jnp.zeros((D, D), F32)
            dgt_ref[...] = jnp.zeros((8, D), F32)
            dpw_ref[...] = jnp.zeros((NG, 128, 128), F32)
            dps_ref[...] = jnp.zeros((8, PW), F32)
            dgn_ref[...] = jnp.zeros((8, DH), F32)

        dx2 = dx_ref[...]
        dgt_ref[...] += _sum8(mx_ref[...] * dx2)
        dmix = (gt_ref[...] * dx2).astype(BF)
        dcat = _nt(dmix, wo_ref[...])
        dwo_acc[...] += _tn(cat_ref[...], dmix)

        pv = p_ref[...]
        pext[pl.ds(0, HALO), :] = jnp.where(ti == 0, 0.0, ph_ref[...])
        pext[pl.ds(HALO, tm), :] = pv
        inv_cnt = _pool_count(tm, ti)
        pooled = _pool_windows(pext[...], tm, False) * inv_cnt - pv
        dpooled = []
        for gi in range(NG):
            cols = slice(gi * 128, (gi + 1) * 128)
            pgb = pooled[:, cols].astype(BF)
            pm = _nn(pgb, pw_ref[gi])
            dpo = dcat[:, GW + gi * 128:GW + (gi + 1) * 128]
            dps_ref[:, cols] += _sum8(dpo * pm)
            dpm = (dpo * ps_ref[:, cols]).astype(BF)
            dpooled.append(_nt(dpm, pw_ref[gi]))
            dpw_ref[gi] += _tn(pgb, dpm)
        dpooled = jnp.concatenate(dpooled, axis=1)
        y = dpooled * inv_cnt
        yext[pl.ds(0, tm), :] = y
        dp_ref[...] = _pool_windows(yext[...], tm, True) - dpooled
        yext[pl.ds(tm, HALO), :] = y[0:HALO]

        gnm = gnm_ref[...]
        dgn = jnp.zeros((8, DH), F32)
        for hd in range(NH):
            cols = slice(hd * DH, (hd + 1) * DH)
            oh = o_ref[:, cols]
            zh = z_ref[:, cols]
            r = lax.rsqrt(jnp.mean(oh * oh, axis=-1, keepdims=True) + EPS)
            n = oh * r
            sg = _sigmoid(zh)
            zs = zh * sg
            dgo = dcat[:, cols]
            dgn = dgn + _sum8(dgo * zs * n)
            dn = dgo * zs * gnm
            do_buf[:, cols] = r * (dn - n * jnp.mean(dn * n, axis=-1, keepdims=True))
            dz_ref[:, cols] = dgo * n * gnm * (sg * (1.0 + zh * (1.0 - sg)))
        dgn_ref[...] += dgn

        cn = _pair_consts()
        lane_c = lax.broadcasted_iota(jnp.int32, (CH, 128), 1)

        def intra(g, carry):
            idx = [(g * cg + dc, pr) for dc in range(cg) for pr in range(NH // 2)]
            res = _pair_intra([_chunk_item(qn_ref, kn_ref, v_ref, bg_ref, gc_ref, c, pr) for c, pr in idx], cn)
            for (c, pr), d in zip(idx, res):
                pi = c * (NH // 2) + pr
                gam_s[pi], p_s[pi], qk_s[pi], tm_s[pi], u_s[pi] = d["gam"], d["P"], d["QK"], d["Tm"], d["u"]
                w_s[pi], qe_s[pi], kf_s[pi], q_s[pi] = d["w_bd"], d["qE_bd"], d["kF_bd"], d["Q"].astype(BF)
            return carry

        lax.fori_loop(0, nc // cg, intra, 0)

        def scan(cc, carry):
            c = nc - 1 - cc
            r0 = pl.multiple_of(c * CH, CH)
            rows = pl.ds(r0, CH)
            gl_all = gc_ref[pl.ds(r0 + CH - 1, 1), :]
            for pr in range(NH // 2):
                ha, hb = 2 * pr, 2 * pr + 1
                pi = c * (NH // 2) + pr
                srows = pl.ds(pr * 2 * DH, 2 * DH)
                S = sall_ref[c, ha:hb + 1].reshape(2 * DH, DH)
                egl, egl_st = _egl_rows(_stack_cols(gl_all, NH + ha, NH + hb, CH))
                it = dict(u=u_s[pi], w_bd=w_s[pi], qE_bd=qe_s[pi], kF_bd=kf_s[pi], Q=q_s[pi])
                g, dS_new = _pair_scan_bwd(it, S, dS_scr[srows, :], _stack_heads(do_buf, rows, pr), egl, egl_st, cn)
                dvn_s[pi], dqq_s[pi], dqe_s[pi], dkf_s[pi], dw_s[pi] = g["dvn"], g["dQ"], g["dqE"], g["dkF"], g["dw"]
                degl_s[pi] = g["degl"]
                dS_scr[srows, :] = dS_new
            return carry

        lax.fori_loop(0, nc, scan, 0)

        def intra_bwd(g, carry):
            idx = [(g * cg + dc, pr) for dc in range(cg) for pr in range(NH // 2)]
            items = []
            for c, pr in idx:
                pi = c * (NH // 2) + pr
                q, k, vv, beta, gcv, gl = _chunk_item(qn_ref, kn_ref, v_ref, bg_ref, gc_ref, c, pr)
                items.append(dict(q=q, k=k, v=vv, beta=beta, gcv=gcv, gl=gl, gam=gam_s[pi], P=p_s[pi], QK=qk_s[pi],
                                  Tm=tm_s[pi], dvn=dvn_s[pi], dQ=dqq_s[pi], dqE=dqe_s[pi], dkF=dkf_s[pi], dw=dw_s[pi],
                                  degl=degl_s[pi]))
            res = _pair_intra_bwd(items, cn)
            for dc in range(cg):
                c = g * cg + dc
                rows = pl.ds(pl.multiple_of(c * CH, CH), CH)
                dbg = jnp.zeros((CH, 128), F32)
                for pr in range(NH // 2):
                    dq, dk, dv, dbeta, dgc = res[dc * (NH // 2) + pr]
                    for hd, half in ((2 * pr, slice(0, CH)), (2 * pr + 1, slice(CH, R2))):
                        cols = slice(hd * DH, (hd + 1) * DH)
                        dq_ref[rows, cols] = dq[half]
                        dk_ref[rows, cols] = dk[half]
                        dv_ref[rows, cols] = dv[half]
                        dbg = dbg + jnp.where(lane_c == hd, dbeta[half], 0.0) + jnp.where(lane_c == NH + hd, dgc[half], 0.0)
                dbg_ref[rows, :] = dbg
            return carry

        lax.fori_loop(0, nc // cg, intra_bwd, 0)

        lane = lax.broadcasted_iota(jnp.int32, (tm, 128), 1)
        row = lax.broadcasted_iota(jnp.int32, (tm, 128), 0) % CH
        dbg_all = dbg_ref[...]
        dg = _seg_cumsum(jnp.where(lane >= NH, dbg_all, 0.0), row, reverse=True)
        dbg_ref[...] = jnp.where(lane < NH, dbg_all, dg)

        @pl.when(i == nt - 1)
        def _():
            dwo_ref[...] = dwo_acc[...].astype(BF)

        if ns:
            _comm_end(i, plan, nt - 1)

    rev = lambda w: pl.BlockSpec((tm, w), lambda i: (nt - 1 - i, 0))
    halo = pl.BlockSpec((HALO, PW), lambda i: (jnp.maximum((nt - 1 - i) * (tm // HALO) - 1, 0), 0))
    shp = lambda w: jax.ShapeDtypeStruct((T, w), F32)
    fix = lambda *s: pl.BlockSpec(s, lambda i: (0,) * len(s))
    return pl.pallas_call(
        body, grid=(nt,), name="mix_core_bwd",
        in_specs=[rev(D), _const_spec((1, D)), rev(D), rev(D), rev(GW),
                  pl.BlockSpec((nc, NH, DH, DH), lambda i: (nt - 1 - i, 0, 0, 0)),
                  rev(GW), rev(GW), rev(GW), rev(GW), rev(PW), halo, rev(128), rev(128),
                  _const_spec((1, DH)), _const_spec((NG, 128, 128)), _const_spec((1, PW)), _const_spec((D, D))]
        + [ANY] * ns,
        out_specs=[rev(GW), rev(GW), rev(GW), rev(GW), rev(PW), rev(128),
                   fix(8, D), fix(D, D), fix(NG, 128, 128), fix(8, PW), fix(8, DH)] + [ANY] * ns,
        out_shape=[shp(GW), shp(GW), shp(GW), shp(GW), shp(PW), shp(128),
                   jax.ShapeDtypeStruct((8, D), F32), jax.ShapeDtypeStruct((D, D), BF),
                   jax.ShapeDtypeStruct((NG, 128, 128), F32), jax.ShapeDtypeStruct((8, PW), F32),
                   jax.ShapeDtypeStruct((8, DH), F32)] + [jax.ShapeDtypeStruct(t.shape, t.dtype) for t in scatter],
        scratch_shapes=[pltpu.VMEM((NH * DH, DH), F32), pltpu.VMEM((tm + HALO, PW), F32),
                        pltpu.VMEM((tm + HALO, PW), F32), pltpu.VMEM((tm, GW), F32), pltpu.VMEM((D, D), F32)]
        + [pltpu.VMEM((npb, R2, R2), F32)] * 5 + [pltpu.VMEM((npb, R2, DH), F32)] * 5
        + [pltpu.VMEM((npb, R2, 2 * DH), BF)] * 3 + [pltpu.VMEM((npb, R2, R2), BF), pltpu.VMEM((npb, R2, 1), F32)]
        + (_scatter_sems(ns) if ns else []),
        compiler_params=_cparams(),
    )(dxo, gate, mixed, cat, o, sall, qn, kn, v, z, p, p, bg, gc, gnorm, pool_w, pool_scale, w_out, *scatter)


def _mix_proj_bwd(dxo, dqn, dkn, dv, dz, dp, dbg, xq, ba, x, shift, scale, gain, winT, conv_w, alog, dtb):
    T = x.shape[0]
    tm = min(256, T)
    nt = T // tm
    W3 = 3 * GW

    def body(dxo_ref, dqn_ref, dkn_ref, dv_ref, dz_ref, dp_ref, dbg_ref, xq_ref, xh_ref, ba_ref, x_ref, sh_ref, sc_ref,
             gn_ref, w_ref, cw_ref, al_ref, dt_ref,
             dx_ref, dw_ref, dcw_ref, dal_ref, ddt_ref, dsh_ref, dsc_ref, dgn_ref,
             ext, dcext, dproj, dw_acc):
        i = pl.program_id(0)
        ti = nt - 1 - i

        @pl.when(i == 0)
        def _():
            dcext[pl.ds(tm, 8), :] = jnp.zeros((8, W3), F32)
            dw_acc[...] = jnp.zeros((DINP, D), F32)
            dcw_ref[...] = jnp.zeros((4, 8, W3), F32)
            dal_ref[...] = jnp.zeros((8, 128), F32)
            ddt_ref[...] = jnp.zeros((8, 128), F32)
            dsh_ref[...] = jnp.zeros((8, D), F32)
            dsc_ref[...] = jnp.zeros((8, D), F32)
            dgn_ref[...] = jnp.zeros((8, D), F32)

        ext[pl.ds(0, 8), :] = jnp.where(ti == 0, 0.0, xh_ref[...])
        ext[pl.ds(8, tm), :] = xq_ref[...]
        c, sg = _conv_silu(ext, cw_ref, tm)
        qt = c * sg
        dsilu = sg * (1.0 + c * (1.0 - sg))
        for hd in range(NH):
            for part, dref, mult in ((0, dqn_ref, DH ** -0.5), (1, dkn_ref, 1.0)):
                cols = slice(part * GW + hd * DH, part * GW + (hd + 1) * DH)
                xh = qt[:, cols]
                rr = lax.rsqrt(jnp.sum(xh * xh, axis=-1, keepdims=True) + EPS)
                unit = xh * rr
                du = dref[:, hd * DH:(hd + 1) * DH] * mult
                dxh = rr * (du - unit * jnp.sum(du * unit, axis=-1, keepdims=True))
                dcext[pl.ds(0, tm), cols] = dxh * dsilu[:, cols]
        dcext[pl.ds(0, tm), 2 * GW:W3] = dv_ref[...] * dsilu[:, 2 * GW:W3]
        dc = dcext[pl.ds(0, tm), :]
        dxq = jnp.zeros((tm, W3), F32)
        for j in range(4):
            dcw_ref[j] += _sum8(dc * ext[pl.ds(5 + j, tm), :])
            dxq = dxq + cw_ref[j:j + 1, :] * dcext[pl.ds(3 - j, tm), :]
        dcext[pl.ds(tm, 8), :] = dc[0:8]
        lane = lax.broadcasted_iota(jnp.int32, (tm, 128), 1)
        bav = ba_ref[...]
        beta, g, sarg = _gates(bav, al_ref[...], dt_ref[...], lane)
        dbg_v = dbg_ref[...]
        is_g = (lane >= NH) & (lane < 2 * NH)
        dbraw = jnp.where(lane < NH, dbg_v * beta * (1.0 - beta), 0.0)
        daraw = jnp.where(is_g, dbg_v * (-jnp.exp(al_ref[...])) * sarg, 0.0)
        dal_ref[...] += _sum8(jnp.where(is_g, dbg_v * g, 0.0))
        ddt_ref[...] += _sum8(daraw)
        dproj[:, 0:W3] = dxq.astype(BF)
        dproj[:, W3:W3 + GW] = dz_ref[...].astype(BF)
        dproj[:, W3 + GW:W3 + GW + PW] = dp_ref[...].astype(BF)
        dproj[:, W3 + GW + PW:DINP] = (dbraw + daraw).astype(BF)
        gain_v, scale_v = gn_ref[...], sc_ref[...]
        n, r, y, h = _norm_mod_fwd(x_ref[...], gain_v, sh_ref[...], scale_v)
        dpj = dproj[...]
        dh = _nn(dpj, w_ref[...])
        dw_acc[...] += _tn(dpj, h.astype(BF))
        dxn, dsh, dsc, dgn = _norm_mod_bwd(dh, n, r, y, gain_v, scale_v)
        dx_ref[...] = dxo_ref[...] + dxn
        dsh_ref[...] += dsh
        dsc_ref[...] += dsc
        dgn_ref[...] += dgn

        @pl.when(i == nt - 1)
        def _():
            dw_ref[...] = dw_acc[...].astype(BF)

    rev = lambda w: pl.BlockSpec((tm, w), lambda i: (nt - 1 - i, 0))
    halo = pl.BlockSpec((8, W3), lambda i: (jnp.maximum((nt - 1 - i) * (tm // 8) - 1, 0), 0))
    fix = lambda *s: pl.BlockSpec(s, lambda i: (0,) * len(s))
    vec = _const_spec((1, D))
    return pl.pallas_call(
        body, grid=(nt,), name="mix_proj_bwd",
        in_specs=[rev(D), rev(GW), rev(GW), rev(GW), rev(GW), rev(PW), rev(128), rev(W3), halo, rev(128), rev(D),
                  vec, vec, vec, _const_spec((DINP, D)), _const_spec((4, W3)), _const_spec((1, 128)),
                  _const_spec((1, 128))],
        out_specs=[rev(D), fix(DINP, D), fix(4, 8, W3), fix(8, 128), fix(8, 128), fix(8, D), fix(8, D), fix(8, D)],
        out_shape=[jax.ShapeDtypeStruct((T, D), F32), jax.ShapeDtypeStruct((DINP, D), BF),
                   jax.ShapeDtypeStruct((4, 8, W3), F32), jax.ShapeDtypeStruct((8, 128), F32),
                   jax.ShapeDtypeStruct((8, 128), F32), jax.ShapeDtypeStruct((8, D), F32),
                   jax.ShapeDtypeStruct((8, D), F32), jax.ShapeDtypeStruct((8, D), F32)],
        scratch_shapes=[pltpu.VMEM((tm + 8, W3), F32), pltpu.VMEM((tm + 8, W3), F32), pltpu.VMEM((tm, DINP), BF),
                        pltpu.VMEM((DINP, D), F32)],
        compiler_params=_cparams(),
    )(dxo, dqn, dkn, dv, dz, dp, dbg, xq, xq, ba, x, shift, scale, gain, winT, conv_w, alog, dtb)


MESH = pl.DeviceIdType.MESH
CHIP_RELS = ((1, 0), (0, 1), (1, 1))
DEV_RELS = tuple((dx, dy, dc) for dx in (0, 1) for dy in (0, 1) for dc in (0, 1) if (dx, dy, dc) != (0, 0, 0))
NMOD = 9
ADA_SH = NMOD * D // 4
WIN_SH = DIN // 4
WIN_PAD = 672
MSG_ROWS = 16
ANY = pl.BlockSpec(memory_space=pl.ANY)
VM = pl.BlockSpec(memory_space=pltpu.VMEM)


def _place():
    x, y, c = lax.axis_index("x"), lax.axis_index("y"), lax.axis_index("c")
    return x, y, c


class _SplitGather:
    N_SEMS = (3, 3, 3, 3, 1)

    def __init__(self, src, dst, ici_s, ici_r, d2d_s, d2d_r, lsem):
        self.src, self.dst = src, dst
        self.sems = (ici_s, ici_r, d2d_s, d2d_r)
        self.x, self.y, self.c = _place()
        self.myj = 2 * self.x + self.y
        half = src.shape[0] // 2
        self.mine = pl.ds(pl.multiple_of(self.c * half, 16), half)
        self.other = pl.ds(pl.multiple_of((1 - self.c) * half, 16), half)
        self.local = pltpu.make_async_copy(src, dst.at[self.myj], lsem.at[0])

    def _ici(self, k, slot):
        dx, dy = CHIP_RELS[k]
        return pltpu.make_async_remote_copy(
            src_ref=self.src.at[self.mine], dst_ref=self.dst.at[slot, self.mine], send_sem=self.sems[0].at[k],
            recv_sem=self.sems[1].at[k], device_id=(self.x ^ dx, self.y ^ dy, self.c), device_id_type=MESH)

    def _d2d(self, k, rows):
        dx, dy = CHIP_RELS[k]
        blk = self.dst.at[2 * (self.x ^ dx) + (self.y ^ dy), rows]
        return pltpu.make_async_remote_copy(
            src_ref=blk, dst_ref=blk, send_sem=self.sems[2].at[k], recv_sem=self.sems[3].at[k],
            device_id=(self.x, self.y, 1 - self.c), device_id_type=MESH)

    def start(self):
        self.local.start()
        for k in range(3):
            self._ici(k, self.myj).start()

    def forward(self):
        for k, (dx, dy) in enumerate(CHIP_RELS):
            self._ici(k, 2 * (self.x ^ dx) + (self.y ^ dy)).wait_recv()
            self._d2d(k, self.mine).start()

    def finish(self):
        for k in range(3):
            self._d2d(k, self.other).wait_recv()
        for k in range(3):
            self._d2d(k, self.mine).wait_send()
            self._ici(k, self.myj).wait_send()
        self.local.wait()


class _Scatter:
    def __init__(self, ins, outs, send, recv, lsem):
        self.ins, self.outs, self.send, self.recv, self.lsem = ins, outs, send, recv, lsem
        self.x, self.y, self.c = _place()
        self.myj = 2 * self.x + self.y

    def _copy(self, a, k, landing):
        dx, dy = CHIP_RELS[k]
        pj = 2 * (self.x ^ dx) + (self.y ^ dy)
        return pltpu.make_async_remote_copy(
            src_ref=self.ins[a].at[pj], dst_ref=self.outs[a].at[pj if landing else self.myj],
            send_sem=self.send.at[a, k], recv_sem=self.recv.at[a, k],
            device_id=(self.x ^ dx, self.y ^ dy, self.c), device_id_type=MESH)

    def _local(self, a):
        return pltpu.make_async_copy(self.ins[a].at[self.myj], self.outs[a].at[self.myj], self.lsem.at[a])

    def start(self):
        for a in range(len(self.ins)):
            self._local(a).start()
            for k in range(3):
                self._copy(a, k, False).start()

    def finish(self):
        for a in range(len(self.ins)):
            for k in range(3):
                self._copy(a, k, True).wait_recv()
            for k in range(3):
                self._copy(a, k, False).wait_send()
            self._local(a).wait()


def _scatter_sems(n):
    return [pltpu.SemaphoreType.DMA((n, 3)), pltpu.SemaphoreType.DMA((n, 3)), pltpu.SemaphoreType.DMA((n,))]


def _gather_sems():
    return [pltpu.SemaphoreType.DMA((k,)) for k in _SplitGather.N_SEMS]


def _comm_begin(i, plan, last):
    @pl.when(i == 0)
    def _():
        plan.start()

    if hasattr(plan, "forward"):
        @pl.when(i == max(last - 3, 0))
        def _():
            plan.forward()


def _comm_end(i, plan, last):
    @pl.when(i == last)
    def _():
        plan.finish()


def _ada_exchange(msg, w_ada, b_ada, wblock):
    def body(msg_ref, w_ref, b_ref, wb_ref, all_ref, mod_ref, wg_ref, modp, send1, recv1, send2, recv2, lsem, *gsems):
        x, y, c = _place()
        me = 4 * x + 2 * y + c
        own = pltpu.make_async_copy(msg_ref, all_ref.at[me], lsem.at[0])
        own.start()

        def gather(k, rel, slot):
            dx, dy, dc = rel
            return pltpu.make_async_remote_copy(
                src_ref=msg_ref, dst_ref=all_ref.at[slot], send_sem=send1.at[k], recv_sem=recv1.at[k],
                device_id=(x ^ dx, y ^ dy, c ^ dc), device_id_type=MESH)

        for k, rel in enumerate(DEV_RELS):
            gather(k, rel, me).start()
        for k, (dx, dy, dc) in enumerate(DEV_RELS):
            gather(k, (dx, dy, dc), 4 * (x ^ dx) + 2 * (y ^ dy) + (c ^ dc)).wait_recv()
        for k, rel in enumerate(DEV_RELS):
            gather(k, rel, me).wait_send()
        own.wait()
        wgather = _SplitGather(wb_ref, wg_ref, *gsems)
        wgather.start()

        for d in range(8):
            cv = all_ref[d, 0:8, :]
            act = cv * _sigmoid(cv)
            modp[d] = _nn(act, w_ref[...], precision=HI) + b_ref[...]

        myj = 2 * x + y
        keep = pltpu.make_async_copy(modp.at[me], mod_ref.at[myj], lsem.at[1])
        keep.start()

        def scatter(k, rel):
            dx, dy = rel
            return pltpu.make_async_remote_copy(
                src_ref=modp.at[4 * (x ^ dx) + 2 * (y ^ dy) + c], dst_ref=mod_ref.at[myj],
                send_sem=send2.at[k], recv_sem=recv2.at[k], device_id=(x ^ dx, y ^ dy, c), device_id_type=MESH)

        def landed(k, rel):
            dx, dy = rel
            return pltpu.make_async_remote_copy(
                src_ref=modp.at[me], dst_ref=mod_ref.at[2 * (x ^ dx) + (y ^ dy)],
                send_sem=send2.at[k], recv_sem=recv2.at[k], device_id=(x ^ dx, y ^ dy, c), device_id_type=MESH)

        for k, rel in enumerate(CHIP_RELS):
            scatter(k, rel).start()
        for k, rel in enumerate(CHIP_RELS):
            landed(k, rel).wait_recv()
        for k, rel in enumerate(CHIP_RELS):
            scatter(k, rel).wait_send()
        keep.wait()
        wgather.forward()
        wgather.finish()

    return pl.pallas_call(
        body, name="ada_exchange", in_specs=[VM, VM, VM, ANY], out_specs=[VM, VM, ANY],
        out_shape=[jax.ShapeDtypeStruct((8, MSG_ROWS, D), F32), jax.ShapeDtypeStruct((4, 8, ADA_SH), F32),
                   jax.ShapeDtypeStruct((4,) + wblock.shape, wblock.dtype)],
        scratch_shapes=[pltpu.VMEM((8, 8, ADA_SH), F32), pltpu.SemaphoreType.DMA((7,)), pltpu.SemaphoreType.DMA((7,)),
                        pltpu.SemaphoreType.DMA((3,)), pltpu.SemaphoreType.DMA((3,)), pltpu.SemaphoreType.DMA((2,))]
        + _gather_sems(),
        compiler_params=pltpu.CompilerParams(vmem_limit_bytes=VMEM_LIMIT),
    )(msg, w_ada, b_ada, wblock)


def _chip_exchange(parts, name):
    n = len(parts)

    def body(*refs):
        plan = _Scatter(refs[:n], refs[n:2 * n], *refs[2 * n:])
        plan.start()
        plan.finish()

    return pl.pallas_call(
        body, name=name, in_specs=[ANY] * n, out_specs=[ANY] * n,
        out_shape=[jax.ShapeDtypeStruct(p.shape, p.dtype) for p in parts], scratch_shapes=_scatter_sems(n),
    )(*parts)


class _AllGather:
    def __init__(self, ins, outs, send, recv, lsem):
        self.ins, self.outs, self.send, self.recv, self.lsem = ins, outs, send, recv, lsem
        self.x, self.y, self.c = _place()
        self.me = 4 * self.x + 2 * self.y + self.c

    def _copy(self, a, k, landing):
        dx, dy, dc = DEV_RELS[k]
        peer = 4 * (self.x ^ dx) + 2 * (self.y ^ dy) + (self.c ^ dc)
        return pltpu.make_async_remote_copy(
            src_ref=self.ins[a], dst_ref=self.outs[a].at[peer if landing else self.me],
            send_sem=self.send.at[a, k], recv_sem=self.recv.at[a, k],
            device_id=(self.x ^ dx, self.y ^ dy, self.c ^ dc), device_id_type=MESH)

    def _local(self, a):
        return pltpu.make_async_copy(self.ins[a], self.outs[a].at[self.me], self.lsem.at[a])

    def start(self):
        for a in range(len(self.ins)):
            self._local(a).start()
            for k in range(7):
                self._copy(a, k, False).start()

    def finish(self):
        for a in range(len(self.ins)):
            for k in range(7):
                self._copy(a, k, True).wait_recv()
            for k in range(7):
                self._copy(a, k, False).wait_send()
            self._local(a).wait()


def _allgather_sems(n):
    return [pltpu.SemaphoreType.DMA((n, 7)), pltpu.SemaphoreType.DMA((n, 7)), pltpu.SemaphoreType.DMA((n,))]


class _Plans:
    def __init__(self, plans):
        self.plans = plans

    def start(self):
        for p in self.plans:
            p.start()

    def finish(self):
        for p in self.plans:
            p.finish()


def _pair_exchange(parts, name):
    n = len(parts)

    def body(*refs):
        ins, outs = refs[:n], refs[n:2 * n]
        send, recv = refs[2 * n:]
        x, y, c = _place()
        cps = [pltpu.make_async_remote_copy(
            src_ref=ins[a], dst_ref=outs[a], send_sem=send.at[a], recv_sem=recv.at[a],
            device_id=(x, y, 1 - c), device_id_type=MESH) for a in range(n)]
        for cp in cps:
            cp.start()
        for cp in cps:
            cp.wait_recv()
        for cp in cps:
            cp.wait_send()

    shapes = [jax.ShapeDtypeStruct(p.shape, p.dtype) for p in parts]
    return pl.pallas_call(
        body, name=name, in_specs=[ANY] * n, out_specs=[ANY] * n, out_shape=shapes,
        scratch_shapes=[pltpu.SemaphoreType.DMA((n,)), pltpu.SemaphoreType.DMA((n,))],
    )(*parts)


def _row_tile(rows, cap):
    best = rows
    for t in range(8, min(cap, rows) + 1, 8):
        if rows % t == 0:
            best = t
    return best if rows % 8 == 0 else rows


def _sum_slots(parts, name):
    n, rows, width = parts.shape
    tr = _row_tile(rows, 352)

    def body(p_ref, o_ref):
        acc = p_ref[0].astype(F32)
        for j in range(1, n):
            acc = acc + p_ref[j].astype(F32)
        o_ref[...] = acc

    return pl.pallas_call(
        body, grid=(rows // tr,), name=name,
        in_specs=[pl.BlockSpec((n, tr, width), lambda i: (0, i, 0))],
        out_specs=pl.BlockSpec((tr, width), lambda i: (i, 0)),
        out_shape=jax.ShapeDtypeStruct((rows, width), F32),
        compiler_params=_cparams(),
    )(parts)


def _adamw_math(g, w, m, v):
    m_new = ADAM_B1 * m + (1.0 - ADAM_B1) * g
    v_new = ADAM_B2 * v + (1.0 - ADAM_B2) * (g * g)
    m_hat = m_new / (1.0 - ADAM_B1 ** ADAM_STEP)
    v_hat = v_new / (1.0 - ADAM_B2 ** ADAM_STEP)
    delta = -ADAM_LR * (m_hat / (jnp.sqrt(v_hat) + ADAM_EPS) + ADAM_WD * w)
    return delta, m_new, v_new


def _adamw(grads, w, m, v, name):
    rows, width = w.shape
    tr = _row_tile(rows, 256 if width <= 1024 else 128)
    ng = len(grads)

    def body(*refs):
        g = refs[0][...]
        for r in refs[1:ng]:
            g = g + r[...]
        w_ref, m_ref, v_ref, g_out, d_out, m_out, v_out = refs[ng:]
        delta, m_new, v_new = _adamw_math(g, w_ref[...], m_ref[...], v_ref[...])
        g_out[...] = g
        d_out[...] = delta
        m_out[...] = m_new
        v_out[...] = v_new

    if rows % 8 == 0 or width % 512:
        blk, steps = pl.BlockSpec((tr, width), lambda i: (i, 0)), rows // tr
    else:
        blk, steps = pl.BlockSpec((rows, 256), lambda i: (0, i)), width // 256
    return pl.pallas_call(
        body, grid=(steps,), name=name,
        in_specs=[blk] * (ng + 3), out_specs=[blk] * 4,
        out_shape=[jax.ShapeDtypeStruct((rows, width), F32)] * 4,
        compiler_params=_cparams(),
    )(*grads, w, m, v)


def _adamw_ada(msgs, dmods, w, m, v):
    rows, width = w.shape
    tr = 128

    def body(c_ref, dm_ref, w_ref, m_ref, v_ref, g_out, d_out, m_out, v_out):
        cv = jnp.concatenate([c_ref[d, 0:1, :] for d in range(8)], axis=0)
        act = cv * _sigmoid(cv)
        g = _tn(act, dm_ref[...], precision=HI)
        delta, m_new, v_new = _adamw_math(g, w_ref[...], m_ref[...], v_ref[...])
        g_out[...] = g
        d_out[...] = delta
        m_out[...] = m_new
        v_out[...] = v_new

    blk = pl.BlockSpec((tr, width), lambda i: (i, 0))
    return pl.pallas_call(
        body, grid=(rows // tr,), name="adamw_w_ada",
        in_specs=[pl.BlockSpec((8, MSG_ROWS, tr), lambda i: (0, 0, i)), pl.BlockSpec((8, width), lambda i: (0, 0)),
                  blk, blk, blk],
        out_specs=[blk] * 4, out_shape=[jax.ShapeDtypeStruct((rows, width), F32)] * 4,
        compiler_params=_cparams(),
    )(msgs, dmods, w, m, v)


def _adamw_small(parts, w, m, v, name):
    n, rows, width = parts.shape

    def body(p_ref, w_ref, m_ref, v_ref, g_out, d_out, m_out, v_out):
        g = p_ref[0]
        for j in range(1, n):
            g = g + p_ref[j]
        delta, m_new, v_new = _adamw_math(g, w_ref[...], m_ref[...], v_ref[...])
        g_out[...] = g
        d_out[...] = delta
        m_out[...] = m_new
        v_out[...] = v_new

    return pl.pallas_call(
        body, name=name, in_specs=[VM] * 4, out_specs=[VM] * 4,
        out_shape=[jax.ShapeDtypeStruct((rows, width), F32)] * 4,
        compiler_params=pltpu.CompilerParams(vmem_limit_bytes=VMEM_LIMIT),
    )(parts, w, m, v)


SMALL_ROWS = 24


def _pad_row(vec, width=D):
    vec = vec.reshape(1, -1)
    return jnp.pad(vec, ((0, 0), (0, width - vec.shape[1])))


def _lanes_4_7(vec4):
    return jnp.zeros((1, 128), F32).at[0, NH:2 * NH].set(vec4.reshape(NH))


def kernel(x, c, w_ada, b_ada, norm_ffn1, ffn1_gate, ffn1_up, ffn1_down, norm_mix, w_in, conv_w, a_log, dt_bias, gdn_norm, pool_w, pool_scale, w_out, norm_ffn2, ffn2_gate, ffn2_up, ffn2_down, final_norm, loss_target, m_w_ada, m_b_ada, m_norm_ffn1, m_ffn1_gate, m_ffn1_up, m_ffn1_down, m_norm_mix, m_w_in, m_conv_w, m_a_log, m_dt_bias, m_gdn_norm, m_pool_w, m_pool_scale, m_w_out, m_norm_ffn2, m_ffn2_gate, m_ffn2_up, m_ffn2_down, m_final_norm, v_w_ada, v_b_ada, v_norm_ffn1, v_ffn1_gate, v_ffn1_up, v_ffn1_down, v_norm_mix, v_w_in, v_conv_w, v_a_log, v_dt_bias, v_gdn_norm, v_pool_w, v_pool_scale, v_w_out, v_norm_ffn2, v_ffn2_gate, v_ffn2_up, v_ffn2_down, v_final_norm):
    xs = x[0]
    tgt = loss_target[0]
    chip = 2 * lax.axis_index("x") + lax.axis_index("y")
    me = 2 * chip + lax.axis_index("c")

    fsh = FF // 4
    block_a = jnp.concatenate([ffn1_gate[0].T, ffn1_up[0].T, ffn1_down[0]], axis=0).astype(BF)
    block_b = jnp.concatenate([ffn2_gate[0].T, ffn2_up[0].T, ffn2_down[0], w_out[0],
                               jnp.pad(w_in[0].T, ((0, WIN_PAD - WIN_SH), (0, 0)))], axis=0).astype(BF)

    msg = jnp.concatenate([jnp.broadcast_to(c, (8, D)), jnp.pad(conv_w[0], ((0, 0), (0, D - 3 * GW // 4))),
                           jnp.zeros((MSG_ROWS - 12, D), F32)], axis=0)
    b_sh = lax.dynamic_slice(b_ada, (0, chip * ADA_SH), (1, ADA_SH))
    msgs, mod4, gath_a = _ada_exchange(msg, w_ada[0], b_sh, block_a)
    mod = mod4[:, 0, :].reshape(NMOD, D)
    mrow = [mod[i:i + 1] for i in range(NMOD)]
    conv_full = jnp.concatenate([msgs[2 * j, 8:12, :3 * GW // 4] for j in range(4)], axis=1)
    alog, dtb = _lanes_4_7(a_log), _lanes_4_7(dt_bias)
    gnm = gdn_norm.reshape(1, DH)
    pwb = pool_w[0].astype(BF)
    psc = pool_scale.reshape(1, PW)
    fin = final_norm.reshape(1, D)

    x1, f1, a1, b1, s1, gath_b = _ffn_fwd(xs, mrow[0], mrow[1], mrow[2], norm_ffn1, gath_a, 0, "ffn1_fwd", block_b)
    wo = gath_b[:, 3 * fsh:3 * fsh + D // 4, :].reshape(D, D)
    win_nat = gath_b[:, 3 * fsh + D // 4:3 * fsh + D // 4 + WIN_SH, :].reshape(DIN, D)
    winT = jnp.concatenate([win_nat[:4 * GW], win_nat[4 * GW + 2 * NH:], win_nat[4 * GW:4 * GW + 2 * NH],
                            jnp.zeros((128 - 2 * NH, D), BF)], axis=0)
    xq, ba, qn, kn, vv, z, pp, bg, gc = _mix_proj(x1, mrow[3], mrow[4], norm_mix, winT, conv_full, alog, dtb)
    x2, mixed, cat, o, sall = _mix_core(x1, mrow[5], qn, kn, vv, z, pp, bg, gc, gnm, pwb, psc, wo)
    x3, f2, a2, b2, s2 = _ffn_fwd(x2, mrow[6], mrow[7], mrow[8], norm_ffn2, gath_b, 0, "ffn2_fwd")
    lpart, dx3, dfin = _loss_head(x3, tgt, fin)
    loss = lax.psum(jnp.sum(lpart), ("x", "y", "c"))

    slots = lambda t: t.reshape(4, t.shape[0] // 4, D)
    dx2, da2, db2, h2, df2, dsh3, dsc3, dgt3, dn3 = _ffn_dgrad(
        dx3, x2, f2, a2, b2, mrow[6], mrow[7], mrow[8], norm_ffn2, gath_b, 0, "ffn2_dgrad")
    gg2, gu2, gd2 = _ffn_wgrad(da2, db2, s2, h2, df2, "ffn2_wgrad")
    dqn, dkn, dvv, dz, dpp, dbg, dgt2, dwo, dpw, dps, dgnm, *landed2 = _mix_core_bwd(
        dx2, mrow[5], mixed, cat, o, sall, qn, kn, vv, z, pp, bg, gc, gnm, pwb, psc, wo,
        scatter=[slots(gg2), slots(gu2), slots(gd2)])
    dx1, dwin, dcw, dal, ddt, dsh2, dsc2, dn2 = _mix_proj_bwd(
        dx2, dqn, dkn, dvv, dz, dpp, dbg, xq, ba, x1, mrow[3], mrow[4], norm_mix, winT, conv_full, alog, dtb)
    dwin_nat = jnp.concatenate([dwin[:4 * GW], dwin[4 * GW + PW:4 * GW + PW + 2 * NH], dwin[4 * GW:4 * GW + PW]], axis=0)
    dwin_sl = jnp.pad(dwin_nat.reshape(4, WIN_SH, D), ((0, 0), (0, WIN_PAD - WIN_SH), (0, 0)))
    dx0, da1, db1, h1, df1, dsh1, dsc1, dgt1, dn1 = _ffn_dgrad(
        dx1, xs, f1, a1, b1, mrow[0], mrow[1], mrow[2], norm_ffn1, gath_a, 0, "ffn1_dgrad")
    red = lambda t: jnp.sum(t, axis=0, keepdims=True)
    small = jnp.concatenate(
        [red(dn1), red(dn2), red(dn3), red(dfin),
         red(dsh1), red(dsc1), red(dgt1), red(dsh2), red(dsc2), red(dgt2), red(dsh3), red(dsc3), red(dgt3),
         _pad_row(red(dps)), _pad_row(red(dgnm)), _pad_row(red(dal)), _pad_row(red(ddt)),
         jnp.sum(dcw, axis=1).reshape(6, D), jnp.zeros((1, D), F32)], axis=0)
    gg1, gu1, gd1, land_wo, land_win, small_all, dpw_all = _ffn_wgrad(
        da1, db1, s1, h1, df1, "ffn1_wgrad", scatter=[slots(dwo), dwin_sl],
        allgather=[small, dpw.reshape(NG * 128, 128)])
    landed1 = _chip_exchange([slots(gg1), slots(gu1), slots(gd1)], "grad_scatter")

    landed = list(landed1) + list(landed2) + [land_wo, land_win]
    names = ("g1", "u1", "d1", "g2", "u2", "d2", "wo", "win")
    psum = [_sum_slots(t, "sum_" + nm) for t, nm in zip(landed, names)]
    qsum = _pair_exchange(psum, "grad_pair")

    def adamw_t(i, w, m, v, name, rows):
        res = _adamw([psum[i][:rows], qsum[i][:rows]], w[0].T, m[0].T, v[0].T, name)
        return [t.T for t in res]

    upd = {}
    upd["ffn1_gate"] = adamw_t(0, ffn1_gate, m_ffn1_gate, v_ffn1_gate, "adamw_g1", fsh)
    upd["ffn1_up"] = adamw_t(1, ffn1_up, m_ffn1_up, v_ffn1_up, "adamw_u1", fsh)
    upd["ffn1_down"] = _adamw([psum[2], qsum[2]], ffn1_down[0], m_ffn1_down[0], v_ffn1_down[0], "adamw_d1")
    upd["ffn2_gate"] = adamw_t(3, ffn2_gate, m_ffn2_gate, v_ffn2_gate, "adamw_g2", fsh)
    upd["ffn2_up"] = adamw_t(4, ffn2_up, m_ffn2_up, v_ffn2_up, "adamw_u2", fsh)
    upd["ffn2_down"] = _adamw([psum[5], qsum[5]], ffn2_down[0], m_ffn2_down[0], v_ffn2_down[0], "adamw_d2")
    upd["w_out"] = _adamw([psum[6], qsum[6]], w_out[0], m_w_out[0], v_w_out[0], "adamw_wo")
    upd["w_in"] = adamw_t(7, w_in, m_w_in, v_w_in, "adamw_win", WIN_SH)
    dmods = lax.dynamic_slice(small_all[:, 4:4 + NMOD, :].reshape(8, NMOD * D), (0, chip * ADA_SH), (8, ADA_SH))
    upd["w_ada"] = _adamw_ada(msgs, dmods, w_ada[0], m_w_ada[0], v_w_ada[0])

    def pack_small(nf1, nmx, nf2, fn, bada, psc_, gn_, al_, dt_):
        return jnp.concatenate(
            [nf1.reshape(1, D), nmx.reshape(1, D), nf2.reshape(1, D), fn.reshape(1, D), bada.reshape(NMOD, D),
             _pad_row(psc_), _pad_row(gn_), _pad_row(_lanes_4_7(al_)), _pad_row(_lanes_4_7(dt_)),
             jnp.zeros((7, D), F32)], axis=0)

    ws = pack_small(norm_ffn1, norm_mix, norm_ffn2, final_norm, b_ada, pool_scale, gdn_norm, a_log, dt_bias)
    ms = pack_small(m_norm_ffn1, m_norm_mix, m_norm_ffn2, m_final_norm, m_b_ada, m_pool_scale, m_gdn_norm, m_a_log, m_dt_bias)
    vs = pack_small(v_norm_ffn1, v_norm_mix, v_norm_ffn2, v_final_norm, v_b_ada, v_pool_scale, v_gdn_norm, v_a_log, v_dt_bias)
    sm = _adamw_small(small_all, ws, ms, vs, "adamw_small")
    pw2 = lambda t: t.reshape(NG * 128, 128)
    upd_pw = _adamw_small(dpw_all, pw2(pool_w), pw2(m_pool_w), pw2(v_pool_w), "adamw_pool_w")
    csh = 3 * GW // 4
    gconv = lax.dynamic_slice(sm[0][17:23].reshape(4, 3 * GW), (0, chip * csh), (4, csh))
    upd["conv_w"] = _adamw([gconv], conv_w[0], m_conv_w[0], v_conv_w[0], "adamw_conv")

    def small_out(k):
        t = sm[k]
        return {
            "norm_ffn1": t[0:1], "norm_mix": t[1:2], "norm_ffn2": t[2:3], "final_norm": t[3],
            "b_ada": t[4:4 + NMOD].reshape(1, NMOD * D), "pool_scale": t[13:14, :PW], "gdn_norm": t[14:15, :DH],
            "a_log": t[15:16, NH:2 * NH], "dt_bias": t[16:17, NH:2 * NH],
        }

    order = ["w_ada", "b_ada", "norm_ffn1", "ffn1_gate", "ffn1_up", "ffn1_down", "norm_mix", "w_in", "conv_w", "a_log",
             "dt_bias", "gdn_norm", "pool_w", "pool_scale", "w_out", "norm_ffn2", "ffn2_gate", "ffn2_up", "ffn2_down",
             "final_norm"]
    outs = [loss, dx0[None]]
    for k in range(4):
        smk = small_out(k)
        for nm in order:
            if nm in upd:
                outs.append(upd[nm][k][None])
            elif nm == "pool_w":
                outs.append(upd_pw[k].reshape(1, NG, 128, 128))
            else:
                outs.append(smk[nm])
    return tuple(outs)
```

```python
import functools

import jax
import jax.numpy as jnp
from jax import lax
from jax.experimental import pallas as pl
from jax.experimental.pallas import tpu as pltpu

F32 = jnp.float32
BF = jnp.bfloat16

D = 1024
FF = 2816
FH = FF // 2
NH = 4
DH = 128
GW = NH * DH
CH = 64
PW = 512
NG = 4
POOL_WINDOWS = (2, 4, 8, 16)
HALO = 16
DIN = 4 * GW + 2 * NH + PW
DINP = 3 * GW + GW + PW + 128
EPS = 1e-6
ADAM_LR, ADAM_B1, ADAM_B2, ADAM_EPS, ADAM_WD, ADAM_STEP = 0.001, 0.9, 0.999, 1e-08, 0.01, 10

VMEM_LIMIT = 56 * 1024 * 1024

NT_DIMS = (((1,), (1,)), ((), ()))
TN_DIMS = (((0,), (0,)), ((), ()))
HI = lax.Precision.HIGHEST


def _nt(a, b, **kw):
    return lax.dot_general(a, b, NT_DIMS, preferred_element_type=F32, **kw)


def _tn(a, b, **kw):
    return lax.dot_general(a, b, TN_DIMS, preferred_element_type=F32, **kw)


def _nn(a, b, **kw):
    return jnp.dot(a, b, preferred_element_type=F32, **kw)


def _cparams(sem=("arbitrary",), **kw):
    return pltpu.CompilerParams(dimension_semantics=sem, vmem_limit_bytes=VMEM_LIMIT, **kw)


def _const_spec(shape):
    nd = len(shape)
    return pl.BlockSpec(shape, lambda *_: (0,) * nd, pipeline_mode=pl.Buffered(1))


def _row_spec(tm, width):
    return pl.BlockSpec((tm, width), lambda i: (i, 0))


def _sum8(v):
    return jnp.sum(v.reshape(v.shape[0] // 8, 8, v.shape[1]), axis=0)


def _sigmoid(v):
    return 0.5 * jnp.tanh(0.5 * v) + 0.5


def _tile(T, cap=512):
    return min(cap, T)


def _norm_mod_fwd(xv, gain, shift, scale):
    r = lax.rsqrt(jnp.mean(xv * xv, axis=-1, keepdims=True) + EPS)
    n = xv * r
    y = n * gain
    return n, r, y, y * (1.0 + scale) + shift


def _norm_mod_bwd(dh, n, r, y, gain, scale):
    dy = dh * (1.0 + scale)
    dn = dy * gain
    dx = r * (dn - n * jnp.mean(dn * n, axis=-1, keepdims=True))
    return dx, _sum8(dh), _sum8(dh * y), _sum8(dy * n)


def _ffn_wspecs(k0):
    return [pl.BlockSpec((4, FF // 4, D), lambda i, k=k0 + n: (0, k, 0), pipeline_mode=pl.Buffered(1)) for n in range(3)]


def _half(w_ref, j):
    return w_ref[2 * j:2 * j + 2].reshape(FH, D)


def _ffn_fwd(x, shift, scale, gate, gain, wall, k0, name, gather_block=None):
    T = x.shape[0]
    tm = _tile(T)
    nt = T // tm
    comm = gather_block is not None

    def body(*refs):
        x_ref, sh_ref, sc_ref, gt_ref, gn_ref, wg_ref, wu_ref, wd_ref = refs[:8]
        if comm:
            xo_ref, f_ref, sa_ref, ga_ref, s_ref = refs[9:14]
            plan = _SplitGather(refs[8], refs[14], *refs[15:])
            _comm_begin(pl.program_id(0), plan, nt - 1)
        else:
            xo_ref, f_ref, sa_ref, ga_ref, s_ref = refs[8:13]
        xv = x_ref[...]
        _, _, _, h = _norm_mod_fwd(xv, gn_ref[...], sh_ref[...], sc_ref[...])
        hb = h.astype(BF)
        facc = jnp.zeros((tm, D), F32)
        for j in range(2):
            cols = slice(j * FH, (j + 1) * FH)
            a = _nt(hb, _half(wg_ref, j))
            b = _nt(hb, _half(wu_ref, j))
            sig = _sigmoid(a)
            sa = a * sig
            sa_ref[:, cols] = sa.astype(BF)
            ga_ref[:, cols] = (b * (sig * (1.0 + a * (1.0 - sig)))).astype(BF)
            s = (sa * b).astype(BF)
            s_ref[:, cols] = s
            facc = facc + _nn(s, _half(wd_ref, j))
        f_ref[...] = facc
        xo_ref[...] = xv + 0.5 * gt_ref[...] * facc
        if comm:
            _comm_end(pl.program_id(0), plan, nt - 1)

    vec = _const_spec((1, D))
    extra_in = [gather_block] if comm else []
    return pl.pallas_call(
        body, grid=(nt,), name=name,
        in_specs=[_row_spec(tm, D), vec, vec, vec, vec] + _ffn_wspecs(k0) + [ANY] * comm,
        out_specs=[_row_spec(tm, D), _row_spec(tm, D)] + [_row_spec(tm, FF)] * 3 + [ANY] * comm,
        out_shape=[jax.ShapeDtypeStruct((T, D), F32), jax.ShapeDtypeStruct((T, D), F32)]
        + [jax.ShapeDtypeStruct((T, FF), BF)] * 3
        + ([jax.ShapeDtypeStruct((4,) + gather_block.shape, gather_block.dtype)] if comm else []),
        scratch_shapes=_gather_sems() if comm else [],
        compiler_params=_cparams(),
    )(x, shift, scale, gate, gain, wall, wall, wall, *extra_in)


def _ffn_dgrad(dxo, x, f, sa, ga, shift, scale, gate, gain, wall, k0, name):
    T = x.shape[0]
    tm = _tile(T)
    nt = T // tm
    wspecs = _ffn_wspecs(k0)
    vec = _const_spec((1, D))
    acc = pl.BlockSpec((8, D), lambda i: (0, 0))
    accs = jax.ShapeDtypeStruct((8, D), F32)

    def body_a(dxo_ref, f_ref, sa_ref, ga_ref, gt_ref, wd_ref, da_ref, db_ref, df_ref, dgt_ref):
        i = pl.program_id(0)
        dxo_v = dxo_ref[...]
        dgate = _sum8(0.5 * f_ref[...] * dxo_v)
        dfb = (0.5 * gt_ref[...] * dxo_v).astype(BF)
        df_ref[...] = dfb
        for j in range(2):
            cols = slice(j * FH, (j + 1) * FH)
            ds = _nt(dfb, _half(wd_ref, j))
            da_ref[:, cols] = (ds * ga_ref[:, cols].astype(F32)).astype(BF)
            db_ref[:, cols] = (ds * sa_ref[:, cols].astype(F32)).astype(BF)

        @pl.when(i == 0)
        def _():
            dgt_ref[...] = dgate

        @pl.when(i > 0)
        def _():
            dgt_ref[...] += dgate

    wide = jax.ShapeDtypeStruct((T, FF), BF)
    da, db, df, dgt = pl.pallas_call(
        body_a, grid=(nt,), name=name + "_a",
        in_specs=[_row_spec(tm, D), _row_spec(tm, D), _row_spec(tm, FF), _row_spec(tm, FF), vec, wspecs[2]],
        out_specs=[_row_spec(tm, FF), _row_spec(tm, FF), _row_spec(tm, D), acc],
        out_shape=[wide, wide, jax.ShapeDtypeStruct((T, D), BF), accs],
        compiler_params=_cparams(),
    )(dxo, f, sa, ga, gate, wall)

    def body_b(dxo_ref, x_ref, da_ref, db_ref, sh_ref, sc_ref, gn_ref, wg_ref, wu_ref,
               dx_ref, h_ref, dsh_ref, dsc_ref, dgn_ref):
        i = pl.program_id(0)
        gain_v, scale_v = gn_ref[...], sc_ref[...]
        n, r, y, h = _norm_mod_fwd(x_ref[...], gain_v, sh_ref[...], scale_v)
        h_ref[...] = h.astype(BF)
        dh = _nn(da_ref[...], wg_ref[...].reshape(FF, D)) + _nn(db_ref[...], wu_ref[...].reshape(FF, D))
        dxn, dsh, dsc, dgn = _norm_mod_bwd(dh, n, r, y, gain_v, scale_v)
        dx_ref[...] = dxo_ref[...] + dxn

        @pl.when(i == 0)
        def _():
            dsh_ref[...] = dsh
            dsc_ref[...] = dsc
            dgn_ref[...] = dgn

        @pl.when(i > 0)
        def _():
            dsh_ref[...] += dsh
            dsc_ref[...] += dsc
            dgn_ref[...] += dgn

    dx, h, dsh, dsc, dgn = pl.pallas_call(
        body_b, grid=(nt,), name=name + "_b",
        in_specs=[_row_spec(tm, D), _row_spec(tm, D), _row_spec(tm, FF), _row_spec(tm, FF), vec, vec, vec,
                  wspecs[0], wspecs[1]],
        out_specs=[_row_spec(tm, D), _row_spec(tm, D), acc, acc, acc],
        out_shape=[jax.ShapeDtypeStruct((T, D), F32), jax.ShapeDtypeStruct((T, D), BF), accs, accs, accs],
        compiler_params=_cparams(),
    )(dxo, x, da, db, shift, scale, gain, wall, wall)
    return dx, da, db, h, df, dsh, dsc, dgt, dgn


def _ffn_wgrad(da, db, s, h, df, name, scatter=(), allgather=()):
    T = h.shape[0]
    tk = _tile(T, 512)
    nk = T // tk
    nj = 2
    groups = [g for g in (
        (list(scatter), _Scatter, _scatter_sems, lambda t: t.shape),
        (list(allgather), _AllGather, _allgather_sems, lambda t: (8,) + t.shape),
    ) if g[0]]
    extra = [t for g in groups for t in g[0]]
    ne = len(extra)

    def body(*refs):
        da_ref, db_ref, s_ref, h_ref, df_ref = refs[:5]
        og_ref, ou_ref, od_ref = refs[5 + ne:8 + ne]
        ag, au, ad = refs[8 + 2 * ne:11 + 2 * ne]
        k = pl.program_id(1)
        step = pl.program_id(0) * nk + k
        if ne:
            plans, at, sem_at = [], 0, 11 + 2 * ne
            for arrs, make, _, _ in groups:
                n = len(arrs)
                plans.append(make(refs[5 + at:5 + at + n], refs[8 + ne + at:8 + ne + at + n], *refs[sem_at:sem_at + 3]))
                at, sem_at = at + n, sem_at + 3
            plan = _Plans(plans)
            _comm_begin(step, plan, nj * nk - 1)

        @pl.when(k == 0)
        def _():
            for acc in (ag, au, ad):
                acc[...] = jnp.zeros((FH, D), F32)

        ag[...] += _tn(da_ref[...], h_ref[...])
        au[...] += _tn(db_ref[...], h_ref[...])
        ad[...] += _tn(s_ref[...], df_ref[...])

        @pl.when(k == nk - 1)
        def _():
            og_ref[...] = ag[...].astype(BF)
            ou_ref[...] = au[...].astype(BF)
            od_ref[...] = ad[...].astype(BF)

        if ne:
            _comm_end(step, plan, nj * nk - 1)

    colblk = pl.BlockSpec((tk, FH), lambda j, k: (k, j))
    rowblk = pl.BlockSpec((tk, D), lambda j, k: (k, 0))
    outblk = pl.BlockSpec((FH, D), lambda j, k: (j, 0))
    outs = jax.ShapeDtypeStruct((nj * FH, D), BF)
    return pl.pallas_call(
        body, grid=(nj, nk), name=name,
        in_specs=[colblk, colblk, colblk, rowblk, rowblk] + [ANY] * ne,
        out_specs=[outblk, outblk, outblk] + [ANY] * ne,
        out_shape=[outs, outs, outs] + [jax.ShapeDtypeStruct(shape(t), t.dtype) for g in groups for t, shape in
                                        ((t, g[3]) for t in g[0])],
        scratch_shapes=[pltpu.VMEM((FH, D), F32)] * 3 + [sm for g in groups for sm in g[2](len(g[0]))],
        compiler_params=_cparams(("arbitrary", "arbitrary")),
    )(da, db, s, h, df, *extra)


def _loss_head(x, target, gain):
    T = x.shape[0]
    tm = _tile(T)

    def body(x_ref, t_ref, gn_ref, ls_ref, dx_ref, dgn_ref):
        i = pl.program_id(0)
        xv = x_ref[...]
        gain_v = gn_ref[...]
        r = lax.rsqrt(jnp.mean(xv * xv, axis=-1, keepdims=True) + EPS)
        n = xv * r
        err = n * gain_v - t_ref[...]
        e2 = err * err
        part = e2[:, 0:128]
        for q in range(1, D // 128):
            part = part + e2[:, q * 128:(q + 1) * 128]
        lsum = _sum8(part) * (0.5 / D)
        dy = err * (1.0 / D)
        dn = dy * gain_v
        dx_ref[...] = r * (dn - n * jnp.mean(dn * n, axis=-1, keepdims=True))
        dgn = _sum8(dy * n)

        @pl.when(i == 0)
        def _():
            ls_ref[...] = lsum
            dgn_ref[...] = dgn

        @pl.when(i > 0)
        def _():
            ls_ref[...] += lsum
            dgn_ref[...] += dgn

    return pl.pallas_call(
        body, grid=(T // tm,), name="loss_head",
        in_specs=[_row_spec(tm, D), _row_spec(tm, D), _const_spec((1, D))],
        out_specs=[pl.BlockSpec((8, 128), lambda i: (0, 0)), _row_spec(tm, D), pl.BlockSpec((8, D), lambda i: (0, 0))],
        out_shape=[jax.ShapeDtypeStruct((8, 128), F32), jax.ShapeDtypeStruct((T, D), F32),
                   jax.ShapeDtypeStruct((8, D), F32)],
        compiler_params=_cparams(),
    )(x, target, gain)


def _seg_cumsum(v, row_in_chunk, reverse=False):
    n = v.shape[0]
    s = 1
    while s < CH:
        if reverse:
            moved = pltpu.roll(v, n - s, 0)
            ok = row_in_chunk < CH - s
        else:
            moved = pltpu.roll(v, s, 0)
            ok = row_in_chunk >= s
        v = v + jnp.where(ok, moved, 0.0)
        s *= 2
    return v


def _conv_silu(xq_ext_ref, cw_ref, tm):
    c = cw_ref[0:1, :] * xq_ext_ref[pl.ds(5, tm), :]
    for j in range(1, 4):
        c = c + cw_ref[j:j + 1, :] * xq_ext_ref[pl.ds(5 + j, tm), :]
    return c, _sigmoid(c)


def _gates(ba, alog, dtb, lane):
    beta = _sigmoid(ba)
    arg = ba + dtb
    softplus = jnp.maximum(arg, 0.0) + jnp.log(1.0 + jnp.exp(-jnp.abs(arg)))
    g = -jnp.exp(alog) * softplus
    return jnp.where(lane < NH, beta, 0.0), jnp.where((lane >= NH) & (lane < 2 * NH), g, 0.0), _sigmoid(arg)


def _mix_proj(x, shift, scale, gain, winT, conv_w, alog, dtb):
    T = x.shape[0]
    tm = _tile(T)

    def body(x_ref, sh_ref, sc_ref, gn_ref, w_ref, cw_ref, al_ref, dt_ref,
             xq_ref, ba_ref, qn_ref, kn_ref, v_ref, z_ref, p_ref, bg_ref, gc_ref, ext):
        i = pl.program_id(0)
        _, _, _, h = _norm_mod_fwd(x_ref[...], gn_ref[...], sh_ref[...], sc_ref[...])
        hb = h.astype(BF)

        @pl.when(i == 0)
        def _():
            ext[pl.ds(0, 8), :] = jnp.zeros((8, 3 * GW), F32)

        xq = _nt(hb, w_ref[pl.ds(0, 3 * GW), :])
        xq_ref[...] = xq
        ext[pl.ds(8, tm), :] = xq
        z_ref[...] = _nt(hb, w_ref[pl.ds(3 * GW, GW), :])
        p_ref[...] = _nt(hb, w_ref[pl.ds(4 * GW, PW), :])
        ba = _nt(hb, w_ref[pl.ds(4 * GW + PW, 128), :])
        ba_ref[...] = ba

        c, sg = _conv_silu(ext, cw_ref, tm)
        ext[pl.ds(0, 8), :] = ext[pl.ds(tm, 8), :]
        qt = c * sg
        for hd in range(NH):
            cq = slice(hd * DH, (hd + 1) * DH)
            ck = slice(GW + hd * DH, GW + (hd + 1) * DH)
            qh, kh = qt[:, cq], qt[:, ck]
            qn_ref[:, cq] = qh * (lax.rsqrt(jnp.sum(qh * qh, axis=-1, keepdims=True) + EPS) * DH ** -0.5)
            kn_ref[:, cq] = kh * lax.rsqrt(jnp.sum(kh * kh, axis=-1, keepdims=True) + EPS)
        v_ref[...] = qt[:, 2 * GW:3 * GW]

        lane = lax.broadcasted_iota(jnp.int32, (tm, 128), 1)
        row = lax.broadcasted_iota(jnp.int32, (tm, 128), 0) % CH
        beta, g, _ = _gates(ba, al_ref[...], dt_ref[...], lane)
        bg_ref[...] = beta + g
        gc_ref[...] = _seg_cumsum(g, row)

    wide = lambda w: _row_spec(tm, w)
    shp = lambda w: jax.ShapeDtypeStruct((T, w), F32)
    return pl.pallas_call(
        body, grid=(T // tm,), name="mix_proj",
        in_specs=[wide(D), _const_spec((1, D)), _const_spec((1, D)), _const_spec((1, D)), _const_spec((DINP, D)),
                  _const_spec((4, 3 * GW)), _const_spec((1, 128)), _const_spec((1, 128))],
        out_specs=[wide(3 * GW), wide(128), wide(GW), wide(GW), wide(GW), wide(GW), wide(PW), wide(128), wide(128)],
        out_shape=[shp(3 * GW), shp(128), shp(GW), shp(GW), shp(GW), shp(GW), shp(PW), shp(128), shp(128)],
        scratch_shapes=[pltpu.VMEM((tm + 8, 3 * GW), F32)],
        compiler_params=_cparams(),
    )(x, shift, scale, gain, winT, conv_w, alog, dtb)


R2 = 2 * CH
TRI_PREC = None


def _tmm(fn, a, b):
    if TRI_PREC is None:
        return fn(a.astype(BF), b.astype(BF))
    return fn(a, b, precision=TRI_PREC)


def _pair_consts():
    ii = lax.broadcasted_iota(jnp.int32, (R2, R2), 0)
    jj = lax.broadcasted_iota(jnp.int32, (R2, R2), 1)
    same = (ii < CH) == (jj < CH)
    r = lax.broadcasted_iota(jnp.int32, (R2, 1), 0)
    return dict(causal=same & (ii >= jj), strict=same & (ii > jj), eye=(ii == jj).astype(F32), rowA=r < CH,
                last=(r == CH - 1) | (r == R2 - 1),
                rowS=lax.broadcasted_iota(jnp.int32, (2 * DH, 1), 0) < DH)


def _tri_inverse_many(ms, eye):
    pws = [-m for m in ms]
    ts = [eye + p for p in pws]
    for _ in range(5):
        pws = [_tmm(_nn, p, p) for p in pws]
        ts = [_tmm(_nn, t, eye + p) for t, p in zip(ts, pws)]
    return ts


def _egl_rows(gl):
    egl = jnp.exp(gl)
    return egl, jnp.concatenate([jnp.broadcast_to(egl[0:1], (DH, 1)), jnp.broadcast_to(egl[CH:CH + 1], (DH, 1))], axis=0)


def _pair_intra(items, cn):
    causal, rowA = cn["causal"], cn["rowA"]

    def bd(t):
        return jnp.concatenate([jnp.where(rowA, t, 0.0), jnp.where(rowA, 0.0, t)], axis=1).astype(BF)

    outs = []
    for q, k, v, beta, gcv, gl in items:
        gc_b = jnp.broadcast_to(gcv, (R2, R2))
        gam = jnp.where(causal, jnp.exp(jnp.where(causal, gc_b - gc_b.T, 0.0)), 0.0)
        kb = k * beta
        kbf = k.astype(BF)
        P = _nt(kb.astype(BF), kbf)
        QK = _nt(q.astype(BF), kbf)
        E = jnp.exp(gcv)
        outs.append(dict(gam=gam, kb=kb, vb=v * beta, P=P, QK=QK, E=E, Fd=jnp.exp(gl - gcv), kbE=kb * E,
                         Q=QK * gam, qE_bd=bd(q * E)))
    tms = _tri_inverse_many([jnp.where(cn["strict"], d["P"] * d["gam"], 0.0) for d in outs], cn["eye"])
    for d, tm_, (q, k, v, beta, gcv, gl) in zip(outs, tms, items):
        d["Tm"] = tm_
        d["u"] = _tmm(_nn, tm_, d["vb"])
        d["w_bd"] = bd(_tmm(_nn, tm_, d["kbE"]))
        d["kF_bd"] = bd(k * d["Fd"])
    return outs


def _pair_scan(it, S, egl_st):
    Sb = S.astype(BF)
    vn = it["u"] - _nn(it["w_bd"], Sb)
    vnb = vn.astype(BF)
    o = _nn(it["qE_bd"], Sb) + _nn(it["Q"].astype(BF), vnb)
    return vnb, o, S * egl_st + _tn(it["kF_bd"], vnb)


def _pair_forward(q, k, v, beta, gcv, gl, S, cn):
    fw = _pair_intra([(q, k, v, beta, gcv, gl)], cn)[0]
    fw["egl"], fw["egl_st"] = _egl_rows(gl)
    fw["vnb"], fw["o"], fw["S_new"] = _pair_scan(fw, S, fw["egl_st"])
    return fw


def _stack_heads(ref, rows, pair):
    return jnp.concatenate([ref[rows, (2 * pair) * DH:(2 * pair + 1) * DH],
                            ref[rows, (2 * pair + 1) * DH:(2 * pair + 2) * DH]], axis=0)


def _stack_cols(val, lane_a, lane_b, bcast_rows=None):
    a, b = val[:, lane_a:lane_a + 1], val[:, lane_b:lane_b + 1]
    if bcast_rows:
        a, b = jnp.broadcast_to(a, (bcast_rows, 1)), jnp.broadcast_to(b, (bcast_rows, 1))
    return jnp.concatenate([a, b], axis=0)


def _pool_windows(ext, tm, reverse):
    n = tm + HALO
    outs = []
    for gi in range(NG):
        a = ext[:, gi * 128:(gi + 1) * 128]
        s = 1
        while s < POOL_WINDOWS[gi]:
            a = a + pltpu.roll(a, (n - s) if reverse else s, 0)
            s *= 2
        outs.append(a[0:tm] if reverse else a[HALO:HALO + tm])
    return jnp.concatenate(outs, axis=1)


def _pool_count(tm, tile_index):
    t1 = (lax.broadcasted_iota(jnp.int32, (tm, PW), 0) + tile_index * tm + 1).astype(F32)
    win = jnp.concatenate([jnp.full((tm, 128), float(w), F32) for w in POOL_WINDOWS], axis=1)
    return 1.0 / jnp.minimum(t1, win)


def _chunk_item(qn_ref, kn_ref, v_ref, bg_ref, gc_ref, c, pr):
    r0 = pl.multiple_of(c * CH, CH)
    rows = pl.ds(r0, CH)
    bgv = bg_ref[rows, :]
    gcv_all = gc_ref[rows, :]
    gl_all = gc_ref[pl.ds(r0 + CH - 1, 1), :]
    ha, hb = 2 * pr, 2 * pr + 1
    return (_stack_heads(qn_ref, rows, pr), _stack_heads(kn_ref, rows, pr), _stack_heads(v_ref, rows, pr),
            _stack_cols(bgv, ha, hb), _stack_cols(gcv_all, NH + ha, NH + hb), _stack_cols(gl_all, NH + ha, NH + hb, CH))


CHUNK_GROUP = 4


def _mix_core(x, gate, qn, kn, v, z, p, bg, gc, gnorm, pool_w, pool_scale, w_out):
    T = x.shape[0]
    tm = _tile(T)
    nc = tm // CH
    cg = CHUNK_GROUP if nc % CHUNK_GROUP == 0 else 1
    npb = nc * (NH // 2)

    def body(x_ref, gt_ref, qn_ref, kn_ref, v_ref, z_ref, p_ref, bg_ref, gc_ref, gnm_ref, pw_ref, ps_ref, wo_ref,
             xo_ref, mx_ref, cat_ref, o_ref, sall_ref, S_scr, pext, u_s, w_s, qe_s, kf_s, q_s):
        i = pl.program_id(0)

        @pl.when(i == 0)
        def _():
            S_scr[...] = jnp.zeros((NH * DH, DH), F32)
            pext[pl.ds(0, HALO), :] = jnp.zeros((HALO, PW), F32)

        cn = _pair_consts()

        def intra(g, carry):
            idx = [(g * cg + dc, pr) for dc in range(cg) for pr in range(NH // 2)]
            res = _pair_intra([_chunk_item(qn_ref, kn_ref, v_ref, bg_ref, gc_ref, c, pr) for c, pr in idx], cn)
            for (c, pr), d in zip(idx, res):
                pi = c * (NH // 2) + pr
                u_s[pi] = d["u"]
                w_s[pi] = d["w_bd"]
                qe_s[pi] = d["qE_bd"]
                kf_s[pi] = d["kF_bd"]
                q_s[pi] = d["Q"].astype(BF)
            return carry

        lax.fori_loop(0, nc // cg, intra, 0)

        def scan(c, carry):
            r0 = pl.multiple_of(c * CH, CH)
            rows = pl.ds(r0, CH)
            gl_all = gc_ref[pl.ds(r0 + CH - 1, 1), :]
            for pr in range(NH // 2):
                ha, hb = 2 * pr, 2 * pr + 1
                pi = c * (NH // 2) + pr
                S = S_scr[pl.ds(pr * 2 * DH, 2 * DH), :]
                sall_ref[c, ha:hb + 1] = S.reshape(2, DH, DH)
                _, egl_st = _egl_rows(_stack_cols(gl_all, NH + ha, NH + hb, CH))
                it = dict(u=u_s[pi], w_bd=w_s[pi], qE_bd=qe_s[pi], kF_bd=kf_s[pi], Q=q_s[pi])
                _, o, S_new = _pair_scan(it, S, egl_st)
                o_ref[rows, ha * DH:(ha + 1) * DH] = o[0:CH]
                o_ref[rows, hb * DH:(hb + 1) * DH] = o[CH:R2]
                S_scr[pl.ds(pr * 2 * DH, 2 * DH), :] = S_new
            return carry

        lax.fori_loop(0, nc, scan, 0)

        for hd in range(NH):
            cols = slice(hd * DH, (hd + 1) * DH)
            oh = o_ref[:, cols]
            zh = z_ref[:, cols]
            r = lax.rsqrt(jnp.mean(oh * oh, axis=-1, keepdims=True) + EPS)
            cat_ref[:, cols] = (oh * r * gnm_ref[...] * (zh * _sigmoid(zh))).astype(BF)

        pv = p_ref[...]
        pext[pl.ds(HALO, tm), :] = pv
        pooled = _pool_windows(pext[...], tm, False) * _pool_count(tm, i) - pv
        pext[pl.ds(0, HALO), :] = pext[pl.ds(tm, HALO), :]
        for gi in range(NG):
            cols = slice(gi * 128, (gi + 1) * 128)
            pm = _nn(pooled[:, cols].astype(BF), pw_ref[gi])
            cat_ref[:, GW + gi * 128:GW + (gi + 1) * 128] = (pm * ps_ref[:, cols]).astype(BF)

        mixed = _nn(cat_ref[...], wo_ref[...])
        mx_ref[...] = mixed
        xo_ref[...] = x_ref[...] + gt_ref[...] * mixed

    wide = lambda w: _row_spec(tm, w)
    return pl.pallas_call(
        body, grid=(T // tm,), name="mix_core",
        in_specs=[wide(D), _const_spec((1, D)), wide(GW), wide(GW), wide(GW), wide(GW), wide(PW), wide(128), wide(128),
                  _const_spec((1, DH)), _const_spec((NG, 128, 128)), _const_spec((1, PW)), _const_spec((D, D))],
        out_specs=[wide(D), wide(D), wide(D), wide(GW), pl.BlockSpec((nc, NH, DH, DH), lambda i: (i, 0, 0, 0))],
        out_shape=[jax.ShapeDtypeStruct((T, D), F32), jax.ShapeDtypeStruct((T, D), F32),
                   jax.ShapeDtypeStruct((T, D), BF), jax.ShapeDtypeStruct((T, GW), F32),
                   jax.ShapeDtypeStruct((T // CH, NH, DH, DH), F32)],
        scratch_shapes=[pltpu.VMEM((NH * DH, DH), F32), pltpu.VMEM((tm + HALO, PW), F32),
                        pltpu.VMEM((npb, R2, DH), F32), pltpu.VMEM((npb, R2, 2 * DH), BF),
                        pltpu.VMEM((npb, R2, 2 * DH), BF), pltpu.VMEM((npb, R2, 2 * DH), BF),
                        pltpu.VMEM((npb, R2, R2), BF)],
        compiler_params=_cparams(),
    )(x, gate, qn, kn, v, z, p, bg, gc, gnorm, pool_w, pool_scale, w_out)


def _pair_scan_bwd(it, S, dSn, do, egl, egl_st, cn):
    bf = lambda t: t.astype(BF)
    rowA, rowS = cn["rowA"], cn["rowS"]
    sel = lambda t: jnp.where(rowA, t[:, 0:DH], t[:, DH:2 * DH])
    Sb, dSb, dob = bf(S), bf(dSn), bf(do)
    vnb = bf(it["u"] - _nn(it["w_bd"], Sb))
    dvn = _tn(bf(it["Q"]), dob) + _nn(it["kF_bd"], dSb)
    dQ = _nt(dob, vnb)
    dqE = sel(_nt(dob, Sb))
    dkF = sel(_nt(vnb, dSb))
    dvnb = bf(dvn)
    dw = -sel(_nt(dvnb, Sb))
    dS_new = _tn(it["qE_bd"], dob) + egl_st * dSn - _tn(it["w_bd"], dvnb)
    prod = jnp.sum(dSn * S, axis=1, keepdims=True)
    d_egl_a = jnp.sum(jnp.where(rowS, prod, 0.0), axis=0, keepdims=True)
    d_egl_b = jnp.sum(jnp.where(rowS, 0.0, prod), axis=0, keepdims=True)
    return dict(dvn=dvn, dQ=dQ, dqE=dqE, dkF=dkF, dw=dw, degl=jnp.where(rowA, d_egl_a, d_egl_b) * egl), dS_new


def _pair_intra_bwd(items, cn):
    bf = lambda t: t.astype(BF)
    rowA = cn["rowA"]
    for d in items:
        d["kb"] = d["k"] * d["beta"]
        d["E"] = jnp.exp(d["gcv"])
        d["Fd"] = jnp.exp(d["gl"] - d["gcv"])
        d["TmT"] = d["Tm"].T
        d["dvb"] = _tmm(_nn, d["TmT"], d["dvn"])
        d["dkbE"] = _tmm(_nn, d["TmT"], d["dw"])
        dTm = _tmm(_nt, d["dvn"], d["v"] * d["beta"]) + _tmm(_nt, d["dw"], d["kb"] * d["E"])
        d["X"] = _tmm(_nt, dTm, d["Tm"])
    for d in items:
        d["dA"] = -_tmm(_nn, d["TmT"], d["X"])
    outs = []
    for d in items:
        q, k, v, beta, gam = d["q"], d["k"], d["v"], d["beta"], d["gam"]
        N = jnp.where(cn["strict"], d["dA"] * gam, 0.0)
        Rm = d["dQ"] * gam
        Wm = Rm * d["QK"] + N * d["P"]
        dgc = jnp.sum(Wm, axis=1, keepdims=True) - jnp.sum(Wm.T, axis=1, keepdims=True)
        kbf, Nb, Rb = bf(k), bf(N), bf(Rm)
        E, Fd = d["E"], d["Fd"]
        dq = d["dqE"] * E + _nn(Rb, kbf)
        dkb = d["dkbE"] * E + _nn(Nb, kbf)
        dk = d["dkF"] * Fd + _tn(Rb, bf(q)) + _tn(Nb, bf(d["kb"])) + beta * dkb
        dbeta = jnp.sum(dkb * k + d["dvb"] * v, axis=1, keepdims=True)
        dE = jnp.sum(d["dqE"] * q + d["dkbE"] * d["kb"], axis=1, keepdims=True)
        fdf = jnp.sum(d["dkF"] * k, axis=1, keepdims=True) * Fd
        dgl = d["degl"] + jnp.where(rowA, jnp.sum(jnp.where(rowA, fdf, 0.0), axis=0, keepdims=True),
                                    jnp.sum(jnp.where(rowA, 0.0, fdf), axis=0, keepdims=True))
        dgc = dgc + dE * E - fdf + jnp.where(cn["last"], dgl, 0.0)
        outs.append((dq, dk, beta * d["dvb"], dbeta, dgc))
    return outs


def _mix_core_bwd(dxo, gate, mixed, cat, o, sall, qn, kn, v, z, p, bg, gc, gnorm, pool_w, pool_scale, w_out,
                  scatter=()):
    T = dxo.shape[0]
    tm = _tile(T, 256)
    nt = T // tm
    nc = tm // CH
    ns = len(scatter)
    cg = CHUNK_GROUP if nc % CHUNK_GROUP == 0 else 1
    npb = nc * (NH // 2)

    def body(*refs):
        (dx_ref, gt_ref, mx_ref, cat_ref, o_ref, sall_ref, qn_ref, kn_ref, v_ref, z_ref, p_ref, ph_ref, bg_ref,
         gc_ref, gnm_ref, pw_ref, ps_ref, wo_ref) = refs[:18]
        (dq_ref, dk_ref, dv_ref, dz_ref, dp_ref, dbg_ref, dgt_ref, dwo_ref, dpw_ref, dps_ref,
         dgn_ref) = refs[18 + ns:29 + ns]
        dS_scr, pext, yext, do_buf, dwo_acc = refs[29 + 2 * ns:34 + 2 * ns]
        (gam_s, p_s, qk_s, tm_s, dqq_s, u_s, dvn_s, dqe_s, dkf_s, dw_s, w_s, qe_s, kf_s, q_s,
         degl_s) = refs[34 + 2 * ns:49 + 2 * ns]
        i = pl.program_id(0)
        ti = nt - 1 - i
        if ns:
            plan = _Scatter(refs[18:18 + ns], refs[29 + ns:29 + 2 * ns], *refs[49 + 2 * ns:])
            _comm_begin(i, plan, nt - 1)

        @pl.when(i == 0)
        def _():
            dS_scr[...] = jnp.zeros((NH * DH, DH), F32)
            yext[pl.ds(tm, HALO), :] = jnp.zeros((HALO, PW), F32)
            dwo_acc[...] = jnp.zeros((D, D), F32)
            dgt_ref[...] = jnp.zeros((8, D), F32)
            dpw_ref[...] = jnp.zeros((NG, 128, 128), F32)
            dps_ref[...] = jnp.zeros((8, PW), F32)
            dgn_ref[...] = jnp.zeros((8, DH), F32)

        dx2 = dx_ref[...]
        dgt_ref[...] += _sum8(mx_ref[...] * dx2)
        dmix = (gt_ref[...] * dx2).astype(BF)
        dcat = _nt(dmix, wo_ref[...])
        dwo_acc[...] += _tn(cat_ref[...], dmix)

        pv = p_ref[...]
        pext[pl.ds(0, HALO), :] = jnp.where(ti == 0, 0.0, ph_ref[...])
        pext[pl.ds(HALO, tm), :] = pv
        inv_cnt = _pool_count(tm, ti)
        pooled = _pool_windows(pext[...], tm, False) * inv_cnt - pv
        dpooled = []
        for gi in range(NG):
            cols = slice(gi * 128, (gi + 1) * 128)
            pgb = pooled[:, cols].astype(BF)
            pm = _nn(pgb, pw_ref[gi])
            dpo = dcat[:, GW + gi * 128:GW + (gi + 1) * 128]
            dps_ref[:, cols] += _sum8(dpo * pm)
            dpm = (dpo * ps_ref[:, cols]).astype(BF)
            dpooled.append(_nt(dpm, pw_ref[gi]))
            dpw_ref[gi] += _tn(pgb, dpm)
        dpooled = jnp.concatenate(dpooled, axis=1)
        y = dpooled * inv_cnt
        yext[pl.ds(0, tm), :] = y
        dp_ref[...] = _pool_windows(yext[...], tm, True) - dpooled
        yext[pl.ds(tm, HALO), :] = y[0:HALO]

        gnm = gnm_ref[...]
        dgn = jnp.zeros((8, DH), F32)
        for hd in range(NH):
            cols = slice(hd * DH, (hd + 1) * DH)
            oh = o_ref[:, cols]
            zh = z_ref[:, cols]
            r = lax.rsqrt(jnp.mean(oh * oh, axis=-1, keepdims=True) + EPS)
            n = oh * r
            sg = _sigmoid(zh)
            zs = zh * sg
            dgo = dcat[:, cols]
            dgn = dgn + _sum8(dgo * zs * n)
            dn = dgo * zs * gnm
            do_buf[:, cols] = r * (dn - n * jnp.mean(dn * n, axis=-1, keepdims=True))
            dz_ref[:, cols] = dgo * n * gnm * (sg * (1.0 + zh * (1.0 - sg)))
        dgn_ref[...] += dgn

        cn = _pair_consts()
        lane_c = lax.broadcasted_iota(jnp.int32, (CH, 128), 1)

        def intra(g, carry):
            idx = [(g * cg + dc, pr) for dc in range(cg) for pr in range(NH // 2)]
            res = _pair_intra([_chunk_item(qn_ref, kn_ref, v_ref, bg_ref, gc_ref, c, pr) for c, pr in idx], cn)
            for (c, pr), d in zip(idx, res):
                pi = c * (NH // 2) + pr
                gam_s[pi], p_s[pi], qk_s[pi], tm_s[pi], u_s[pi] = d["gam"], d["P"], d["QK"], d["Tm"], d["u"]
                w_s[pi], qe_s[pi], kf_s[pi], q_s[pi] = d["w_bd"], d["qE_bd"], d["kF_bd"], d["Q"].astype(BF)
            return carry

        lax.fori_loop(0, nc // cg, intra, 0)

        def scan(cc, carry):
            c = nc - 1 - cc
            r0 = pl.multiple_of(c * CH, CH)
            rows = pl.ds(r0, CH)
            gl_all = gc_ref[pl.ds(r0 + CH - 1, 1), :]
            for pr in range(NH // 2):
                ha, hb = 2 * pr, 2 * pr + 1
                pi = c * (NH // 2) + pr
                srows = pl.ds(pr * 2 * DH, 2 * DH)
                S = sall_ref[c, ha:hb + 1].reshape(2 * DH, DH)
                egl, egl_st = _egl_rows(_stack_cols(gl_all, NH + ha, NH + hb, CH))
                it = dict(u=u_s[pi], w_bd=w_s[pi], qE_bd=qe_s[pi], kF_bd=kf_s[pi], Q=q_s[pi])
                g, dS_new = _pair_scan_bwd(it, S, dS_scr[srows, :], _stack_heads(do_buf, rows, pr), egl, egl_st, cn)
                dvn_s[pi], dqq_s[pi], dqe_s[pi], dkf_s[pi], dw_s[pi] = g["dvn"], g["dQ"], g["dqE"], g["dkF"], g["dw"]
                degl_s[pi] = g["degl"]
                dS_scr[srows, :] = dS_new
            return carry

        lax.fori_loop(0, nc, scan, 0)

        def intra_bwd(g, carry):
            idx = [(g * cg + dc, pr) for dc in range(cg) for pr in range(NH // 2)]
            items = []
            for c, pr in idx:
                pi = c * (NH // 2) + pr
                q, k, vv, beta, gcv, gl = _chunk_item(qn_ref, kn_ref, v_ref, bg_ref, gc_ref, c, pr)
                items.append(dict(q=q, k=k, v=vv, beta=beta, gcv=gcv, gl=gl, gam=gam_s[pi], P=p_s[pi], QK=qk_s[pi],
                                  Tm=tm_s[pi], dvn=dvn_s[pi], dQ=dqq_s[pi], dqE=dqe_s[pi], dkF=dkf_s[pi], dw=dw_s[pi],
                                  degl=degl_s[pi]))
            res = _pair_intra_bwd(items, cn)
            for dc in range(cg):
                c = g * cg + dc
                rows = pl.ds(pl.multiple_of(c * CH, CH), CH)
                dbg = jnp.zeros((CH, 128), F32)
                for pr in range(NH // 2):
                    dq, dk, dv, dbeta, dgc = res[dc * (NH // 2) + pr]
                    for hd, half in ((2 * pr, slice(0, CH)), (2 * pr + 1, slice(CH, R2))):
                        cols = slice(hd * DH, (hd + 1) * DH)
                        dq_ref[rows, cols] = dq[half]
                        dk_ref[rows, cols] = dk[half]
                        dv_ref[rows, cols] = dv[half]
                        dbg = dbg + jnp.where(lane_c == hd, dbeta[half], 0.0) + jnp.where(lane_c == NH + hd, dgc[half], 0.0)
                dbg_ref[rows, :] = dbg
            return carry

        lax.fori_loop(0, nc // cg, intra_bwd, 0)

        lane = lax.broadcasted_iota(jnp.int32, (tm, 128), 1)
        row = lax.broadcasted_iota(jnp.int32, (tm, 128), 0) % CH
        dbg_all = dbg_ref[...]
        dg = _seg_cumsum(jnp.where(lane >= NH, dbg_all, 0.0), row, reverse=True)
        dbg_ref[...] = jnp.where(lane < NH, dbg_all, dg)

        @pl.when(i == nt - 1)
        def _():
            dwo_ref[...] = dwo_acc[...].astype(BF)

        if ns:
            _comm_end(i, plan, nt - 1)

    rev = lambda w: pl.BlockSpec((tm, w), lambda i: (nt - 1 - i, 0))
    halo = pl.BlockSpec((HALO, PW), lambda i: (jnp.maximum((nt - 1 - i) * (tm // HALO) - 1, 0), 0))
    shp = lambda w: jax.ShapeDtypeStruct((T, w), F32)
    fix = lambda *s: pl.BlockSpec(s, lambda i: (0,) * len(s))
    return pl.pallas_call(
        body, grid=(nt,), name="mix_core_bwd",
        in_specs=[rev(D), _const_spec((1, D)), rev(D), rev(D), rev(GW),
                  pl.BlockSpec((nc, NH, DH, DH), lambda i: (nt - 1 - i, 0, 0, 0)),
                  rev(GW), rev(GW), rev(GW), rev(GW), rev(PW), halo, rev(128), rev(128),
                  _const_spec((1, DH)), _const_spec((NG, 128, 128)), _const_spec((1, PW)), _const_spec((D, D))]
        + [ANY] * ns,
        out_specs=[rev(GW), rev(GW), rev(GW), rev(GW), rev(PW), rev(128),
                   fix(8, D), fix(D, D), fix(NG, 128, 128), fix(8, PW), fix(8, DH)] + [ANY] * ns,
        out_shape=[shp(GW), shp(GW), shp(GW), shp(GW), shp(PW), shp(128),
                   jax.ShapeDtypeStruct((8, D), F32), jax.ShapeDtypeStruct((D, D), BF),
                   jax.ShapeDtypeStruct((NG, 128, 128), F32), jax.ShapeDtypeStruct((8, PW), F32),
                   jax.ShapeDtypeStruct((8, DH), F32)] + [jax.ShapeDtypeStruct(t.shape, t.dtype) for t in scatter],
        scratch_shapes=[pltpu.VMEM((NH * DH, DH), F32), pltpu.VMEM((tm + HALO, PW), F32),
                        pltpu.VMEM((tm + HALO, PW), F32), pltpu.VMEM((tm, GW), F32), pltpu.VMEM((D, D), F32)]
        + [pltpu.VMEM((npb, R2, R2), F32)] * 5 + [pltpu.VMEM((npb, R2, DH), F32)] * 5
        + [pltpu.VMEM((npb, R2, 2 * DH), BF)] * 3 + [pltpu.VMEM((npb, R2, R2), BF), pltpu.VMEM((npb, R2, 1), F32)]
        + (_scatter_sems(ns) if ns else []),
        compiler_params=_cparams(),
    )(dxo, gate, mixed, cat, o, sall, qn, kn, v, z, p, p, bg, gc, gnorm, pool_w, pool_scale, w_out, *scatter)


def _mix_proj_bwd(dxo, dqn, dkn, dv, dz, dp, dbg, xq, ba, x, shift, scale, gain, winT, conv_w, alog, dtb):
    T = x.shape[0]
    tm = min(256, T)
    nt = T // tm
    W3 = 3 * GW

    def body(dxo_ref, dqn_ref, dkn_ref, dv_ref, dz_ref, dp_ref, dbg_ref, xq_ref, xh_ref, ba_ref, x_ref, sh_ref, sc_ref,
             gn_ref, w_ref, cw_ref, al_ref, dt_ref,
             dx_ref, dw_ref, dcw_ref, dal_ref, ddt_ref, dsh_ref, dsc_ref, dgn_ref,
             ext, dcext, dproj, dw_acc):
        i = pl.program_id(0)
        ti = nt - 1 - i

        @pl.when(i == 0)
        def _():
            dcext[pl.ds(tm, 8), :] = jnp.zeros((8, W3), F32)
            dw_acc[...] = jnp.zeros((DINP, D), F32)
            dcw_ref[...] = jnp.zeros((4, 8, W3), F32)
            dal_ref[...] = jnp.zeros((8, 128), F32)
            ddt_ref[...] = jnp.zeros((8, 128), F32)
            dsh_ref[...] = jnp.zeros((8, D), F32)
            dsc_ref[...] = jnp.zeros((8, D), F32)
            dgn_ref[...] = jnp.zeros((8, D), F32)

        ext[pl.ds(0, 8), :] = jnp.where(ti == 0, 0.0, xh_ref[...])
        ext[pl.ds(8, tm), :] = xq_ref[...]
        c, sg = _conv_silu(ext, cw_ref, tm)
        qt = c * sg
        dsilu = sg * (1.0 + c * (1.0 - sg))
        for hd in range(NH):
            for part, dref, mult in ((0, dqn_ref, DH ** -0.5), (1, dkn_ref, 1.0)):
                cols = slice(part * GW + hd * DH, part * GW + (hd + 1) * DH)
                xh = qt[:, cols]
                rr = lax.rsqrt(jnp.sum(xh * xh, axis=-1, keepdims=True) + EPS)
                unit = xh * rr
                du = dref[:, hd * DH:(hd + 1) * DH] * mult
                dxh = rr * (du - unit * jnp.sum(du * unit, axis=-1, keepdims=True))
                dcext[pl.ds(0, tm), cols] = dxh * dsilu[:, cols]
        dcext[pl.ds(0, tm), 2 * GW:W3] = dv_ref[...] * dsilu[:, 2 * GW:W3]
        dc = dcext[pl.ds(0, tm), :]
        dxq = jnp.zeros((tm, W3), F32)
        for j in range(4):
            dcw_ref[j] += _sum8(dc * ext[pl.ds(5 + j, tm), :])
            dxq = dxq + cw_ref[j:j + 1, :] * dcext[pl.ds(3 - j, tm), :]
        dcext[pl.ds(tm, 8), :] = dc[0:8]
        lane = lax.broadcasted_iota(jnp.int32, (tm, 128), 1)
        bav = ba_ref[...]
        beta, g, sarg = _gates(bav, al_ref[...], dt_ref[...], lane)
        dbg_v = dbg_ref[...]
        is_g = (lane >= NH) & (lane < 2 * NH)
        dbraw = jnp.where(lane < NH, dbg_v * beta * (1.0 - beta), 0.0)
        daraw = jnp.where(is_g, dbg_v * (-jnp.exp(al_ref[...])) * sarg, 0.0)
        dal_ref[...] += _sum8(jnp.where(is_g, dbg_v * g, 0.0))
        ddt_ref[...] += _sum8(daraw)
        dproj[:, 0:W3] = dxq.astype(BF)
        dproj[:, W3:W3 + GW] = dz_ref[...].astype(BF)
        dproj[:, W3 + GW:W3 + GW + PW] = dp_ref[...].astype(BF)
        dproj[:, W3 + GW + PW:DINP] = (dbraw + daraw).astype(BF)
        gain_v, scale_v = gn_ref[...], sc_ref[...]
        n, r, y, h = _norm_mod_fwd(x_ref[...], gain_v, sh_ref[...], scale_v)
        dpj = dproj[...]
        dh = _nn(dpj, w_ref[...])
        dw_acc[...] += _tn(dpj, h.astype(BF))
        dxn, dsh, dsc, dgn = _norm_mod_bwd(dh, n, r, y, gain_v, scale_v)
        dx_ref[...] = dxo_ref[...] + dxn
        dsh_ref[...] += dsh
        dsc_ref[...] += dsc
        dgn_ref[...] += dgn

        @pl.when(i == nt - 1)
        def _():
            dw_ref[...] = dw_acc[...].astype(BF)

    rev = lambda w: pl.BlockSpec((tm, w), lambda i: (nt - 1 - i, 0))
    halo = pl.BlockSpec((8, W3), lambda i: (jnp.maximum((nt - 1 - i) * (tm // 8) - 1, 0), 0))
    fix = lambda *s: pl.BlockSpec(s, lambda i: (0,) * len(s))
    vec = _const_spec((1, D))
    return pl.pallas_call(
        body, grid=(nt,), name="mix_proj_bwd",
        in_specs=[rev(D), rev(GW), rev(GW), rev(GW), rev(GW), rev(PW), rev(128), rev(W3), halo, rev(128), rev(D),
                  vec, vec, vec, _const_spec((DINP, D)), _const_spec((4, W3)), _const_spec((1, 128)),
                  _const_spec((1, 128))],
        out_specs=[rev(D), fix(DINP, D), fix(4, 8, W3), fix(8, 128), fix(8, 128), fix(8, D), fix(8, D), fix(8, D)],
        out_shape=[jax.ShapeDtypeStruct((T, D), F32), jax.ShapeDtypeStruct((DINP, D), BF),
                   jax.ShapeDtypeStruct((4, 8, W3), F32), jax.ShapeDtypeStruct((8, 128), F32),
                   jax.ShapeDtypeStruct((8, 128), F32), jax.ShapeDtypeStruct((8, D), F32),
                   jax.ShapeDtypeStruct((8, D), F32), jax.ShapeDtypeStruct((8, D), F32)],
        scratch_shapes=[pltpu.VMEM((tm + 8, W3), F32), pltpu.VMEM((tm + 8, W3), F32), pltpu.VMEM((tm, DINP), BF),
                        pltpu.VMEM((DINP, D), F32)],
        compiler_params=_cparams(),
    )(dxo, dqn, dkn, dv, dz, dp, dbg, xq, xq, ba, x, shift, scale, gain, winT, conv_w, alog, dtb)


MESH = pl.DeviceIdType.MESH
CHIP_RELS = ((1, 0), (0, 1), (1, 1))
DEV_RELS = tuple((dx, dy, dc) for dx in (0, 1) for dy in (0, 1) for dc in (0, 1) if (dx, dy, dc) != (0, 0, 0))
NMOD = 9
ADA_SH = NMOD * D // 4
WIN_SH = DIN // 4
WIN_PAD = 672
MSG_ROWS = 16
ANY = pl.BlockSpec(memory_space=pl.ANY)
VM = pl.BlockSpec(memory_space=pltpu.VMEM)


def _place():
    x, y, c = lax.axis_index("x"), lax.axis_index("y"), lax.axis_index("c")
    return x, y, c


class _SplitGather:
    N_SEMS = (3, 3, 3, 3, 1)

    def __init__(self, src, dst, ici_s, ici_r, d2d_s, d2d_r, lsem):
        self.src, self.dst = src, dst
        self.sems = (ici_s, ici_r, d2d_s, d2d_r)
        self.x, self.y, self.c = _place()
        self.myj = 2 * self.x + self.y
        half = src.shape[0] // 2
        self.mine = pl.ds(pl.multiple_of(self.c * half, 16), half)
        self.other = pl.ds(pl.multiple_of((1 - self.c) * half, 16), half)
        self.local = pltpu.make_async_copy(src, dst.at[self.myj], lsem.at[0])

    def _ici(self, k, slot):
        dx, dy = CHIP_RELS[k]
        return pltpu.make_async_remote_copy(
            src_ref=self.src.at[self.mine], dst_ref=self.dst.at[slot, self.mine], send_sem=self.sems[0].at[k],
            recv_sem=self.sems[1].at[k], device_id=(self.x ^ dx, self.y ^ dy, self.c), device_id_type=MESH)

    def _d2d(self, k, rows):
        dx, dy = CHIP_RELS[k]
        blk = self.dst.at[2 * (self.x ^ dx) + (self.y ^ dy), rows]
        return pltpu.make_async_remote_copy(
            src_ref=blk, dst_ref=blk, send_sem=self.sems[2].at[k], recv_sem=self.sems[3].at[k],
            device_id=(self.x, self.y, 1 - self.c), device_id_type=MESH)

    def start(self):
        self.local.start()
        for k in range(3):
            self._ici(k, self.myj).start()

    def forward(self):
        for k, (dx, dy) in enumerate(CHIP_RELS):
            self._ici(k, 2 * (self.x ^ dx) + (self.y ^ dy)).wait_recv()
            self._d2d(k, self.mine).start()

    def finish(self):
        for k in range(3):
            self._d2d(k, self.other).wait_recv()
        for k in range(3):
            self._d2d(k, self.mine).wait_send()
            self._ici(k, self.myj).wait_send()
        self.local.wait()


class _Scatter:
    def __init__(self, ins, outs, send, recv, lsem):
        self.ins, self.outs, self.send, self.recv, self.lsem = ins, outs, send, recv, lsem
        self.x, self.y, self.c = _place()
        self.myj = 2 * self.x + self.y

    def _copy(self, a, k, landing):
        dx, dy = CHIP_RELS[k]
        pj = 2 * (self.x ^ dx) + (self.y ^ dy)
        return pltpu.make_async_remote_copy(
            src_ref=self.ins[a].at[pj], dst_ref=self.outs[a].at[pj if landing else self.myj],
            send_sem=self.send.at[a, k], recv_sem=self.recv.at[a, k],
            device_id=(self.x ^ dx, self.y ^ dy, self.c), device_id_type=MESH)

    def _local(self, a):
        return pltpu.make_async_copy(self.ins[a].at[self.myj], self.outs[a].at[self.myj], self.lsem.at[a])

    def start(self):
        for a in range(len(self.ins)):
            self._local(a).start()
            for k in range(3):
                self._copy(a, k, False).start()

    def finish(self):
        for a in range(len(self.ins)):
            for k in range(3):
                self._copy(a, k, True).wait_recv()
            for k in range(3):
                self._copy(a, k, False).wait_send()
            self._local(a).wait()


def _scatter_sems(n):
    return [pltpu.SemaphoreType.DMA((n, 3)), pltpu.SemaphoreType.DMA((n, 3)), pltpu.SemaphoreType.DMA((n,))]


def _gather_sems():
    return [pltpu.SemaphoreType.DMA((k,)) for k in _SplitGather.N_SEMS]


def _comm_begin(i, plan, last):
    @pl.when(i == 0)
    def _():
        plan.start()

    if hasattr(plan, "forward"):
        @pl.when(i == max(last - 3, 0))
        def _():
            plan.forward()


def _comm_end(i, plan, last):
    @pl.when(i == last)
    def _():
        plan.finish()


def _ada_exchange(msg, w_ada, b_ada, wblock):
    def body(msg_ref, w_ref, b_ref, wb_ref, all_ref, mod_ref, wg_ref, modp, send1, recv1, send2, recv2, lsem, *gsems):
        x, y, c = _place()
        me = 4 * x + 2 * y + c
        own = pltpu.make_async_copy(msg_ref, all_ref.at[me], lsem.at[0])
        own.start()

        def gather(k, rel, slot):
            dx, dy, dc = rel
            return pltpu.make_async_remote_copy(
                src_ref=msg_ref, dst_ref=all_ref.at[slot], send_sem=send1.at[k], recv_sem=recv1.at[k],
                device_id=(x ^ dx, y ^ dy, c ^ dc), device_id_type=MESH)

        for k, rel in enumerate(DEV_RELS):
            gather(k, rel, me).start()
        for k, (dx, dy, dc) in enumerate(DEV_RELS):
            gather(k, (dx, dy, dc), 4 * (x ^ dx) + 2 * (y ^ dy) + (c ^ dc)).wait_recv()
        for k, rel in enumerate(DEV_RELS):
            gather(k, rel, me).wait_send()
        own.wait()
        wgather = _SplitGather(wb_ref, wg_ref, *gsems)
        wgather.start()

        for d in range(8):
            cv = all_ref[d, 0:8, :]
            act = cv * _sigmoid(cv)
            modp[d] = _nn(act, w_ref[...], precision=HI) + b_ref[...]

        myj = 2 * x + y
        keep = pltpu.make_async_copy(modp.at[me], mod_ref.at[myj], lsem.at[1])
        keep.start()

        def scatter(k, rel):
            dx, dy = rel
            return pltpu.make_async_remote_copy(
                src_ref=modp.at[4 * (x ^ dx) + 2 * (y ^ dy) + c], dst_ref=mod_ref.at[myj],
                send_sem=send2.at[k], recv_sem=recv2.at[k], device_id=(x ^ dx, y ^ dy, c), device_id_type=MESH)

        def landed(k, rel):
            dx, dy = rel
            return pltpu.make_async_remote_copy(
                src_ref=modp.at[me], dst_ref=mod_ref.at[2 * (x ^ dx) + (y ^ dy)],
                send_sem=send2.at[k], recv_sem=recv2.at[k], device_id=(x ^ dx, y ^ dy, c), device_id_type=MESH)

        for k, rel in enumerate(CHIP_RELS):
            scatter(k, rel).start()
        for k, rel in enumerate(CHIP_RELS):
            landed(k, rel).wait_recv()
        for k, rel in enumerate(CHIP_RELS):
            scatter(k, rel).wait_send()
        keep.wait()
        wgather.forward()
        wgather.finish()

    return pl.pallas_call(
        body, name="ada_exchange", in_specs=[VM, VM, VM, ANY], out_specs=[VM, VM, ANY],
        out_shape=[jax.ShapeDtypeStruct((8, MSG_ROWS, D), F32), jax.ShapeDtypeStruct((4, 8, ADA_SH), F32),
                   jax.ShapeDtypeStruct((4,) + wblock.shape, wblock.dtype)],
        scratch_shapes=[pltpu.VMEM((8, 8, ADA_SH), F32), pltpu.SemaphoreType.DMA((7,)), pltpu.SemaphoreType.DMA((7,)),
                        pltpu.SemaphoreType.DMA((3,)), pltpu.SemaphoreType.DMA((3,)), pltpu.SemaphoreType.DMA((2,))]
        + _gather_sems(),
        compiler_params=pltpu.CompilerParams(vmem_limit_bytes=VMEM_LIMIT),
    )(msg, w_ada, b_ada, wblock)


def _chip_exchange(parts, name):
    n = len(parts)

    def body(*refs):
        plan = _Scatter(refs[:n], refs[n:2 * n], *refs[2 * n:])
        plan.start()
        plan.finish()

    return pl.pallas_call(
        body, name=name, in_specs=[ANY] * n, out_specs=[ANY] * n,
        out_shape=[jax.ShapeDtypeStruct(p.shape, p.dtype) for p in parts], scratch_shapes=_scatter_sems(n),
    )(*parts)


class _AllGather:
    def __init__(self, ins, outs, send, recv, lsem):
        self.ins, self.outs, self.send, self.recv, self.lsem = ins, outs, send, recv, lsem
        self.x, self.y, self.c = _place()
        self.me = 4 * self.x + 2 * self.y + self.c

    def _copy(self, a, k, landing):
        dx, dy, dc = DEV_RELS[k]
        peer = 4 * (self.x ^ dx) + 2 * (self.y ^ dy) + (self.c ^ dc)
        return pltpu.make_async_remote_copy(
            src_ref=self.ins[a], dst_ref=self.outs[a].at[peer if landing else self.me],
            send_sem=self.send.at[a, k], recv_sem=self.recv.at[a, k],
            device_id=(self.x ^ dx, self.y ^ dy, self.c ^ dc), device_id_type=MESH)

    def _local(self, a):
        return pltpu.make_async_copy(self.ins[a], self.outs[a].at[self.me], self.lsem.at[a])

    def start(self):
        for a in range(len(self.ins)):
            self._local(a).start()
            for k in range(7):
                self._copy(a, k, False).start()

    def finish(self):
        for a in range(len(self.ins)):
            for k in range(7):
                self._copy(a, k, True).wait_recv()
            for k in range(7):
                self._copy(a, k, False).wait_send()
            self._local(a).wait()


def _allgather_sems(n):
    return [pltpu.SemaphoreType.DMA((n, 7)), pltpu.SemaphoreType.DMA((n, 7)), pltpu.SemaphoreType.DMA((n,))]


class _Plans:
    def __init__(self, plans):
        self.plans = plans

    def start(self):
        for p in self.plans:
            p.start()

    def finish(self):
        for p in self.plans:
            p.finish()


def _pair_exchange(parts, name):
    n = len(parts)

    def body(*refs):
        ins, outs = refs[:n], refs[n:2 * n]
        send, recv = refs[2 * n:]
        x, y, c = _place()
        cps = [pltpu.make_async_remote_copy(
            src_ref=ins[a], dst_ref=outs[a], send_sem=send.at[a], recv_sem=recv.at[a],
            device_id=(x, y, 1 - c), device_id_type=MESH) for a in range(n)]
        for cp in cps:
            cp.start()
        for cp in cps:
            cp.wait_recv()
        for cp in cps:
            cp.wait_send()

    shapes = [jax.ShapeDtypeStruct(p.shape, p.dtype) for p in parts]
    return pl.pallas_call(
        body, name=name, in_specs=[ANY] * n, out_specs=[ANY] * n, out_shape=shapes,
        scratch_shapes=[pltpu.SemaphoreType.DMA((n,)), pltpu.SemaphoreType.DMA((n,))],
    )(*parts)


def _pair_swap_rows(parts, name):
    n = len(parts)

    def body(*refs):
        ins, outs = refs[:n], refs[n:2 * n]
        send, recv = refs[2 * n:]
        x, y, c = _place()
        half = ins[0].shape[1] // 2
        theirs = pl.ds(pl.multiple_of((1 - c) * half, 16), half)
        cps = [pltpu.make_async_remote_copy(
            src_ref=ins[a].at[pl.ds(0, 4), theirs], dst_ref=outs[a], send_sem=send.at[a], recv_sem=recv.at[a],
            device_id=(x, y, 1 - c), device_id_type=MESH) for a in range(n)]
        for cp in cps:
            cp.start()
        for cp in cps:
            cp.wait_recv()
        for cp in cps:
            cp.wait_send()

    shapes = [jax.ShapeDtypeStruct((4, p.shape[1] // 2, p.shape[2]), p.dtype) for p in parts]
    return pl.pallas_call(
        body, name=name, in_specs=[ANY] * n, out_specs=[ANY] * n, out_shape=shapes,
        scratch_shapes=[pltpu.SemaphoreType.DMA((n,)), pltpu.SemaphoreType.DMA((n,))],
    )(*parts)


def _pair_assemble(parts, name):
    n = len(parts)

    def body(*refs):
        ins, outs = refs[:n], refs[n:2 * n]
        send, recv, lsem = refs[2 * n:]
        x, y, c = _place()
        half = ins[0].shape[0]
        mine = pl.ds(pl.multiple_of(c * half, 8), half)
        theirs = pl.ds(pl.multiple_of((1 - c) * half, 8), half)
        local = [pltpu.make_async_copy(ins[a], outs[a].at[mine], lsem.at[a]) for a in range(n)]
        push = [pltpu.make_async_remote_copy(
            src_ref=ins[a], dst_ref=outs[a].at[mine], send_sem=send.at[a], recv_sem=recv.at[a],
            device_id=(x, y, 1 - c), device_id_type=MESH) for a in range(n)]
        for cp in local + push:
            cp.start()
        for a in range(n):
            pltpu.make_async_remote_copy(
                src_ref=ins[a], dst_ref=outs[a].at[theirs], send_sem=send.at[a], recv_sem=recv.at[a],
                device_id=(x, y, 1 - c), device_id_type=MESH).wait_recv()
        for cp in push:
            cp.wait_send()
        for cp in local:
            cp.wait()

    shapes = [jax.ShapeDtypeStruct((2 * p.shape[0], p.shape[1]), p.dtype) for p in parts]
    return pl.pallas_call(
        body, name=name, in_specs=[ANY] * n, out_specs=[ANY] * n, out_shape=shapes,
        scratch_shapes=[pltpu.SemaphoreType.DMA((n,)), pltpu.SemaphoreType.DMA((n,)), pltpu.SemaphoreType.DMA((n,))],
    )(*parts)


def _add_blocks(p, q, name):
    n, rows, width = p.shape

    def body(p_ref, q_ref, o_ref):
        o_ref[...] = (p_ref[...].astype(F32) + q_ref[...].astype(F32)).astype(o_ref.dtype)

    blk = pl.BlockSpec((None, rows, width), lambda j: (j, 0, 0))
    return pl.pallas_call(
        body, grid=(n,), name=name, in_specs=[blk, blk], out_specs=blk,
        out_shape=jax.ShapeDtypeStruct(p.shape, p.dtype), compiler_params=_cparams(),
    )(p, q)


def _row_tile(rows, cap):
    best = rows
    for t in range(8, min(cap, rows) + 1, 8):
        if rows % t == 0:
            best = t
    return best if rows % 8 == 0 else rows


def _sum_slots(parts, name):
    n, rows, width = parts.shape
    tr = _row_tile(rows, 352)

    def body(p_ref, o_ref):
        acc = p_ref[0].astype(F32)
        for j in range(1, n):
            acc = acc + p_ref[j].astype(F32)
        o_ref[...] = acc

    return pl.pallas_call(
        body, grid=(rows // tr,), name=name,
        in_specs=[pl.BlockSpec((n, tr, width), lambda i: (0, i, 0))],
        out_specs=pl.BlockSpec((tr, width), lambda i: (i, 0)),
        out_shape=jax.ShapeDtypeStruct((rows, width), F32),
        compiler_params=_cparams(),
    )(parts)


def _adamw_math(g, w, m, v):
    m_new = ADAM_B1 * m + (1.0 - ADAM_B1) * g
    v_new = ADAM_B2 * v + (1.0 - ADAM_B2) * (g * g)
    m_hat = m_new / (1.0 - ADAM_B1 ** ADAM_STEP)
    v_hat = v_new / (1.0 - ADAM_B2 ** ADAM_STEP)
    delta = -ADAM_LR * (m_hat / (jnp.sqrt(v_hat) + ADAM_EPS) + ADAM_WD * w)
    return delta, m_new, v_new


def _adamw(grads, w, m, v, name):
    rows, width = w.shape
    tr = _row_tile(rows, 256 if width <= 1024 else 128)
    ng = len(grads)

    def body(*refs):
        g = refs[0][...]
        for r in refs[1:ng]:
            g = g + r[...]
        w_ref, m_ref, v_ref, g_out, d_out, m_out, v_out = refs[ng:]
        delta, m_new, v_new = _adamw_math(g, w_ref[...], m_ref[...], v_ref[...])
        g_out[...] = g
        d_out[...] = delta
        m_out[...] = m_new
        v_out[...] = v_new

    if rows % 8 == 0 or width % 512:
        blk, steps = pl.BlockSpec((tr, width), lambda i: (i, 0)), rows // tr
    else:
        blk, steps = pl.BlockSpec((rows, 256), lambda i: (0, i)), width // 256
    return pl.pallas_call(
        body, grid=(steps,), name=name,
        in_specs=[blk] * (ng + 3), out_specs=[blk] * 4,
        out_shape=[jax.ShapeDtypeStruct((rows, width), F32)] * 4,
        compiler_params=_cparams(),
    )(*grads, w, m, v)


def _adamw_ada(msgs, dmods, w, m, v):
    rows, width = w.shape
    tr = 128

    def body(c_ref, dm_ref, w_ref, m_ref, v_ref, g_out, d_out, m_out, v_out):
        cv = jnp.concatenate([c_ref[d, 0:1, :] for d in range(8)], axis=0)
        act = cv * _sigmoid(cv)
        g = _tn(act, dm_ref[...], precision=HI)
        delta, m_new, v_new = _adamw_math(g, w_ref[...], m_ref[...], v_ref[...])
        g_out[...] = g
        d_out[...] = delta
        m_out[...] = m_new
        v_out[...] = v_new

    blk = pl.BlockSpec((tr, width), lambda i: (i, 0))
    return pl.pallas_call(
        body, grid=(rows // tr,), name="adamw_w_ada",
        in_specs=[pl.BlockSpec((8, MSG_ROWS, tr), lambda i: (0, 0, i)), pl.BlockSpec((8, width), lambda i: (0, 0)),
                  blk, blk, blk],
        out_specs=[blk] * 4, out_shape=[jax.ShapeDtypeStruct((rows, width), F32)] * 4,
        compiler_params=_cparams(),
    )(msgs, dmods, w, m, v)


def _adamw_small(parts, w, m, v, name):
    n, rows, width = parts.shape

    def body(p_ref, w_ref, m_ref, v_ref, g_out, d_out, m_out, v_out):
        g = p_ref[0]
        for j in range(1, n):
            g = g + p_ref[j]
        delta, m_new, v_new = _adamw_math(g, w_ref[...], m_ref[...], v_ref[...])
        g_out[...] = g
        d_out[...] = delta
        m_out[...] = m_new
        v_out[...] = v_new

    return pl.pallas_call(
        body, name=name, in_specs=[VM] * 4, out_specs=[VM] * 4,
        out_shape=[jax.ShapeDtypeStruct((rows, width), F32)] * 4,
        compiler_params=pltpu.CompilerParams(vmem_limit_bytes=VMEM_LIMIT),
    )(parts, w, m, v)


SMALL_ROWS = 24


def _pad_row(vec, width=D):
    vec = vec.reshape(1, -1)
    return jnp.pad(vec, ((0, 0), (0, width - vec.shape[1])))


def _lanes_4_7(vec4):
    return jnp.zeros((1, 128), F32).at[0, NH:2 * NH].set(vec4.reshape(NH))


def kernel(x, c, w_ada, b_ada, norm_ffn1, ffn1_gate, ffn1_up, ffn1_down, norm_mix, w_in, conv_w, a_log, dt_bias, gdn_norm, pool_w, pool_scale, w_out, norm_ffn2, ffn2_gate, ffn2_up, ffn2_down, final_norm, loss_target, m_w_ada, m_b_ada, m_norm_ffn1, m_ffn1_gate, m_ffn1_up, m_ffn1_down, m_norm_mix, m_w_in, m_conv_w, m_a_log, m_dt_bias, m_gdn_norm, m_pool_w, m_pool_scale, m_w_out, m_norm_ffn2, m_ffn2_gate, m_ffn2_up, m_ffn2_down, m_final_norm, v_w_ada, v_b_ada, v_norm_ffn1, v_ffn1_gate, v_ffn1_up, v_ffn1_down, v_norm_mix, v_w_in, v_conv_w, v_a_log, v_dt_bias, v_gdn_norm, v_pool_w, v_pool_scale, v_w_out, v_norm_ffn2, v_ffn2_gate, v_ffn2_up, v_ffn2_down, v_final_norm):
    xs = x[0]
    tgt = loss_target[0]
    chip = 2 * lax.axis_index("x") + lax.axis_index("y")
    me = 2 * chip + lax.axis_index("c")

    fsh = FF // 4
    block_a = jnp.concatenate([ffn1_gate[0].T, ffn1_up[0].T, ffn1_down[0]], axis=0).astype(BF)
    block_b = jnp.concatenate([ffn2_gate[0].T, ffn2_up[0].T, ffn2_down[0], w_out[0],
                               jnp.pad(w_in[0].T, ((0, WIN_PAD - WIN_SH), (0, 0)))], axis=0).astype(BF)

    msg = jnp.concatenate([jnp.broadcast_to(c, (8, D)), jnp.pad(conv_w[0], ((0, 0), (0, D - 3 * GW // 4))),
                           jnp.zeros((MSG_ROWS - 12, D), F32)], axis=0)
    b_sh = lax.dynamic_slice(b_ada, (0, chip * ADA_SH), (1, ADA_SH))
    msgs, mod4, gath_a = _ada_exchange(msg, w_ada[0], b_sh, block_a)
    mod = mod4[:, 0, :].reshape(NMOD, D)
    mrow = [mod[i:i + 1] for i in range(NMOD)]
    conv_full = jnp.concatenate([msgs[2 * j, 8:12, :3 * GW // 4] for j in range(4)], axis=1)
    alog, dtb = _lanes_4_7(a_log), _lanes_4_7(dt_bias)
    gnm = gdn_norm.reshape(1, DH)
    pwb = pool_w[0].astype(BF)
    psc = pool_scale.reshape(1, PW)
    fin = final_norm.reshape(1, D)

    x1, f1, a1, b1, s1, gath_b = _ffn_fwd(xs, mrow[0], mrow[1], mrow[2], norm_ffn1, gath_a, 0, "ffn1_fwd", block_b)
    wo = gath_b[:, 3 * fsh:3 * fsh + D // 4, :].reshape(D, D)
    win_nat = gath_b[:, 3 * fsh + D // 4:3 * fsh + D // 4 + WIN_SH, :].reshape(DIN, D)
    winT = jnp.concatenate([win_nat[:4 * GW], win_nat[4 * GW + 2 * NH:], win_nat[4 * GW:4 * GW + 2 * NH],
                            jnp.zeros((128 - 2 * NH, D), BF)], axis=0)
    xq, ba, qn, kn, vv, z, pp, bg, gc = _mix_proj(x1, mrow[3], mrow[4], norm_mix, winT, conv_full, alog, dtb)
    x2, mixed, cat, o, sall = _mix_core(x1, mrow[5], qn, kn, vv, z, pp, bg, gc, gnm, pwb, psc, wo)
    x3, f2, a2, b2, s2 = _ffn_fwd(x2, mrow[6], mrow[7], mrow[8], norm_ffn2, gath_b, 0, "ffn2_fwd")
    lpart, dx3, dfin = _loss_head(x3, tgt, fin)
    loss = lax.psum(jnp.sum(lpart), ("x", "y", "c"))

    slots = lambda t: t.reshape(4, t.shape[0] // 4, D)
    dx2, da2, db2, h2, df2, dsh3, dsc3, dgt3, dn3 = _ffn_dgrad(
        dx3, x2, f2, a2, b2, mrow[6], mrow[7], mrow[8], norm_ffn2, gath_b, 0, "ffn2_dgrad")
    gg2, gu2, gd2 = _ffn_wgrad(da2, db2, s2, h2, df2, "ffn2_wgrad")
    dqn, dkn, dvv, dz, dpp, dbg, dgt2, dwo, dpw, dps, dgnm, *landed2 = _mix_core_bwd(
        dx2, mrow[5], mixed, cat, o, sall, qn, kn, vv, z, pp, bg, gc, gnm, pwb, psc, wo,
        scatter=[slots(gg2), slots(gu2), slots(gd2)])
    dx1, dwin, dcw, dal, ddt, dsh2, dsc2, dn2 = _mix_proj_bwd(
        dx2, dqn, dkn, dvv, dz, dpp, dbg, xq, ba, x1, mrow[3], mrow[4], norm_mix, winT, conv_full, alog, dtb)
    dwin_nat = jnp.concatenate([dwin[:4 * GW], dwin[4 * GW + PW:4 * GW + PW + 2 * NH], dwin[4 * GW:4 * GW + PW]], axis=0)
    dwin_sl = jnp.pad(dwin_nat.reshape(4, WIN_SH, D), ((0, 0), (0, WIN_PAD - WIN_SH), (0, 0)))
    dx0, da1, db1, h1, df1, dsh1, dsc1, dgt1, dn1 = _ffn_dgrad(
        dx1, xs, f1, a1, b1, mrow[0], mrow[1], mrow[2], norm_ffn1, gath_a, 0, "ffn1_dgrad")
    red = lambda t: jnp.sum(t, axis=0, keepdims=True)
    small = jnp.concatenate(
        [red(dn1), red(dn2), red(dn3), red(dfin),
         red(dsh1), red(dsc1), red(dgt1), red(dsh2), red(dsc2), red(dgt2), red(dsh3), red(dsc3), red(dgt3),
         _pad_row(red(dps)), _pad_row(red(dgnm)), _pad_row(red(dal)), _pad_row(red(ddt)),
         jnp.sum(dcw, axis=1).reshape(6, D), jnp.zeros((1, D), F32)], axis=0)
    gg1, gu1, gd1, land_wo, land_win, small_all, dpw_all = _ffn_wgrad(
        da1, db1, s1, h1, df1, "ffn1_wgrad", scatter=[slots(dwo), dwin_sl],
        allgather=[small, dpw.reshape(NG * 128, 128)])

    ffn1_blocks = [slots(gg1), slots(gu1), slots(gd1)]
    sibling_share = _pair_swap_rows(ffn1_blocks, "grad_pair_rows")
    own_rows = lax.axis_index("c") * (fsh // 2)
    pair_sums = [_add_blocks(lax.dynamic_slice(g, (0, own_rows, 0), (4, fsh // 2, D)), sh_, "pair_add_" + nm)
                 for g, sh_, nm in zip(ffn1_blocks, sibling_share, ("g1", "u1", "d1"))]
    landed1 = _chip_exchange(pair_sums, "grad_scatter")
    half_sums = [_sum_slots(t, "sum_" + nm) for t, nm in zip(landed1, ("g1", "u1", "d1"))]
    full1 = _pair_assemble(half_sums, "grad_pair_ffn1")

    landed = list(landed2) + [land_wo, land_win]
    psum = [_sum_slots(t, "sum_" + nm) for t, nm in zip(landed, ("g2", "u2", "d2", "wo", "win"))]
    qsum = _pair_exchange(psum, "grad_pair")

    def adamw_t(grads, w, m, v, name, rows):
        res = _adamw([g[:rows] for g in grads], w[0].T, m[0].T, v[0].T, name)
        return [t.T for t in res]

    upd = {}
    upd["ffn1_gate"] = adamw_t([full1[0]], ffn1_gate, m_ffn1_gate, v_ffn1_gate, "adamw_g1", fsh)
    upd["ffn1_up"] = adamw_t([full1[1]], ffn1_up, m_ffn1_up, v_ffn1_up, "adamw_u1", fsh)
    upd["ffn1_down"] = _adamw([full1[2]], ffn1_down[0], m_ffn1_down[0], v_ffn1_down[0], "adamw_d1")
    upd["ffn2_gate"] = adamw_t([psum[0], qsum[0]], ffn2_gate, m_ffn2_gate, v_ffn2_gate, "adamw_g2", fsh)
    upd["ffn2_up"] = adamw_t([psum[1], qsum[1]], ffn2_up, m_ffn2_up, v_ffn2_up, "adamw_u2", fsh)
    upd["ffn2_down"] = _adamw([psum[2], qsum[2]], ffn2_down[0], m_ffn2_down[0], v_ffn2_down[0], "adamw_d2")
    upd["w_out"] = _adamw([psum[3], qsum[3]], w_out[0], m_w_out[0], v_w_out[0], "adamw_wo")
    upd["w_in"] = adamw_t([psum[4], qsum[4]], w_in, m_w_in, v_w_in, "adamw_win", WIN_SH)
    dmods = lax.dynamic_slice(small_all[:, 4:4 + NMOD, :].reshape(8, NMOD * D), (0, chip * ADA_SH), (8, ADA_SH))
    upd["w_ada"] = _adamw_ada(msgs, dmods, w_ada[0], m_w_ada[0], v_w_ada[0])

    def pack_small(nf1, nmx, nf2, fn, bada, psc_, gn_, al_, dt_):
        return jnp.concatenate(
            [nf1.reshape(1, D), nmx.reshape(1, D), nf2.reshape(1, D), fn.reshape(1, D), bada.reshape(NMOD, D),
             _pad_row(psc_), _pad_row(gn_), _pad_row(_lanes_4_7(al_)), _pad_row(_lanes_4_7(dt_)),
             jnp.zeros((7, D), F32)], axis=0)

    ws = pack_small(norm_ffn1, norm_mix, norm_ffn2, final_norm, b_ada, pool_scale, gdn_norm, a_log, dt_bias)
    ms = pack_small(m_norm_ffn1, m_norm_mix, m_norm_ffn2, m_final_norm, m_b_ada, m_pool_scale, m_gdn_norm, m_a_log, m_dt_bias)
    vs = pack_small(v_norm_ffn1, v_norm_mix, v_norm_ffn2, v_final_norm, v_b_ada, v_pool_scale, v_gdn_norm, v_a_log, v_dt_bias)
    sm = _adamw_small(small_all, ws, ms, vs, "adamw_small")
    pw2 = lambda t: t.reshape(NG * 128, 128)
    upd_pw = _adamw_small(dpw_all, pw2(pool_w), pw2(m_pool_w), pw2(v_pool_w), "adamw_pool_w")
    csh = 3 * GW // 4
    gconv = lax.dynamic_slice(sm[0][17:23].reshape(4, 3 * GW), (0, chip * csh), (4, csh))
    upd["conv_w"] = _adamw([gconv], conv_w[0], m_conv_w[0], v_conv_w[0], "adamw_conv")

    def small_out(k):
        t = sm[k]
        return {
            "norm_ffn1": t[0:1], "norm_mix": t[1:2], "norm_ffn2": t[2:3], "final_norm": t[3],
            "b_ada": t[4:4 + NMOD].reshape(1, NMOD * D), "pool_scale": t[13:14, :PW], "gdn_norm": t[14:15, :DH],
            "a_log": t[15:16, NH:2 * NH], "dt_bias": t[16:17, NH:2 * NH],
        }

    order = ["w_ada", "b_ada", "norm_ffn1", "ffn1_gate", "ffn1_up", "ffn1_down", "norm_mix", "w_in", "conv_w", "a_log",
             "dt_bias", "gdn_norm", "pool_w", "pool_scale", "w_out", "norm_ffn2", "ffn2_gate", "ffn2_up", "ffn2_down",
             "final_norm"]
    outs = [loss, dx0[None]]
    for k in range(4):
        smk = small_out(k)
        for nm in order:
            if nm in upd:
                outs.append(upd[nm][k][None])
            elif nm == "pool_w":
                outs.append(upd_pw[k].reshape(1, NG, 128, 128))
            else:
                outs.append(smk[nm])
    return tuple(outs)
```

```python
import functools

import jax
import jax.numpy as jnp
from jax import lax
from jax.experimental import pallas as pl
from jax.experimental.pallas import tpu as pltpu

F32 = jnp.float32
BF = jnp.bfloat16

D = 1024
FF = 2816
FH = FF // 2
NH = 4
DH = 128
GW = NH * DH
CH = 64
PW = 512
NG = 4
POOL_WINDOWS = (2, 4, 8, 16)
HALO = 16
DIN = 4 * GW + 2 * NH + PW
DINP = 3 * GW + GW + PW + 128
EPS = 1e-6
ADAM_LR, ADAM_B1, ADAM_B2, ADAM_EPS, ADAM_WD, ADAM_STEP = 0.001, 0.9, 0.999, 1e-08, 0.01, 10

VMEM_LIMIT = 60 * 1024 * 1024

NT_DIMS = (((1,), (1,)), ((), ()))
TN_DIMS = (((0,), (0,)), ((), ()))
HI = lax.Precision.HIGHEST


def _nt(a, b, **kw):
    return lax.dot_general(a, b, NT_DIMS, preferred_element_type=F32, **kw)


def _tn(a, b, **kw):
    return lax.dot_general(a, b, TN_DIMS, preferred_element_type=F32, **kw)


def _nn(a, b, **kw):
    return jnp.dot(a, b, preferred_element_type=F32, **kw)


def _cparams(sem=("arbitrary",), **kw):
    return pltpu.CompilerParams(dimension_semantics=sem, vmem_limit_bytes=VMEM_LIMIT, **kw)


def _const_spec(shape):
    nd = len(shape)
    return pl.BlockSpec(shape, lambda *_: (0,) * nd, pipeline_mode=pl.Buffered(1))


def _row_spec(tm, width):
    return pl.BlockSpec((tm, width), lambda i: (i, 0))


def _sum8(v):
    return jnp.sum(v.reshape(v.shape[0] // 8, 8, v.shape[1]), axis=0)


def _sigmoid(v):
    return 0.5 * jnp.tanh(0.5 * v) + 0.5


def _tile(T, cap=512):
    return min(cap, T)


def _norm_mod_fwd(xv, gain, shift, scale):
    r = lax.rsqrt(jnp.mean(xv * xv, axis=-1, keepdims=True) + EPS)
    n = xv * r
    y = n * gain
    return n, r, y, y * (1.0 + scale) + shift


def _norm_mod_bwd(dh, n, r, y, gain, scale):
    dy = dh * (1.0 + scale)
    dn = dy * gain
    dx = r * (dn - n * jnp.mean(dn * n, axis=-1, keepdims=True))
    return dx, _sum8(dh), _sum8(dh * y), _sum8(dy * n)


def _ffn_wspecs(k0):
    return [pl.BlockSpec((4, FF // 4, D), lambda i, k=k0 + n: (0, k, 0), pipeline_mode=pl.Buffered(1)) for n in range(3)]


def _half(w_ref, j):
    return w_ref[2 * j:2 * j + 2].reshape(FH, D)


def _ffn_fwd(x, shift, scale, gate, gain, wall, k0, name, gather_block=None, loss=None):
    T = x.shape[0]
    tm = _tile(T)
    nt = T // tm
    comm = gather_block is not None
    assert not (comm and loss)

    def body(*refs):
        x_ref, sh_ref, sc_ref, gt_ref, gn_ref, wg_ref, wu_ref, wd_ref = refs[:8]
        if comm:
            xo_ref, f_ref, sa_ref, ga_ref, s_ref = refs[9:14]
            plan = _SplitGather(refs[8], refs[14], *refs[15:])
            _comm_begin(pl.program_id(0), plan, nt - 1)
        elif loss:
            t_ref, fg_ref, ls_ref, xo_ref, dfg_ref, f_ref, sa_ref, ga_ref, s_ref = refs[8:17]
        else:
            xo_ref, f_ref, sa_ref, ga_ref, s_ref = refs[8:13]
        xv = x_ref[...]
        _, _, _, h = _norm_mod_fwd(xv, gn_ref[...], sh_ref[...], sc_ref[...])
        hb = h.astype(BF)
        facc = jnp.zeros((tm, D), F32)
        for j in range(2):
            cols = slice(j * FH, (j + 1) * FH)
            a = _nt(hb, _half(wg_ref, j))
            b = _nt(hb, _half(wu_ref, j))
            sig = _sigmoid(a)
            sa = a * sig
            sa_ref[:, cols] = sa.astype(BF)
            ga_ref[:, cols] = (b * (sig * (1.0 + a * (1.0 - sig)))).astype(BF)
            s = (sa * b).astype(BF)
            s_ref[:, cols] = s
            facc = facc + _nn(s, _half(wd_ref, j))
        f_ref[...] = facc
        xo = xv + 0.5 * gt_ref[...] * facc
        if loss:
            i = pl.program_id(0)
            lsum, dxo, dfg = _loss_math(xo, t_ref[...], fg_ref[...])
            xo_ref[...] = dxo

            @pl.when(i == 0)
            def _():
                ls_ref[...] = lsum
                dfg_ref[...] = dfg

            @pl.when(i > 0)
            def _():
                ls_ref[...] += lsum
                dfg_ref[...] += dfg
        else:
            xo_ref[...] = xo
        if comm:
            _comm_end(pl.program_id(0), plan, nt - 1)

    vec = _const_spec((1, D))
    tok = jax.ShapeDtypeStruct((T, D), F32)
    extra_in, extra_specs = ([gather_block], [ANY]) if comm else (list(loss), [_row_spec(tm, D), vec]) if loss else ([], [])
    head_specs, head_shapes = [_row_spec(tm, D)], [tok]
    if loss:
        head_specs = [pl.BlockSpec((8, 128), lambda i: (0, 0)), _row_spec(tm, D), pl.BlockSpec((8, D), lambda i: (0, 0))]
        head_shapes = [jax.ShapeDtypeStruct((8, 128), F32), tok, jax.ShapeDtypeStruct((8, D), F32)]
    return pl.pallas_call(
        body, grid=(nt,), name=name,
        in_specs=[_row_spec(tm, D), vec, vec, vec, vec] + _ffn_wspecs(k0) + extra_specs,
        out_specs=head_specs + [_row_spec(tm, D)] + [_row_spec(tm, FF)] * 3 + [ANY] * comm,
        out_shape=head_shapes + [tok] + [jax.ShapeDtypeStruct((T, FF), BF)] * 3
        + ([jax.ShapeDtypeStruct((4,) + gather_block.shape, gather_block.dtype)] if comm else []),
        scratch_shapes=_gather_sems() if comm else [],
        compiler_params=_cparams(),
    )(x, shift, scale, gate, gain, wall, wall, wall, *extra_in)


def _ffn_dgrad(dxo, x, f, sa, ga, shift, scale, gate, gain, wall, k0, name):
    T = x.shape[0]
    tm = _tile(T)
    nt = T // tm
    wspecs = _ffn_wspecs(k0)
    vec = _const_spec((1, D))
    acc = pl.BlockSpec((8, D), lambda i: (0, 0))
    accs = jax.ShapeDtypeStruct((8, D), F32)

    def body_a(dxo_ref, f_ref, sa_ref, ga_ref, gt_ref, wd_ref, da_ref, db_ref, df_ref, dgt_ref):
        i = pl.program_id(0)
        dxo_v = dxo_ref[...]
        dgate = _sum8(0.5 * f_ref[...] * dxo_v)
        dfb = (0.5 * gt_ref[...] * dxo_v).astype(BF)
        df_ref[...] = dfb
        for j in range(2):
            cols = slice(j * FH, (j + 1) * FH)
            ds = _nt(dfb, _half(wd_ref, j))
            da_ref[:, cols] = (ds * ga_ref[:, cols].astype(F32)).astype(BF)
            db_ref[:, cols] = (ds * sa_ref[:, cols].astype(F32)).astype(BF)

        @pl.when(i == 0)
        def _():
            dgt_ref[...] = dgate

        @pl.when(i > 0)
        def _():
            dgt_ref[...] += dgate

    wide = jax.ShapeDtypeStruct((T, FF), BF)
    da, db, df, dgt = pl.pallas_call(
        body_a, grid=(nt,), name=name + "_a",
        in_specs=[_row_spec(tm, D), _row_spec(tm, D), _row_spec(tm, FF), _row_spec(tm, FF), vec, wspecs[2]],
        out_specs=[_row_spec(tm, FF), _row_spec(tm, FF), _row_spec(tm, D), acc],
        out_shape=[wide, wide, jax.ShapeDtypeStruct((T, D), BF), accs],
        compiler_params=_cparams(),
    )(dxo, f, sa, ga, gate, wall)

    def body_b(dxo_ref, x_ref, da_ref, db_ref, sh_ref, sc_ref, gn_ref, wg_ref, wu_ref,
               dx_ref, h_ref, dsh_ref, dsc_ref, dgn_ref):
        i = pl.program_id(0)
        gain_v, scale_v = gn_ref[...], sc_ref[...]
        n, r, y, h = _norm_mod_fwd(x_ref[...], gain_v, sh_ref[...], scale_v)
        h_ref[...] = h.astype(BF)
        dh = _nn(da_ref[...], wg_ref[...].reshape(FF, D)) + _nn(db_ref[...], wu_ref[...].reshape(FF, D))
        dxn, dsh, dsc, dgn = _norm_mod_bwd(dh, n, r, y, gain_v, scale_v)
        dx_ref[...] = dxo_ref[...] + dxn

        @pl.when(i == 0)
        def _():
            dsh_ref[...] = dsh
            dsc_ref[...] = dsc
            dgn_ref[...] = dgn

        @pl.when(i > 0)
        def _():
            dsh_ref[...] += dsh
            dsc_ref[...] += dsc
            dgn_ref[...] += dgn

    dx, h, dsh, dsc, dgn = pl.pallas_call(
        body_b, grid=(nt,), name=name + "_b",
        in_specs=[_row_spec(tm, D), _row_spec(tm, D), _row_spec(tm, FF), _row_spec(tm, FF), vec, vec, vec,
                  wspecs[0], wspecs[1]],
        out_specs=[_row_spec(tm, D), _row_spec(tm, D), acc, acc, acc],
        out_shape=[jax.ShapeDtypeStruct((T, D), F32), jax.ShapeDtypeStruct((T, D), BF), accs, accs, accs],
        compiler_params=_cparams(),
    )(dxo, x, da, db, shift, scale, gain, wall, wall)
    return dx, da, db, h, df, dsh, dsc, dgt, dgn


def _ffn_wgrad(da, db, s, h, df, name, scatter=(), allgather=()):
    T = h.shape[0]
    tk = _tile(T, 512)
    nk = T // tk
    nj = 2
    groups = [g for g in (
        (list(scatter), _Scatter, _scatter_sems, lambda t: t.shape),
        (list(allgather), _AllGather, _allgather_sems, lambda t: (8,) + t.shape),
    ) if g[0]]
    extra = [t for g in groups for t in g[0]]
    ne = len(extra)

    def body(*refs):
        da_ref, db_ref, s_ref, h_ref, df_ref = refs[:5]
        og_ref, ou_ref, od_ref = refs[5 + ne:8 + ne]
        ag, au, ad = refs[8 + 2 * ne:11 + 2 * ne]
        k = pl.program_id(1)
        step = pl.program_id(0) * nk + k
        if ne:
            plans, at, sem_at = [], 0, 11 + 2 * ne
            for arrs, make, _, _ in groups:
                n = len(arrs)
                plans.append(make(refs[5 + at:5 + at + n], refs[8 + ne + at:8 + ne + at + n], *refs[sem_at:sem_at + 3]))
                at, sem_at = at + n, sem_at + 3
            plan = _Plans(plans)
            _comm_begin(step, plan, nj * nk - 1)

        @pl.when(k == 0)
        def _():
            for acc in (ag, au, ad):
                acc[...] = jnp.zeros((FH, D), F32)

        ag[...] += _tn(da_ref[...], h_ref[...])
        au[...] += _tn(db_ref[...], h_ref[...])
        ad[...] += _tn(s_ref[...], df_ref[...])

        @pl.when(k == nk - 1)
        def _():
            og_ref[...] = ag[...].astype(BF)
            ou_ref[...] = au[...].astype(BF)
            od_ref[...] = ad[...].astype(BF)

        if ne:
            _comm_end(step, plan, nj * nk - 1)

    colblk = pl.BlockSpec((tk, FH), lambda j, k: (k, j))
    rowblk = pl.BlockSpec((tk, D), lambda j, k: (k, 0))
    outblk = pl.BlockSpec((FH, D), lambda j, k: (j, 0))
    outs = jax.ShapeDtypeStruct((nj * FH, D), BF)
    return pl.pallas_call(
        body, grid=(nj, nk), name=name,
        in_specs=[colblk, colblk, colblk, rowblk, rowblk] + [ANY] * ne,
        out_specs=[outblk, outblk, outblk] + [ANY] * ne,
        out_shape=[outs, outs, outs] + [jax.ShapeDtypeStruct(shape(t), t.dtype) for g in groups for t, shape in
                                        ((t, g[3]) for t in g[0])],
        scratch_shapes=[pltpu.VMEM((FH, D), F32)] * 3 + [sm for g in groups for sm in g[2](len(g[0]))],
        compiler_params=_cparams(("arbitrary", "arbitrary")),
    )(da, db, s, h, df, *extra)


def _loss_math(xv, target, gain_v):
    r = lax.rsqrt(jnp.mean(xv * xv, axis=-1, keepdims=True) + EPS)
    n = xv * r
    err = n * gain_v - target
    e2 = err * err
    part = e2[:, 0:128]
    for q in range(1, D // 128):
        part = part + e2[:, q * 128:(q + 1) * 128]
    dy = err * (1.0 / D)
    dn = dy * gain_v
    return _sum8(part) * (0.5 / D), r * (dn - n * jnp.mean(dn * n, axis=-1, keepdims=True)), _sum8(dy * n)


def _seg_cumsum(v, row_in_chunk, reverse=False):
    n = v.shape[0]
    s = 1
    while s < CH:
        if reverse:
            moved = pltpu.roll(v, n - s, 0)
            ok = row_in_chunk < CH - s
        else:
            moved = pltpu.roll(v, s, 0)
            ok = row_in_chunk >= s
        v = v + jnp.where(ok, moved, 0.0)
        s *= 2
    return v


def _conv_silu(xq_ext_ref, cw_ref, tm):
    c = cw_ref[0:1, :] * xq_ext_ref[pl.ds(5, tm), :]
    for j in range(1, 4):
        c = c + cw_ref[j:j + 1, :] * xq_ext_ref[pl.ds(5 + j, tm), :]
    return c, _sigmoid(c)


def _gates(ba, alog, dtb, lane):
    beta = _sigmoid(ba)
    arg = ba + dtb
    softplus = jnp.maximum(arg, 0.0) + jnp.log(1.0 + jnp.exp(-jnp.abs(arg)))
    g = -jnp.exp(alog) * softplus
    return jnp.where(lane < NH, beta, 0.0), jnp.where((lane >= NH) & (lane < 2 * NH), g, 0.0), _sigmoid(arg)


def _mix_proj(x, shift, scale, gain, winT, conv_w, alog, dtb):
    T = x.shape[0]
    tm = _tile(T)

    def body(x_ref, sh_ref, sc_ref, gn_ref, w_ref, cw_ref, al_ref, dt_ref,
             xq_ref, ba_ref, qn_ref, kn_ref, v_ref, z_ref, p_ref, bg_ref, gc_ref, ext):
        i = pl.program_id(0)
        _, _, _, h = _norm_mod_fwd(x_ref[...], gn_ref[...], sh_ref[...], sc_ref[...])
        hb = h.astype(BF)

        @pl.when(i == 0)
        def _():
            ext[pl.ds(0, 8), :] = jnp.zeros((8, 3 * GW), F32)

        xq = _nt(hb, w_ref[pl.ds(0, 3 * GW), :])
        xq_ref[...] = xq
        ext[pl.ds(8, tm), :] = xq
        z_ref[...] = _nt(hb, w_ref[pl.ds(3 * GW, GW), :])
        p_ref[...] = _nt(hb, w_ref[pl.ds(4 * GW, PW), :])
        ba = _nt(hb, w_ref[pl.ds(4 * GW + PW, 128), :])
        ba_ref[...] = ba

        c, sg = _conv_silu(ext, cw_ref, tm)
        ext[pl.ds(0, 8), :] = ext[pl.ds(tm, 8), :]
        qt = c * sg
        for hd in range(NH):
            cq = slice(hd * DH, (hd + 1) * DH)
            ck = slice(GW + hd * DH, GW + (hd + 1) * DH)
            qh, kh = qt[:, cq], qt[:, ck]
            qn_ref[:, cq] = qh * (lax.rsqrt(jnp.sum(qh * qh, axis=-1, keepdims=True) + EPS) * DH ** -0.5)
            kn_ref[:, cq] = kh * lax.rsqrt(jnp.sum(kh * kh, axis=-1, keepdims=True) + EPS)
        v_ref[...] = qt[:, 2 * GW:3 * GW]

        lane = lax.broadcasted_iota(jnp.int32, (tm, 128), 1)
        row = lax.broadcasted_iota(jnp.int32, (tm, 128), 0) % CH
        beta, g, _ = _gates(ba, al_ref[...], dt_ref[...], lane)
        bg_ref[...] = beta + g
        gc_ref[...] = _seg_cumsum(g, row)

    wide = lambda w: _row_spec(tm, w)
    shp = lambda w: jax.ShapeDtypeStruct((T, w), F32)
    return pl.pallas_call(
        body, grid=(T // tm,), name="mix_proj",
        in_specs=[wide(D), _const_spec((1, D)), _const_spec((1, D)), _const_spec((1, D)), _const_spec((DINP, D)),
                  _const_spec((4, 3 * GW)), _const_spec((1, 128)), _const_spec((1, 128))],
        out_specs=[wide(3 * GW), wide(128), wide(GW), wide(GW), wide(GW), wide(GW), wide(PW), wide(128), wide(128)],
        out_shape=[shp(3 * GW), shp(128), shp(GW), shp(GW), shp(GW), shp(GW), shp(PW), shp(128), shp(128)],
        scratch_shapes=[pltpu.VMEM((tm + 8, 3 * GW), F32)],
        compiler_params=_cparams(),
    )(x, shift, scale, gain, winT, conv_w, alog, dtb)


R2 = 2 * CH
TRI_PREC = None


def _tmm(fn, a, b):
    if TRI_PREC is None:
        return fn(a.astype(BF), b.astype(BF))
    return fn(a, b, precision=TRI_PREC)


def _pair_consts():
    ii = lax.broadcasted_iota(jnp.int32, (R2, R2), 0)
    jj = lax.broadcasted_iota(jnp.int32, (R2, R2), 1)
    same = (ii < CH) == (jj < CH)
    r = lax.broadcasted_iota(jnp.int32, (R2, 1), 0)
    return dict(causal=same & (ii >= jj), strict=same & (ii > jj), eye=(ii == jj).astype(F32), rowA=r < CH,
                last=(r == CH - 1) | (r == R2 - 1),
                rowS=lax.broadcasted_iota(jnp.int32, (2 * DH, 1), 0) < DH)


def _tri_inverse_many(ms, eye):
    pws = [-m for m in ms]
    ts = [eye + p for p in pws]
    for _ in range(5):
        pws = [_tmm(_nn, p, p) for p in pws]
        ts = [_tmm(_nn, t, eye + p) for t, p in zip(ts, pws)]
    return ts


def _egl_rows(gl):
    egl = jnp.exp(gl)
    return egl, jnp.concatenate([jnp.broadcast_to(egl[0:1], (DH, 1)), jnp.broadcast_to(egl[CH:CH + 1], (DH, 1))], axis=0)


def _pair_intra(items, cn):
    causal, rowA = cn["causal"], cn["rowA"]

    def bd(t):
        return jnp.concatenate([jnp.where(rowA, t, 0.0), jnp.where(rowA, 0.0, t)], axis=1).astype(BF)

    outs = []
    for q, k, v, beta, gcv, gl in items:
        gc_b = jnp.broadcast_to(gcv, (R2, R2))
        gam = jnp.where(causal, jnp.exp(jnp.where(causal, gc_b - gc_b.T, 0.0)), 0.0)
        kb = k * beta
        kbf = k.astype(BF)
        P = _nt(kb.astype(BF), kbf)
        QK = _nt(q.astype(BF), kbf)
        E = jnp.exp(gcv)
        outs.append(dict(gam=gam, kb=kb, vb=v * beta, P=P, QK=QK, E=E, Fd=jnp.exp(gl - gcv), kbE=kb * E,
                         Q=QK * gam, qE_bd=bd(q * E)))
    tms = _tri_inverse_many([jnp.where(cn["strict"], d["P"] * d["gam"], 0.0) for d in outs], cn["eye"])
    for d, tm_, (q, k, v, beta, gcv, gl) in zip(outs, tms, items):
        d["Tm"] = tm_
        d["u"] = _tmm(_nn, tm_, d["vb"])
        d["w_bd"] = bd(_tmm(_nn, tm_, d["kbE"]))
        d["kF_bd"] = bd(k * d["Fd"])
    return outs


def _pair_scan(it, S, egl_st):
    Sb = S.astype(BF)
    vn = it["u"] - _nn(it["w_bd"], Sb)
    vnb = vn.astype(BF)
    o = _nn(it["qE_bd"], Sb) + _nn(it["Q"].astype(BF), vnb)
    return vnb, o, S * egl_st + _tn(it["kF_bd"], vnb)


def _pair_forward(q, k, v, beta, gcv, gl, S, cn):
    fw = _pair_intra([(q, k, v, beta, gcv, gl)], cn)[0]
    fw["egl"], fw["egl_st"] = _egl_rows(gl)
    fw["vnb"], fw["o"], fw["S_new"] = _pair_scan(fw, S, fw["egl_st"])
    return fw


def _stack_heads(ref, rows, pair):
    return jnp.concatenate([ref[rows, (2 * pair) * DH:(2 * pair + 1) * DH],
                            ref[rows, (2 * pair + 1) * DH:(2 * pair + 2) * DH]], axis=0)


def _stack_cols(val, lane_a, lane_b, bcast_rows=None):
    a, b = val[:, lane_a:lane_a + 1], val[:, lane_b:lane_b + 1]
    if bcast_rows:
        a, b = jnp.broadcast_to(a, (bcast_rows, 1)), jnp.broadcast_to(b, (bcast_rows, 1))
    return jnp.concatenate([a, b], axis=0)


def _pool_windows(ext, tm, reverse):
    n = tm + HALO
    outs = []
    for gi in range(NG):
        a = ext[:, gi * 128:(gi + 1) * 128]
        s = 1
        while s < POOL_WINDOWS[gi]:
            a = a + pltpu.roll(a, (n - s) if reverse else s, 0)
            s *= 2
        outs.append(a[0:tm] if reverse else a[HALO:HALO + tm])
    return jnp.concatenate(outs, axis=1)


def _pool_count(tm, tile_index):
    t1 = (lax.broadcasted_iota(jnp.int32, (tm, PW), 0) + tile_index * tm + 1).astype(F32)
    win = jnp.concatenate([jnp.full((tm, 128), float(w), F32) for w in POOL_WINDOWS], axis=1)
    return 1.0 / jnp.minimum(t1, win)


def _chunk_item(qn_ref, kn_ref, v_ref, bg_ref, gc_ref, c, pr):
    r0 = pl.multiple_of(c * CH, CH)
    rows = pl.ds(r0, CH)
    bgv = bg_ref[rows, :]
    gcv_all = gc_ref[rows, :]
    gl_all = gc_ref[pl.ds(r0 + CH - 1, 1), :]
    ha, hb = 2 * pr, 2 * pr + 1
    return (_stack_heads(qn_ref, rows, pr), _stack_heads(kn_ref, rows, pr), _stack_heads(v_ref, rows, pr),
            _stack_cols(bgv, ha, hb), _stack_cols(gcv_all, NH + ha, NH + hb), _stack_cols(gl_all, NH + ha, NH + hb, CH))


CHUNK_GROUP = 4


def _mix_core(x, gate, qn, kn, v, z, p, bg, gc, gnorm, pool_w, pool_scale, w_out):
    T = x.shape[0]
    tm = _tile(T)
    nc = tm // CH
    cg = CHUNK_GROUP if nc % CHUNK_GROUP == 0 else 1
    npb = nc * (NH // 2)

    def body(x_ref, gt_ref, qn_ref, kn_ref, v_ref, z_ref, p_ref, bg_ref, gc_ref, gnm_ref, pw_ref, ps_ref, wo_ref,
             xo_ref, mx_ref, cat_ref, o_ref, sall_ref, S_scr, pext, u_s, w_s, qe_s, kf_s, q_s):
        i = pl.program_id(0)

        @pl.when(i == 0)
        def _():
            S_scr[...] = jnp.zeros((NH * DH, DH), F32)
            pext[pl.ds(0, HALO), :] = jnp.zeros((HALO, PW), F32)

        cn = _pair_consts()

        def intra(g, carry):
            idx = [(g * cg + dc, pr) for dc in range(cg) for pr in range(NH // 2)]
            res = _pair_intra([_chunk_item(qn_ref, kn_ref, v_ref, bg_ref, gc_ref, c, pr) for c, pr in idx], cn)
            for (c, pr), d in zip(idx, res):
                pi = c * (NH // 2) + pr
                u_s[pi] = d["u"]
                w_s[pi] = d["w_bd"]
                qe_s[pi] = d["qE_bd"]
                kf_s[pi] = d["kF_bd"]
                q_s[pi] = d["Q"].astype(BF)
            return carry

        lax.fori_loop(0, nc // cg, intra, 0)

        def scan(c, carry):
            r0 = pl.multiple_of(c * CH, CH)
            rows = pl.ds(r0, CH)
            gl_all = gc_ref[pl.ds(r0 + CH - 1, 1), :]
            for pr in range(NH // 2):
                ha, hb = 2 * pr, 2 * pr + 1
                pi = c * (NH // 2) + pr
                S = S_scr[pl.ds(pr * 2 * DH, 2 * DH), :]
                sall_ref[c, ha:hb + 1] = S.reshape(2, DH, DH)
                _, egl_st = _egl_rows(_stack_cols(gl_all, NH + ha, NH + hb, CH))
                it = dict(u=u_s[pi], w_bd=w_s[pi], qE_bd=qe_s[pi], kF_bd=kf_s[pi], Q=q_s[pi])
                _, o, S_new = _pair_scan(it, S, egl_st)
                o_ref[rows, ha * DH:(ha + 1) * DH] = o[0:CH]
                o_ref[rows, hb * DH:(hb + 1) * DH] = o[CH:R2]
                S_scr[pl.ds(pr * 2 * DH, 2 * DH), :] = S_new
            return carry

        lax.fori_loop(0, nc, scan, 0)

        for hd in range(NH):
            cols = slice(hd * DH, (hd + 1) * DH)
            oh = o_ref[:, cols]
            zh = z_ref[:, cols]
            r = lax.rsqrt(jnp.mean(oh * oh, axis=-1, keepdims=True) + EPS)
            cat_ref[:, cols] = (oh * r * gnm_ref[...] * (zh * _sigmoid(zh))).astype(BF)

        pv = p_ref[...]
        pext[pl.ds(HALO, tm), :] = pv
        pooled = _pool_windows(pext[...], tm, False) * _pool_count(tm, i) - pv
        pext[pl.ds(0, HALO), :] = pext[pl.ds(tm, HALO), :]
        for gi in range(NG):
            cols = slice(gi * 128, (gi + 1) * 128)
            pm = _nn(pooled[:, cols].astype(BF), pw_ref[gi])
            cat_ref[:, GW + gi * 128:GW + (gi + 1) * 128] = (pm * ps_ref[:, cols]).astype(BF)

        mixed = _nn(cat_ref[...], wo_ref[...])
        mx_ref[...] = mixed
        xo_ref[...] = x_ref[...] + gt_ref[...] * mixed

    wide = lambda w: _row_spec(tm, w)
    return pl.pallas_call(
        body, grid=(T // tm,), name="mix_core",
        in_specs=[wide(D), _const_spec((1, D)), wide(GW), wide(GW), wide(GW), wide(GW), wide(PW), wide(128), wide(128),
                  _const_spec((1, DH)), _const_spec((NG, 128, 128)), _const_spec((1, PW)), _const_spec((D, D))],
        out_specs=[wide(D), wide(D), wide(D), wide(GW), pl.BlockSpec((nc, NH, DH, DH), lambda i: (i, 0, 0, 0))],
        out_shape=[jax.ShapeDtypeStruct((T, D), F32), jax.ShapeDtypeStruct((T, D), F32),
                   jax.ShapeDtypeStruct((T, D), BF), jax.ShapeDtypeStruct((T, GW), F32),
                   jax.ShapeDtypeStruct((T // CH, NH, DH, DH), F32)],
        scratch_shapes=[pltpu.VMEM((NH * DH, DH), F32), pltpu.VMEM((tm + HALO, PW), F32),
                        pltpu.VMEM((npb, R2, DH), F32), pltpu.VMEM((npb, R2, 2 * DH), BF),
                        pltpu.VMEM((npb, R2, 2 * DH), BF), pltpu.VMEM((npb, R2, 2 * DH), BF),
                        pltpu.VMEM((npb, R2, R2), BF)],
        compiler_params=_cparams(),
    )(x, gate, qn, kn, v, z, p, bg, gc, gnorm, pool_w, pool_scale, w_out)


def _pair_scan_bwd(it, S, dSn, do, egl, egl_st, cn):
    bf = lambda t: t.astype(BF)
    rowA, rowS = cn["rowA"], cn["rowS"]
    sel = lambda t: jnp.where(rowA, t[:, 0:DH], t[:, DH:2 * DH])
    Sb, dSb, dob = bf(S), bf(dSn), bf(do)
    vnb = bf(it["u"] - _nn(it["w_bd"], Sb))
    dvn = _tn(bf(it["Q"]), dob) + _nn(it["kF_bd"], dSb)
    dQ = _nt(dob, vnb)
    dqE = sel(_nt(dob, Sb))
    dkF = sel(_nt(vnb, dSb))
    dvnb = bf(dvn)
    dw = -sel(_nt(dvnb, Sb))
    dS_new = _tn(it["qE_bd"], dob) + egl_st * dSn - _tn(it["w_bd"], dvnb)
    prod = jnp.sum(dSn * S, axis=1, keepdims=True)
    d_egl_a = jnp.sum(jnp.where(rowS, prod, 0.0), axis=0, keepdims=True)
    d_egl_b = jnp.sum(jnp.where(rowS, 0.0, prod), axis=0, keepdims=True)
    return dict(dvn=dvn, dQ=dQ, dqE=dqE, dkF=dkF, dw=dw, degl=jnp.where(rowA, d_egl_a, d_egl_b) * egl), dS_new


def _pair_intra_bwd(items, cn):
    bf = lambda t: t.astype(BF)
    rowA = cn["rowA"]
    for d in items:
        d["kb"] = d["k"] * d["beta"]
        d["E"] = jnp.exp(d["gcv"])
        d["Fd"] = jnp.exp(d["gl"] - d["gcv"])
        d["TmT"] = d["Tm"].T
        d["dvb"] = _tmm(_nn, d["TmT"], d["dvn"])
        d["dkbE"] = _tmm(_nn, d["TmT"], d["dw"])
        dTm = _tmm(_nt, d["dvn"], d["v"] * d["beta"]) + _tmm(_nt, d["dw"], d["kb"] * d["E"])
        d["X"] = _tmm(_nt, dTm, d["Tm"])
    for d in items:
        d["dA"] = -_tmm(_nn, d["TmT"], d["X"])
    outs = []
    for d in items:
        q, k, v, beta, gam = d["q"], d["k"], d["v"], d["beta"], d["gam"]
        N = jnp.where(cn["strict"], d["dA"] * gam, 0.0)
        Rm = d["dQ"] * gam
        Wm = Rm * d["QK"] + N * d["P"]
        dgc = jnp.sum(Wm, axis=1, keepdims=True) - jnp.sum(Wm.T, axis=1, keepdims=True)
        kbf, Nb, Rb = bf(k), bf(N), bf(Rm)
        E, Fd = d["E"], d["Fd"]
        dq = d["dqE"] * E + _nn(Rb, kbf)
        dkb = d["dkbE"] * E + _nn(Nb, kbf)
        dk = d["dkF"] * Fd + _tn(Rb, bf(q)) + _tn(Nb, bf(d["kb"])) + beta * dkb
        dbeta = jnp.sum(dkb * k + d["dvb"] * v, axis=1, keepdims=True)
        dE = jnp.sum(d["dqE"] * q + d["dkbE"] * d["kb"], axis=1, keepdims=True)
        fdf = jnp.sum(d["dkF"] * k, axis=1, keepdims=True) * Fd
        dgl = d["degl"] + jnp.where(rowA, jnp.sum(jnp.where(rowA, fdf, 0.0), axis=0, keepdims=True),
                                    jnp.sum(jnp.where(rowA, 0.0, fdf), axis=0, keepdims=True))
        dgc = dgc + dE * E - fdf + jnp.where(cn["last"], dgl, 0.0)
        outs.append((dq, dk, beta * d["dvb"], dbeta, dgc))
    return outs


def _mix_core_bwd(dxo, gate, mixed, cat, o, sall, qn, kn, v, z, p, bg, gc, gnorm, pool_w, pool_scale, w_out,
                  scatter=()):
    T = dxo.shape[0]
    tm = _tile(T, 256)
    nt = T // tm
    nc = tm // CH
    ns = len(scatter)
    cg = CHUNK_GROUP if nc % CHUNK_GROUP == 0 else 1
    npb = nc * (NH // 2)

    def body(*refs):
        (dx_ref, gt_ref, mx_ref, cat_ref, o_ref, sall_ref, qn_ref, kn_ref, v_ref, z_ref, p_ref, ph_ref, bg_ref,
         gc_ref, gnm_ref, pw_ref, ps_ref, wo_ref) = refs[:18]
        (dq_ref, dk_ref, dv_ref, dz_ref, dp_ref, dbg_ref, dgt_ref, dwo_ref, dpw_ref, dps_ref,
         dgn_ref) = refs[18 + ns:29 + ns]
        dS_scr, pext, yext, do_buf, dwo_acc = refs[29 + 2 * ns:34 + 2 * ns]
        (gam_s, p_s, qk_s, tm_s, dqq_s, u_s, dvn_s, dqe_s, dkf_s, dw_s, w_s, qe_s, kf_s, q_s,
         degl_s) = refs[34 + 2 * ns:49 + 2 * ns]
        i = pl.program_id(0)
        ti = nt - 1 - i
        if ns:
            plan = _Scatter(refs[18:18 + ns], refs[29 + ns:29 + 2 * ns], *refs[49 + 2 * ns:])
            _comm_begin(i, plan, nt - 1)

        @pl.when(i == 0)
        def _():
            dS_scr[...] = jnp.zeros((NH * DH, DH), F32)
            yext[pl.ds(tm, HALO), :] = jnp.zeros((HALO, PW), F32)
            dwo_acc[...] = jnp.zeros((D, D), F32)
            dgt_ref[...] = jnp.zeros((8, D), F32)
            dpw_ref[...] = jnp.zeros((NG, 128, 128), F32)
            dps_ref[...] = jnp.zeros((8, PW), F32)
            dgn_ref[...] = jnp.zeros((8, DH), F32)

        dx2 = dx_ref[...]
        dgt_ref[...] += _sum8(mx_ref[...] * dx2)
        dmix = (gt_ref[...] * dx2).astype(BF)
        dcat = _nt(dmix, wo_ref[...])
        dwo_acc[...] += _tn(cat_ref[...], dmix)

        pv = p_ref[...]
        pext[pl.ds(0, HALO), :] = jnp.where(ti == 0, 0.0, ph_ref[...])
        pext[pl.ds(HALO, tm), :] = pv
        inv_cnt = _pool_count(tm, ti)
        pooled = _pool_windows(pext[...], tm, False) * inv_cnt - pv
        dpooled = []
        for gi in range(NG):
            cols = slice(gi * 128, (gi + 1) * 128)
            pgb = pooled[:, cols].astype(BF)
            pm = _nn(pgb, pw_ref[gi])
            dpo = dcat[:, GW + gi * 128:GW + (gi + 1) * 128]
            dps_ref[:, cols] += _sum8(dpo * pm)
            dpm = (dpo * ps_ref[:, cols]).astype(BF)
            dpooled.append(_nt(dpm, pw_ref[gi]))
            dpw_ref[gi] += _tn(pgb, dpm)
        dpooled = jnp.concatenate(dpooled, axis=1)
        y = dpooled * inv_cnt
        yext[pl.ds(0, tm), :] = y
        dp_ref[...] = _pool_windows(yext[...], tm, True) - dpooled
        yext[pl.ds(tm, HALO), :] = y[0:HALO]

        gnm = gnm_ref[...]
        dgn = jnp.zeros((8, DH), F32)
        for hd in range(NH):
            cols = slice(hd * DH, (hd + 1) * DH)
            oh = o_ref[:, cols]
            zh = z_ref[:, cols]
            r = lax.rsqrt(jnp.mean(oh * oh, axis=-1, keepdims=True) + EPS)
            n = oh * r
            sg = _sigmoid(zh)
            zs = zh * sg
            dgo = dcat[:, cols]
            dgn = dgn + _sum8(dgo * zs * n)
            dn = dgo * zs * gnm
            do_buf[:, cols] = r * (dn - n * jnp.mean(dn * n, axis=-1, keepdims=True))
            dz_ref[:, cols] = dgo * n * gnm * (sg * (1.0 + zh * (1.0 - sg)))
        dgn_ref[...] += dgn

        cn = _pair_consts()
        lane_c = lax.broadcasted_iota(jnp.int32, (CH, 128), 1)

        def intra(g, carry):
            idx = [(g * cg + dc, pr) for dc in range(cg) for pr in range(NH // 2)]
            res = _pair_intra([_chunk_item(qn_ref, kn_ref, v_ref, bg_ref, gc_ref, c, pr) for c, pr in idx], cn)
            for (c, pr), d in zip(idx, res):
                pi = c * (NH // 2) + pr
                gam_s[pi], p_s[pi], qk_s[pi], tm_s[pi], u_s[pi] = d["gam"], d["P"], d["QK"], d["Tm"], d["u"]
                w_s[pi], qe_s[pi], kf_s[pi], q_s[pi] = d["w_bd"], d["qE_bd"], d["kF_bd"], d["Q"].astype(BF)
            return carry

        lax.fori_loop(0, nc // cg, intra, 0)

        def scan(cc, carry):
            c = nc - 1 - cc
            r0 = pl.multiple_of(c * CH, CH)
            rows = pl.ds(r0, CH)
            gl_all = gc_ref[pl.ds(r0 + CH - 1, 1), :]
            for pr in range(NH // 2):
                ha, hb = 2 * pr, 2 * pr + 1
                pi = c * (NH // 2) + pr
                srows = pl.ds(pr * 2 * DH, 2 * DH)
                S = sall_ref[c, ha:hb + 1].reshape(2 * DH, DH)
                egl, egl_st = _egl_rows(_stack_cols(gl_all, NH + ha, NH + hb, CH))
                it = dict(u=u_s[pi], w_bd=w_s[pi], qE_bd=qe_s[pi], kF_bd=kf_s[pi], Q=q_s[pi])
                g, dS_new = _pair_scan_bwd(it, S, dS_scr[srows, :], _stack_heads(do_buf, rows, pr), egl, egl_st, cn)
                dvn_s[pi], dqq_s[pi], dqe_s[pi], dkf_s[pi], dw_s[pi] = g["dvn"], g["dQ"], g["dqE"], g["dkF"], g["dw"]
                degl_s[pi] = g["degl"]
                dS_scr[srows, :] = dS_new
            return carry

        lax.fori_loop(0, nc, scan, 0)

        def intra_bwd(g, carry):
            idx = [(g * cg + dc, pr) for dc in range(cg) for pr in range(NH // 2)]
            items = []
            for c, pr in idx:
                pi = c * (NH // 2) + pr
                q, k, vv, beta, gcv, gl = _chunk_item(qn_ref, kn_ref, v_ref, bg_ref, gc_ref, c, pr)
                items.append(dict(q=q, k=k, v=vv, beta=beta, gcv=gcv, gl=gl, gam=gam_s[pi], P=p_s[pi], QK=qk_s[pi],
                                  Tm=tm_s[pi], dvn=dvn_s[pi], dQ=dqq_s[pi], dqE=dqe_s[pi], dkF=dkf_s[pi], dw=dw_s[pi],
                                  degl=degl_s[pi]))
            res = _pair_intra_bwd(items, cn)
            for dc in range(cg):
                c = g * cg + dc
                rows = pl.ds(pl.multiple_of(c * CH, CH), CH)
                dbg = jnp.zeros((CH, 128), F32)
                for pr in range(NH // 2):
                    dq, dk, dv, dbeta, dgc = res[dc * (NH // 2) + pr]
                    for hd, half in ((2 * pr, slice(0, CH)), (2 * pr + 1, slice(CH, R2))):
                        cols = slice(hd * DH, (hd + 1) * DH)
                        dq_ref[rows, cols] = dq[half]
                        dk_ref[rows, cols] = dk[half]
                        dv_ref[rows, cols] = dv[half]
                        dbg = dbg + jnp.where(lane_c == hd, dbeta[half], 0.0) + jnp.where(lane_c == NH + hd, dgc[half], 0.0)
                dbg_ref[rows, :] = dbg
            return carry

        lax.fori_loop(0, nc // cg, intra_bwd, 0)

        lane = lax.broadcasted_iota(jnp.int32, (tm, 128), 1)
        row = lax.broadcasted_iota(jnp.int32, (tm, 128), 0) % CH
        dbg_all = dbg_ref[...]
        dg = _seg_cumsum(jnp.where(lane >= NH, dbg_all, 0.0), row, reverse=True)
        dbg_ref[...] = jnp.where(lane < NH, dbg_all, dg)

        @pl.when(i == nt - 1)
        def _():
            dwo_ref[...] = dwo_acc[...].astype(BF)

        if ns:
            _comm_end(i, plan, nt - 1)

    rev = lambda w: pl.BlockSpec((tm, w), lambda i: (nt - 1 - i, 0))
    halo = pl.BlockSpec((HALO, PW), lambda i: (jnp.maximum((nt - 1 - i) * (tm // HALO) - 1, 0), 0))
    shp = lambda w: jax.ShapeDtypeStruct((T, w), F32)
    fix = lambda *s: pl.BlockSpec(s, lambda i: (0,) * len(s))
    return pl.pallas_call(
        body, grid=(nt,), name="mix_core_bwd",
        in_specs=[rev(D), _const_spec((1, D)), rev(D), rev(D), rev(GW),
                  pl.BlockSpec((nc, NH, DH, DH), lambda i: (nt - 1 - i, 0, 0, 0)),
                  rev(GW), rev(GW), rev(GW), rev(GW), rev(PW), halo, rev(128), rev(128),
                  _const_spec((1, DH)), _const_spec((NG, 128, 128)), _const_spec((1, PW)), _const_spec((D, D))]
        + [ANY] * ns,
        out_specs=[rev(GW), rev(GW), rev(GW), rev(GW), rev(PW), rev(128),
                   fix(8, D), fix(D, D), fix(NG, 128, 128), fix(8, PW), fix(8, DH)] + [ANY] * ns,
        out_shape=[shp(GW), shp(GW), shp(GW), shp(GW), shp(PW), shp(128),
                   jax.ShapeDtypeStruct((8, D), F32), jax.ShapeDtypeStruct((D, D), BF),
                   jax.ShapeDtypeStruct((NG, 128, 128), F32), jax.ShapeDtypeStruct((8, PW), F32),
                   jax.ShapeDtypeStruct((8, DH), F32)] + [jax.ShapeDtypeStruct(t.shape, t.dtype) for t in scatter],
        scratch_shapes=[pltpu.VMEM((NH * DH, DH), F32), pltpu.VMEM((tm + HALO, PW), F32),
                        pltpu.VMEM((tm + HALO, PW), F32), pltpu.VMEM((tm, GW), F32), pltpu.VMEM((D, D), F32)]
        + [pltpu.VMEM((npb, R2, R2), F32)] * 5 + [pltpu.VMEM((npb, R2, DH), F32)] * 5
        + [pltpu.VMEM((npb, R2, 2 * DH), BF)] * 3 + [pltpu.VMEM((npb, R2, R2), BF), pltpu.VMEM((npb, R2, 1), F32)]
        + (_scatter_sems(ns) if ns else []),
        compiler_params=_cparams(),
    )(dxo, gate, mixed, cat, o, sall, qn, kn, v, z, p, p, bg, gc, gnorm, pool_w, pool_scale, w_out, *scatter)


def _mix_proj_bwd(dxo, dqn, dkn, dv, dz, dp, dbg, xq, ba, x, shift, scale, gain, winT, conv_w, alog, dtb):
    T = x.shape[0]
    tm = min(256, T)
    nt = T // tm
    W3 = 3 * GW

    def body(dxo_ref, dqn_ref, dkn_ref, dv_ref, dz_ref, dp_ref, dbg_ref, xq_ref, xh_ref, ba_ref, x_ref, sh_ref, sc_ref,
             gn_ref, w_ref, cw_ref, al_ref, dt_ref,
             dx_ref, dw_ref, dcw_ref, dal_ref, ddt_ref, dsh_ref, dsc_ref, dgn_ref,
             ext, dcext, dproj, dw_acc):
        i = pl.program_id(0)
        ti = nt - 1 - i

        @pl.when(i == 0)
        def _():
            dcext[pl.ds(tm, 8), :] = jnp.zeros((8, W3), F32)
            dw_acc[...] = jnp.zeros((DINP, D), F32)
            dcw_ref[...] = jnp.zeros((4, 8, W3), F32)
            dal_ref[...] = jnp.zeros((8, 128), F32)
            ddt_ref[...] = jnp.zeros((8, 128), F32)
            dsh_ref[...] = jnp.zeros((8, D), F32)
            dsc_ref[...] = jnp.zeros((8, D), F32)
            dgn_ref[...] = jnp.zeros((8, D), F32)

        ext[pl.ds(0, 8), :] = jnp.where(ti == 0, 0.0, xh_ref[...])
        ext[pl.ds(8, tm), :] = xq_ref[...]
        c, sg = _conv_silu(ext, cw_ref, tm)
        qt = c * sg
        dsilu = sg * (1.0 + c * (1.0 - sg))
        for hd in range(NH):
            for part, dref, mult in ((0, dqn_ref, DH ** -0.5), (1, dkn_ref, 1.0)):
                cols = slice(part * GW + hd * DH, part * GW + (hd + 1) * DH)
                xh = qt[:, cols]
                rr = lax.rsqrt(jnp.sum(xh * xh, axis=-1, keepdims=True) + EPS)
                unit = xh * rr
                du = dref[:, hd * DH:(hd + 1) * DH] * mult
                dxh = rr * (du - unit * jnp.sum(du * unit, axis=-1, keepdims=True))
                dcext[pl.ds(0, tm), cols] = dxh * dsilu[:, cols]
        dcext[pl.ds(0, tm), 2 * GW:W3] = dv_ref[...] * dsilu[:, 2 * GW:W3]
        dc = dcext[pl.ds(0, tm), :]
        dxq = jnp.zeros((tm, W3), F32)
        for j in range(4):
            dcw_ref[j] += _sum8(dc * ext[pl.ds(5 + j, tm), :])
            dxq = dxq + cw_ref[j:j + 1, :] * dcext[pl.ds(3 - j, tm), :]
        dcext[pl.ds(tm, 8), :] = dc[0:8]
        lane = lax.broadcasted_iota(jnp.int32, (tm, 128), 1)
        bav = ba_ref[...]
        beta, g, sarg = _gates(bav, al_ref[...], dt_ref[...], lane)
        dbg_v = dbg_ref[...]
        is_g = (lane >= NH) & (lane < 2 * NH)
        dbraw = jnp.where(lane < NH, dbg_v * beta * (1.0 - beta), 0.0)
        daraw = jnp.where(is_g, dbg_v * (-jnp.exp(al_ref[...])) * sarg, 0.0)
        dal_ref[...] += _sum8(jnp.where(is_g, dbg_v * g, 0.0))
        ddt_ref[...] += _sum8(daraw)
        dproj[:, 0:W3] = dxq.astype(BF)
        dproj[:, W3:W3 + GW] = dz_ref[...].astype(BF)
        dproj[:, W3 + GW:W3 + GW + PW] = dp_ref[...].astype(BF)
        dproj[:, W3 + GW + PW:DINP] = (dbraw + daraw).astype(BF)
        gain_v, scale_v = gn_ref[...], sc_ref[...]
        n, r, y, h = _norm_mod_fwd(x_ref[...], gain_v, sh_ref[...], scale_v)
        dpj = dproj[...]
        dh = _nn(dpj, w_ref[...])
        dw_acc[...] += _tn(dpj, h.astype(BF))
        dxn, dsh, dsc, dgn = _norm_mod_bwd(dh, n, r, y, gain_v, scale_v)
        dx_ref[...] = dxo_ref[...] + dxn
        dsh_ref[...] += dsh
        dsc_ref[...] += dsc
        dgn_ref[...] += dgn

        @pl.when(i == nt - 1)
        def _():
            dw_ref[...] = dw_acc[...].astype(BF)

    rev = lambda w: pl.BlockSpec((tm, w), lambda i: (nt - 1 - i, 0))
    halo = pl.BlockSpec((8, W3), lambda i: (jnp.maximum((nt - 1 - i) * (tm // 8) - 1, 0), 0))
    fix = lambda *s: pl.BlockSpec(s, lambda i: (0,) * len(s))
    vec = _const_spec((1, D))
    return pl.pallas_call(
        body, grid=(nt,), name="mix_proj_bwd",
        in_specs=[rev(D), rev(GW), rev(GW), rev(GW), rev(GW), rev(PW), rev(128), rev(W3), halo, rev(128), rev(D),
                  vec, vec, vec, _const_spec((DINP, D)), _const_spec((4, W3)), _const_spec((1, 128)),
                  _const_spec((1, 128))],
        out_specs=[rev(D), fix(DINP, D), fix(4, 8, W3), fix(8, 128), fix(8, 128), fix(8, D), fix(8, D), fix(8, D)],
        out_shape=[jax.ShapeDtypeStruct((T, D), F32), jax.ShapeDtypeStruct((DINP, D), BF),
                   jax.ShapeDtypeStruct((4, 8, W3), F32), jax.ShapeDtypeStruct((8, 128), F32),
                   jax.ShapeDtypeStruct((8, 128), F32), jax.ShapeDtypeStruct((8, D), F32),
                   jax.ShapeDtypeStruct((8, D), F32), jax.ShapeDtypeStruct((8, D), F32)],
        scratch_shapes=[pltpu.VMEM((tm + 8, W3), F32), pltpu.VMEM((tm + 8, W3), F32), pltpu.VMEM((tm, DINP), BF),
                        pltpu.VMEM((DINP, D), F32)],
        compiler_params=_cparams(),
    )(dxo, dqn, dkn, dv, dz, dp, dbg, xq, xq, ba, x, shift, scale, gain, winT, conv_w, alog, dtb)


MESH = pl.DeviceIdType.MESH
CHIP_RELS = ((1, 0), (0, 1), (1, 1))
DEV_RELS = tuple((dx, dy, dc) for dx in (0, 1) for dy in (0, 1) for dc in (0, 1) if (dx, dy, dc) != (0, 0, 0))
NMOD = 9
ADA_SH = NMOD * D // 4
WIN_SH = DIN // 4
WIN_PAD = 672
MSG_ROWS = 16
ANY = pl.BlockSpec(memory_space=pl.ANY)
VM = pl.BlockSpec(memory_space=pltpu.VMEM)


def _place():
    x, y, c = lax.axis_index("x"), lax.axis_index("y"), lax.axis_index("c")
    return x, y, c


class _SplitGather:
    N_SEMS = (3, 3, 3, 3, 1)

    def __init__(self, src, dst, ici_s, ici_r, d2d_s, d2d_r, lsem):
        self.src, self.dst = src, dst
        self.sems = (ici_s, ici_r, d2d_s, d2d_r)
        self.x, self.y, self.c = _place()
        self.myj = 2 * self.x + self.y
        half = src.shape[0] // 2
        self.mine = pl.ds(pl.multiple_of(self.c * half, 16), half)
        self.other = pl.ds(pl.multiple_of((1 - self.c) * half, 16), half)
        self.local = pltpu.make_async_copy(src, dst.at[self.myj], lsem.at[0])

    def _ici(self, k, slot):
        dx, dy = CHIP_RELS[k]
        return pltpu.make_async_remote_copy(
            src_ref=self.src.at[self.mine], dst_ref=self.dst.at[slot, self.mine], send_sem=self.sems[0].at[k],
            recv_sem=self.sems[1].at[k], device_id=(self.x ^ dx, self.y ^ dy, self.c), device_id_type=MESH)

    def _d2d(self, k, rows):
        dx, dy = CHIP_RELS[k]
        blk = self.dst.at[2 * (self.x ^ dx) + (self.y ^ dy), rows]
        return pltpu.make_async_remote_copy(
            src_ref=blk, dst_ref=blk, send_sem=self.sems[2].at[k], recv_sem=self.sems[3].at[k],
            device_id=(self.x, self.y, 1 - self.c), device_id_type=MESH)

    def start(self):
        self.local.start()
        for k in range(3):
            self._ici(k, self.myj).start()

    def forward(self):
        for k, (dx, dy) in enumerate(CHIP_RELS):
            self._ici(k, 2 * (self.x ^ dx) + (self.y ^ dy)).wait_recv()
            self._d2d(k, self.mine).start()

    def finish(self):
        for k in range(3):
            self._d2d(k, self.other).wait_recv()
        for k in range(3):
            self._d2d(k, self.mine).wait_send()
            self._ici(k, self.myj).wait_send()
        self.local.wait()


class _Scatter:
    def __init__(self, ins, outs, send, recv, lsem):
        self.ins, self.outs, self.send, self.recv, self.lsem = ins, outs, send, recv, lsem
        self.x, self.y, self.c = _place()
        self.myj = 2 * self.x + self.y

    def _copy(self, a, k, landing):
        dx, dy = CHIP_RELS[k]
        pj = 2 * (self.x ^ dx) + (self.y ^ dy)
        return pltpu.make_async_remote_copy(
            src_ref=self.ins[a].at[pj], dst_ref=self.outs[a].at[pj if landing else self.myj],
            send_sem=self.send.at[a, k], recv_sem=self.recv.at[a, k],
            device_id=(self.x ^ dx, self.y ^ dy, self.c), device_id_type=MESH)

    def _local(self, a):
        return pltpu.make_async_copy(self.ins[a].at[self.myj], self.outs[a].at[self.myj], self.lsem.at[a])

    def start(self):
        for a in range(len(self.ins)):
            self._local(a).start()
            for k in range(3):
                self._copy(a, k, False).start()

    def finish(self):
        for a in range(len(self.ins)):
            for k in range(3):
                self._copy(a, k, True).wait_recv()
            for k in range(3):
                self._copy(a, k, False).wait_send()
            self._local(a).wait()


def _scatter_sems(n):
    return [pltpu.SemaphoreType.DMA((n, 3)), pltpu.SemaphoreType.DMA((n, 3)), pltpu.SemaphoreType.DMA((n,))]


def _gather_sems():
    return [pltpu.SemaphoreType.DMA((k,)) for k in _SplitGather.N_SEMS]


def _comm_begin(i, plan, last):
    @pl.when(i == 0)
    def _():
        plan.start()

    if hasattr(plan, "forward"):
        @pl.when(i == max(last - 3, 0))
        def _():
            plan.forward()


def _comm_end(i, plan, last):
    @pl.when(i == last)
    def _():
        plan.finish()


def _ada_exchange(msg, w_ada, b_ada, wblock):
    def body(msg_ref, w_ref, b_ref, wb_ref, all_ref, mod_ref, wg_ref, modp, send1, recv1, send2, recv2, lsem, *gsems):
        x, y, c = _place()
        me = 4 * x + 2 * y + c
        own = pltpu.make_async_copy(msg_ref, all_ref.at[me], lsem.at[0])
        own.start()

        def gather(k, rel, slot):
            dx, dy, dc = rel
            return pltpu.make_async_remote_copy(
                src_ref=msg_ref, dst_ref=all_ref.at[slot], send_sem=send1.at[k], recv_sem=recv1.at[k],
                device_id=(x ^ dx, y ^ dy, c ^ dc), device_id_type=MESH)

        for k, rel in enumerate(DEV_RELS):
            gather(k, rel, me).start()
        for k, (dx, dy, dc) in enumerate(DEV_RELS):
            gather(k, (dx, dy, dc), 4 * (x ^ dx) + 2 * (y ^ dy) + (c ^ dc)).wait_recv()
        for k, rel in enumerate(DEV_RELS):
            gather(k, rel, me).wait_send()
        own.wait()
        wgather = _SplitGather(wb_ref, wg_ref, *gsems)
        wgather.start()

        for d in range(8):
            cv = all_ref[d, 0:8, :]
            act = cv * _sigmoid(cv)
            modp[d] = _nn(act, w_ref[...], precision=HI) + b_ref[...]

        myj = 2 * x + y
        keep = pltpu.make_async_copy(modp.at[me], mod_ref.at[myj], lsem.at[1])
        keep.start()

        def scatter(k, rel):
            dx, dy = rel
            return pltpu.make_async_remote_copy(
                src_ref=modp.at[4 * (x ^ dx) + 2 * (y ^ dy) + c], dst_ref=mod_ref.at[myj],
                send_sem=send2.at[k], recv_sem=recv2.at[k], device_id=(x ^ dx, y ^ dy, c), device_id_type=MESH)

        def landed(k, rel):
            dx, dy = rel
            return pltpu.make_async_remote_copy(
                src_ref=modp.at[me], dst_ref=mod_ref.at[2 * (x ^ dx) + (y ^ dy)],
                send_sem=send2.at[k], recv_sem=recv2.at[k], device_id=(x ^ dx, y ^ dy, c), device_id_type=MESH)

        for k, rel in enumerate(CHIP_RELS):
            scatter(k, rel).start()
        for k, rel in enumerate(CHIP_RELS):
            landed(k, rel).wait_recv()
        for k, rel in enumerate(CHIP_RELS):
            scatter(k, rel).wait_send()
        keep.wait()
        wgather.forward()
        wgather.finish()

    return pl.pallas_call(
        body, name="ada_exchange", in_specs=[VM, VM, VM, ANY], out_specs=[VM, VM, ANY],
        out_shape=[jax.ShapeDtypeStruct((8, MSG_ROWS, D), F32), jax.ShapeDtypeStruct((4, 8, ADA_SH), F32),
                   jax.ShapeDtypeStruct((4,) + wblock.shape, wblock.dtype)],
        scratch_shapes=[pltpu.VMEM((8, 8, ADA_SH), F32), pltpu.SemaphoreType.DMA((7,)), pltpu.SemaphoreType.DMA((7,)),
                        pltpu.SemaphoreType.DMA((3,)), pltpu.SemaphoreType.DMA((3,)), pltpu.SemaphoreType.DMA((2,))]
        + _gather_sems(),
        compiler_params=pltpu.CompilerParams(vmem_limit_bytes=VMEM_LIMIT),
    )(msg, w_ada, b_ada, wblock)


def _chip_exchange(parts, name):
    n = len(parts)

    def body(*refs):
        plan = _Scatter(refs[:n], refs[n:2 * n], *refs[2 * n:])
        plan.start()
        plan.finish()

    return pl.pallas_call(
        body, name=name, in_specs=[ANY] * n, out_specs=[ANY] * n,
        out_shape=[jax.ShapeDtypeStruct(p.shape, p.dtype) for p in parts], scratch_shapes=_scatter_sems(n),
    )(*parts)


class _AllGather:
    def __init__(self, ins, outs, send, recv, lsem):
        self.ins, self.outs, self.send, self.recv, self.lsem = ins, outs, send, recv, lsem
        self.x, self.y, self.c = _place()
        self.me = 4 * self.x + 2 * self.y + self.c

    def _copy(self, a, k, landing):
        dx, dy, dc = DEV_RELS[k]
        peer = 4 * (self.x ^ dx) + 2 * (self.y ^ dy) + (self.c ^ dc)
        return pltpu.make_async_remote_copy(
            src_ref=self.ins[a], dst_ref=self.outs[a].at[peer if landing else self.me],
            send_sem=self.send.at[a, k], recv_sem=self.recv.at[a, k],
            device_id=(self.x ^ dx, self.y ^ dy, self.c ^ dc), device_id_type=MESH)

    def _local(self, a):
        return pltpu.make_async_copy(self.ins[a], self.outs[a].at[self.me], self.lsem.at[a])

    def start(self):
        for a in range(len(self.ins)):
            self._local(a).start()
            for k in range(7):
                self._copy(a, k, False).start()

    def finish(self):
        for a in range(len(self.ins)):
            for k in range(7):
                self._copy(a, k, True).wait_recv()
            for k in range(7):
                self._copy(a, k, False).wait_send()
            self._local(a).wait()


def _allgather_sems(n):
    return [pltpu.SemaphoreType.DMA((n, 7)), pltpu.SemaphoreType.DMA((n, 7)), pltpu.SemaphoreType.DMA((n,))]


class _Plans:
    def __init__(self, plans):
        self.plans = plans

    def start(self):
        for p in self.plans:
            p.start()

    def finish(self):
        for p in self.plans:
            p.finish()


def _pair_exchange(parts, name):
    n = len(parts)

    def body(*refs):
        ins, outs = refs[:n], refs[n:2 * n]
        send, recv = refs[2 * n:]
        x, y, c = _place()
        cps = [pltpu.make_async_remote_copy(
            src_ref=ins[a], dst_ref=outs[a], send_sem=send.at[a], recv_sem=recv.at[a],
            device_id=(x, y, 1 - c), device_id_type=MESH) for a in range(n)]
        for cp in cps:
            cp.start()
        for cp in cps:
            cp.wait_recv()
        for cp in cps:
            cp.wait_send()

    shapes = [jax.ShapeDtypeStruct(p.shape, p.dtype) for p in parts]
    return pl.pallas_call(
        body, name=name, in_specs=[ANY] * n, out_specs=[ANY] * n, out_shape=shapes,
        scratch_shapes=[pltpu.SemaphoreType.DMA((n,)), pltpu.SemaphoreType.DMA((n,))],
    )(*parts)


def _pair_swap_rows(parts, name):
    n = len(parts)

    def body(*refs):
        ins, outs = refs[:n], refs[n:2 * n]
        send, recv = refs[2 * n:]
        x, y, c = _place()
        half = ins[0].shape[1] // 2
        theirs = pl.ds(pl.multiple_of((1 - c) * half, 16), half)
        cps = [pltpu.make_async_remote_copy(
            src_ref=ins[a].at[pl.ds(0, 4), theirs], dst_ref=outs[a], send_sem=send.at[a], recv_sem=recv.at[a],
            device_id=(x, y, 1 - c), device_id_type=MESH) for a in range(n)]
        for cp in cps:
            cp.start()
        for cp in cps:
            cp.wait_recv()
        for cp in cps:
            cp.wait_send()

    shapes = [jax.ShapeDtypeStruct((4, p.shape[1] // 2, p.shape[2]), p.dtype) for p in parts]
    return pl.pallas_call(
        body, name=name, in_specs=[ANY] * n, out_specs=[ANY] * n, out_shape=shapes,
        scratch_shapes=[pltpu.SemaphoreType.DMA((n,)), pltpu.SemaphoreType.DMA((n,))],
    )(*parts)


def _add_blocks(p, q, name):
    n, rows, width = p.shape

    def body(p_ref, q_ref, o_ref):
        o_ref[...] = (p_ref[...].astype(F32) + q_ref[...].astype(F32)).astype(o_ref.dtype)

    blk = pl.BlockSpec((None, rows, width), lambda j: (j, 0, 0))
    return pl.pallas_call(
        body, grid=(n,), name=name, in_specs=[blk, blk], out_specs=blk,
        out_shape=jax.ShapeDtypeStruct(p.shape, p.dtype), compiler_params=_cparams(),
    )(p, q)


def _row_tile(rows, cap):
    best = rows
    for t in range(8, min(cap, rows) + 1, 8):
        if rows % t == 0:
            best = t
    return best if rows % 8 == 0 else rows


def _sum_slots(parts, name):
    n, rows, width = parts.shape
    tr = _row_tile(rows, 352)

    def body(p_ref, o_ref):
        acc = p_ref[0].astype(F32)
        for j in range(1, n):
            acc = acc + p_ref[j].astype(F32)
        o_ref[...] = acc

    return pl.pallas_call(
        body, grid=(rows // tr,), name=name,
        in_specs=[pl.BlockSpec((n, tr, width), lambda i: (0, i, 0))],
        out_specs=pl.BlockSpec((tr, width), lambda i: (i, 0)),
        out_shape=jax.ShapeDtypeStruct((rows, width), F32),
        compiler_params=_cparams(),
    )(parts)


def _adamw_math(g, w, m, v):
    m_new = ADAM_B1 * m + (1.0 - ADAM_B1) * g
    v_new = ADAM_B2 * v + (1.0 - ADAM_B2) * (g * g)
    m_hat = m_new / (1.0 - ADAM_B1 ** ADAM_STEP)
    v_hat = v_new / (1.0 - ADAM_B2 ** ADAM_STEP)
    delta = -ADAM_LR * (m_hat / (jnp.sqrt(v_hat) + ADAM_EPS) + ADAM_WD * w)
    return delta, m_new, v_new


def _adamw(grads, w, m, v, name):
    rows, width = w.shape
    tr = _row_tile(rows, 256 if width <= 1024 else 128)
    ng = len(grads)

    def body(*refs):
        g = refs[0][...]
        for r in refs[1:ng]:
            g = g + r[...]
        w_ref, m_ref, v_ref, g_out, d_out, m_out, v_out = refs[ng:]
        delta, m_new, v_new = _adamw_math(g, w_ref[...], m_ref[...], v_ref[...])
        g_out[...] = g
        d_out[...] = delta
        m_out[...] = m_new
        v_out[...] = v_new

    if rows % 8 == 0 or width % 512:
        blk, steps = pl.BlockSpec((tr, width), lambda i: (i, 0)), rows // tr
    else:
        blk, steps = pl.BlockSpec((rows, 256), lambda i: (0, i)), width // 256
    return pl.pallas_call(
        body, grid=(steps,), name=name,
        in_specs=[blk] * (ng + 3), out_specs=[blk] * 4,
        out_shape=[jax.ShapeDtypeStruct((rows, width), F32)] * 4,
        compiler_params=_cparams(),
    )(*grads, w, m, v)


def _adamw_ada(msgs, dmods, w, m, v):
    rows, width = w.shape
    tr = 128

    def body(c_ref, dm_ref, w_ref, m_ref, v_ref, g_out, d_out, m_out, v_out):
        cv = jnp.concatenate([c_ref[d, 0:1, :] for d in range(8)], axis=0)
        act = cv * _sigmoid(cv)
        g = _tn(act, dm_ref[...], precision=HI)
        delta, m_new, v_new = _adamw_math(g, w_ref[...], m_ref[...], v_ref[...])
        g_out[...] = g
        d_out[...] = delta
        m_out[...] = m_new
        v_out[...] = v_new

    blk = pl.BlockSpec((tr, width), lambda i: (i, 0))
    return pl.pallas_call(
        body, grid=(rows // tr,), name="adamw_w_ada",
        in_specs=[pl.BlockSpec((8, MSG_ROWS, tr), lambda i: (0, 0, i)), pl.BlockSpec((8, width), lambda i: (0, 0)),
                  blk, blk, blk],
        out_specs=[blk] * 4, out_shape=[jax.ShapeDtypeStruct((rows, width), F32)] * 4,
        compiler_params=_cparams(),
    )(msgs, dmods, w, m, v)


def _adamw_small(parts, w, m, v, name):
    n, rows, width = parts.shape

    def body(p_ref, w_ref, m_ref, v_ref, g_out, d_out, m_out, v_out):
        g = p_ref[0]
        for j in range(1, n):
            g = g + p_ref[j]
        delta, m_new, v_new = _adamw_math(g, w_ref[...], m_ref[...], v_ref[...])
        g_out[...] = g
        d_out[...] = delta
        m_out[...] = m_new
        v_out[...] = v_new

    return pl.pallas_call(
        body, name=name, in_specs=[VM] * 4, out_specs=[VM] * 4,
        out_shape=[jax.ShapeDtypeStruct((rows, width), F32)] * 4,
        compiler_params=pltpu.CompilerParams(vmem_limit_bytes=VMEM_LIMIT),
    )(parts, w, m, v)


SMALL_ROWS = 24


def _pad_row(vec, width=D):
    vec = vec.reshape(1, -1)
    return jnp.pad(vec, ((0, 0), (0, width - vec.shape[1])))


def _lanes_4_7(vec4):
    return jnp.zeros((1, 128), F32).at[0, NH:2 * NH].set(vec4.reshape(NH))


def kernel(x, c, w_ada, b_ada, norm_ffn1, ffn1_gate, ffn1_up, ffn1_down, norm_mix, w_in, conv_w, a_log, dt_bias, gdn_norm, pool_w, pool_scale, w_out, norm_ffn2, ffn2_gate, ffn2_up, ffn2_down, final_norm, loss_target, m_w_ada, m_b_ada, m_norm_ffn1, m_ffn1_gate, m_ffn1_up, m_ffn1_down, m_norm_mix, m_w_in, m_conv_w, m_a_log, m_dt_bias, m_gdn_norm, m_pool_w, m_pool_scale, m_w_out, m_norm_ffn2, m_ffn2_gate, m_ffn2_up, m_ffn2_down, m_final_norm, v_w_ada, v_b_ada, v_norm_ffn1, v_ffn1_gate, v_ffn1_up, v_ffn1_down, v_norm_mix, v_w_in, v_conv_w, v_a_log, v_dt_bias, v_gdn_norm, v_pool_w, v_pool_scale, v_w_out, v_norm_ffn2, v_ffn2_gate, v_ffn2_up, v_ffn2_down, v_final_norm):
    xs = x[0]
    tgt = loss_target[0]
    chip = 2 * lax.axis_index("x") + lax.axis_index("y")
    me = 2 * chip + lax.axis_index("c")

    fsh = FF // 4
    block_a = jnp.concatenate([ffn1_gate[0].T, ffn1_up[0].T, ffn1_down[0]], axis=0).astype(BF)
    block_b = jnp.concatenate([ffn2_gate[0].T, ffn2_up[0].T, ffn2_down[0], w_out[0],
                               jnp.pad(w_in[0].T, ((0, WIN_PAD - WIN_SH), (0, 0)))], axis=0).astype(BF)

    msg = jnp.concatenate([jnp.broadcast_to(c, (8, D)), jnp.pad(conv_w[0], ((0, 0), (0, D - 3 * GW // 4))),
                           jnp.zeros((MSG_ROWS - 12, D), F32)], axis=0)
    b_sh = lax.dynamic_slice(b_ada, (0, chip * ADA_SH), (1, ADA_SH))
    msgs, mod4, gath_a = _ada_exchange(msg, w_ada[0], b_sh, block_a)
    mod = mod4[:, 0, :].reshape(NMOD, D)
    mrow = [mod[i:i + 1] for i in range(NMOD)]
    conv_full = jnp.concatenate([msgs[2 * j, 8:12, :3 * GW // 4] for j in range(4)], axis=1)
    alog, dtb = _lanes_4_7(a_log), _lanes_4_7(dt_bias)
    gnm = gdn_norm.reshape(1, DH)
    pwb = pool_w[0].astype(BF)
    psc = pool_scale.reshape(1, PW)
    fin = final_norm.reshape(1, D)

    x1, f1, a1, b1, s1, gath_b = _ffn_fwd(xs, mrow[0], mrow[1], mrow[2], norm_ffn1, gath_a, 0, "ffn1_fwd", block_b)
    wo = gath_b[:, 3 * fsh:3 * fsh + D // 4, :].reshape(D, D)
    win_nat = gath_b[:, 3 * fsh + D // 4:3 * fsh + D // 4 + WIN_SH, :].reshape(DIN, D)
    winT = jnp.concatenate([win_nat[:4 * GW], win_nat[4 * GW + 2 * NH:], win_nat[4 * GW:4 * GW + 2 * NH],
                            jnp.zeros((128 - 2 * NH, D), BF)], axis=0)
    xq, ba, qn, kn, vv, z, pp, bg, gc = _mix_proj(x1, mrow[3], mrow[4], norm_mix, winT, conv_full, alog, dtb)
    x2, mixed, cat, o, sall = _mix_core(x1, mrow[5], qn, kn, vv, z, pp, bg, gc, gnm, pwb, psc, wo)
    lpart, dx3, dfin, f2, a2, b2, s2 = _ffn_fwd(x2, mrow[6], mrow[7], mrow[8], norm_ffn2, gath_b, 0, "ffn2_fwd",
                                                loss=(tgt, fin))
    loss = lax.psum(jnp.sum(lpart), ("x", "y", "c"))

    slots = lambda t: t.reshape(4, t.shape[0] // 4, D)
    dx2, da2, db2, h2, df2, dsh3, dsc3, dgt3, dn3 = _ffn_dgrad(
        dx3, x2, f2, a2, b2, mrow[6], mrow[7], mrow[8], norm_ffn2, gath_b, 0, "ffn2_dgrad")
    gg2, gu2, gd2 = _ffn_wgrad(da2, db2, s2, h2, df2, "ffn2_wgrad")
    dqn, dkn, dvv, dz, dpp, dbg, dgt2, dwo, dpw, dps, dgnm, *landed2 = _mix_core_bwd(
        dx2, mrow[5], mixed, cat, o, sall, qn, kn, vv, z, pp, bg, gc, gnm, pwb, psc, wo,
        scatter=[slots(gg2), slots(gu2), slots(gd2)])
    dx1, dwin, dcw, dal, ddt, dsh2, dsc2, dn2 = _mix_proj_bwd(
        dx2, dqn, dkn, dvv, dz, dpp, dbg, xq, ba, x1, mrow[3], mrow[4], norm_mix, winT, conv_full, alog, dtb)
    dwin_nat = jnp.concatenate([dwin[:4 * GW], dwin[4 * GW + PW:4 * GW + PW + 2 * NH], dwin[4 * GW:4 * GW + PW]], axis=0)
    dwin_sl = jnp.pad(dwin_nat.reshape(4, WIN_SH, D), ((0, 0), (0, WIN_PAD - WIN_SH), (0, 0)))
    dx0, da1, db1, h1, df1, dsh1, dsc1, dgt1, dn1 = _ffn_dgrad(
        dx1, xs, f1, a1, b1, mrow[0], mrow[1], mrow[2], norm_ffn1, gath_a, 0, "ffn1_dgrad")
    red = lambda t: jnp.sum(t, axis=0, keepdims=True)
    small = jnp.concatenate(
        [red(dn1), red(dn2), red(dn3), red(dfin),
         red(dsh1), red(dsc1), red(dgt1), red(dsh2), red(dsc2), red(dgt2), red(dsh3), red(dsc3), red(dgt3),
         _pad_row(red(dps)), _pad_row(red(dgnm)), _pad_row(red(dal)), _pad_row(red(ddt)),
         jnp.sum(dcw, axis=1).reshape(6, D), jnp.zeros((1, D), F32)], axis=0)
    gg1, gu1, gd1, land_wo, land_win, small_all, dpw_all = _ffn_wgrad(
        da1, db1, s1, h1, df1, "ffn1_wgrad", scatter=[slots(dwo), dwin_sl],
        allgather=[small, dpw.reshape(NG * 128, 128)])

    ffn1_blocks = [slots(gg1), slots(gu1), slots(gd1)]
    sibling_share = _pair_swap_rows(ffn1_blocks, "grad_pair_rows")
    own_rows = lax.axis_index("c") * (fsh // 2)
    pair_sums = [_add_blocks(lax.dynamic_slice(g, (0, own_rows, 0), (4, fsh // 2, D)), sh_, "pair_add_" + nm)
                 for g, sh_, nm in zip(ffn1_blocks, sibling_share, ("g1", "u1", "d1"))]
    landed1 = _chip_exchange(pair_sums, "grad_scatter")
    half_sums = [_sum_slots(t, "sum_" + nm) for t, nm in zip(landed1, ("g1", "u1", "d1"))]
    other_sums = _pair_exchange(half_sums, "grad_pair_ffn1")
    other_rows = (1 - lax.axis_index("c")) * (fsh // 2)
    full1 = [lax.dynamic_update_slice(lax.dynamic_update_slice(jnp.zeros((fsh, D), F32), mine_, (own_rows, 0)),
                                      theirs_, (other_rows, 0)) for mine_, theirs_ in zip(half_sums, other_sums)]

    landed = list(landed2) + [land_wo, land_win]
    psum = [_sum_slots(t, "sum_" + nm) for t, nm in zip(landed, ("g2", "u2", "d2", "wo", "win"))]
    qsum = _pair_exchange(psum, "grad_pair")

    def adamw_t(grads, w, m, v, name, rows):
        res = _adamw([g[:rows] for g in grads], w[0].T, m[0].T, v[0].T, name)
        return [t.T for t in res]

    upd = {}
    upd["ffn1_gate"] = adamw_t([full1[0]], ffn1_gate, m_ffn1_gate, v_ffn1_gate, "adamw_g1", fsh)
    upd["ffn1_up"] = adamw_t([full1[1]], ffn1_up, m_ffn1_up, v_ffn1_up, "adamw_u1", fsh)
    upd["ffn1_down"] = _adamw([full1[2]], ffn1_down[0], m_ffn1_down[0], v_ffn1_down[0], "adamw_d1")
    upd["ffn2_gate"] = adamw_t([psum[0], qsum[0]], ffn2_gate, m_ffn2_gate, v_ffn2_gate, "adamw_g2", fsh)
    upd["ffn2_up"] = adamw_t([psum[1], qsum[1]], ffn2_up, m_ffn2_up, v_ffn2_up, "adamw_u2", fsh)
    upd["ffn2_down"] = _adamw([psum[2], qsum[2]], ffn2_down[0], m_ffn2_down[0], v_ffn2_down[0], "adamw_d2")
    upd["w_out"] = _adamw([psum[3], qsum[3]], w_out[0], m_w_out[0], v_w_out[0], "adamw_wo")
    upd["w_in"] = adamw_t([psum[4], qsum[4]], w_in, m_w_in, v_w_in, "adamw_win", WIN_SH)
    dmods = lax.dynamic_slice(small_all[:, 4:4 + NMOD, :].reshape(8, NMOD * D), (0, chip * ADA_SH), (8, ADA_SH))
    upd["w_ada"] = _adamw_ada(msgs, dmods, w_ada[0], m_w_ada[0], v_w_ada[0])

    def pack_small(nf1, nmx, nf2, fn, bada, psc_, gn_, al_, dt_):
        return jnp.concatenate(
            [nf1.reshape(1, D), nmx.reshape(1, D), nf2.reshape(1, D), fn.reshape(1, D), bada.reshape(NMOD, D),
             _pad_row(psc_), _pad_row(gn_), _pad_row(_lanes_4_7(al_)), _pad_row(_lanes_4_7(dt_)),
             jnp.zeros((7, D), F32)], axis=0)

    ws = pack_small(norm_ffn1, norm_mix, norm_ffn2, final_norm, b_ada, pool_scale, gdn_norm, a_log, dt_bias)
    ms = pack_small(m_norm_ffn1, m_norm_mix, m_norm_ffn2, m_final_norm, m_b_ada, m_pool_scale, m_gdn_norm, m_a_log, m_dt_bias)
    vs = pack_small(v_norm_ffn1, v_norm_mix, v_norm_ffn2, v_final_norm, v_b_ada, v_pool_scale, v_gdn_norm, v_a_log, v_dt_bias)
    sm = _adamw_small(small_all, ws, ms, vs, "adamw_small")
    pw2 = lambda t: t.reshape(NG * 128, 128)
    upd_pw = _adamw_small(dpw_all, pw2(pool_w), pw2(m_pool_w), pw2(v_pool_w), "adamw_pool_w")
    csh = 3 * GW // 4
    gconv = lax.dynamic_slice(sm[0][17:23].reshape(4, 3 * GW), (0, chip * csh), (4, csh))
    upd["conv_w"] = _adamw([gconv], conv_w[0], m_conv_w[0], v_conv_w[0], "adamw_conv")

    def small_out(k):
        t = sm[k]
        return {
            "norm_ffn1": t[0:1], "norm_mix": t[1:2], "norm_ffn2": t[2:3], "final_norm": t[3],
            "b_ada": t[4:4 + NMOD].reshape(1, NMOD * D), "pool_scale": t[13:14, :PW], "gdn_norm": t[14:15, :DH],
            "a_log": t[15:16, NH:2 * NH], "dt_bias": t[16:17, NH:2 * NH],
        }

    order = ["w_ada", "b_ada", "norm_ffn1", "ffn1_gate", "ffn1_up", "ffn1_down", "norm_mix", "w_in", "conv_w", "a_log",
             "dt_bias", "gdn_norm", "pool_w", "pool_scale", "w_out", "norm_ffn2", "ffn2_gate", "ffn2_up", "ffn2_down",
             "final_norm"]
    outs = [loss, dx0[None]]
    for k in range(4):
        smk = small_out(k)
        for nm in order:
            if nm in upd:
                outs.append(upd[nm][k][None])
            elif nm == "pool_w":
                outs.append(upd_pw[k].reshape(1, NG, 128, 128))
            else:
                outs.append(smk[nm])
    return tuple(outs)
```

```python
import functools

import jax
import jax.numpy as jnp
from jax import lax
from jax.experimental import pallas as pl
from jax.experimental.pallas import tpu as pltpu

F32 = jnp.float32
BF = jnp.bfloat16

D = 1024
FF = 2816
FH = FF // 2
NH = 4
DH = 128
GW = NH * DH
CH = 64
PW = 512
NG = 4
POOL_WINDOWS = (2, 4, 8, 16)
HALO = 16
DIN = 4 * GW + 2 * NH + PW
DINP = 3 * GW + GW + PW + 128
EPS = 1e-6
ADAM_LR, ADAM_B1, ADAM_B2, ADAM_EPS, ADAM_WD, ADAM_STEP = 0.001, 0.9, 0.999, 1e-08, 0.01, 10

VMEM_LIMIT = 60 * 1024 * 1024

NT_DIMS = (((1,), (1,)), ((), ()))
TN_DIMS = (((0,), (0,)), ((), ()))
HI = lax.Precision.HIGHEST


def _nt(a, b, **kw):
    return lax.dot_general(a, b, NT_DIMS, preferred_element_type=F32, **kw)


def _tn(a, b, **kw):
    return lax.dot_general(a, b, TN_DIMS, preferred_element_type=F32, **kw)


def _nn(a, b, **kw):
    return jnp.dot(a, b, preferred_element_type=F32, **kw)


def _cparams(sem=("arbitrary",), **kw):
    return pltpu.CompilerParams(dimension_semantics=sem, vmem_limit_bytes=VMEM_LIMIT, **kw)


def _const_spec(shape):
    nd = len(shape)
    return pl.BlockSpec(shape, lambda *_: (0,) * nd, pipeline_mode=pl.Buffered(1))


def _row_spec(tm, width):
    return pl.BlockSpec((tm, width), lambda i: (i, 0))


def _sum8(v):
    return jnp.sum(v.reshape(v.shape[0] // 8, 8, v.shape[1]), axis=0)


def _sigmoid(v):
    return 0.5 * jnp.tanh(0.5 * v) + 0.5


def _tile(T, cap=512):
    return min(cap, T)


def _norm_mod_fwd(xv, gain, shift, scale):
    r = lax.rsqrt(jnp.mean(xv * xv, axis=-1, keepdims=True) + EPS)
    n = xv * r
    y = n * gain
    return n, r, y, y * (1.0 + scale) + shift


def _norm_mod_bwd(dh, n, r, y, gain, scale):
    dy = dh * (1.0 + scale)
    dn = dy * gain
    dx = r * (dn - n * jnp.mean(dn * n, axis=-1, keepdims=True))
    return dx, _sum8(dh), _sum8(dh * y), _sum8(dy * n)


def _ffn_wspecs(k0):
    return [pl.BlockSpec((4, FF // 4, D), lambda i, k=k0 + n: (0, k, 0), pipeline_mode=pl.Buffered(1)) for n in range(3)]


def _half(w_ref, j):
    return w_ref[2 * j:2 * j + 2].reshape(FH, D)


def _ffn_fwd(x, shift, scale, gate, gain, wall, k0, name, gather_block=None, loss=None):
    T = x.shape[0]
    tm = _tile(T)
    nt = T // tm
    comm = gather_block is not None
    assert not (comm and loss)

    def body(*refs):
        x_ref, sh_ref, sc_ref, gt_ref, gn_ref, wg_ref, wu_ref, wd_ref = refs[:8]
        if comm:
            xo_ref, f_ref, sa_ref, ga_ref, s_ref = refs[9:14]
            plan = _SplitGather(refs[8], refs[14], *refs[15:])
            _comm_begin(pl.program_id(0), plan, nt - 1)
        elif loss:
            t_ref, fg_ref, ls_ref, xo_ref, dfg_ref, f_ref, sa_ref, ga_ref, s_ref = refs[8:17]
        else:
            xo_ref, f_ref, sa_ref, ga_ref, s_ref = refs[8:13]
        xv = x_ref[...]
        _, _, _, h = _norm_mod_fwd(xv, gn_ref[...], sh_ref[...], sc_ref[...])
        hb = h.astype(BF)
        facc = jnp.zeros((tm, D), F32)
        for j in range(2):
            cols = slice(j * FH, (j + 1) * FH)
            a = _nt(hb, _half(wg_ref, j))
            b = _nt(hb, _half(wu_ref, j))
            sig = _sigmoid(a)
            sa = a * sig
            sa_ref[:, cols] = sa.astype(BF)
            ga_ref[:, cols] = (b * (sig * (1.0 + a * (1.0 - sig)))).astype(BF)
            s = (sa * b).astype(BF)
            s_ref[:, cols] = s
            facc = facc + _nn(s, _half(wd_ref, j))
        f_ref[...] = facc
        xo = xv + 0.5 * gt_ref[...] * facc
        if loss:
            i = pl.program_id(0)
            lsum, dxo, dfg = _loss_math(xo, t_ref[...], fg_ref[...])
            xo_ref[...] = dxo

            @pl.when(i == 0)
            def _():
                ls_ref[...] = lsum
                dfg_ref[...] = dfg

            @pl.when(i > 0)
            def _():
                ls_ref[...] += lsum
                dfg_ref[...] += dfg
        else:
            xo_ref[...] = xo
        if comm:
            _comm_end(pl.program_id(0), plan, nt - 1)

    vec = _const_spec((1, D))
    tok = jax.ShapeDtypeStruct((T, D), F32)
    extra_in, extra_specs = ([gather_block], [ANY]) if comm else (list(loss), [_row_spec(tm, D), vec]) if loss else ([], [])
    head_specs, head_shapes = [_row_spec(tm, D)], [tok]
    if loss:
        head_specs = [pl.BlockSpec((8, 128), lambda i: (0, 0)), _row_spec(tm, D), pl.BlockSpec((8, D), lambda i: (0, 0))]
        head_shapes = [jax.ShapeDtypeStruct((8, 128), F32), tok, jax.ShapeDtypeStruct((8, D), F32)]
    return pl.pallas_call(
        body, grid=(nt,), name=name,
        in_specs=[_row_spec(tm, D), vec, vec, vec, vec] + _ffn_wspecs(k0) + extra_specs,
        out_specs=head_specs + [_row_spec(tm, D)] + [_row_spec(tm, FF)] * 3 + [ANY] * comm,
        out_shape=head_shapes + [tok] + [jax.ShapeDtypeStruct((T, FF), BF)] * 3
        + ([jax.ShapeDtypeStruct((4,) + gather_block.shape, gather_block.dtype)] if comm else []),
        scratch_shapes=_gather_sems() if comm else [],
        compiler_params=_cparams(),
    )(x, shift, scale, gate, gain, wall, wall, wall, *extra_in)


def _ffn_dgrad(dxo, x, f, sa, ga, shift, scale, gate, gain, wall, k0, name):
    T = x.shape[0]
    tm = _tile(T)
    nt = T // tm
    wspecs = _ffn_wspecs(k0)
    vec = _const_spec((1, D))
    acc = pl.BlockSpec((8, D), lambda i: (0, 0))
    accs = jax.ShapeDtypeStruct((8, D), F32)

    def body_a(dxo_ref, f_ref, sa_ref, ga_ref, gt_ref, wd_ref, da_ref, db_ref, df_ref, dgt_ref):
        i = pl.program_id(0)
        dxo_v = dxo_ref[...]
        dgate = _sum8(0.5 * f_ref[...] * dxo_v)
        dfb = (0.5 * gt_ref[...] * dxo_v).astype(BF)
        df_ref[...] = dfb
        for j in range(2):
            cols = slice(j * FH, (j + 1) * FH)
            ds = _nt(dfb, _half(wd_ref, j))
            da_ref[:, cols] = (ds * ga_ref[:, cols].astype(F32)).astype(BF)
            db_ref[:, cols] = (ds * sa_ref[:, cols].astype(F32)).astype(BF)

        @pl.when(i == 0)
        def _():
            dgt_ref[...] = dgate

        @pl.when(i > 0)
        def _():
            dgt_ref[...] += dgate

    wide = jax.ShapeDtypeStruct((T, FF), BF)
    da, db, df, dgt = pl.pallas_call(
        body_a, grid=(nt,), name=name + "_a",
        in_specs=[_row_spec(tm, D), _row_spec(tm, D), _row_spec(tm, FF), _row_spec(tm, FF), vec, wspecs[2]],
        out_specs=[_row_spec(tm, FF), _row_spec(tm, FF), _row_spec(tm, D), acc],
        out_shape=[wide, wide, jax.ShapeDtypeStruct((T, D), BF), accs],
        compiler_params=_cparams(),
    )(dxo, f, sa, ga, gate, wall)

    def body_b(dxo_ref, x_ref, da_ref, db_ref, sh_ref, sc_ref, gn_ref, wg_ref, wu_ref,
               dx_ref, h_ref, dsh_ref, dsc_ref, dgn_ref):
        i = pl.program_id(0)
        gain_v, scale_v = gn_ref[...], sc_ref[...]
        n, r, y, h = _norm_mod_fwd(x_ref[...], gain_v, sh_ref[...], scale_v)
        h_ref[...] = h.astype(BF)
        dh = _nn(da_ref[...], wg_ref[...].reshape(FF, D)) + _nn(db_ref[...], wu_ref[...].reshape(FF, D))
        dxn, dsh, dsc, dgn = _norm_mod_bwd(dh, n, r, y, gain_v, scale_v)
        dx_ref[...] = dxo_ref[...] + dxn

        @pl.when(i == 0)
        def _():
            dsh_ref[...] = dsh
            dsc_ref[...] = dsc
            dgn_ref[...] = dgn

        @pl.when(i > 0)
        def _():
            dsh_ref[...] += dsh
            dsc_ref[...] += dsc
            dgn_ref[...] += dgn

    dx, h, dsh, dsc, dgn = pl.pallas_call(
        body_b, grid=(nt,), name=name + "_b",
        in_specs=[_row_spec(tm, D), _row_spec(tm, D), _row_spec(tm, FF), _row_spec(tm, FF), vec, vec, vec,
                  wspecs[0], wspecs[1]],
        out_specs=[_row_spec(tm, D), _row_spec(tm, D), acc, acc, acc],
        out_shape=[jax.ShapeDtypeStruct((T, D), F32), jax.ShapeDtypeStruct((T, D), BF), accs, accs, accs],
        compiler_params=_cparams(),
    )(dxo, x, da, db, shift, scale, gain, wall, wall)
    return dx, da, db, h, df, dsh, dsc, dgt, dgn


def _ffn_wgrad(da, db, s, h, df, name, scatter=(), allgather=()):
    T = h.shape[0]
    tk = _tile(T, 512)
    nk = T // tk
    nj = 2
    groups = [g for g in (
        (list(scatter), _Scatter, _scatter_sems, lambda t: t.shape),
        (list(allgather), _AllGather, _allgather_sems, lambda t: (8,) + t.shape),
    ) if g[0]]
    extra = [t for g in groups for t in g[0]]
    ne = len(extra)

    def body(*refs):
        da_ref, db_ref, s_ref, h_ref, df_ref = refs[:5]
        og_ref, ou_ref, od_ref = refs[5 + ne:8 + ne]
        ag, au, ad = refs[8 + 2 * ne:11 + 2 * ne]
        k = pl.program_id(1)
        step = pl.program_id(0) * nk + k
        if ne:
            plans, at, sem_at = [], 0, 11 + 2 * ne
            for arrs, make, _, _ in groups:
                n = len(arrs)
                plans.append(make(refs[5 + at:5 + at + n], refs[8 + ne + at:8 + ne + at + n], *refs[sem_at:sem_at + 3]))
                at, sem_at = at + n, sem_at + 3
            plan = _Plans(plans)
            _comm_begin(step, plan, nj * nk - 1)

        @pl.when(k == 0)
        def _():
            for acc in (ag, au, ad):
                acc[...] = jnp.zeros((FH, D), F32)

        ag[...] += _tn(da_ref[...], h_ref[...])
        au[...] += _tn(db_ref[...], h_ref[...])
        ad[...] += _tn(s_ref[...], df_ref[...])

        @pl.when(k == nk - 1)
        def _():
            og_ref[...] = ag[...].astype(BF)
            ou_ref[...] = au[...].astype(BF)
            od_ref[...] = ad[...].astype(BF)

        if ne:
            _comm_end(step, plan, nj * nk - 1)

    colblk = pl.BlockSpec((tk, FH), lambda j, k: (k, j))
    rowblk = pl.BlockSpec((tk, D), lambda j, k: (k, 0))
    outblk = pl.BlockSpec((FH, D), lambda j, k: (j, 0))
    outs = jax.ShapeDtypeStruct((nj * FH, D), BF)
    return pl.pallas_call(
        body, grid=(nj, nk), name=name,
        in_specs=[colblk, colblk, colblk, rowblk, rowblk] + [ANY] * ne,
        out_specs=[outblk, outblk, outblk] + [ANY] * ne,
        out_shape=[outs, outs, outs] + [jax.ShapeDtypeStruct(shape(t), t.dtype) for g in groups for t, shape in
                                        ((t, g[3]) for t in g[0])],
        scratch_shapes=[pltpu.VMEM((FH, D), F32)] * 3 + [sm for g in groups for sm in g[2](len(g[0]))],
        compiler_params=_cparams(("arbitrary", "arbitrary")),
    )(da, db, s, h, df, *extra)


def _loss_math(xv, target, gain_v):
    r = lax.rsqrt(jnp.mean(xv * xv, axis=-1, keepdims=True) + EPS)
    n = xv * r
    err = n * gain_v - target
    e2 = err * err
    part = e2[:, 0:128]
    for q in range(1, D // 128):
        part = part + e2[:, q * 128:(q + 1) * 128]
    dy = err * (1.0 / D)
    dn = dy * gain_v
    return _sum8(part) * (0.5 / D), r * (dn - n * jnp.mean(dn * n, axis=-1, keepdims=True)), _sum8(dy * n)


def _seg_cumsum(v, row_in_chunk, reverse=False):
    n = v.shape[0]
    s = 1
    while s < CH:
        if reverse:
            moved = pltpu.roll(v, n - s, 0)
            ok = row_in_chunk < CH - s
        else:
            moved = pltpu.roll(v, s, 0)
            ok = row_in_chunk >= s
        v = v + jnp.where(ok, moved, 0.0)
        s *= 2
    return v


def _conv_silu(xq_ext_ref, cw_ref, tm):
    c = cw_ref[0:1, :] * xq_ext_ref[pl.ds(5, tm), :]
    for j in range(1, 4):
        c = c + cw_ref[j:j + 1, :] * xq_ext_ref[pl.ds(5 + j, tm), :]
    return c, _sigmoid(c)


def _gates(ba, alog, dtb, lane):
    beta = _sigmoid(ba)
    arg = ba + dtb
    softplus = jnp.maximum(arg, 0.0) + jnp.log(1.0 + jnp.exp(-jnp.abs(arg)))
    g = -jnp.exp(alog) * softplus
    return jnp.where(lane < NH, beta, 0.0), jnp.where((lane >= NH) & (lane < 2 * NH), g, 0.0), _sigmoid(arg)


def _mix_proj(x, shift, scale, gain, winT, conv_w, alog, dtb):
    T = x.shape[0]
    tm = _tile(T)

    def body(x_ref, sh_ref, sc_ref, gn_ref, w_ref, cw_ref, al_ref, dt_ref,
             xq_ref, ba_ref, qn_ref, kn_ref, v_ref, z_ref, p_ref, bg_ref, gc_ref, ext):
        i = pl.program_id(0)
        _, _, _, h = _norm_mod_fwd(x_ref[...], gn_ref[...], sh_ref[...], sc_ref[...])
        hb = h.astype(BF)

        @pl.when(i == 0)
        def _():
            ext[pl.ds(0, 8), :] = jnp.zeros((8, 3 * GW), F32)

        xq = _nt(hb, w_ref[pl.ds(0, 3 * GW), :])
        xq_ref[...] = xq
        ext[pl.ds(8, tm), :] = xq
        z_ref[...] = _nt(hb, w_ref[pl.ds(3 * GW, GW), :])
        p_ref[...] = _nt(hb, w_ref[pl.ds(4 * GW, PW), :])
        ba = _nt(hb, w_ref[pl.ds(4 * GW + PW, 128), :])
        ba_ref[...] = ba

        c, sg = _conv_silu(ext, cw_ref, tm)
        ext[pl.ds(0, 8), :] = ext[pl.ds(tm, 8), :]
        qt = c * sg
        for hd in range(NH):
            cq = slice(hd * DH, (hd + 1) * DH)
            ck = slice(GW + hd * DH, GW + (hd + 1) * DH)
            qh, kh = qt[:, cq], qt[:, ck]
            qn_ref[:, cq] = qh * (lax.rsqrt(jnp.sum(qh * qh, axis=-1, keepdims=True) + EPS) * DH ** -0.5)
            kn_ref[:, cq] = kh * lax.rsqrt(jnp.sum(kh * kh, axis=-1, keepdims=True) + EPS)
        v_ref[...] = qt[:, 2 * GW:3 * GW]

        lane = lax.broadcasted_iota(jnp.int32, (tm, 128), 1)
        row = lax.broadcasted_iota(jnp.int32, (tm, 128), 0) % CH
        beta, g, _ = _gates(ba, al_ref[...], dt_ref[...], lane)
        bg_ref[...] = beta + g
        gc_ref[...] = _seg_cumsum(g, row)

    wide = lambda w: _row_spec(tm, w)
    shp = lambda w: jax.ShapeDtypeStruct((T, w), F32)
    return pl.pallas_call(
        body, grid=(T // tm,), name="mix_proj",
        in_specs=[wide(D), _const_spec((1, D)), _const_spec((1, D)), _const_spec((1, D)), _const_spec((DINP, D)),
                  _const_spec((4, 3 * GW)), _const_spec((1, 128)), _const_spec((1, 128))],
        out_specs=[wide(3 * GW), wide(128), wide(GW), wide(GW), wide(GW), wide(GW), wide(PW), wide(128), wide(128)],
        out_shape=[shp(3 * GW), shp(128), shp(GW), shp(GW), shp(GW), shp(GW), shp(PW), shp(128), shp(128)],
        scratch_shapes=[pltpu.VMEM((tm + 8, 3 * GW), F32)],
        compiler_params=_cparams(),
    )(x, shift, scale, gain, winT, conv_w, alog, dtb)


R2 = 2 * CH
TRI_PREC = None


def _tmm(fn, a, b):
    if TRI_PREC is None:
        return fn(a.astype(BF), b.astype(BF))
    return fn(a, b, precision=TRI_PREC)


def _pair_consts():
    ii = lax.broadcasted_iota(jnp.int32, (R2, R2), 0)
    jj = lax.broadcasted_iota(jnp.int32, (R2, R2), 1)
    same = (ii < CH) == (jj < CH)
    r = lax.broadcasted_iota(jnp.int32, (R2, 1), 0)
    return dict(causal=same & (ii >= jj), strict=same & (ii > jj), eye=(ii == jj).astype(F32), rowA=r < CH,
                last=(r == CH - 1) | (r == R2 - 1),
                rowS=lax.broadcasted_iota(jnp.int32, (2 * DH, 1), 0) < DH)


def _tri_inverse_many(ms, eye):
    pws = [-m for m in ms]
    ts = [eye + p for p in pws]
    for _ in range(5):
        pws = [_mm_split(p, p) for p in pws]
        ts = [_mm_split(t, eye + p) for t, p in zip(ts, pws)]
    return ts


def _mm_split(a, b):
    ah, bh = a.astype(BF), b.astype(BF)
    al, bl = (a - ah.astype(F32)).astype(BF), (b - bh.astype(F32)).astype(BF)
    return _nn(ah, bh) + _nn(ah, bl) + _nn(al, bh)


def _egl_rows(gl):
    egl = jnp.exp(gl)
    return egl, jnp.concatenate([jnp.broadcast_to(egl[0:1], (DH, 1)), jnp.broadcast_to(egl[CH:CH + 1], (DH, 1))], axis=0)


def _pair_intra(items, cn, tms=None):
    causal, rowA = cn["causal"], cn["rowA"]

    def bd(t):
        return jnp.concatenate([jnp.where(rowA, t, 0.0), jnp.where(rowA, 0.0, t)], axis=1).astype(BF)

    outs = []
    for q, k, v, beta, gcv, gl in items:
        gc_b = jnp.broadcast_to(gcv, (R2, R2))
        gam = jnp.where(causal, jnp.exp(jnp.where(causal, gc_b - gc_b.T, 0.0)), 0.0)
        kb = k * beta
        kbf = k.astype(BF)
        P = _nt(kb.astype(BF), kbf)
        QK = _nt(q.astype(BF), kbf)
        E = jnp.exp(gcv)
        outs.append(dict(gam=gam, kb=kb, vb=v * beta, P=P, QK=QK, E=E, Fd=jnp.exp(gl - gcv), kbE=kb * E,
                         Q=QK * gam, qE_bd=bd(q * E)))
    if tms is None:
        tms = _tri_inverse_many([jnp.where(cn["strict"], d["P"] * d["gam"], 0.0) for d in outs], cn["eye"])
    for d, tm_, (q, k, v, beta, gcv, gl) in zip(outs, tms, items):
        d["Tm"] = tm_
        d["u"] = _tmm(_nn, tm_, d["vb"])
        d["w_bd"] = bd(_tmm(_nn, tm_, d["kbE"]))
        d["kF_bd"] = bd(k * d["Fd"])
    return outs


def _pair_scan(it, S, egl_st):
    Sb = S.astype(BF)
    vn = it["u"] - _nn(it["w_bd"], Sb)
    vnb = vn.astype(BF)
    o = _nn(it["qE_bd"], Sb) + _nn(it["Q"].astype(BF), vnb)
    return vnb, o, S * egl_st + _tn(it["kF_bd"], vnb)


def _pair_forward(q, k, v, beta, gcv, gl, S, cn):
    fw = _pair_intra([(q, k, v, beta, gcv, gl)], cn)[0]
    fw["egl"], fw["egl_st"] = _egl_rows(gl)
    fw["vnb"], fw["o"], fw["S_new"] = _pair_scan(fw, S, fw["egl_st"])
    return fw


def _stack_heads(ref, rows, pair):
    return jnp.concatenate([ref[rows, (2 * pair) * DH:(2 * pair + 1) * DH],
                            ref[rows, (2 * pair + 1) * DH:(2 * pair + 2) * DH]], axis=0)


def _stack_cols(val, lane_a, lane_b, bcast_rows=None):
    a, b = val[:, lane_a:lane_a + 1], val[:, lane_b:lane_b + 1]
    if bcast_rows:
        a, b = jnp.broadcast_to(a, (bcast_rows, 1)), jnp.broadcast_to(b, (bcast_rows, 1))
    return jnp.concatenate([a, b], axis=0)


def _pool_windows(ext, tm, reverse):
    n = tm + HALO
    outs = []
    for gi in range(NG):
        a = ext[:, gi * 128:(gi + 1) * 128]
        s = 1
        while s < POOL_WINDOWS[gi]:
            a = a + pltpu.roll(a, (n - s) if reverse else s, 0)
            s *= 2
        outs.append(a[0:tm] if reverse else a[HALO:HALO + tm])
    return jnp.concatenate(outs, axis=1)


def _pool_count(tm, tile_index):
    t1 = (lax.broadcasted_iota(jnp.int32, (tm, PW), 0) + tile_index * tm + 1).astype(F32)
    win = jnp.concatenate([jnp.full((tm, 128), float(w), F32) for w in POOL_WINDOWS], axis=1)
    return 1.0 / jnp.minimum(t1, win)


def _chunk_item(qn_ref, kn_ref, v_ref, bg_ref, gc_ref, c, pr):
    r0 = pl.multiple_of(c * CH, CH)
    rows = pl.ds(r0, CH)
    bgv = bg_ref[rows, :]
    gcv_all = gc_ref[rows, :]
    gl_all = gc_ref[pl.ds(r0 + CH - 1, 1), :]
    ha, hb = 2 * pr, 2 * pr + 1
    return (_stack_heads(qn_ref, rows, pr), _stack_heads(kn_ref, rows, pr), _stack_heads(v_ref, rows, pr),
            _stack_cols(bgv, ha, hb), _stack_cols(gcv_all, NH + ha, NH + hb), _stack_cols(gl_all, NH + ha, NH + hb, CH))


CHUNK_GROUP = 4


def _mix_core(x, gate, qn, kn, v, z, p, bg, gc, gnorm, pool_w, pool_scale, w_out):
    T = x.shape[0]
    tm = _tile(T)
    nc = tm // CH
    cg = CHUNK_GROUP if nc % CHUNK_GROUP == 0 else 1
    npb = nc * (NH // 2)

    def body(x_ref, gt_ref, qn_ref, kn_ref, v_ref, z_ref, p_ref, bg_ref, gc_ref, gnm_ref, pw_ref, ps_ref, wo_ref,
             xo_ref, mx_ref, cat_ref, o_ref, sall_ref, tmall_ref, S_scr, pext, u_s, w_s, qe_s, kf_s, q_s):
        i = pl.program_id(0)

        @pl.when(i == 0)
        def _():
            S_scr[...] = jnp.zeros((NH * DH, DH), F32)
            pext[pl.ds(0, HALO), :] = jnp.zeros((HALO, PW), F32)

        cn = _pair_consts()

        def intra(g, carry):
            idx = [(g * cg + dc, pr) for dc in range(cg) for pr in range(NH // 2)]
            res = _pair_intra([_chunk_item(qn_ref, kn_ref, v_ref, bg_ref, gc_ref, c, pr) for c, pr in idx], cn)
            for (c, pr), d in zip(idx, res):
                pi = c * (NH // 2) + pr
                u_s[pi] = d["u"]
                w_s[pi] = d["w_bd"]
                qe_s[pi] = d["qE_bd"]
                kf_s[pi] = d["kF_bd"]
                q_s[pi] = d["Q"].astype(BF)
                tmall_ref[pi] = d["Tm"].astype(BF)
            return carry

        lax.fori_loop(0, nc // cg, intra, 0)

        def scan(c, carry):
            r0 = pl.multiple_of(c * CH, CH)
            rows = pl.ds(r0, CH)
            gl_all = gc_ref[pl.ds(r0 + CH - 1, 1), :]
            for pr in range(NH // 2):
                ha, hb = 2 * pr, 2 * pr + 1
                pi = c * (NH // 2) + pr
                S = S_scr[pl.ds(pr * 2 * DH, 2 * DH), :]
                sall_ref[c, ha:hb + 1] = S.reshape(2, DH, DH)
                _, egl_st = _egl_rows(_stack_cols(gl_all, NH + ha, NH + hb, CH))
                it = dict(u=u_s[pi], w_bd=w_s[pi], qE_bd=qe_s[pi], kF_bd=kf_s[pi], Q=q_s[pi])
                _, o, S_new = _pair_scan(it, S, egl_st)
                o_ref[rows, ha * DH:(ha + 1) * DH] = o[0:CH]
                o_ref[rows, hb * DH:(hb + 1) * DH] = o[CH:R2]
                S_scr[pl.ds(pr * 2 * DH, 2 * DH), :] = S_new
            return carry

        lax.fori_loop(0, nc, scan, 0)

        for hd in range(NH):
            cols = slice(hd * DH, (hd + 1) * DH)
            oh = o_ref[:, cols]
            zh = z_ref[:, cols]
            r = lax.rsqrt(jnp.mean(oh * oh, axis=-1, keepdims=True) + EPS)
            cat_ref[:, cols] = (oh * r * gnm_ref[...] * (zh * _sigmoid(zh))).astype(BF)

        pv = p_ref[...]
        pext[pl.ds(HALO, tm), :] = pv
        pooled = _pool_windows(pext[...], tm, False) * _pool_count(tm, i) - pv
        pext[pl.ds(0, HALO), :] = pext[pl.ds(tm, HALO), :]
        for gi in range(NG):
            cols = slice(gi * 128, (gi + 1) * 128)
            pm = _nn(pooled[:, cols].astype(BF), pw_ref[gi])
            cat_ref[:, GW + gi * 128:GW + (gi + 1) * 128] = (pm * ps_ref[:, cols]).astype(BF)

        mixed = _nn(cat_ref[...], wo_ref[...])
        mx_ref[...] = mixed
        xo_ref[...] = x_ref[...] + gt_ref[...] * mixed

    wide = lambda w: _row_spec(tm, w)
    return pl.pallas_call(
        body, grid=(T // tm,), name="mix_core",
        in_specs=[wide(D), _const_spec((1, D)), wide(GW), wide(GW), wide(GW), wide(GW), wide(PW), wide(128), wide(128),
                  _const_spec((1, DH)), _const_spec((NG, 128, 128)), _const_spec((1, PW)), _const_spec((D, D))],
        out_specs=[wide(D), wide(D), wide(D), wide(GW), pl.BlockSpec((nc, NH, DH, DH), lambda i: (i, 0, 0, 0)),
                   pl.BlockSpec((npb, R2, R2), lambda i: (i, 0, 0))],
        out_shape=[jax.ShapeDtypeStruct((T, D), F32), jax.ShapeDtypeStruct((T, D), F32),
                   jax.ShapeDtypeStruct((T, D), BF), jax.ShapeDtypeStruct((T, GW), F32),
                   jax.ShapeDtypeStruct((T // CH, NH, DH, DH), F32),
                   jax.ShapeDtypeStruct((T // CH * (NH // 2), R2, R2), BF)],
        scratch_shapes=[pltpu.VMEM((NH * DH, DH), F32), pltpu.VMEM((tm + HALO, PW), F32),
                        pltpu.VMEM((npb, R2, DH), F32), pltpu.VMEM((npb, R2, 2 * DH), BF),
                        pltpu.VMEM((npb, R2, 2 * DH), BF), pltpu.VMEM((npb, R2, 2 * DH), BF),
                        pltpu.VMEM((npb, R2, R2), BF)],
        compiler_params=_cparams(),
    )(x, gate, qn, kn, v, z, p, bg, gc, gnorm, pool_w, pool_scale, w_out)


def _pair_scan_bwd(it, S, dSn, do, egl, egl_st, cn):
    bf = lambda t: t.astype(BF)
    rowA, rowS = cn["rowA"], cn["rowS"]
    sel = lambda t: jnp.where(rowA, t[:, 0:DH], t[:, DH:2 * DH])
    Sb, dSb, dob = bf(S), bf(dSn), bf(do)
    vnb = bf(it["u"] - _nn(it["w_bd"], Sb))
    dvn = _tn(bf(it["Q"]), dob) + _nn(it["kF_bd"], dSb)
    dQ = _nt(dob, vnb)
    dqE = sel(_nt(dob, Sb))
    dkF = sel(_nt(vnb, dSb))
    dvnb = bf(dvn)
    dw = -sel(_nt(dvnb, Sb))
    dS_new = _tn(it["qE_bd"], dob) + egl_st * dSn - _tn(it["w_bd"], dvnb)
    prod = jnp.sum(dSn * S, axis=1, keepdims=True)
    d_egl_a = jnp.sum(jnp.where(rowS, prod, 0.0), axis=0, keepdims=True)
    d_egl_b = jnp.sum(jnp.where(rowS, 0.0, prod), axis=0, keepdims=True)
    return dict(dvn=dvn, dQ=dQ, dqE=dqE, dkF=dkF, dw=dw, degl=jnp.where(rowA, d_egl_a, d_egl_b) * egl), dS_new


def _pair_intra_bwd(items, cn):
    bf = lambda t: t.astype(BF)
    rowA = cn["rowA"]
    for d in items:
        d["kb"] = d["k"] * d["beta"]
        d["E"] = jnp.exp(d["gcv"])
        d["Fd"] = jnp.exp(d["gl"] - d["gcv"])
        d["TmT"] = d["Tm"].T
        d["dvb"] = _tmm(_nn, d["TmT"], d["dvn"])
        d["dkbE"] = _tmm(_nn, d["TmT"], d["dw"])
        dTm = _tmm(_nt, d["dvn"], d["v"] * d["beta"]) + _tmm(_nt, d["dw"], d["kb"] * d["E"])
        d["X"] = _tmm(_nt, dTm, d["Tm"])
    for d in items:
        d["dA"] = -_tmm(_nn, d["TmT"], d["X"])
    outs = []
    for d in items:
        q, k, v, beta, gam = d["q"], d["k"], d["v"], d["beta"], d["gam"]
        N = jnp.where(cn["strict"], d["dA"] * gam, 0.0)
        Rm = d["dQ"] * gam
        Wm = Rm * d["QK"] + N * d["P"]
        dgc = jnp.sum(Wm, axis=1, keepdims=True) - jnp.sum(Wm.T, axis=1, keepdims=True)
        kbf, Nb, Rb = bf(k), bf(N), bf(Rm)
        E, Fd = d["E"], d["Fd"]
        dq = d["dqE"] * E + _nn(Rb, kbf)
        dkb = d["dkbE"] * E + _nn(Nb, kbf)
        dk = d["dkF"] * Fd + _tn(Rb, bf(q)) + _tn(Nb, bf(d["kb"])) + beta * dkb
        dbeta = jnp.sum(dkb * k + d["dvb"] * v, axis=1, keepdims=True)
        dE = jnp.sum(d["dqE"] * q + d["dkbE"] * d["kb"], axis=1, keepdims=True)
        fdf = jnp.sum(d["dkF"] * k, axis=1, keepdims=True) * Fd
        dgl = d["degl"] + jnp.where(rowA, jnp.sum(jnp.where(rowA, fdf, 0.0), axis=0, keepdims=True),
                                    jnp.sum(jnp.where(rowA, 0.0, fdf), axis=0, keepdims=True))
        dgc = dgc + dE * E - fdf + jnp.where(cn["last"], dgl, 0.0)
        outs.append((dq, dk, beta * d["dvb"], dbeta, dgc))
    return outs


def _mix_core_bwd(dxo, gate, mixed, cat, o, sall, qn, kn, v, z, p, bg, gc, gnorm, pool_w, pool_scale, w_out, tm_all,
                  scatter=()):
    T = dxo.shape[0]
    tm = _tile(T, 256)
    nt = T // tm
    nc = tm // CH
    ns = len(scatter)
    cg = CHUNK_GROUP if nc % CHUNK_GROUP == 0 else 1
    npb = nc * (NH // 2)

    def body(*refs):
        (dx_ref, gt_ref, mx_ref, cat_ref, o_ref, sall_ref, qn_ref, kn_ref, v_ref, z_ref, p_ref, ph_ref, bg_ref,
         gc_ref, gnm_ref, pw_ref, ps_ref, wo_ref, tmall_ref) = refs[:19]
        (dq_ref, dk_ref, dv_ref, dz_ref, dp_ref, dbg_ref, dgt_ref, dwo_ref, dpw_ref, dps_ref,
         dgn_ref) = refs[19 + ns:30 + ns]
        dS_scr, pext, yext, do_buf, dwo_acc = refs[30 + 2 * ns:35 + 2 * ns]
        (gam_s, p_s, qk_s, tm_s, dqq_s, u_s, dvn_s, dqe_s, dkf_s, dw_s, w_s, qe_s, kf_s, q_s,
         degl_s) = refs[35 + 2 * ns:50 + 2 * ns]
        i = pl.program_id(0)
        ti = nt - 1 - i
        if ns:
            plan = _Scatter(refs[19:19 + ns], refs[30 + ns:30 + 2 * ns], *refs[50 + 2 * ns:])
            _comm_begin(i, plan, nt - 1)

        @pl.when(i == 0)
        def _():
            dS_scr[...] = jnp.zeros((NH * DH, DH), F32)
            yext[pl.ds(tm, HALO), :] = jnp.zeros((HALO, PW), F32)
            dwo_acc[...] = jnp.zeros((D, D), F32)
            dgt_ref[...] = jnp.zeros((8, D), F32)
            dpw_ref[...] = jnp.zeros((NG, 128, 128), F32)
            dps_ref[...] = jnp.zeros((8, PW), F32)
            dgn_ref[...] = jnp.zeros((8, DH), F32)

        dx2 = dx_ref[...]
        dgt_ref[...] += _sum8(mx_ref[...] * dx2)
        dmix = (gt_ref[...] * dx2).astype(BF)
        dcat = _nt(dmix, wo_ref[...])
        dwo_acc[...] += _tn(cat_ref[...], dmix)

        pv = p_ref[...]
        pext[pl.ds(0, HALO), :] = jnp.where(ti == 0, 0.0, ph_ref[...])
        pext[pl.ds(HALO, tm), :] = pv
        inv_cnt = _pool_count(tm, ti)
        pooled = _pool_windows(pext[...], tm, False) * inv_cnt - pv
        dpooled = []
        for gi in range(NG):
            cols = slice(gi * 128, (gi + 1) * 128)
            pgb = pooled[:, cols].astype(BF)
            pm = _nn(pgb, pw_ref[gi])
            dpo = dcat[:, GW + gi * 128:GW + (gi + 1) * 128]
            dps_ref[:, cols] += _sum8(dpo * pm)
            dpm = (dpo * ps_ref[:, cols]).astype(BF)
            dpooled.append(_nt(dpm, pw_ref[gi]))
            dpw_ref[gi] += _tn(pgb, dpm)
        dpooled = jnp.concatenate(dpooled, axis=1)
        y = dpooled * inv_cnt
        yext[pl.ds(0, tm), :] = y
        dp_ref[...] = _pool_windows(yext[...], tm, True) - dpooled
        yext[pl.ds(tm, HALO), :] = y[0:HALO]

        gnm = gnm_ref[...]
        dgn = jnp.zeros((8, DH), F32)
        for hd in range(NH):
            cols = slice(hd * DH, (hd + 1) * DH)
            oh = o_ref[:, cols]
            zh = z_ref[:, cols]
            r = lax.rsqrt(jnp.mean(oh * oh, axis=-1, keepdims=True) + EPS)
            n = oh * r
            sg = _sigmoid(zh)
            zs = zh * sg
            dgo = dcat[:, cols]
            dgn = dgn + _sum8(dgo * zs * n)
            dn = dgo * zs * gnm
            do_buf[:, cols] = r * (dn - n * jnp.mean(dn * n, axis=-1, keepdims=True))
            dz_ref[:, cols] = dgo * n * gnm * (sg * (1.0 + zh * (1.0 - sg)))
        dgn_ref[...] += dgn

        cn = _pair_consts()
        lane_c = lax.broadcasted_iota(jnp.int32, (CH, 128), 1)

        def intra(g, carry):
            idx = [(g * cg + dc, pr) for dc in range(cg) for pr in range(NH // 2)]
            res = _pair_intra([_chunk_item(qn_ref, kn_ref, v_ref, bg_ref, gc_ref, c, pr) for c, pr in idx], cn,
                              tms=[tmall_ref[c * (NH // 2) + pr].astype(F32) for c, pr in idx])
            for (c, pr), d in zip(idx, res):
                pi = c * (NH // 2) + pr
                gam_s[pi], p_s[pi], qk_s[pi], tm_s[pi], u_s[pi] = d["gam"], d["P"], d["QK"], d["Tm"], d["u"]
                w_s[pi], qe_s[pi], kf_s[pi], q_s[pi] = d["w_bd"], d["qE_bd"], d["kF_bd"], d["Q"].astype(BF)
            return carry

        lax.fori_loop(0, nc // cg, intra, 0)

        def scan(cc, carry):
            c = nc - 1 - cc
            r0 = pl.multiple_of(c * CH, CH)
            rows = pl.ds(r0, CH)
            gl_all = gc_ref[pl.ds(r0 + CH - 1, 1), :]
            for pr in range(NH // 2):
                ha, hb = 2 * pr, 2 * pr + 1
                pi = c * (NH // 2) + pr
                srows = pl.ds(pr * 2 * DH, 2 * DH)
                S = sall_ref[c, ha:hb + 1].reshape(2 * DH, DH)
                egl, egl_st = _egl_rows(_stack_cols(gl_all, NH + ha, NH + hb, CH))
                it = dict(u=u_s[pi], w_bd=w_s[pi], qE_bd=qe_s[pi], kF_bd=kf_s[pi], Q=q_s[pi])
                g, dS_new = _pair_scan_bwd(it, S, dS_scr[srows, :], _stack_heads(do_buf, rows, pr), egl, egl_st, cn)
                dvn_s[pi], dqq_s[pi], dqe_s[pi], dkf_s[pi], dw_s[pi] = g["dvn"], g["dQ"], g["dqE"], g["dkF"], g["dw"]
                degl_s[pi] = g["degl"]
                dS_scr[srows, :] = dS_new
            return carry

        lax.fori_loop(0, nc, scan, 0)

        def intra_bwd(g, carry):
            idx = [(g * cg + dc, pr) for dc in range(cg) for pr in range(NH // 2)]
            items = []
            for c, pr in idx:
                pi = c * (NH // 2) + pr
                q, k, vv, beta, gcv, gl = _chunk_item(qn_ref, kn_ref, v_ref, bg_ref, gc_ref, c, pr)
                items.append(dict(q=q, k=k, v=vv, beta=beta, gcv=gcv, gl=gl, gam=gam_s[pi], P=p_s[pi], QK=qk_s[pi],
                                  Tm=tm_s[pi], dvn=dvn_s[pi], dQ=dqq_s[pi], dqE=dqe_s[pi], dkF=dkf_s[pi], dw=dw_s[pi],
                                  degl=degl_s[pi]))
            res = _pair_intra_bwd(items, cn)
            for dc in range(cg):
                c = g * cg + dc
                rows = pl.ds(pl.multiple_of(c * CH, CH), CH)
                dbg = jnp.zeros((CH, 128), F32)
                for pr in range(NH // 2):
                    dq, dk, dv, dbeta, dgc = res[dc * (NH // 2) + pr]
                    for hd, half in ((2 * pr, slice(0, CH)), (2 * pr + 1, slice(CH, R2))):
                        cols = slice(hd * DH, (hd + 1) * DH)
                        dq_ref[rows, cols] = dq[half]
                        dk_ref[rows, cols] = dk[half]
                        dv_ref[rows, cols] = dv[half]
                        dbg = dbg + jnp.where(lane_c == hd, dbeta[half], 0.0) + jnp.where(lane_c == NH + hd, dgc[half], 0.0)
                dbg_ref[rows, :] = dbg
            return carry

        lax.fori_loop(0, nc // cg, intra_bwd, 0)

        lane = lax.broadcasted_iota(jnp.int32, (tm, 128), 1)
        row = lax.broadcasted_iota(jnp.int32, (tm, 128), 0) % CH
        dbg_all = dbg_ref[...]
        dg = _seg_cumsum(jnp.where(lane >= NH, dbg_all, 0.0), row, reverse=True)
        dbg_ref[...] = jnp.where(lane < NH, dbg_all, dg)

        @pl.when(i == nt - 1)
        def _():
            dwo_ref[...] = dwo_acc[...].astype(BF)

        if ns:
            _comm_end(i, plan, nt - 1)

    rev = lambda w: pl.BlockSpec((tm, w), lambda i: (nt - 1 - i, 0))
    halo = pl.BlockSpec((HALO, PW), lambda i: (jnp.maximum((nt - 1 - i) * (tm // HALO) - 1, 0), 0))
    shp = lambda w: jax.ShapeDtypeStruct((T, w), F32)
    fix = lambda *s: pl.BlockSpec(s, lambda i: (0,) * len(s))
    return pl.pallas_call(
        body, grid=(nt,), name="mix_core_bwd",
        in_specs=[rev(D), _const_spec((1, D)), rev(D), rev(D), rev(GW),
                  pl.BlockSpec((nc, NH, DH, DH), lambda i: (nt - 1 - i, 0, 0, 0)),
                  rev(GW), rev(GW), rev(GW), rev(GW), rev(PW), halo, rev(128), rev(128),
                  _const_spec((1, DH)), _const_spec((NG, 128, 128)), _const_spec((1, PW)), _const_spec((D, D)),
                  pl.BlockSpec((npb, R2, R2), lambda i: (nt - 1 - i, 0, 0))]
        + [ANY] * ns,
        out_specs=[rev(GW), rev(GW), rev(GW), rev(GW), rev(PW), rev(128),
                   fix(8, D), fix(D, D), fix(NG, 128, 128), fix(8, PW), fix(8, DH)] + [ANY] * ns,
        out_shape=[shp(GW), shp(GW), shp(GW), shp(GW), shp(PW), shp(128),
                   jax.ShapeDtypeStruct((8, D), F32), jax.ShapeDtypeStruct((D, D), BF),
                   jax.ShapeDtypeStruct((NG, 128, 128), F32), jax.ShapeDtypeStruct((8, PW), F32),
                   jax.ShapeDtypeStruct((8, DH), F32)] + [jax.ShapeDtypeStruct(t.shape, t.dtype) for t in scatter],
        scratch_shapes=[pltpu.VMEM((NH * DH, DH), F32), pltpu.VMEM((tm + HALO, PW), F32),
                        pltpu.VMEM((tm + HALO, PW), F32), pltpu.VMEM((tm, GW), F32), pltpu.VMEM((D, D), F32)]
        + [pltpu.VMEM((npb, R2, R2), F32)] * 5 + [pltpu.VMEM((npb, R2, DH), F32)] * 5
        + [pltpu.VMEM((npb, R2, 2 * DH), BF)] * 3 + [pltpu.VMEM((npb, R2, R2), BF), pltpu.VMEM((npb, R2, 1), F32)]
        + (_scatter_sems(ns) if ns else []),
        compiler_params=_cparams(),
    )(dxo, gate, mixed, cat, o, sall, qn, kn, v, z, p, p, bg, gc, gnorm, pool_w, pool_scale, w_out, tm_all, *scatter)


def _mix_proj_bwd(dxo, dqn, dkn, dv, dz, dp, dbg, xq, ba, x, shift, scale, gain, winT, conv_w, alog, dtb):
    T = x.shape[0]
    tm = min(256, T)
    nt = T // tm
    W3 = 3 * GW

    def body(dxo_ref, dqn_ref, dkn_ref, dv_ref, dz_ref, dp_ref, dbg_ref, xq_ref, xh_ref, ba_ref, x_ref, sh_ref, sc_ref,
             gn_ref, w_ref, cw_ref, al_ref, dt_ref,
             dx_ref, dw_ref, dcw_ref, dal_ref, ddt_ref, dsh_ref, dsc_ref, dgn_ref,
             ext, dcext, dproj, dw_acc):
        i = pl.program_id(0)
        ti = nt - 1 - i

        @pl.when(i == 0)
        def _():
            dcext[pl.ds(tm, 8), :] = jnp.zeros((8, W3), F32)
            dw_acc[...] = jnp.zeros((DINP, D), F32)
            dcw_ref[...] = jnp.zeros((4, 8, W3), F32)
            dal_ref[...] = jnp.zeros((8, 128), F32)
            ddt_ref[...] = jnp.zeros((8, 128), F32)
            dsh_ref[...] = jnp.zeros((8, D), F32)
            dsc_ref[...] = jnp.zeros((8, D), F32)
            dgn_ref[...] = jnp.zeros((8, D), F32)

        ext[pl.ds(0, 8), :] = jnp.where(ti == 0, 0.0, xh_ref[...])
        ext[pl.ds(8, tm), :] = xq_ref[...]
        c, sg = _conv_silu(ext, cw_ref, tm)
        qt = c * sg
        dsilu = sg * (1.0 + c * (1.0 - sg))
        for hd in range(NH):
            for part, dref, mult in ((0, dqn_ref, DH ** -0.5), (1, dkn_ref, 1.0)):
                cols = slice(part * GW + hd * DH, part * GW + (hd + 1) * DH)
                xh = qt[:, cols]
                rr = lax.rsqrt(jnp.sum(xh * xh, axis=-1, keepdims=True) + EPS)
                unit = xh * rr
                du = dref[:, hd * DH:(hd + 1) * DH] * mult
                dxh = rr * (du - unit * jnp.sum(du * unit, axis=-1, keepdims=True))
                dcext[pl.ds(0, tm), cols] = dxh * dsilu[:, cols]
        dcext[pl.ds(0, tm), 2 * GW:W3] = dv_ref[...] * dsilu[:, 2 * GW:W3]
        dc = dcext[pl.ds(0, tm), :]
        dxq = jnp.zeros((tm, W3), F32)
        for j in range(4):
            dcw_ref[j] += _sum8(dc * ext[pl.ds(5 + j, tm), :])
            dxq = dxq + cw_ref[j:j + 1, :] * dcext[pl.ds(3 - j, tm), :]
        dcext[pl.ds(tm, 8), :] = dc[0:8]
        lane = lax.broadcasted_iota(jnp.int32, (tm, 128), 1)
        bav = ba_ref[...]
        beta, g, sarg = _gates(bav, al_ref[...], dt_ref[...], lane)
        dbg_v = dbg_ref[...]
        is_g = (lane >= NH) & (lane < 2 * NH)
        dbraw = jnp.where(lane < NH, dbg_v * beta * (1.0 - beta), 0.0)
        daraw = jnp.where(is_g, dbg_v * (-jnp.exp(al_ref[...])) * sarg, 0.0)
        dal_ref[...] += _sum8(jnp.where(is_g, dbg_v * g, 0.0))
        ddt_ref[...] += _sum8(daraw)
        dproj[:, 0:W3] = dxq.astype(BF)
        dproj[:, W3:W3 + GW] = dz_ref[...].astype(BF)
        dproj[:, W3 + GW:W3 + GW + PW] = dp_ref[...].astype(BF)
        dproj[:, W3 + GW + PW:DINP] = (dbraw + daraw).astype(BF)
        gain_v, scale_v = gn_ref[...], sc_ref[...]
        n, r, y, h = _norm_mod_fwd(x_ref[...], gain_v, sh_ref[...], scale_v)
        dpj = dproj[...]
        dh = _nn(dpj, w_ref[...])
        dw_acc[...] += _tn(dpj, h.astype(BF))
        dxn, dsh, dsc, dgn = _norm_mod_bwd(dh, n, r, y, gain_v, scale_v)
        dx_ref[...] = dxo_ref[...] + dxn
        dsh_ref[...] += dsh
        dsc_ref[...] += dsc
        dgn_ref[...] += dgn

        @pl.when(i == nt - 1)
        def _():
            dw_ref[...] = dw_acc[...].astype(BF)

    rev = lambda w: pl.BlockSpec((tm, w), lambda i: (nt - 1 - i, 0))
    halo = pl.BlockSpec((8, W3), lambda i: (jnp.maximum((nt - 1 - i) * (tm // 8) - 1, 0), 0))
    fix = lambda *s: pl.BlockSpec(s, lambda i: (0,) * len(s))
    vec = _const_spec((1, D))
    return pl.pallas_call(
        body, grid=(nt,), name="mix_proj_bwd",
        in_specs=[rev(D), rev(GW), rev(GW), rev(GW), rev(GW), rev(PW), rev(128), rev(W3), halo, rev(128), rev(D),
                  vec, vec, vec, _const_spec((DINP, D)), _const_spec((4, W3)), _const_spec((1, 128)),
                  _const_spec((1, 128))],
        out_specs=[rev(D), fix(DINP, D), fix(4, 8, W3), fix(8, 128), fix(8, 128), fix(8, D), fix(8, D), fix(8, D)],
        out_shape=[jax.ShapeDtypeStruct((T, D), F32), jax.ShapeDtypeStruct((DINP, D), BF),
                   jax.ShapeDtypeStruct((4, 8, W3), F32), jax.ShapeDtypeStruct((8, 128), F32),
                   jax.ShapeDtypeStruct((8, 128), F32), jax.ShapeDtypeStruct((8, D), F32),
                   jax.ShapeDtypeStruct((8, D), F32), jax.ShapeDtypeStruct((8, D), F32)],
        scratch_shapes=[pltpu.VMEM((tm + 8, W3), F32), pltpu.VMEM((tm + 8, W3), F32), pltpu.VMEM((tm, DINP), BF),
                        pltpu.VMEM((DINP, D), F32)],
        compiler_params=_cparams(),
    )(dxo, dqn, dkn, dv, dz, dp, dbg, xq, xq, ba, x, shift, scale, gain, winT, conv_w, alog, dtb)


MESH = pl.DeviceIdType.MESH
CHIP_RELS = ((1, 0), (0, 1), (1, 1))
DEV_RELS = tuple((dx, dy, dc) for dx in (0, 1) for dy in (0, 1) for dc in (0, 1) if (dx, dy, dc) != (0, 0, 0))
NMOD = 9
ADA_SH = NMOD * D // 4
WIN_SH = DIN // 4
WIN_PAD = 672
MSG_ROWS = 16
ANY = pl.BlockSpec(memory_space=pl.ANY)
VM = pl.BlockSpec(memory_space=pltpu.VMEM)


def _place():
    x, y, c = lax.axis_index("x"), lax.axis_index("y"), lax.axis_index("c")
    return x, y, c


class _SplitGather:
    N_SEMS = (3, 3, 3, 3, 1)

    def __init__(self, src, dst, ici_s, ici_r, d2d_s, d2d_r, lsem):
        self.src, self.dst = src, dst
        self.sems = (ici_s, ici_r, d2d_s, d2d_r)
        self.x, self.y, self.c = _place()
        self.myj = 2 * self.x + self.y
        half = src.shape[0] // 2
        self.mine = pl.ds(pl.multiple_of(self.c * half, 16), half)
        self.other = pl.ds(pl.multiple_of((1 - self.c) * half, 16), half)
        self.local = pltpu.make_async_copy(src, dst.at[self.myj], lsem.at[0])

    def _ici(self, k, slot):
        dx, dy = CHIP_RELS[k]
        return pltpu.make_async_remote_copy(
            src_ref=self.src.at[self.mine], dst_ref=self.dst.at[slot, self.mine], send_sem=self.sems[0].at[k],
            recv_sem=self.sems[1].at[k], device_id=(self.x ^ dx, self.y ^ dy, self.c), device_id_type=MESH)

    def _d2d(self, k, rows):
        dx, dy = CHIP_RELS[k]
        blk = self.dst.at[2 * (self.x ^ dx) + (self.y ^ dy), rows]
        return pltpu.make_async_remote_copy(
            src_ref=blk, dst_ref=blk, send_sem=self.sems[2].at[k], recv_sem=self.sems[3].at[k],
            device_id=(self.x, self.y, 1 - self.c), device_id_type=MESH)

    def start(self):
        self.local.start()
        for k in range(3):
            self._ici(k, self.myj).start()

    def forward(self):
        for k, (dx, dy) in enumerate(CHIP_RELS):
            self._ici(k, 2 * (self.x ^ dx) + (self.y ^ dy)).wait_recv()
            self._d2d(k, self.mine).start()

    def finish(self):
        for k in range(3):
            self._d2d(k, self.other).wait_recv()
        for k in range(3):
            self._d2d(k, self.mine).wait_send()
            self._ici(k, self.myj).wait_send()
        self.local.wait()


class _Scatter:
    def __init__(self, ins, outs, send, recv, lsem):
        self.ins, self.outs, self.send, self.recv, self.lsem = ins, outs, send, recv, lsem
        self.x, self.y, self.c = _place()
        self.myj = 2 * self.x + self.y

    def _copy(self, a, k, landing):
        dx, dy = CHIP_RELS[k]
        pj = 2 * (self.x ^ dx) + (self.y ^ dy)
        return pltpu.make_async_remote_copy(
            src_ref=self.ins[a].at[pj], dst_ref=self.outs[a].at[pj if landing else self.myj],
            send_sem=self.send.at[a, k], recv_sem=self.recv.at[a, k],
            device_id=(self.x ^ dx, self.y ^ dy, self.c), device_id_type=MESH)

    def _local(self, a):
        return pltpu.make_async_copy(self.ins[a].at[self.myj], self.outs[a].at[self.myj], self.lsem.at[a])

    def start(self):
        for a in range(len(self.ins)):
            self._local(a).start()
            for k in range(3):
                self._copy(a, k, False).start()

    def finish(self):
        for a in range(len(self.ins)):
            for k in range(3):
                self._copy(a, k, True).wait_recv()
            for k in range(3):
                self._copy(a, k, False).wait_send()
            self._local(a).wait()


def _scatter_sems(n):
    return [pltpu.SemaphoreType.DMA((n, 3)), pltpu.SemaphoreType.DMA((n, 3)), pltpu.SemaphoreType.DMA((n,))]


def _gather_sems():
    return [pltpu.SemaphoreType.DMA((k,)) for k in _SplitGather.N_SEMS]


def _comm_begin(i, plan, last):
    @pl.when(i == 0)
    def _():
        plan.start()

    if hasattr(plan, "forward"):
        @pl.when(i == max(last - 3, 0))
        def _():
            plan.forward()


def _comm_end(i, plan, last):
    @pl.when(i == last)
    def _():
        plan.finish()


def _ada_exchange(msg, w_ada, b_ada, wblock):
    def body(msg_ref, w_ref, b_ref, wb_ref, all_ref, mod_ref, wg_ref, modp, send1, recv1, send2, recv2, lsem, *gsems):
        x, y, c = _place()
        me = 4 * x + 2 * y + c
        own = pltpu.make_async_copy(msg_ref, all_ref.at[me], lsem.at[0])
        own.start()

        def gather(k, rel, slot):
            dx, dy, dc = rel
            return pltpu.make_async_remote_copy(
                src_ref=msg_ref, dst_ref=all_ref.at[slot], send_sem=send1.at[k], recv_sem=recv1.at[k],
                device_id=(x ^ dx, y ^ dy, c ^ dc), device_id_type=MESH)

        for k, rel in enumerate(DEV_RELS):
            gather(k, rel, me).start()
        for k, (dx, dy, dc) in enumerate(DEV_RELS):
            gather(k, (dx, dy, dc), 4 * (x ^ dx) + 2 * (y ^ dy) + (c ^ dc)).wait_recv()
        for k, rel in enumerate(DEV_RELS):
            gather(k, rel, me).wait_send()
        own.wait()
        wgather = _SplitGather(wb_ref, wg_ref, *gsems)
        wgather.start()

        for d in range(8):
            cv = all_ref[d, 0:8, :]
            act = cv * _sigmoid(cv)
            modp[d] = _nn(act, w_ref[...], precision=HI) + b_ref[...]

        myj = 2 * x + y
        keep = pltpu.make_async_copy(modp.at[me], mod_ref.at[myj], lsem.at[1])
        keep.start()

        def scatter(k, rel):
            dx, dy = rel
            return pltpu.make_async_remote_copy(
                src_ref=modp.at[4 * (x ^ dx) + 2 * (y ^ dy) + c], dst_ref=mod_ref.at[myj],
                send_sem=send2.at[k], recv_sem=recv2.at[k], device_id=(x ^ dx, y ^ dy, c), device_id_type=MESH)

        def landed(k, rel):
            dx, dy = rel
            return pltpu.make_async_remote_copy(
                src_ref=modp.at[me], dst_ref=mod_ref.at[2 * (x ^ dx) + (y ^ dy)],
                send_sem=send2.at[k], recv_sem=recv2.at[k], device_id=(x ^ dx, y ^ dy, c), device_id_type=MESH)

        for k, rel in enumerate(CHIP_RELS):
            scatter(k, rel).start()
        for k, rel in enumerate(CHIP_RELS):
            landed(k, rel).wait_recv()
        for k, rel in enumerate(CHIP_RELS):
            scatter(k, rel).wait_send()
        keep.wait()
        wgather.forward()
        wgather.finish()

    return pl.pallas_call(
        body, name="ada_exchange", in_specs=[VM, VM, VM, ANY], out_specs=[VM, VM, ANY],
        out_shape=[jax.ShapeDtypeStruct((8, MSG_ROWS, D), F32), jax.ShapeDtypeStruct((4, 8, ADA_SH), F32),
                   jax.ShapeDtypeStruct((4,) + wblock.shape, wblock.dtype)],
        scratch_shapes=[pltpu.VMEM((8, 8, ADA_SH), F32), pltpu.SemaphoreType.DMA((7,)), pltpu.SemaphoreType.DMA((7,)),
                        pltpu.SemaphoreType.DMA((3,)), pltpu.SemaphoreType.DMA((3,)), pltpu.SemaphoreType.DMA((2,))]
        + _gather_sems(),
        compiler_params=pltpu.CompilerParams(vmem_limit_bytes=VMEM_LIMIT),
    )(msg, w_ada, b_ada, wblock)


def _chip_exchange(parts, name):
    n = len(parts)

    def body(*refs):
        plan = _Scatter(refs[:n], refs[n:2 * n], *refs[2 * n:])
        plan.start()
        plan.finish()

    return pl.pallas_call(
        body, name=name, in_specs=[ANY] * n, out_specs=[ANY] * n,
        out_shape=[jax.ShapeDtypeStruct(p.shape, p.dtype) for p in parts], scratch_shapes=_scatter_sems(n),
    )(*parts)


class _AllGather:
    def __init__(self, ins, outs, send, recv, lsem):
        self.ins, self.outs, self.send, self.recv, self.lsem = ins, outs, send, recv, lsem
        self.x, self.y, self.c = _place()
        self.me = 4 * self.x + 2 * self.y + self.c

    def _copy(self, a, k, landing):
        dx, dy, dc = DEV_RELS[k]
        peer = 4 * (self.x ^ dx) + 2 * (self.y ^ dy) + (self.c ^ dc)
        return pltpu.make_async_remote_copy(
            src_ref=self.ins[a], dst_ref=self.outs[a].at[peer if landing else self.me],
            send_sem=self.send.at[a, k], recv_sem=self.recv.at[a, k],
            device_id=(self.x ^ dx, self.y ^ dy, self.c ^ dc), device_id_type=MESH)

    def _local(self, a):
        return pltpu.make_async_copy(self.ins[a], self.outs[a].at[self.me], self.lsem.at[a])

    def start(self):
        for a in range(len(self.ins)):
            self._local(a).start()
            for k in range(7):
                self._copy(a, k, False).start()

    def finish(self):
        for a in range(len(self.ins)):
            for k in range(7):
                self._copy(a, k, True).wait_recv()
            for k in range(7):
                self._copy(a, k, False).wait_send()
            self._local(a).wait()


def _allgather_sems(n):
    return [pltpu.SemaphoreType.DMA((n, 7)), pltpu.SemaphoreType.DMA((n, 7)), pltpu.SemaphoreType.DMA((n,))]


class _Plans:
    def __init__(self, plans):
        self.plans = plans

    def start(self):
        for p in self.plans:
            p.start()

    def finish(self):
        for p in self.plans:
            p.finish()


def _pair_exchange(parts, name):
    n = len(parts)

    def body(*refs):
        ins, outs = refs[:n], refs[n:2 * n]
        send, recv = refs[2 * n:]
        x, y, c = _place()
        cps = [pltpu.make_async_remote_copy(
            src_ref=ins[a], dst_ref=outs[a], send_sem=send.at[a], recv_sem=recv.at[a],
            device_id=(x, y, 1 - c), device_id_type=MESH) for a in range(n)]
        for cp in cps:
            cp.start()
        for cp in cps:
            cp.wait_recv()
        for cp in cps:
            cp.wait_send()

    shapes = [jax.ShapeDtypeStruct(p.shape, p.dtype) for p in parts]
    return pl.pallas_call(
        body, name=name, in_specs=[ANY] * n, out_specs=[ANY] * n, out_shape=shapes,
        scratch_shapes=[pltpu.SemaphoreType.DMA((n,)), pltpu.SemaphoreType.DMA((n,))],
    )(*parts)


def _pair_swap_rows(parts, name):
    n = len(parts)

    def body(*refs):
        ins, outs = refs[:n], refs[n:2 * n]
        send, recv = refs[2 * n:]
        x, y, c = _place()
        half = ins[0].shape[1] // 2
        theirs = pl.ds(pl.multiple_of((1 - c) * half, 16), half)
        cps = [pltpu.make_async_remote_copy(
            src_ref=ins[a].at[pl.ds(0, 4), theirs], dst_ref=outs[a], send_sem=send.at[a], recv_sem=recv.at[a],
            device_id=(x, y, 1 - c), device_id_type=MESH) for a in range(n)]
        for cp in cps:
            cp.start()
        for cp in cps:
            cp.wait_recv()
        for cp in cps:
            cp.wait_send()

    shapes = [jax.ShapeDtypeStruct((4, p.shape[1] // 2, p.shape[2]), p.dtype) for p in parts]
    return pl.pallas_call(
        body, name=name, in_specs=[ANY] * n, out_specs=[ANY] * n, out_shape=shapes,
        scratch_shapes=[pltpu.SemaphoreType.DMA((n,)), pltpu.SemaphoreType.DMA((n,))],
    )(*parts)


def _add_blocks(p, q, name):
    n, rows, width = p.shape

    def body(p_ref, q_ref, o_ref):
        o_ref[...] = (p_ref[...].astype(F32) + q_ref[...].astype(F32)).astype(o_ref.dtype)

    blk = pl.BlockSpec((None, rows, width), lambda j: (j, 0, 0))
    return pl.pallas_call(
        body, grid=(n,), name=name, in_specs=[blk, blk], out_specs=blk,
        out_shape=jax.ShapeDtypeStruct(p.shape, p.dtype), compiler_params=_cparams(),
    )(p, q)


def _row_tile(rows, cap):
    best = rows
    for t in range(8, min(cap, rows) + 1, 8):
        if rows % t == 0:
            best = t
    return best if rows % 8 == 0 else rows


def _sum_slots(parts, name):
    n, rows, width = parts.shape
    tr = _row_tile(rows, 352)

    def body(p_ref, o_ref):
        acc = p_ref[0].astype(F32)
        for j in range(1, n):
            acc = acc + p_ref[j].astype(F32)
        o_ref[...] = acc

    return pl.pallas_call(
        body, grid=(rows // tr,), name=name,
        in_specs=[pl.BlockSpec((n, tr, width), lambda i: (0, i, 0))],
        out_specs=pl.BlockSpec((tr, width), lambda i: (i, 0)),
        out_shape=jax.ShapeDtypeStruct((rows, width), F32),
        compiler_params=_cparams(),
    )(parts)


def _adamw_math(g, w, m, v):
    m_new = ADAM_B1 * m + (1.0 - ADAM_B1) * g
    v_new = ADAM_B2 * v + (1.0 - ADAM_B2) * (g * g)
    m_hat = m_new / (1.0 - ADAM_B1 ** ADAM_STEP)
    v_hat = v_new / (1.0 - ADAM_B2 ** ADAM_STEP)
    delta = -ADAM_LR * (m_hat / (jnp.sqrt(v_hat) + ADAM_EPS) + ADAM_WD * w)
    return delta, m_new, v_new


def _adamw(grads, w, m, v, name):
    rows, width = w.shape
    tr = _row_tile(rows, 256 if width <= 1024 else 128)
    ng = len(grads)

    def body(*refs):
        g = refs[0][...]
        for r in refs[1:ng]:
            g = g + r[...]
        w_ref, m_ref, v_ref, g_out, d_out, m_out, v_out = refs[ng:]
        delta, m_new, v_new = _adamw_math(g, w_ref[...], m_ref[...], v_ref[...])
        g_out[...] = g
        d_out[...] = delta
        m_out[...] = m_new
        v_out[...] = v_new

    if rows % 8 == 0 or width % 512:
        blk, steps = pl.BlockSpec((tr, width), lambda i: (i, 0)), rows // tr
    else:
        blk, steps = pl.BlockSpec((rows, 256), lambda i: (0, i)), width // 256
    return pl.pallas_call(
        body, grid=(steps,), name=name,
        in_specs=[blk] * (ng + 3), out_specs=[blk] * 4,
        out_shape=[jax.ShapeDtypeStruct((rows, width), F32)] * 4,
        compiler_params=_cparams(),
    )(*grads, w, m, v)


def _adamw_ada(msgs, dmods, w, m, v):
    rows, width = w.shape
    tr = 128

    def body(c_ref, dm_ref, w_ref, m_ref, v_ref, g_out, d_out, m_out, v_out):
        cv = jnp.concatenate([c_ref[d, 0:1, :] for d in range(8)], axis=0)
        act = cv * _sigmoid(cv)
        g = _tn(act, dm_ref[...], precision=HI)
        delta, m_new, v_new = _adamw_math(g, w_ref[...], m_ref[...], v_ref[...])
        g_out[...] = g
        d_out[...] = delta
        m_out[...] = m_new
        v_out[...] = v_new

    blk = pl.BlockSpec((tr, width), lambda i: (i, 0))
    return pl.pallas_call(
        body, grid=(rows // tr,), name="adamw_w_ada",
        in_specs=[pl.BlockSpec((8, MSG_ROWS, tr), lambda i: (0, 0, i)), pl.BlockSpec((8, width), lambda i: (0, 0)),
                  blk, blk, blk],
        out_specs=[blk] * 4, out_shape=[jax.ShapeDtypeStruct((rows, width), F32)] * 4,
        compiler_params=_cparams(),
    )(msgs, dmods, w, m, v)


def _adamw_small(parts, w, m, v, name):
    n, rows, width = parts.shape

    def body(p_ref, w_ref, m_ref, v_ref, g_out, d_out, m_out, v_out):
        g = p_ref[0]
        for j in range(1, n):
            g = g + p_ref[j]
        delta, m_new, v_new = _adamw_math(g, w_ref[...], m_ref[...], v_ref[...])
        g_out[...] = g
        d_out[...] = delta
        m_out[...] = m_new
        v_out[...] = v_new

    return pl.pallas_call(
        body, name=name, in_specs=[VM] * 4, out_specs=[VM] * 4,
        out_shape=[jax.ShapeDtypeStruct((rows, width), F32)] * 4,
        compiler_params=pltpu.CompilerParams(vmem_limit_bytes=VMEM_LIMIT),
    )(parts, w, m, v)


SMALL_ROWS = 24


def _pad_row(vec, width=D):
    vec = vec.reshape(1, -1)
    return jnp.pad(vec, ((0, 0), (0, width - vec.shape[1])))


def _lanes_4_7(vec4):
    return jnp.zeros((1, 128), F32).at[0, NH:2 * NH].set(vec4.reshape(NH))


def kernel(x, c, w_ada, b_ada, norm_ffn1, ffn1_gate, ffn1_up, ffn1_down, norm_mix, w_in, conv_w, a_log, dt_bias, gdn_norm, pool_w, pool_scale, w_out, norm_ffn2, ffn2_gate, ffn2_up, ffn2_down, final_norm, loss_target, m_w_ada, m_b_ada, m_norm_ffn1, m_ffn1_gate, m_ffn1_up, m_ffn1_down, m_norm_mix, m_w_in, m_conv_w, m_a_log, m_dt_bias, m_gdn_norm, m_pool_w, m_pool_scale, m_w_out, m_norm_ffn2, m_ffn2_gate, m_ffn2_up, m_ffn2_down, m_final_norm, v_w_ada, v_b_ada, v_norm_ffn1, v_ffn1_gate, v_ffn1_up, v_ffn1_down, v_norm_mix, v_w_in, v_conv_w, v_a_log, v_dt_bias, v_gdn_norm, v_pool_w, v_pool_scale, v_w_out, v_norm_ffn2, v_ffn2_gate, v_ffn2_up, v_ffn2_down, v_final_norm):
    xs = x[0]
    tgt = loss_target[0]
    chip = 2 * lax.axis_index("x") + lax.axis_index("y")
    me = 2 * chip + lax.axis_index("c")

    fsh = FF // 4
    block_a = jnp.concatenate([ffn1_gate[0].T, ffn1_up[0].T, ffn1_down[0]], axis=0).astype(BF)
    block_b = jnp.concatenate([ffn2_gate[0].T, ffn2_up[0].T, ffn2_down[0], w_out[0],
                               jnp.pad(w_in[0].T, ((0, WIN_PAD - WIN_SH), (0, 0)))], axis=0).astype(BF)

    msg = jnp.concatenate([jnp.broadcast_to(c, (8, D)), jnp.pad(conv_w[0], ((0, 0), (0, D - 3 * GW // 4))),
                           jnp.zeros((MSG_ROWS - 12, D), F32)], axis=0)
    b_sh = lax.dynamic_slice(b_ada, (0, chip * ADA_SH), (1, ADA_SH))
    msgs, mod4, gath_a = _ada_exchange(msg, w_ada[0], b_sh, block_a)
    mod = mod4[:, 0, :].reshape(NMOD, D)
    mrow = [mod[i:i + 1] for i in range(NMOD)]
    conv_full = jnp.concatenate([msgs[2 * j, 8:12, :3 * GW // 4] for j in range(4)], axis=1)
    alog, dtb = _lanes_4_7(a_log), _lanes_4_7(dt_bias)
    gnm = gdn_norm.reshape(1, DH)
    pwb = pool_w[0].astype(BF)
    psc = pool_scale.reshape(1, PW)
    fin = final_norm.reshape(1, D)

    x1, f1, a1, b1, s1, gath_b = _ffn_fwd(xs, mrow[0], mrow[1], mrow[2], norm_ffn1, gath_a, 0, "ffn1_fwd", block_b)
    wo = gath_b[:, 3 * fsh:3 * fsh + D // 4, :].reshape(D, D)
    win_nat = gath_b[:, 3 * fsh + D // 4:3 * fsh + D // 4 + WIN_SH, :].reshape(DIN, D)
    winT = jnp.concatenate([win_nat[:4 * GW], win_nat[4 * GW + 2 * NH:], win_nat[4 * GW:4 * GW + 2 * NH],
                            jnp.zeros((128 - 2 * NH, D), BF)], axis=0)
    xq, ba, qn, kn, vv, z, pp, bg, gc = _mix_proj(x1, mrow[3], mrow[4], norm_mix, winT, conv_full, alog, dtb)
    x2, mixed, cat, o, sall, tmall = _mix_core(x1, mrow[5], qn, kn, vv, z, pp, bg, gc, gnm, pwb, psc, wo)
    lpart, dx3, dfin, f2, a2, b2, s2 = _ffn_fwd(x2, mrow[6], mrow[7], mrow[8], norm_ffn2, gath_b, 0, "ffn2_fwd",
                                                loss=(tgt, fin))
    loss = lax.psum(jnp.sum(lpart), ("x", "y", "c"))

    slots = lambda t: t.reshape(4, t.shape[0] // 4, D)
    dx2, da2, db2, h2, df2, dsh3, dsc3, dgt3, dn3 = _ffn_dgrad(
        dx3, x2, f2, a2, b2, mrow[6], mrow[7], mrow[8], norm_ffn2, gath_b, 0, "ffn2_dgrad")
    gg2, gu2, gd2 = _ffn_wgrad(da2, db2, s2, h2, df2, "ffn2_wgrad")
    dqn, dkn, dvv, dz, dpp, dbg, dgt2, dwo, dpw, dps, dgnm, *landed2 = _mix_core_bwd(
        dx2, mrow[5], mixed, cat, o, sall, qn, kn, vv, z, pp, bg, gc, gnm, pwb, psc, wo, tmall,
        scatter=[slots(gg2), slots(gu2), slots(gd2)])
    dx1, dwin, dcw, dal, ddt, dsh2, dsc2, dn2 = _mix_proj_bwd(
        dx2, dqn, dkn, dvv, dz, dpp, dbg, xq, ba, x1, mrow[3], mrow[4], norm_mix, winT, conv_full, alog, dtb)
    dwin_nat = jnp.concatenate([dwin[:4 * GW], dwin[4 * GW + PW:4 * GW + PW + 2 * NH], dwin[4 * GW:4 * GW + PW]], axis=0)
    dwin_sl = jnp.pad(dwin_nat.reshape(4, WIN_SH, D), ((0, 0), (0, WIN_PAD - WIN_SH), (0, 0)))
    dx0, da1, db1, h1, df1, dsh1, dsc1, dgt1, dn1 = _ffn_dgrad(
        dx1, xs, f1, a1, b1, mrow[0], mrow[1], mrow[2], norm_ffn1, gath_a, 0, "ffn1_dgrad")
    red = lambda t: jnp.sum(t, axis=0, keepdims=True)
    small = jnp.concatenate(
        [red(dn1), red(dn2), red(dn3), red(dfin),
         red(dsh1), red(dsc1), red(dgt1), red(dsh2), red(dsc2), red(dgt2), red(dsh3), red(dsc3), red(dgt3),
         _pad_row(red(dps)), _pad_row(red(dgnm)), _pad_row(red(dal)), _pad_row(red(ddt)),
         jnp.sum(dcw, axis=1).reshape(6, D), jnp.zeros((1, D), F32)], axis=0)
    gg1, gu1, gd1, land_wo, land_win, small_all, dpw_all = _ffn_wgrad(
        da1, db1, s1, h1, df1, "ffn1_wgrad", scatter=[slots(dwo), dwin_sl],
        allgather=[small, dpw.reshape(NG * 128, 128)])

    ffn1_blocks = [slots(gg1), slots(gu1), slots(gd1)]
    sibling_share = _pair_swap_rows(ffn1_blocks, "grad_pair_rows")
    own_rows = lax.axis_index("c") * (fsh // 2)
    pair_sums = [_add_blocks(lax.dynamic_slice(g, (0, own_rows, 0), (4, fsh // 2, D)), sh_, "pair_add_" + nm)
                 for g, sh_, nm in zip(ffn1_blocks, sibling_share, ("g1", "u1", "d1"))]
    landed1 = _chip_exchange(pair_sums, "grad_scatter")
    half_sums = [_sum_slots(t, "sum_" + nm) for t, nm in zip(landed1, ("g1", "u1", "d1"))]
    other_sums = _pair_exchange(half_sums, "grad_pair_ffn1")
    other_rows = (1 - lax.axis_index("c")) * (fsh // 2)
    full1 = [lax.dynamic_update_slice(lax.dynamic_update_slice(jnp.zeros((fsh, D), F32), mine_, (own_rows, 0)),
                                      theirs_, (other_rows, 0)) for mine_, theirs_ in zip(half_sums, other_sums)]

    landed = list(landed2) + [land_wo, land_win]
    psum = [_sum_slots(t, "sum_" + nm) for t, nm in zip(landed, ("g2", "u2", "d2", "wo", "win"))]
    qsum = _pair_exchange(psum, "grad_pair")

    def adamw_t(grads, w, m, v, name, rows):
        res = _adamw([g[:rows] for g in grads], w[0].T, m[0].T, v[0].T, name)
        return [t.T for t in res]

    upd = {}
    upd["ffn1_gate"] = adamw_t([full1[0]], ffn1_gate, m_ffn1_gate, v_ffn1_gate, "adamw_g1", fsh)
    upd["ffn1_up"] = adamw_t([full1[1]], ffn1_up, m_ffn1_up, v_ffn1_up, "adamw_u1", fsh)
    upd["ffn1_down"] = _adamw([full1[2]], ffn1_down[0], m_ffn1_down[0], v_ffn1_down[0], "adamw_d1")
    upd["ffn2_gate"] = adamw_t([psum[0], qsum[0]], ffn2_gate, m_ffn2_gate, v_ffn2_gate, "adamw_g2", fsh)
    upd["ffn2_up"] = adamw_t([psum[1], qsum[1]], ffn2_up, m_ffn2_up, v_ffn2_up, "adamw_u2", fsh)
    upd["ffn2_down"] = _adamw([psum[2], qsum[2]], ffn2_down[0], m_ffn2_down[0], v_ffn2_down[0], "adamw_d2")
    upd["w_out"] = _adamw([psum[3], qsum[3]], w_out[0], m_w_out[0], v_w_out[0], "adamw_wo")
    upd["w_in"] = adamw_t([psum[4], qsum[4]], w_in, m_w_in, v_w_in, "adamw_win", WIN_SH)
    dmods = lax.dynamic_slice(small_all[:, 4:4 + NMOD, :].reshape(8, NMOD * D), (0, chip * ADA_SH), (8, ADA_SH))
    upd["w_ada"] = _adamw_ada(msgs, dmods, w_ada[0], m_w_ada[0], v_w_ada[0])

    def pack_small(nf1, nmx, nf2, fn, bada, psc_, gn_, al_, dt_):
        return jnp.concatenate(
            [nf1.reshape(1, D), nmx.reshape(1, D), nf2.reshape(1, D), fn.reshape(1, D), bada.reshape(NMOD, D),
             _pad_row(psc_), _pad_row(gn_), _pad_row(_lanes_4_7(al_)), _pad_row(_lanes_4_7(dt_)),
             jnp.zeros((7, D), F32)], axis=0)

    ws = pack_small(norm_ffn1, norm_mix, norm_ffn2, final_norm, b_ada, pool_scale, gdn_norm, a_log, dt_bias)
    ms = pack_small(m_norm_ffn1, m_norm_mix, m_norm_ffn2, m_final_norm, m_b_ada, m_pool_scale, m_gdn_norm, m_a_log, m_dt_bias)
    vs = pack_small(v_norm_ffn1, v_norm_mix, v_norm_ffn2, v_final_norm, v_b_ada, v_pool_scale, v_gdn_norm, v_a_log, v_dt_bias)
    sm = _adamw_small(small_all, ws, ms, vs, "adamw_small")
    pw2 = lambda t: t.reshape(NG * 128, 128)
    upd_pw = _adamw_small(dpw_all, pw2(pool_w), pw2(m_pool_w), pw2(v_pool_w), "adamw_pool_w")
    csh = 3 * GW // 4
    gconv = lax.dynamic_slice(sm[0][17:23].reshape(4, 3 * GW), (0, chip * csh), (4, csh))
    upd["conv_w"] = _adamw([gconv], conv_w[0], m_conv_w[0], v_conv_w[0], "adamw_conv")

    def small_out(k):
        t = sm[k]
        return {
            "norm_ffn1": t[0:1], "norm_mix": t[1:2], "norm_ffn2": t[2:3], "final_norm": t[3],
            "b_ada": t[4:4 + NMOD].reshape(1, NMOD * D), "pool_scale": t[13:14, :PW], "gdn_norm": t[14:15, :DH],
            "a_log": t[15:16, NH:2 * NH], "dt_bias": t[16:17, NH:2 * NH],
        }

    order = ["w_ada", "b_ada", "norm_ffn1", "ffn1_gate", "ffn1_up", "ffn1_down", "norm_mix", "w_in", "conv_w", "a_log",
             "dt_bias", "gdn_norm", "pool_w", "pool_scale", "w_out", "norm_ffn2", "ffn2_gate", "ffn2_up", "ffn2_down",
             "final_norm"]
    outs = [loss, dx0[None]]
    for k in range(4):
        smk = small_out(k)
        for nm in order:
            if nm in upd:
                outs.append(upd[nm][k][None])
            elif nm == "pool_w":
                outs.append(upd_pw[k].reshape(1, NG, 128, 128))
            else:
                outs.append(smk[nm])
    return tuple(outs)
```

```python
import functools

import jax
import jax.numpy as jnp
from jax import lax
from jax.experimental import pallas as pl
from jax.experimental.pallas import tpu as pltpu

F32 = jnp.float32
BF = jnp.bfloat16

D = 1024
FF = 2816
FH = FF // 2
NH = 4
DH = 128
GW = NH * DH
CH = 64
PW = 512
NG = 4
POOL_WINDOWS = (2, 4, 8, 16)
HALO = 16
DIN = 4 * GW + 2 * NH + PW
DINP = 3 * GW + GW + PW + 128
EPS = 1e-6
ADAM_LR, ADAM_B1, ADAM_B2, ADAM_EPS, ADAM_WD, ADAM_STEP = 0.001, 0.9, 0.999, 1e-08, 0.01, 10

VMEM_LIMIT = 60 * 1024 * 1024

NT_DIMS = (((1,), (1,)), ((), ()))
TN_DIMS = (((0,), (0,)), ((), ()))
HI = lax.Precision.HIGHEST


def _nt(a, b, **kw):
    return lax.dot_general(a, b, NT_DIMS, preferred_element_type=F32, **kw)


def _tn(a, b, **kw):
    return lax.dot_general(a, b, TN_DIMS, preferred_element_type=F32, **kw)


def _nn(a, b, **kw):
    return jnp.dot(a, b, preferred_element_type=F32, **kw)


def _cparams(sem=("arbitrary",), **kw):
    return pltpu.CompilerParams(dimension_semantics=sem, vmem_limit_bytes=VMEM_LIMIT, **kw)


def _const_spec(shape):
    nd = len(shape)
    return pl.BlockSpec(shape, lambda *_: (0,) * nd, pipeline_mode=pl.Buffered(1))


def _row_spec(tm, width):
    return pl.BlockSpec((tm, width), lambda i: (i, 0))


def _sum8(v):
    return jnp.sum(v.reshape(v.shape[0] // 8, 8, v.shape[1]), axis=0)


def _sigmoid(v):
    return 0.5 * jnp.tanh(0.5 * v) + 0.5


def _tile(T, cap=512):
    return min(cap, T)


def _norm_mod_fwd(xv, gain, shift, scale):
    r = lax.rsqrt(jnp.mean(xv * xv, axis=-1, keepdims=True) + EPS)
    n = xv * r
    y = n * gain
    return n, r, y, y * (1.0 + scale) + shift


def _norm_mod_bwd(dh, n, r, y, gain, scale):
    dy = dh * (1.0 + scale)
    dn = dy * gain
    dx = r * (dn - n * jnp.mean(dn * n, axis=-1, keepdims=True))
    return dx, _sum8(dh), _sum8(dh * y), _sum8(dy * n)


def _ffn_wspecs(k0):
    return [pl.BlockSpec((4, FF // 4, D), lambda i, k=k0 + n: (0, k, 0), pipeline_mode=pl.Buffered(1)) for n in range(3)]


def _half(w_ref, j):
    return w_ref[2 * j:2 * j + 2].reshape(FH, D)


def _ffn_fwd(x, shift, scale, gate, gain, wall, k0, name, gather_block=None, loss=None):
    T = x.shape[0]
    tm = _tile(T)
    nt = T // tm
    comm = gather_block is not None
    assert not (comm and loss)

    def body(*refs):
        x_ref, sh_ref, sc_ref, gt_ref, gn_ref, wg_ref, wu_ref, wd_ref = refs[:8]
        if comm:
            xo_ref, f_ref, sa_ref, ga_ref, s_ref = refs[9:14]
            plan = _SplitGather(refs[8], refs[14], *refs[15:])
            _comm_begin(pl.program_id(0), plan, nt - 1)
        elif loss:
            t_ref, fg_ref, ls_ref, xo_ref, dfg_ref, f_ref, sa_ref, ga_ref, s_ref = refs[8:17]
        else:
            xo_ref, f_ref, sa_ref, ga_ref, s_ref = refs[8:13]
        xv = x_ref[...]
        _, _, _, h = _norm_mod_fwd(xv, gn_ref[...], sh_ref[...], sc_ref[...])
        hb = h.astype(BF)
        facc = jnp.zeros((tm, D), F32)
        for j in range(2):
            cols = slice(j * FH, (j + 1) * FH)
            a = _nt(hb, _half(wg_ref, j))
            b = _nt(hb, _half(wu_ref, j))
            sig = _sigmoid(a)
            sa = a * sig
            sa_ref[:, cols] = sa.astype(BF)
            ga_ref[:, cols] = (b * (sig * (1.0 + a * (1.0 - sig)))).astype(BF)
            s = (sa * b).astype(BF)
            s_ref[:, cols] = s
            facc = facc + _nn(s, _half(wd_ref, j))
        f_ref[...] = facc
        xo = xv + 0.5 * gt_ref[...] * facc
        if loss:
            i = pl.program_id(0)
            lsum, dxo, dfg = _loss_math(xo, t_ref[...], fg_ref[...])
            xo_ref[...] = dxo

            @pl.when(i == 0)
            def _():
                ls_ref[...] = lsum
                dfg_ref[...] = dfg

            @pl.when(i > 0)
            def _():
                ls_ref[...] += lsum
                dfg_ref[...] += dfg
        else:
            xo_ref[...] = xo
        if comm:
            _comm_end(pl.program_id(0), plan, nt - 1)

    vec = _const_spec((1, D))
    tok = jax.ShapeDtypeStruct((T, D), F32)
    extra_in, extra_specs = ([gather_block], [ANY]) if comm else (list(loss), [_row_spec(tm, D), vec]) if loss else ([], [])
    head_specs, head_shapes = [_row_spec(tm, D)], [tok]
    if loss:
        head_specs = [pl.BlockSpec((8, 128), lambda i: (0, 0)), _row_spec(tm, D), pl.BlockSpec((8, D), lambda i: (0, 0))]
        head_shapes = [jax.ShapeDtypeStruct((8, 128), F32), tok, jax.ShapeDtypeStruct((8, D), F32)]
    return pl.pallas_call(
        body, grid=(nt,), name=name,
        in_specs=[_row_spec(tm, D), vec, vec, vec, vec] + _ffn_wspecs(k0) + extra_specs,
        out_specs=head_specs + [_row_spec(tm, D)] + [_row_spec(tm, FF)] * 3 + [ANY] * comm,
        out_shape=head_shapes + [tok] + [jax.ShapeDtypeStruct((T, FF), BF)] * 3
        + ([jax.ShapeDtypeStruct((4,) + gather_block.shape, gather_block.dtype)] if comm else []),
        scratch_shapes=_gather_sems() if comm else [],
        compiler_params=_cparams(),
    )(x, shift, scale, gate, gain, wall, wall, wall, *extra_in)


def _ffn_dgrad(dxo, x, f, sa, ga, shift, scale, gate, gain, wall, k0, name):
    T = x.shape[0]
    tm = _tile(T)
    nt = T // tm
    wspecs = _ffn_wspecs(k0)
    vec = _const_spec((1, D))
    acc = pl.BlockSpec((8, D), lambda i: (0, 0))
    accs = jax.ShapeDtypeStruct((8, D), F32)

    def body_a(dxo_ref, f_ref, sa_ref, ga_ref, gt_ref, wd_ref, da_ref, db_ref, df_ref, dgt_ref):
        i = pl.program_id(0)
        dxo_v = dxo_ref[...]
        dgate = _sum8(0.5 * f_ref[...] * dxo_v)
        dfb = (0.5 * gt_ref[...] * dxo_v).astype(BF)
        df_ref[...] = dfb
        for j in range(2):
            cols = slice(j * FH, (j + 1) * FH)
            ds = _nt(dfb, _half(wd_ref, j))
            da_ref[:, cols] = (ds * ga_ref[:, cols].astype(F32)).astype(BF)
            db_ref[:, cols] = (ds * sa_ref[:, cols].astype(F32)).astype(BF)

        @pl.when(i == 0)
        def _():
            dgt_ref[...] = dgate

        @pl.when(i > 0)
        def _():
            dgt_ref[...] += dgate

    wide = jax.ShapeDtypeStruct((T, FF), BF)
    da, db, df, dgt = pl.pallas_call(
        body_a, grid=(nt,), name=name + "_a",
        in_specs=[_row_spec(tm, D), _row_spec(tm, D), _row_spec(tm, FF), _row_spec(tm, FF), vec, wspecs[2]],
        out_specs=[_row_spec(tm, FF), _row_spec(tm, FF), _row_spec(tm, D), acc],
        out_shape=[wide, wide, jax.ShapeDtypeStruct((T, D), BF), accs],
        compiler_params=_cparams(),
    )(dxo, f, sa, ga, gate, wall)

    def body_b(dxo_ref, x_ref, da_ref, db_ref, sh_ref, sc_ref, gn_ref, wg_ref, wu_ref,
               dx_ref, h_ref, dsh_ref, dsc_ref, dgn_ref):
        i = pl.program_id(0)
        gain_v, scale_v = gn_ref[...], sc_ref[...]
        n, r, y, h = _norm_mod_fwd(x_ref[...], gain_v, sh_ref[...], scale_v)
        h_ref[...] = h.astype(BF)
        dh = _nn(da_ref[...], wg_ref[...].reshape(FF, D)) + _nn(db_ref[...], wu_ref[...].reshape(FF, D))
        dxn, dsh, dsc, dgn = _norm_mod_bwd(dh, n, r, y, gain_v, scale_v)
        dx_ref[...] = dxo_ref[...] + dxn

        @pl.when(i == 0)
        def _():
            dsh_ref[...] = dsh
            dsc_ref[...] = dsc
            dgn_ref[...] = dgn

        @pl.when(i > 0)
        def _():
            dsh_ref[...] += dsh
            dsc_ref[...] += dsc
            dgn_ref[...] += dgn

    dx, h, dsh, dsc, dgn = pl.pallas_call(
        body_b, grid=(nt,), name=name + "_b",
        in_specs=[_row_spec(tm, D), _row_spec(tm, D), _row_spec(tm, FF), _row_spec(tm, FF), vec, vec, vec,
                  wspecs[0], wspecs[1]],
        out_specs=[_row_spec(tm, D), _row_spec(tm, D), acc, acc, acc],
        out_shape=[jax.ShapeDtypeStruct((T, D), F32), jax.ShapeDtypeStruct((T, D), BF), accs, accs, accs],
        compiler_params=_cparams(),
    )(dxo, x, da, db, shift, scale, gain, wall, wall)
    return dx, da, db, h, df, dsh, dsc, dgt, dgn


def _ffn_wgrad(da, db, s, h, df, name, scatter=(), allgather=()):
    T = h.shape[0]
    tk = _tile(T, 512)
    nk = T // tk
    nj = 2
    groups = [g for g in (
        (list(scatter), _Scatter, _scatter_sems, lambda t: t.shape),
        (list(allgather), _AllGather, _allgather_sems, lambda t: (8,) + t.shape),
    ) if g[0]]
    extra = [t for g in groups for t in g[0]]
    ne = len(extra)

    def body(*refs):
        da_ref, db_ref, s_ref, h_ref, df_ref = refs[:5]
        og_ref, ou_ref, od_ref = refs[5 + ne:8 + ne]
        ag, au, ad = refs[8 + 2 * ne:11 + 2 * ne]
        k = pl.program_id(1)
        step = pl.program_id(0) * nk + k
        if ne:
            plans, at, sem_at = [], 0, 11 + 2 * ne
            for arrs, make, _, _ in groups:
                n = len(arrs)
                plans.append(make(refs[5 + at:5 + at + n], refs[8 + ne + at:8 + ne + at + n], *refs[sem_at:sem_at + 3]))
                at, sem_at = at + n, sem_at + 3
            plan = _Plans(plans)
            _comm_begin(step, plan, nj * nk - 1)

        @pl.when(k == 0)
        def _():
            for acc in (ag, au, ad):
                acc[...] = jnp.zeros((FH, D), F32)

        ag[...] += _tn(da_ref[...], h_ref[...])
        au[...] += _tn(db_ref[...], h_ref[...])
        ad[...] += _tn(s_ref[...], df_ref[...])

        @pl.when(k == nk - 1)
        def _():
            og_ref[...] = ag[...].astype(BF)
            ou_ref[...] = au[...].astype(BF)
            od_ref[...] = ad[...].astype(BF)

        if ne:
            _comm_end(step, plan, nj * nk - 1)

    colblk = pl.BlockSpec((tk, FH), lambda j, k: (k, j))
    rowblk = pl.BlockSpec((tk, D), lambda j, k: (k, 0))
    outblk = pl.BlockSpec((FH, D), lambda j, k: (j, 0))
    outs = jax.ShapeDtypeStruct((nj * FH, D), BF)
    return pl.pallas_call(
        body, grid=(nj, nk), name=name,
        in_specs=[colblk, colblk, colblk, rowblk, rowblk] + [ANY] * ne,
        out_specs=[outblk, outblk, outblk] + [ANY] * ne,
        out_shape=[outs, outs, outs] + [jax.ShapeDtypeStruct(shape(t), t.dtype) for g in groups for t, shape in
                                        ((t, g[3]) for t in g[0])],
        scratch_shapes=[pltpu.VMEM((FH, D), F32)] * 3 + [sm for g in groups for sm in g[2](len(g[0]))],
        compiler_params=_cparams(("arbitrary", "arbitrary")),
    )(da, db, s, h, df, *extra)


def _loss_math(xv, target, gain_v):
    r = lax.rsqrt(jnp.mean(xv * xv, axis=-1, keepdims=True) + EPS)
    n = xv * r
    err = n * gain_v - target
    e2 = err * err
    part = e2[:, 0:128]
    for q in range(1, D // 128):
        part = part + e2[:, q * 128:(q + 1) * 128]
    dy = err * (1.0 / D)
    dn = dy * gain_v
    return _sum8(part) * (0.5 / D), r * (dn - n * jnp.mean(dn * n, axis=-1, keepdims=True)), _sum8(dy * n)


def _seg_cumsum(v, row_in_chunk, reverse=False):
    n = v.shape[0]
    s = 1
    while s < CH:
        if reverse:
            moved = pltpu.roll(v, n - s, 0)
            ok = row_in_chunk < CH - s
        else:
            moved = pltpu.roll(v, s, 0)
            ok = row_in_chunk >= s
        v = v + jnp.where(ok, moved, 0.0)
        s *= 2
    return v


def _conv_silu(xq_ext_ref, cw_ref, tm):
    c = cw_ref[0:1, :] * xq_ext_ref[pl.ds(5, tm), :]
    for j in range(1, 4):
        c = c + cw_ref[j:j + 1, :] * xq_ext_ref[pl.ds(5 + j, tm), :]
    return c, _sigmoid(c)


def _gates(ba, alog, dtb, lane):
    beta = _sigmoid(ba)
    arg = ba + dtb
    softplus = jnp.maximum(arg, 0.0) + jnp.log(1.0 + jnp.exp(-jnp.abs(arg)))
    g = -jnp.exp(alog) * softplus
    return jnp.where(lane < NH, beta, 0.0), jnp.where((lane >= NH) & (lane < 2 * NH), g, 0.0), _sigmoid(arg)


def _mix_proj(x, shift, scale, gain, winT, conv_w, alog, dtb):
    T = x.shape[0]
    tm = _tile(T)

    def body(x_ref, sh_ref, sc_ref, gn_ref, w_ref, cw_ref, al_ref, dt_ref,
             xq_ref, ba_ref, qn_ref, kn_ref, v_ref, z_ref, p_ref, bg_ref, gc_ref, ext):
        i = pl.program_id(0)
        _, _, _, h = _norm_mod_fwd(x_ref[...], gn_ref[...], sh_ref[...], sc_ref[...])
        hb = h.astype(BF)

        @pl.when(i == 0)
        def _():
            ext[pl.ds(0, 8), :] = jnp.zeros((8, 3 * GW), F32)

        xq = _nt(hb, w_ref[pl.ds(0, 3 * GW), :])
        xq_ref[...] = xq
        ext[pl.ds(8, tm), :] = xq
        z_ref[...] = _nt(hb, w_ref[pl.ds(3 * GW, GW), :])
        p_ref[...] = _nt(hb, w_ref[pl.ds(4 * GW, PW), :])
        ba = _nt(hb, w_ref[pl.ds(4 * GW + PW, 128), :])
        ba_ref[...] = ba

        c, sg = _conv_silu(ext, cw_ref, tm)
        ext[pl.ds(0, 8), :] = ext[pl.ds(tm, 8), :]
        qt = c * sg
        for hd in range(NH):
            cq = slice(hd * DH, (hd + 1) * DH)
            ck = slice(GW + hd * DH, GW + (hd + 1) * DH)
            qh, kh = qt[:, cq], qt[:, ck]
            qn_ref[:, cq] = qh * (lax.rsqrt(jnp.sum(qh * qh, axis=-1, keepdims=True) + EPS) * DH ** -0.5)
            kn_ref[:, cq] = kh * lax.rsqrt(jnp.sum(kh * kh, axis=-1, keepdims=True) + EPS)
        v_ref[...] = qt[:, 2 * GW:3 * GW]

        lane = lax.broadcasted_iota(jnp.int32, (tm, 128), 1)
        row = lax.broadcasted_iota(jnp.int32, (tm, 128), 0) % CH
        beta, g, _ = _gates(ba, al_ref[...], dt_ref[...], lane)
        bg_ref[...] = beta + g
        gc_ref[...] = _seg_cumsum(g, row)

    wide = lambda w: _row_spec(tm, w)
    shp = lambda w: jax.ShapeDtypeStruct((T, w), F32)
    return pl.pallas_call(
        body, grid=(T // tm,), name="mix_proj",
        in_specs=[wide(D), _const_spec((1, D)), _const_spec((1, D)), _const_spec((1, D)), _const_spec((DINP, D)),
                  _const_spec((4, 3 * GW)), _const_spec((1, 128)), _const_spec((1, 128))],
        out_specs=[wide(3 * GW), wide(128), wide(GW), wide(GW), wide(GW), wide(GW), wide(PW), wide(128), wide(128)],
        out_shape=[shp(3 * GW), shp(128), shp(GW), shp(GW), shp(GW), shp(GW), shp(PW), shp(128), shp(128)],
        scratch_shapes=[pltpu.VMEM((tm + 8, 3 * GW), F32)],
        compiler_params=_cparams(),
    )(x, shift, scale, gain, winT, conv_w, alog, dtb)


R2 = 2 * CH
TRI_PREC = None


def _tmm(fn, a, b):
    if TRI_PREC is None:
        return fn(a.astype(BF), b.astype(BF))
    return fn(a, b, precision=TRI_PREC)


def _pair_consts():
    ii = lax.broadcasted_iota(jnp.int32, (R2, R2), 0)
    jj = lax.broadcasted_iota(jnp.int32, (R2, R2), 1)
    same = (ii < CH) == (jj < CH)
    r = lax.broadcasted_iota(jnp.int32, (R2, 1), 0)
    return dict(causal=same & (ii >= jj), strict=same & (ii > jj), eye=(ii == jj).astype(F32), rowA=r < CH,
                last=(r == CH - 1) | (r == R2 - 1),
                rowS=lax.broadcasted_iota(jnp.int32, (2 * DH, 1), 0) < DH)


def _tri_inverse_many(ms, eye):
    pws = [-m for m in ms]
    ts = [eye + p for p in pws]
    for _ in range(5):
        pws = [_mm_split(p, p) for p in pws]
        ts = [_mm_split(t, eye + p) for t, p in zip(ts, pws)]
    return ts


def _mm_split(a, b):
    ah, bh = a.astype(BF), b.astype(BF)
    al, bl = (a - ah.astype(F32)).astype(BF), (b - bh.astype(F32)).astype(BF)
    return _nn(ah, bh) + _nn(ah, bl) + _nn(al, bh)


def _egl_rows(gl):
    egl = jnp.exp(gl)
    return egl, jnp.concatenate([jnp.broadcast_to(egl[0:1], (DH, 1)), jnp.broadcast_to(egl[CH:CH + 1], (DH, 1))], axis=0)


def _pair_intra(items, cn, tms=None):
    causal, rowA = cn["causal"], cn["rowA"]

    def bd(t):
        return jnp.concatenate([jnp.where(rowA, t, 0.0), jnp.where(rowA, 0.0, t)], axis=1).astype(BF)

    outs = []
    for q, k, v, beta, gcv, gl in items:
        gc_b = jnp.broadcast_to(gcv, (R2, R2))
        gam = jnp.where(causal, jnp.exp(jnp.where(causal, gc_b - gc_b.T, 0.0)), 0.0)
        kb = k * beta
        kbf = k.astype(BF)
        P = _nt(kb.astype(BF), kbf)
        QK = _nt(q.astype(BF), kbf)
        E = jnp.exp(gcv)
        outs.append(dict(gam=gam, kb=kb, vb=v * beta, P=P, QK=QK, E=E, Fd=jnp.exp(gl - gcv), kbE=kb * E,
                         Q=QK * gam, qE_bd=bd(q * E)))
    if tms is None:
        tms = _tri_inverse_many([jnp.where(cn["strict"], d["P"] * d["gam"], 0.0) for d in outs], cn["eye"])
    for d, tm_, (q, k, v, beta, gcv, gl) in zip(outs, tms, items):
        d["Tm"] = tm_
        d["u"] = _tmm(_nn, tm_, d["vb"])
        d["w_bd"] = bd(_tmm(_nn, tm_, d["kbE"]))
        d["kF_bd"] = bd(k * d["Fd"])
    return outs


def _pair_scan(it, S, egl_st):
    Sb = S.astype(BF)
    vn = it["u"] - _nn(it["w_bd"], Sb)
    vnb = vn.astype(BF)
    o = _nn(it["qE_bd"], Sb) + _nn(it["Q"].astype(BF), vnb)
    return vnb, o, S * egl_st + _tn(it["kF_bd"], vnb)


def _pair_forward(q, k, v, beta, gcv, gl, S, cn):
    fw = _pair_intra([(q, k, v, beta, gcv, gl)], cn)[0]
    fw["egl"], fw["egl_st"] = _egl_rows(gl)
    fw["vnb"], fw["o"], fw["S_new"] = _pair_scan(fw, S, fw["egl_st"])
    return fw


def _stack_heads(ref, rows, pair):
    return jnp.concatenate([ref[rows, (2 * pair) * DH:(2 * pair + 1) * DH],
                            ref[rows, (2 * pair + 1) * DH:(2 * pair + 2) * DH]], axis=0)


def _stack_cols(val, lane_a, lane_b, bcast_rows=None):
    a, b = val[:, lane_a:lane_a + 1], val[:, lane_b:lane_b + 1]
    if bcast_rows:
        a, b = jnp.broadcast_to(a, (bcast_rows, 1)), jnp.broadcast_to(b, (bcast_rows, 1))
    return jnp.concatenate([a, b], axis=0)


def _pool_windows(ext, tm, reverse):
    n = tm + HALO
    outs = []
    for gi in range(NG):
        a = ext[:, gi * 128:(gi + 1) * 128]
        s = 1
        while s < POOL_WINDOWS[gi]:
            a = a + pltpu.roll(a, (n - s) if reverse else s, 0)
            s *= 2
        outs.append(a[0:tm] if reverse else a[HALO:HALO + tm])
    return jnp.concatenate(outs, axis=1)


def _pool_count(tm, tile_index):
    t1 = (lax.broadcasted_iota(jnp.int32, (tm, PW), 0) + tile_index * tm + 1).astype(F32)
    win = jnp.concatenate([jnp.full((tm, 128), float(w), F32) for w in POOL_WINDOWS], axis=1)
    return 1.0 / jnp.minimum(t1, win)


def _chunk_item(qn_ref, kn_ref, v_ref, bg_ref, gc_ref, c, pr):
    r0 = pl.multiple_of(c * CH, CH)
    rows = pl.ds(r0, CH)
    bgv = bg_ref[rows, :]
    gcv_all = gc_ref[rows, :]
    gl_all = gc_ref[pl.ds(r0 + CH - 1, 1), :]
    ha, hb = 2 * pr, 2 * pr + 1
    return (_stack_heads(qn_ref, rows, pr), _stack_heads(kn_ref, rows, pr), _stack_heads(v_ref, rows, pr),
            _stack_cols(bgv, ha, hb), _stack_cols(gcv_all, NH + ha, NH + hb), _stack_cols(gl_all, NH + ha, NH + hb, CH))


CHUNK_GROUP = 4


def _mix_core(x, gate, qn, kn, v, z, p, bg, gc, gnorm, pool_w, pool_scale, w_out):
    T = x.shape[0]
    tm = _tile(T)
    nc = tm // CH
    cg = CHUNK_GROUP if nc % CHUNK_GROUP == 0 else 1
    npb = nc * (NH // 2)

    def body(x_ref, gt_ref, qn_ref, kn_ref, v_ref, z_ref, p_ref, bg_ref, gc_ref, gnm_ref, pw_ref, ps_ref, wo_ref,
             xo_ref, mx_ref, cat_ref, o_ref, sall_ref, tmall_ref, S_scr, pext, u_s, w_s, qe_s, kf_s, q_s):
        i = pl.program_id(0)

        @pl.when(i == 0)
        def _():
            S_scr[...] = jnp.zeros((NH * DH, DH), F32)
            pext[pl.ds(0, HALO), :] = jnp.zeros((HALO, PW), F32)

        cn = _pair_consts()

        def intra(g, carry):
            idx = [(g * cg + dc, pr) for dc in range(cg) for pr in range(NH // 2)]
            res = _pair_intra([_chunk_item(qn_ref, kn_ref, v_ref, bg_ref, gc_ref, c, pr) for c, pr in idx], cn)
            for (c, pr), d in zip(idx, res):
                pi = c * (NH // 2) + pr
                u_s[pi] = d["u"]
                w_s[pi] = d["w_bd"]
                qe_s[pi] = d["qE_bd"]
                kf_s[pi] = d["kF_bd"]
                q_s[pi] = d["Q"].astype(BF)
                tmall_ref[pi] = d["Tm"].astype(BF)
            return carry

        lax.fori_loop(0, nc // cg, intra, 0)

        def scan(c, carry):
            r0 = pl.multiple_of(c * CH, CH)
            rows = pl.ds(r0, CH)
            gl_all = gc_ref[pl.ds(r0 + CH - 1, 1), :]
            for pr in range(NH // 2):
                ha, hb = 2 * pr, 2 * pr + 1
                pi = c * (NH // 2) + pr
                S = S_scr[pl.ds(pr * 2 * DH, 2 * DH), :]
                sall_ref[c, ha:hb + 1] = S.reshape(2, DH, DH)
                _, egl_st = _egl_rows(_stack_cols(gl_all, NH + ha, NH + hb, CH))
                it = dict(u=u_s[pi], w_bd=w_s[pi], qE_bd=qe_s[pi], kF_bd=kf_s[pi], Q=q_s[pi])
                _, o, S_new = _pair_scan(it, S, egl_st)
                o_ref[rows, ha * DH:(ha + 1) * DH] = o[0:CH]
                o_ref[rows, hb * DH:(hb + 1) * DH] = o[CH:R2]
                S_scr[pl.ds(pr * 2 * DH, 2 * DH), :] = S_new
            return carry

        lax.fori_loop(0, nc, scan, 0)

        for hd in range(NH):
            cols = slice(hd * DH, (hd + 1) * DH)
            oh = o_ref[:, cols]
            zh = z_ref[:, cols]
            r = lax.rsqrt(jnp.mean(oh * oh, axis=-1, keepdims=True) + EPS)
            cat_ref[:, cols] = (oh * r * gnm_ref[...] * (zh * _sigmoid(zh))).astype(BF)

        pv = p_ref[...]
        pext[pl.ds(HALO, tm), :] = pv
        pooled = _pool_windows(pext[...], tm, False) * _pool_count(tm, i) - pv
        pext[pl.ds(0, HALO), :] = pext[pl.ds(tm, HALO), :]
        for gi in range(NG):
            cols = slice(gi * 128, (gi + 1) * 128)
            pm = _nn(pooled[:, cols].astype(BF), pw_ref[gi])
            cat_ref[:, GW + gi * 128:GW + (gi + 1) * 128] = (pm * ps_ref[:, cols]).astype(BF)

        mixed = _nn(cat_ref[...], wo_ref[...])
        mx_ref[...] = mixed
        xo_ref[...] = x_ref[...] + gt_ref[...] * mixed

    wide = lambda w: _row_spec(tm, w)
    return pl.pallas_call(
        body, grid=(T // tm,), name="mix_core",
        in_specs=[wide(D), _const_spec((1, D)), wide(GW), wide(GW), wide(GW), wide(GW), wide(PW), wide(128), wide(128),
                  _const_spec((1, DH)), _const_spec((NG, 128, 128)), _const_spec((1, PW)), _const_spec((D, D))],
        out_specs=[wide(D), wide(D), wide(D), wide(GW), pl.BlockSpec((nc, NH, DH, DH), lambda i: (i, 0, 0, 0)),
                   pl.BlockSpec((npb, R2, R2), lambda i: (i, 0, 0))],
        out_shape=[jax.ShapeDtypeStruct((T, D), F32), jax.ShapeDtypeStruct((T, D), F32),
                   jax.ShapeDtypeStruct((T, D), BF), jax.ShapeDtypeStruct((T, GW), F32),
                   jax.ShapeDtypeStruct((T // CH, NH, DH, DH), F32),
                   jax.ShapeDtypeStruct((T // CH * (NH // 2), R2, R2), BF)],
        scratch_shapes=[pltpu.VMEM((NH * DH, DH), F32), pltpu.VMEM((tm + HALO, PW), F32),
                        pltpu.VMEM((npb, R2, DH), F32), pltpu.VMEM((npb, R2, 2 * DH), BF),
                        pltpu.VMEM((npb, R2, 2 * DH), BF), pltpu.VMEM((npb, R2, 2 * DH), BF),
                        pltpu.VMEM((npb, R2, R2), BF)],
        compiler_params=_cparams(),
    )(x, gate, qn, kn, v, z, p, bg, gc, gnorm, pool_w, pool_scale, w_out)


def _pair_scan_bwd(it, S, dSn, do, egl, egl_st, cn):
    bf = lambda t: t.astype(BF)
    rowA, rowS = cn["rowA"], cn["rowS"]
    sel = lambda t: jnp.where(rowA, t[:, 0:DH], t[:, DH:2 * DH])
    Sb, dSb, dob = bf(S), bf(dSn), bf(do)
    vnb = bf(it["u"] - _nn(it["w_bd"], Sb))
    dvn = _tn(bf(it["Q"]), dob) + _nn(it["kF_bd"], dSb)
    dQ = _nt(dob, vnb)
    dqE = sel(_nt(dob, Sb))
    dkF = sel(_nt(vnb, dSb))
    dvnb = bf(dvn)
    dw = -sel(_nt(dvnb, Sb))
    dS_new = _tn(it["qE_bd"], dob) + egl_st * dSn - _tn(it["w_bd"], dvnb)
    prod = jnp.sum(dSn * S, axis=1, keepdims=True)
    d_egl_a = jnp.sum(jnp.where(rowS, prod, 0.0), axis=0, keepdims=True)
    d_egl_b = jnp.sum(jnp.where(rowS, 0.0, prod), axis=0, keepdims=True)
    return dict(dvn=dvn, dQ=dQ, dqE=dqE, dkF=dkF, dw=dw, degl=jnp.where(rowA, d_egl_a, d_egl_b) * egl), dS_new


def _pair_intra_bwd(items, cn):
    bf = lambda t: t.astype(BF)
    rowA = cn["rowA"]
    for d in items:
        d["kb"] = d["k"] * d["beta"]
        d["E"] = jnp.exp(d["gcv"])
        d["Fd"] = jnp.exp(d["gl"] - d["gcv"])
        d["TmT"] = d["Tm"].T
        d["dvb"] = _tmm(_nn, d["TmT"], d["dvn"])
        d["dkbE"] = _tmm(_nn, d["TmT"], d["dw"])
        dTm = _tmm(_nt, d["dvn"], d["v"] * d["beta"]) + _tmm(_nt, d["dw"], d["kb"] * d["E"])
        d["X"] = _tmm(_nt, dTm, d["Tm"])
    for d in items:
        d["dA"] = -_tmm(_nn, d["TmT"], d["X"])
    outs = []
    for d in items:
        q, k, v, beta, gam = d["q"], d["k"], d["v"], d["beta"], d["gam"]
        N = jnp.where(cn["strict"], d["dA"] * gam, 0.0)
        Rm = d["dQ"] * gam
        Wm = Rm * d["QK"] + N * d["P"]
        dgc = jnp.sum(Wm, axis=1, keepdims=True) - jnp.sum(Wm.T, axis=1, keepdims=True)
        kbf, Nb, Rb = bf(k), bf(N), bf(Rm)
        E, Fd = d["E"], d["Fd"]
        dq = d["dqE"] * E + _nn(Rb, kbf)
        dkb = d["dkbE"] * E + _nn(Nb, kbf)
        dk = d["dkF"] * Fd + _tn(Rb, bf(q)) + _tn(Nb, bf(d["kb"])) + beta * dkb
        dbeta = jnp.sum(dkb * k + d["dvb"] * v, axis=1, keepdims=True)
        dE = jnp.sum(d["dqE"] * q + d["dkbE"] * d["kb"], axis=1, keepdims=True)
        fdf = jnp.sum(d["dkF"] * k, axis=1, keepdims=True) * Fd
        dgl = d["degl"] + jnp.where(rowA, jnp.sum(jnp.where(rowA, fdf, 0.0), axis=0, keepdims=True),
                                    jnp.sum(jnp.where(rowA, 0.0, fdf), axis=0, keepdims=True))
        dgc = dgc + dE * E - fdf + jnp.where(cn["last"], dgl, 0.0)
        outs.append((dq, dk, beta * d["dvb"], dbeta, dgc))
    return outs


def _mix_core_bwd(dxo, gate, mixed, cat, o, sall, qn, kn, v, z, p, bg, gc, gnorm, pool_w, pool_scale, w_out, tm_all,
                  scatter=()):
    T = dxo.shape[0]
    tm = _tile(T, 256)
    nt = T // tm
    nc = tm // CH
    ns = len(scatter)
    cg = CHUNK_GROUP if nc % CHUNK_GROUP == 0 else 1
    npb = nc * (NH // 2)

    def body(*refs):
        (dx_ref, gt_ref, mx_ref, cat_ref, o_ref, sall_ref, qn_ref, kn_ref, v_ref, z_ref, p_ref, ph_ref, bg_ref,
         gc_ref, gnm_ref, pw_ref, ps_ref, wo_ref, tmall_ref) = refs[:19]
        (dq_ref, dk_ref, dv_ref, dz_ref, dp_ref, dbg_ref, dgt_ref, dwo_ref, dpw_ref, dps_ref,
         dgn_ref) = refs[19 + ns:30 + ns]
        dS_scr, pext, yext, do_buf, dwo_acc = refs[30 + 2 * ns:35 + 2 * ns]
        (gam_s, p_s, qk_s, tm_s, dqq_s, u_s, dvn_s, dqe_s, dkf_s, dw_s, w_s, qe_s, kf_s, q_s,
         degl_s) = refs[35 + 2 * ns:50 + 2 * ns]
        i = pl.program_id(0)
        ti = nt - 1 - i
        if ns:
            plan = _Scatter(refs[19:19 + ns], refs[30 + ns:30 + 2 * ns], *refs[50 + 2 * ns:])
            _comm_begin(i, plan, nt - 1)

        @pl.when(i == 0)
        def _():
            dS_scr[...] = jnp.zeros((NH * DH, DH), F32)
            yext[pl.ds(tm, HALO), :] = jnp.zeros((HALO, PW), F32)
            dwo_acc[...] = jnp.zeros((D, D), F32)
            dgt_ref[...] = jnp.zeros((8, D), F32)
            dpw_ref[...] = jnp.zeros((NG, 128, 128), F32)
            dps_ref[...] = jnp.zeros((8, PW), F32)
            dgn_ref[...] = jnp.zeros((8, DH), F32)

        dx2 = dx_ref[...]
        dgt_ref[...] += _sum8(mx_ref[...] * dx2)
        dmix = (gt_ref[...] * dx2).astype(BF)
        dcat = _nt(dmix, wo_ref[...])
        dwo_acc[...] += _tn(cat_ref[...], dmix)

        pv = p_ref[...]
        pext[pl.ds(0, HALO), :] = jnp.where(ti == 0, 0.0, ph_ref[...])
        pext[pl.ds(HALO, tm), :] = pv
        inv_cnt = _pool_count(tm, ti)
        pooled = _pool_windows(pext[...], tm, False) * inv_cnt - pv
        dpooled = []
        for gi in range(NG):
            cols = slice(gi * 128, (gi + 1) * 128)
            pgb = pooled[:, cols].astype(BF)
            pm = _nn(pgb, pw_ref[gi])
            dpo = dcat[:, GW + gi * 128:GW + (gi + 1) * 128]
            dps_ref[:, cols] += _sum8(dpo * pm)
            dpm = (dpo * ps_ref[:, cols]).astype(BF)
            dpooled.append(_nt(dpm, pw_ref[gi]))
            dpw_ref[gi] += _tn(pgb, dpm)
        dpooled = jnp.concatenate(dpooled, axis=1)
        y = dpooled * inv_cnt
        yext[pl.ds(0, tm), :] = y
        dp_ref[...] = _pool_windows(yext[...], tm, True) - dpooled
        yext[pl.ds(tm, HALO), :] = y[0:HALO]

        gnm = gnm_ref[...]
        dgn = jnp.zeros((8, DH), F32)
        for hd in range(NH):
            cols = slice(hd * DH, (hd + 1) * DH)
            oh = o_ref[:, cols]
            zh = z_ref[:, cols]
            r = lax.rsqrt(jnp.mean(oh * oh, axis=-1, keepdims=True) + EPS)
            n = oh * r
            sg = _sigmoid(zh)
            zs = zh * sg
            dgo = dcat[:, cols]
            dgn = dgn + _sum8(dgo * zs * n)
            dn = dgo * zs * gnm
            do_buf[:, cols] = r * (dn - n * jnp.mean(dn * n, axis=-1, keepdims=True))
            dz_ref[:, cols] = dgo * n * gnm * (sg * (1.0 + zh * (1.0 - sg)))
        dgn_ref[...] += dgn

        cn = _pair_consts()
        lane_c = lax.broadcasted_iota(jnp.int32, (CH, 128), 1)

        def intra(g, carry):
            idx = [(g * cg + dc, pr) for dc in range(cg) for pr in range(NH // 2)]
            res = _pair_intra([_chunk_item(qn_ref, kn_ref, v_ref, bg_ref, gc_ref, c, pr) for c, pr in idx], cn,
                              tms=[tmall_ref[c * (NH // 2) + pr].astype(F32) for c, pr in idx])
            for (c, pr), d in zip(idx, res):
                pi = c * (NH // 2) + pr
                gam_s[pi], p_s[pi], qk_s[pi], tm_s[pi], u_s[pi] = d["gam"], d["P"], d["QK"], d["Tm"], d["u"]
                w_s[pi], qe_s[pi], kf_s[pi], q_s[pi] = d["w_bd"], d["qE_bd"], d["kF_bd"], d["Q"].astype(BF)
            return carry

        lax.fori_loop(0, nc // cg, intra, 0)

        def scan(cc, carry):
            c = nc - 1 - cc
            r0 = pl.multiple_of(c * CH, CH)
            rows = pl.ds(r0, CH)
            gl_all = gc_ref[pl.ds(r0 + CH - 1, 1), :]
            for pr in range(NH // 2):
                ha, hb = 2 * pr, 2 * pr + 1
                pi = c * (NH // 2) + pr
                srows = pl.ds(pr * 2 * DH, 2 * DH)
                S = sall_ref[c, ha:hb + 1].reshape(2 * DH, DH)
                egl, egl_st = _egl_rows(_stack_cols(gl_all, NH + ha, NH + hb, CH))
                it = dict(u=u_s[pi], w_bd=w_s[pi], qE_bd=qe_s[pi], kF_bd=kf_s[pi], Q=q_s[pi])
                g, dS_new = _pair_scan_bwd(it, S, dS_scr[srows, :], _stack_heads(do_buf, rows, pr), egl, egl_st, cn)
                dvn_s[pi], dqq_s[pi], dqe_s[pi], dkf_s[pi], dw_s[pi] = g["dvn"], g["dQ"], g["dqE"], g["dkF"], g["dw"]
                degl_s[pi] = g["degl"]
                dS_scr[srows, :] = dS_new
            return carry

        lax.fori_loop(0, nc, scan, 0)

        def intra_bwd(g, carry):
            idx = [(g * cg + dc, pr) for dc in range(cg) for pr in range(NH // 2)]
            items = []
            for c, pr in idx:
                pi = c * (NH // 2) + pr
                q, k, vv, beta, gcv, gl = _chunk_item(qn_ref, kn_ref, v_ref, bg_ref, gc_ref, c, pr)
                items.append(dict(q=q, k=k, v=vv, beta=beta, gcv=gcv, gl=gl, gam=gam_s[pi], P=p_s[pi], QK=qk_s[pi],
                                  Tm=tm_s[pi], dvn=dvn_s[pi], dQ=dqq_s[pi], dqE=dqe_s[pi], dkF=dkf_s[pi], dw=dw_s[pi],
                                  degl=degl_s[pi]))
            res = _pair_intra_bwd(items, cn)
            for dc in range(cg):
                c = g * cg + dc
                rows = pl.ds(pl.multiple_of(c * CH, CH), CH)
                dbg = jnp.zeros((CH, 128), F32)
                for pr in range(NH // 2):
                    dq, dk, dv, dbeta, dgc = res[dc * (NH // 2) + pr]
                    for hd, half in ((2 * pr, slice(0, CH)), (2 * pr + 1, slice(CH, R2))):
                        cols = slice(hd * DH, (hd + 1) * DH)
                        dq_ref[rows, cols] = dq[half]
                        dk_ref[rows, cols] = dk[half]
                        dv_ref[rows, cols] = dv[half]
                        dbg = dbg + jnp.where(lane_c == hd, dbeta[half], 0.0) + jnp.where(lane_c == NH + hd, dgc[half], 0.0)
                dbg_ref[rows, :] = dbg
            return carry

        lax.fori_loop(0, nc // cg, intra_bwd, 0)

        lane = lax.broadcasted_iota(jnp.int32, (tm, 128), 1)
        row = lax.broadcasted_iota(jnp.int32, (tm, 128), 0) % CH
        dbg_all = dbg_ref[...]
        dg = _seg_cumsum(jnp.where(lane >= NH, dbg_all, 0.0), row, reverse=True)
        dbg_ref[...] = jnp.where(lane < NH, dbg_all, dg)

        @pl.when(i == nt - 1)
        def _():
            dwo_ref[...] = dwo_acc[...].astype(BF)

        if ns:
            _comm_end(i, plan, nt - 1)

    rev = lambda w: pl.BlockSpec((tm, w), lambda i: (nt - 1 - i, 0))
    halo = pl.BlockSpec((HALO, PW), lambda i: (jnp.maximum((nt - 1 - i) * (tm // HALO) - 1, 0), 0))
    shp = lambda w: jax.ShapeDtypeStruct((T, w), F32)
    fix = lambda *s: pl.BlockSpec(s, lambda i: (0,) * len(s))
    return pl.pallas_call(
        body, grid=(nt,), name="mix_core_bwd",
        in_specs=[rev(D), _const_spec((1, D)), rev(D), rev(D), rev(GW),
                  pl.BlockSpec((nc, NH, DH, DH), lambda i: (nt - 1 - i, 0, 0, 0)),
                  rev(GW), rev(GW), rev(GW), rev(GW), rev(PW), halo, rev(128), rev(128),
                  _const_spec((1, DH)), _const_spec((NG, 128, 128)), _const_spec((1, PW)), _const_spec((D, D)),
                  pl.BlockSpec((npb, R2, R2), lambda i: (nt - 1 - i, 0, 0))]
        + [ANY] * ns,
        out_specs=[rev(GW), rev(GW), rev(GW), rev(GW), rev(PW), rev(128),
                   fix(8, D), fix(D, D), fix(NG, 128, 128), fix(8, PW), fix(8, DH)] + [ANY] * ns,
        out_shape=[shp(GW), shp(GW), shp(GW), shp(GW), shp(PW), shp(128),
                   jax.ShapeDtypeStruct((8, D), F32), jax.ShapeDtypeStruct((D, D), BF),
                   jax.ShapeDtypeStruct((NG, 128, 128), F32), jax.ShapeDtypeStruct((8, PW), F32),
                   jax.ShapeDtypeStruct((8, DH), F32)] + [jax.ShapeDtypeStruct(t.shape, t.dtype) for t in scatter],
        scratch_shapes=[pltpu.VMEM((NH * DH, DH), F32), pltpu.VMEM((tm + HALO, PW), F32),
                        pltpu.VMEM((tm + HALO, PW), F32), pltpu.VMEM((tm, GW), F32), pltpu.VMEM((D, D), F32)]
        + [pltpu.VMEM((npb, R2, R2), F32)] * 5 + [pltpu.VMEM((npb, R2, DH), F32)] * 5
        + [pltpu.VMEM((npb, R2, 2 * DH), BF)] * 3 + [pltpu.VMEM((npb, R2, R2), BF), pltpu.VMEM((npb, R2, 1), F32)]
        + (_scatter_sems(ns) if ns else []),
        compiler_params=_cparams(),
    )(dxo, gate, mixed, cat, o, sall, qn, kn, v, z, p, p, bg, gc, gnorm, pool_w, pool_scale, w_out, tm_all, *scatter)


def _mix_proj_bwd(dxo, dqn, dkn, dv, dz, dp, dbg, xq, ba, x, shift, scale, gain, winT, conv_w, alog, dtb):
    T = x.shape[0]
    tm = min(256, T)
    nt = T // tm
    W3 = 3 * GW

    def body(dxo_ref, dqn_ref, dkn_ref, dv_ref, dz_ref, dp_ref, dbg_ref, xq_ref, xh_ref, ba_ref, x_ref, sh_ref, sc_ref,
             gn_ref, w_ref, cw_ref, al_ref, dt_ref,
             dx_ref, dw_ref, dcw_ref, dal_ref, ddt_ref, dsh_ref, dsc_ref, dgn_ref,
             ext, dcext, dproj, dw_acc):
        i = pl.program_id(0)
        ti = nt - 1 - i

        @pl.when(i == 0)
        def _():
            dcext[pl.ds(tm, 8), :] = jnp.zeros((8, W3), F32)
            dw_acc[...] = jnp.zeros((DINP, D), F32)
            dcw_ref[...] = jnp.zeros((4, 8, W3), F32)
            dal_ref[...] = jnp.zeros((8, 128), F32)
            ddt_ref[...] = jnp.zeros((8, 128), F32)
            dsh_ref[...] = jnp.zeros((8, D), F32)
            dsc_ref[...] = jnp.zeros((8, D), F32)
            dgn_ref[...] = jnp.zeros((8, D), F32)

        ext[pl.ds(0, 8), :] = jnp.where(ti == 0, 0.0, xh_ref[...])
        ext[pl.ds(8, tm), :] = xq_ref[...]
        c, sg = _conv_silu(ext, cw_ref, tm)
        qt = c * sg
        dsilu = sg * (1.0 + c * (1.0 - sg))
        for hd in range(NH):
            for part, dref, mult in ((0, dqn_ref, DH ** -0.5), (1, dkn_ref, 1.0)):
                cols = slice(part * GW + hd * DH, part * GW + (hd + 1) * DH)
                xh = qt[:, cols]
                rr = lax.rsqrt(jnp.sum(xh * xh, axis=-1, keepdims=True) + EPS)
                unit = xh * rr
                du = dref[:, hd * DH:(hd + 1) * DH] * mult
                dxh = rr * (du - unit * jnp.sum(du * unit, axis=-1, keepdims=True))
                dcext[pl.ds(0, tm), cols] = dxh * dsilu[:, cols]
        dcext[pl.ds(0, tm), 2 * GW:W3] = dv_ref[...] * dsilu[:, 2 * GW:W3]
        dc = dcext[pl.ds(0, tm), :]
        dxq = jnp.zeros((tm, W3), F32)
        for j in range(4):
            dcw_ref[j] += _sum8(dc * ext[pl.ds(5 + j, tm), :])
            dxq = dxq + cw_ref[j:j + 1, :] * dcext[pl.ds(3 - j, tm), :]
        dcext[pl.ds(tm, 8), :] = dc[0:8]
        lane = lax.broadcasted_iota(jnp.int32, (tm, 128), 1)
        bav = ba_ref[...]
        beta, g, sarg = _gates(bav, al_ref[...], dt_ref[...], lane)
        dbg_v = dbg_ref[...]
        is_g = (lane >= NH) & (lane < 2 * NH)
        dbraw = jnp.where(lane < NH, dbg_v * beta * (1.0 - beta), 0.0)
        daraw = jnp.where(is_g, dbg_v * (-jnp.exp(al_ref[...])) * sarg, 0.0)
        dal_ref[...] += _sum8(jnp.where(is_g, dbg_v * g, 0.0))
        ddt_ref[...] += _sum8(daraw)
        dproj[:, 0:W3] = dxq.astype(BF)
        dproj[:, W3:W3 + GW] = dz_ref[...].astype(BF)
        dproj[:, W3 + GW:W3 + GW + PW] = dp_ref[...].astype(BF)
        dproj[:, W3 + GW + PW:DINP] = (dbraw + daraw).astype(BF)
        gain_v, scale_v = gn_ref[...], sc_ref[...]
        n, r, y, h = _norm_mod_fwd(x_ref[...], gain_v, sh_ref[...], scale_v)
        dpj = dproj[...]
        dh = _nn(dpj, w_ref[...])
        dw_acc[...] += _tn(dpj, h.astype(BF))
        dxn, dsh, dsc, dgn = _norm_mod_bwd(dh, n, r, y, gain_v, scale_v)
        dx_ref[...] = dxo_ref[...] + dxn
        dsh_ref[...] += dsh
        dsc_ref[...] += dsc
        dgn_ref[...] += dgn

        @pl.when(i == nt - 1)
        def _():
            dw_ref[...] = dw_acc[...].astype(BF)

    rev = lambda w: pl.BlockSpec((tm, w), lambda i: (nt - 1 - i, 0))
    halo = pl.BlockSpec((8, W3), lambda i: (jnp.maximum((nt - 1 - i) * (tm // 8) - 1, 0), 0))
    fix = lambda *s: pl.BlockSpec(s, lambda i: (0,) * len(s))
    vec = _const_spec((1, D))
    return pl.pallas_call(
        body, grid=(nt,), name="mix_proj_bwd",
        in_specs=[rev(D), rev(GW), rev(GW), rev(GW), rev(GW), rev(PW), rev(128), rev(W3), halo, rev(128), rev(D),
                  vec, vec, vec, _const_spec((DINP, D)), _const_spec((4, W3)), _const_spec((1, 128)),
                  _const_spec((1, 128))],
        out_specs=[rev(D), fix(DINP, D), fix(4, 8, W3), fix(8, 128), fix(8, 128), fix(8, D), fix(8, D), fix(8, D)],
        out_shape=[jax.ShapeDtypeStruct((T, D), F32), jax.ShapeDtypeStruct((DINP, D), BF),
                   jax.ShapeDtypeStruct((4, 8, W3), F32), jax.ShapeDtypeStruct((8, 128), F32),
                   jax.ShapeDtypeStruct((8, 128), F32), jax.ShapeDtypeStruct((8, D), F32),
                   jax.ShapeDtypeStruct((8, D), F32), jax.ShapeDtypeStruct((8, D), F32)],
        scratch_shapes=[pltpu.VMEM((tm + 8, W3), F32), pltpu.VMEM((tm + 8, W3), F32), pltpu.VMEM((tm, DINP), BF),
                        pltpu.VMEM((DINP, D), F32)],
        compiler_params=_cparams(),
    )(dxo, dqn, dkn, dv, dz, dp, dbg, xq, xq, ba, x, shift, scale, gain, winT, conv_w, alog, dtb)


MESH = pl.DeviceIdType.MESH
CHIP_RELS = ((1, 0), (0, 1), (1, 1))
DEV_RELS = tuple((dx, dy, dc) for dx in (0, 1) for dy in (0, 1) for dc in (0, 1) if (dx, dy, dc) != (0, 0, 0))
NMOD = 9
ADA_SH = NMOD * D // 4
WIN_SH = DIN // 4
WIN_PAD = 672
MSG_ROWS = 16
ANY = pl.BlockSpec(memory_space=pl.ANY)
VM = pl.BlockSpec(memory_space=pltpu.VMEM)


def _place():
    x, y, c = lax.axis_index("x"), lax.axis_index("y"), lax.axis_index("c")
    return x, y, c


class _SplitGather:
    N_SEMS = (3, 3, 3, 3, 1)

    def __init__(self, src, dst, ici_s, ici_r, d2d_s, d2d_r, lsem):
        self.src, self.dst = src, dst
        self.sems = (ici_s, ici_r, d2d_s, d2d_r)
        self.x, self.y, self.c = _place()
        self.myj = 2 * self.x + self.y
        half = src.shape[0] // 2
        self.mine = pl.ds(pl.multiple_of(self.c * half, 16), half)
        self.other = pl.ds(pl.multiple_of((1 - self.c) * half, 16), half)
        self.local = pltpu.make_async_copy(src, dst.at[self.myj], lsem.at[0])

    def _ici(self, k, slot):
        dx, dy = CHIP_RELS[k]
        return pltpu.make_async_remote_copy(
            src_ref=self.src.at[self.mine], dst_ref=self.dst.at[slot, self.mine], send_sem=self.sems[0].at[k],
            recv_sem=self.sems[1].at[k], device_id=(self.x ^ dx, self.y ^ dy, self.c), device_id_type=MESH)

    def _d2d(self, k, rows):
        dx, dy = CHIP_RELS[k]
        blk = self.dst.at[2 * (self.x ^ dx) + (self.y ^ dy), rows]
        return pltpu.make_async_remote_copy(
            src_ref=blk, dst_ref=blk, send_sem=self.sems[2].at[k], recv_sem=self.sems[3].at[k],
            device_id=(self.x, self.y, 1 - self.c), device_id_type=MESH)

    def start(self):
        self.local.start()
        for k in range(3):
            self._ici(k, self.myj).start()

    def forward(self):
        for k, (dx, dy) in enumerate(CHIP_RELS):
            self._ici(k, 2 * (self.x ^ dx) + (self.y ^ dy)).wait_recv()
            self._d2d(k, self.mine).start()

    def finish(self):
        for k in range(3):
            self._d2d(k, self.other).wait_recv()
        for k in range(3):
            self._d2d(k, self.mine).wait_send()
            self._ici(k, self.myj).wait_send()
        self.local.wait()


class _Scatter:
    def __init__(self, ins, outs, send, recv, lsem):
        self.ins, self.outs, self.send, self.recv, self.lsem = ins, outs, send, recv, lsem
        self.x, self.y, self.c = _place()
        self.myj = 2 * self.x + self.y

    def _copy(self, a, k, landing):
        dx, dy = CHIP_RELS[k]
        pj = 2 * (self.x ^ dx) + (self.y ^ dy)
        return pltpu.make_async_remote_copy(
            src_ref=self.ins[a].at[pj], dst_ref=self.outs[a].at[pj if landing else self.myj],
            send_sem=self.send.at[a, k], recv_sem=self.recv.at[a, k],
            device_id=(self.x ^ dx, self.y ^ dy, self.c), device_id_type=MESH)

    def _local(self, a):
        return pltpu.make_async_copy(self.ins[a].at[self.myj], self.outs[a].at[self.myj], self.lsem.at[a])

    def start(self):
        for a in range(len(self.ins)):
            self._local(a).start()
            for k in range(3):
                self._copy(a, k, False).start()

    def finish(self):
        for a in range(len(self.ins)):
            for k in range(3):
                self._copy(a, k, True).wait_recv()
            for k in range(3):
                self._copy(a, k, False).wait_send()
            self._local(a).wait()


def _scatter_sems(n):
    return [pltpu.SemaphoreType.DMA((n, 3)), pltpu.SemaphoreType.DMA((n, 3)), pltpu.SemaphoreType.DMA((n,))]


def _gather_sems():
    return [pltpu.SemaphoreType.DMA((k,)) for k in _SplitGather.N_SEMS]


def _comm_begin(i, plan, last):
    @pl.when(i == 0)
    def _():
        plan.start()

    if hasattr(plan, "forward"):
        @pl.when(i == max(last - 3, 0))
        def _():
            plan.forward()


def _comm_end(i, plan, last):
    @pl.when(i == last)
    def _():
        plan.finish()


def _ada_exchange(msg, w_ada, b_ada, wblock):
    def body(msg_ref, w_ref, b_ref, wb_ref, all_ref, mod_ref, wg_ref, modp, send1, recv1, send2, recv2, lsem, *gsems):
        x, y, c = _place()
        me = 4 * x + 2 * y + c
        own = pltpu.make_async_copy(msg_ref, all_ref.at[me], lsem.at[0])
        own.start()

        def gather(k, rel, slot):
            dx, dy, dc = rel
            return pltpu.make_async_remote_copy(
                src_ref=msg_ref, dst_ref=all_ref.at[slot], send_sem=send1.at[k], recv_sem=recv1.at[k],
                device_id=(x ^ dx, y ^ dy, c ^ dc), device_id_type=MESH)

        for k, rel in enumerate(DEV_RELS):
            gather(k, rel, me).start()
        for k, (dx, dy, dc) in enumerate(DEV_RELS):
            gather(k, (dx, dy, dc), 4 * (x ^ dx) + 2 * (y ^ dy) + (c ^ dc)).wait_recv()
        for k, rel in enumerate(DEV_RELS):
            gather(k, rel, me).wait_send()
        own.wait()
        wgather = _SplitGather(wb_ref, wg_ref, *gsems)
        wgather.start()

        cv = all_ref[:, 0:8, :].reshape(64, D)
        act = cv * _sigmoid(cv)
        modp[...] = (_nn(act, w_ref[...], precision=HI) + b_ref[...]).reshape(8, 8, ADA_SH)

        myj = 2 * x + y
        keep = pltpu.make_async_copy(modp.at[me], mod_ref.at[myj], lsem.at[1])
        keep.start()

        def scatter(k, rel):
            dx, dy = rel
            return pltpu.make_async_remote_copy(
                src_ref=modp.at[4 * (x ^ dx) + 2 * (y ^ dy) + c], dst_ref=mod_ref.at[myj],
                send_sem=send2.at[k], recv_sem=recv2.at[k], device_id=(x ^ dx, y ^ dy, c), device_id_type=MESH)

        def landed(k, rel):
            dx, dy = rel
            return pltpu.make_async_remote_copy(
                src_ref=modp.at[me], dst_ref=mod_ref.at[2 * (x ^ dx) + (y ^ dy)],
                send_sem=send2.at[k], recv_sem=recv2.at[k], device_id=(x ^ dx, y ^ dy, c), device_id_type=MESH)

        for k, rel in enumerate(CHIP_RELS):
            scatter(k, rel).start()
        for k, rel in enumerate(CHIP_RELS):
            landed(k, rel).wait_recv()
        for k, rel in enumerate(CHIP_RELS):
            scatter(k, rel).wait_send()
        keep.wait()
        wgather.forward()
        wgather.finish()

    return pl.pallas_call(
        body, name="ada_exchange", in_specs=[VM, VM, VM, ANY], out_specs=[VM, VM, ANY],
        out_shape=[jax.ShapeDtypeStruct((8, MSG_ROWS, D), F32), jax.ShapeDtypeStruct((4, 8, ADA_SH), F32),
                   jax.ShapeDtypeStruct((4,) + wblock.shape, wblock.dtype)],
        scratch_shapes=[pltpu.VMEM((8, 8, ADA_SH), F32), pltpu.SemaphoreType.DMA((7,)), pltpu.SemaphoreType.DMA((7,)),
                        pltpu.SemaphoreType.DMA((3,)), pltpu.SemaphoreType.DMA((3,)), pltpu.SemaphoreType.DMA((2,))]
        + _gather_sems(),
        compiler_params=pltpu.CompilerParams(vmem_limit_bytes=VMEM_LIMIT),
    )(msg, w_ada, b_ada, wblock)


def _chip_exchange(parts, name):
    n = len(parts)

    def body(*refs):
        plan = _Scatter(refs[:n], refs[n:2 * n], *refs[2 * n:])
        plan.start()
        plan.finish()

    return pl.pallas_call(
        body, name=name, in_specs=[ANY] * n, out_specs=[ANY] * n,
        out_shape=[jax.ShapeDtypeStruct(p.shape, p.dtype) for p in parts], scratch_shapes=_scatter_sems(n),
    )(*parts)


class _AllGather:
    def __init__(self, ins, outs, send, recv, lsem):
        self.ins, self.outs, self.send, self.recv, self.lsem = ins, outs, send, recv, lsem
        self.x, self.y, self.c = _place()
        self.me = 4 * self.x + 2 * self.y + self.c

    def _copy(self, a, k, landing):
        dx, dy, dc = DEV_RELS[k]
        peer = 4 * (self.x ^ dx) + 2 * (self.y ^ dy) + (self.c ^ dc)
        return pltpu.make_async_remote_copy(
            src_ref=self.ins[a], dst_ref=self.outs[a].at[peer if landing else self.me],
            send_sem=self.send.at[a, k], recv_sem=self.recv.at[a, k],
            device_id=(self.x ^ dx, self.y ^ dy, self.c ^ dc), device_id_type=MESH)

    def _local(self, a):
        return pltpu.make_async_copy(self.ins[a], self.outs[a].at[self.me], self.lsem.at[a])

    def start(self):
        for a in range(len(self.ins)):
            self._local(a).start()
            for k in range(7):
                self._copy(a, k, False).start()

    def finish(self):
        for a in range(len(self.ins)):
            for k in range(7):
                self._copy(a, k, True).wait_recv()
            for k in range(7):
                self._copy(a, k, False).wait_send()
            self._local(a).wait()


def _allgather_sems(n):
    return [pltpu.SemaphoreType.DMA((n, 7)), pltpu.SemaphoreType.DMA((n, 7)), pltpu.SemaphoreType.DMA((n,))]


class _Plans:
    def __init__(self, plans):
        self.plans = plans

    def start(self):
        for p in self.plans:
            p.start()

    def finish(self):
        for p in self.plans:
            p.finish()


def _pair_exchange(parts, name):
    n = len(parts)

    def body(*refs):
        ins, outs = refs[:n], refs[n:2 * n]
        send, recv = refs[2 * n:]
        x, y, c = _place()
        cps = [pltpu.make_async_remote_copy(
            src_ref=ins[a], dst_ref=outs[a], send_sem=send.at[a], recv_sem=recv.at[a],
            device_id=(x, y, 1 - c), device_id_type=MESH) for a in range(n)]
        for cp in cps:
            cp.start()
        for cp in cps:
            cp.wait_recv()
        for cp in cps:
            cp.wait_send()

    shapes = [jax.ShapeDtypeStruct(p.shape, p.dtype) for p in parts]
    return pl.pallas_call(
        body, name=name, in_specs=[ANY] * n, out_specs=[ANY] * n, out_shape=shapes,
        scratch_shapes=[pltpu.SemaphoreType.DMA((n,)), pltpu.SemaphoreType.DMA((n,))],
    )(*parts)


def _pair_swap_rows(parts, name):
    n = len(parts)

    def body(*refs):
        ins, outs = refs[:n], refs[n:2 * n]
        send, recv = refs[2 * n:]
        x, y, c = _place()
        half = ins[0].shape[1] // 2
        theirs = pl.ds(pl.multiple_of((1 - c) * half, 16), half)
        cps = [pltpu.make_async_remote_copy(
            src_ref=ins[a].at[pl.ds(0, 4), theirs], dst_ref=outs[a], send_sem=send.at[a], recv_sem=recv.at[a],
            device_id=(x, y, 1 - c), device_id_type=MESH) for a in range(n)]
        for cp in cps:
            cp.start()
        for cp in cps:
            cp.wait_recv()
        for cp in cps:
            cp.wait_send()

    shapes = [jax.ShapeDtypeStruct((4, p.shape[1] // 2, p.shape[2]), p.dtype) for p in parts]
    return pl.pallas_call(
        body, name=name, in_specs=[ANY] * n, out_specs=[ANY] * n, out_shape=shapes,
        scratch_shapes=[pltpu.SemaphoreType.DMA((n,)), pltpu.SemaphoreType.DMA((n,))],
    )(*parts)


def _add_blocks(p, q, name):
    n, rows, width = p.shape

    def body(p_ref, q_ref, o_ref):
        o_ref[...] = (p_ref[...].astype(F32) + q_ref[...].astype(F32)).astype(o_ref.dtype)

    blk = pl.BlockSpec((None, rows, width), lambda j: (j, 0, 0))
    return pl.pallas_call(
        body, grid=(n,), name=name, in_specs=[blk, blk], out_specs=blk,
        out_shape=jax.ShapeDtypeStruct(p.shape, p.dtype), compiler_params=_cparams(),
    )(p, q)


def _row_tile(rows, cap):
    best = rows
    for t in range(8, min(cap, rows) + 1, 8):
        if rows % t == 0:
            best = t
    return best if rows % 8 == 0 else rows


def _sum_slots(parts, name):
    n, rows, width = parts.shape
    tr = _row_tile(rows, 352)

    def body(p_ref, o_ref):
        acc = p_ref[0].astype(F32)
        for j in range(1, n):
            acc = acc + p_ref[j].astype(F32)
        o_ref[...] = acc

    return pl.pallas_call(
        body, grid=(rows // tr,), name=name,
        in_specs=[pl.BlockSpec((n, tr, width), lambda i: (0, i, 0))],
        out_specs=pl.BlockSpec((tr, width), lambda i: (i, 0)),
        out_shape=jax.ShapeDtypeStruct((rows, width), F32),
        compiler_params=_cparams(),
    )(parts)


def _adamw_math(g, w, m, v):
    m_new = ADAM_B1 * m + (1.0 - ADAM_B1) * g
    v_new = ADAM_B2 * v + (1.0 - ADAM_B2) * (g * g)
    m_hat = m_new / (1.0 - ADAM_B1 ** ADAM_STEP)
    v_hat = v_new / (1.0 - ADAM_B2 ** ADAM_STEP)
    delta = -ADAM_LR * (m_hat / (jnp.sqrt(v_hat) + ADAM_EPS) + ADAM_WD * w)
    return delta, m_new, v_new


def _adamw(grads, w, m, v, name):
    rows, width = w.shape
    tr = _row_tile(rows, 256 if width <= 1024 else 128)
    ng = len(grads)

    def body(*refs):
        g = refs[0][...]
        for r in refs[1:ng]:
            g = g + r[...]
        w_ref, m_ref, v_ref, g_out, d_out, m_out, v_out = refs[ng:]
        delta, m_new, v_new = _adamw_math(g, w_ref[...], m_ref[...], v_ref[...])
        g_out[...] = g
        d_out[...] = delta
        m_out[...] = m_new
        v_out[...] = v_new

    if rows % 8 == 0 or width % 512:
        blk, steps = pl.BlockSpec((tr, width), lambda i: (i, 0)), rows // tr
    else:
        blk, steps = pl.BlockSpec((rows, 256), lambda i: (0, i)), width // 256
    return pl.pallas_call(
        body, grid=(steps,), name=name,
        in_specs=[blk] * (ng + 3), out_specs=[blk] * 4,
        out_shape=[jax.ShapeDtypeStruct((rows, width), F32)] * 4,
        compiler_params=_cparams(),
    )(*grads, w, m, v)


def _adamw_ada(msgs, dmods, w, m, v):
    rows, width = w.shape
    tr = 128

    def body(c_ref, dm_ref, w_ref, m_ref, v_ref, g_out, d_out, m_out, v_out):
        cv = jnp.concatenate([c_ref[d, 0:1, :] for d in range(8)], axis=0)
        act = cv * _sigmoid(cv)
        g = _tn(act, dm_ref[...], precision=HI)
        delta, m_new, v_new = _adamw_math(g, w_ref[...], m_ref[...], v_ref[...])
        g_out[...] = g
        d_out[...] = delta
        m_out[...] = m_new
        v_out[...] = v_new

    blk = pl.BlockSpec((tr, width), lambda i: (i, 0))
    return pl.pallas_call(
        body, grid=(rows // tr,), name="adamw_w_ada",
        in_specs=[pl.BlockSpec((8, MSG_ROWS, tr), lambda i: (0, 0, i)), pl.BlockSpec((8, width), lambda i: (0, 0)),
                  blk, blk, blk],
        out_specs=[blk] * 4, out_shape=[jax.ShapeDtypeStruct((rows, width), F32)] * 4,
        compiler_params=_cparams(),
    )(msgs, dmods, w, m, v)


def _adamw_small(parts, w, m, v, name):
    n, rows, width = parts.shape

    def body(p_ref, w_ref, m_ref, v_ref, g_out, d_out, m_out, v_out):
        g = p_ref[0]
        for j in range(1, n):
            g = g + p_ref[j]
        delta, m_new, v_new = _adamw_math(g, w_ref[...], m_ref[...], v_ref[...])
        g_out[...] = g
        d_out[...] = delta
        m_out[...] = m_new
        v_out[...] = v_new

    return pl.pallas_call(
        body, name=name, in_specs=[VM] * 4, out_specs=[VM] * 4,
        out_shape=[jax.ShapeDtypeStruct((rows, width), F32)] * 4,
        compiler_params=pltpu.CompilerParams(vmem_limit_bytes=VMEM_LIMIT),
    )(parts, w, m, v)


SMALL_ROWS = 24


def _pad_row(vec, width=D):
    vec = vec.reshape(1, -1)
    return jnp.pad(vec, ((0, 0), (0, width - vec.shape[1])))


def _lanes_4_7(vec4):
    return jnp.zeros((1, 128), F32).at[0, NH:2 * NH].set(vec4.reshape(NH))


def kernel(x, c, w_ada, b_ada, norm_ffn1, ffn1_gate, ffn1_up, ffn1_down, norm_mix, w_in, conv_w, a_log, dt_bias, gdn_norm, pool_w, pool_scale, w_out, norm_ffn2, ffn2_gate, ffn2_up, ffn2_down, final_norm, loss_target, m_w_ada, m_b_ada, m_norm_ffn1, m_ffn1_gate, m_ffn1_up, m_ffn1_down, m_norm_mix, m_w_in, m_conv_w, m_a_log, m_dt_bias, m_gdn_norm, m_pool_w, m_pool_scale, m_w_out, m_norm_ffn2, m_ffn2_gate, m_ffn2_up, m_ffn2_down, m_final_norm, v_w_ada, v_b_ada, v_norm_ffn1, v_ffn1_gate, v_ffn1_up, v_ffn1_down, v_norm_mix, v_w_in, v_conv_w, v_a_log, v_dt_bias, v_gdn_norm, v_pool_w, v_pool_scale, v_w_out, v_norm_ffn2, v_ffn2_gate, v_ffn2_up, v_ffn2_down, v_final_norm):
    xs = x[0]
    tgt = loss_target[0]
    chip = 2 * lax.axis_index("x") + lax.axis_index("y")
    me = 2 * chip + lax.axis_index("c")

    fsh = FF // 4
    block_a = jnp.concatenate([ffn1_gate[0].T, ffn1_up[0].T, ffn1_down[0]], axis=0).astype(BF)
    block_b = jnp.concatenate([ffn2_gate[0].T, ffn2_up[0].T, ffn2_down[0], w_out[0],
                               jnp.pad(w_in[0].T, ((0, WIN_PAD - WIN_SH), (0, 0)))], axis=0).astype(BF)

    msg = jnp.concatenate([jnp.broadcast_to(c, (8, D)), jnp.pad(conv_w[0], ((0, 0), (0, D - 3 * GW // 4))),
                           jnp.zeros((MSG_ROWS - 12, D), F32)], axis=0)
    b_sh = lax.dynamic_slice(b_ada, (0, chip * ADA_SH), (1, ADA_SH))
    msgs, mod4, gath_a = _ada_exchange(msg, w_ada[0], b_sh, block_a)
    mod = mod4[:, 0, :].reshape(NMOD, D)
    mrow = [mod[i:i + 1] for i in range(NMOD)]
    conv_full = jnp.concatenate([msgs[2 * j, 8:12, :3 * GW // 4] for j in range(4)], axis=1)
    alog, dtb = _lanes_4_7(a_log), _lanes_4_7(dt_bias)
    gnm = gdn_norm.reshape(1, DH)
    pwb = pool_w[0].astype(BF)
    psc = pool_scale.reshape(1, PW)
    fin = final_norm.reshape(1, D)

    x1, f1, a1, b1, s1, gath_b = _ffn_fwd(xs, mrow[0], mrow[1], mrow[2], norm_ffn1, gath_a, 0, "ffn1_fwd", block_b)
    wo = gath_b[:, 3 * fsh:3 * fsh + D // 4, :].reshape(D, D)
    win_nat = gath_b[:, 3 * fsh + D // 4:3 * fsh + D // 4 + WIN_SH, :].reshape(DIN, D)
    winT = jnp.concatenate([win_nat[:4 * GW], win_nat[4 * GW + 2 * NH:], win_nat[4 * GW:4 * GW + 2 * NH],
                            jnp.zeros((128 - 2 * NH, D), BF)], axis=0)
    xq, ba, qn, kn, vv, z, pp, bg, gc = _mix_proj(x1, mrow[3], mrow[4], norm_mix, winT, conv_full, alog, dtb)
    x2, mixed, cat, o, sall, tmall = _mix_core(x1, mrow[5], qn, kn, vv, z, pp, bg, gc, gnm, pwb, psc, wo)
    lpart, dx3, dfin, f2, a2, b2, s2 = _ffn_fwd(x2, mrow[6], mrow[7], mrow[8], norm_ffn2, gath_b, 0, "ffn2_fwd",
                                                loss=(tgt, fin))
    loss = lax.psum(jnp.sum(lpart), ("x", "y", "c"))

    slots = lambda t: t.reshape(4, t.shape[0] // 4, D)
    dx2, da2, db2, h2, df2, dsh3, dsc3, dgt3, dn3 = _ffn_dgrad(
        dx3, x2, f2, a2, b2, mrow[6], mrow[7], mrow[8], norm_ffn2, gath_b, 0, "ffn2_dgrad")
    gg2, gu2, gd2 = _ffn_wgrad(da2, db2, s2, h2, df2, "ffn2_wgrad")
    dqn, dkn, dvv, dz, dpp, dbg, dgt2, dwo, dpw, dps, dgnm, *landed2 = _mix_core_bwd(
        dx2, mrow[5], mixed, cat, o, sall, qn, kn, vv, z, pp, bg, gc, gnm, pwb, psc, wo, tmall,
        scatter=[slots(gg2), slots(gu2), slots(gd2)])
    dx1, dwin, dcw, dal, ddt, dsh2, dsc2, dn2 = _mix_proj_bwd(
        dx2, dqn, dkn, dvv, dz, dpp, dbg, xq, ba, x1, mrow[3], mrow[4], norm_mix, winT, conv_full, alog, dtb)
    dwin_nat = jnp.concatenate([dwin[:4 * GW], dwin[4 * GW + PW:4 * GW + PW + 2 * NH], dwin[4 * GW:4 * GW + PW]], axis=0)
    dwin_sl = jnp.pad(dwin_nat.reshape(4, WIN_SH, D), ((0, 0), (0, WIN_PAD - WIN_SH), (0, 0)))
    dx0, da1, db1, h1, df1, dsh1, dsc1, dgt1, dn1 = _ffn_dgrad(
        dx1, xs, f1, a1, b1, mrow[0], mrow[1], mrow[2], norm_ffn1, gath_a, 0, "ffn1_dgrad")
    red = lambda t: jnp.sum(t, axis=0, keepdims=True)
    small = jnp.concatenate(
        [red(dn1), red(dn2), red(dn3), red(dfin),
         red(dsh1), red(dsc1), red(dgt1), red(dsh2), red(dsc2), red(dgt2), red(dsh3), red(dsc3), red(dgt3),
         _pad_row(red(dps)), _pad_row(red(dgnm)), _pad_row(red(dal)), _pad_row(red(ddt)),
         jnp.sum(dcw, axis=1).reshape(6, D), jnp.zeros((1, D), F32)], axis=0)
    gg1, gu1, gd1, land_wo, land_win, small_all, dpw_all = _ffn_wgrad(
        da1, db1, s1, h1, df1, "ffn1_wgrad", scatter=[slots(dwo), dwin_sl],
        allgather=[small, dpw.reshape(NG * 128, 128)])

    ffn1_blocks = [slots(gg1), slots(gu1), slots(gd1)]
    sibling_share = _pair_swap_rows(ffn1_blocks, "grad_pair_rows")
    own_rows = lax.axis_index("c") * (fsh // 2)
    pair_sums = [_add_blocks(lax.dynamic_slice(g, (0, own_rows, 0), (4, fsh // 2, D)), sh_, "pair_add_" + nm)
                 for g, sh_, nm in zip(ffn1_blocks, sibling_share, ("g1", "u1", "d1"))]
    landed1 = _chip_exchange(pair_sums, "grad_scatter")
    half_sums = [_sum_slots(t, "sum_" + nm) for t, nm in zip(landed1, ("g1", "u1", "d1"))]
    other_sums = _pair_exchange(half_sums, "grad_pair_ffn1")
    other_rows = (1 - lax.axis_index("c")) * (fsh // 2)
    full1 = [lax.dynamic_update_slice(lax.dynamic_update_slice(jnp.zeros((fsh, D), F32), mine_, (own_rows, 0)),
                                      theirs_, (other_rows, 0)) for mine_, theirs_ in zip(half_sums, other_sums)]

    landed = list(landed2) + [land_wo, land_win]
    psum = [_sum_slots(t, "sum_" + nm) for t, nm in zip(landed, ("g2", "u2", "d2", "wo", "win"))]
    qsum = _pair_exchange(psum, "grad_pair")

    def adamw_t(grads, w, m, v, name, rows):
        res = _adamw([g[:rows] for g in grads], w[0].T, m[0].T, v[0].T, name)
        return [t.T for t in res]

    upd = {}
    upd["ffn1_gate"] = adamw_t([full1[0]], ffn1_gate, m_ffn1_gate, v_ffn1_gate, "adamw_g1", fsh)
    upd["ffn1_up"] = adamw_t([full1[1]], ffn1_up, m_ffn1_up, v_ffn1_up, "adamw_u1", fsh)
    upd["ffn1_down"] = _adamw([full1[2]], ffn1_down[0], m_ffn1_down[0], v_ffn1_down[0], "adamw_d1")
    upd["ffn2_gate"] = adamw_t([psum[0], qsum[0]], ffn2_gate, m_ffn2_gate, v_ffn2_gate, "adamw_g2", fsh)
    upd["ffn2_up"] = adamw_t([psum[1], qsum[1]], ffn2_up, m_ffn2_up, v_ffn2_up, "adamw_u2", fsh)
    upd["ffn2_down"] = _adamw([psum[2], qsum[2]], ffn2_down[0], m_ffn2_down[0], v_ffn2_down[0], "adamw_d2")
    upd["w_out"] = _adamw([psum[3], qsum[3]], w_out[0], m_w_out[0], v_w_out[0], "adamw_wo")
    upd["w_in"] = adamw_t([psum[4], qsum[4]], w_in, m_w_in, v_w_in, "adamw_win", WIN_SH)
    dmods = lax.dynamic_slice(small_all[:, 4:4 + NMOD, :].reshape(8, NMOD * D), (0, chip * ADA_SH), (8, ADA_SH))
    upd["w_ada"] = _adamw_ada(msgs, dmods, w_ada[0], m_w_ada[0], v_w_ada[0])

    def pack_small(nf1, nmx, nf2, fn, bada, psc_, gn_, al_, dt_):
        return jnp.concatenate(
            [nf1.reshape(1, D), nmx.reshape(1, D), nf2.reshape(1, D), fn.reshape(1, D), bada.reshape(NMOD, D),
             _pad_row(psc_), _pad_row(gn_), _pad_row(_lanes_4_7(al_)), _pad_row(_lanes_4_7(dt_)),
             jnp.zeros((7, D), F32)], axis=0)

    ws = pack_small(norm_ffn1, norm_mix, norm_ffn2, final_norm, b_ada, pool_scale, gdn_norm, a_log, dt_bias)
    ms = pack_small(m_norm_ffn1, m_norm_mix, m_norm_ffn2, m_final_norm, m_b_ada, m_pool_scale, m_gdn_norm, m_a_log, m_dt_bias)
    vs = pack_small(v_norm_ffn1, v_norm_mix, v_norm_ffn2, v_final_norm, v_b_ada, v_pool_scale, v_gdn_norm, v_a_log, v_dt_bias)
    sm = _adamw_small(small_all, ws, ms, vs, "adamw_small")
    pw2 = lambda t: t.reshape(NG * 128, 128)
    upd_pw = _adamw_small(dpw_all, pw2(pool_w), pw2(m_pool_w), pw2(v_pool_w), "adamw_pool_w")
    csh = 3 * GW // 4
    gconv = lax.dynamic_slice(sm[0][17:23].reshape(4, 3 * GW), (0, chip * csh), (4, csh))
    upd["conv_w"] = _adamw([gconv], conv_w[0], m_conv_w[0], v_conv_w[0], "adamw_conv")

    def small_out(k):
        t = sm[k]
        return {
            "norm_ffn1": t[0:1], "norm_mix": t[1:2], "norm_ffn2": t[2:3], "final_norm": t[3],
            "b_ada": t[4:4 + NMOD].reshape(1, NMOD * D), "pool_scale": t[13:14, :PW], "gdn_norm": t[14:15, :DH],
            "a_log": t[15:16, NH:2 * NH], "dt_bias": t[16:17, NH:2 * NH],
        }

    order = ["w_ada", "b_ada", "norm_ffn1", "ffn1_gate", "ffn1_up", "ffn1_down", "norm_mix", "w_in", "conv_w", "a_log",
             "dt_bias", "gdn_norm", "pool_w", "pool_scale", "w_out", "norm_ffn2", "ffn2_gate", "ffn2_up", "ffn2_down",
             "final_norm"]
    outs = [loss, dx0[None]]
    for k in range(4):
        smk = small_out(k)
        for nm in order:
            if nm in upd:
                outs.append(upd[nm][k][None])
            elif nm == "pool_w":
                outs.append(upd_pw[k].reshape(1, NG, 128, 128))
            else:
                outs.append(smk[nm])
    return tuple(outs)
```

```python
import functools

import jax
import jax.numpy as jnp
from jax import lax
from jax.experimental import pallas as pl
from jax.experimental.pallas import tpu as pltpu

F32 = jnp.float32
BF = jnp.bfloat16

D = 1024
FF = 2816
FH = FF // 2
NH = 4
DH = 128
GW = NH * DH
CH = 64
PW = 512
NG = 4
POOL_WINDOWS = (2, 4, 8, 16)
HALO = 16
DIN = 4 * GW + 2 * NH + PW
DINP = 3 * GW + GW + PW + 128
EPS = 1e-6
ADAM_LR, ADAM_B1, ADAM_B2, ADAM_EPS, ADAM_WD, ADAM_STEP = 0.001, 0.9, 0.999, 1e-08, 0.01, 10

VMEM_LIMIT = 60 * 1024 * 1024

NT_DIMS = (((1,), (1,)), ((), ()))
TN_DIMS = (((0,), (0,)), ((), ()))
HI = lax.Precision.HIGHEST


def _nt(a, b, **kw):
    return lax.dot_general(a, b, NT_DIMS, preferred_element_type=F32, **kw)


def _tn(a, b, **kw):
    return lax.dot_general(a, b, TN_DIMS, preferred_element_type=F32, **kw)


def _nn(a, b, **kw):
    return jnp.dot(a, b, preferred_element_type=F32, **kw)


def _cparams(sem=("arbitrary",), **kw):
    return pltpu.CompilerParams(dimension_semantics=sem, vmem_limit_bytes=VMEM_LIMIT, **kw)


def _const_spec(shape):
    nd = len(shape)
    return pl.BlockSpec(shape, lambda *_: (0,) * nd, pipeline_mode=pl.Buffered(1))


def _row_spec(tm, width):
    return pl.BlockSpec((tm, width), lambda i: (i, 0))


def _sum8(v):
    return jnp.sum(v.reshape(v.shape[0] // 8, 8, v.shape[1]), axis=0)


def _sigmoid(v):
    return 0.5 * jnp.tanh(0.5 * v) + 0.5


def _tile(T, cap=512):
    return min(cap, T)


def _norm_mod_fwd(xv, gain, shift, scale):
    r = lax.rsqrt(jnp.mean(xv * xv, axis=-1, keepdims=True) + EPS)
    n = xv * r
    y = n * gain
    return n, r, y, y * (1.0 + scale) + shift


def _norm_mod_bwd(dh, n, r, y, gain, scale):
    dy = dh * (1.0 + scale)
    dn = dy * gain
    dx = r * (dn - n * jnp.mean(dn * n, axis=-1, keepdims=True))
    return dx, _sum8(dh), _sum8(dh * y), _sum8(dy * n)


def _ffn_wspecs(k0):
    return [pl.BlockSpec((4, FF // 4, D), lambda i, k=k0 + n: (0, k, 0), pipeline_mode=pl.Buffered(1)) for n in range(3)]


def _half(w_ref, j):
    return w_ref[2 * j:2 * j + 2].reshape(FH, D)


def _ffn_fwd(x, shift, scale, gate, gain, wall, k0, name, gather_block=None, loss=None):
    T = x.shape[0]
    tm = _tile(T)
    nt = T // tm
    comm = gather_block is not None
    assert not (comm and loss)

    def body(*refs):
        x_ref, sh_ref, sc_ref, gt_ref, gn_ref, wg_ref, wu_ref, wd_ref = refs[:8]
        if comm:
            xo_ref, f_ref, sa_ref, ga_ref, s_ref = refs[9:14]
            plan = _SplitGather(refs[8], refs[14], *refs[15:])
            _comm_begin(pl.program_id(0), plan, nt - 1)
        elif loss:
            t_ref, fg_ref, ls_ref, xo_ref, dfg_ref, f_ref, sa_ref, ga_ref, s_ref = refs[8:17]
        else:
            xo_ref, f_ref, sa_ref, ga_ref, s_ref = refs[8:13]
        xv = x_ref[...]
        _, _, _, h = _norm_mod_fwd(xv, gn_ref[...], sh_ref[...], sc_ref[...])
        hb = h.astype(BF)
        facc = jnp.zeros((tm, D), F32)
        for j in range(2):
            cols = slice(j * FH, (j + 1) * FH)
            a = _nt(hb, _half(wg_ref, j))
            b = _nt(hb, _half(wu_ref, j))
            sig = _sigmoid(a)
            sa = a * sig
            sa_ref[:, cols] = sa.astype(BF)
            ga_ref[:, cols] = (b * (sig * (1.0 + a * (1.0 - sig)))).astype(BF)
            s = (sa * b).astype(BF)
            s_ref[:, cols] = s
            facc = facc + _nn(s, _half(wd_ref, j))
        f_ref[...] = facc
        xo = xv + 0.5 * gt_ref[...] * facc
        if loss:
            i = pl.program_id(0)
            lsum, dxo, dfg = _loss_math(xo, t_ref[...], fg_ref[...])
            xo_ref[...] = dxo

            @pl.when(i == 0)
            def _():
                ls_ref[...] = lsum
                dfg_ref[...] = dfg

            @pl.when(i > 0)
            def _():
                ls_ref[...] += lsum
                dfg_ref[...] += dfg
        else:
            xo_ref[...] = xo
        if comm:
            _comm_end(pl.program_id(0), plan, nt - 1)

    vec = _const_spec((1, D))
    tok = jax.ShapeDtypeStruct((T, D), F32)
    extra_in, extra_specs = ([gather_block], [ANY]) if comm else (list(loss), [_row_spec(tm, D), vec]) if loss else ([], [])
    head_specs, head_shapes = [_row_spec(tm, D)], [tok]
    if loss:
        head_specs = [pl.BlockSpec((8, 128), lambda i: (0, 0)), _row_spec(tm, D), pl.BlockSpec((8, D), lambda i: (0, 0))]
        head_shapes = [jax.ShapeDtypeStruct((8, 128), F32), tok, jax.ShapeDtypeStruct((8, D), F32)]
    return pl.pallas_call(
        body, grid=(nt,), name=name,
        in_specs=[_row_spec(tm, D), vec, vec, vec, vec] + _ffn_wspecs(k0) + extra_specs,
        out_specs=head_specs + [_row_spec(tm, D)] + [_row_spec(tm, FF)] * 3 + [ANY] * comm,
        out_shape=head_shapes + [tok] + [jax.ShapeDtypeStruct((T, FF), BF)] * 3
        + ([jax.ShapeDtypeStruct((4,) + gather_block.shape, gather_block.dtype)] if comm else []),
        scratch_shapes=_gather_sems() if comm else [],
        compiler_params=_cparams(),
    )(x, shift, scale, gate, gain, wall, wall, wall, *extra_in)


def _ffn_dgrad(dxo, x, f, sa, ga, shift, scale, gate, gain, wall, k0, name):
    T = x.shape[0]
    tm = _tile(T)
    nt = T // tm
    wspecs = _ffn_wspecs(k0)
    vec = _const_spec((1, D))
    acc = pl.BlockSpec((8, D), lambda i: (0, 0))
    accs = jax.ShapeDtypeStruct((8, D), F32)

    def body_a(dxo_ref, f_ref, sa_ref, ga_ref, gt_ref, wd_ref, da_ref, db_ref, df_ref, dgt_ref):
        i = pl.program_id(0)
        dxo_v = dxo_ref[...]
        dgate = _sum8(0.5 * f_ref[...] * dxo_v)
        dfb = (0.5 * gt_ref[...] * dxo_v).astype(BF)
        df_ref[...] = dfb
        for j in range(2):
            cols = slice(j * FH, (j + 1) * FH)
            ds = _nt(dfb, _half(wd_ref, j))
            da_ref[:, cols] = (ds * ga_ref[:, cols].astype(F32)).astype(BF)
            db_ref[:, cols] = (ds * sa_ref[:, cols].astype(F32)).astype(BF)

        @pl.when(i == 0)
        def _():
            dgt_ref[...] = dgate

        @pl.when(i > 0)
        def _():
            dgt_ref[...] += dgate

    wide = jax.ShapeDtypeStruct((T, FF), BF)
    da, db, df, dgt = pl.pallas_call(
        body_a, grid=(nt,), name=name + "_a",
        in_specs=[_row_spec(tm, D), _row_spec(tm, D), _row_spec(tm, FF), _row_spec(tm, FF), vec, wspecs[2]],
        out_specs=[_row_spec(tm, FF), _row_spec(tm, FF), _row_spec(tm, D), acc],
        out_shape=[wide, wide, jax.ShapeDtypeStruct((T, D), BF), accs],
        compiler_params=_cparams(),
    )(dxo, f, sa, ga, gate, wall)

    def body_b(dxo_ref, x_ref, da_ref, db_ref, sh_ref, sc_ref, gn_ref, wg_ref, wu_ref,
               dx_ref, h_ref, dsh_ref, dsc_ref, dgn_ref):
        i = pl.program_id(0)
        gain_v, scale_v = gn_ref[...], sc_ref[...]
        n, r, y, h = _norm_mod_fwd(x_ref[...], gain_v, sh_ref[...], scale_v)
        h_ref[...] = h.astype(BF)
        dh = _nn(da_ref[...], wg_ref[...].reshape(FF, D)) + _nn(db_ref[...], wu_ref[...].reshape(FF, D))
        dxn, dsh, dsc, dgn = _norm_mod_bwd(dh, n, r, y, gain_v, scale_v)
        dx_ref[...] = dxo_ref[...] + dxn

        @pl.when(i == 0)
        def _():
            dsh_ref[...] = dsh
            dsc_ref[...] = dsc
            dgn_ref[...] = dgn

        @pl.when(i > 0)
        def _():
            dsh_ref[...] += dsh
            dsc_ref[...] += dsc
            dgn_ref[...] += dgn

    dx, h, dsh, dsc, dgn = pl.pallas_call(
        body_b, grid=(nt,), name=name + "_b",
        in_specs=[_row_spec(tm, D), _row_spec(tm, D), _row_spec(tm, FF), _row_spec(tm, FF), vec, vec, vec,
                  wspecs[0], wspecs[1]],
        out_specs=[_row_spec(tm, D), _row_spec(tm, D), acc, acc, acc],
        out_shape=[jax.ShapeDtypeStruct((T, D), F32), jax.ShapeDtypeStruct((T, D), BF), accs, accs, accs],
        compiler_params=_cparams(),
    )(dxo, x, da, db, shift, scale, gain, wall, wall)
    return dx, da, db, h, df, dsh, dsc, dgt, dgn


def _ffn_wgrad(da, db, s, h, df, name, scatter=(), allgather=()):
    T = h.shape[0]
    tk = _tile(T, 512)
    nk = T // tk
    nj = 2
    groups = [g for g in (
        (list(scatter), _Scatter, _scatter_sems, lambda t: t.shape),
        (list(allgather), _AllGather, _allgather_sems, lambda t: (8,) + t.shape),
    ) if g[0]]
    extra = [t for g in groups for t in g[0]]
    ne = len(extra)

    def body(*refs):
        da_ref, db_ref, s_ref, h_ref, df_ref = refs[:5]
        og_ref, ou_ref, od_ref = refs[5 + ne:8 + ne]
        ag, au, ad = refs[8 + 2 * ne:11 + 2 * ne]
        k = pl.program_id(1)
        step = pl.program_id(0) * nk + k
        if ne:
            plans, at, sem_at = [], 0, 11 + 2 * ne
            for arrs, make, _, _ in groups:
                n = len(arrs)
                plans.append(make(refs[5 + at:5 + at + n], refs[8 + ne + at:8 + ne + at + n], *refs[sem_at:sem_at + 3]))
                at, sem_at = at + n, sem_at + 3
            plan = _Plans(plans)
            _comm_begin(step, plan, nj * nk - 1)

        @pl.when(k == 0)
        def _():
            for acc in (ag, au, ad):
                acc[...] = jnp.zeros((FH, D), F32)

        ag[...] += _tn(da_ref[...], h_ref[...])
        au[...] += _tn(db_ref[...], h_ref[...])
        ad[...] += _tn(s_ref[...], df_ref[...])

        @pl.when(k == nk - 1)
        def _():
            og_ref[...] = ag[...].astype(BF)
            ou_ref[...] = au[...].astype(BF)
            od_ref[...] = ad[...].astype(BF)

        if ne:
            _comm_end(step, plan, nj * nk - 1)

    colblk = pl.BlockSpec((tk, FH), lambda j, k: (k, j))
    rowblk = pl.BlockSpec((tk, D), lambda j, k: (k, 0))
    outblk = pl.BlockSpec((FH, D), lambda j, k: (j, 0))
    outs = jax.ShapeDtypeStruct((nj * FH, D), BF)
    return pl.pallas_call(
        body, grid=(nj, nk), name=name,
        in_specs=[colblk, colblk, colblk, rowblk, rowblk] + [ANY] * ne,
        out_specs=[outblk, outblk, outblk] + [ANY] * ne,
        out_shape=[outs, outs, outs] + [jax.ShapeDtypeStruct(shape(t), t.dtype) for g in groups for t, shape in
                                        ((t, g[3]) for t in g[0])],
        scratch_shapes=[pltpu.VMEM((FH, D), F32)] * 3 + [sm for g in groups for sm in g[2](len(g[0]))],
        compiler_params=_cparams(("arbitrary", "arbitrary")),
    )(da, db, s, h, df, *extra)


def _loss_math(xv, target, gain_v):
    r = lax.rsqrt(jnp.mean(xv * xv, axis=-1, keepdims=True) + EPS)
    n = xv * r
    err = n * gain_v - target
    e2 = err * err
    part = e2[:, 0:128]
    for q in range(1, D // 128):
        part = part + e2[:, q * 128:(q + 1) * 128]
    dy = err * (1.0 / D)
    dn = dy * gain_v
    return _sum8(part) * (0.5 / D), r * (dn - n * jnp.mean(dn * n, axis=-1, keepdims=True)), _sum8(dy * n)


def _seg_cumsum(v, row_in_chunk, reverse=False):
    n = v.shape[0]
    s = 1
    while s < CH:
        if reverse:
            moved = pltpu.roll(v, n - s, 0)
            ok = row_in_chunk < CH - s
        else:
            moved = pltpu.roll(v, s, 0)
            ok = row_in_chunk >= s
        v = v + jnp.where(ok, moved, 0.0)
        s *= 2
    return v


def _conv_silu(xq_ext_ref, cw_ref, tm):
    c = cw_ref[0:1, :] * xq_ext_ref[pl.ds(5, tm), :]
    for j in range(1, 4):
        c = c + cw_ref[j:j + 1, :] * xq_ext_ref[pl.ds(5 + j, tm), :]
    return c, _sigmoid(c)


def _gates(ba, alog, dtb, lane):
    beta = _sigmoid(ba)
    arg = ba + dtb
    softplus = jnp.maximum(arg, 0.0) + jnp.log(1.0 + jnp.exp(-jnp.abs(arg)))
    g = -jnp.exp(alog) * softplus
    return jnp.where(lane < NH, beta, 0.0), jnp.where((lane >= NH) & (lane < 2 * NH), g, 0.0), _sigmoid(arg)


def _mix_proj(x, shift, scale, gain, winT, conv_w, alog, dtb):
    T = x.shape[0]
    tm = _tile(T)

    def body(x_ref, sh_ref, sc_ref, gn_ref, w_ref, cw_ref, al_ref, dt_ref,
             xq_ref, ba_ref, qn_ref, kn_ref, v_ref, z_ref, p_ref, bg_ref, gc_ref, ext):
        i = pl.program_id(0)
        _, _, _, h = _norm_mod_fwd(x_ref[...], gn_ref[...], sh_ref[...], sc_ref[...])
        hb = h.astype(BF)

        @pl.when(i == 0)
        def _():
            ext[pl.ds(0, 8), :] = jnp.zeros((8, 3 * GW), F32)

        xq = _nt(hb, w_ref[pl.ds(0, 3 * GW), :])
        xq_ref[...] = xq
        ext[pl.ds(8, tm), :] = xq
        z_ref[...] = _nt(hb, w_ref[pl.ds(3 * GW, GW), :])
        p_ref[...] = _nt(hb, w_ref[pl.ds(4 * GW, PW), :])
        ba = _nt(hb, w_ref[pl.ds(4 * GW + PW, 128), :])
        ba_ref[...] = ba

        c, sg = _conv_silu(ext, cw_ref, tm)
        ext[pl.ds(0, 8), :] = ext[pl.ds(tm, 8), :]
        qt = c * sg
        for hd in range(NH):
            cq = slice(hd * DH, (hd + 1) * DH)
            ck = slice(GW + hd * DH, GW + (hd + 1) * DH)
            qh, kh = qt[:, cq], qt[:, ck]
            qn_ref[:, cq] = qh * (lax.rsqrt(jnp.sum(qh * qh, axis=-1, keepdims=True) + EPS) * DH ** -0.5)
            kn_ref[:, cq] = kh * lax.rsqrt(jnp.sum(kh * kh, axis=-1, keepdims=True) + EPS)
        v_ref[...] = qt[:, 2 * GW:3 * GW]

        lane = lax.broadcasted_iota(jnp.int32, (tm, 128), 1)
        row = lax.broadcasted_iota(jnp.int32, (tm, 128), 0) % CH
        beta, g, _ = _gates(ba, al_ref[...], dt_ref[...], lane)
        bg_ref[...] = beta + g
        gc_ref[...] = _seg_cumsum(g, row)

    wide = lambda w: _row_spec(tm, w)
    shp = lambda w: jax.ShapeDtypeStruct((T, w), F32)
    return pl.pallas_call(
        body, grid=(T // tm,), name="mix_proj",
        in_specs=[wide(D), _const_spec((1, D)), _const_spec((1, D)), _const_spec((1, D)), _const_spec((DINP, D)),
                  _const_spec((4, 3 * GW)), _const_spec((1, 128)), _const_spec((1, 128))],
        out_specs=[wide(3 * GW), wide(128), wide(GW), wide(GW), wide(GW), wide(GW), wide(PW), wide(128), wide(128)],
        out_shape=[shp(3 * GW), shp(128), shp(GW), shp(GW), shp(GW), shp(GW), shp(PW), shp(128), shp(128)],
        scratch_shapes=[pltpu.VMEM((tm + 8, 3 * GW), F32)],
        compiler_params=_cparams(),
    )(x, shift, scale, gain, winT, conv_w, alog, dtb)


R2 = 2 * CH
TRI_PREC = None


def _tmm(fn, a, b):
    if TRI_PREC is None:
        return fn(a.astype(BF), b.astype(BF))
    return fn(a, b, precision=TRI_PREC)


def _pair_consts():
    ii = lax.broadcasted_iota(jnp.int32, (R2, R2), 0)
    jj = lax.broadcasted_iota(jnp.int32, (R2, R2), 1)
    same = (ii < CH) == (jj < CH)
    r = lax.broadcasted_iota(jnp.int32, (R2, 1), 0)
    return dict(causal=same & (ii >= jj), strict=same & (ii > jj), eye=(ii == jj).astype(F32), rowA=r < CH,
                last=(r == CH - 1) | (r == R2 - 1),
                rowS=lax.broadcasted_iota(jnp.int32, (2 * DH, 1), 0) < DH)


def _tri_inverse_many(ms, eye):
    pws = [-m for m in ms]
    ts = [eye + p for p in pws]
    for _ in range(5):
        pws = [_mm_split(p, p) for p in pws]
        ts = [_mm_split(t, eye + p) for t, p in zip(ts, pws)]
    return ts


def _mm_split(a, b):
    ah, bh = a.astype(BF), b.astype(BF)
    al, bl = (a - ah.astype(F32)).astype(BF), (b - bh.astype(F32)).astype(BF)
    return _nn(ah, bh) + _nn(ah, bl) + _nn(al, bh)


def _egl_rows(gl):
    egl = jnp.exp(gl)
    return egl, jnp.concatenate([jnp.broadcast_to(egl[0:1], (DH, 1)), jnp.broadcast_to(egl[CH:CH + 1], (DH, 1))], axis=0)


def _pair_intra(items, cn, tms=None):
    causal, rowA = cn["causal"], cn["rowA"]

    def bd(t):
        return jnp.concatenate([jnp.where(rowA, t, 0.0), jnp.where(rowA, 0.0, t)], axis=1).astype(BF)

    outs = []
    for q, k, v, beta, gcv, gl in items:
        gc_b = jnp.broadcast_to(gcv, (R2, R2))
        gam = jnp.where(causal, jnp.exp(jnp.where(causal, gc_b - gc_b.T, 0.0)), 0.0)
        kb = k * beta
        kbf = k.astype(BF)
        P = _nt(kb.astype(BF), kbf)
        QK = _nt(q.astype(BF), kbf)
        E = jnp.exp(gcv)
        outs.append(dict(gam=gam, kb=kb, vb=v * beta, P=P, QK=QK, E=E, Fd=jnp.exp(gl - gcv), kbE=kb * E,
                         Q=QK * gam, qE_bd=bd(q * E)))
    if tms is None:
        tms = _tri_inverse_many([jnp.where(cn["strict"], d["P"] * d["gam"], 0.0) for d in outs], cn["eye"])
    for d, tm_, (q, k, v, beta, gcv, gl) in zip(outs, tms, items):
        d["Tm"] = tm_
        d["u"] = _tmm(_nn, tm_, d["vb"])
        d["w_bd"] = bd(_tmm(_nn, tm_, d["kbE"]))
        d["kF_bd"] = bd(k * d["Fd"])
    return outs


def _pair_scan(it, S, egl_st):
    Sb = S.astype(BF)
    vn = it["u"] - _nn(it["w_bd"], Sb)
    vnb = vn.astype(BF)
    o = _nn(it["qE_bd"], Sb) + _nn(it["Q"].astype(BF), vnb)
    return vnb, o, S * egl_st + _tn(it["kF_bd"], vnb)


def _pair_forward(q, k, v, beta, gcv, gl, S, cn):
    fw = _pair_intra([(q, k, v, beta, gcv, gl)], cn)[0]
    fw["egl"], fw["egl_st"] = _egl_rows(gl)
    fw["vnb"], fw["o"], fw["S_new"] = _pair_scan(fw, S, fw["egl_st"])
    return fw


def _stack_heads(ref, rows, pair):
    return jnp.concatenate([ref[rows, (2 * pair) * DH:(2 * pair + 1) * DH],
                            ref[rows, (2 * pair + 1) * DH:(2 * pair + 2) * DH]], axis=0)


def _stack_cols(val, lane_a, lane_b, bcast_rows=None):
    a, b = val[:, lane_a:lane_a + 1], val[:, lane_b:lane_b + 1]
    if bcast_rows:
        a, b = jnp.broadcast_to(a, (bcast_rows, 1)), jnp.broadcast_to(b, (bcast_rows, 1))
    return jnp.concatenate([a, b], axis=0)


def _pool_windows(ext, tm, reverse):
    n = tm + HALO
    outs = []
    for gi in range(NG):
        a = ext[:, gi * 128:(gi + 1) * 128]
        s = 1
        while s < POOL_WINDOWS[gi]:
            a = a + pltpu.roll(a, (n - s) if reverse else s, 0)
            s *= 2
        outs.append(a[0:tm] if reverse else a[HALO:HALO + tm])
    return jnp.concatenate(outs, axis=1)


def _pool_count(tm, tile_index):
    t1 = (lax.broadcasted_iota(jnp.int32, (tm, PW), 0) + tile_index * tm + 1).astype(F32)
    win = jnp.concatenate([jnp.full((tm, 128), float(w), F32) for w in POOL_WINDOWS], axis=1)
    return 1.0 / jnp.minimum(t1, win)


def _chunk_item(qn_ref, kn_ref, v_ref, bg_ref, gc_ref, c, pr):
    r0 = pl.multiple_of(c * CH, CH)
    rows = pl.ds(r0, CH)
    bgv = bg_ref[rows, :]
    gcv_all = gc_ref[rows, :]
    gl_all = gc_ref[pl.ds(r0 + CH - 1, 1), :]
    ha, hb = 2 * pr, 2 * pr + 1
    return (_stack_heads(qn_ref, rows, pr), _stack_heads(kn_ref, rows, pr), _stack_heads(v_ref, rows, pr),
            _stack_cols(bgv, ha, hb), _stack_cols(gcv_all, NH + ha, NH + hb), _stack_cols(gl_all, NH + ha, NH + hb, CH))


CHUNK_GROUP = 4


def _mix_core(x, gate, qn, kn, v, z, p, bg, gc, gnorm, pool_w, pool_scale, w_out):
    T = x.shape[0]
    tm = _tile(T)
    nc = tm // CH
    cg = CHUNK_GROUP if nc % CHUNK_GROUP == 0 else 1
    npb = nc * (NH // 2)

    def body(x_ref, gt_ref, qn_ref, kn_ref, v_ref, z_ref, p_ref, bg_ref, gc_ref, gnm_ref, pw_ref, ps_ref, wo_ref,
             xo_ref, mx_ref, cat_ref, o_ref, sall_ref, tmall_ref, S_scr, pext, u_s, w_s, qe_s, kf_s, q_s):
        i = pl.program_id(0)

        @pl.when(i == 0)
        def _():
            S_scr[...] = jnp.zeros((NH * DH, DH), F32)
            pext[pl.ds(0, HALO), :] = jnp.zeros((HALO, PW), F32)

        cn = _pair_consts()

        def intra(g, carry):
            idx = [(g * cg + dc, pr) for dc in range(cg) for pr in range(NH // 2)]
            res = _pair_intra([_chunk_item(qn_ref, kn_ref, v_ref, bg_ref, gc_ref, c, pr) for c, pr in idx], cn)
            for (c, pr), d in zip(idx, res):
                pi = c * (NH // 2) + pr
                u_s[pi] = d["u"]
                w_s[pi] = d["w_bd"]
                qe_s[pi] = d["qE_bd"]
                kf_s[pi] = d["kF_bd"]
                q_s[pi] = d["Q"].astype(BF)
                tmall_ref[pi] = d["Tm"].astype(BF)
            return carry

        lax.fori_loop(0, nc // cg, intra, 0)

        def scan(c, carry):
            r0 = pl.multiple_of(c * CH, CH)
            rows = pl.ds(r0, CH)
            gl_all = gc_ref[pl.ds(r0 + CH - 1, 1), :]
            for pr in range(NH // 2):
                ha, hb = 2 * pr, 2 * pr + 1
                pi = c * (NH // 2) + pr
                S = S_scr[pl.ds(pr * 2 * DH, 2 * DH), :]
                sall_ref[c, ha:hb + 1] = S.reshape(2, DH, DH)
                _, egl_st = _egl_rows(_stack_cols(gl_all, NH + ha, NH + hb, CH))
                it = dict(u=u_s[pi], w_bd=w_s[pi], qE_bd=qe_s[pi], kF_bd=kf_s[pi], Q=q_s[pi])
                _, o, S_new = _pair_scan(it, S, egl_st)
                o_ref[rows, ha * DH:(ha + 1) * DH] = o[0:CH]
                o_ref[rows, hb * DH:(hb + 1) * DH] = o[CH:R2]
                S_scr[pl.ds(pr * 2 * DH, 2 * DH), :] = S_new
            return carry

        lax.fori_loop(0, nc, scan, 0)

        for hd in range(NH):
            cols = slice(hd * DH, (hd + 1) * DH)
            oh = o_ref[:, cols]
            zh = z_ref[:, cols]
            r = lax.rsqrt(jnp.mean(oh * oh, axis=-1, keepdims=True) + EPS)
            cat_ref[:, cols] = (oh * r * gnm_ref[...] * (zh * _sigmoid(zh))).astype(BF)

        pv = p_ref[...]
        pext[pl.ds(HALO, tm), :] = pv
        pooled = _pool_windows(pext[...], tm, False) * _pool_count(tm, i) - pv
        pext[pl.ds(0, HALO), :] = pext[pl.ds(tm, HALO), :]
        for gi in range(NG):
            cols = slice(gi * 128, (gi + 1) * 128)
            pm = _nn(pooled[:, cols].astype(BF), pw_ref[gi])
            cat_ref[:, GW + gi * 128:GW + (gi + 1) * 128] = (pm * ps_ref[:, cols]).astype(BF)

        mixed = _nn(cat_ref[...], wo_ref[...])
        mx_ref[...] = mixed
        xo_ref[...] = x_ref[...] + gt_ref[...] * mixed

    wide = lambda w: _row_spec(tm, w)
    return pl.pallas_call(
        body, grid=(T // tm,), name="mix_core",
        in_specs=[wide(D), _const_spec((1, D)), wide(GW), wide(GW), wide(GW), wide(GW), wide(PW), wide(128), wide(128),
                  _const_spec((1, DH)), _const_spec((NG, 128, 128)), _const_spec((1, PW)), _const_spec((D, D))],
        out_specs=[wide(D), wide(D), wide(D), wide(GW), pl.BlockSpec((nc, NH, DH, DH), lambda i: (i, 0, 0, 0)),
                   pl.BlockSpec((npb, R2, R2), lambda i: (i, 0, 0))],
        out_shape=[jax.ShapeDtypeStruct((T, D), F32), jax.ShapeDtypeStruct((T, D), F32),
                   jax.ShapeDtypeStruct((T, D), BF), jax.ShapeDtypeStruct((T, GW), F32),
                   jax.ShapeDtypeStruct((T // CH, NH, DH, DH), F32),
                   jax.ShapeDtypeStruct((T // CH * (NH // 2), R2, R2), BF)],
        scratch_shapes=[pltpu.VMEM((NH * DH, DH), F32), pltpu.VMEM((tm + HALO, PW), F32),
                        pltpu.VMEM((npb, R2, DH), F32), pltpu.VMEM((npb, R2, 2 * DH), BF),
                        pltpu.VMEM((npb, R2, 2 * DH), BF), pltpu.VMEM((npb, R2, 2 * DH), BF),
                        pltpu.VMEM((npb, R2, R2), BF)],
        compiler_params=_cparams(),
    )(x, gate, qn, kn, v, z, p, bg, gc, gnorm, pool_w, pool_scale, w_out)


def _pair_scan_bwd(it, S, dSn, do, egl, egl_st, cn):
    bf = lambda t: t.astype(BF)
    rowA, rowS = cn["rowA"], cn["rowS"]
    sel = lambda t: jnp.where(rowA, t[:, 0:DH], t[:, DH:2 * DH])
    Sb, dSb, dob = bf(S), bf(dSn), bf(do)
    vnb = bf(it["u"] - _nn(it["w_bd"], Sb))
    dvn = _tn(bf(it["Q"]), dob) + _nn(it["kF_bd"], dSb)
    dQ = _nt(dob, vnb)
    dqE = sel(_nt(dob, Sb))
    dkF = sel(_nt(vnb, dSb))
    dvnb = bf(dvn)
    dw = -sel(_nt(dvnb, Sb))
    dS_new = _tn(it["qE_bd"], dob) + egl_st * dSn - _tn(it["w_bd"], dvnb)
    prod = jnp.sum(dSn * S, axis=1, keepdims=True)
    d_egl_a = jnp.sum(jnp.where(rowS, prod, 0.0), axis=0, keepdims=True)
    d_egl_b = jnp.sum(jnp.where(rowS, 0.0, prod), axis=0, keepdims=True)
    return dict(dvn=dvn, dQ=dQ, dqE=dqE, dkF=dkF, dw=dw, degl=jnp.where(rowA, d_egl_a, d_egl_b) * egl), dS_new


def _pair_intra_bwd(items, cn):
    bf = lambda t: t.astype(BF)
    rowA = cn["rowA"]
    for d in items:
        d["kb"] = d["k"] * d["beta"]
        d["E"] = jnp.exp(d["gcv"])
        d["Fd"] = jnp.exp(d["gl"] - d["gcv"])
        d["TmT"] = d["Tm"].T
        d["dvb"] = _tmm(_nn, d["TmT"], d["dvn"])
        d["dkbE"] = _tmm(_nn, d["TmT"], d["dw"])
        dTm = _tmm(_nt, d["dvn"], d["v"] * d["beta"]) + _tmm(_nt, d["dw"], d["kb"] * d["E"])
        d["X"] = _tmm(_nt, dTm, d["Tm"])
    for d in items:
        d["dA"] = -_tmm(_nn, d["TmT"], d["X"])
    outs = []
    for d in items:
        q, k, v, beta, gam = d["q"], d["k"], d["v"], d["beta"], d["gam"]
        N = jnp.where(cn["strict"], d["dA"] * gam, 0.0)
        Rm = d["dQ"] * gam
        Wm = Rm * d["QK"] + N * d["P"]
        dgc = jnp.sum(Wm, axis=1, keepdims=True) - jnp.sum(Wm.T, axis=1, keepdims=True)
        kbf, Nb, Rb = bf(k), bf(N), bf(Rm)
        E, Fd = d["E"], d["Fd"]
        dq = d["dqE"] * E + _nn(Rb, kbf)
        dkb = d["dkbE"] * E + _nn(Nb, kbf)
        dk = d["dkF"] * Fd + _tn(Rb, bf(q)) + _tn(Nb, bf(d["kb"])) + beta * dkb
        dbeta = jnp.sum(dkb * k + d["dvb"] * v, axis=1, keepdims=True)
        dE = jnp.sum(d["dqE"] * q + d["dkbE"] * d["kb"], axis=1, keepdims=True)
        fdf = jnp.sum(d["dkF"] * k, axis=1, keepdims=True) * Fd
        dgl = d["degl"] + jnp.where(rowA, jnp.sum(jnp.where(rowA, fdf, 0.0), axis=0, keepdims=True),
                                    jnp.sum(jnp.where(rowA, 0.0, fdf), axis=0, keepdims=True))
        dgc = dgc + dE * E - fdf + jnp.where(cn["last"], dgl, 0.0)
        outs.append((dq, dk, beta * d["dvb"], dbeta, dgc))
    return outs


def _mix_core_bwd(dxo, gate, mixed, cat, o, sall, qn, kn, v, z, p, bg, gc, gnorm, pool_w, pool_scale, w_out, tm_all,
                  scatter=()):
    T = dxo.shape[0]
    tm = _tile(T, 256)
    nt = T // tm
    nc = tm // CH
    ns = len(scatter)
    cg = CHUNK_GROUP if nc % CHUNK_GROUP == 0 else 1
    npb = nc * (NH // 2)

    def body(*refs):
        (dx_ref, gt_ref, mx_ref, cat_ref, o_ref, sall_ref, qn_ref, kn_ref, v_ref, z_ref, p_ref, ph_ref, bg_ref,
         gc_ref, gnm_ref, pw_ref, ps_ref, wo_ref, tmall_ref) = refs[:19]
        (dq_ref, dk_ref, dv_ref, dz_ref, dp_ref, dbg_ref, dgt_ref, dwo_ref, dpw_ref, dps_ref,
         dgn_ref) = refs[19 + ns:30 + ns]
        dS_scr, pext, yext, do_buf, dwo_acc = refs[30 + 2 * ns:35 + 2 * ns]
        (gam_s, p_s, qk_s, tm_s, dqq_s, u_s, dvn_s, dqe_s, dkf_s, dw_s, w_s, qe_s, kf_s, q_s,
         degl_s) = refs[35 + 2 * ns:50 + 2 * ns]
        i = pl.program_id(0)
        ti = nt - 1 - i
        if ns:
            plan = _Scatter(refs[19:19 + ns], refs[30 + ns:30 + 2 * ns], *refs[50 + 2 * ns:])
            _comm_begin(i, plan, nt - 1)

        @pl.when(i == 0)
        def _():
            dS_scr[...] = jnp.zeros((NH * DH, DH), F32)
            yext[pl.ds(tm, HALO), :] = jnp.zeros((HALO, PW), F32)
            dwo_acc[...] = jnp.zeros((D, D), F32)
            dgt_ref[...] = jnp.zeros((8, D), F32)
            dpw_ref[...] = jnp.zeros((NG, 128, 128), F32)
            dps_ref[...] = jnp.zeros((8, PW), F32)
            dgn_ref[...] = jnp.zeros((8, DH), F32)

        dx2 = dx_ref[...]
        dgt_ref[...] += _sum8(mx_ref[...] * dx2)
        dmix = (gt_ref[...] * dx2).astype(BF)
        dcat = _nt(dmix, wo_ref[...])
        dwo_acc[...] += _tn(cat_ref[...], dmix)

        pv = p_ref[...]
        pext[pl.ds(0, HALO), :] = jnp.where(ti == 0, 0.0, ph_ref[...])
        pext[pl.ds(HALO, tm), :] = pv
        inv_cnt = _pool_count(tm, ti)
        pooled = _pool_windows(pext[...], tm, False) * inv_cnt - pv
        dpooled = []
        for gi in range(NG):
            cols = slice(gi * 128, (gi + 1) * 128)
            pgb = pooled[:, cols].astype(BF)
            pm = _nn(pgb, pw_ref[gi])
            dpo = dcat[:, GW + gi * 128:GW + (gi + 1) * 128]
            dps_ref[:, cols] += _sum8(dpo * pm)
            dpm = (dpo * ps_ref[:, cols]).astype(BF)
            dpooled.append(_nt(dpm, pw_ref[gi]))
            dpw_ref[gi] += _tn(pgb, dpm)
        dpooled = jnp.concatenate(dpooled, axis=1)
        y = dpooled * inv_cnt
        yext[pl.ds(0, tm), :] = y
        dp_ref[...] = _pool_windows(yext[...], tm, True) - dpooled
        yext[pl.ds(tm, HALO), :] = y[0:HALO]

        gnm = gnm_ref[...]
        dgn = jnp.zeros((8, DH), F32)
        for hd in range(NH):
            cols = slice(hd * DH, (hd + 1) * DH)
            oh = o_ref[:, cols]
            zh = z_ref[:, cols]
            r = lax.rsqrt(jnp.mean(oh * oh, axis=-1, keepdims=True) + EPS)
            n = oh * r
            sg = _sigmoid(zh)
            zs = zh * sg
            dgo = dcat[:, cols]
            dgn = dgn + _sum8(dgo * zs * n)
            dn = dgo * zs * gnm
            do_buf[:, cols] = r * (dn - n * jnp.mean(dn * n, axis=-1, keepdims=True))
            dz_ref[:, cols] = dgo * n * gnm * (sg * (1.0 + zh * (1.0 - sg)))
        dgn_ref[...] += dgn

        cn = _pair_consts()
        lane_c = lax.broadcasted_iota(jnp.int32, (CH, 128), 1)

        def intra(g, carry):
            idx = [(g * cg + dc, pr) for dc in range(cg) for pr in range(NH // 2)]
            res = _pair_intra([_chunk_item(qn_ref, kn_ref, v_ref, bg_ref, gc_ref, c, pr) for c, pr in idx], cn,
                              tms=[tmall_ref[c * (NH // 2) + pr].astype(F32) for c, pr in idx])
            for (c, pr), d in zip(idx, res):
                pi = c * (NH // 2) + pr
                gam_s[pi], p_s[pi], qk_s[pi], tm_s[pi], u_s[pi] = d["gam"], d["P"], d["QK"], d["Tm"], d["u"]
                w_s[pi], qe_s[pi], kf_s[pi], q_s[pi] = d["w_bd"], d["qE_bd"], d["kF_bd"], d["Q"].astype(BF)
            return carry

        lax.fori_loop(0, nc // cg, intra, 0)

        def scan(cc, carry):
            c = nc - 1 - cc
            r0 = pl.multiple_of(c * CH, CH)
            rows = pl.ds(r0, CH)
            gl_all = gc_ref[pl.ds(r0 + CH - 1, 1), :]
            for pr in range(NH // 2):
                ha, hb = 2 * pr, 2 * pr + 1
                pi = c * (NH // 2) + pr
                srows = pl.ds(pr * 2 * DH, 2 * DH)
                S = sall_ref[c, ha:hb + 1].reshape(2 * DH, DH)
                egl, egl_st = _egl_rows(_stack_cols(gl_all, NH + ha, NH + hb, CH))
                it = dict(u=u_s[pi], w_bd=w_s[pi], qE_bd=qe_s[pi], kF_bd=kf_s[pi], Q=q_s[pi])
                g, dS_new = _pair_scan_bwd(it, S, dS_scr[srows, :], _stack_heads(do_buf, rows, pr), egl, egl_st, cn)
                dvn_s[pi], dqq_s[pi], dqe_s[pi], dkf_s[pi], dw_s[pi] = g["dvn"], g["dQ"], g["dqE"], g["dkF"], g["dw"]
                degl_s[pi] = g["degl"]
                dS_scr[srows, :] = dS_new
            return carry

        lax.fori_loop(0, nc, scan, 0)

        def intra_bwd(g, carry):
            idx = [(g * cg + dc, pr) for dc in range(cg) for pr in range(NH // 2)]
            items = []
            for c, pr in idx:
                pi = c * (NH // 2) + pr
                q, k, vv, beta, gcv, gl = _chunk_item(qn_ref, kn_ref, v_ref, bg_ref, gc_ref, c, pr)
                items.append(dict(q=q, k=k, v=vv, beta=beta, gcv=gcv, gl=gl, gam=gam_s[pi], P=p_s[pi], QK=qk_s[pi],
                                  Tm=tm_s[pi], dvn=dvn_s[pi], dQ=dqq_s[pi], dqE=dqe_s[pi], dkF=dkf_s[pi], dw=dw_s[pi],
                                  degl=degl_s[pi]))
            res = _pair_intra_bwd(items, cn)
            for dc in range(cg):
                c = g * cg + dc
                rows = pl.ds(pl.multiple_of(c * CH, CH), CH)
                dbg = jnp.zeros((CH, 128), F32)
                for pr in range(NH // 2):
                    dq, dk, dv, dbeta, dgc = res[dc * (NH // 2) + pr]
                    for hd, half in ((2 * pr, slice(0, CH)), (2 * pr + 1, slice(CH, R2))):
                        cols = slice(hd * DH, (hd + 1) * DH)
                        dq_ref[rows, cols] = dq[half]
                        dk_ref[rows, cols] = dk[half]
                        dv_ref[rows, cols] = dv[half]
                        dbg = dbg + jnp.where(lane_c == hd, dbeta[half], 0.0) + jnp.where(lane_c == NH + hd, dgc[half], 0.0)
                dbg_ref[rows, :] = dbg
            return carry

        lax.fori_loop(0, nc // cg, intra_bwd, 0)

        lane = lax.broadcasted_iota(jnp.int32, (tm, 128), 1)
        row = lax.broadcasted_iota(jnp.int32, (tm, 128), 0) % CH
        dbg_all = dbg_ref[...]
        dg = _seg_cumsum(jnp.where(lane >= NH, dbg_all, 0.0), row, reverse=True)
        dbg_ref[...] = jnp.where(lane < NH, dbg_all, dg)

        @pl.when(i == nt - 1)
        def _():
            dwo_ref[...] = dwo_acc[...].astype(BF)

        if ns:
            _comm_end(i, plan, nt - 1)

    rev = lambda w: pl.BlockSpec((tm, w), lambda i: (nt - 1 - i, 0))
    halo = pl.BlockSpec((HALO, PW), lambda i: (jnp.maximum((nt - 1 - i) * (tm // HALO) - 1, 0), 0))
    shp = lambda w: jax.ShapeDtypeStruct((T, w), F32)
    fix = lambda *s: pl.BlockSpec(s, lambda i: (0,) * len(s))
    return pl.pallas_call(
        body, grid=(nt,), name="mix_core_bwd",
        in_specs=[rev(D), _const_spec((1, D)), rev(D), rev(D), rev(GW),
                  pl.BlockSpec((nc, NH, DH, DH), lambda i: (nt - 1 - i, 0, 0, 0)),
                  rev(GW), rev(GW), rev(GW), rev(GW), rev(PW), halo, rev(128), rev(128),
                  _const_spec((1, DH)), _const_spec((NG, 128, 128)), _const_spec((1, PW)), _const_spec((D, D)),
                  pl.BlockSpec((npb, R2, R2), lambda i: (nt - 1 - i, 0, 0))]
        + [ANY] * ns,
        out_specs=[rev(GW), rev(GW), rev(GW), rev(GW), rev(PW), rev(128),
                   fix(8, D), fix(D, D), fix(NG, 128, 128), fix(8, PW), fix(8, DH)] + [ANY] * ns,
        out_shape=[shp(GW), shp(GW), shp(GW), shp(GW), shp(PW), shp(128),
                   jax.ShapeDtypeStruct((8, D), F32), jax.ShapeDtypeStruct((D, D), BF),
                   jax.ShapeDtypeStruct((NG, 128, 128), F32), jax.ShapeDtypeStruct((8, PW), F32),
                   jax.ShapeDtypeStruct((8, DH), F32)] + [jax.ShapeDtypeStruct(t.shape, t.dtype) for t in scatter],
        scratch_shapes=[pltpu.VMEM((NH * DH, DH), F32), pltpu.VMEM((tm + HALO, PW), F32),
                        pltpu.VMEM((tm + HALO, PW), F32), pltpu.VMEM((tm, GW), F32), pltpu.VMEM((D, D), F32)]
        + [pltpu.VMEM((npb, R2, R2), F32)] * 5 + [pltpu.VMEM((npb, R2, DH), F32)] * 5
        + [pltpu.VMEM((npb, R2, 2 * DH), BF)] * 3 + [pltpu.VMEM((npb, R2, R2), BF), pltpu.VMEM((npb, R2, 1), F32)]
        + (_scatter_sems(ns) if ns else []),
        compiler_params=_cparams(),
    )(dxo, gate, mixed, cat, o, sall, qn, kn, v, z, p, p, bg, gc, gnorm, pool_w, pool_scale, w_out, tm_all, *scatter)


def _mix_proj_bwd(dxo, dqn, dkn, dv, dz, dp, dbg, xq, ba, x, shift, scale, gain, winT, conv_w, alog, dtb):
    T = x.shape[0]
    tm = min(256, T)
    nt = T // tm
    W3 = 3 * GW

    def body(dxo_ref, dqn_ref, dkn_ref, dv_ref, dz_ref, dp_ref, dbg_ref, xq_ref, xh_ref, ba_ref, x_ref, sh_ref, sc_ref,
             gn_ref, w_ref, cw_ref, al_ref, dt_ref,
             dx_ref, dw_ref, dcw_ref, dal_ref, ddt_ref, dsh_ref, dsc_ref, dgn_ref,
             ext, dcext, dproj, dw_acc):
        i = pl.program_id(0)
        ti = nt - 1 - i

        @pl.when(i == 0)
        def _():
            dcext[pl.ds(tm, 8), :] = jnp.zeros((8, W3), F32)
            dw_acc[...] = jnp.zeros((DINP, D), F32)
            dcw_ref[...] = jnp.zeros((4, 8, W3), F32)
            dal_ref[...] = jnp.zeros((8, 128), F32)
            ddt_ref[...] = jnp.zeros((8, 128), F32)
            dsh_ref[...] = jnp.zeros((8, D), F32)
            dsc_ref[...] = jnp.zeros((8, D), F32)
            dgn_ref[...] = jnp.zeros((8, D), F32)

        ext[pl.ds(0, 8), :] = jnp.where(ti == 0, 0.0, xh_ref[...])
        ext[pl.ds(8, tm), :] = xq_ref[...]
        c, sg = _conv_silu(ext, cw_ref, tm)
        qt = c * sg
        dsilu = sg * (1.0 + c * (1.0 - sg))
        for hd in range(NH):
            for part, dref, mult in ((0, dqn_ref, DH ** -0.5), (1, dkn_ref, 1.0)):
                cols = slice(part * GW + hd * DH, part * GW + (hd + 1) * DH)
                xh = qt[:, cols]
                rr = lax.rsqrt(jnp.sum(xh * xh, axis=-1, keepdims=True) + EPS)
                unit = xh * rr
                du = dref[:, hd * DH:(hd + 1) * DH] * mult
                dxh = rr * (du - unit * jnp.sum(du * unit, axis=-1, keepdims=True))
                dcext[pl.ds(0, tm), cols] = dxh * dsilu[:, cols]
        dcext[pl.ds(0, tm), 2 * GW:W3] = dv_ref[...] * dsilu[:, 2 * GW:W3]
        dc = dcext[pl.ds(0, tm), :]
        dxq = jnp.zeros((tm, W3), F32)
        for j in range(4):
            dcw_ref[j] += _sum8(dc * ext[pl.ds(5 + j, tm), :])
            dxq = dxq + cw_ref[j:j + 1, :] * dcext[pl.ds(3 - j, tm), :]
        dcext[pl.ds(tm, 8), :] = dc[0:8]
        lane = lax.broadcasted_iota(jnp.int32, (tm, 128), 1)
        bav = ba_ref[...]
        beta, g, sarg = _gates(bav, al_ref[...], dt_ref[...], lane)
        dbg_v = dbg_ref[...]
        is_g = (lane >= NH) & (lane < 2 * NH)
        dbraw = jnp.where(lane < NH, dbg_v * beta * (1.0 - beta), 0.0)
        daraw = jnp.where(is_g, dbg_v * (-jnp.exp(al_ref[...])) * sarg, 0.0)
        dal_ref[...] += _sum8(jnp.where(is_g, dbg_v * g, 0.0))
        ddt_ref[...] += _sum8(daraw)
        dproj[:, 0:W3] = dxq.astype(BF)
        dproj[:, W3:W3 + GW] = dz_ref[...].astype(BF)
        dproj[:, W3 + GW:W3 + GW + PW] = dp_ref[...].astype(BF)
        dproj[:, W3 + GW + PW:DINP] = (dbraw + daraw).astype(BF)
        gain_v, scale_v = gn_ref[...], sc_ref[...]
        n, r, y, h = _norm_mod_fwd(x_ref[...], gain_v, sh_ref[...], scale_v)
        dpj = dproj[...]
        dh = _nn(dpj, w_ref[...])
        dw_acc[...] += _tn(dpj, h.astype(BF))
        dxn, dsh, dsc, dgn = _norm_mod_bwd(dh, n, r, y, gain_v, scale_v)
        dx_ref[...] = dxo_ref[...] + dxn
        dsh_ref[...] += dsh
        dsc_ref[...] += dsc
        dgn_ref[...] += dgn

        @pl.when(i == nt - 1)
        def _():
            dw_ref[...] = dw_acc[...].astype(BF)

    rev = lambda w: pl.BlockSpec((tm, w), lambda i: (nt - 1 - i, 0))
    halo = pl.BlockSpec((8, W3), lambda i: (jnp.maximum((nt - 1 - i) * (tm // 8) - 1, 0), 0))
    fix = lambda *s: pl.BlockSpec(s, lambda i: (0,) * len(s))
    vec = _const_spec((1, D))
    return pl.pallas_call(
        body, grid=(nt,), name="mix_proj_bwd",
        in_specs=[rev(D), rev(GW), rev(GW), rev(GW), rev(GW), rev(PW), rev(128), rev(W3), halo, rev(128), rev(D),
                  vec, vec, vec, _const_spec((DINP, D)), _const_spec((4, W3)), _const_spec((1, 128)),
                  _const_spec((1, 128))],
        out_specs=[rev(D), fix(DINP, D), fix(4, 8, W3), fix(8, 128), fix(8, 128), fix(8, D), fix(8, D), fix(8, D)],
        out_shape=[jax.ShapeDtypeStruct((T, D), F32), jax.ShapeDtypeStruct((DINP, D), BF),
                   jax.ShapeDtypeStruct((4, 8, W3), F32), jax.ShapeDtypeStruct((8, 128), F32),
                   jax.ShapeDtypeStruct((8, 128), F32), jax.ShapeDtypeStruct((8, D), F32),
                   jax.ShapeDtypeStruct((8, D), F32), jax.ShapeDtypeStruct((8, D), F32)],
        scratch_shapes=[pltpu.VMEM((tm + 8, W3), F32), pltpu.VMEM((tm + 8, W3), F32), pltpu.VMEM((tm, DINP), BF),
                        pltpu.VMEM((DINP, D), F32)],
        compiler_params=_cparams(),
    )(dxo, dqn, dkn, dv, dz, dp, dbg, xq, xq, ba, x, shift, scale, gain, winT, conv_w, alog, dtb)


MESH = pl.DeviceIdType.MESH
CHIP_RELS = ((1, 0), (0, 1), (1, 1))
DEV_RELS = tuple((dx, dy, dc) for dx in (0, 1) for dy in (0, 1) for dc in (0, 1) if (dx, dy, dc) != (0, 0, 0))
NMOD = 9
ADA_SH = NMOD * D // 4
WIN_SH = DIN // 4
WIN_PAD = 672
MSG_ROWS = 16
ANY = pl.BlockSpec(memory_space=pl.ANY)
VM = pl.BlockSpec(memory_space=pltpu.VMEM)


def _place():
    x, y, c = lax.axis_index("x"), lax.axis_index("y"), lax.axis_index("c")
    return x, y, c


class _SplitGather:
    N_SEMS = (3, 3, 3, 3, 1)

    def __init__(self, src, dst, ici_s, ici_r, d2d_s, d2d_r, lsem):
        self.src, self.dst = src, dst
        self.sems = (ici_s, ici_r, d2d_s, d2d_r)
        self.x, self.y, self.c = _place()
        self.myj = 2 * self.x + self.y
        half = src.shape[0] // 2
        self.mine = pl.ds(pl.multiple_of(self.c * half, 16), half)
        self.other = pl.ds(pl.multiple_of((1 - self.c) * half, 16), half)
        self.local = pltpu.make_async_copy(src, dst.at[self.myj], lsem.at[0])

    def _ici(self, k, slot):
        dx, dy = CHIP_RELS[k]
        return pltpu.make_async_remote_copy(
            src_ref=self.src.at[self.mine], dst_ref=self.dst.at[slot, self.mine], send_sem=self.sems[0].at[k],
            recv_sem=self.sems[1].at[k], device_id=(self.x ^ dx, self.y ^ dy, self.c), device_id_type=MESH)

    def _d2d(self, k, rows):
        dx, dy = CHIP_RELS[k]
        blk = self.dst.at[2 * (self.x ^ dx) + (self.y ^ dy), rows]
        return pltpu.make_async_remote_copy(
            src_ref=blk, dst_ref=blk, send_sem=self.sems[2].at[k], recv_sem=self.sems[3].at[k],
            device_id=(self.x, self.y, 1 - self.c), device_id_type=MESH)

    def start(self):
        self.local.start()
        for k in range(3):
            self._ici(k, self.myj).start()

    def forward(self):
        for k, (dx, dy) in enumerate(CHIP_RELS):
            self._ici(k, 2 * (self.x ^ dx) + (self.y ^ dy)).wait_recv()
            self._d2d(k, self.mine).start()

    def finish(self):
        for k in range(3):
            self._d2d(k, self.other).wait_recv()
        for k in range(3):
            self._d2d(k, self.mine).wait_send()
            self._ici(k, self.myj).wait_send()
        self.local.wait()


class _Scatter:
    def __init__(self, ins, outs, send, recv, lsem):
        self.ins, self.outs, self.send, self.recv, self.lsem = ins, outs, send, recv, lsem
        self.x, self.y, self.c = _place()
        self.myj = 2 * self.x + self.y

    def _copy(self, a, k, landing):
        dx, dy = CHIP_RELS[k]
        pj = 2 * (self.x ^ dx) + (self.y ^ dy)
        return pltpu.make_async_remote_copy(
            src_ref=self.ins[a].at[pj], dst_ref=self.outs[a].at[pj if landing else self.myj],
            send_sem=self.send.at[a, k], recv_sem=self.recv.at[a, k],
            device_id=(self.x ^ dx, self.y ^ dy, self.c), device_id_type=MESH)

    def _local(self, a):
        return pltpu.make_async_copy(self.ins[a].at[self.myj], self.outs[a].at[self.myj], self.lsem.at[a])

    def start(self):
        for a in range(len(self.ins)):
            self._local(a).start()
            for k in range(3):
                self._copy(a, k, False).start()

    def finish(self):
        for a in range(len(self.ins)):
            for k in range(3):
                self._copy(a, k, True).wait_recv()
            for k in range(3):
                self._copy(a, k, False).wait_send()
            self._local(a).wait()


def _scatter_sems(n):
    return [pltpu.SemaphoreType.DMA((n, 3)), pltpu.SemaphoreType.DMA((n, 3)), pltpu.SemaphoreType.DMA((n,))]


def _gather_sems():
    return [pltpu.SemaphoreType.DMA((k,)) for k in _SplitGather.N_SEMS]


def _comm_begin(i, plan, last):
    @pl.when(i == 0)
    def _():
        plan.start()

    if hasattr(plan, "forward"):
        @pl.when(i == max(last - 3, 0))
        def _():
            plan.forward()


def _comm_end(i, plan, last):
    @pl.when(i == last)
    def _():
        plan.finish()


def _ada_exchange(msg, w_ada, b_ada, wblock):
    def body(msg_ref, w_ref, b_ref, wb_ref, all_ref, mod_ref, wg_ref, modp, send1, recv1, send2, recv2, lsem, *gsems):
        x, y, c = _place()
        me = 4 * x + 2 * y + c
        own = pltpu.make_async_copy(msg_ref, all_ref.at[me], lsem.at[0])
        own.start()

        def gather(k, rel, slot):
            dx, dy, dc = rel
            return pltpu.make_async_remote_copy(
                src_ref=msg_ref, dst_ref=all_ref.at[slot], send_sem=send1.at[k], recv_sem=recv1.at[k],
                device_id=(x ^ dx, y ^ dy, c ^ dc), device_id_type=MESH)

        for k, rel in enumerate(DEV_RELS):
            gather(k, rel, me).start()
        for k, (dx, dy, dc) in enumerate(DEV_RELS):
            gather(k, (dx, dy, dc), 4 * (x ^ dx) + 2 * (y ^ dy) + (c ^ dc)).wait_recv()
        for k, rel in enumerate(DEV_RELS):
            gather(k, rel, me).wait_send()
        own.wait()
        wgather = _SplitGather(wb_ref, wg_ref, *gsems)
        wgather.start()

        for d in range(8):
            cv = all_ref[d, 0:8, :]
            act = cv * _sigmoid(cv)
            modp[d] = _nn(act, w_ref[...], precision=HI) + b_ref[...]

        myj = 2 * x + y
        keep = pltpu.make_async_copy(modp.at[me], mod_ref.at[myj], lsem.at[1])
        keep.start()

        def scatter(k, rel):
            dx, dy = rel
            return pltpu.make_async_remote_copy(
                src_ref=modp.at[4 * (x ^ dx) + 2 * (y ^ dy) + c], dst_ref=mod_ref.at[myj],
                send_sem=send2.at[k], recv_sem=recv2.at[k], device_id=(x ^ dx, y ^ dy, c), device_id_type=MESH)

        def landed(k, rel):
            dx, dy = rel
            return pltpu.make_async_remote_copy(
                src_ref=modp.at[me], dst_ref=mod_ref.at[2 * (x ^ dx) + (y ^ dy)],
                send_sem=send2.at[k], recv_sem=recv2.at[k], device_id=(x ^ dx, y ^ dy, c), device_id_type=MESH)

        for k, rel in enumerate(CHIP_RELS):
            scatter(k, rel).start()
        for k, rel in enumerate(CHIP_RELS):
            landed(k, rel).wait_recv()
        for k, rel in enumerate(CHIP_RELS):
            scatter(k, rel).wait_send()
        keep.wait()
        wgather.forward()
        wgather.finish()

    return pl.pallas_call(
        body, name="ada_exchange", in_specs=[VM, VM, VM, ANY], out_specs=[VM, VM, ANY],
        out_shape=[jax.ShapeDtypeStruct((8, MSG_ROWS, D), F32), jax.ShapeDtypeStruct((4, 8, ADA_SH), F32),
                   jax.ShapeDtypeStruct((4,) + wblock.shape, wblock.dtype)],
        scratch_shapes=[pltpu.VMEM((8, 8, ADA_SH), F32), pltpu.SemaphoreType.DMA((7,)), pltpu.SemaphoreType.DMA((7,)),
                        pltpu.SemaphoreType.DMA((3,)), pltpu.SemaphoreType.DMA((3,)), pltpu.SemaphoreType.DMA((2,))]
        + _gather_sems(),
        compiler_params=pltpu.CompilerParams(vmem_limit_bytes=VMEM_LIMIT),
    )(msg, w_ada, b_ada, wblock)


def _chip_exchange(parts, name):
    n = len(parts)

    def body(*refs):
        plan = _Scatter(refs[:n], refs[n:2 * n], *refs[2 * n:])
        plan.start()
        plan.finish()

    return pl.pallas_call(
        body, name=name, in_specs=[ANY] * n, out_specs=[ANY] * n,
        out_shape=[jax.ShapeDtypeStruct(p.shape, p.dtype) for p in parts], scratch_shapes=_scatter_sems(n),
    )(*parts)


class _AllGather:
    def __init__(self, ins, outs, send, recv, lsem):
        self.ins, self.outs, self.send, self.recv, self.lsem = ins, outs, send, recv, lsem
        self.x, self.y, self.c = _place()
        self.me = 4 * self.x + 2 * self.y + self.c

    def _copy(self, a, k, landing):
        dx, dy, dc = DEV_RELS[k]
        peer = 4 * (self.x ^ dx) + 2 * (self.y ^ dy) + (self.c ^ dc)
        return pltpu.make_async_remote_copy(
            src_ref=self.ins[a], dst_ref=self.outs[a].at[peer if landing else self.me],
            send_sem=self.send.at[a, k], recv_sem=self.recv.at[a, k],
            device_id=(self.x ^ dx, self.y ^ dy, self.c ^ dc), device_id_type=MESH)

    def _local(self, a):
        return pltpu.make_async_copy(self.ins[a], self.outs[a].at[self.me], self.lsem.at[a])

    def start(self):
        for a in range(len(self.ins)):
            self._local(a).start()
            for k in range(7):
                self._copy(a, k, False).start()

    def finish(self):
        for a in range(len(self.ins)):
            for k in range(7):
                self._copy(a, k, True).wait_recv()
            for k in range(7):
                self._copy(a, k, False).wait_send()
            self._local(a).wait()


def _allgather_sems(n):
    return [pltpu.SemaphoreType.DMA((n, 7)), pltpu.SemaphoreType.DMA((n, 7)), pltpu.SemaphoreType.DMA((n,))]


class _Plans:
    def __init__(self, plans):
        self.plans = plans

    def start(self):
        for p in self.plans:
            p.start()

    def finish(self):
        for p in self.plans:
            p.finish()


def _pair_exchange(parts, name):
    n = len(parts)

    def body(*refs):
        ins, outs = refs[:n], refs[n:2 * n]
        send, recv = refs[2 * n:]
        x, y, c = _place()
        cps = [pltpu.make_async_remote_copy(
            src_ref=ins[a], dst_ref=outs[a], send_sem=send.at[a], recv_sem=recv.at[a],
            device_id=(x, y, 1 - c), device_id_type=MESH) for a in range(n)]
        for cp in cps:
            cp.start()
        for cp in cps:
            cp.wait_recv()
        for cp in cps:
            cp.wait_send()

    shapes = [jax.ShapeDtypeStruct(p.shape, p.dtype) for p in parts]
    return pl.pallas_call(
        body, name=name, in_specs=[ANY] * n, out_specs=[ANY] * n, out_shape=shapes,
        scratch_shapes=[pltpu.SemaphoreType.DMA((n,)), pltpu.SemaphoreType.DMA((n,))],
    )(*parts)


def _pair_swap_rows(parts, name):
    n = len(parts)

    def body(*refs):
        ins, outs = refs[:n], refs[n:2 * n]
        send, recv = refs[2 * n:]
        x, y, c = _place()
        half = ins[0].shape[1] // 2
        theirs = pl.ds(pl.multiple_of((1 - c) * half, 16), half)
        cps = [pltpu.make_async_remote_copy(
            src_ref=ins[a].at[pl.ds(0, 4), theirs], dst_ref=outs[a], send_sem=send.at[a], recv_sem=recv.at[a],
            device_id=(x, y, 1 - c), device_id_type=MESH) for a in range(n)]
        for cp in cps:
            cp.start()
        for cp in cps:
            cp.wait_recv()
        for cp in cps:
            cp.wait_send()

    shapes = [jax.ShapeDtypeStruct((4, p.shape[1] // 2, p.shape[2]), p.dtype) for p in parts]
    return pl.pallas_call(
        body, name=name, in_specs=[ANY] * n, out_specs=[ANY] * n, out_shape=shapes,
        scratch_shapes=[pltpu.SemaphoreType.DMA((n,)), pltpu.SemaphoreType.DMA((n,))],
    )(*parts)


def _add_blocks(p, q, name):
    n, rows, width = q.shape

    def body(p_ref, q_ref, o_ref):
        mine = pl.ds(pl.multiple_of(lax.axis_index("c") * rows, 16), rows)
        o_ref[...] = (p_ref[mine, :].astype(F32) + q_ref[...].astype(F32)).astype(o_ref.dtype)

    blk = pl.BlockSpec((None, rows, width), lambda j: (j, 0, 0))
    return pl.pallas_call(
        body, grid=(n,), name=name, in_specs=[pl.BlockSpec((None, 2 * rows, width), lambda j: (j, 0, 0)), blk],
        out_specs=blk, out_shape=jax.ShapeDtypeStruct(q.shape, q.dtype), compiler_params=_cparams(),
    )(p, q)


def _adamw_halves(mine, theirs, w, m, v, name):
    half, width = mine.shape

    def body(mine_ref, theirs_ref, w_ref, m_ref, v_ref, g_out, d_out, m_out, v_out):
        g = jnp.where(lax.axis_index("c") == pl.program_id(0), mine_ref[...], theirs_ref[...])
        delta, m_new, v_new = _adamw_math(g, w_ref[...], m_ref[...], v_ref[...])
        g_out[...] = g
        d_out[...] = delta
        m_out[...] = m_new
        v_out[...] = v_new

    whole = pl.BlockSpec((half, width), lambda j: (0, 0))
    blk = pl.BlockSpec((half, width), lambda j: (j, 0))
    return pl.pallas_call(
        body, grid=(2,), name=name, in_specs=[whole, whole, blk, blk, blk], out_specs=[blk] * 4,
        out_shape=[jax.ShapeDtypeStruct((2 * half, width), F32)] * 4, compiler_params=_cparams(),
    )(mine, theirs, w, m, v)


def _row_tile(rows, cap):
    best = rows
    for t in range(8, min(cap, rows) + 1, 8):
        if rows % t == 0:
            best = t
    return best if rows % 8 == 0 else rows


def _sum_slots(parts, name):
    n, rows, width = parts.shape
    tr = _row_tile(rows, 352)

    def body(p_ref, o_ref):
        acc = p_ref[0].astype(F32)
        for j in range(1, n):
            acc = acc + p_ref[j].astype(F32)
        o_ref[...] = acc

    return pl.pallas_call(
        body, grid=(rows // tr,), name=name,
        in_specs=[pl.BlockSpec((n, tr, width), lambda i: (0, i, 0))],
        out_specs=pl.BlockSpec((tr, width), lambda i: (i, 0)),
        out_shape=jax.ShapeDtypeStruct((rows, width), F32),
        compiler_params=_cparams(),
    )(parts)


def _adamw_math(g, w, m, v):
    m_new = ADAM_B1 * m + (1.0 - ADAM_B1) * g
    v_new = ADAM_B2 * v + (1.0 - ADAM_B2) * (g * g)
    m_hat = m_new / (1.0 - ADAM_B1 ** ADAM_STEP)
    v_hat = v_new / (1.0 - ADAM_B2 ** ADAM_STEP)
    delta = -ADAM_LR * (m_hat / (jnp.sqrt(v_hat) + ADAM_EPS) + ADAM_WD * w)
    return delta, m_new, v_new


def _adamw(grads, w, m, v, name):
    rows, width = w.shape
    tr = _row_tile(rows, 256 if width <= 1024 else 128)
    ng = len(grads)

    def body(*refs):
        g = refs[0][...]
        for r in refs[1:ng]:
            g = g + r[...]
        w_ref, m_ref, v_ref, g_out, d_out, m_out, v_out = refs[ng:]
        delta, m_new, v_new = _adamw_math(g, w_ref[...], m_ref[...], v_ref[...])
        g_out[...] = g
        d_out[...] = delta
        m_out[...] = m_new
        v_out[...] = v_new

    if rows % 8 == 0 or width % 512:
        blk, steps = pl.BlockSpec((tr, width), lambda i: (i, 0)), rows // tr
    else:
        blk, steps = pl.BlockSpec((rows, 256), lambda i: (0, i)), width // 256
    return pl.pallas_call(
        body, grid=(steps,), name=name,
        in_specs=[blk] * (ng + 3), out_specs=[blk] * 4,
        out_shape=[jax.ShapeDtypeStruct((rows, width), F32)] * 4,
        compiler_params=_cparams(),
    )(*grads, w, m, v)


def _adamw_ada(msgs, dmods, w, m, v):
    rows, width = w.shape
    tr = 128

    def body(c_ref, dm_ref, w_ref, m_ref, v_ref, g_out, d_out, m_out, v_out):
        cv = jnp.concatenate([c_ref[d, 0:1, :] for d in range(8)], axis=0)
        act = cv * _sigmoid(cv)
        g = _tn(act, dm_ref[...], precision=HI)
        delta, m_new, v_new = _adamw_math(g, w_ref[...], m_ref[...], v_ref[...])
        g_out[...] = g
        d_out[...] = delta
        m_out[...] = m_new
        v_out[...] = v_new

    blk = pl.BlockSpec((tr, width), lambda i: (i, 0))
    return pl.pallas_call(
        body, grid=(rows // tr,), name="adamw_w_ada",
        in_specs=[pl.BlockSpec((8, MSG_ROWS, tr), lambda i: (0, 0, i)), pl.BlockSpec((8, width), lambda i: (0, 0)),
                  blk, blk, blk],
        out_specs=[blk] * 4, out_shape=[jax.ShapeDtypeStruct((rows, width), F32)] * 4,
        compiler_params=_cparams(),
    )(msgs, dmods, w, m, v)


def _adamw_small(parts, w, m, v, name):
    n, rows, width = parts.shape

    def body(p_ref, w_ref, m_ref, v_ref, g_out, d_out, m_out, v_out):
        g = p_ref[0]
        for j in range(1, n):
            g = g + p_ref[j]
        delta, m_new, v_new = _adamw_math(g, w_ref[...], m_ref[...], v_ref[...])
        g_out[...] = g
        d_out[...] = delta
        m_out[...] = m_new
        v_out[...] = v_new

    return pl.pallas_call(
        body, name=name, in_specs=[VM] * 4, out_specs=[VM] * 4,
        out_shape=[jax.ShapeDtypeStruct((rows, width), F32)] * 4,
        compiler_params=pltpu.CompilerParams(vmem_limit_bytes=VMEM_LIMIT),
    )(parts, w, m, v)


SMALL_ROWS = 24


def _pad_row(vec, width=D):
    vec = vec.reshape(1, -1)
    return jnp.pad(vec, ((0, 0), (0, width - vec.shape[1])))


def _lanes_4_7(vec4):
    return jnp.zeros((1, 128), F32).at[0, NH:2 * NH].set(vec4.reshape(NH))


def kernel(x, c, w_ada, b_ada, norm_ffn1, ffn1_gate, ffn1_up, ffn1_down, norm_mix, w_in, conv_w, a_log, dt_bias, gdn_norm, pool_w, pool_scale, w_out, norm_ffn2, ffn2_gate, ffn2_up, ffn2_down, final_norm, loss_target, m_w_ada, m_b_ada, m_norm_ffn1, m_ffn1_gate, m_ffn1_up, m_ffn1_down, m_norm_mix, m_w_in, m_conv_w, m_a_log, m_dt_bias, m_gdn_norm, m_pool_w, m_pool_scale, m_w_out, m_norm_ffn2, m_ffn2_gate, m_ffn2_up, m_ffn2_down, m_final_norm, v_w_ada, v_b_ada, v_norm_ffn1, v_ffn1_gate, v_ffn1_up, v_ffn1_down, v_norm_mix, v_w_in, v_conv_w, v_a_log, v_dt_bias, v_gdn_norm, v_pool_w, v_pool_scale, v_w_out, v_norm_ffn2, v_ffn2_gate, v_ffn2_up, v_ffn2_down, v_final_norm):
    xs = x[0]
    tgt = loss_target[0]
    chip = 2 * lax.axis_index("x") + lax.axis_index("y")
    me = 2 * chip + lax.axis_index("c")

    fsh = FF // 4
    block_a = jnp.concatenate([ffn1_gate[0].T, ffn1_up[0].T, ffn1_down[0]], axis=0).astype(BF)
    block_b = jnp.concatenate([ffn2_gate[0].T, ffn2_up[0].T, ffn2_down[0], w_out[0],
                               jnp.pad(w_in[0].T, ((0, WIN_PAD - WIN_SH), (0, 0)))], axis=0).astype(BF)

    msg = jnp.concatenate([jnp.broadcast_to(c, (8, D)), jnp.pad(conv_w[0], ((0, 0), (0, D - 3 * GW // 4))),
                           jnp.zeros((MSG_ROWS - 12, D), F32)], axis=0)
    b_sh = lax.dynamic_slice(b_ada, (0, chip * ADA_SH), (1, ADA_SH))
    msgs, mod4, gath_a = _ada_exchange(msg, w_ada[0], b_sh, block_a)
    mod = mod4[:, 0, :].reshape(NMOD, D)
    mrow = [mod[i:i + 1] for i in range(NMOD)]
    conv_full = jnp.concatenate([msgs[2 * j, 8:12, :3 * GW // 4] for j in range(4)], axis=1)
    alog, dtb = _lanes_4_7(a_log), _lanes_4_7(dt_bias)
    gnm = gdn_norm.reshape(1, DH)
    pwb = pool_w[0].astype(BF)
    psc = pool_scale.reshape(1, PW)
    fin = final_norm.reshape(1, D)

    x1, f1, a1, b1, s1, gath_b = _ffn_fwd(xs, mrow[0], mrow[1], mrow[2], norm_ffn1, gath_a, 0, "ffn1_fwd", block_b)
    wo = gath_b[:, 3 * fsh:3 * fsh + D // 4, :].reshape(D, D)
    win_nat = gath_b[:, 3 * fsh + D // 4:3 * fsh + D // 4 + WIN_SH, :].reshape(DIN, D)
    winT = jnp.concatenate([win_nat[:4 * GW], win_nat[4 * GW + 2 * NH:], win_nat[4 * GW:4 * GW + 2 * NH],
                            jnp.zeros((128 - 2 * NH, D), BF)], axis=0)
    xq, ba, qn, kn, vv, z, pp, bg, gc = _mix_proj(x1, mrow[3], mrow[4], norm_mix, winT, conv_full, alog, dtb)
    x2, mixed, cat, o, sall, tmall = _mix_core(x1, mrow[5], qn, kn, vv, z, pp, bg, gc, gnm, pwb, psc, wo)
    lpart, dx3, dfin, f2, a2, b2, s2 = _ffn_fwd(x2, mrow[6], mrow[7], mrow[8], norm_ffn2, gath_b, 0, "ffn2_fwd",
                                                loss=(tgt, fin))
    loss = lax.psum(jnp.sum(lpart), ("x", "y", "c"))

    slots = lambda t: t.reshape(4, t.shape[0] // 4, D)
    dx2, da2, db2, h2, df2, dsh3, dsc3, dgt3, dn3 = _ffn_dgrad(
        dx3, x2, f2, a2, b2, mrow[6], mrow[7], mrow[8], norm_ffn2, gath_b, 0, "ffn2_dgrad")
    gg2, gu2, gd2 = _ffn_wgrad(da2, db2, s2, h2, df2, "ffn2_wgrad")
    dqn, dkn, dvv, dz, dpp, dbg, dgt2, dwo, dpw, dps, dgnm, *landed2 = _mix_core_bwd(
        dx2, mrow[5], mixed, cat, o, sall, qn, kn, vv, z, pp, bg, gc, gnm, pwb, psc, wo, tmall,
        scatter=[slots(gg2), slots(gu2), slots(gd2)])
    dx1, dwin, dcw, dal, ddt, dsh2, dsc2, dn2 = _mix_proj_bwd(
        dx2, dqn, dkn, dvv, dz, dpp, dbg, xq, ba, x1, mrow[3], mrow[4], norm_mix, winT, conv_full, alog, dtb)
    dwin_nat = jnp.concatenate([dwin[:4 * GW], dwin[4 * GW + PW:4 * GW + PW + 2 * NH], dwin[4 * GW:4 * GW + PW]], axis=0)
    dwin_sl = jnp.pad(dwin_nat.reshape(4, WIN_SH, D), ((0, 0), (0, WIN_PAD - WIN_SH), (0, 0)))
    dx0, da1, db1, h1, df1, dsh1, dsc1, dgt1, dn1 = _ffn_dgrad(
        dx1, xs, f1, a1, b1, mrow[0], mrow[1], mrow[2], norm_ffn1, gath_a, 0, "ffn1_dgrad")
    red = lambda t: jnp.sum(t, axis=0, keepdims=True)
    small = jnp.concatenate(
        [red(dn1), red(dn2), red(dn3), red(dfin),
         red(dsh1), red(dsc1), red(dgt1), red(dsh2), red(dsc2), red(dgt2), red(dsh3), red(dsc3), red(dgt3),
         _pad_row(red(dps)), _pad_row(red(dgnm)), _pad_row(red(dal)), _pad_row(red(ddt)),
         jnp.sum(dcw, axis=1).reshape(6, D), jnp.zeros((1, D), F32)], axis=0)
    gg1, gu1, gd1, land_wo, land_win, small_all, dpw_all = _ffn_wgrad(
        da1, db1, s1, h1, df1, "ffn1_wgrad", scatter=[slots(dwo), dwin_sl],
        allgather=[small, dpw.reshape(NG * 128, 128)])

    ffn1_blocks = [slots(gg1), slots(gu1), slots(gd1)]
    sibling_share = _pair_swap_rows(ffn1_blocks, "grad_pair_rows")
    pair_sums = [_add_blocks(g, sh_, "pair_add_" + nm)
                 for g, sh_, nm in zip(ffn1_blocks, sibling_share, ("g1", "u1", "d1"))]
    landed1 = _chip_exchange(pair_sums, "grad_scatter")
    half_sums = [_sum_slots(t, "sum_" + nm) for t, nm in zip(landed1, ("g1", "u1", "d1"))]
    other_sums = _pair_exchange(half_sums, "grad_pair_ffn1")

    landed = list(landed2) + [land_wo, land_win]
    psum = [_sum_slots(t, "sum_" + nm) for t, nm in zip(landed, ("g2", "u2", "d2", "wo", "win"))]
    qsum = _pair_exchange(psum, "grad_pair")

    def adamw_t(grads, w, m, v, name, rows):
        res = _adamw([g[:rows] for g in grads], w[0].T, m[0].T, v[0].T, name)
        return [t.T for t in res]

    upd = {}
    upd["ffn1_gate"] = [t.T for t in _adamw_halves(half_sums[0], other_sums[0], ffn1_gate[0].T, m_ffn1_gate[0].T,
                                                   v_ffn1_gate[0].T, "adamw_g1")]
    upd["ffn1_up"] = [t.T for t in _adamw_halves(half_sums[1], other_sums[1], ffn1_up[0].T, m_ffn1_up[0].T,
                                                 v_ffn1_up[0].T, "adamw_u1")]
    upd["ffn1_down"] = _adamw_halves(half_sums[2], other_sums[2], ffn1_down[0], m_ffn1_down[0], v_ffn1_down[0],
                                     "adamw_d1")
    upd["ffn2_gate"] = adamw_t([psum[0], qsum[0]], ffn2_gate, m_ffn2_gate, v_ffn2_gate, "adamw_g2", fsh)
    upd["ffn2_up"] = adamw_t([psum[1], qsum[1]], ffn2_up, m_ffn2_up, v_ffn2_up, "adamw_u2", fsh)
    upd["ffn2_down"] = _adamw([psum[2], qsum[2]], ffn2_down[0], m_ffn2_down[0], v_ffn2_down[0], "adamw_d2")
    upd["w_out"] = _adamw([psum[3], qsum[3]], w_out[0], m_w_out[0], v_w_out[0], "adamw_wo")
    upd["w_in"] = adamw_t([psum[4], qsum[4]], w_in, m_w_in, v_w_in, "adamw_win", WIN_SH)
    dmods = lax.dynamic_slice(small_all[:, 4:4 + NMOD, :].reshape(8, NMOD * D), (0, chip * ADA_SH), (8, ADA_SH))
    upd["w_ada"] = _adamw_ada(msgs, dmods, w_ada[0], m_w_ada[0], v_w_ada[0])

    def pack_small(nf1, nmx, nf2, fn, bada, psc_, gn_, al_, dt_):
        return jnp.concatenate(
            [nf1.reshape(1, D), nmx.reshape(1, D), nf2.reshape(1, D), fn.reshape(1, D), bada.reshape(NMOD, D),
             _pad_row(psc_), _pad_row(gn_), _pad_row(_lanes_4_7(al_)), _pad_row(_lanes_4_7(dt_)),
             jnp.zeros((7, D), F32)], axis=0)

    ws = pack_small(norm_ffn1, norm_mix, norm_ffn2, final_norm, b_ada, pool_scale, gdn_norm, a_log, dt_bias)
    ms = pack_small(m_norm_ffn1, m_norm_mix, m_norm_ffn2, m_final_norm, m_b_ada, m_pool_scale, m_gdn_norm, m_a_log, m_dt_bias)
    vs = pack_small(v_norm_ffn1, v_norm_mix, v_norm_ffn2, v_final_norm, v_b_ada, v_pool_scale, v_gdn_norm, v_a_log, v_dt_bias)
    sm = _adamw_small(small_all, ws, ms, vs, "adamw_small")
    pw2 = lambda t: t.reshape(NG * 128, 128)
    upd_pw = _adamw_small(dpw_all, pw2(pool_w), pw2(m_pool_w), pw2(v_pool_w), "adamw_pool_w")
    csh = 3 * GW // 4
    gconv = lax.dynamic_slice(sm[0][17:23].reshape(4, 3 * GW), (0, chip * csh), (4, csh))
    upd["conv_w"] = _adamw([gconv], conv_w[0], m_conv_w[0], v_conv_w[0], "adamw_conv")

    def small_out(k):
        t = sm[k]
        return {
            "norm_ffn1": t[0:1], "norm_mix": t[1:2], "norm_ffn2": t[2:3], "final_norm": t[3],
            "b_ada": t[4:4 + NMOD].reshape(1, NMOD * D), "pool_scale": t[13:14, :PW], "gdn_norm": t[14:15, :DH],
            "a_log": t[15:16, NH:2 * NH], "dt_bias": t[16:17, NH:2 * NH],
        }

    order = ["w_ada", "b_ada", "norm_ffn1", "ffn1_gate", "ffn1_up", "ffn1_down", "norm_mix", "w_in", "conv_w", "a_log",
             "dt_bias", "gdn_norm", "pool_w", "pool_scale", "w_out", "norm_ffn2", "ffn2_gate", "ffn2_up", "ffn2_down",
             "final_norm"]
    outs = [loss, dx0[None]]
    for k in range(4):
        smk = small_out(k)
        for nm in order:
            if nm in upd:
                outs.append(upd[nm][k][None])
            elif nm == "pool_w":
                outs.append(upd_pw[k].reshape(1, NG, 128, 128))
            else:
                outs.append(smk[nm])
    return tuple(outs)
```
